```python
import math
import jax, jax.numpy as jnp
from jax import lax
import numpy as np

D_MODEL = 1024
BATCH = 32
SEQ = 2048
DEPTH = 2

HEAD_DIM = 64
N_HEADS = D_MODEL // HEAD_DIM
N_KV_HEADS = N_HEADS // 4
QKV_WIDTH = (N_HEADS + 2 * N_KV_HEADS) * HEAD_DIM
ATTN_HALF_WINDOW = 128
DILATED_GROUPS = ((128, 1), (512, 4), (2048, 16))
N_DGROUPS = len(DILATED_GROUPS)
N_MIXERS = 2
N_LAYERS_A = (DEPTH + 1) // 2
N_LAYERS_B = DEPTH // 2
D_FF = -(-8 * D_MODEL // (3 * 256)) * 256
ROPE_THETA = 10000.0
RMS_EPS = 1e-6
NEG_INF = -1e30

kernel_name = "hybrid_window_sink_dilated_encoder"


def rmsnorm(x, g):
    x32 = x.astype(jnp.float32)
    y = x32 * lax.rsqrt(jnp.mean(x32 * x32, axis=-1, keepdims=True) + RMS_EPS)
    return (y * g.astype(jnp.float32)).astype(x.dtype)


def rope_tables(seq):
    inv_freq = 1.0 / (ROPE_THETA ** (jnp.arange(0, HEAD_DIM, 2, dtype=jnp.float32) / HEAD_DIM))
    ang = jnp.arange(seq, dtype=jnp.float32)[:, None] * inv_freq[None, :]
    return jnp.cos(ang)[:, None, :], jnp.sin(ang)[:, None, :]


def apply_rope(t, cos, sin):
    t32 = t.astype(jnp.float32)
    t1, t2 = jnp.split(t32, 2, axis=-1)
    out = jnp.concatenate([t1 * cos - t2 * sin, t2 * cos + t1 * sin], axis=-1)
    return out.astype(t.dtype)


def split_qkv(proj, cos, sin):
    b, s, _ = proj.shape
    qw = N_HEADS * HEAD_DIM
    kw = N_KV_HEADS * HEAD_DIM
    q = proj[..., :qw].reshape(b, s, N_HEADS, HEAD_DIM)
    k = proj[..., qw:qw + kw].reshape(b, s, N_KV_HEADS, HEAD_DIM)
    v = proj[..., qw + kw:].reshape(b, s, N_KV_HEADS, HEAD_DIM)
    return apply_rope(q, cos, sin), apply_rope(k, cos, sin), v


def banded_attention(q, k, v, half_window, sink=None):
    n, length, n_q, dh = q.shape
    n_kv = k.shape[2]
    grp = n_q // n_kv
    w = half_window
    nb = -(-length // w)
    lp = nb * w
    qb = jnp.pad(q, ((0, 0), (0, lp - length), (0, 0), (0, 0))).reshape(n, nb, w, n_kv, grp, dh)
    pad_kv = ((0, 0), (w, w + lp - length), (0, 0), (0, 0))
    kp = jnp.pad(k, pad_kv)
    vp = jnp.pad(v, pad_kv)
    scale = 1.0 / math.sqrt(dh)
    offs_q = jnp.arange(w)
    offs_k = jnp.arange(3 * w) - w
    sink_l = None if sink is None else sink.astype(jnp.float32).reshape(n_kv, grp)[None, :, :, None]

    def one_block(i):
        start = i * w
        q_i = lax.dynamic_index_in_dim(qb, i, axis=1, keepdims=False)
        k_i = lax.dynamic_slice_in_dim(kp, start, 3 * w, axis=1)
        v_i = lax.dynamic_slice_in_dim(vp, start, 3 * w, axis=1)
        s = jnp.einsum('nqkgd,nskd->nkgqs', q_i, k_i).astype(jnp.float32) * scale
        qpos = start + offs_q
        kpos = start + offs_k
        valid = ((jnp.abs(qpos[:, None] - kpos[None, :]) <= w)
                 & (kpos[None, :] >= 0) & (kpos[None, :] < length))
        s = jnp.where(valid, s, NEG_INF)
        m = jnp.max(s, axis=-1)
        if sink_l is not None:
            m = jnp.maximum(m, sink_l)
        p = jnp.exp(s - m[..., None])
        denom = jnp.sum(p, axis=-1)
        if sink_l is not None:
            denom = denom + jnp.exp(sink_l - m)
        o = jnp.einsum('nkgqs,nskd->nqkgd', p, v_i.astype(jnp.float32))
        o = o / jnp.transpose(denom, (0, 3, 1, 2))[..., None]
        lse = jnp.transpose(m + jnp.log(denom), (0, 3, 1, 2))
        return o.astype(q.dtype), lse

    o, lse = lax.map(one_block, jnp.arange(nb))
    o = jnp.moveaxis(o, 0, 1).reshape(n, lp, n_q, dh)[:, :length]
    lse = jnp.moveaxis(lse, 0, 1).reshape(n, lp, n_q)[:, :length]
    return o, lse


def dilated_attention(q, k, v, dilation, half_window):
    b, s, n_q, dh = q.shape
    d = dilation

    def to_residue(t):
        return t.reshape(b, s // d, d, t.shape[2], dh).transpose(0, 2, 1, 3, 4).reshape(b * d, s // d, t.shape[2], dh)

    o, lse = banded_attention(to_residue(q), to_residue(k), to_residue(v), half_window // d)
    o = o.reshape(b, d, s // d, n_q, dh).transpose(0, 2, 1, 3, 4).reshape(b, s, n_q, dh)
    lse = lse.reshape(b, d, s // d, n_q).transpose(0, 2, 1, 3).reshape(b, s, n_q)
    return o, lse


def mixer_window_sink(h, w_in, sink, w_out, cos, sin):
    b, s, _ = h.shape
    q, k, v = split_qkv(h @ w_in, cos, sin)
    o, _ = banded_attention(q, k, v, ATTN_HALF_WINDOW, sink)
    return o.reshape(b, s, N_HEADS * HEAD_DIM) @ w_out


def mixer_dilated(h, w_in, w_out, cos, sin):
    b, s, _ = h.shape
    proj = (h @ w_in).reshape(b, s, N_DGROUPS, QKV_WIDTH)
    outs, lses = [], []
    for g, (window, dilation) in enumerate(DILATED_GROUPS):
        q, k, v = split_qkv(proj[:, :, g], cos, sin)
        o, lse = dilated_attention(q, k, v, dilation, window // 2)
        outs.append(o)
        lses.append(lse)
    wts = jax.nn.softmax(jnp.stack(lses, axis=0), axis=0)
    o = (wts[0][..., None] * outs[0].astype(jnp.float32)
         + wts[1][..., None] * outs[1].astype(jnp.float32)
         + wts[2][..., None] * outs[2].astype(jnp.float32))
    return o.astype(h.dtype).reshape(b, s, N_HEADS * HEAD_DIM) @ w_out


def swiglu(h, w_gate, w_up, w_down):
    return (jax.nn.silu(h @ w_gate) * (h @ w_up)) @ w_down


def _fwd_setup_inputs(seed: int = 0) -> dict:
    key = jax.random.key(seed)
    ks = jax.random.split(key, 14)
    f32 = jnp.float32
    d = D_MODEL
    hd = N_HEADS * HEAD_DIM
    x = jax.random.normal(ks[0], (BATCH, SEQ, d), f32)
    a_w_in = jax.random.normal(ks[1], (N_LAYERS_A, d, QKV_WIDTH), f32) * d ** -0.5
    a_sink = jax.random.normal(ks[2], (N_LAYERS_A, N_HEADS), f32) * 0.5
    a_w_out = jax.random.normal(ks[3], (N_LAYERS_A, hd, d), f32) * hd ** -0.5
    b_w_in = jax.random.normal(ks[4], (N_LAYERS_B, d, N_DGROUPS * QKV_WIDTH), f32) * d ** -0.5
    b_w_out = jax.random.normal(ks[5], (N_LAYERS_B, hd, d), f32) * hd ** -0.5
    norm_mix = 1.0 + 0.02 * jax.random.normal(ks[6], (DEPTH, d), f32)
    norm_ffn = 1.0 + 0.02 * jax.random.normal(ks[7], (DEPTH, d), f32)
    w_gate = jax.random.normal(ks[8], (DEPTH, d, D_FF), f32) * d ** -0.5
    w_up = jax.random.normal(ks[9], (DEPTH, d, D_FF), f32) * d ** -0.5
    w_down = jax.random.normal(ks[10], (DEPTH, D_FF, d), f32) * D_FF ** -0.5
    final_norm = 1.0 + 0.02 * jax.random.normal(ks[11], (d,), f32)
    return {"x": x, "a_w_in": a_w_in, "a_sink": a_sink, "a_w_out": a_w_out,
            "b_w_in": b_w_in, "b_w_out": b_w_out, "norm_mix": norm_mix, "norm_ffn": norm_ffn,
            "w_gate": w_gate, "w_up": w_up, "w_down": w_down, "final_norm": final_norm}


def _fwd_reference(x, a_w_in, a_sink, a_w_out, b_w_in, b_w_out, norm_mix, norm_ffn,
              w_gate, w_up, w_down, final_norm):
    cos, sin = rope_tables(x.shape[1])
    for i in range(DEPTH):
        h = rmsnorm(x, norm_mix[i])
        j = i // N_MIXERS
        if i % N_MIXERS == 0:
            mix = mixer_window_sink(h, a_w_in[j], a_sink[j], a_w_out[j], cos, sin)
        else:
            mix = mixer_dilated(h, b_w_in[j], b_w_out[j], cos, sin)
        x = x + mix
        h = rmsnorm(x, norm_ffn[i])
        x = x + swiglu(h, w_gate[i], w_up[i], w_down[i])
    return rmsnorm(x, final_norm)


import jax as _jax
import jax.numpy as _jnp

TWIN_FORMAT = 'train_step'
FWD_PARAMS = ['x', 'a_w_in', 'a_sink', 'a_w_out', 'b_w_in', 'b_w_out', 'norm_mix', 'norm_ffn', 'w_gate', 'w_up', 'w_down', 'final_norm']
TWIN_WEIGHTS = ['a_w_in', 'a_sink', 'a_w_out', 'b_w_in', 'b_w_out', 'norm_mix', 'norm_ffn', 'w_gate', 'w_up', 'w_down', 'final_norm']
TWIN_DIFF_INPUT = 'x'
TWIN_INPUTS = ['x', 'a_w_in', 'a_sink', 'a_w_out', 'b_w_in', 'b_w_out', 'norm_mix', 'norm_ffn', 'w_gate', 'w_up', 'w_down', 'final_norm', 'loss_target', 'm_a_w_in', 'm_a_sink', 'm_a_w_out', 'm_b_w_in', 'm_b_w_out', 'm_norm_mix', 'm_norm_ffn', 'm_w_gate', 'm_w_up', 'm_w_down', 'm_final_norm', 'v_a_w_in', 'v_a_sink', 'v_a_w_out', 'v_b_w_in', 'v_b_w_out', 'v_norm_mix', 'v_norm_ffn', 'v_w_gate', 'v_w_up', 'v_w_down', 'v_final_norm']
TWIN_OUTPUTS = ['loss', 'grad_x', 'grad_a_w_in', 'grad_a_sink', 'grad_a_w_out', 'grad_b_w_in', 'grad_b_w_out', 'grad_norm_mix', 'grad_norm_ffn', 'grad_w_gate', 'grad_w_up', 'grad_w_down', 'grad_final_norm', 'delta_a_w_in', 'delta_a_sink', 'delta_a_w_out', 'delta_b_w_in', 'delta_b_w_out', 'delta_norm_mix', 'delta_norm_ffn', 'delta_w_gate', 'delta_w_up', 'delta_w_down', 'delta_final_norm', 'new_m_a_w_in', 'new_m_a_sink', 'new_m_a_w_out', 'new_m_b_w_in', 'new_m_b_w_out', 'new_m_norm_mix', 'new_m_norm_ffn', 'new_m_w_gate', 'new_m_w_up', 'new_m_w_down', 'new_m_final_norm', 'new_v_a_w_in', 'new_v_a_sink', 'new_v_a_w_out', 'new_v_b_w_in', 'new_v_b_w_out', 'new_v_norm_mix', 'new_v_norm_ffn', 'new_v_w_gate', 'new_v_w_up', 'new_v_w_down', 'new_v_final_norm']
TWIN_LEAF_KINDS = {'loss': 'loss', 'grad_x': 'grad_x', 'grad_a_w_in': 'grad_w', 'grad_a_sink': 'grad_w', 'grad_a_w_out': 'grad_w', 'grad_b_w_in': 'grad_w', 'grad_b_w_out': 'grad_w', 'grad_norm_mix': 'grad_w', 'grad_norm_ffn': 'grad_w', 'grad_w_gate': 'grad_w', 'grad_w_up': 'grad_w', 'grad_w_down': 'grad_w', 'grad_final_norm': 'grad_w', 'delta_a_w_in': 'delta_w', 'delta_a_sink': 'delta_w', 'delta_a_w_out': 'delta_w', 'delta_b_w_in': 'delta_w', 'delta_b_w_out': 'delta_w', 'delta_norm_mix': 'delta_w', 'delta_norm_ffn': 'delta_w', 'delta_w_gate': 'delta_w', 'delta_w_up': 'delta_w', 'delta_w_down': 'delta_w', 'delta_final_norm': 'delta_w', 'new_m_a_w_in': 'new_m', 'new_m_a_sink': 'new_m', 'new_m_a_w_out': 'new_m', 'new_m_b_w_in': 'new_m', 'new_m_b_w_out': 'new_m', 'new_m_norm_mix': 'new_m', 'new_m_norm_ffn': 'new_m', 'new_m_w_gate': 'new_m', 'new_m_w_up': 'new_m', 'new_m_w_down': 'new_m', 'new_m_final_norm': 'new_m', 'new_v_a_w_in': 'new_v', 'new_v_a_sink': 'new_v', 'new_v_a_w_out': 'new_v', 'new_v_b_w_in': 'new_v', 'new_v_b_w_out': 'new_v', 'new_v_norm_mix': 'new_v', 'new_v_norm_ffn': 'new_v', 'new_v_w_gate': 'new_v', 'new_v_w_up': 'new_v', 'new_v_w_down': 'new_v', 'new_v_final_norm': 'new_v'}


def _forward(args):
    return _fwd_reference(*[args[k] for k in FWD_PARAMS])


def _output_shape():
    out = _jax.eval_shape(lambda: _forward(_fwd_setup_inputs(0)))
    return out.shape, out.dtype

N_MICROBATCH = 1
ADAM_LR = 0.001
ADAM_B1 = 0.9
ADAM_B2 = 0.999
ADAM_EPS = 1e-08
ADAM_WD = 0.01
ADAM_STEP = 10
PER_EXAMPLE_BATCH_AXIS = {'x': 0, 'loss_target': 0}
SHARED_INPUTS = []
_WEIGHT_DTYPES = {'a_w_in': _jnp.float32, 'a_sink': _jnp.float32, 'a_w_out': _jnp.float32, 'b_w_in': _jnp.float32, 'b_w_out': _jnp.float32, 'norm_mix': _jnp.float32, 'norm_ffn': _jnp.float32, 'w_gate': _jnp.float32, 'w_up': _jnp.float32, 'w_down': _jnp.float32, 'final_norm': _jnp.float32}
MOMENT_SCALE = {'a_w_in': 5.380602e-02, 'a_sink': 1.840506e-03, 'a_w_out': 3.951796e-02, 'b_w_in': 2.323221e-02, 'b_w_out': 3.009439e-02, 'norm_mix': 5.900153e-02, 'norm_ffn': 1.905381e-01, 'w_gate': 8.019778e-02, 'w_up': 7.752556e-02, 'w_down': 1.286697e-01, 'final_norm': 6.383983e+01}


def _to_microbatches(a, axis):
    t = _jnp.moveaxis(a, axis, 0)
    t = t.reshape((N_MICROBATCH, t.shape[0] // N_MICROBATCH) + t.shape[1:])
    return _jnp.moveaxis(t, 1, axis + 1)


def setup_inputs(seed: int = 0) -> dict:
    inp = _fwd_setup_inputs(seed)
    key = _jax.random.fold_in(_jax.random.key(seed), 7919)
    shape, _ = _output_shape()
    out = dict(inp)
    out["loss_target"] = _jax.random.normal(_jax.random.fold_in(key, 0), shape, _jnp.float32)
    for i, name in enumerate(TWIN_WEIGHTS):
        w = inp[name].astype(_jnp.float32)
        if MOMENT_SCALE is None:
            s = _jnp.sqrt(_jnp.mean(_jnp.square(w)) + 1e-30)
        else:
            s = MOMENT_SCALE[name]
        km, kv = _jax.random.split(_jax.random.fold_in(key, i + 1))
        out[name] = w
        out["m_" + name] = s * _jax.random.normal(km, w.shape, _jnp.float32)
        out["v_" + name] = (s * s) * _jax.random.uniform(kv, w.shape, _jnp.float32, 0.5, 1.5)
    if N_MICROBATCH > 1:
        for name, axis in PER_EXAMPLE_BATCH_AXIS.items():
            out[name] = _to_microbatches(out[name], axis)
    return {'x': out['x'], 'a_w_in': out['a_w_in'], 'a_sink': out['a_sink'], 'a_w_out': out['a_w_out'], 'b_w_in': out['b_w_in'], 'b_w_out': out['b_w_out'], 'norm_mix': out['norm_mix'], 'norm_ffn': out['norm_ffn'], 'w_gate': out['w_gate'], 'w_up': out['w_up'], 'w_down': out['w_down'], 'final_norm': out['final_norm'], 'loss_target': out['loss_target'], 'm_a_w_in': out['m_a_w_in'], 'm_a_sink': out['m_a_sink'], 'm_a_w_out': out['m_a_w_out'], 'm_b_w_in': out['m_b_w_in'], 'm_b_w_out': out['m_b_w_out'], 'm_norm_mix': out['m_norm_mix'], 'm_norm_ffn': out['m_norm_ffn'], 'm_w_gate': out['m_w_gate'], 'm_w_up': out['m_w_up'], 'm_w_down': out['m_w_down'], 'm_final_norm': out['m_final_norm'], 'v_a_w_in': out['v_a_w_in'], 'v_a_sink': out['v_a_sink'], 'v_a_w_out': out['v_a_w_out'], 'v_b_w_in': out['v_b_w_in'], 'v_b_w_out': out['v_b_w_out'], 'v_norm_mix': out['v_norm_mix'], 'v_norm_ffn': out['v_norm_ffn'], 'v_w_gate': out['v_w_gate'], 'v_w_up': out['v_w_up'], 'v_w_down': out['v_w_down'], 'v_final_norm': out['v_final_norm']}


def _loss(weights, diff, rest, loss_target):
    with _jax.named_scope("forward"):
        args = {**rest, TWIN_DIFF_INPUT: diff, **{k: w.astype(_WEIGHT_DTYPES[k]) for k, w in weights.items()}}
        y = _forward(args)
    with _jax.named_scope("loss_head"):
        err = _jnp.square(y.astype(_jnp.float32) - loss_target)
        return 0.5 * _jnp.sum(_jnp.mean(err, axis=-1)) if err.ndim else 0.5 * err


def _adamw(w, g, m, v):
    m = ADAM_B1 * m + (1.0 - ADAM_B1) * g
    v = ADAM_B2 * v + (1.0 - ADAM_B2) * _jnp.square(g)
    m_hat = m / (1.0 - ADAM_B1 ** ADAM_STEP)
    v_hat = v / (1.0 - ADAM_B2 ** ADAM_STEP)
    delta = -ADAM_LR * (m_hat / (_jnp.sqrt(v_hat) + ADAM_EPS) + ADAM_WD * w)
    return delta, m, v


def reference(x, a_w_in, a_sink, a_w_out, b_w_in, b_w_out, norm_mix, norm_ffn, w_gate, w_up, w_down, final_norm, loss_target, m_a_w_in, m_a_sink, m_a_w_out, m_b_w_in, m_b_w_out, m_norm_mix, m_norm_ffn, m_w_gate, m_w_up, m_w_down, m_final_norm, v_a_w_in, v_a_sink, v_a_w_out, v_b_w_in, v_b_w_out, v_norm_mix, v_norm_ffn, v_w_gate, v_w_up, v_w_down, v_final_norm):
    given = dict(x=x, a_w_in=a_w_in, a_sink=a_sink, a_w_out=a_w_out, b_w_in=b_w_in, b_w_out=b_w_out, norm_mix=norm_mix, norm_ffn=norm_ffn, w_gate=w_gate, w_up=w_up, w_down=w_down, final_norm=final_norm, loss_target=loss_target, m_a_w_in=m_a_w_in, m_a_sink=m_a_sink, m_a_w_out=m_a_w_out, m_b_w_in=m_b_w_in, m_b_w_out=m_b_w_out, m_norm_mix=m_norm_mix, m_norm_ffn=m_norm_ffn, m_w_gate=m_w_gate, m_w_up=m_w_up, m_w_down=m_w_down, m_final_norm=m_final_norm, v_a_w_in=v_a_w_in, v_a_sink=v_a_sink, v_a_w_out=v_a_w_out, v_b_w_in=v_b_w_in, v_b_w_out=v_b_w_out, v_norm_mix=v_norm_mix, v_norm_ffn=v_norm_ffn, v_w_gate=v_w_gate, v_w_up=v_w_up, v_w_down=v_w_down, v_final_norm=v_final_norm)
    weights = {n: given[n] for n in TWIN_WEIGHTS}
    shared = {n: given[n] for n in SHARED_INPUTS}
    per_example = {n: given[n] for n in ['x']}
    grad_fn = _jax.value_and_grad(_loss, argnums=(0, 1))

    def one_microbatch(ex, loss_target):
        ex = dict(ex)
        diff = ex.pop(TWIN_DIFF_INPUT)
        return grad_fn(weights, diff, {**shared, **ex}, loss_target)

    if N_MICROBATCH == 1:
        loss, (grad_w, grad_x) = one_microbatch(per_example, given["loss_target"])
    else:
        def body(carry, xs):
            loss_sum, grad_sum = carry
            l_k, (gw_k, gx_k) = one_microbatch(xs[0], xs[1])
            with _jax.named_scope("update"):
                return (loss_sum + l_k, _jax.tree.map(_jnp.add, grad_sum, gw_k)), gx_k

        init = (_jnp.zeros((), _jnp.float32), _jax.tree.map(_jnp.zeros_like, weights))
        (loss, grad_w), grad_x = _jax.lax.scan(body, init, (per_example, given["loss_target"]))
    with _jax.named_scope("update"):
        delta_w, new_m, new_v = {}, {}, {}
        for n in TWIN_WEIGHTS:
            delta_w[n], new_m[n], new_v[n] = _adamw(weights[n], grad_w[n], given["m_" + n], given["v_" + n])
    return (loss, grad_x, *[grad_w[n] for n in TWIN_WEIGHTS], *[delta_w[n] for n in TWIN_WEIGHTS],
            *[new_m[n] for n in TWIN_WEIGHTS], *[new_v[n] for n in TWIN_WEIGHTS])
```

```python
import functools
import math

import jax
import jax.numpy as jnp
from jax import lax
from jax.experimental import pallas as pl
from jax.experimental.pallas import tpu as pltpu

F32 = jnp.float32
BF16 = jnp.bfloat16

HEAD_DIM = 64
N_HEADS = 16
N_KV = 4
GRP = N_HEADS // N_KV
Q_W = N_HEADS * HEAD_DIM
KV_W = N_KV * HEAD_DIM
QKV_W = Q_W + 2 * KV_W
ATTN_HALF_WINDOW = 128
DILATED_GROUPS = ((128, 1), (512, 4), (2048, 16))
ROPE_THETA = 10000.0
RMS_EPS = 1e-6
NEG_INF = -1e30
SCALE = 1.0 / math.sqrt(HEAD_DIM)

ADAM_LR = 0.001
ADAM_B1 = 0.9
ADAM_B2 = 0.999
ADAM_EPS = 1e-08
ADAM_WD = 0.01
ADAM_STEP = 10

LANES = 128
VMEM_LIMIT = 56 * 1024 * 1024
N_DEV = 8
N_CHIP = 4
MESH = pl.DeviceIdType.MESH

NT = (((1,), (1,)), ((), ()))
TN = (((0,), (0,)), ((), ()))


def _params(*sem):
    return pltpu.CompilerParams(dimension_semantics=tuple(sem) if sem else None, vmem_limit_bytes=VMEM_LIMIT)


def _resident(shape):
    return pl.BlockSpec(shape, lambda *_: (0,) * len(shape), pipeline_mode=pl.Buffered(1))


def _rope_tables(seq):
    inv_freq = 1.0 / (ROPE_THETA ** (jnp.arange(0, HEAD_DIM, 2, dtype=F32) / HEAD_DIM))
    ang = jnp.arange(seq, dtype=F32)[:, None] * inv_freq[None, :]
    cos, sin = jnp.cos(ang), jnp.sin(ang)
    return jnp.tile(cos, (1, 4)), jnp.concatenate([-sin, sin, -sin, sin], axis=1)


def _rope(t, cos, sin_signed):
    lane = lax.broadcasted_iota(jnp.int32, t.shape, 1)
    first = (lane & (HEAD_DIM // 2)) == 0
    swapped = jnp.where(first, pltpu.roll(t, LANES - HEAD_DIM // 2, 1), pltpu.roll(t, HEAD_DIM // 2, 1))
    return t * cos + swapped * sin_signed


def _rms(x):
    return lax.rsqrt(jnp.mean(x * x, axis=-1, keepdims=True) + RMS_EPS)


def _rms_bwd(dh, x, gain):
    r = _rms(x)
    xhat = x * r
    dxh = dh * gain
    dx = r * (dxh - xhat * jnp.mean(dxh * xhat, axis=-1, keepdims=True))
    return dx, xhat


def _accumulate(ref, value, first):
    @pl.when(first)
    def _():
        ref[...] = jnp.zeros_like(ref)

    ref[...] += value


def _qkv_proj(x, gain, w, cos, sin, seq, tag):
    t, d = x.shape
    n = w.shape[1]
    tm = min(512, seq)
    per_seq = seq // tm

    def body(x_ref, g_ref, w_ref, cos_ref, sin_ref, o_ref, h_ref):
        @pl.when(pl.program_id(1) == 0)
        def _():
            xv = x_ref[...]
            h_ref[...] = (xv * _rms(xv) * g_ref[...]).astype(BF16)

        acc = jnp.dot(h_ref[...], w_ref[...], preferred_element_type=F32)
        c, s = cos_ref[...], sin_ref[...]
        for j in range((Q_W + KV_W) // LANES):
            cols = slice(j * LANES, (j + 1) * LANES)
            o_ref[:, cols] = _rope(acc[:, cols], c, s).astype(BF16)
        o_ref[:, Q_W + KV_W:] = acc[:, Q_W + KV_W:].astype(BF16)

    return pl.pallas_call(
        body, name=f"qkv_proj_{tag}", grid=(t // tm, n // QKV_W),
        in_specs=[pl.BlockSpec((tm, d), lambda i, j: (i, 0)), pl.BlockSpec((1, d), lambda i, j: (0, 0)),
                  pl.BlockSpec((d, QKV_W), lambda i, j: (0, j)),
                  pl.BlockSpec((tm, LANES), lambda i, j: (i % per_seq, 0)),
                  pl.BlockSpec((tm, LANES), lambda i, j: (i % per_seq, 0))],
        out_specs=pl.BlockSpec((tm, QKV_W), lambda i, j: (i, j)),
        out_shape=jax.ShapeDtypeStruct((t, n), BF16),
        scratch_shapes=[pltpu.VMEM((tm, d), BF16)],
        compiler_params=_params("parallel", "arbitrary"),
    )(x, gain, w, cos, sin)


def _band(w, wk):
    a = lax.broadcasted_iota(jnp.int32, (GRP * w, wk), 0) & (w - 1)
    return a - lax.broadcasted_iota(jnp.int32, (GRP * w, wk), 1)


def _stack_heads(ref, rows, kv, width):
    return jnp.concatenate([ref[rows, pl.ds((kv * GRP + h) * width, width)] for h in range(GRP)], axis=0)


def _attn_fwd(qkv, group, dil, w, tag, sink=None, prev=None):
    b, s, n = qkv.shape
    length = s // dil
    wk = min(3 * w, length)
    nb = length // w
    ncol = n // QKV_W
    has_sink, has_prev = sink is not None, prev is not None

    def body(*refs):
        refs = list(refs)
        qkv_ref = refs.pop(0)
        sink_ref = refs.pop(0) if has_sink else None
        op_ref, lp_ref = (refs.pop(0), refs.pop(0)) if has_prev else (None, None)
        o_ref, lse_ref = refs
        lse_ref[...] = jnp.zeros_like(lse_ref)
        band = _band(w, wk)

        def block(i, carry):
            q0 = pl.multiple_of(i * w, w)
            k0 = pl.multiple_of(jnp.clip((i - 1) * w, 0, length - wk), w)
            valid = jnp.abs(band + (q0 - k0)) <= w
            rows, krows = pl.ds(q0, w), pl.ds(k0, wk)
            for kv in range(N_KV):
                q4 = _stack_heads(qkv_ref, rows, kv, HEAD_DIM)
                k = qkv_ref[krows, pl.ds(Q_W + kv * HEAD_DIM, HEAD_DIM)]
                v = qkv_ref[krows, pl.ds(Q_W + KV_W + kv * HEAD_DIM, HEAD_DIM)]
                sc = lax.dot_general(q4, k, NT, preferred_element_type=F32) * SCALE
                sc = jnp.where(valid, sc, NEG_INF)
                m = jnp.max(sc, axis=-1, keepdims=True)
                if has_sink:
                    sk = jnp.concatenate([jnp.full((w, 1), sink_ref[kv * GRP + h], F32) for h in range(GRP)], axis=0)
                    m = jnp.maximum(m, sk)
                p = jnp.exp(sc - m)
                den = jnp.sum(p, axis=-1, keepdims=True)
                if has_sink:
                    den = den + jnp.exp(sk - m)
                o4 = jnp.dot(p.astype(BF16), v, preferred_element_type=F32) / den
                lse4 = m + jnp.log(den)
                for h in range(GRP):
                    hd = kv * GRP + h
                    oh, lh = o4[h * w:(h + 1) * w], lse4[h * w:(h + 1) * w]
                    if has_prev:
                        op = op_ref[rows, pl.ds(hd * HEAD_DIM, HEAD_DIM)]
                        lp = lp_ref[rows, pl.ds(hd, 1)]
                        mx = jnp.maximum(lp, lh)
                        ln = mx + jnp.log(jnp.exp(lp - mx) + jnp.exp(lh - mx))
                        oh = op * jnp.exp(lp - ln) + oh * jnp.exp(lh - ln)
                        lh = ln
                    o_ref[rows, pl.ds(hd * HEAD_DIM, HEAD_DIM)] = oh
                    lse_ref[rows, pl.ds(hd, 1)] = lh
            return carry

        lax.fori_loop(0, nb, block, 0)

    args = [qkv.reshape(b, length, dil * n)]
    in_specs = [pl.BlockSpec((None, length, QKV_W), lambda bi, r: (bi, 0, r * ncol + group))]
    if has_sink:
        args.append(sink)
        in_specs.append(pl.BlockSpec(memory_space=pltpu.SMEM))
    o_spec = pl.BlockSpec((None, length, Q_W), lambda bi, r: (bi, 0, r))
    l_spec = pl.BlockSpec((None, length, LANES), lambda bi, r: (bi, 0, r))
    if has_prev:
        args += [prev[0].reshape(b, length, dil * Q_W), prev[1].reshape(b, length, dil * LANES)]
        in_specs += [o_spec, l_spec]
    o, lse = pl.pallas_call(
        body, name=f"attn_fwd_{tag}", grid=(b, dil), in_specs=in_specs, out_specs=[o_spec, l_spec],
        out_shape=[jax.ShapeDtypeStruct((b, length, dil * Q_W), F32), jax.ShapeDtypeStruct((b, length, dil * LANES), F32)],
        compiler_params=_params("parallel", "parallel"),
    )(*args)
    return o.reshape(b, s, Q_W), lse.reshape(b, s, LANES)


def _out_proj(x, o, w, tag):
    t, d = x.shape
    tm = min(512, t)

    def body(x_ref, o_ref, w_ref, y_ref):
        y_ref[...] = x_ref[...] + jnp.dot(o_ref[...].astype(BF16), w_ref[...], preferred_element_type=F32)

    row = pl.BlockSpec((tm, d), lambda i: (i, 0))
    return pl.pallas_call(
        body, name=f"out_proj_{tag}", grid=(t // tm,), in_specs=[row, row, _resident(w.shape)], out_specs=row,
        out_shape=jax.ShapeDtypeStruct((t, d), F32), compiler_params=_params("parallel"),
    )(x, o, w)


def _sigmoid(g):
    return 1.0 / (1.0 + jnp.exp(-g))


def _ffn_fwd(x, gain, wg, wu, wd, tag):
    t, d = x.shape
    f = wg.shape[1]
    tm = min(256, t)

    def body(x_ref, gain_ref, wg_ref, wu_ref, wd_ref, y_ref, g_ref, u_ref):
        xv = x_ref[...]
        h = (xv * _rms(xv) * gain_ref[...]).astype(BF16)
        g = jnp.dot(h, wg_ref[...], preferred_element_type=F32)
        u = jnp.dot(h, wu_ref[...], preferred_element_type=F32)
        g_ref[...] = g.astype(BF16)
        u_ref[...] = u.astype(BF16)
        a = (g * _sigmoid(g) * u).astype(BF16)
        y_ref[...] = xv + jnp.dot(a, wd_ref[...], preferred_element_type=F32)

    row = pl.BlockSpec((tm, d), lambda i: (i, 0))
    wide = pl.BlockSpec((tm, f), lambda i: (i, 0))
    return pl.pallas_call(
        body, name=f"ffn_fwd_{tag}", grid=(t // tm,),
        in_specs=[row, _resident((1, d)), _resident(wg.shape), _resident(wu.shape), _resident(wd.shape)],
        out_specs=[row, wide, wide],
        out_shape=[jax.ShapeDtypeStruct((t, d), F32), jax.ShapeDtypeStruct((t, f), BF16), jax.ShapeDtypeStruct((t, f), BF16)],
        compiler_params=_params("parallel"),
    )(x, gain, wg, wu, wd)


def _loss_bwd(x, gain, target):
    t, d = x.shape
    tm = min(512, t)

    def body(x_ref, gain_ref, t_ref, dx_ref, loss_ref, dgain_ref):
        xv, gain_v = x_ref[...], gain_ref[...]
        xhat = xv * _rms(xv)
        err = xhat * gain_v - t_ref[...]
        dy = err * (1.0 / d)
        dx, _ = _rms_bwd(dy, xv, gain_v)
        dx_ref[...] = dx
        first = pl.program_id(0) == 0
        part = 0.5 * jnp.sum(jnp.mean(err * err, axis=-1, keepdims=True), axis=0, keepdims=True)
        _accumulate(loss_ref, jnp.broadcast_to(part, loss_ref.shape), first)
        _accumulate(dgain_ref, jnp.sum(dy * xhat, axis=0, keepdims=True), first)

    row = pl.BlockSpec((tm, d), lambda i: (i, 0))
    return pl.pallas_call(
        body, name="loss_bwd", grid=(t // tm,), in_specs=[row, _resident((1, d)), row],
        out_specs=[row, pl.BlockSpec((1, LANES), lambda i: (0, 0)), pl.BlockSpec((1, d), lambda i: (0, 0))],
        out_shape=[jax.ShapeDtypeStruct((t, d), F32), jax.ShapeDtypeStruct((1, LANES), F32), jax.ShapeDtypeStruct((1, d), F32)],
        compiler_params=_params("arbitrary"),
    )(x, gain, target)


def _ffn_bwd(dy, x, gain, g, u, wg, wu, wd, tag):
    t, d = x.shape
    f = wg.shape[1]
    tm = min(256, t)

    def body(dy_ref, x_ref, gain_ref, g_ref, u_ref, wg_ref, wu_ref, wd_ref, dx_ref, dg_ref, du_ref, a_ref, h_ref, dgain_ref):
        dyv = dy_ref[...]
        da = lax.dot_general(dyv.astype(BF16), wd_ref[...], NT, preferred_element_type=F32)
        gv, uv = g_ref[...].astype(F32), u_ref[...].astype(F32)
        sg = _sigmoid(gv)
        act = gv * sg
        a_ref[...] = (act * uv).astype(BF16)
        du = (da * act).astype(BF16)
        dg = (da * uv * (sg * (1.0 + gv * (1.0 - sg)))).astype(BF16)
        du_ref[...] = du
        dg_ref[...] = dg
        dh = (lax.dot_general(dg, wg_ref[...], NT, preferred_element_type=F32)
              + lax.dot_general(du, wu_ref[...], NT, preferred_element_type=F32))
        xv, gain_v = x_ref[...], gain_ref[...]
        dx, xhat = _rms_bwd(dh, xv, gain_v)
        dx_ref[...] = dyv + dx
        h_ref[...] = (xhat * gain_v).astype(BF16)
        _accumulate(dgain_ref, jnp.sum(dh * xhat, axis=0, keepdims=True), pl.program_id(0) == 0)

    row = pl.BlockSpec((tm, d), lambda i: (i, 0))
    wide = pl.BlockSpec((tm, f), lambda i: (i, 0))
    return pl.pallas_call(
        body, name=f"ffn_bwd_{tag}", grid=(t // tm,),
        in_specs=[row, row, _resident((1, d)), wide, wide, _resident(wg.shape), _resident(wu.shape), _resident(wd.shape)],
        out_specs=[row, wide, wide, wide, row, pl.BlockSpec((1, d), lambda i: (0, 0))],
        out_shape=[jax.ShapeDtypeStruct((t, d), F32), jax.ShapeDtypeStruct((t, f), BF16), jax.ShapeDtypeStruct((t, f), BF16),
                   jax.ShapeDtypeStruct((t, f), BF16), jax.ShapeDtypeStruct((t, d), BF16), jax.ShapeDtypeStruct((1, d), F32)],
        compiler_params=_params("arbitrary"),
    )(dy, x, gain, g, u, wg, wu, wd)


def _tn_matmul(a, bs, name):
    t, k = a.shape
    tk = k // 2 if (k // 2) % LANES == 0 else k
    tt = min(512, t)
    nb = len(bs)

    def body(a_ref, *refs):
        at = a_ref[...].astype(BF16)
        for b_ref, o_ref in zip(refs[:nb], refs[nb:]):
            _accumulate(o_ref, lax.dot_general(at, b_ref[...].astype(BF16), TN, preferred_element_type=F32),
                        pl.program_id(1) == 0)

    return pl.pallas_call(
        body, name=name, grid=(k // tk, t // tt),
        in_specs=[pl.BlockSpec((tt, tk), lambda i, j: (j, i))] + [pl.BlockSpec((tt, b.shape[1]), lambda i, j: (j, 0)) for b in bs],
        out_specs=[pl.BlockSpec((tk, b.shape[1]), lambda i, j: (i, 0)) for b in bs],
        out_shape=[jax.ShapeDtypeStruct((k, b.shape[1]), F32) for b in bs],
        compiler_params=_params("parallel", "arbitrary"),
    )(a, *bs)


def _attn_out_bwd(dx, w, o, tag, lse=None, sink=None):
    t, d = dx.shape
    tm = min(512, t)
    has_sink = sink is not None
    expand = (jnp.arange(d)[:, None] // HEAD_DIM == jnp.arange(LANES)[None, :]).astype(BF16)

    def body(*refs):
        refs = list(refs)
        dx_ref, w_ref, o_ref, e_ref = refs[:4]
        do_ref, dl_ref = refs[6:8] if has_sink else refs[4:6]
        do = lax.dot_general(dx_ref[...].astype(BF16), w_ref[...], NT, preferred_element_type=F32)
        do_ref[...] = do.astype(BF16)
        prod = do * o_ref[...]
        hi = prod.astype(BF16)
        lo = (prod - hi.astype(F32)).astype(BF16)
        e = e_ref[...]
        dl = jnp.dot(hi, e, preferred_element_type=F32) + jnp.dot(lo, e, preferred_element_type=F32)
        dl_ref[...] = dl
        if has_sink:
            lse_ref, sink_ref, dsink_ref = refs[4], refs[5], refs[8]
            part = -jnp.exp(sink_ref[...] - lse_ref[...]) * dl
            _accumulate(dsink_ref, jnp.sum(part, axis=0, keepdims=True), pl.program_id(0) == 0)

    row = pl.BlockSpec((tm, d), lambda i: (i, 0))
    narrow = pl.BlockSpec((tm, LANES), lambda i: (i, 0))
    args = [dx, w, o, expand]
    in_specs = [row, _resident(w.shape), row, _resident(expand.shape)]
    out_specs = [row, narrow]
    out_shape = [jax.ShapeDtypeStruct((t, d), BF16), jax.ShapeDtypeStruct((t, LANES), F32)]
    if has_sink:
        args += [lse, jnp.pad(sink.reshape(1, N_HEADS), ((0, 0), (0, LANES - N_HEADS)))]
        in_specs += [narrow, _resident((1, LANES))]
        out_specs.append(pl.BlockSpec((1, LANES), lambda i: (0, 0)))
        out_shape.append(jax.ShapeDtypeStruct((1, LANES), F32))
    return pl.pallas_call(
        body, name=f"attn_out_bwd_{tag}", grid=(t // tm,), in_specs=in_specs, out_specs=out_specs, out_shape=out_shape,
        compiler_params=_params("arbitrary" if has_sink else "parallel"),
    )(*args)


def _attn_bwd(qkv, do, lse, delta, cos, sin, group, dil, w, tag):
    b, s, n = qkv.shape
    length = s // dil
    wk = min(3 * w, length)
    nb = length // w
    ncol = n // QKV_W

    def body(qkv_ref, do_ref, lse_ref, dl_ref, cos_ref, sin_ref, dp_ref, acc_ref, dq_ref):
        acc_ref[...] = jnp.zeros_like(acc_ref)
        band = _band(w, wk)

        def block(i, carry):
            q0 = pl.multiple_of(i * w, w)
            k0 = pl.multiple_of(jnp.clip((i - 1) * w, 0, length - wk), w)
            valid = jnp.abs(band + (q0 - k0)) <= w
            rows, krows = pl.ds(q0, w), pl.ds(k0, wk)
            for kv in range(N_KV):
                q4 = _stack_heads(qkv_ref, rows, kv, HEAD_DIM)
                do4 = _stack_heads(do_ref, rows, kv, HEAD_DIM)
                lse4 = _stack_heads(lse_ref, rows, kv, 1)
                dl4 = _stack_heads(dl_ref, rows, kv, 1)
                k = qkv_ref[krows, pl.ds(Q_W + kv * HEAD_DIM, HEAD_DIM)]
                v = qkv_ref[krows, pl.ds(Q_W + KV_W + kv * HEAD_DIM, HEAD_DIM)]
                sc = lax.dot_general(q4, k, NT, preferred_element_type=F32) * SCALE
                p = jnp.exp(jnp.where(valid, sc, NEG_INF) - lse4)
                dp = lax.dot_general(do4, v, NT, preferred_element_type=F32)
                ds = (p * (dp - dl4) * SCALE).astype(BF16)
                dq4 = jnp.dot(ds, k, preferred_element_type=F32)
                for h in range(GRP):
                    dq_ref[:, pl.ds((kv * GRP + h) * HEAD_DIM, HEAD_DIM)] = dq4[h * w:(h + 1) * w]
                acc_ref[krows, pl.ds(kv * HEAD_DIM, HEAD_DIM)] += lax.dot_general(ds, q4, TN, preferred_element_type=F32)
                acc_ref[krows, pl.ds(KV_W + kv * HEAD_DIM, HEAD_DIM)] += lax.dot_general(
                    p.astype(BF16), do4, TN, preferred_element_type=F32)
            c, sn = cos_ref[rows, :], -sin_ref[rows, :]
            for j in range(Q_W // LANES):
                cols = slice(j * LANES, (j + 1) * LANES)
                dp_ref[rows, cols] = _rope(dq_ref[:, cols], c, sn).astype(BF16)
            return carry

        lax.fori_loop(0, nb, block, 0)
        c, sn = cos_ref[...], -sin_ref[...]
        for j in range(KV_W // LANES):
            dp_ref[:, Q_W + j * LANES:Q_W + (j + 1) * LANES] = _rope(acc_ref[:, j * LANES:(j + 1) * LANES], c, sn).astype(BF16)
        dp_ref[:, Q_W + KV_W:] = acc_ref[:, KV_W:].astype(BF16)

    narrow = pl.BlockSpec((None, length, LANES), lambda bi, r: (bi, 0, r))
    table = pl.BlockSpec((length, LANES), lambda bi, r: (0, r))
    out = pl.pallas_call(
        body, name=f"attn_bwd_{tag}", grid=(b, dil),
        in_specs=[pl.BlockSpec((None, length, QKV_W), lambda bi, r: (bi, 0, r * ncol + group)),
                  pl.BlockSpec((None, length, Q_W), lambda bi, r: (bi, 0, r)), narrow, narrow, table, table],
        out_specs=pl.BlockSpec((None, length, QKV_W), lambda bi, r: (bi, 0, r)),
        out_shape=jax.ShapeDtypeStruct((b, length, dil * QKV_W), BF16),
        scratch_shapes=[pltpu.VMEM((length, 2 * KV_W), F32), pltpu.VMEM((w, Q_W), F32)],
        compiler_params=_params("parallel", "parallel"),
    )(qkv.reshape(b, length, dil * n), do.reshape(b, length, dil * Q_W), lse.reshape(b, length, dil * LANES),
      delta.reshape(b, length, dil * LANES), cos.reshape(length, dil * LANES), sin.reshape(length, dil * LANES))
    return out.reshape(b * s, QKV_W)


def _qkv_bwd(dy, x, gain, w, dps, tag):
    t, d = x.shape
    tm = min(256, t)
    ng = len(dps)

    def body(dy_ref, x_ref, gain_ref, w_ref, *refs):
        dp_refs, (dx_ref, h_ref, dgain_ref) = refs[:ng], refs[ng:]
        dh = None
        for gi, dp_ref in enumerate(dp_refs):
            part = lax.dot_general(dp_ref[...], w_ref[:, gi * QKV_W:(gi + 1) * QKV_W], NT, preferred_element_type=F32)
            dh = part if dh is None else dh + part
        xv, gain_v = x_ref[...], gain_ref[...]
        dx, xhat = _rms_bwd(dh, xv, gain_v)
        dx_ref[...] = dy_ref[...] + dx
        h_ref[...] = (xhat * gain_v).astype(BF16)
        _accumulate(dgain_ref, jnp.sum(dh * xhat, axis=0, keepdims=True), pl.program_id(0) == 0)

    row = pl.BlockSpec((tm, d), lambda i: (i, 0))
    return pl.pallas_call(
        body, name=f"qkv_bwd_{tag}", grid=(t // tm,),
        in_specs=[row, row, _resident((1, d)), _resident(w.shape)] + [pl.BlockSpec((tm, QKV_W), lambda i: (i, 0))] * ng,
        out_specs=[row, row, pl.BlockSpec((1, d), lambda i: (0, 0))],
        out_shape=[jax.ShapeDtypeStruct((t, d), F32), jax.ShapeDtypeStruct((t, d), BF16), jax.ShapeDtypeStruct((1, d), F32)],
        compiler_params=_params("arbitrary"),
    )(dy, x, gain, w, *dps)


ANY = pl.BlockSpec(memory_space=pl.ANY)


def _place():
    x, y, c = lax.axis_index("x"), lax.axis_index("y"), lax.axis_index("c")
    return x, y, c


def _chip_xy(chip):
    return chip // 2, chip % 2


def _all_gather(shards):
    n = len(shards)

    def body(*refs):
        ins, outs = refs[:n], refs[n:2 * n]
        send_sems, recv_sems, local_sems = refs[2 * n:]
        x, y, c = _place()
        sibling = (x, y, 1 - c)
        chips = [(1 - x, y), (x, 1 - y), (1 - x, 1 - y)]

        def copy(a, k, block, to, src=None):
            px, py, pc = block
            rows = outs[a].at[4 * px + 2 * py + pc]
            return pltpu.make_async_remote_copy(
                src_ref=rows if src is None else src, dst_ref=rows, send_sem=send_sems.at[a, k], recv_sem=recv_sems.at[a, k],
                device_id=to, device_id_type=MESH)

        sent = []
        for a in range(n):
            mine = pltpu.make_async_copy(ins[a], outs[a].at[4 * x + 2 * y + c], local_sems.at[a])
            mine.start()
            sent.append(mine)
        for a in range(n):
            first = [copy(a, 0, (x, y, c), sibling, src=ins[a])]
            first += [copy(a, 1 + j, (x, y, c), (*chip, c), src=ins[a]) for j, chip in enumerate(chips)]
            for cp in first:
                cp.start()
            sent += first
        for a in range(n):
            for j, chip in enumerate(chips):
                copy(a, 1 + j, (*chip, c), (x, y, c)).wait_recv()
                passed = copy(a, 4 + j, (*chip, c), sibling)
                passed.start()
                sent.append(passed)
        for a in range(n):
            copy(a, 0, sibling, (x, y, c)).wait_recv()
            for j, chip in enumerate(chips):
                copy(a, 4 + j, (*chip, 1 - c), (x, y, c)).wait_recv()
        for cp in sent[n:]:
            cp.wait_send()
        for mine in sent[:n]:
            mine.wait()

    return pl.pallas_call(
        body, name="all_gather_weights", in_specs=[ANY] * n, out_specs=[ANY] * n,
        out_shape=[jax.ShapeDtypeStruct((N_DEV,) + s.shape, s.dtype) for s in shards],
        scratch_shapes=[pltpu.SemaphoreType.DMA((n, 7)), pltpu.SemaphoreType.DMA((n, 7)), pltpu.SemaphoreType.DMA((n,))],
    )(*shards)


def _exchange_siblings(parts):
    n = len(parts)

    def body(*refs):
        ins, outs = refs[:n], refs[n:2 * n]
        send_sems, recv_sems = refs[2 * n:]
        x, y, c = _place()
        chip = 2 * x + y
        copies = []
        for a in range(n):
            for rel in range(N_CHIP):
                copies.append(pltpu.make_async_remote_copy(
                    src_ref=ins[a].at[2 * (chip ^ rel) + (1 - c)], dst_ref=outs[a].at[rel],
                    send_sem=send_sems.at[a, rel], recv_sem=recv_sems.at[a, rel], device_id=(x, y, 1 - c), device_id_type=MESH))
        for cp in copies:
            cp.start()
        for cp in copies:
            cp.wait()

    return pl.pallas_call(
        body, name="reduce_scatter_d2d", in_specs=[ANY] * n, out_specs=[ANY] * n,
        out_shape=[jax.ShapeDtypeStruct((N_CHIP,) + p.shape[1:], p.dtype) for p in parts],
        scratch_shapes=[pltpu.SemaphoreType.DMA((n, N_CHIP)), pltpu.SemaphoreType.DMA((n, N_CHIP))],
    )(*parts)


def _pair_sum(part, recv, own_blocks, tag):
    _, r, c = part.shape
    tr = r // 2 if r % 16 == 0 else r

    def body(ids_ref, p_ref, r_ref, o_ref):
        o_ref[...] = (p_ref[...] + r_ref[...]).astype(BF16)

    return pl.pallas_call(
        body, name=f"pair_sum_{tag}",
        grid_spec=pltpu.PrefetchScalarGridSpec(
            num_scalar_prefetch=1, grid=(N_CHIP, r // tr),
            in_specs=[pl.BlockSpec((None, tr, c), lambda rel, i, ids: (ids[rel], i, 0)),
                      pl.BlockSpec((None, tr, c), lambda rel, i, ids: (rel, i, 0))],
            out_specs=pl.BlockSpec((None, tr, c), lambda rel, i, ids: (rel, i, 0))),
        out_shape=jax.ShapeDtypeStruct((N_CHIP, r, c), BF16),
        compiler_params=_params("parallel", "parallel"),
    )(own_blocks, part, recv)


def _exchange_chips(sums):
    n = len(sums)

    def body(*refs):
        ins, outs = refs[:n], refs[n:2 * n]
        send_sems, recv_sems = refs[2 * n:]
        x, y, c = _place()
        chip = 2 * x + y
        copies = []
        for a in range(n):
            for rel in range(1, N_CHIP):
                copies.append(pltpu.make_async_remote_copy(
                    src_ref=ins[a].at[rel], dst_ref=outs[a].at[rel - 1], send_sem=send_sems.at[a, rel - 1],
                    recv_sem=recv_sems.at[a, rel - 1], device_id=(*_chip_xy(chip ^ rel), c), device_id_type=MESH))
        for cp in copies:
            cp.start()
        for cp in copies:
            cp.wait()

    return pl.pallas_call(
        body, name="reduce_scatter_ici", in_specs=[ANY] * n, out_specs=[ANY] * n,
        out_shape=[jax.ShapeDtypeStruct((N_CHIP - 1,) + s.shape[1:], s.dtype) for s in sums],
        scratch_shapes=[pltpu.SemaphoreType.DMA((n, N_CHIP - 1)), pltpu.SemaphoreType.DMA((n, N_CHIP - 1))],
    )(*sums)


def _all_reduce_small(v):
    def body(v_ref, o_ref, recv_ref, send_sems, recv_sems):
        x, y, c = _place()
        me = 4 * x + 2 * y + c
        copies = []
        for k in range(1, N_DEV):
            peer = me ^ k
            copies.append(pltpu.make_async_remote_copy(
                src_ref=v_ref, dst_ref=recv_ref.at[k], send_sem=send_sems.at[k - 1], recv_sem=recv_sems.at[k - 1],
                device_id=(peer // 4, (peer // 2) % 2, peer % 2), device_id_type=MESH))
        for cp in copies:
            cp.start()
        recv_ref[0] = v_ref[...]
        for cp in copies:
            cp.wait()
        acc = recv_ref[me]
        for src in range(1, N_DEV):
            acc = acc + recv_ref[me ^ src]
        o_ref[...] = acc

    vm = pl.BlockSpec(memory_space=pltpu.VMEM)
    return pl.pallas_call(
        body, name="all_reduce_small", in_specs=[vm], out_specs=vm, out_shape=jax.ShapeDtypeStruct(v.shape, F32),
        scratch_shapes=[pltpu.VMEM((N_DEV,) + v.shape, F32), pltpu.SemaphoreType.DMA((N_DEV - 1,)),
                        pltpu.SemaphoreType.DMA((N_DEV - 1,))],
    )(v)


def _adamw_math(w, g, m, v):
    m = ADAM_B1 * m + (1.0 - ADAM_B1) * g
    v = ADAM_B2 * v + (1.0 - ADAM_B2) * (g * g)
    m_hat = m / (1.0 - ADAM_B1 ** ADAM_STEP)
    v_hat = v / (1.0 - ADAM_B2 ** ADAM_STEP)
    delta = -ADAM_LR * (m_hat / (jnp.sqrt(v_hat) + ADAM_EPS) + ADAM_WD * w)
    return delta, m, v


def _adamw(parts, w, m, v, name):
    r, c = w.shape
    tr = r // 2 if r % 16 == 0 and r >= 256 else r
    n = len(parts)

    def body(*refs):
        w_ref, m_ref, v_ref, g_ref, d_ref, nm_ref, nv_ref = refs[n:]
        g = refs[0][...].astype(F32)
        for p_ref in refs[1:n]:
            g = g + p_ref[...].astype(F32)
        g_ref[...] = g
        d_ref[...], nm_ref[...], nv_ref[...] = _adamw_math(w_ref[...], g, m_ref[...], v_ref[...])

    tile = pl.BlockSpec((tr, c), lambda i: (i, 0))
    arrays, in_specs = [], []
    for p in parts:
        if isinstance(p, tuple):
            arrays.append(p[0])
            in_specs.append(pl.BlockSpec((None, tr, c), functools.partial(lambda i, slot: (slot, i, 0), slot=p[1])))
        else:
            arrays.append(p)
            in_specs.append(tile)
    return pl.pallas_call(
        body, name=name, grid=(r // tr,), in_specs=in_specs + [tile] * 3, out_specs=[tile] * 4,
        out_shape=[jax.ShapeDtypeStruct((r, c), F32)] * 4, compiler_params=_params("parallel"),
    )(*arrays, w, m, v)


def _columns(g):
    return g.transpose(1, 0, 2).reshape(g.shape[1], -1)


def _rows(g):
    return g.reshape(-1, g.shape[-1])


def _column_blocks(dw):
    k, n = dw.shape
    return dw.reshape(k, N_DEV, n // N_DEV).transpose(1, 0, 2)


def _row_blocks(dw):
    k, n = dw.shape
    return dw.reshape(N_DEV, k // N_DEV, n)


def _pack_rows(rows, width):
    out = None
    for i, r in enumerate(rows):
        r = r.reshape(1, -1).astype(F32)
        r = jnp.pad(r, ((i, 8 - 1 - i), (0, width - r.shape[1])))
        out = r if out is None else out + r
    return out


def _mixer_bwd(dy, x_in, gain, w_in, w_out, qkv, o, lse, cos, sin, groups, tag, sink=None):
    bl, seq, _ = qkv.shape
    outs = _attn_out_bwd(dy, w_out, o, tag, lse=lse if sink is not None else None, sink=sink)
    do, delta = outs[0], outs[1]
    (dw_out,) = _tn_matmul(o, [dy], f"dw_out_{tag}")
    do3, delta3, lse3 = do.reshape(bl, seq, -1), delta.reshape(bl, seq, -1), lse.reshape(bl, seq, -1)
    dps = [_attn_bwd(qkv, do3, lse3, delta3, cos, sin, gi, dil, w, f"{tag}g{gi}") for gi, (dil, w) in enumerate(groups)]
    dx, h, dgain = _qkv_bwd(dy, x_in, gain, w_in, dps, tag)
    dw_in = _tn_matmul(h, dps, f"dw_in_{tag}")
    dw_in = dw_in[0] if len(dw_in) == 1 else jnp.concatenate(dw_in, axis=1)
    return (dx, dw_in, dw_out, dgain) + tuple(outs[2:])


def _ffn_layer_bwd(dy, x_in, gain, g, u, wg, wu, wd, tag):
    dx, dg, du, act, h, dgain = _ffn_bwd(dy, x_in, gain, g, u, wg, wu, wd, tag)
    (dwd,) = _tn_matmul(act, [dy], f"dw_down_{tag}")
    dwg, dwu = _tn_matmul(h, [dg, du], f"dw_gate_up_{tag}")
    return dx, dwg, dwu, dwd, dgain


def kernel(x, a_w_in, a_sink, a_w_out, b_w_in, b_w_out, norm_mix, norm_ffn, w_gate, w_up, w_down, final_norm, loss_target, m_a_w_in, m_a_sink, m_a_w_out, m_b_w_in, m_b_w_out, m_norm_mix, m_norm_ffn, m_w_gate, m_w_up, m_w_down, m_final_norm, v_a_w_in, v_a_sink, v_a_w_out, v_b_w_in, v_b_w_out, v_norm_mix, v_norm_ffn, v_w_gate, v_w_up, v_w_down, v_final_norm):
    bl, seq, d = x.shape
    t = bl * seq
    xf = x.reshape(t, d)
    target = loss_target.reshape(t, d)
    cos, sin = _rope_tables(seq)
    groups_a = [(1, ATTN_HALF_WINDOW)]
    groups_b = [(dil, window // 2 // dil) for window, dil in DILATED_GROUPS]

    col_shards = [a_w_in[0], b_w_in[0], w_gate[0], w_gate[1], w_up[0], w_up[1]]
    row_shards = [a_w_out[0], b_w_out[0], w_down[0], w_down[1]]
    gathered = _all_gather([s.astype(BF16) for s in col_shards + row_shards])
    wa_in, wb_in, wg0, wg1, wu0, wu1 = [_columns(g) for g in gathered[:6]]
    wa_out, wb_out, wd0, wd1 = [_rows(g) for g in gathered[6:]]

    qkv0 = _qkv_proj(xf, norm_mix[0:1], wa_in, cos, sin, seq, "a").reshape(bl, seq, -1)
    o0, lse0 = _attn_fwd(qkv0, 0, 1, ATTN_HALF_WINDOW, "a", sink=a_sink[0])
    o0, lse0 = o0.reshape(t, d), lse0.reshape(t, LANES)
    x1_0 = _out_proj(xf, o0, wa_out, "a")
    x2_0, g0, u0 = _ffn_fwd(x1_0, norm_ffn[0:1], wg0, wu0, wd0, "0")

    qkv1 = _qkv_proj(x2_0, norm_mix[1:2], wb_in, cos, sin, seq, "b").reshape(bl, seq, -1)
    prev = None
    for gi, (dil, w) in enumerate(groups_b):
        prev = _attn_fwd(qkv1, gi, dil, w, f"b{gi}", prev=prev)
    o1, lse1 = prev[0].reshape(t, d), prev[1].reshape(t, LANES)
    x1_1 = _out_proj(x2_0, o1, wb_out, "b")
    x2_1, g1, u1 = _ffn_fwd(x1_1, norm_ffn[1:2], wg1, wu1, wd1, "1")

    dy, loss_part, d_final = _loss_bwd(x2_1, final_norm.reshape(1, d), target)
    dy, dwg1, dwu1, dwd1, d_nf1 = _ffn_layer_bwd(dy, x1_1, norm_ffn[1:2], g1, u1, wg1, wu1, wd1, "1")
    dy, dwb_in, dwb_out, d_nm1 = _mixer_bwd(dy, x2_0, norm_mix[1:2], wb_in, wb_out, qkv1, o1, lse1, cos, sin, groups_b, "b")
    dy, dwg0, dwu0, dwd0, d_nf0 = _ffn_layer_bwd(dy, x1_0, norm_ffn[0:1], g0, u0, wg0, wu0, wd0, "0")
    dy, dwa_in, dwa_out, d_nm0, d_sink = _mixer_bwd(dy, xf, norm_mix[0:1], wa_in, wa_out, qkv0, o0, lse0, cos, sin, groups_a,
                                                    "a", sink=a_sink[0])
    grad_x = dy.reshape(bl, seq, d)

    parts = [_column_blocks(g) for g in (dwa_in, dwb_in, dwg0, dwg1, dwu0, dwu1)]
    parts += [_row_blocks(g) for g in (dwa_out, dwb_out, dwd0, dwd1)]
    cx, cy, cc = lax.axis_index("x"), lax.axis_index("y"), lax.axis_index("c")
    own_blocks = jnp.stack([2 * ((2 * cx + cy) ^ rel) + cc for rel in range(N_CHIP)]).astype(jnp.int32)
    from_sibling = _exchange_siblings(parts)
    sums = [_pair_sum(p, r, own_blocks, str(a)) for a, (p, r) in enumerate(zip(parts, from_sibling))]
    from_chips = _exchange_chips(sums)

    def shard2d(a):
        return a.reshape(-1, a.shape[-1])

    big = [(a_w_in, m_a_w_in, v_a_w_in), (b_w_in, m_b_w_in, v_b_w_in),
           (w_gate[0], m_w_gate[0], v_w_gate[0]), (w_gate[1], m_w_gate[1], v_w_gate[1]),
           (w_up[0], m_w_up[0], v_w_up[0]), (w_up[1], m_w_up[1], v_w_up[1]),
           (a_w_out, m_a_w_out, v_a_w_out), (b_w_out, m_b_w_out, v_b_w_out),
           (w_down[0], m_w_down[0], v_w_down[0]), (w_down[1], m_w_down[1], v_w_down[1])]
    upd = []
    for a, (w_, m_, v_) in enumerate(big):
        grad_parts = [(sums[a], 0)] + [(from_chips[a], rel) for rel in range(N_CHIP - 1)]
        upd.append(_adamw(grad_parts, shard2d(w_), shard2d(m_), shard2d(v_), f"adamw_{a}"))
    (u_a_in, u_b_in, u_g0, u_g1, u_u0, u_u1, u_a_out, u_b_out, u_d0, u_d1) = upd

    small = _pack_rows([d_nm0, d_nm1, d_nf0, d_nf1, d_final, d_sink, loss_part], d)
    total = _all_reduce_small(small)
    small_w = _pack_rows([norm_mix[0], norm_mix[1], norm_ffn[0], norm_ffn[1], final_norm, a_sink], d)
    small_m = _pack_rows([m_norm_mix[0], m_norm_mix[1], m_norm_ffn[0], m_norm_ffn[1], m_final_norm, m_a_sink], d)
    small_v = _pack_rows([v_norm_mix[0], v_norm_mix[1], v_norm_ffn[0], v_norm_ffn[1], v_final_norm, v_a_sink], d)
    u_small = _adamw([total], small_w, small_m, small_v, "adamw_small")
    loss = total[6, 0]

    outs = []
    for k in range(4):
        sm = u_small[k]
        outs += [
            u_a_in[k].reshape(a_w_in.shape), sm[5:6, :N_HEADS], u_a_out[k].reshape(a_w_out.shape),
            u_b_in[k].reshape(b_w_in.shape), u_b_out[k].reshape(b_w_out.shape), sm[0:2], sm[2:4],
            jnp.stack([u_g0[k], u_g1[k]]), jnp.stack([u_u0[k], u_u1[k]]), jnp.stack([u_d0[k], u_d1[k]]), sm[4],
        ]
    return (loss, grad_x, *outs)
```

```python
import functools
import math

import jax
import jax.numpy as jnp
from jax import lax
from jax.experimental import pallas as pl
from jax.experimental.pallas import tpu as pltpu

F32 = jnp.float32
BF16 = jnp.bfloat16

HEAD_DIM = 64
N_HEADS = 16
N_KV = 4
GRP = N_HEADS // N_KV
Q_W = N_HEADS * HEAD_DIM
KV_W = N_KV * HEAD_DIM
QKV_W = Q_W + 2 * KV_W
ATTN_HALF_WINDOW = 128
DILATED_GROUPS = ((128, 1), (512, 4), (2048, 16))
ROPE_THETA = 10000.0
RMS_EPS = 1e-6
NEG_INF = -1e30
SCALE = 1.0 / math.sqrt(HEAD_DIM)

ADAM_LR = 0.001
ADAM_B1 = 0.9
ADAM_B2 = 0.999
ADAM_EPS = 1e-08
ADAM_WD = 0.01
ADAM_STEP = 10

LANES = 128
VMEM_LIMIT = 56 * 1024 * 1024
QUERY_BLOCK = 128
N_DEV = 8
N_CHIP = 4
MESH = pl.DeviceIdType.MESH

NT = (((1,), (1,)), ((), ()))
TN = (((0,), (0,)), ((), ()))


def _params(*sem):
    return pltpu.CompilerParams(dimension_semantics=tuple(sem) if sem else None, vmem_limit_bytes=VMEM_LIMIT)


def _resident(shape):
    return pl.BlockSpec(shape, lambda *_: (0,) * len(shape), pipeline_mode=pl.Buffered(1))


def _rope_tables(seq):
    inv_freq = 1.0 / (ROPE_THETA ** (jnp.arange(0, HEAD_DIM, 2, dtype=F32) / HEAD_DIM))
    ang = jnp.arange(seq, dtype=F32)[:, None] * inv_freq[None, :]
    cos, sin = jnp.cos(ang), jnp.sin(ang)
    return jnp.tile(cos, (1, 4)), jnp.concatenate([-sin, sin, -sin, sin], axis=1)


def _rope(t, cos, sin_signed):
    lane = lax.broadcasted_iota(jnp.int32, t.shape, 1)
    first = (lane & (HEAD_DIM // 2)) == 0
    swapped = jnp.where(first, pltpu.roll(t, LANES - HEAD_DIM // 2, 1), pltpu.roll(t, HEAD_DIM // 2, 1))
    return t * cos + swapped * sin_signed


def _rms(x):
    return lax.rsqrt(jnp.mean(x * x, axis=-1, keepdims=True) + RMS_EPS)


def _rms_bwd(dh, x, gain):
    r = _rms(x)
    xhat = x * r
    dxh = dh * gain
    dx = r * (dxh - xhat * jnp.mean(dxh * xhat, axis=-1, keepdims=True))
    return dx, xhat


def _accumulate(ref, value, first):
    @pl.when(first)
    def _():
        ref[...] = jnp.zeros_like(ref)

    ref[...] += value


def _tile_rows(seq):
    return min(512, seq)


def _res_shape(bl, seq, dil, c):
    ts = _tile_rows(seq)
    return (bl, dil, seq // ts, ts // dil, c)


def _res_spec(seq, dil, c):
    ts = _tile_rows(seq)
    per_seq = seq // ts
    return pl.BlockSpec((None, dil, None, ts // dil, c), lambda i: (i // per_seq, 0, i % per_seq, 0, 0))


def _seq_view(a):
    bl, dil, tiles, n, c = a.shape
    return a.reshape(bl * dil, tiles * n, c)


def _stage(ts, c):
    return pltpu.VMEM((c // LANES, ts, LANES), F32)


def _split_rows(val, stage_ref, dil):
    if dil == 1:
        return [val]
    ts, c = val.shape
    n, nc = ts // dil, c // LANES
    for k in range(nc):
        stage_ref[k] = val[:, k * LANES:(k + 1) * LANES]
    return [jnp.concatenate([stage_ref[k, pl.ds(r, n, stride=dil), :] for k in range(nc)], axis=1) for r in range(dil)]


def _merge_rows(parts, stage_ref, dil):
    if dil == 1:
        return parts[0]
    n, c = parts[0].shape
    nc = c // LANES
    for r, part in enumerate(parts):
        for k in range(nc):
            stage_ref[k, pl.ds(r, n, stride=dil), :] = part[:, k * LANES:(k + 1) * LANES]
    return jnp.concatenate([stage_ref[k] for k in range(nc)], axis=1)


def _tables_tiled(table, seq, dil):
    ts = _tile_rows(seq)
    return table.reshape(seq // ts, ts // dil, dil, LANES).transpose(0, 2, 1, 3).reshape(seq, LANES)


def _tables_by_residue(table, seq, dil):
    return table.reshape(seq // dil, dil, LANES).transpose(1, 0, 2)


def _qkv_proj(x, gain, w, cos, sin, seq, dil, group, tag):
    t, d = x.shape
    ts = _tile_rows(seq)
    n = ts // dil
    per_seq = seq // ts

    def body(x_ref, g_ref, w_ref, cos_ref, sin_ref, o_ref, stage_ref):
        xv = jnp.concatenate(_split_rows(x_ref[...], stage_ref, dil), axis=0)
        h = (xv * _rms(xv) * g_ref[...]).astype(BF16)
        acc = jnp.dot(h, w_ref[...], preferred_element_type=F32)
        c, s = cos_ref[...], sin_ref[...]
        for j in range(QKV_W // LANES):
            cols = slice(j * LANES, (j + 1) * LANES)
            val = acc[:, cols]
            if j < (Q_W + KV_W) // LANES:
                val = _rope(val, c, s)
            if j < Q_W // LANES:
                val = val * SCALE
            val = val.astype(BF16)
            for r in range(dil):
                o_ref[r, :, cols] = val[r * n:(r + 1) * n]

    table = pl.BlockSpec((ts, LANES), lambda i: (i % per_seq, 0))
    return pl.pallas_call(
        body, name=f"qkv_proj_{tag}", grid=(t // ts,),
        in_specs=[pl.BlockSpec((ts, d), lambda i: (i, 0)), pl.BlockSpec((1, d), lambda i: (0, 0)),
                  pl.BlockSpec((d, QKV_W), lambda i: (0, group)), table, table],
        out_specs=_res_spec(seq, dil, QKV_W),
        out_shape=jax.ShapeDtypeStruct(_res_shape(t // seq, seq, dil, QKV_W), BF16),
        scratch_shapes=[_stage(ts, d)],
        compiler_params=_params("parallel"),
    )(x, gain, w, cos, sin)


def _band(bq, wk):
    return lax.broadcasted_iota(jnp.int32, (bq, wk), 0) - lax.broadcasted_iota(jnp.int32, (bq, wk), 1)


def _pair_variants(src_ref, base, dst_ref):
    lo = lax.broadcasted_iota(jnp.int32, (src_ref.shape[0], LANES), 1) < HEAD_DIM
    for c in range(KV_W // LANES):
        chunk = src_ref[:, base + c * LANES:base + (c + 1) * LANES]
        rolled = pltpu.roll(chunk, HEAD_DIM, 1)
        zero = jnp.zeros_like(chunk)
        dst_ref[2 * c, 0] = jnp.where(lo, chunk, zero)
        dst_ref[2 * c, 1] = jnp.where(lo, zero, rolled)
        dst_ref[2 * c + 1, 0] = jnp.where(lo, rolled, zero)
        dst_ref[2 * c + 1, 1] = jnp.where(lo, zero, chunk)


def _over_keys(col, wk):
    if wk % LANES:
        return jnp.broadcast_to(col, (col.shape[0], wk))
    wide = jnp.broadcast_to(col, (col.shape[0], LANES))
    return wide if wk == LANES else jnp.concatenate([wide] * (wk // LANES), axis=1)


def _window(i, bq, wk, length):
    q0 = pl.multiple_of(i * bq, bq)
    k0 = pl.multiple_of(jnp.clip(q0 - bq, 0, length - wk), bq)
    return q0, k0


def _attn_fwd(qkv, w, tag, sink=None):
    shape = qkv.shape
    rows_all = _seq_view(qkv)
    nseq, length, _ = rows_all.shape
    bq = min(QUERY_BLOCK, length)
    wk = min(3 * bq, length)
    nb = length // bq
    has_sink = sink is not None

    def body(*refs):
        qkv_ref = refs[0]
        sink_ref = refs[1] if has_sink else None
        o_ref, lse_ref, kk_ref, vv_ref = refs[-4:]
        _pair_variants(qkv_ref, Q_W, kk_ref)
        _pair_variants(qkv_ref, Q_W + KV_W, vv_ref)
        band = _band(bq, wk)
        lane = lax.broadcasted_iota(jnp.int32, (bq, LANES), 1)
        lo = lane < HEAD_DIM

        def block(i, carry):
            q0, k0 = _window(i, bq, wk, length)
            valid = jnp.abs(band + (q0 - k0)) <= w
            rows, krows = pl.ds(q0, bq), pl.ds(k0, wk)
            lse_tile = jnp.zeros((bq, LANES), F32)
            for kv in range(N_KV):
                heads = [(kv * GRP + h, h % 2) for h in range(GRP)]
                qp = [qkv_ref[rows, (kv * 2 + j) * LANES:(kv * 2 + j + 1) * LANES] for j in range(GRP // 2)]
                sc = [lax.dot_general(qp[h // 2], kk_ref[kv, half, krows, :], NT, preferred_element_type=F32)
                      for h, (_, half) in enumerate(heads)]
                sc = [jnp.where(valid, s_, NEG_INF) for s_ in sc]
                m = [jnp.max(s_, axis=-1, keepdims=True) for s_ in sc]
                if has_sink:
                    m = [jnp.maximum(m_, sink_ref[hd]) for m_, (hd, _) in zip(m, heads)]
                mb = [jnp.broadcast_to(m_, (bq, LANES)) for m_ in m]
                p = [jnp.exp(s_ - _over_keys(m_, wk)) for s_, m_ in zip(sc, m)]
                den = [jnp.sum(p_, axis=-1, keepdims=True) for p_ in p]
                if has_sink:
                    den = [d_ + jnp.exp(sink_ref[hd] - m_) for d_, m_, (hd, _) in zip(den, m, heads)]
                inv = [jnp.broadcast_to(1.0 / d_, (bq, LANES)) for d_ in den]
                pb = [p_.astype(BF16) for p_ in p]
                for j in range(GRP // 2):
                    o = (jnp.dot(pb[2 * j], vv_ref[kv, 0, krows, :], preferred_element_type=F32)
                         + jnp.dot(pb[2 * j + 1], vv_ref[kv, 1, krows, :], preferred_element_type=F32))
                    o = o * jnp.where(lo, inv[2 * j], inv[2 * j + 1])
                    o_ref[rows, (kv * 2 + j) * LANES:(kv * 2 + j + 1) * LANES] = o.astype(BF16)
                for h, (hd, _) in enumerate(heads):
                    lse_tile = jnp.where(lane == hd, mb[h] - jnp.log(inv[h]), lse_tile)
            lse_ref[rows, :] = lse_tile
            return carry

        lax.fori_loop(0, nb, block, 0)

    args = [rows_all]
    in_specs = [pl.BlockSpec((None, length, QKV_W), lambda i: (i, 0, 0))]
    if has_sink:
        args.append(sink)
        in_specs.append(pl.BlockSpec(memory_space=pltpu.SMEM))
    o, lse = pl.pallas_call(
        body, name=f"attn_fwd_{tag}", grid=(nseq,), in_specs=in_specs,
        out_specs=[pl.BlockSpec((None, length, Q_W), lambda i: (i, 0, 0)), pl.BlockSpec((None, length, LANES), lambda i: (i, 0, 0))],
        out_shape=[jax.ShapeDtypeStruct((nseq, length, Q_W), BF16), jax.ShapeDtypeStruct((nseq, length, LANES), F32)],
        scratch_shapes=[pltpu.VMEM((N_KV, 2, length, LANES), BF16), pltpu.VMEM((N_KV, 2, length, LANES), BF16)],
        compiler_params=_params("parallel"),
    )(*args)
    return o.reshape(shape[:-1] + (Q_W,)), lse.reshape(shape[:-1] + (LANES,))


def _head_expand():
    return (jnp.arange(LANES)[:, None] == jnp.arange(Q_W)[None, :] // HEAD_DIM).astype(BF16)


def _out_proj(x, os, lses, dils, w, seq, tag):
    t, d = x.shape
    ts = _tile_rows(seq)
    ng = len(os)
    bl = t // seq
    if ng == 1:
        def body1(x_ref, o_ref, w_ref, y_ref):
            y_ref[...] = x_ref[...] + jnp.dot(o_ref[...], w_ref[...], preferred_element_type=F32)

        row = pl.BlockSpec((ts, d), lambda i: (i, 0))
        o = os[0].reshape(t, Q_W)
        y = pl.pallas_call(
            body1, name=f"out_proj_{tag}", grid=(t // ts,), in_specs=[row, row, _resident(w.shape)], out_specs=row,
            out_shape=jax.ShapeDtypeStruct((t, d), F32), compiler_params=_params("parallel"),
        )(x, o, w)
        return y, o, [lses[0]]

    def body(*refs):
        x_ref, w_ref, e_ref = refs[:3]
        o_refs, l_refs = refs[3:3 + ng], refs[3 + ng:3 + 2 * ng]
        y_ref, om_ref = refs[3 + 2 * ng:5 + 2 * ng]
        lt_refs = refs[5 + 2 * ng:5 + 3 * ng]
        wide_ref, narrow_ref = refs[5 + 3 * ng:]
        ls = [_merge_rows([l_refs[g][r] for r in range(dils[g])], narrow_ref, dils[g]) for g in range(ng)]
        mx = functools.reduce(jnp.maximum, ls)
        tot = mx + jnp.log(functools.reduce(lambda a, b: a + b, [jnp.exp(l_ - mx) for l_ in ls]))
        e = e_ref[...]
        o = None
        for g in range(ng):
            wt = jnp.exp(ls[g] - tot)
            hi = wt.astype(BF16)
            lo = (wt - hi.astype(F32)).astype(BF16)
            wide = jnp.dot(hi, e, preferred_element_type=F32) + jnp.dot(lo, e, preferred_element_type=F32)
            term = wide * _merge_rows([o_refs[g][r].astype(F32) for r in range(dils[g])], wide_ref, dils[g])
            o = term if o is None else o + term
        ob = o.astype(BF16)
        om_ref[...] = ob
        y_ref[...] = x_ref[...] + jnp.dot(ob, w_ref[...], preferred_element_type=F32)
        for g in range(ng):
            for r, part in enumerate(_split_rows(tot, narrow_ref, dils[g])):
                lt_refs[g][r] = part

    row = pl.BlockSpec((ts, d), lambda i: (i, 0))
    e = _head_expand()
    outs = pl.pallas_call(
        body, name=f"out_proj_{tag}", grid=(t // ts,),
        in_specs=[row, _resident(w.shape), _resident(e.shape)] + [_res_spec(seq, dl, Q_W) for dl in dils]
                 + [_res_spec(seq, dl, LANES) for dl in dils],
        out_specs=[row, pl.BlockSpec((ts, Q_W), lambda i: (i, 0))] + [_res_spec(seq, dl, LANES) for dl in dils],
        out_shape=[jax.ShapeDtypeStruct((t, d), F32), jax.ShapeDtypeStruct((t, Q_W), BF16)]
                  + [jax.ShapeDtypeStruct(_res_shape(bl, seq, dl, LANES), F32) for dl in dils],
        scratch_shapes=[_stage(ts, Q_W), _stage(ts, LANES)],
        compiler_params=_params("parallel"),
    )(x, w, e, *os, *lses)
    return outs[0], outs[1], list(outs[2:])


def _sigmoid(g):
    return 1.0 / (1.0 + jnp.exp(-g))


def _ffn_fwd(x, gain, wg, wu, wd, tag):
    t, d = x.shape
    f = wg.shape[1]
    tm = min(256, t)

    def body(x_ref, gain_ref, wg_ref, wu_ref, wd_ref, y_ref, g_ref, u_ref):
        xv = x_ref[...]
        h = (xv * _rms(xv) * gain_ref[...]).astype(BF16)
        g = jnp.dot(h, wg_ref[...], preferred_element_type=F32)
        u = jnp.dot(h, wu_ref[...], preferred_element_type=F32)
        g_ref[...] = g.astype(BF16)
        u_ref[...] = u.astype(BF16)
        a = (g * _sigmoid(g) * u).astype(BF16)
        y_ref[...] = xv + jnp.dot(a, wd_ref[...], preferred_element_type=F32)

    row = pl.BlockSpec((tm, d), lambda i: (i, 0))
    wide = pl.BlockSpec((tm, f), lambda i: (i, 0))
    return pl.pallas_call(
        body, name=f"ffn_fwd_{tag}", grid=(t // tm,),
        in_specs=[row, _resident((1, d)), _resident(wg.shape), _resident(wu.shape), _resident(wd.shape)],
        out_specs=[row, wide, wide],
        out_shape=[jax.ShapeDtypeStruct((t, d), F32), jax.ShapeDtypeStruct((t, f), BF16), jax.ShapeDtypeStruct((t, f), BF16)],
        compiler_params=_params("parallel"),
    )(x, gain, wg, wu, wd)


def _loss_bwd(x, gain, target):
    t, d = x.shape
    tm = min(512, t)

    def body(x_ref, gain_ref, t_ref, dx_ref, loss_ref, dgain_ref):
        xv, gain_v = x_ref[...], gain_ref[...]
        xhat = xv * _rms(xv)
        err = xhat * gain_v - t_ref[...]
        dy = err * (1.0 / d)
        dx, _ = _rms_bwd(dy, xv, gain_v)
        dx_ref[...] = dx
        first = pl.program_id(0) == 0
        part = 0.5 * jnp.sum(jnp.mean(err * err, axis=-1, keepdims=True), axis=0, keepdims=True)
        _accumulate(loss_ref, jnp.broadcast_to(part, loss_ref.shape), first)
        _accumulate(dgain_ref, jnp.sum(dy * xhat, axis=0, keepdims=True), first)

    row = pl.BlockSpec((tm, d), lambda i: (i, 0))
    return pl.pallas_call(
        body, name="loss_bwd", grid=(t // tm,), in_specs=[row, _resident((1, d)), row],
        out_specs=[row, pl.BlockSpec((1, LANES), lambda i: (0, 0)), pl.BlockSpec((1, d), lambda i: (0, 0))],
        out_shape=[jax.ShapeDtypeStruct((t, d), F32), jax.ShapeDtypeStruct((1, LANES), F32), jax.ShapeDtypeStruct((1, d), F32)],
        compiler_params=_params("arbitrary"),
    )(x, gain, target)


def _ffn_bwd(dy, x, gain, g, u, wg, wu, wd, tag):
    t, d = x.shape
    f = wg.shape[1]
    tm = min(256, t)

    def body(dy_ref, x_ref, gain_ref, g_ref, u_ref, wg_ref, wu_ref, wd_ref, dx_ref, dg_ref, du_ref, a_ref, h_ref, dgain_ref):
        dyv = dy_ref[...]
        da = lax.dot_general(dyv.astype(BF16), wd_ref[...], NT, preferred_element_type=F32)
        gv, uv = g_ref[...].astype(F32), u_ref[...].astype(F32)
        sg = _sigmoid(gv)
        act = gv * sg
        a_ref[...] = (act * uv).astype(BF16)
        du = (da * act).astype(BF16)
        dg = (da * uv * (sg * (1.0 + gv * (1.0 - sg)))).astype(BF16)
        du_ref[...] = du
        dg_ref[...] = dg
        dh = (lax.dot_general(dg, wg_ref[...], NT, preferred_element_type=F32)
              + lax.dot_general(du, wu_ref[...], NT, preferred_element_type=F32))
        xv, gain_v = x_ref[...], gain_ref[...]
        dx, xhat = _rms_bwd(dh, xv, gain_v)
        dx_ref[...] = dyv + dx
        h_ref[...] = (xhat * gain_v).astype(BF16)
        _accumulate(dgain_ref, jnp.sum(dh * xhat, axis=0, keepdims=True), pl.program_id(0) == 0)

    row = pl.BlockSpec((tm, d), lambda i: (i, 0))
    wide = pl.BlockSpec((tm, f), lambda i: (i, 0))
    return pl.pallas_call(
        body, name=f"ffn_bwd_{tag}", grid=(t // tm,),
        in_specs=[row, row, _resident((1, d)), wide, wide, _resident(wg.shape), _resident(wu.shape), _resident(wd.shape)],
        out_specs=[row, wide, wide, wide, row, pl.BlockSpec((1, d), lambda i: (0, 0))],
        out_shape=[jax.ShapeDtypeStruct((t, d), F32), jax.ShapeDtypeStruct((t, f), BF16), jax.ShapeDtypeStruct((t, f), BF16),
                   jax.ShapeDtypeStruct((t, f), BF16), jax.ShapeDtypeStruct((t, d), BF16), jax.ShapeDtypeStruct((1, d), F32)],
        compiler_params=_params("arbitrary"),
    )(dy, x, gain, g, u, wg, wu, wd)


def _tn_matmul(a, bs, name):
    t, k = a.shape
    tk = k // 2 if (k // 2) % LANES == 0 else k
    tt = min(512, t)
    nb = len(bs)

    def body(a_ref, *refs):
        at = a_ref[...].astype(BF16)
        for b_ref, o_ref in zip(refs[:nb], refs[nb:]):
            _accumulate(o_ref, lax.dot_general(at, b_ref[...].astype(BF16), TN, preferred_element_type=F32),
                        pl.program_id(1) == 0)

    return pl.pallas_call(
        body, name=name, grid=(k // tk, t // tt),
        in_specs=[pl.BlockSpec((tt, tk), lambda i, j: (j, i))] + [pl.BlockSpec((tt, b.shape[1]), lambda i, j: (j, 0)) for b in bs],
        out_specs=[pl.BlockSpec((tk, b.shape[1]), lambda i, j: (i, 0)) for b in bs],
        out_shape=[jax.ShapeDtypeStruct((k, b.shape[1]), F32) for b in bs],
        compiler_params=_params("parallel", "arbitrary"),
    )(a, *bs)


def _attn_out_bwd(dx, w, o, dils, seq, tag, lse=None, sink=None):
    t, d = dx.shape
    ts = _tile_rows(seq)
    bl = t // seq
    ng = len(dils)
    has_sink = sink is not None
    expand = _head_expand().T

    def body(*refs):
        refs = list(refs)
        dx_ref, w_ref, o_ref, e_ref = refs[:4]
        refs = refs[4:]
        lse_ref, sink_ref = (refs.pop(0), refs.pop(0)) if has_sink else (None, None)
        do_refs, dl_refs = refs[:ng], refs[ng:2 * ng]
        refs = refs[2 * ng:]
        dsink_ref = refs.pop(0) if has_sink else None
        dof_ref, dlf_ref = refs
        do = lax.dot_general(dx_ref[...].astype(BF16), w_ref[...], NT, preferred_element_type=F32)
        prod = do * o_ref[...].astype(F32)
        hi = prod.astype(BF16)
        lo = (prod - hi.astype(F32)).astype(BF16)
        e = e_ref[...]
        dl = jnp.dot(hi, e, preferred_element_type=F32) + jnp.dot(lo, e, preferred_element_type=F32)
        for g in range(ng):
            for r, part in enumerate(_split_rows(do, dof_ref, dils[g])):
                do_refs[g][r] = part.astype(BF16)
            for r, part in enumerate(_split_rows(dl, dlf_ref, dils[g])):
                dl_refs[g][r] = part
        if has_sink:
            part = -jnp.exp(sink_ref[...] - lse_ref[...]) * dl
            _accumulate(dsink_ref, jnp.sum(part, axis=0, keepdims=True), pl.program_id(0) == 0)

    row = pl.BlockSpec((ts, d), lambda i: (i, 0))
    narrow = pl.BlockSpec((ts, LANES), lambda i: (i, 0))
    args = [dx, w, o, expand]
    in_specs = [row, _resident(w.shape), pl.BlockSpec((ts, Q_W), lambda i: (i, 0)), _resident(expand.shape)]
    if has_sink:
        args += [lse, jnp.pad(sink.reshape(1, N_HEADS), ((0, 0), (0, LANES - N_HEADS)))]
        in_specs += [narrow, _resident((1, LANES))]
    out_specs = [_res_spec(seq, dl, Q_W) for dl in dils] + [_res_spec(seq, dl, LANES) for dl in dils]
    out_shape = ([jax.ShapeDtypeStruct(_res_shape(bl, seq, dl, Q_W), BF16) for dl in dils]
                 + [jax.ShapeDtypeStruct(_res_shape(bl, seq, dl, LANES), F32) for dl in dils])
    if has_sink:
        out_specs.append(pl.BlockSpec((1, LANES), lambda i: (0, 0)))
        out_shape.append(jax.ShapeDtypeStruct((1, LANES), F32))
    outs = pl.pallas_call(
        body, name=f"attn_out_bwd_{tag}", grid=(t // ts,), in_specs=in_specs, out_specs=out_specs, out_shape=out_shape,
        scratch_shapes=[_stage(ts, Q_W), _stage(ts, LANES)],
        compiler_params=_params("arbitrary" if has_sink else "parallel"),
    )(*args)
    return list(outs[:ng]), list(outs[ng:2 * ng]), (outs[2 * ng] if has_sink else None)


def _attn_bwd(qkv, do, lse, delta, cos, sin, w, tag):
    shape = qkv.shape
    dil = shape[1]
    rows_all = _seq_view(qkv)
    nseq, length, _ = rows_all.shape
    bq = min(QUERY_BLOCK, length)
    wk = min(3 * bq, length)
    nb = length // bq

    def body(qkv_ref, do_ref, lse_ref, dl_ref, cos_ref, sin_ref, dp_ref, kk_ref, vv_ref, dk_ref, dv_ref):
        _pair_variants(qkv_ref, Q_W, kk_ref)
        _pair_variants(qkv_ref, Q_W + KV_W, vv_ref)
        dk_ref[...] = jnp.zeros_like(dk_ref)
        dv_ref[...] = jnp.zeros_like(dv_ref)
        band = _band(bq, wk)
        lo_k = lax.broadcasted_iota(jnp.int32, (wk, LANES), 1) < HEAD_DIM

        def block(i, carry):
            q0, k0 = _window(i, bq, wk, length)
            valid = jnp.abs(band + (q0 - k0)) <= w
            rows, krows = pl.ds(q0, bq), pl.ds(k0, wk)
            c, sn = cos_ref[rows, :], -sin_ref[rows, :]
            lse_t, dl_t = lse_ref[rows, :], dl_ref[rows, :]
            for kv in range(N_KV):
                heads = [(kv * GRP + h, h % 2) for h in range(GRP)]
                cols = [slice((kv * 2 + j) * LANES, (kv * 2 + j + 1) * LANES) for j in range(GRP // 2)]
                qp = [qkv_ref[rows, cs] for cs in cols]
                dop = [do_ref[rows, cs] for cs in cols]
                sc = [lax.dot_general(qp[h // 2], kk_ref[kv, half, krows, :], NT, preferred_element_type=F32)
                      for h, (_, half) in enumerate(heads)]
                dp = [lax.dot_general(dop[h // 2], vv_ref[kv, half, krows, :], NT, preferred_element_type=F32)
                      for h, (_, half) in enumerate(heads)]
                p = [jnp.exp(jnp.where(valid, s_, NEG_INF) - _over_keys(lse_t[:, hd:hd + 1], wk))
                     for s_, (hd, _) in zip(sc, heads)]
                ds = [(p_ * (dp_ - _over_keys(dl_t[:, hd:hd + 1], wk))).astype(BF16) for p_, dp_, (hd, _) in zip(p, dp, heads)]
                pb = [p_.astype(BF16) for p_ in p]
                for j in range(GRP // 2):
                    dq = (jnp.dot(ds[2 * j], kk_ref[kv, 0, krows, :], preferred_element_type=F32)
                          + jnp.dot(ds[2 * j + 1], kk_ref[kv, 1, krows, :], preferred_element_type=F32)) * SCALE
                    dp_ref[rows, cols[j]] = _rope(dq, c, sn).astype(BF16)
                dk = [lax.dot_general(ds[h], qp[h // 2], TN, preferred_element_type=F32) for h in range(GRP)]
                dv = [lax.dot_general(pb[h], dop[h // 2], TN, preferred_element_type=F32) for h in range(GRP)]
                dk_ref[kv, krows, :] += jnp.where(lo_k, dk[0] + dk[2], dk[1] + dk[3])
                dv_ref[kv, krows, :] += jnp.where(lo_k, dv[0] + dv[2], dv[1] + dv[3])
            return carry

        lax.fori_loop(0, nb, block, 0)
        lo = lax.broadcasted_iota(jnp.int32, (length, LANES), 1) < HEAD_DIM
        c, sn = cos_ref[...], -sin_ref[...]
        for ch in range(KV_W // LANES):
            halves = []
            for acc_ref in (dk_ref, dv_ref):
                even, odd = acc_ref[2 * ch], acc_ref[2 * ch + 1]
                even = even + pltpu.roll(even, HEAD_DIM, 1)
                odd = odd + pltpu.roll(odd, HEAD_DIM, 1)
                halves.append(jnp.where(lo, even, odd))
            dp_ref[:, Q_W + ch * LANES:Q_W + (ch + 1) * LANES] = _rope(halves[0], c, sn).astype(BF16)
            dp_ref[:, Q_W + KV_W + ch * LANES:Q_W + KV_W + (ch + 1) * LANES] = halves[1].astype(BF16)

    def seq_block(c):
        return pl.BlockSpec((None, length, c), lambda i: (i, 0, 0), pipeline_mode=pl.Buffered(1))

    table = pl.BlockSpec((None, length, LANES), lambda i: (i % dil, 0, 0), pipeline_mode=pl.Buffered(1))
    out = pl.pallas_call(
        body, name=f"attn_bwd_{tag}", grid=(nseq,),
        in_specs=[seq_block(QKV_W), seq_block(Q_W), seq_block(LANES), seq_block(LANES), table, table],
        out_specs=pl.BlockSpec((None, length, QKV_W), lambda i: (i, 0, 0)),
        out_shape=jax.ShapeDtypeStruct((nseq, length, QKV_W), BF16),
        scratch_shapes=[pltpu.VMEM((N_KV, 2, length, LANES), BF16), pltpu.VMEM((N_KV, 2, length, LANES), BF16),
                        pltpu.VMEM((N_KV, length, LANES), F32), pltpu.VMEM((N_KV, length, LANES), F32)],
        compiler_params=_params("parallel"),
    )(rows_all, _seq_view(do), _seq_view(lse), _seq_view(delta), cos, sin)
    return out.reshape(shape)


def _qkv_bwd(dy, x, gain, w, dps, dils, seq, tag):
    t, d = x.shape
    ts = _tile_rows(seq)
    bl = t // seq
    ng = len(dps)

    def body(dy_ref, x_ref, gain_ref, w_ref, *refs):
        dp_refs, dx_ref = refs[:ng], refs[ng]
        h_refs = refs[ng + 1:2 * ng + 1]
        dgain_ref, stage_ref = refs[2 * ng + 1:]
        dh = None
        for gi in range(ng):
            dil = dils[gi]
            n = ts // dil
            dp = dp_refs[gi][0] if dil == 1 else jnp.concatenate([dp_refs[gi][r] for r in range(dil)], axis=0)
            part = lax.dot_general(dp, w_ref[:, gi * QKV_W:(gi + 1) * QKV_W], NT, preferred_element_type=F32)
            part = _merge_rows([part[r * n:(r + 1) * n] for r in range(dil)], stage_ref, dil)
            dh = part if dh is None else dh + part
        xv, gain_v = x_ref[...], gain_ref[...]
        dx, xhat = _rms_bwd(dh, xv, gain_v)
        dx_ref[...] = dy_ref[...] + dx
        h = xhat * gain_v
        for gi in range(ng):
            for r, part in enumerate(_split_rows(h, stage_ref, dils[gi])):
                h_refs[gi][r] = part.astype(BF16)
        _accumulate(dgain_ref, jnp.sum(dh * xhat, axis=0, keepdims=True), pl.program_id(0) == 0)

    row = pl.BlockSpec((ts, d), lambda i: (i, 0))
    outs = pl.pallas_call(
        body, name=f"qkv_bwd_{tag}", grid=(t // ts,),
        in_specs=[row, row, _resident((1, d)), _resident(w.shape)] + [_res_spec(seq, dl, QKV_W) for dl in dils],
        out_specs=[row] + [_res_spec(seq, dl, d) for dl in dils] + [pl.BlockSpec((1, d), lambda i: (0, 0))],
        out_shape=[jax.ShapeDtypeStruct((t, d), F32)] + [jax.ShapeDtypeStruct(_res_shape(bl, seq, dl, d), BF16) for dl in dils]
                  + [jax.ShapeDtypeStruct((1, d), F32)],
        scratch_shapes=[_stage(ts, d)],
        compiler_params=_params("arbitrary"),
    )(dy, x, gain, w, *dps)
    return outs[0], list(outs[1:1 + ng]), outs[1 + ng]


ANY = pl.BlockSpec(memory_space=pl.ANY)


def _place():
    x, y, c = lax.axis_index("x"), lax.axis_index("y"), lax.axis_index("c")
    return x, y, c


def _chip_xy(chip):
    return chip // 2, chip % 2


def _all_gather(shards):
    n = len(shards)

    def body(*refs):
        ins, outs = refs[:n], refs[n:2 * n]
        send_sems, recv_sems, local_sems = refs[2 * n:]
        x, y, c = _place()
        sibling = (x, y, 1 - c)
        chips = [(1 - x, y), (x, 1 - y), (1 - x, 1 - y)]

        def copy(a, k, block, to, src=None):
            px, py, pc = block
            rows = outs[a].at[4 * px + 2 * py + pc]
            return pltpu.make_async_remote_copy(
                src_ref=rows if src is None else src, dst_ref=rows, send_sem=send_sems.at[a, k], recv_sem=recv_sems.at[a, k],
                device_id=to, device_id_type=MESH)

        sent = []
        for a in range(n):
            mine = pltpu.make_async_copy(ins[a], outs[a].at[4 * x + 2 * y + c], local_sems.at[a])
            mine.start()
            sent.append(mine)
        for a in range(n):
            first = [copy(a, 0, (x, y, c), sibling, src=ins[a])]
            first += [copy(a, 1 + j, (x, y, c), (*chip, c), src=ins[a]) for j, chip in enumerate(chips)]
            for cp in first:
                cp.start()
            sent += first
        for a in range(n):
            for j, chip in enumerate(chips):
                copy(a, 1 + j, (*chip, c), (x, y, c)).wait_recv()
                passed = copy(a, 4 + j, (*chip, c), sibling)
                passed.start()
                sent.append(passed)
        for a in range(n):
            copy(a, 0, sibling, (x, y, c)).wait_recv()
            for j, chip in enumerate(chips):
                copy(a, 4 + j, (*chip, 1 - c), (x, y, c)).wait_recv()
        for cp in sent[n:]:
            cp.wait_send()
        for mine in sent[:n]:
            mine.wait()

    return pl.pallas_call(
        body, name="all_gather_weights", in_specs=[ANY] * n, out_specs=[ANY] * n,
        out_shape=[jax.ShapeDtypeStruct((N_DEV,) + s.shape, s.dtype) for s in shards],
        scratch_shapes=[pltpu.SemaphoreType.DMA((n, 7)), pltpu.SemaphoreType.DMA((n, 7)), pltpu.SemaphoreType.DMA((n,))],
    )(*shards)


def _exchange_siblings(parts):
    n = len(parts)

    def body(*refs):
        ins, outs = refs[:n], refs[n:2 * n]
        send_sems, recv_sems = refs[2 * n:]
        x, y, c = _place()
        chip = 2 * x + y
        copies = []
        for a in range(n):
            for rel in range(N_CHIP):
                copies.append(pltpu.make_async_remote_copy(
                    src_ref=ins[a].at[2 * (chip ^ rel) + (1 - c)], dst_ref=outs[a].at[rel],
                    send_sem=send_sems.at[a, rel], recv_sem=recv_sems.at[a, rel], device_id=(x, y, 1 - c), device_id_type=MESH))
        for cp in copies:
            cp.start()
        for cp in copies:
            cp.wait()

    return pl.pallas_call(
        body, name="reduce_scatter_d2d", in_specs=[ANY] * n, out_specs=[ANY] * n,
        out_shape=[jax.ShapeDtypeStruct((N_CHIP,) + p.shape[1:], p.dtype) for p in parts],
        scratch_shapes=[pltpu.SemaphoreType.DMA((n, N_CHIP)), pltpu.SemaphoreType.DMA((n, N_CHIP))],
    )(*parts)


def _pair_sum(part, recv, own_blocks, tag):
    _, r, c = part.shape
    tr = r // 2 if r % 16 == 0 else r

    def body(ids_ref, p_ref, r_ref, o_ref):
        o_ref[...] = (p_ref[...] + r_ref[...]).astype(BF16)

    return pl.pallas_call(
        body, name=f"pair_sum_{tag}",
        grid_spec=pltpu.PrefetchScalarGridSpec(
            num_scalar_prefetch=1, grid=(N_CHIP, r // tr),
            in_specs=[pl.BlockSpec((None, tr, c), lambda rel, i, ids: (ids[rel], i, 0)),
                      pl.BlockSpec((None, tr, c), lambda rel, i, ids: (rel, i, 0))],
            out_specs=pl.BlockSpec((None, tr, c), lambda rel, i, ids: (rel, i, 0))),
        out_shape=jax.ShapeDtypeStruct((N_CHIP, r, c), BF16),
        compiler_params=_params("parallel", "parallel"),
    )(own_blocks, part, recv)


def _exchange_chips(sums):
    n = len(sums)

    def body(*refs):
        ins, outs = refs[:n], refs[n:2 * n]
        send_sems, recv_sems = refs[2 * n:]
        x, y, c = _place()
        chip = 2 * x + y
        copies = []
        for a in range(n):
            for rel in range(1, N_CHIP):
                copies.append(pltpu.make_async_remote_copy(
                    src_ref=ins[a].at[rel], dst_ref=outs[a].at[rel - 1], send_sem=send_sems.at[a, rel - 1],
                    recv_sem=recv_sems.at[a, rel - 1], device_id=(*_chip_xy(chip ^ rel), c), device_id_type=MESH))
        for cp in copies:
            cp.start()
        for cp in copies:
            cp.wait()

    return pl.pallas_call(
        body, name="reduce_scatter_ici", in_specs=[ANY] * n, out_specs=[ANY] * n,
        out_shape=[jax.ShapeDtypeStruct((N_CHIP - 1,) + s.shape[1:], s.dtype) for s in sums],
        scratch_shapes=[pltpu.SemaphoreType.DMA((n, N_CHIP - 1)), pltpu.SemaphoreType.DMA((n, N_CHIP - 1))],
    )(*sums)


def _all_reduce_small(v):
    def body(v_ref, o_ref, recv_ref, send_sems, recv_sems):
        x, y, c = _place()
        me = 4 * x + 2 * y + c
        copies = []
        for k in range(1, N_DEV):
            peer = me ^ k
            copies.append(pltpu.make_async_remote_copy(
                src_ref=v_ref, dst_ref=recv_ref.at[k], send_sem=send_sems.at[k - 1], recv_sem=recv_sems.at[k - 1],
                device_id=(peer // 4, (peer // 2) % 2, peer % 2), device_id_type=MESH))
        for cp in copies:
            cp.start()
        recv_ref[0] = v_ref[...]
        for cp in copies:
            cp.wait()
        acc = recv_ref[me]
        for src in range(1, N_DEV):
            acc = acc + recv_ref[me ^ src]
        o_ref[...] = acc

    vm = pl.BlockSpec(memory_space=pltpu.VMEM)
    return pl.pallas_call(
        body, name="all_reduce_small", in_specs=[vm], out_specs=vm, out_shape=jax.ShapeDtypeStruct(v.shape, F32),
        scratch_shapes=[pltpu.VMEM((N_DEV,) + v.shape, F32), pltpu.SemaphoreType.DMA((N_DEV - 1,)),
                        pltpu.SemaphoreType.DMA((N_DEV - 1,))],
    )(v)


def _adamw_math(w, g, m, v):
    m = ADAM_B1 * m + (1.0 - ADAM_B1) * g
    v = ADAM_B2 * v + (1.0 - ADAM_B2) * (g * g)
    m_hat = m / (1.0 - ADAM_B1 ** ADAM_STEP)
    v_hat = v / (1.0 - ADAM_B2 ** ADAM_STEP)
    delta = -ADAM_LR * (m_hat / (jnp.sqrt(v_hat) + ADAM_EPS) + ADAM_WD * w)
    return delta, m, v


def _adamw(parts, w, m, v, name):
    r, c = w.shape
    tr = r // 2 if r % 16 == 0 and r >= 256 else r
    n = len(parts)

    def body(*refs):
        w_ref, m_ref, v_ref, g_ref, d_ref, nm_ref, nv_ref = refs[n:]
        g = refs[0][...].astype(F32)
        for p_ref in refs[1:n]:
            g = g + p_ref[...].astype(F32)
        g_ref[...] = g
        d_ref[...], nm_ref[...], nv_ref[...] = _adamw_math(w_ref[...], g, m_ref[...], v_ref[...])

    tile = pl.BlockSpec((tr, c), lambda i: (i, 0))
    arrays, in_specs = [], []
    for p in parts:
        if isinstance(p, tuple):
            arrays.append(p[0])
            in_specs.append(pl.BlockSpec((None, tr, c), functools.partial(lambda i, slot: (slot, i, 0), slot=p[1])))
        else:
            arrays.append(p)
            in_specs.append(tile)
    return pl.pallas_call(
        body, name=name, grid=(r // tr,), in_specs=in_specs + [tile] * 3, out_specs=[tile] * 4,
        out_shape=[jax.ShapeDtypeStruct((r, c), F32)] * 4, compiler_params=_params("parallel"),
    )(*arrays, w, m, v)


def _columns(g):
    return g.transpose(1, 0, 2).reshape(g.shape[1], -1)


def _rows(g):
    return g.reshape(-1, g.shape[-1])


def _column_blocks(dw):
    k, n = dw.shape
    return dw.reshape(k, N_DEV, n // N_DEV).transpose(1, 0, 2)


def _row_blocks(dw):
    k, n = dw.shape
    return dw.reshape(N_DEV, k // N_DEV, n)


def _pack_rows(rows, width):
    out = None
    for i, r in enumerate(rows):
        r = r.reshape(1, -1).astype(F32)
        r = jnp.pad(r, ((i, 8 - 1 - i), (0, width - r.shape[1])))
        out = r if out is None else out + r
    return out


def _mixer_fwd(x, gain, w_in, w_out, cos, sin, seq, groups, tag, sink=None):
    qkvs, os, lses = [], [], []
    for gi, (dil, w) in enumerate(groups):
        qkv = _qkv_proj(x, gain, w_in, _tables_tiled(cos, seq, dil), _tables_tiled(sin, seq, dil), seq, dil, gi, f"{tag}{gi}")
        o, lse = _attn_fwd(qkv, w, f"{tag}{gi}", sink=sink)
        qkvs.append(qkv)
        os.append(o)
        lses.append(lse)
    y, o, lses = _out_proj(x, os, lses, [dl for dl, _ in groups], w_out, seq, tag)
    return y, (qkvs, o, lses)


def _mixer_bwd(dy, x_in, gain, w_in, w_out, saved, cos, sin, seq, groups, tag, sink=None):
    qkvs, o, lses = saved
    t, d = x_in.shape
    dils = [dl for dl, _ in groups]
    lse_tokens = lses[0].reshape(t, LANES) if sink is not None else None
    dos, dls, dsink = _attn_out_bwd(dy, w_out, o, dils, seq, tag, lse=lse_tokens, sink=sink)
    (dw_out,) = _tn_matmul(o, [dy], f"dw_out_{tag}")
    dps = [_attn_bwd(qkvs[gi], dos[gi], lses[gi], dls[gi], _tables_by_residue(cos, seq, dil), _tables_by_residue(sin, seq, dil),
                     w, f"{tag}{gi}") for gi, (dil, w) in enumerate(groups)]
    dx, hs, dgain = _qkv_bwd(dy, x_in, gain, w_in, dps, dils, seq, tag)
    dw_in = [_tn_matmul(hs[gi].reshape(t, d), [dps[gi].reshape(t, QKV_W)], f"dw_in_{tag}{gi}")[0] for gi in range(len(groups))]
    dw_in = dw_in[0] if len(dw_in) == 1 else jnp.concatenate(dw_in, axis=1)
    return dx, dw_in, dw_out, dgain, dsink


def _ffn_layer_bwd(dy, x_in, gain, g, u, wg, wu, wd, tag):
    dx, dg, du, act, h, dgain = _ffn_bwd(dy, x_in, gain, g, u, wg, wu, wd, tag)
    (dwd,) = _tn_matmul(act, [dy], f"dw_down_{tag}")
    dwg, dwu = _tn_matmul(h, [dg, du], f"dw_gate_up_{tag}")
    return dx, dwg, dwu, dwd, dgain


def kernel(x, a_w_in, a_sink, a_w_out, b_w_in, b_w_out, norm_mix, norm_ffn, w_gate, w_up, w_down, final_norm, loss_target, m_a_w_in, m_a_sink, m_a_w_out, m_b_w_in, m_b_w_out, m_norm_mix, m_norm_ffn, m_w_gate, m_w_up, m_w_down, m_final_norm, v_a_w_in, v_a_sink, v_a_w_out, v_b_w_in, v_b_w_out, v_norm_mix, v_norm_ffn, v_w_gate, v_w_up, v_w_down, v_final_norm):
    bl, seq, d = x.shape
    t = bl * seq
    xf = x.reshape(t, d)
    target = loss_target.reshape(t, d)
    cos, sin = _rope_tables(seq)
    groups_a = [(1, ATTN_HALF_WINDOW)]
    groups_b = [(dil, window // 2 // dil) for window, dil in DILATED_GROUPS]

    col_shards = [a_w_in[0], b_w_in[0], w_gate[0], w_gate[1], w_up[0], w_up[1]]
    row_shards = [a_w_out[0], b_w_out[0], w_down[0], w_down[1]]
    gathered = _all_gather([s.astype(BF16) for s in col_shards + row_shards])
    wa_in, wb_in, wg0, wg1, wu0, wu1 = [_columns(g) for g in gathered[:6]]
    wa_out, wb_out, wd0, wd1 = [_rows(g) for g in gathered[6:]]

    x1_0, saved_a = _mixer_fwd(xf, norm_mix[0:1], wa_in, wa_out, cos, sin, seq, groups_a, "a", sink=a_sink[0])
    x2_0, g0, u0 = _ffn_fwd(x1_0, norm_ffn[0:1], wg0, wu0, wd0, "0")
    x1_1, saved_b = _mixer_fwd(x2_0, norm_mix[1:2], wb_in, wb_out, cos, sin, seq, groups_b, "b")
    x2_1, g1, u1 = _ffn_fwd(x1_1, norm_ffn[1:2], wg1, wu1, wd1, "1")

    dy, loss_part, d_final = _loss_bwd(x2_1, final_norm.reshape(1, d), target)
    dy, dwg1, dwu1, dwd1, d_nf1 = _ffn_layer_bwd(dy, x1_1, norm_ffn[1:2], g1, u1, wg1, wu1, wd1, "1")
    dy, dwb_in, dwb_out, d_nm1, _ = _mixer_bwd(dy, x2_0, norm_mix[1:2], wb_in, wb_out, saved_b, cos, sin, seq, groups_b, "b")
    dy, dwg0, dwu0, dwd0, d_nf0 = _ffn_layer_bwd(dy, x1_0, norm_ffn[0:1], g0, u0, wg0, wu0, wd0, "0")
    dy, dwa_in, dwa_out, d_nm0, d_sink = _mixer_bwd(dy, xf, norm_mix[0:1], wa_in, wa_out, saved_a, cos, sin, seq, groups_a, "a",
                                                    sink=a_sink[0])
    grad_x = dy.reshape(bl, seq, d)

    parts = [_column_blocks(g) for g in (dwa_in, dwb_in, dwg0, dwg1, dwu0, dwu1)]
    parts += [_row_blocks(g) for g in (dwa_out, dwb_out, dwd0, dwd1)]
    cx, cy, cc = lax.axis_index("x"), lax.axis_index("y"), lax.axis_index("c")
    own_blocks = jnp.stack([2 * ((2 * cx + cy) ^ rel) + cc for rel in range(N_CHIP)]).astype(jnp.int32)
    from_sibling = _exchange_siblings(parts)
    sums = [_pair_sum(p, r, own_blocks, str(a)) for a, (p, r) in enumerate(zip(parts, from_sibling))]
    from_chips = _exchange_chips(sums)

    def shard2d(a):
        return a.reshape(-1, a.shape[-1])

    big = [(a_w_in, m_a_w_in, v_a_w_in), (b_w_in, m_b_w_in, v_b_w_in),
           (w_gate[0], m_w_gate[0], v_w_gate[0]), (w_gate[1], m_w_gate[1], v_w_gate[1]),
           (w_up[0], m_w_up[0], v_w_up[0]), (w_up[1], m_w_up[1], v_w_up[1]),
           (a_w_out, m_a_w_out, v_a_w_out), (b_w_out, m_b_w_out, v_b_w_out),
           (w_down[0], m_w_down[0], v_w_down[0]), (w_down[1], m_w_down[1], v_w_down[1])]
    upd = []
    for a, (w_, m_, v_) in enumerate(big):
        grad_parts = [(sums[a], 0)] + [(from_chips[a], rel) for rel in range(N_CHIP - 1)]
        upd.append(_adamw(grad_parts, shard2d(w_), shard2d(m_), shard2d(v_), f"adamw_{a}"))
    (u_a_in, u_b_in, u_g0, u_g1, u_u0, u_u1, u_a_out, u_b_out, u_d0, u_d1) = upd

    small = _pack_rows([d_nm0, d_nm1, d_nf0, d_nf1, d_final, d_sink, loss_part], d)
    total = _all_reduce_small(small)
    small_w = _pack_rows([norm_mix[0], norm_mix[1], norm_ffn[0], norm_ffn[1], final_norm, a_sink], d)
    small_m = _pack_rows([m_norm_mix[0], m_norm_mix[1], m_norm_ffn[0], m_norm_ffn[1], m_final_norm, m_a_sink], d)
    small_v = _pack_rows([v_norm_mix[0], v_norm_mix[1], v_norm_ffn[0], v_norm_ffn[1], v_final_norm, v_a_sink], d)
    u_small = _adamw([total], small_w, small_m, small_v, "adamw_small")
    loss = total[6, 0]

    outs = []
    for k in range(4):
        sm = u_small[k]
        outs += [
            u_a_in[k].reshape(a_w_in.shape), sm[5:6, :N_HEADS], u_a_out[k].reshape(a_w_out.shape),
            u_b_in[k].reshape(b_w_in.shape), u_b_out[k].reshape(b_w_out.shape), sm[0:2], sm[2:4],
            jnp.stack([u_g0[k], u_g1[k]]), jnp.stack([u_u0[k], u_u1[k]]), jnp.stack([u_d0[k], u_d1[k]]), sm[4],
        ]
    return (loss, grad_x, *outs)
```

```python
import functools
import math

import jax
import jax.numpy as jnp
from jax import lax
from jax.experimental import pallas as pl
from jax.experimental.pallas import tpu as pltpu

F32 = jnp.float32
BF16 = jnp.bfloat16

HEAD_DIM = 64
N_HEADS = 16
N_KV = 4
GRP = N_HEADS // N_KV
Q_W = N_HEADS * HEAD_DIM
KV_W = N_KV * HEAD_DIM
QKV_W = Q_W + 2 * KV_W
ATTN_HALF_WINDOW = 128
DILATED_GROUPS = ((128, 1), (512, 4), (2048, 16))
ROPE_THETA = 10000.0
RMS_EPS = 1e-6
NEG_INF = -1e30
SCALE = 1.0 / math.sqrt(HEAD_DIM)

ADAM_LR = 0.001
ADAM_B1 = 0.9
ADAM_B2 = 0.999
ADAM_EPS = 1e-08
ADAM_WD = 0.01
ADAM_STEP = 10

LANES = 128
VMEM_LIMIT = 56 * 1024 * 1024
QUERY_BLOCK = 128
N_DEV = 8
MESH = pl.DeviceIdType.MESH

NT = (((1,), (1,)), ((), ()))
TN = (((0,), (0,)), ((), ()))


def _params(*sem):
    return pltpu.CompilerParams(dimension_semantics=tuple(sem) if sem else None, vmem_limit_bytes=VMEM_LIMIT)


def _resident(shape):
    return pl.BlockSpec(shape, lambda *_: (0,) * len(shape), pipeline_mode=pl.Buffered(1))


def _rope_tables(seq):
    inv_freq = 1.0 / (ROPE_THETA ** (jnp.arange(0, HEAD_DIM, 2, dtype=F32) / HEAD_DIM))
    ang = jnp.arange(seq, dtype=F32)[:, None] * inv_freq[None, :]
    cos, sin = jnp.cos(ang), jnp.sin(ang)
    return jnp.tile(cos, (1, 4)), jnp.concatenate([-sin, sin, -sin, sin], axis=1)


def _rope(t, cos, sin_signed):
    lane = lax.broadcasted_iota(jnp.int32, t.shape, 1)
    first = (lane & (HEAD_DIM // 2)) == 0
    swapped = jnp.where(first, pltpu.roll(t, LANES - HEAD_DIM // 2, 1), pltpu.roll(t, HEAD_DIM // 2, 1))
    return t * cos + swapped * sin_signed


def _rms(x):
    return lax.rsqrt(jnp.mean(x * x, axis=-1, keepdims=True) + RMS_EPS)


def _rms_bwd(dh, x, gain):
    r = _rms(x)
    xhat = x * r
    dxh = dh * gain
    dx = r * (dxh - xhat * jnp.mean(dxh * xhat, axis=-1, keepdims=True))
    return dx, xhat


def _accumulate(ref, value, first):
    @pl.when(first)
    def _():
        ref[...] = jnp.zeros_like(ref)

    ref[...] += value


def _tile_rows(seq):
    return min(512, seq)


def _res_shape(bl, seq, dil, c):
    ts = _tile_rows(seq)
    return (bl, dil, seq // ts, ts // dil, c)


def _res_spec(seq, dil, c):
    ts = _tile_rows(seq)
    per_seq = seq // ts
    return pl.BlockSpec((None, dil, None, ts // dil, c), lambda i: (i // per_seq, 0, i % per_seq, 0, 0))


def _seq_view(a):
    bl, dil, tiles, n, c = a.shape
    return a.reshape(bl * dil, tiles * n, c)


def _stage(ts, c):
    return pltpu.VMEM((c // LANES, ts, LANES), F32)


def _split_rows(val, stage_ref, dil):
    if dil == 1:
        return [val]
    ts, c = val.shape
    n, nc = ts // dil, c // LANES
    for k in range(nc):
        stage_ref[k] = val[:, k * LANES:(k + 1) * LANES]
    return [jnp.concatenate([stage_ref[k, pl.ds(r, n, stride=dil), :] for k in range(nc)], axis=1) for r in range(dil)]


def _merge_rows(parts, stage_ref, dil):
    if dil == 1:
        return parts[0]
    n, c = parts[0].shape
    nc = c // LANES
    for r, part in enumerate(parts):
        for k in range(nc):
            stage_ref[k, pl.ds(r, n, stride=dil), :] = part[:, k * LANES:(k + 1) * LANES]
    return jnp.concatenate([stage_ref[k] for k in range(nc)], axis=1)


def _tables_tiled(table, seq, dil):
    ts = _tile_rows(seq)
    return table.reshape(seq // ts, ts // dil, dil, LANES).transpose(0, 2, 1, 3).reshape(seq, LANES)


def _tables_by_residue(table, seq, dil):
    return table.reshape(seq // dil, dil, LANES).transpose(1, 0, 2)


def _qkv_proj(x, gain, w, cos, sin, seq, dil, group, tag):
    t, d = x.shape
    ts = _tile_rows(seq)
    n = ts // dil
    per_seq = seq // ts

    def body(x_ref, g_ref, w_ref, cos_ref, sin_ref, o_ref, stage_ref):
        xv = jnp.concatenate(_split_rows(x_ref[...], stage_ref, dil), axis=0)
        h = (xv * _rms(xv) * g_ref[...]).astype(BF16)
        acc = jnp.dot(h, w_ref[...], preferred_element_type=F32)
        c, s = cos_ref[...], sin_ref[...]
        for j in range(QKV_W // LANES):
            cols = slice(j * LANES, (j + 1) * LANES)
            val = acc[:, cols]
            if j < (Q_W + KV_W) // LANES:
                val = _rope(val, c, s)
            if j < Q_W // LANES:
                val = val * SCALE
            val = val.astype(BF16)
            for r in range(dil):
                o_ref[r, :, cols] = val[r * n:(r + 1) * n]

    table = pl.BlockSpec((ts, LANES), lambda i: (i % per_seq, 0))
    return pl.pallas_call(
        body, name=f"qkv_proj_{tag}", grid=(t // ts,),
        in_specs=[pl.BlockSpec((ts, d), lambda i: (i, 0)), pl.BlockSpec((1, d), lambda i: (0, 0)),
                  pl.BlockSpec((d, QKV_W), lambda i: (0, group)), table, table],
        out_specs=_res_spec(seq, dil, QKV_W),
        out_shape=jax.ShapeDtypeStruct(_res_shape(t // seq, seq, dil, QKV_W), BF16),
        scratch_shapes=[_stage(ts, d)],
        compiler_params=_params("parallel"),
    )(x, gain, w, cos, sin)


def _band(bq, wk):
    return lax.broadcasted_iota(jnp.int32, (bq, wk), 0) - lax.broadcasted_iota(jnp.int32, (bq, wk), 1)


def _pair_variants(src_ref, base, dst_ref):
    lo = lax.broadcasted_iota(jnp.int32, (src_ref.shape[0], LANES), 1) < HEAD_DIM
    for c in range(KV_W // LANES):
        chunk = src_ref[:, base + c * LANES:base + (c + 1) * LANES]
        rolled = pltpu.roll(chunk, HEAD_DIM, 1)
        zero = jnp.zeros_like(chunk)
        dst_ref[2 * c, 0] = jnp.where(lo, chunk, zero)
        dst_ref[2 * c, 1] = jnp.where(lo, zero, rolled)
        dst_ref[2 * c + 1, 0] = jnp.where(lo, rolled, zero)
        dst_ref[2 * c + 1, 1] = jnp.where(lo, zero, chunk)


def _over_keys(col, wk):
    if wk % LANES:
        return jnp.broadcast_to(col, (col.shape[0], wk))
    wide = jnp.broadcast_to(col, (col.shape[0], LANES))
    return wide if wk == LANES else jnp.concatenate([wide] * (wk // LANES), axis=1)


def _window(i, bq, wk, length):
    q0 = pl.multiple_of(i * bq, bq)
    k0 = pl.multiple_of(jnp.clip(q0 - bq, 0, length - wk), bq)
    return q0, k0


def _attn_fwd(qkv, w, tag, sink=None, exchange=()):
    shape = qkv.shape
    rows_all = _seq_view(qkv)
    nseq, length, _ = rows_all.shape
    bq = min(QUERY_BLOCK, length)
    wk = min(3 * bq, length)
    nb = length // bq
    has_sink = sink is not None

    def body(*refs):
        qkv_ref = refs[0]
        sink_ref = refs[1] if has_sink else None
        o_ref, lse_ref, kk_ref, vv_ref = refs[-4:]
        _pair_variants(qkv_ref, Q_W, kk_ref)
        _pair_variants(qkv_ref, Q_W + KV_W, vv_ref)
        band = _band(bq, wk)
        lane = lax.broadcasted_iota(jnp.int32, (bq, LANES), 1)
        lo = lane < HEAD_DIM

        def block(i, carry):
            q0, k0 = _window(i, bq, wk, length)
            valid = jnp.abs(band + (q0 - k0)) <= w
            rows, krows = pl.ds(q0, bq), pl.ds(k0, wk)
            lse_tile = jnp.zeros((bq, LANES), F32)
            for kv in range(N_KV):
                heads = [(kv * GRP + h, h % 2) for h in range(GRP)]
                qp = [qkv_ref[rows, (kv * 2 + j) * LANES:(kv * 2 + j + 1) * LANES] for j in range(GRP // 2)]
                sc = [lax.dot_general(qp[h // 2], kk_ref[kv, half, krows, :], NT, preferred_element_type=F32)
                      for h, (_, half) in enumerate(heads)]
                sc = [jnp.where(valid, s_, NEG_INF) for s_ in sc]
                m = [jnp.max(s_, axis=-1, keepdims=True) for s_ in sc]
                if has_sink:
                    m = [jnp.maximum(m_, sink_ref[hd]) for m_, (hd, _) in zip(m, heads)]
                mb = [jnp.broadcast_to(m_, (bq, LANES)) for m_ in m]
                p = [jnp.exp(s_ - _over_keys(m_, wk)) for s_, m_ in zip(sc, m)]
                den = [jnp.sum(p_, axis=-1, keepdims=True) for p_ in p]
                if has_sink:
                    den = [d_ + jnp.exp(sink_ref[hd] - m_) for d_, m_, (hd, _) in zip(den, m, heads)]
                inv = [jnp.broadcast_to(1.0 / d_, (bq, LANES)) for d_ in den]
                pb = [p_.astype(BF16) for p_ in p]
                for j in range(GRP // 2):
                    o = (jnp.dot(pb[2 * j], vv_ref[kv, 0, krows, :], preferred_element_type=F32)
                         + jnp.dot(pb[2 * j + 1], vv_ref[kv, 1, krows, :], preferred_element_type=F32))
                    o = o * jnp.where(lo, inv[2 * j], inv[2 * j + 1])
                    o_ref[rows, (kv * 2 + j) * LANES:(kv * 2 + j + 1) * LANES] = o.astype(BF16)
                for h, (hd, _) in enumerate(heads):
                    lse_tile = jnp.where(lane == hd, mb[h] - jnp.log(inv[h]), lse_tile)
            lse_ref[rows, :] = lse_tile
            return carry

        lax.fori_loop(0, nb, block, 0)

    args = [rows_all]
    in_specs = [pl.BlockSpec((None, length, QKV_W), lambda i: (i, 0, 0))]
    if has_sink:
        args.append(sink)
        in_specs.append(pl.BlockSpec(memory_space=pltpu.SMEM))
    (o, lse), exchanged = _hosted_call(
        body, exchange, name=f"attn_fwd_{tag}", grid=(nseq,), in_specs=in_specs,
        out_specs=[pl.BlockSpec((None, length, Q_W), lambda i: (i, 0, 0)), pl.BlockSpec((None, length, LANES), lambda i: (i, 0, 0))],
        out_shape=[jax.ShapeDtypeStruct((nseq, length, Q_W), BF16), jax.ShapeDtypeStruct((nseq, length, LANES), F32)],
        scratch_shapes=[pltpu.VMEM((N_KV, 2, length, LANES), BF16), pltpu.VMEM((N_KV, 2, length, LANES), BF16)],
        semantics=("parallel",), args=args)
    return o.reshape(shape[:-1] + (Q_W,)), lse.reshape(shape[:-1] + (LANES,)), exchanged


def _head_expand():
    return (jnp.arange(LANES)[:, None] == jnp.arange(Q_W)[None, :] // HEAD_DIM).astype(BF16)


def _out_proj(x, os, lses, dils, w, seq, tag):
    t, d = x.shape
    ts = _tile_rows(seq)
    ng = len(os)
    bl = t // seq
    if ng == 1:
        def body1(x_ref, o_ref, w_ref, y_ref):
            y_ref[...] = x_ref[...] + jnp.dot(o_ref[...], w_ref[...], preferred_element_type=F32)

        row = pl.BlockSpec((ts, d), lambda i: (i, 0))
        o = os[0].reshape(t, Q_W)
        y = pl.pallas_call(
            body1, name=f"out_proj_{tag}", grid=(t // ts,), in_specs=[row, row, _resident(w.shape)], out_specs=row,
            out_shape=jax.ShapeDtypeStruct((t, d), F32), compiler_params=_params("parallel"),
        )(x, o, w)
        return y, o, [lses[0]]

    def body(*refs):
        x_ref, w_ref, e_ref = refs[:3]
        o_refs, l_refs = refs[3:3 + ng], refs[3 + ng:3 + 2 * ng]
        y_ref, om_ref = refs[3 + 2 * ng:5 + 2 * ng]
        lt_refs = refs[5 + 2 * ng:5 + 3 * ng]
        wide_ref, narrow_ref = refs[5 + 3 * ng:]
        ls = [_merge_rows([l_refs[g][r] for r in range(dils[g])], narrow_ref, dils[g]) for g in range(ng)]
        mx = functools.reduce(jnp.maximum, ls)
        tot = mx + jnp.log(functools.reduce(lambda a, b: a + b, [jnp.exp(l_ - mx) for l_ in ls]))
        e = e_ref[...]
        o = None
        for g in range(ng):
            wt = jnp.exp(ls[g] - tot)
            hi = wt.astype(BF16)
            lo = (wt - hi.astype(F32)).astype(BF16)
            wide = jnp.dot(hi, e, preferred_element_type=F32) + jnp.dot(lo, e, preferred_element_type=F32)
            term = wide * _merge_rows([o_refs[g][r].astype(F32) for r in range(dils[g])], wide_ref, dils[g])
            o = term if o is None else o + term
        ob = o.astype(BF16)
        om_ref[...] = ob
        y_ref[...] = x_ref[...] + jnp.dot(ob, w_ref[...], preferred_element_type=F32)
        for g in range(ng):
            for r, part in enumerate(_split_rows(tot, narrow_ref, dils[g])):
                lt_refs[g][r] = part

    row = pl.BlockSpec((ts, d), lambda i: (i, 0))
    e = _head_expand()
    outs = pl.pallas_call(
        body, name=f"out_proj_{tag}", grid=(t // ts,),
        in_specs=[row, _resident(w.shape), _resident(e.shape)] + [_res_spec(seq, dl, Q_W) for dl in dils]
                 + [_res_spec(seq, dl, LANES) for dl in dils],
        out_specs=[row, pl.BlockSpec((ts, Q_W), lambda i: (i, 0))] + [_res_spec(seq, dl, LANES) for dl in dils],
        out_shape=[jax.ShapeDtypeStruct((t, d), F32), jax.ShapeDtypeStruct((t, Q_W), BF16)]
                  + [jax.ShapeDtypeStruct(_res_shape(bl, seq, dl, LANES), F32) for dl in dils],
        scratch_shapes=[_stage(ts, Q_W), _stage(ts, LANES)],
        compiler_params=_params("parallel"),
    )(x, w, e, *os, *lses)
    return outs[0], outs[1], list(outs[2:])


def _sigmoid(g):
    return 1.0 / (1.0 + jnp.exp(-g))


def _ffn_fwd(x, gain, wg, wu, wd, tag, exchange=()):
    t, d = x.shape
    f = wg.shape[1]
    tm = min(256, t)

    def body(x_ref, gain_ref, wg_ref, wu_ref, wd_ref, y_ref, g_ref, u_ref):
        xv = x_ref[...]
        h = (xv * _rms(xv) * gain_ref[...]).astype(BF16)
        g = jnp.dot(h, wg_ref[...], preferred_element_type=F32)
        u = jnp.dot(h, wu_ref[...], preferred_element_type=F32)
        g_ref[...] = g.astype(BF16)
        u_ref[...] = u.astype(BF16)
        a = (g * _sigmoid(g) * u).astype(BF16)
        y_ref[...] = xv + jnp.dot(a, wd_ref[...], preferred_element_type=F32)

    row = pl.BlockSpec((tm, d), lambda i: (i, 0))
    wide = pl.BlockSpec((tm, f), lambda i: (i, 0))
    outs, exchanged = _hosted_call(
        body, exchange, name=f"ffn_fwd_{tag}", grid=(t // tm,),
        in_specs=[row, _resident((1, d)), _resident(wg.shape), _resident(wu.shape), _resident(wd.shape)],
        out_specs=[row, wide, wide],
        out_shape=[jax.ShapeDtypeStruct((t, d), F32), jax.ShapeDtypeStruct((t, f), BF16), jax.ShapeDtypeStruct((t, f), BF16)],
        scratch_shapes=[], semantics=("parallel",), args=(x, gain, wg, wu, wd))
    return (*outs, exchanged)


def _loss_bwd(x, gain, target):
    t, d = x.shape
    tm = min(512, t)

    def body(x_ref, gain_ref, t_ref, dx_ref, loss_ref, dgain_ref):
        xv, gain_v = x_ref[...], gain_ref[...]
        xhat = xv * _rms(xv)
        err = xhat * gain_v - t_ref[...]
        dy = err * (1.0 / d)
        dx, _ = _rms_bwd(dy, xv, gain_v)
        dx_ref[...] = dx
        first = pl.program_id(0) == 0
        part = 0.5 * jnp.sum(jnp.mean(err * err, axis=-1, keepdims=True), axis=0, keepdims=True)
        _accumulate(loss_ref, jnp.broadcast_to(part, loss_ref.shape), first)
        _accumulate(dgain_ref, jnp.sum(dy * xhat, axis=0, keepdims=True), first)

    row = pl.BlockSpec((tm, d), lambda i: (i, 0))
    return pl.pallas_call(
        body, name="loss_bwd", grid=(t // tm,), in_specs=[row, _resident((1, d)), row],
        out_specs=[row, pl.BlockSpec((1, LANES), lambda i: (0, 0)), pl.BlockSpec((1, d), lambda i: (0, 0))],
        out_shape=[jax.ShapeDtypeStruct((t, d), F32), jax.ShapeDtypeStruct((1, LANES), F32), jax.ShapeDtypeStruct((1, d), F32)],
        compiler_params=_params("arbitrary"),
    )(x, gain, target)


def _ffn_bwd(dy, x, gain, g, u, wg, wu, wd, tag, exchange=()):
    t, d = x.shape
    f = wg.shape[1]
    tm = min(256, t)

    def body(dy_ref, x_ref, gain_ref, g_ref, u_ref, wg_ref, wu_ref, wd_ref, dx_ref, dg_ref, du_ref, a_ref, h_ref, dgain_ref):
        dyv = dy_ref[...]
        da = lax.dot_general(dyv.astype(BF16), wd_ref[...], NT, preferred_element_type=F32)
        gv, uv = g_ref[...].astype(F32), u_ref[...].astype(F32)
        sg = _sigmoid(gv)
        act = gv * sg
        a_ref[...] = (act * uv).astype(BF16)
        du = (da * act).astype(BF16)
        dg = (da * uv * (sg * (1.0 + gv * (1.0 - sg)))).astype(BF16)
        du_ref[...] = du
        dg_ref[...] = dg
        dh = (lax.dot_general(dg, wg_ref[...], NT, preferred_element_type=F32)
              + lax.dot_general(du, wu_ref[...], NT, preferred_element_type=F32))
        xv, gain_v = x_ref[...], gain_ref[...]
        dx, xhat = _rms_bwd(dh, xv, gain_v)
        dx_ref[...] = dyv + dx
        h_ref[...] = (xhat * gain_v).astype(BF16)
        _accumulate(dgain_ref, jnp.sum(dh * xhat, axis=0, keepdims=True), pl.program_id(0) == 0)

    row = pl.BlockSpec((tm, d), lambda i: (i, 0))
    wide = pl.BlockSpec((tm, f), lambda i: (i, 0))
    outs, exchanged = _hosted_call(
        body, exchange, name=f"ffn_bwd_{tag}", grid=(t // tm,),
        in_specs=[row, row, _resident((1, d)), wide, wide, _resident(wg.shape), _resident(wu.shape), _resident(wd.shape)],
        out_specs=[row, wide, wide, wide, row, pl.BlockSpec((1, d), lambda i: (0, 0))],
        out_shape=[jax.ShapeDtypeStruct((t, d), F32), jax.ShapeDtypeStruct((t, f), BF16), jax.ShapeDtypeStruct((t, f), BF16),
                   jax.ShapeDtypeStruct((t, f), BF16), jax.ShapeDtypeStruct((t, d), BF16), jax.ShapeDtypeStruct((1, d), F32)],
        scratch_shapes=[], semantics=("arbitrary",), args=(dy, x, gain, g, u, wg, wu, wd))
    return (*outs, exchanged)


def _tn_matmul(a, bs, name):
    t, k = a.shape
    tk = k // 2 if (k // 2) % LANES == 0 else k
    tt = min(512, t)
    nb = len(bs)

    def body(a_ref, *refs):
        at = a_ref[...].astype(BF16)
        for b_ref, o_ref in zip(refs[:nb], refs[nb:]):
            _accumulate(o_ref, lax.dot_general(at, b_ref[...].astype(BF16), TN, preferred_element_type=F32),
                        pl.program_id(1) == 0)

    return pl.pallas_call(
        body, name=name, grid=(k // tk, t // tt),
        in_specs=[pl.BlockSpec((tt, tk), lambda i, j: (j, i))] + [pl.BlockSpec((tt, b.shape[1]), lambda i, j: (j, 0)) for b in bs],
        out_specs=[pl.BlockSpec((tk, b.shape[1]), lambda i, j: (i, 0)) for b in bs],
        out_shape=[jax.ShapeDtypeStruct((k, b.shape[1]), F32) for b in bs],
        compiler_params=_params("parallel", "arbitrary"),
    )(a, *bs)


def _attn_out_bwd(dx, w, o, dils, seq, tag, lse=None, sink=None):
    t, d = dx.shape
    ts = _tile_rows(seq)
    bl = t // seq
    ng = len(dils)
    has_sink = sink is not None
    expand = _head_expand().T

    def body(*refs):
        refs = list(refs)
        dx_ref, w_ref, o_ref, e_ref = refs[:4]
        refs = refs[4:]
        lse_ref, sink_ref = (refs.pop(0), refs.pop(0)) if has_sink else (None, None)
        do_refs, dl_refs = refs[:ng], refs[ng:2 * ng]
        refs = refs[2 * ng:]
        dsink_ref = refs.pop(0) if has_sink else None
        dof_ref, dlf_ref = refs
        do = lax.dot_general(dx_ref[...].astype(BF16), w_ref[...], NT, preferred_element_type=F32)
        prod = do * o_ref[...].astype(F32)
        hi = prod.astype(BF16)
        lo = (prod - hi.astype(F32)).astype(BF16)
        e = e_ref[...]
        dl = jnp.dot(hi, e, preferred_element_type=F32) + jnp.dot(lo, e, preferred_element_type=F32)
        for g in range(ng):
            for r, part in enumerate(_split_rows(do, dof_ref, dils[g])):
                do_refs[g][r] = part.astype(BF16)
            for r, part in enumerate(_split_rows(dl, dlf_ref, dils[g])):
                dl_refs[g][r] = part
        if has_sink:
            part = -jnp.exp(sink_ref[...] - lse_ref[...]) * dl
            _accumulate(dsink_ref, jnp.sum(part, axis=0, keepdims=True), pl.program_id(0) == 0)

    row = pl.BlockSpec((ts, d), lambda i: (i, 0))
    narrow = pl.BlockSpec((ts, LANES), lambda i: (i, 0))
    args = [dx, w, o, expand]
    in_specs = [row, _resident(w.shape), pl.BlockSpec((ts, Q_W), lambda i: (i, 0)), _resident(expand.shape)]
    if has_sink:
        args += [lse, jnp.pad(sink.reshape(1, N_HEADS), ((0, 0), (0, LANES - N_HEADS)))]
        in_specs += [narrow, _resident((1, LANES))]
    out_specs = [_res_spec(seq, dl, Q_W) for dl in dils] + [_res_spec(seq, dl, LANES) for dl in dils]
    out_shape = ([jax.ShapeDtypeStruct(_res_shape(bl, seq, dl, Q_W), BF16) for dl in dils]
                 + [jax.ShapeDtypeStruct(_res_shape(bl, seq, dl, LANES), F32) for dl in dils])
    if has_sink:
        out_specs.append(pl.BlockSpec((1, LANES), lambda i: (0, 0)))
        out_shape.append(jax.ShapeDtypeStruct((1, LANES), F32))
    outs = pl.pallas_call(
        body, name=f"attn_out_bwd_{tag}", grid=(t // ts,), in_specs=in_specs, out_specs=out_specs, out_shape=out_shape,
        scratch_shapes=[_stage(ts, Q_W), _stage(ts, LANES)],
        compiler_params=_params("arbitrary" if has_sink else "parallel"),
    )(*args)
    return list(outs[:ng]), list(outs[ng:2 * ng]), (outs[2 * ng] if has_sink else None)


def _attn_bwd(qkv, do, lse, delta, cos, sin, w, tag, exchange=()):
    shape = qkv.shape
    dil = shape[1]
    rows_all = _seq_view(qkv)
    nseq, length, _ = rows_all.shape
    bq = min(QUERY_BLOCK, length)
    wk = min(3 * bq, length)
    nb = length // bq

    def body(qkv_ref, do_ref, lse_ref, dl_ref, cos_ref, sin_ref, dp_ref, kk_ref, vv_ref, dk_ref, dv_ref):
        _pair_variants(qkv_ref, Q_W, kk_ref)
        _pair_variants(qkv_ref, Q_W + KV_W, vv_ref)
        dk_ref[...] = jnp.zeros_like(dk_ref)
        dv_ref[...] = jnp.zeros_like(dv_ref)
        band = _band(bq, wk)
        lo_k = lax.broadcasted_iota(jnp.int32, (wk, LANES), 1) < HEAD_DIM

        def block(i, carry):
            q0, k0 = _window(i, bq, wk, length)
            valid = jnp.abs(band + (q0 - k0)) <= w
            rows, krows = pl.ds(q0, bq), pl.ds(k0, wk)
            c, sn = cos_ref[rows, :], -sin_ref[rows, :]
            lse_t, dl_t = lse_ref[rows, :], dl_ref[rows, :]
            for kv in range(N_KV):
                heads = [(kv * GRP + h, h % 2) for h in range(GRP)]
                cols = [slice((kv * 2 + j) * LANES, (kv * 2 + j + 1) * LANES) for j in range(GRP // 2)]
                qp = [qkv_ref[rows, cs] for cs in cols]
                dop = [do_ref[rows, cs] for cs in cols]
                sc = [lax.dot_general(qp[h // 2], kk_ref[kv, half, krows, :], NT, preferred_element_type=F32)
                      for h, (_, half) in enumerate(heads)]
                dp = [lax.dot_general(dop[h // 2], vv_ref[kv, half, krows, :], NT, preferred_element_type=F32)
                      for h, (_, half) in enumerate(heads)]
                p = [jnp.exp(jnp.where(valid, s_, NEG_INF) - _over_keys(lse_t[:, hd:hd + 1], wk))
                     for s_, (hd, _) in zip(sc, heads)]
                ds = [(p_ * (dp_ - _over_keys(dl_t[:, hd:hd + 1], wk))).astype(BF16) for p_, dp_, (hd, _) in zip(p, dp, heads)]
                pb = [p_.astype(BF16) for p_ in p]
                for j in range(GRP // 2):
                    dq = (jnp.dot(ds[2 * j], kk_ref[kv, 0, krows, :], preferred_element_type=F32)
                          + jnp.dot(ds[2 * j + 1], kk_ref[kv, 1, krows, :], preferred_element_type=F32)) * SCALE
                    dp_ref[rows, cols[j]] = _rope(dq, c, sn).astype(BF16)
                dk = [lax.dot_general(ds[h], qp[h // 2], TN, preferred_element_type=F32) for h in range(GRP)]
                dv = [lax.dot_general(pb[h], dop[h // 2], TN, preferred_element_type=F32) for h in range(GRP)]
                dk_ref[kv, krows, :] += jnp.where(lo_k, dk[0] + dk[2], dk[1] + dk[3])
                dv_ref[kv, krows, :] += jnp.where(lo_k, dv[0] + dv[2], dv[1] + dv[3])
            return carry

        lax.fori_loop(0, nb, block, 0)
        lo = lax.broadcasted_iota(jnp.int32, (length, LANES), 1) < HEAD_DIM
        c, sn = cos_ref[...], -sin_ref[...]
        for ch in range(KV_W // LANES):
            halves = []
            for acc_ref in (dk_ref, dv_ref):
                even, odd = acc_ref[2 * ch], acc_ref[2 * ch + 1]
                even = even + pltpu.roll(even, HEAD_DIM, 1)
                odd = odd + pltpu.roll(odd, HEAD_DIM, 1)
                halves.append(jnp.where(lo, even, odd))
            dp_ref[:, Q_W + ch * LANES:Q_W + (ch + 1) * LANES] = _rope(halves[0], c, sn).astype(BF16)
            dp_ref[:, Q_W + KV_W + ch * LANES:Q_W + KV_W + (ch + 1) * LANES] = halves[1].astype(BF16)

    def seq_block(c):
        return pl.BlockSpec((None, length, c), lambda i: (i, 0, 0), pipeline_mode=pl.Buffered(1))

    table = pl.BlockSpec((None, length, LANES), lambda i: (i % dil, 0, 0), pipeline_mode=pl.Buffered(1))
    (out,), exchanged = _hosted_call(
        body, exchange, name=f"attn_bwd_{tag}", grid=(nseq,),
        in_specs=[seq_block(QKV_W), seq_block(Q_W), seq_block(LANES), seq_block(LANES), table, table],
        out_specs=[pl.BlockSpec((None, length, QKV_W), lambda i: (i, 0, 0))],
        out_shape=[jax.ShapeDtypeStruct((nseq, length, QKV_W), BF16)],
        scratch_shapes=[pltpu.VMEM((N_KV, 2, length, LANES), BF16), pltpu.VMEM((N_KV, 2, length, LANES), BF16),
                        pltpu.VMEM((N_KV, length, LANES), F32), pltpu.VMEM((N_KV, length, LANES), F32)],
        semantics=("parallel",), args=(rows_all, _seq_view(do), _seq_view(lse), _seq_view(delta), cos, sin))
    return out.reshape(shape), exchanged


def _qkv_bwd(dy, x, gain, w, dps, dils, seq, tag):
    t, d = x.shape
    ts = _tile_rows(seq)
    bl = t // seq
    ng = len(dps)

    def body(dy_ref, x_ref, gain_ref, w_ref, *refs):
        dp_refs, dx_ref = refs[:ng], refs[ng]
        h_refs = refs[ng + 1:2 * ng + 1]
        dgain_ref, stage_ref = refs[2 * ng + 1:]
        dh = None
        for gi in range(ng):
            dil = dils[gi]
            n = ts // dil
            dp = dp_refs[gi][0] if dil == 1 else jnp.concatenate([dp_refs[gi][r] for r in range(dil)], axis=0)
            part = lax.dot_general(dp, w_ref[:, gi * QKV_W:(gi + 1) * QKV_W], NT, preferred_element_type=F32)
            part = _merge_rows([part[r * n:(r + 1) * n] for r in range(dil)], stage_ref, dil)
            dh = part if dh is None else dh + part
        xv, gain_v = x_ref[...], gain_ref[...]
        dx, xhat = _rms_bwd(dh, xv, gain_v)
        dx_ref[...] = dy_ref[...] + dx
        h = xhat * gain_v
        for gi in range(ng):
            for r, part in enumerate(_split_rows(h, stage_ref, dils[gi])):
                h_refs[gi][r] = part.astype(BF16)
        _accumulate(dgain_ref, jnp.sum(dh * xhat, axis=0, keepdims=True), pl.program_id(0) == 0)

    row = pl.BlockSpec((ts, d), lambda i: (i, 0))
    outs = pl.pallas_call(
        body, name=f"qkv_bwd_{tag}", grid=(t // ts,),
        in_specs=[row, row, _resident((1, d)), _resident(w.shape)] + [_res_spec(seq, dl, QKV_W) for dl in dils],
        out_specs=[row] + [_res_spec(seq, dl, d) for dl in dils] + [pl.BlockSpec((1, d), lambda i: (0, 0))],
        out_shape=[jax.ShapeDtypeStruct((t, d), F32)] + [jax.ShapeDtypeStruct(_res_shape(bl, seq, dl, d), BF16) for dl in dils]
                  + [jax.ShapeDtypeStruct((1, d), F32)],
        scratch_shapes=[_stage(ts, d)],
        compiler_params=_params("arbitrary"),
    )(dy, x, gain, w, *dps)
    return outs[0], list(outs[1:1 + ng]), outs[1 + ng]


ANY = pl.BlockSpec(memory_space=pl.ANY)


def _place():
    x, y, c = lax.axis_index("x"), lax.axis_index("y"), lax.axis_index("c")
    return x, y, c


def _all_gather(shards):
    n = len(shards)

    def body(*refs):
        ins, outs = refs[:n], refs[n:2 * n]
        send_sems, recv_sems, local_sems = refs[2 * n:]
        x, y, c = _place()
        sibling = (x, y, 1 - c)
        chips = [(1 - x, y), (x, 1 - y), (1 - x, 1 - y)]

        def copy(a, k, block, to, src=None):
            px, py, pc = block
            rows = outs[a].at[4 * px + 2 * py + pc]
            return pltpu.make_async_remote_copy(
                src_ref=rows if src is None else src, dst_ref=rows, send_sem=send_sems.at[a, k], recv_sem=recv_sems.at[a, k],
                device_id=to, device_id_type=MESH)

        sent = []
        for a in range(n):
            mine = pltpu.make_async_copy(ins[a], outs[a].at[4 * x + 2 * y + c], local_sems.at[a])
            mine.start()
            sent.append(mine)
        for a in range(n):
            first = [copy(a, 0, (x, y, c), sibling, src=ins[a])]
            first += [copy(a, 1 + j, (x, y, c), (*chip, c), src=ins[a]) for j, chip in enumerate(chips)]
            for cp in first:
                cp.start()
            sent += first
        for a in range(n):
            for j, chip in enumerate(chips):
                copy(a, 1 + j, (*chip, c), (x, y, c)).wait_recv()
                passed = copy(a, 4 + j, (*chip, c), sibling)
                passed.start()
                sent.append(passed)
        for a in range(n):
            copy(a, 0, sibling, (x, y, c)).wait_recv()
            for j, chip in enumerate(chips):
                copy(a, 4 + j, (*chip, 1 - c), (x, y, c)).wait_recv()
        for cp in sent[n:]:
            cp.wait_send()
        for mine in sent[:n]:
            mine.wait()

    return pl.pallas_call(
        body, name="all_gather_weights", in_specs=[ANY] * n, out_specs=[ANY] * n,
        out_shape=[jax.ShapeDtypeStruct((N_DEV,) + s.shape, s.dtype) for s in shards],
        scratch_shapes=[pltpu.SemaphoreType.DMA((n, 7)), pltpu.SemaphoreType.DMA((n, 7)), pltpu.SemaphoreType.DMA((n,))],
    )(*shards)


def _exchange_copies(srcs, dsts, gather, send_sems, recv_sems, local_sems):
    x, y, c = _place()
    me = 4 * x + 2 * y + c
    copies = []
    for a, (src, dst) in enumerate(zip(srcs, dsts)):
        copies.append(pltpu.make_async_copy(src if gather[a] else src.at[me], dst.at[me], local_sems.at[a]))
        for k in range(1, N_DEV):
            peer = me ^ k
            copies.append(pltpu.make_async_remote_copy(
                src_ref=src if gather[a] else src.at[peer], dst_ref=dst.at[me], send_sem=send_sems.at[a, k - 1],
                recv_sem=recv_sems.at[a, k - 1], device_id=(peer // 4, (peer // 2) % 2, peer % 2), device_id_type=MESH))
    return copies


def _exchange_scratch(n):
    return [pltpu.SemaphoreType.DMA((n, N_DEV - 1)), pltpu.SemaphoreType.DMA((n, N_DEV - 1)), pltpu.SemaphoreType.DMA((n,))]


def _exchanged_shapes(exchange):
    return [jax.ShapeDtypeStruct(((N_DEV,) + a.shape) if g else a.shape, a.dtype) for a, g in exchange]


def _hosted_call(body, exchange, *, name, grid, in_specs, out_specs, out_shape, scratch_shapes, semantics, args):
    out_specs, out_shape, scratch = list(out_specs), list(out_shape), list(scratch_shapes)
    if not exchange:
        outs = pl.pallas_call(body, name=name, grid=grid, in_specs=in_specs, out_specs=out_specs, out_shape=out_shape,
                              scratch_shapes=scratch, compiler_params=_params(*semantics))(*args)
        return list(outs), []
    n, n_in, n_out, n_scr = len(exchange), len(in_specs), len(out_specs), len(scratch)
    gather = [g for _, g in exchange]

    def hosted(*refs):
        own_in, x_in = refs[:n_in], refs[n_in:n_in + n]
        own_out, x_out = refs[n_in + n:n_in + n + n_out], refs[n_in + n + n_out:n_in + 2 * n + n_out]
        own_scr, sems = refs[n_in + 2 * n + n_out:n_in + 2 * n + n_out + n_scr], refs[-3:]

        @pl.when(pl.program_id(0) == 0)
        def _():
            for cp in _exchange_copies(x_in, x_out, gather, *sems):
                cp.start()

        body(*own_in, *own_out, *own_scr)

        @pl.when(pl.program_id(0) == pl.num_programs(0) - 1)
        def _():
            for cp in _exchange_copies(x_in, x_out, gather, *sems):
                cp.wait()

    outs = pl.pallas_call(
        hosted, name=name, grid=grid, in_specs=list(in_specs) + [ANY] * n, out_specs=out_specs + [ANY] * n,
        out_shape=out_shape + _exchanged_shapes(exchange), scratch_shapes=scratch + _exchange_scratch(n),
        compiler_params=_params("arbitrary"),
    )(*args, *[a for a, _ in exchange])
    return list(outs[:n_out]), list(outs[n_out:])


def _exchange_now(exchange):
    n = len(exchange)
    gather = [g for _, g in exchange]

    def body(*refs):
        copies = _exchange_copies(refs[:n], refs[n:2 * n], gather, *refs[2 * n:])
        for cp in copies:
            cp.start()
        for cp in copies:
            cp.wait()

    return pl.pallas_call(
        body, name="exchange_last", in_specs=[ANY] * n, out_specs=[ANY] * n, out_shape=_exchanged_shapes(exchange),
        scratch_shapes=_exchange_scratch(n),
    )(*[a for a, _ in exchange])


def _all_reduce_small(v):
    def body(v_ref, o_ref, recv_ref, send_sems, recv_sems):
        x, y, c = _place()
        me = 4 * x + 2 * y + c
        copies = []
        for k in range(1, N_DEV):
            peer = me ^ k
            copies.append(pltpu.make_async_remote_copy(
                src_ref=v_ref, dst_ref=recv_ref.at[k], send_sem=send_sems.at[k - 1], recv_sem=recv_sems.at[k - 1],
                device_id=(peer // 4, (peer // 2) % 2, peer % 2), device_id_type=MESH))
        for cp in copies:
            cp.start()
        recv_ref[0] = v_ref[...]
        for cp in copies:
            cp.wait()
        acc = recv_ref[me]
        for src in range(1, N_DEV):
            acc = acc + recv_ref[me ^ src]
        o_ref[...] = acc

    vm = pl.BlockSpec(memory_space=pltpu.VMEM)
    return pl.pallas_call(
        body, name="all_reduce_small", in_specs=[vm], out_specs=vm, out_shape=jax.ShapeDtypeStruct(v.shape, F32),
        scratch_shapes=[pltpu.VMEM((N_DEV,) + v.shape, F32), pltpu.SemaphoreType.DMA((N_DEV - 1,)),
                        pltpu.SemaphoreType.DMA((N_DEV - 1,))],
    )(v)


def _adamw_math(w, g, m, v):
    m = ADAM_B1 * m + (1.0 - ADAM_B1) * g
    v = ADAM_B2 * v + (1.0 - ADAM_B2) * (g * g)
    m_hat = m / (1.0 - ADAM_B1 ** ADAM_STEP)
    v_hat = v / (1.0 - ADAM_B2 ** ADAM_STEP)
    delta = -ADAM_LR * (m_hat / (jnp.sqrt(v_hat) + ADAM_EPS) + ADAM_WD * w)
    return delta, m, v


def _adamw(parts, w, m, v, name):
    r, c = w.shape
    tr = r // 2 if r % 16 == 0 and r >= 256 else r
    n = len(parts)

    def body(*refs):
        w_ref, m_ref, v_ref, g_ref, d_ref, nm_ref, nv_ref = refs[n:]
        g = refs[0][...].astype(F32)
        for p_ref in refs[1:n]:
            g = g + p_ref[...].astype(F32)
        g_ref[...] = g
        d_ref[...], nm_ref[...], nv_ref[...] = _adamw_math(w_ref[...], g, m_ref[...], v_ref[...])

    tile = pl.BlockSpec((tr, c), lambda i: (i, 0))
    arrays, in_specs = [], []
    for p in parts:
        if isinstance(p, tuple):
            arrays.append(p[0])
            in_specs.append(pl.BlockSpec((None, tr, c), functools.partial(lambda i, slot: (slot, i, 0), slot=p[1])))
        else:
            arrays.append(p)
            in_specs.append(tile)
    return pl.pallas_call(
        body, name=name, grid=(r // tr,), in_specs=in_specs + [tile] * 3, out_specs=[tile] * 4,
        out_shape=[jax.ShapeDtypeStruct((r, c), F32)] * 4, compiler_params=_params("parallel"),
    )(*arrays, w, m, v)


def _columns(g):
    return g.transpose(1, 0, 2).reshape(g.shape[1], -1)


def _rows(g):
    return g.reshape(-1, g.shape[-1])


def _column_blocks(dw):
    k, n = dw.shape
    return dw.reshape(k, N_DEV, n // N_DEV).transpose(1, 0, 2)


def _row_blocks(dw):
    k, n = dw.shape
    return dw.reshape(N_DEV, k // N_DEV, n)


def _pack_rows(rows, width):
    out = None
    for i, r in enumerate(rows):
        r = r.reshape(1, -1).astype(F32)
        r = jnp.pad(r, ((i, 8 - 1 - i), (0, width - r.shape[1])))
        out = r if out is None else out + r
    return out


def _mixer_fwd(x, gain, w_in, w_out, cos, sin, seq, groups, tag, sink=None, exchanges=None):
    qkvs, os, lses, got = [], [], [], {}
    for gi, (dil, w) in enumerate(groups):
        qkv = _qkv_proj(x, gain, w_in, _tables_tiled(cos, seq, dil), _tables_tiled(sin, seq, dil), seq, dil, gi, f"{tag}{gi}")
        o, lse, got[gi] = _attn_fwd(qkv, w, f"{tag}{gi}", sink=sink, exchange=(exchanges or {}).get(gi, ()))
        qkvs.append(qkv)
        os.append(o)
        lses.append(lse)
    y, o, lses = _out_proj(x, os, lses, [dl for dl, _ in groups], w_out, seq, tag)
    return y, (qkvs, o, lses), got


def _mixer_bwd(dy, x_in, gain, w_in, w_out, saved, cos, sin, seq, groups, tag, sink=None, exchange=()):
    qkvs, o, lses = saved
    t, d = x_in.shape
    dils = [dl for dl, _ in groups]
    lse_tokens = lses[0].reshape(t, LANES) if sink is not None else None
    dos, dls, dsink = _attn_out_bwd(dy, w_out, o, dils, seq, tag, lse=lse_tokens, sink=sink)
    (dw_out,) = _tn_matmul(o, [dy], f"dw_out_{tag}")
    dps, got = [], []
    for gi, (dil, w) in enumerate(groups):
        dp, brought = _attn_bwd(qkvs[gi], dos[gi], lses[gi], dls[gi], _tables_by_residue(cos, seq, dil),
                                _tables_by_residue(sin, seq, dil), w, f"{tag}{gi}", exchange=exchange if gi == 0 else ())
        dps.append(dp)
        got += brought
    dx, hs, dgain = _qkv_bwd(dy, x_in, gain, w_in, dps, dils, seq, tag)
    dw_in = [_tn_matmul(hs[gi].reshape(t, d), [dps[gi].reshape(t, QKV_W)], f"dw_in_{tag}{gi}")[0] for gi in range(len(groups))]
    dw_in = dw_in[0] if len(dw_in) == 1 else jnp.concatenate(dw_in, axis=1)
    return dx, dw_in, dw_out, dgain, dsink, got


def _ffn_layer_bwd(dy, x_in, gain, g, u, wg, wu, wd, tag, exchange=()):
    dx, dg, du, act, h, dgain, got = _ffn_bwd(dy, x_in, gain, g, u, wg, wu, wd, tag, exchange=exchange)
    (dwd,) = _tn_matmul(act, [dy], f"dw_down_{tag}")
    dwg, dwu = _tn_matmul(h, [dg, du], f"dw_gate_up_{tag}")
    return dx, dwg, dwu, dwd, dgain, got


def _to_send(dws_by_columns, dws_by_rows):
    return ([(_column_blocks(g).astype(BF16), False) for g in dws_by_columns]
            + [(_row_blocks(g).astype(BF16), False) for g in dws_by_rows])


def kernel(x, a_w_in, a_sink, a_w_out, b_w_in, b_w_out, norm_mix, norm_ffn, w_gate, w_up, w_down, final_norm, loss_target, m_a_w_in, m_a_sink, m_a_w_out, m_b_w_in, m_b_w_out, m_norm_mix, m_norm_ffn, m_w_gate, m_w_up, m_w_down, m_final_norm, v_a_w_in, v_a_sink, v_a_w_out, v_b_w_in, v_b_w_out, v_norm_mix, v_norm_ffn, v_w_gate, v_w_up, v_w_down, v_final_norm):
    bl, seq, d = x.shape
    t = bl * seq
    xf = x.reshape(t, d)
    target = loss_target.reshape(t, d)
    cos, sin = _rope_tables(seq)
    groups_a = [(1, ATTN_HALF_WINDOW)]
    groups_b = [(dil, window // 2 // dil) for window, dil in DILATED_GROUPS]

    def shard(w_):
        return w_.astype(BF16)

    wa_in, wa_out = _all_gather([shard(a_w_in[0]), shard(a_w_out[0])])
    wa_in, wa_out = _columns(wa_in), _rows(wa_out)

    x1_0, saved_a, got = _mixer_fwd(xf, norm_mix[0:1], wa_in, wa_out, cos, sin, seq, groups_a, "a", sink=a_sink[0],
                                    exchanges={0: [(shard(w_gate[0]), True), (shard(w_up[0]), True), (shard(w_down[0]), True)]})
    wg0, wu0, wd0 = _columns(got[0][0]), _columns(got[0][1]), _rows(got[0][2])
    x2_0, g0, u0, got = _ffn_fwd(x1_0, norm_ffn[0:1], wg0, wu0, wd0, "0",
                                 exchange=[(shard(b_w_in[0]), True), (shard(b_w_out[0]), True)])
    wb_in, wb_out = _columns(got[0]), _rows(got[1])
    x1_1, saved_b, got = _mixer_fwd(x2_0, norm_mix[1:2], wb_in, wb_out, cos, sin, seq, groups_b, "b",
                                    exchanges={0: [(shard(w_gate[1]), True), (shard(w_up[1]), True)], 1: [(shard(w_down[1]), True)]})
    wg1, wu1, wd1 = _columns(got[0][0]), _columns(got[0][1]), _rows(got[1][0])
    x2_1, g1, u1, _ = _ffn_fwd(x1_1, norm_ffn[1:2], wg1, wu1, wd1, "1")

    dy, loss_part, d_final = _loss_bwd(x2_1, final_norm.reshape(1, d), target)
    dy, dwg1, dwu1, dwd1, d_nf1, _ = _ffn_layer_bwd(dy, x1_1, norm_ffn[1:2], g1, u1, wg1, wu1, wd1, "1")
    dy, dwb_in, dwb_out, d_nm1, _, (r_g1, r_u1, r_d1) = _mixer_bwd(
        dy, x2_0, norm_mix[1:2], wb_in, wb_out, saved_b, cos, sin, seq, groups_b, "b", exchange=_to_send([dwg1, dwu1], [dwd1]))
    dy, dwg0, dwu0, dwd0, d_nf0, (r_b_in, r_b_out) = _ffn_layer_bwd(
        dy, x1_0, norm_ffn[0:1], g0, u0, wg0, wu0, wd0, "0", exchange=_to_send([dwb_in], [dwb_out]))
    dy, dwa_in, dwa_out, d_nm0, d_sink, (r_g0, r_u0, r_d0) = _mixer_bwd(
        dy, xf, norm_mix[0:1], wa_in, wa_out, saved_a, cos, sin, seq, groups_a, "a", sink=a_sink[0],
        exchange=_to_send([dwg0, dwu0], [dwd0]))
    r_a_in, r_a_out = _exchange_now(_to_send([dwa_in], [dwa_out]))
    grad_x = dy.reshape(bl, seq, d)

    def shard2d(a):
        return a.reshape(-1, a.shape[-1])

    big = [(r_a_in, a_w_in, m_a_w_in, v_a_w_in), (r_b_in, b_w_in, m_b_w_in, v_b_w_in),
           (r_g0, w_gate[0], m_w_gate[0], v_w_gate[0]), (r_g1, w_gate[1], m_w_gate[1], v_w_gate[1]),
           (r_u0, w_up[0], m_w_up[0], v_w_up[0]), (r_u1, w_up[1], m_w_up[1], v_w_up[1]),
           (r_a_out, a_w_out, m_a_w_out, v_a_w_out), (r_b_out, b_w_out, m_b_w_out, v_b_w_out),
           (r_d0, w_down[0], m_w_down[0], v_w_down[0]), (r_d1, w_down[1], m_w_down[1], v_w_down[1])]
    upd = [_adamw([(r_, src) for src in range(N_DEV)], shard2d(w_), shard2d(m_), shard2d(v_), f"adamw_{a}")
           for a, (r_, w_, m_, v_) in enumerate(big)]
    (u_a_in, u_b_in, u_g0, u_g1, u_u0, u_u1, u_a_out, u_b_out, u_d0, u_d1) = upd

    small = _pack_rows([d_nm0, d_nm1, d_nf0, d_nf1, d_final, d_sink, loss_part], d)
    total = _all_reduce_small(small)
    small_w = _pack_rows([norm_mix[0], norm_mix[1], norm_ffn[0], norm_ffn[1], final_norm, a_sink], d)
    small_m = _pack_rows([m_norm_mix[0], m_norm_mix[1], m_norm_ffn[0], m_norm_ffn[1], m_final_norm, m_a_sink], d)
    small_v = _pack_rows([v_norm_mix[0], v_norm_mix[1], v_norm_ffn[0], v_norm_ffn[1], v_final_norm, v_a_sink], d)
    u_small = _adamw([total], small_w, small_m, small_v, "adamw_small")
    loss = total[6, 0]

    outs = []
    for k in range(4):
        sm = u_small[k]
        outs += [
            u_a_in[k].reshape(a_w_in.shape), sm[5:6, :N_HEADS], u_a_out[k].reshape(a_w_out.shape),
            u_b_in[k].reshape(b_w_in.shape), u_b_out[k].reshape(b_w_out.shape), sm[0:2], sm[2:4],
            jnp.stack([u_g0[k], u_g1[k]]), jnp.stack([u_u0[k], u_u1[k]]), jnp.stack([u_d0[k], u_d1[k]]), sm[4],
        ]
    return (loss, grad_x, *outs)
```

```python
import functools
import math

import jax
import jax.numpy as jnp
from jax import lax
from jax.experimental import pallas as pl
from jax.experimental.pallas import tpu as pltpu

F32 = jnp.float32
BF16 = jnp.bfloat16

HEAD_DIM = 64
N_HEADS = 16
N_KV = 4
GRP = N_HEADS // N_KV
Q_W = N_HEADS * HEAD_DIM
KV_W = N_KV * HEAD_DIM
QKV_W = Q_W + 2 * KV_W
ATTN_HALF_WINDOW = 128
DILATED_GROUPS = ((128, 1), (512, 4), (2048, 16))
ROPE_THETA = 10000.0
RMS_EPS = 1e-6
NEG_INF = -1e30
SCALE = 1.0 / math.sqrt(HEAD_DIM)

ADAM_LR = 0.001
ADAM_B1 = 0.9
ADAM_B2 = 0.999
ADAM_EPS = 1e-08
ADAM_WD = 0.01
ADAM_STEP = 10

LANES = 128
VMEM_LIMIT = 56 * 1024 * 1024
QUERY_BLOCK = 128
N_DEV = 8
MESH = pl.DeviceIdType.MESH

NT = (((1,), (1,)), ((), ()))
TN = (((0,), (0,)), ((), ()))


def _params(*sem):
    return pltpu.CompilerParams(dimension_semantics=tuple(sem) if sem else None, vmem_limit_bytes=VMEM_LIMIT)


def _resident(shape):
    return pl.BlockSpec(shape, lambda *_: (0,) * len(shape), pipeline_mode=pl.Buffered(1))


def _rope_tables(seq):
    inv_freq = 1.0 / (ROPE_THETA ** (jnp.arange(0, HEAD_DIM, 2, dtype=F32) / HEAD_DIM))
    ang = jnp.arange(seq, dtype=F32)[:, None] * inv_freq[None, :]
    cos, sin = jnp.cos(ang), jnp.sin(ang)
    return jnp.tile(cos, (1, 4)), jnp.concatenate([-sin, sin, -sin, sin], axis=1)


def _rope(t, cos, sin_signed):
    lane = lax.broadcasted_iota(jnp.int32, t.shape, 1)
    first = (lane & (HEAD_DIM // 2)) == 0
    swapped = jnp.where(first, pltpu.roll(t, LANES - HEAD_DIM // 2, 1), pltpu.roll(t, HEAD_DIM // 2, 1))
    return t * cos + swapped * sin_signed


def _rms(x):
    return lax.rsqrt(jnp.mean(x * x, axis=-1, keepdims=True) + RMS_EPS)


def _rms_bwd(dh, x, gain):
    r = _rms(x)
    xhat = x * r
    dxh = dh * gain
    dx = r * (dxh - xhat * jnp.mean(dxh * xhat, axis=-1, keepdims=True))
    return dx, xhat


def _accumulate(ref, value, first):
    @pl.when(first)
    def _():
        ref[...] = jnp.zeros_like(ref)

    ref[...] += value


def _tile_rows(seq):
    return min(512, seq)


def _res_shape(bl, seq, dil, c):
    ts = _tile_rows(seq)
    return (bl, dil, seq // ts, ts // dil, c)


def _res_spec(seq, dil, c):
    ts = _tile_rows(seq)
    per_seq = seq // ts
    return pl.BlockSpec((None, dil, None, ts // dil, c), lambda i: (i // per_seq, 0, i % per_seq, 0, 0))


def _seq_view(a):
    bl, dil, tiles, n, c = a.shape
    return a.reshape(bl * dil, tiles * n, c)


def _stage(ts, c):
    return pltpu.VMEM((c // LANES, ts, LANES), F32)


def _split_rows(val, stage_ref, dil):
    if dil == 1:
        return [val]
    ts, c = val.shape
    n, nc = ts // dil, c // LANES
    for k in range(nc):
        stage_ref[k] = val[:, k * LANES:(k + 1) * LANES]
    return [jnp.concatenate([stage_ref[k, pl.ds(r, n, stride=dil), :] for k in range(nc)], axis=1) for r in range(dil)]


def _merge_rows(parts, stage_ref, dil):
    if dil == 1:
        return parts[0]
    n, c = parts[0].shape
    nc = c // LANES
    for r, part in enumerate(parts):
        for k in range(nc):
            stage_ref[k, pl.ds(r, n, stride=dil), :] = part[:, k * LANES:(k + 1) * LANES]
    return jnp.concatenate([stage_ref[k] for k in range(nc)], axis=1)


def _tables_tiled(table, seq, dil):
    ts = _tile_rows(seq)
    return table.reshape(seq // ts, ts // dil, dil, LANES).transpose(0, 2, 1, 3).reshape(seq, LANES)


def _tables_by_residue(table, seq, dil):
    return table.reshape(seq // dil, dil, LANES).transpose(1, 0, 2)


def _qkv_proj(x, gain, w, cos, sin, seq, dil, group, tag, exchange=()):
    t, d = x.shape
    ts = _tile_rows(seq)
    n = ts // dil
    per_seq = seq // ts

    def body(x_ref, g_ref, w_ref, cos_ref, sin_ref, o_ref, stage_ref):
        xv = jnp.concatenate(_split_rows(x_ref[...], stage_ref, dil), axis=0)
        h = (xv * _rms(xv) * g_ref[...]).astype(BF16)
        acc = jnp.dot(h, w_ref[...], preferred_element_type=F32)
        c, s = cos_ref[...], sin_ref[...]
        for j in range(QKV_W // LANES):
            cols = slice(j * LANES, (j + 1) * LANES)
            val = acc[:, cols]
            if j < (Q_W + KV_W) // LANES:
                val = _rope(val, c, s)
            if j < Q_W // LANES:
                val = val * SCALE
            val = val.astype(BF16)
            for r in range(dil):
                o_ref[r, :, cols] = val[r * n:(r + 1) * n]

    table = pl.BlockSpec((ts, LANES), lambda i: (i % per_seq, 0))
    (qkv,), exchanged = _hosted_call(
        body, exchange, name=f"qkv_proj_{tag}", grid=(t // ts,),
        in_specs=[pl.BlockSpec((ts, d), lambda i: (i, 0)), pl.BlockSpec((1, d), lambda i: (0, 0)),
                  pl.BlockSpec((d, QKV_W), lambda i: (0, group)), table, table],
        out_specs=[_res_spec(seq, dil, QKV_W)],
        out_shape=[jax.ShapeDtypeStruct(_res_shape(t // seq, seq, dil, QKV_W), BF16)],
        scratch_shapes=[_stage(ts, d)], semantics=("parallel",), args=(x, gain, w, cos, sin))
    return qkv, exchanged


def _band(bq, wk):
    return lax.broadcasted_iota(jnp.int32, (bq, wk), 0) - lax.broadcasted_iota(jnp.int32, (bq, wk), 1)


def _pair_variants(src_ref, base, dst_ref):
    lo = lax.broadcasted_iota(jnp.int32, (src_ref.shape[0], LANES), 1) < HEAD_DIM
    for c in range(KV_W // LANES):
        chunk = src_ref[:, base + c * LANES:base + (c + 1) * LANES]
        rolled = pltpu.roll(chunk, HEAD_DIM, 1)
        zero = jnp.zeros_like(chunk)
        dst_ref[2 * c, 0] = jnp.where(lo, chunk, zero)
        dst_ref[2 * c, 1] = jnp.where(lo, zero, rolled)
        dst_ref[2 * c + 1, 0] = jnp.where(lo, rolled, zero)
        dst_ref[2 * c + 1, 1] = jnp.where(lo, zero, chunk)


def _over_keys(col, wk):
    if wk % LANES:
        return jnp.broadcast_to(col, (col.shape[0], wk))
    wide = jnp.broadcast_to(col, (col.shape[0], LANES))
    return wide if wk == LANES else jnp.concatenate([wide] * (wk // LANES), axis=1)


def _window(i, bq, wk, length):
    q0 = pl.multiple_of(i * bq, bq)
    k0 = pl.multiple_of(jnp.clip(q0 - bq, 0, length - wk), bq)
    return q0, k0


def _attn_fwd(qkv, w, tag, sink=None, exchange=()):
    shape = qkv.shape
    rows_all = _seq_view(qkv)
    nseq, length, _ = rows_all.shape
    bq = min(QUERY_BLOCK, length)
    wk = min(3 * bq, length)
    nb = length // bq
    has_sink = sink is not None

    def body(*refs):
        qkv_ref = refs[0]
        sink_ref = refs[1] if has_sink else None
        o_ref, lse_ref, kk_ref, vv_ref = refs[-4:]
        _pair_variants(qkv_ref, Q_W, kk_ref)
        _pair_variants(qkv_ref, Q_W + KV_W, vv_ref)
        band = _band(bq, wk)
        lane = lax.broadcasted_iota(jnp.int32, (bq, LANES), 1)
        lo = lane < HEAD_DIM

        def block(i, carry):
            q0, k0 = _window(i, bq, wk, length)
            valid = jnp.abs(band + (q0 - k0)) <= w
            rows, krows = pl.ds(q0, bq), pl.ds(k0, wk)
            lse_tile = jnp.zeros((bq, LANES), F32)
            for kv in range(N_KV):
                heads = [(kv * GRP + h, h % 2) for h in range(GRP)]
                qp = [qkv_ref[rows, (kv * 2 + j) * LANES:(kv * 2 + j + 1) * LANES] for j in range(GRP // 2)]
                k2 = jnp.concatenate([kk_ref[kv, 0, krows, :], kk_ref[kv, 1, krows, :]], axis=0)
                v2 = jnp.concatenate([vv_ref[kv, 0, krows, :], vv_ref[kv, 1, krows, :]], axis=0)
                sc2 = [lax.dot_general(q_, k2, NT, preferred_element_type=F32) for q_ in qp]
                sc = [jnp.where(valid, s_[:, half * wk:(half + 1) * wk], NEG_INF) for s_ in sc2 for half in range(2)]
                m = [jnp.max(s_, axis=-1, keepdims=True) for s_ in sc]
                if has_sink:
                    m = [jnp.maximum(m_, sink_ref[hd]) for m_, (hd, _) in zip(m, heads)]
                mb = [jnp.broadcast_to(m_, (bq, LANES)) for m_ in m]
                p = [jnp.exp(s_ - _over_keys(m_, wk)) for s_, m_ in zip(sc, m)]
                den = [jnp.sum(p_, axis=-1, keepdims=True) for p_ in p]
                if has_sink:
                    den = [d_ + jnp.exp(sink_ref[hd] - m_) for d_, m_, (hd, _) in zip(den, m, heads)]
                inv = [jnp.broadcast_to(1.0 / d_, (bq, LANES)) for d_ in den]
                pb = [p_.astype(BF16) for p_ in p]
                for j in range(GRP // 2):
                    o = jnp.dot(jnp.concatenate([pb[2 * j], pb[2 * j + 1]], axis=1), v2, preferred_element_type=F32)
                    o = o * jnp.where(lo, inv[2 * j], inv[2 * j + 1])
                    o_ref[rows, (kv * 2 + j) * LANES:(kv * 2 + j + 1) * LANES] = o.astype(BF16)
                for h, (hd, _) in enumerate(heads):
                    lse_tile = jnp.where(lane == hd, mb[h] - jnp.log(inv[h]), lse_tile)
            lse_ref[rows, :] = lse_tile
            return carry

        lax.fori_loop(0, nb, block, 0)

    args = [rows_all]
    in_specs = [pl.BlockSpec((None, length, QKV_W), lambda i: (i, 0, 0))]
    if has_sink:
        args.append(sink)
        in_specs.append(pl.BlockSpec(memory_space=pltpu.SMEM))
    (o, lse), exchanged = _hosted_call(
        body, exchange, name=f"attn_fwd_{tag}", grid=(nseq,), in_specs=in_specs,
        out_specs=[pl.BlockSpec((None, length, Q_W), lambda i: (i, 0, 0)), pl.BlockSpec((None, length, LANES), lambda i: (i, 0, 0))],
        out_shape=[jax.ShapeDtypeStruct((nseq, length, Q_W), BF16), jax.ShapeDtypeStruct((nseq, length, LANES), F32)],
        scratch_shapes=[pltpu.VMEM((N_KV, 2, length, LANES), BF16), pltpu.VMEM((N_KV, 2, length, LANES), BF16)],
        semantics=("parallel",), args=args)
    return o.reshape(shape[:-1] + (Q_W,)), lse.reshape(shape[:-1] + (LANES,)), exchanged


def _head_expand():
    return (jnp.arange(LANES)[:, None] == jnp.arange(Q_W)[None, :] // HEAD_DIM).astype(BF16)


def _out_proj(x, os, lses, dils, w, seq, tag):
    t, d = x.shape
    ts = _tile_rows(seq)
    ng = len(os)
    bl = t // seq
    if ng == 1:
        def body1(x_ref, o_ref, w_ref, y_ref):
            y_ref[...] = x_ref[...] + jnp.dot(o_ref[...], w_ref[...], preferred_element_type=F32)

        row = pl.BlockSpec((ts, d), lambda i: (i, 0))
        o = os[0].reshape(t, Q_W)
        y = pl.pallas_call(
            body1, name=f"out_proj_{tag}", grid=(t // ts,), in_specs=[row, row, _resident(w.shape)], out_specs=row,
            out_shape=jax.ShapeDtypeStruct((t, d), F32), compiler_params=_params("parallel"),
        )(x, o, w)
        return y, o, [lses[0]]

    def body(*refs):
        x_ref, w_ref, e_ref = refs[:3]
        o_refs, l_refs = refs[3:3 + ng], refs[3 + ng:3 + 2 * ng]
        y_ref, om_ref = refs[3 + 2 * ng:5 + 2 * ng]
        lt_refs = refs[5 + 2 * ng:5 + 3 * ng]
        wide_ref, narrow_ref = refs[5 + 3 * ng:]
        ls = [_merge_rows([l_refs[g][r] for r in range(dils[g])], narrow_ref, dils[g]) for g in range(ng)]
        mx = functools.reduce(jnp.maximum, ls)
        tot = mx + jnp.log(functools.reduce(lambda a, b: a + b, [jnp.exp(l_ - mx) for l_ in ls]))
        e = e_ref[...]
        o = None
        for g in range(ng):
            wt = jnp.exp(ls[g] - tot)
            hi = wt.astype(BF16)
            lo = (wt - hi.astype(F32)).astype(BF16)
            wide = jnp.dot(hi, e, preferred_element_type=F32) + jnp.dot(lo, e, preferred_element_type=F32)
            term = wide * _merge_rows([o_refs[g][r].astype(F32) for r in range(dils[g])], wide_ref, dils[g])
            o = term if o is None else o + term
        ob = o.astype(BF16)
        om_ref[...] = ob
        y_ref[...] = x_ref[...] + jnp.dot(ob, w_ref[...], preferred_element_type=F32)
        for g in range(ng):
            for r, part in enumerate(_split_rows(tot, narrow_ref, dils[g])):
                lt_refs[g][r] = part

    row = pl.BlockSpec((ts, d), lambda i: (i, 0))
    e = _head_expand()
    outs = pl.pallas_call(
        body, name=f"out_proj_{tag}", grid=(t // ts,),
        in_specs=[row, _resident(w.shape), _resident(e.shape)] + [_res_spec(seq, dl, Q_W) for dl in dils]
                 + [_res_spec(seq, dl, LANES) for dl in dils],
        out_specs=[row, pl.BlockSpec((ts, Q_W), lambda i: (i, 0))] + [_res_spec(seq, dl, LANES) for dl in dils],
        out_shape=[jax.ShapeDtypeStruct((t, d), F32), jax.ShapeDtypeStruct((t, Q_W), BF16)]
                  + [jax.ShapeDtypeStruct(_res_shape(bl, seq, dl, LANES), F32) for dl in dils],
        scratch_shapes=[_stage(ts, Q_W), _stage(ts, LANES)],
        compiler_params=_params("parallel"),
    )(x, w, e, *os, *lses)
    return outs[0], outs[1], list(outs[2:])


def _sigmoid(g):
    return 1.0 / (1.0 + jnp.exp(-g))


def _ffn_fwd(x, gain, wg, wu, wd, tag, exchange=()):
    t, d = x.shape
    f = wg.shape[1]
    tm = min(256, t)

    def body(x_ref, gain_ref, wg_ref, wu_ref, wd_ref, y_ref, g_ref, u_ref):
        xv = x_ref[...]
        h = (xv * _rms(xv) * gain_ref[...]).astype(BF16)
        g = jnp.dot(h, wg_ref[...], preferred_element_type=F32)
        u = jnp.dot(h, wu_ref[...], preferred_element_type=F32)
        g_ref[...] = g.astype(BF16)
        u_ref[...] = u.astype(BF16)
        a = (g * _sigmoid(g) * u).astype(BF16)
        y_ref[...] = xv + jnp.dot(a, wd_ref[...], preferred_element_type=F32)

    row = pl.BlockSpec((tm, d), lambda i: (i, 0))
    wide = pl.BlockSpec((tm, f), lambda i: (i, 0))
    outs, exchanged = _hosted_call(
        body, exchange, name=f"ffn_fwd_{tag}", grid=(t // tm,),
        in_specs=[row, _resident((1, d)), _resident(wg.shape), _resident(wu.shape), _resident(wd.shape)],
        out_specs=[row, wide, wide],
        out_shape=[jax.ShapeDtypeStruct((t, d), F32), jax.ShapeDtypeStruct((t, f), BF16), jax.ShapeDtypeStruct((t, f), BF16)],
        scratch_shapes=[], semantics=("parallel",), args=(x, gain, wg, wu, wd))
    return (*outs, exchanged)


def _loss_bwd(x, gain, target):
    t, d = x.shape
    tm = min(512, t)

    def body(x_ref, gain_ref, t_ref, dx_ref, loss_ref, dgain_ref):
        xv, gain_v = x_ref[...], gain_ref[...]
        xhat = xv * _rms(xv)
        err = xhat * gain_v - t_ref[...]
        dy = err * (1.0 / d)
        dx, _ = _rms_bwd(dy, xv, gain_v)
        dx_ref[...] = dx
        first = pl.program_id(0) == 0
        part = 0.5 * jnp.sum(jnp.mean(err * err, axis=-1, keepdims=True), axis=0, keepdims=True)
        _accumulate(loss_ref, jnp.broadcast_to(part, loss_ref.shape), first)
        _accumulate(dgain_ref, jnp.sum(dy * xhat, axis=0, keepdims=True), first)

    row = pl.BlockSpec((tm, d), lambda i: (i, 0))
    return pl.pallas_call(
        body, name="loss_bwd", grid=(t // tm,), in_specs=[row, _resident((1, d)), row],
        out_specs=[row, pl.BlockSpec((1, LANES), lambda i: (0, 0)), pl.BlockSpec((1, d), lambda i: (0, 0))],
        out_shape=[jax.ShapeDtypeStruct((t, d), F32), jax.ShapeDtypeStruct((1, LANES), F32), jax.ShapeDtypeStruct((1, d), F32)],
        compiler_params=_params("arbitrary"),
    )(x, gain, target)


def _ffn_bwd(dy, x, gain, g, u, wg, wu, wd, tag, exchange=()):
    t, d = x.shape
    f = wg.shape[1]
    tm = min(256, t)

    def body(dy_ref, x_ref, gain_ref, g_ref, u_ref, wg_ref, wu_ref, wd_ref, dx_ref, dg_ref, du_ref, a_ref, h_ref, dgain_ref):
        dyv = dy_ref[...]
        da = lax.dot_general(dyv.astype(BF16), wd_ref[...], NT, preferred_element_type=F32)
        gv, uv = g_ref[...].astype(F32), u_ref[...].astype(F32)
        sg = _sigmoid(gv)
        act = gv * sg
        a_ref[...] = (act * uv).astype(BF16)
        du = (da * act).astype(BF16)
        dg = (da * uv * (sg * (1.0 + gv * (1.0 - sg)))).astype(BF16)
        du_ref[...] = du
        dg_ref[...] = dg
        dh = (lax.dot_general(dg, wg_ref[...], NT, preferred_element_type=F32)
              + lax.dot_general(du, wu_ref[...], NT, preferred_element_type=F32))
        xv, gain_v = x_ref[...], gain_ref[...]
        dx, xhat = _rms_bwd(dh, xv, gain_v)
        dx_ref[...] = dyv + dx
        h_ref[...] = (xhat * gain_v).astype(BF16)
        _accumulate(dgain_ref, jnp.sum(dh * xhat, axis=0, keepdims=True), pl.program_id(0) == 0)

    row = pl.BlockSpec((tm, d), lambda i: (i, 0))
    wide = pl.BlockSpec((tm, f), lambda i: (i, 0))
    outs, exchanged = _hosted_call(
        body, exchange, name=f"ffn_bwd_{tag}", grid=(t // tm,),
        in_specs=[row, row, _resident((1, d)), wide, wide, _resident(wg.shape), _resident(wu.shape), _resident(wd.shape)],
        out_specs=[row, wide, wide, wide, row, pl.BlockSpec((1, d), lambda i: (0, 0))],
        out_shape=[jax.ShapeDtypeStruct((t, d), F32), jax.ShapeDtypeStruct((t, f), BF16), jax.ShapeDtypeStruct((t, f), BF16),
                   jax.ShapeDtypeStruct((t, f), BF16), jax.ShapeDtypeStruct((t, d), BF16), jax.ShapeDtypeStruct((1, d), F32)],
        scratch_shapes=[], semantics=("arbitrary",), args=(dy, x, gain, g, u, wg, wu, wd))
    return (*outs, exchanged)


def _tn_matmul(a, bs, name):
    t, k = a.shape
    tk = k // 2 if (k // 2) % LANES == 0 else k
    tt = min(2048, t)
    nb = len(bs)

    def body(a_ref, *refs):
        at = a_ref[...].astype(BF16)
        for b_ref, o_ref in zip(refs[:nb], refs[nb:]):
            prod = lax.dot_general(at, b_ref[...].astype(BF16), TN, preferred_element_type=F32)

            @pl.when(pl.program_id(1) == 0)
            def _():
                o_ref[...] = prod

            @pl.when(pl.program_id(1) > 0)
            def _():
                o_ref[...] += prod

    return pl.pallas_call(
        body, name=name, grid=(k // tk, t // tt),
        in_specs=[pl.BlockSpec((tt, tk), lambda i, j: (j, i))] + [pl.BlockSpec((tt, b.shape[1]), lambda i, j: (j, 0)) for b in bs],
        out_specs=[pl.BlockSpec((tk, b.shape[1]), lambda i, j: (i, 0)) for b in bs],
        out_shape=[jax.ShapeDtypeStruct((k, b.shape[1]), F32) for b in bs],
        compiler_params=_params("parallel", "arbitrary"),
    )(a, *bs)


def _attn_out_bwd(dx, w, o, dils, seq, tag, lse=None, sink=None):
    t, d = dx.shape
    ts = _tile_rows(seq)
    bl = t // seq
    ng = len(dils)
    has_sink = sink is not None
    expand = _head_expand().T

    def body(*refs):
        refs = list(refs)
        dx_ref, w_ref, o_ref, e_ref = refs[:4]
        refs = refs[4:]
        lse_ref, sink_ref = (refs.pop(0), refs.pop(0)) if has_sink else (None, None)
        do_refs, dl_refs = refs[:ng], refs[ng:2 * ng]
        refs = refs[2 * ng:]
        dsink_ref = refs.pop(0) if has_sink else None
        dof_ref, dlf_ref = refs
        do = lax.dot_general(dx_ref[...].astype(BF16), w_ref[...], NT, preferred_element_type=F32)
        prod = do * o_ref[...].astype(F32)
        hi = prod.astype(BF16)
        lo = (prod - hi.astype(F32)).astype(BF16)
        e = e_ref[...]
        dl = jnp.dot(hi, e, preferred_element_type=F32) + jnp.dot(lo, e, preferred_element_type=F32)
        for g in range(ng):
            for r, part in enumerate(_split_rows(do, dof_ref, dils[g])):
                do_refs[g][r] = part.astype(BF16)
            for r, part in enumerate(_split_rows(dl, dlf_ref, dils[g])):
                dl_refs[g][r] = part
        if has_sink:
            part = -jnp.exp(sink_ref[...] - lse_ref[...]) * dl
            _accumulate(dsink_ref, jnp.sum(part, axis=0, keepdims=True), pl.program_id(0) == 0)

    row = pl.BlockSpec((ts, d), lambda i: (i, 0))
    narrow = pl.BlockSpec((ts, LANES), lambda i: (i, 0))
    args = [dx, w, o, expand]
    in_specs = [row, _resident(w.shape), pl.BlockSpec((ts, Q_W), lambda i: (i, 0)), _resident(expand.shape)]
    if has_sink:
        args += [lse, jnp.pad(sink.reshape(1, N_HEADS), ((0, 0), (0, LANES - N_HEADS)))]
        in_specs += [narrow, _resident((1, LANES))]
    out_specs = [_res_spec(seq, dl, Q_W) for dl in dils] + [_res_spec(seq, dl, LANES) for dl in dils]
    out_shape = ([jax.ShapeDtypeStruct(_res_shape(bl, seq, dl, Q_W), BF16) for dl in dils]
                 + [jax.ShapeDtypeStruct(_res_shape(bl, seq, dl, LANES), F32) for dl in dils])
    if has_sink:
        out_specs.append(pl.BlockSpec((1, LANES), lambda i: (0, 0)))
        out_shape.append(jax.ShapeDtypeStruct((1, LANES), F32))
    outs = pl.pallas_call(
        body, name=f"attn_out_bwd_{tag}", grid=(t // ts,), in_specs=in_specs, out_specs=out_specs, out_shape=out_shape,
        scratch_shapes=[_stage(ts, Q_W), _stage(ts, LANES)],
        compiler_params=_params("arbitrary" if has_sink else "parallel"),
    )(*args)
    return list(outs[:ng]), list(outs[ng:2 * ng]), (outs[2 * ng] if has_sink else None)


def _attn_bwd(qkv, do, lse, delta, cos, sin, w, tag, exchange=()):
    shape = qkv.shape
    dil = shape[1]
    rows_all = _seq_view(qkv)
    nseq, length, _ = rows_all.shape
    bq = min(QUERY_BLOCK, length)
    wk = min(3 * bq, length)
    nb = length // bq

    def body(qkv_ref, do_ref, lse_ref, dl_ref, cos_ref, sin_ref, dp_ref, kk_ref, vv_ref, dk_ref, dv_ref):
        _pair_variants(qkv_ref, Q_W, kk_ref)
        _pair_variants(qkv_ref, Q_W + KV_W, vv_ref)
        dk_ref[...] = jnp.zeros_like(dk_ref)
        dv_ref[...] = jnp.zeros_like(dv_ref)
        band = _band(bq, wk)
        lo_q = lax.broadcasted_iota(jnp.int32, (bq, LANES), 1) < HEAD_DIM
        hi_q = jnp.logical_not(lo_q)

        def block(i, carry):
            q0, k0 = _window(i, bq, wk, length)
            valid = jnp.abs(band + (q0 - k0)) <= w
            rows, krows = pl.ds(q0, bq), pl.ds(k0, wk)
            c, sn = cos_ref[rows, :], -sin_ref[rows, :]
            lse_t, dl_t = lse_ref[rows, :], dl_ref[rows, :]
            for kv in range(N_KV):
                heads = [(kv * GRP + h, h % 2) for h in range(GRP)]
                cols = [slice((kv * 2 + j) * LANES, (kv * 2 + j + 1) * LANES) for j in range(GRP // 2)]
                qp = [qkv_ref[rows, cs] for cs in cols]
                dop = [do_ref[rows, cs] for cs in cols]
                k2 = jnp.concatenate([kk_ref[kv, 0, krows, :], kk_ref[kv, 1, krows, :]], axis=0)
                v2 = jnp.concatenate([vv_ref[kv, 0, krows, :], vv_ref[kv, 1, krows, :]], axis=0)
                sc2 = [lax.dot_general(q_, k2, NT, preferred_element_type=F32) for q_ in qp]
                dp2 = [lax.dot_general(d_, v2, NT, preferred_element_type=F32) for d_ in dop]
                sc = [s_[:, half * wk:(half + 1) * wk] for s_ in sc2 for half in range(2)]
                dp = [d_[:, half * wk:(half + 1) * wk] for d_ in dp2 for half in range(2)]
                p = [jnp.exp(jnp.where(valid, s_, NEG_INF) - _over_keys(lse_t[:, hd:hd + 1], wk))
                     for s_, (hd, _) in zip(sc, heads)]
                ds = [(p_ * (dp_ - _over_keys(dl_t[:, hd:hd + 1], wk))).astype(BF16) for p_, dp_, (hd, _) in zip(p, dp, heads)]
                pb = [p_.astype(BF16) for p_ in p]
                for j in range(GRP // 2):
                    dq = jnp.dot(jnp.concatenate([ds[2 * j], ds[2 * j + 1]], axis=1), k2, preferred_element_type=F32) * SCALE
                    dp_ref[rows, cols[j]] = _rope(dq, c, sn).astype(BF16)
                zero = jnp.zeros((bq, LANES), BF16)
                q4 = jnp.concatenate([jnp.where(lo_q if h % 2 == 0 else hi_q, qp[h // 2], zero) for h in range(GRP)], axis=0)
                do4 = jnp.concatenate([jnp.where(lo_q if h % 2 == 0 else hi_q, dop[h // 2], zero) for h in range(GRP)], axis=0)
                dk_ref[kv, krows, :] += lax.dot_general(jnp.concatenate(ds, axis=0), q4, TN, preferred_element_type=F32)
                dv_ref[kv, krows, :] += lax.dot_general(jnp.concatenate(pb, axis=0), do4, TN, preferred_element_type=F32)
            return carry

        lax.fori_loop(0, nb, block, 0)
        lo = lax.broadcasted_iota(jnp.int32, (length, LANES), 1) < HEAD_DIM
        c, sn = cos_ref[...], -sin_ref[...]
        for ch in range(KV_W // LANES):
            halves = []
            for acc_ref in (dk_ref, dv_ref):
                even, odd = acc_ref[2 * ch], acc_ref[2 * ch + 1]
                even = even + pltpu.roll(even, HEAD_DIM, 1)
                odd = odd + pltpu.roll(odd, HEAD_DIM, 1)
                halves.append(jnp.where(lo, even, odd))
            dp_ref[:, Q_W + ch * LANES:Q_W + (ch + 1) * LANES] = _rope(halves[0], c, sn).astype(BF16)
            dp_ref[:, Q_W + KV_W + ch * LANES:Q_W + KV_W + (ch + 1) * LANES] = halves[1].astype(BF16)

    mode = dict(pipeline_mode=pl.Buffered(1)) if dil == 1 else {}

    def seq_block(c):
        return pl.BlockSpec((None, length, c), lambda i: (i, 0, 0), **mode)

    table = pl.BlockSpec((None, length, LANES), lambda i: (i % dil, 0, 0), **mode)
    (out,), exchanged = _hosted_call(
        body, exchange, name=f"attn_bwd_{tag}", grid=(nseq,),
        in_specs=[seq_block(QKV_W), seq_block(Q_W), seq_block(LANES), seq_block(LANES), table, table],
        out_specs=[pl.BlockSpec((None, length, QKV_W), lambda i: (i, 0, 0))],
        out_shape=[jax.ShapeDtypeStruct((nseq, length, QKV_W), BF16)],
        scratch_shapes=[pltpu.VMEM((N_KV, 2, length, LANES), BF16), pltpu.VMEM((N_KV, 2, length, LANES), BF16),
                        pltpu.VMEM((N_KV, length, LANES), F32), pltpu.VMEM((N_KV, length, LANES), F32)],
        semantics=("parallel",), args=(rows_all, _seq_view(do), _seq_view(lse), _seq_view(delta), cos, sin))
    return out.reshape(shape), exchanged


def _qkv_bwd(dy, x, gain, w, dps, dils, seq, tag, exchange=()):
    t, d = x.shape
    ts = _tile_rows(seq)
    bl = t // seq
    ng = len(dps)

    def body(dy_ref, x_ref, gain_ref, w_ref, *refs):
        dp_refs, dx_ref = refs[:ng], refs[ng]
        h_refs = refs[ng + 1:2 * ng + 1]
        dgain_ref, stage_ref = refs[2 * ng + 1:]
        dh = None
        for gi in range(ng):
            dil = dils[gi]
            n = ts // dil
            dp = dp_refs[gi][0] if dil == 1 else jnp.concatenate([dp_refs[gi][r] for r in range(dil)], axis=0)
            part = lax.dot_general(dp, w_ref[:, gi * QKV_W:(gi + 1) * QKV_W], NT, preferred_element_type=F32)
            part = _merge_rows([part[r * n:(r + 1) * n] for r in range(dil)], stage_ref, dil)
            dh = part if dh is None else dh + part
        xv, gain_v = x_ref[...], gain_ref[...]
        dx, xhat = _rms_bwd(dh, xv, gain_v)
        dx_ref[...] = dy_ref[...] + dx
        h = xhat * gain_v
        for gi in range(ng):
            for r, part in enumerate(_split_rows(h, stage_ref, dils[gi])):
                h_refs[gi][r] = part.astype(BF16)
        _accumulate(dgain_ref, jnp.sum(dh * xhat, axis=0, keepdims=True), pl.program_id(0) == 0)

    row = pl.BlockSpec((ts, d), lambda i: (i, 0))
    outs, exchanged = _hosted_call(
        body, exchange, name=f"qkv_bwd_{tag}", grid=(t // ts,),
        in_specs=[row, row, _resident((1, d)), _resident(w.shape)] + [_res_spec(seq, dl, QKV_W) for dl in dils],
        out_specs=[row] + [_res_spec(seq, dl, d) for dl in dils] + [pl.BlockSpec((1, d), lambda i: (0, 0))],
        out_shape=[jax.ShapeDtypeStruct((t, d), F32)] + [jax.ShapeDtypeStruct(_res_shape(bl, seq, dl, d), BF16) for dl in dils]
                  + [jax.ShapeDtypeStruct((1, d), F32)],
        scratch_shapes=[_stage(ts, d)], semantics=("arbitrary",), args=(dy, x, gain, w, *dps))
    return outs[0], list(outs[1:1 + ng]), outs[1 + ng], exchanged


ANY = pl.BlockSpec(memory_space=pl.ANY)


def _place():
    x, y, c = lax.axis_index("x"), lax.axis_index("y"), lax.axis_index("c")
    return x, y, c


def _all_gather(shards):
    n = len(shards)

    def body(*refs):
        ins, outs = refs[:n], refs[n:2 * n]
        send_sems, recv_sems, local_sems = refs[2 * n:]
        x, y, c = _place()
        sibling = (x, y, 1 - c)
        chips = [(1 - x, y), (x, 1 - y), (1 - x, 1 - y)]

        def copy(a, k, block, to, src=None):
            px, py, pc = block
            rows = outs[a].at[4 * px + 2 * py + pc]
            return pltpu.make_async_remote_copy(
                src_ref=rows if src is None else src, dst_ref=rows, send_sem=send_sems.at[a, k], recv_sem=recv_sems.at[a, k],
                device_id=to, device_id_type=MESH)

        sent = []
        for a in range(n):
            mine = pltpu.make_async_copy(ins[a], outs[a].at[4 * x + 2 * y + c], local_sems.at[a])
            mine.start()
            sent.append(mine)
        for a in range(n):
            first = [copy(a, 0, (x, y, c), sibling, src=ins[a])]
            first += [copy(a, 1 + j, (x, y, c), (*chip, c), src=ins[a]) for j, chip in enumerate(chips)]
            for cp in first:
                cp.start()
            sent += first
        for a in range(n):
            for j, chip in enumerate(chips):
                copy(a, 1 + j, (*chip, c), (x, y, c)).wait_recv()
                passed = copy(a, 4 + j, (*chip, c), sibling)
                passed.start()
                sent.append(passed)
        for a in range(n):
            copy(a, 0, sibling, (x, y, c)).wait_recv()
            for j, chip in enumerate(chips):
                copy(a, 4 + j, (*chip, 1 - c), (x, y, c)).wait_recv()
        for cp in sent[n:]:
            cp.wait_send()
        for mine in sent[:n]:
            mine.wait()

    return pl.pallas_call(
        body, name="all_gather_weights", in_specs=[ANY] * n, out_specs=[ANY] * n,
        out_shape=[jax.ShapeDtypeStruct((N_DEV,) + s.shape, s.dtype) for s in shards],
        scratch_shapes=[pltpu.SemaphoreType.DMA((n, 7)), pltpu.SemaphoreType.DMA((n, 7)), pltpu.SemaphoreType.DMA((n,))],
    )(*shards)


def _exchange_copies(srcs, dsts, gather, send_sems, recv_sems, local_sems):
    x, y, c = _place()
    me = 4 * x + 2 * y + c
    copies = []
    for a, (src, dst) in enumerate(zip(srcs, dsts)):
        copies.append(pltpu.make_async_copy(src if gather[a] else src.at[me], dst.at[me], local_sems.at[a]))
        for k in range(1, N_DEV):
            peer = me ^ k
            copies.append(pltpu.make_async_remote_copy(
                src_ref=src if gather[a] else src.at[peer], dst_ref=dst.at[me], send_sem=send_sems.at[a, k - 1],
                recv_sem=recv_sems.at[a, k - 1], device_id=(peer // 4, (peer // 2) % 2, peer % 2), device_id_type=MESH))
    return copies


def _exchange_scratch(n):
    return [pltpu.SemaphoreType.DMA((n, N_DEV - 1)), pltpu.SemaphoreType.DMA((n, N_DEV - 1)), pltpu.SemaphoreType.DMA((n,))]


def _exchanged_shapes(exchange):
    return [jax.ShapeDtypeStruct(((N_DEV,) + a.shape) if g else a.shape, a.dtype) for a, g in exchange]


def _hosted_call(body, exchange, *, name, grid, in_specs, out_specs, out_shape, scratch_shapes, semantics, args):
    out_specs, out_shape, scratch = list(out_specs), list(out_shape), list(scratch_shapes)
    if not exchange:
        outs = pl.pallas_call(body, name=name, grid=grid, in_specs=in_specs, out_specs=out_specs, out_shape=out_shape,
                              scratch_shapes=scratch, compiler_params=_params(*semantics))(*args)
        return list(outs), []
    n, n_in, n_out, n_scr = len(exchange), len(in_specs), len(out_specs), len(scratch)
    gather = [g for _, g in exchange]

    def hosted(*refs):
        own_in, x_in = refs[:n_in], refs[n_in:n_in + n]
        own_out, x_out = refs[n_in + n:n_in + n + n_out], refs[n_in + n + n_out:n_in + 2 * n + n_out]
        own_scr, sems = refs[n_in + 2 * n + n_out:n_in + 2 * n + n_out + n_scr], refs[-3:]

        @pl.when(pl.program_id(0) == 0)
        def _():
            for cp in _exchange_copies(x_in, x_out, gather, *sems):
                cp.start()

        body(*own_in, *own_out, *own_scr)

        @pl.when(pl.program_id(0) == pl.num_programs(0) - 1)
        def _():
            for cp in _exchange_copies(x_in, x_out, gather, *sems):
                cp.wait()

    outs = pl.pallas_call(
        hosted, name=name, grid=grid, in_specs=list(in_specs) + [ANY] * n, out_specs=out_specs + [ANY] * n,
        out_shape=out_shape + _exchanged_shapes(exchange), scratch_shapes=scratch + _exchange_scratch(n),
        compiler_params=_params("arbitrary"),
    )(*args, *[a for a, _ in exchange])
    return list(outs[:n_out]), list(outs[n_out:])


def _exchange_now(exchange):
    n = len(exchange)
    gather = [g for _, g in exchange]

    def body(*refs):
        copies = _exchange_copies(refs[:n], refs[n:2 * n], gather, *refs[2 * n:])
        for cp in copies:
            cp.start()
        for cp in copies:
            cp.wait()

    return pl.pallas_call(
        body, name="exchange_last", in_specs=[ANY] * n, out_specs=[ANY] * n, out_shape=_exchanged_shapes(exchange),
        scratch_shapes=_exchange_scratch(n),
    )(*[a for a, _ in exchange])


def _all_reduce_small(v):
    def body(v_ref, o_ref, recv_ref, send_sems, recv_sems):
        x, y, c = _place()
        me = 4 * x + 2 * y + c
        copies = []
        for k in range(1, N_DEV):
            peer = me ^ k
            copies.append(pltpu.make_async_remote_copy(
                src_ref=v_ref, dst_ref=recv_ref.at[k], send_sem=send_sems.at[k - 1], recv_sem=recv_sems.at[k - 1],
                device_id=(peer // 4, (peer // 2) % 2, peer % 2), device_id_type=MESH))
        for cp in copies:
            cp.start()
        recv_ref[0] = v_ref[...]
        for cp in copies:
            cp.wait()
        acc = recv_ref[me]
        for src in range(1, N_DEV):
            acc = acc + recv_ref[me ^ src]
        o_ref[...] = acc

    vm = pl.BlockSpec(memory_space=pltpu.VMEM)
    return pl.pallas_call(
        body, name="all_reduce_small", in_specs=[vm], out_specs=vm, out_shape=jax.ShapeDtypeStruct(v.shape, F32),
        scratch_shapes=[pltpu.VMEM((N_DEV,) + v.shape, F32), pltpu.SemaphoreType.DMA((N_DEV - 1,)),
                        pltpu.SemaphoreType.DMA((N_DEV - 1,))],
    )(v)


def _adamw_math(w, g, m, v):
    m = ADAM_B1 * m + (1.0 - ADAM_B1) * g
    v = ADAM_B2 * v + (1.0 - ADAM_B2) * (g * g)
    m_hat = m / (1.0 - ADAM_B1 ** ADAM_STEP)
    v_hat = v / (1.0 - ADAM_B2 ** ADAM_STEP)
    delta = -ADAM_LR * (m_hat / (jnp.sqrt(v_hat) + ADAM_EPS) + ADAM_WD * w)
    return delta, m, v


def _adamw(parts, w, m, v, name):
    r, c = w.shape
    tr = r // 2 if r % 16 == 0 and r >= 256 else r
    n = len(parts)

    def body(*refs):
        w_ref, m_ref, v_ref, g_ref, d_ref, nm_ref, nv_ref = refs[n:]
        g = refs[0][...].astype(F32)
        for p_ref in refs[1:n]:
            g = g + p_ref[...].astype(F32)
        g_ref[...] = g
        d_ref[...], nm_ref[...], nv_ref[...] = _adamw_math(w_ref[...], g, m_ref[...], v_ref[...])

    tile = pl.BlockSpec((tr, c), lambda i: (i, 0))
    arrays, in_specs = [], []
    for p in parts:
        if isinstance(p, tuple):
            arrays.append(p[0])
            in_specs.append(pl.BlockSpec((None, tr, c), functools.partial(lambda i, slot: (slot, i, 0), slot=p[1])))
        else:
            arrays.append(p)
            in_specs.append(tile)
    return pl.pallas_call(
        body, name=name, grid=(r // tr,), in_specs=in_specs + [tile] * 3, out_specs=[tile] * 4,
        out_shape=[jax.ShapeDtypeStruct((r, c), F32)] * 4, compiler_params=_params("parallel"),
    )(*arrays, w, m, v)


def _columns(g):
    return g.transpose(1, 0, 2).reshape(g.shape[1], -1)


def _rows(g):
    return g.reshape(-1, g.shape[-1])


def _column_blocks(dw):
    k, n = dw.shape
    return dw.reshape(k, N_DEV, n // N_DEV).transpose(1, 0, 2)


def _row_blocks(dw):
    k, n = dw.shape
    return dw.reshape(N_DEV, k // N_DEV, n)


def _pack_rows(rows, width):
    out = None
    for i, r in enumerate(rows):
        r = r.reshape(1, -1).astype(F32)
        r = jnp.pad(r, ((i, 8 - 1 - i), (0, width - r.shape[1])))
        out = r if out is None else out + r
    return out


def _mixer_fwd(x, gain, w_in, w_out, cos, sin, seq, groups, tag, sink=None, exchanges=None):
    qkvs, os, lses, got = [], [], [], {}
    for gi, (dil, w) in enumerate(groups):
        qkv, got["proj", gi] = _qkv_proj(x, gain, w_in, _tables_tiled(cos, seq, dil), _tables_tiled(sin, seq, dil), seq, dil, gi,
                                         f"{tag}{gi}", exchange=(exchanges or {}).get(("proj", gi), ()))
        o, lse, got[gi] = _attn_fwd(qkv, w, f"{tag}{gi}", sink=sink, exchange=(exchanges or {}).get(gi, ()))
        qkvs.append(qkv)
        os.append(o)
        lses.append(lse)
    y, o, lses = _out_proj(x, os, lses, [dl for dl, _ in groups], w_out, seq, tag)
    return y, (qkvs, o, lses), got


def _mixer_bwd(dy, x_in, gain, w_in, w_out, saved, cos, sin, seq, groups, tag, sink=None, exchange=(), scatter_dw_out=False):
    qkvs, o, lses = saved
    t, d = x_in.shape
    dils = [dl for dl, _ in groups]
    lse_tokens = lses[0].reshape(t, LANES) if sink is not None else None
    dos, dls, dsink = _attn_out_bwd(dy, w_out, o, dils, seq, tag, lse=lse_tokens, sink=sink)
    (dw_out,) = _tn_matmul(o, [dy], f"dw_out_{tag}")
    dps, got = [], []
    for gi, (dil, w) in enumerate(groups):
        dp, brought = _attn_bwd(qkvs[gi], dos[gi], lses[gi], dls[gi], _tables_by_residue(cos, seq, dil),
                                _tables_by_residue(sin, seq, dil), w, f"{tag}{gi}", exchange=exchange if gi == 0 else ())
        dps.append(dp)
        got += brought
    dx, hs, dgain, late = _qkv_bwd(dy, x_in, gain, w_in, dps, dils, seq, tag,
                                   exchange=_to_send([], [dw_out]) if scatter_dw_out else ())
    if scatter_dw_out:
        (dw_out,) = late
    dw_in = [_tn_matmul(hs[gi].reshape(t, d), [dps[gi].reshape(t, QKV_W)], f"dw_in_{tag}{gi}")[0] for gi in range(len(groups))]
    dw_in = dw_in[0] if len(dw_in) == 1 else jnp.concatenate(dw_in, axis=1)
    return dx, dw_in, dw_out, dgain, dsink, got


def _ffn_layer_bwd(dy, x_in, gain, g, u, wg, wu, wd, tag, exchange=()):
    dx, dg, du, act, h, dgain, got = _ffn_bwd(dy, x_in, gain, g, u, wg, wu, wd, tag, exchange=exchange)
    (dwd,) = _tn_matmul(act, [dy], f"dw_down_{tag}")
    (dwg,) = _tn_matmul(h, [dg], f"dw_gate_{tag}")
    (dwu,) = _tn_matmul(h, [du], f"dw_up_{tag}")
    return dx, dwg, dwu, dwd, dgain, got


def _to_send(dws_by_columns, dws_by_rows):
    return ([(_column_blocks(g).astype(BF16), False) for g in dws_by_columns]
            + [(_row_blocks(g).astype(BF16), False) for g in dws_by_rows])


def kernel(x, a_w_in, a_sink, a_w_out, b_w_in, b_w_out, norm_mix, norm_ffn, w_gate, w_up, w_down, final_norm, loss_target, m_a_w_in, m_a_sink, m_a_w_out, m_b_w_in, m_b_w_out, m_norm_mix, m_norm_ffn, m_w_gate, m_w_up, m_w_down, m_final_norm, v_a_w_in, v_a_sink, v_a_w_out, v_b_w_in, v_b_w_out, v_norm_mix, v_norm_ffn, v_w_gate, v_w_up, v_w_down, v_final_norm):
    bl, seq, d = x.shape
    t = bl * seq
    xf = x.reshape(t, d)
    target = loss_target.reshape(t, d)
    cos, sin = _rope_tables(seq)
    groups_a = [(1, ATTN_HALF_WINDOW)]
    groups_b = [(dil, window // 2 // dil) for window, dil in DILATED_GROUPS]

    def shard(w_):
        return w_.astype(BF16)

    wa_in, wa_out = _all_gather([shard(a_w_in[0]), shard(a_w_out[0])])
    wa_in, wa_out = _columns(wa_in), _rows(wa_out)

    x1_0, saved_a, got = _mixer_fwd(xf, norm_mix[0:1], wa_in, wa_out, cos, sin, seq, groups_a, "a", sink=a_sink[0],
                                    exchanges={("proj", 0): [(shard(w_down[0]), True)],
                                               0: [(shard(w_gate[0]), True), (shard(w_up[0]), True)]})
    wg0, wu0, wd0 = _columns(got[0][0]), _columns(got[0][1]), _rows(got["proj", 0][0])
    x2_0, g0, u0, got = _ffn_fwd(x1_0, norm_ffn[0:1], wg0, wu0, wd0, "0",
                                 exchange=[(shard(b_w_in[0]), True), (shard(b_w_out[0]), True)])
    wb_in, wb_out = _columns(got[0]), _rows(got[1])
    x1_1, saved_b, got = _mixer_fwd(x2_0, norm_mix[1:2], wb_in, wb_out, cos, sin, seq, groups_b, "b",
                                    exchanges={0: [(shard(w_gate[1]), True), (shard(w_up[1]), True)], 1: [(shard(w_down[1]), True)]})
    wg1, wu1, wd1 = _columns(got[0][0]), _columns(got[0][1]), _rows(got[1][0])
    x2_1, g1, u1, _ = _ffn_fwd(x1_1, norm_ffn[1:2], wg1, wu1, wd1, "1")

    dy, loss_part, d_final = _loss_bwd(x2_1, final_norm.reshape(1, d), target)
    dy, dwg1, dwu1, dwd1, d_nf1, _ = _ffn_layer_bwd(dy, x1_1, norm_ffn[1:2], g1, u1, wg1, wu1, wd1, "1")
    dy, dwb_in, dwb_out, d_nm1, _, (r_g1, r_u1, r_d1) = _mixer_bwd(
        dy, x2_0, norm_mix[1:2], wb_in, wb_out, saved_b, cos, sin, seq, groups_b, "b", exchange=_to_send([dwg1, dwu1], [dwd1]))
    dy, dwg0, dwu0, dwd0, d_nf0, (r_b_in, r_b_out) = _ffn_layer_bwd(
        dy, x1_0, norm_ffn[0:1], g0, u0, wg0, wu0, wd0, "0", exchange=_to_send([dwb_in], [dwb_out]))
    dy, dwa_in, r_a_out, d_nm0, d_sink, (r_g0, r_u0, r_d0) = _mixer_bwd(
        dy, xf, norm_mix[0:1], wa_in, wa_out, saved_a, cos, sin, seq, groups_a, "a", sink=a_sink[0],
        exchange=_to_send([dwg0, dwu0], [dwd0]), scatter_dw_out=True)
    (r_a_in,) = _exchange_now(_to_send([dwa_in], []))
    grad_x = dy.reshape(bl, seq, d)

    def shard2d(a):
        return a.reshape(-1, a.shape[-1])

    big = [(r_a_in, a_w_in, m_a_w_in, v_a_w_in), (r_b_in, b_w_in, m_b_w_in, v_b_w_in),
           (r_g0, w_gate[0], m_w_gate[0], v_w_gate[0]), (r_g1, w_gate[1], m_w_gate[1], v_w_gate[1]),
           (r_u0, w_up[0], m_w_up[0], v_w_up[0]), (r_u1, w_up[1], m_w_up[1], v_w_up[1]),
           (r_a_out, a_w_out, m_a_w_out, v_a_w_out), (r_b_out, b_w_out, m_b_w_out, v_b_w_out),
           (r_d0, w_down[0], m_w_down[0], v_w_down[0]), (r_d1, w_down[1], m_w_down[1], v_w_down[1])]
    upd = [_adamw([(r_, src) for src in range(N_DEV)], shard2d(w_), shard2d(m_), shard2d(v_), f"adamw_{a}")
           for a, (r_, w_, m_, v_) in enumerate(big)]
    (u_a_in, u_b_in, u_g0, u_g1, u_u0, u_u1, u_a_out, u_b_out, u_d0, u_d1) = upd

    small = _pack_rows([d_nm0, d_nm1, d_nf0, d_nf1, d_final, d_sink, loss_part], d)
    total = _all_reduce_small(small)
    small_w = _pack_rows([norm_mix[0], norm_mix[1], norm_ffn[0], norm_ffn[1], final_norm, a_sink], d)
    small_m = _pack_rows([m_norm_mix[0], m_norm_mix[1], m_norm_ffn[0], m_norm_ffn[1], m_final_norm, m_a_sink], d)
    small_v = _pack_rows([v_norm_mix[0], v_norm_mix[1], v_norm_ffn[0], v_norm_ffn[1], v_final_norm, v_a_sink], d)
    u_small = _adamw([total], small_w, small_m, small_v, "adamw_small")
    loss = total[6, 0]

    outs = []
    for k in range(4):
        sm = u_small[k]
        outs += [
            u_a_in[k].reshape(a_w_in.shape), sm[5:6, :N_HEADS], u_a_out[k].reshape(a_w_out.shape),
            u_b_in[k].reshape(b_w_in.shape), u_b_out[k].reshape(b_w_out.shape), sm[0:2], sm[2:4],
            jnp.stack([u_g0[k], u_g1[k]]), jnp.stack([u_u0[k], u_u1[k]]), jnp.stack([u_d0[k], u_d1[k]]), sm[4],
        ]
    return (loss, grad_x, *outs)
```

```python
import functools
import math

import jax
import jax.numpy as jnp
from jax import lax
from jax.experimental import pallas as pl
from jax.experimental.pallas import tpu as pltpu

F32 = jnp.float32
BF16 = jnp.bfloat16

HEAD_DIM = 64
N_HEADS = 16
N_KV = 4
GRP = N_HEADS // N_KV
Q_W = N_HEADS * HEAD_DIM
KV_W = N_KV * HEAD_DIM
QKV_W = Q_W + 2 * KV_W
ATTN_HALF_WINDOW = 128
DILATED_GROUPS = ((128, 1), (512, 4), (2048, 16))
ROPE_THETA = 10000.0
RMS_EPS = 1e-6
NEG_INF = -1e30
SCALE = 1.0 / math.sqrt(HEAD_DIM)

ADAM_LR = 0.001
ADAM_B1 = 0.9
ADAM_B2 = 0.999
ADAM_EPS = 1e-08
ADAM_WD = 0.01
ADAM_STEP = 10

LANES = 128
VMEM_LIMIT = 56 * 1024 * 1024
QUERY_BLOCK = 128
N_DEV = 8
MESH = pl.DeviceIdType.MESH

NT = (((1,), (1,)), ((), ()))
TN = (((0,), (0,)), ((), ()))


def _params(*sem):
    return pltpu.CompilerParams(dimension_semantics=tuple(sem) if sem else None, vmem_limit_bytes=VMEM_LIMIT)


def _resident(shape):
    return pl.BlockSpec(shape, lambda *_: (0,) * len(shape), pipeline_mode=pl.Buffered(1))


def _rope_tables(seq):
    inv_freq = 1.0 / (ROPE_THETA ** (jnp.arange(0, HEAD_DIM, 2, dtype=F32) / HEAD_DIM))
    ang = jnp.arange(seq, dtype=F32)[:, None] * inv_freq[None, :]
    cos, sin = jnp.cos(ang), jnp.sin(ang)
    return jnp.tile(cos, (1, 4)), jnp.concatenate([-sin, sin, -sin, sin], axis=1)


def _rope(t, cos, sin_signed):
    lane = lax.broadcasted_iota(jnp.int32, t.shape, 1)
    first = (lane & (HEAD_DIM // 2)) == 0
    swapped = jnp.where(first, pltpu.roll(t, LANES - HEAD_DIM // 2, 1), pltpu.roll(t, HEAD_DIM // 2, 1))
    return t * cos + swapped * sin_signed


def _rms(x):
    return lax.rsqrt(jnp.mean(x * x, axis=-1, keepdims=True) + RMS_EPS)


def _rms_bwd(dh, x, gain):
    r = _rms(x)
    xhat = x * r
    dxh = dh * gain
    dx = r * (dxh - xhat * jnp.mean(dxh * xhat, axis=-1, keepdims=True))
    return dx, xhat


def _accumulate(ref, value, first):
    @pl.when(first)
    def _():
        ref[...] = jnp.zeros_like(ref)

    ref[...] += value


def _tile_rows(seq):
    return min(512, seq)


def _res_shape(bl, seq, dil, c):
    ts = _tile_rows(seq)
    return (bl, dil, seq // ts, ts // dil, c)


def _res_spec(seq, dil, c):
    ts = _tile_rows(seq)
    per_seq = seq // ts
    return pl.BlockSpec((None, dil, None, ts // dil, c), lambda i: (i // per_seq, 0, i % per_seq, 0, 0))


def _seq_view(a):
    bl, dil, tiles, n, c = a.shape
    return a.reshape(bl * dil, tiles * n, c)


def _stage(ts, c):
    return pltpu.VMEM((c // LANES, ts, LANES), F32)


def _split_rows(val, stage_ref, dil):
    if dil == 1:
        return [val]
    ts, c = val.shape
    n, nc = ts // dil, c // LANES
    for k in range(nc):
        stage_ref[k] = val[:, k * LANES:(k + 1) * LANES]
    return [jnp.concatenate([stage_ref[k, pl.ds(r, n, stride=dil), :] for k in range(nc)], axis=1) for r in range(dil)]


def _merge_rows(parts, stage_ref, dil):
    if dil == 1:
        return parts[0]
    n, c = parts[0].shape
    nc = c // LANES
    for r, part in enumerate(parts):
        for k in range(nc):
            stage_ref[k, pl.ds(r, n, stride=dil), :] = part[:, k * LANES:(k + 1) * LANES]
    return jnp.concatenate([stage_ref[k] for k in range(nc)], axis=1)


def _tables_tiled(table, seq, dil):
    ts = _tile_rows(seq)
    return table.reshape(seq // ts, ts // dil, dil, LANES).transpose(0, 2, 1, 3).reshape(seq, LANES)


def _tables_by_residue(table, seq, dil):
    return table.reshape(seq // dil, dil, LANES).transpose(1, 0, 2)


def _qkv_proj(x, gain, w, cos, sin, seq, dil, group, tag, exchange=()):
    t, d = x.shape
    ts = _tile_rows(seq)
    n = ts // dil
    per_seq = seq // ts

    def body(x_ref, g_ref, w_ref, cos_ref, sin_ref, o_ref, stage_ref):
        xv = jnp.concatenate(_split_rows(x_ref[...], stage_ref, dil), axis=0)
        h = (xv * _rms(xv) * g_ref[...]).astype(BF16)
        acc = jnp.dot(h, w_ref[...], preferred_element_type=F32)
        c, s = cos_ref[...], sin_ref[...]
        for j in range(QKV_W // LANES):
            cols = slice(j * LANES, (j + 1) * LANES)
            val = acc[:, cols]
            if j < (Q_W + KV_W) // LANES:
                val = _rope(val, c, s)
            if j < Q_W // LANES:
                val = val * SCALE
            val = val.astype(BF16)
            for r in range(dil):
                o_ref[r, :, cols] = val[r * n:(r + 1) * n]

    table = pl.BlockSpec((ts, LANES), lambda i: (i % per_seq, 0))
    (qkv,), exchanged = _hosted_call(
        body, exchange, name=f"qkv_proj_{tag}", grid=(t // ts,),
        in_specs=[pl.BlockSpec((ts, d), lambda i: (i, 0)), pl.BlockSpec((1, d), lambda i: (0, 0)),
                  pl.BlockSpec((d, QKV_W), lambda i: (0, group)), table, table],
        out_specs=[_res_spec(seq, dil, QKV_W)],
        out_shape=[jax.ShapeDtypeStruct(_res_shape(t // seq, seq, dil, QKV_W), BF16)],
        scratch_shapes=[_stage(ts, d)], semantics=("parallel",), args=(x, gain, w, cos, sin))
    return qkv, exchanged


def _band(bq, wk):
    return lax.broadcasted_iota(jnp.int32, (bq, wk), 0) - lax.broadcasted_iota(jnp.int32, (bq, wk), 1)


def _pair_variants(src_ref, base, dst_ref):
    lo = lax.broadcasted_iota(jnp.int32, (src_ref.shape[0], LANES), 1) < HEAD_DIM
    for c in range(KV_W // LANES):
        chunk = src_ref[:, base + c * LANES:base + (c + 1) * LANES]
        rolled = pltpu.roll(chunk, HEAD_DIM, 1)
        zero = jnp.zeros_like(chunk)
        dst_ref[2 * c, 0] = jnp.where(lo, chunk, zero)
        dst_ref[2 * c, 1] = jnp.where(lo, zero, rolled)
        dst_ref[2 * c + 1, 0] = jnp.where(lo, rolled, zero)
        dst_ref[2 * c + 1, 1] = jnp.where(lo, zero, chunk)


def _over_keys(col, wk):
    if wk % LANES:
        return jnp.broadcast_to(col, (col.shape[0], wk))
    wide = jnp.broadcast_to(col, (col.shape[0], LANES))
    return wide if wk == LANES else jnp.concatenate([wide] * (wk // LANES), axis=1)


def _window(i, bq, wk, length):
    q0 = pl.multiple_of(i * bq, bq)
    k0 = pl.multiple_of(jnp.clip(q0 - bq, 0, length - wk), bq)
    return q0, k0


def _attn_fwd(qkv, w, tag, sink=None, exchange=()):
    shape = qkv.shape
    rows_all = _seq_view(qkv)
    nseq, length, _ = rows_all.shape
    bq = min(QUERY_BLOCK, length)
    wk = min(3 * bq, length)
    nb = length // bq
    has_sink = sink is not None

    def body(*refs):
        qkv_ref = refs[0]
        sink_ref = refs[1] if has_sink else None
        o_ref, lse_ref, kk_ref, vv_ref = refs[-4:]
        _pair_variants(qkv_ref, Q_W, kk_ref)
        _pair_variants(qkv_ref, Q_W + KV_W, vv_ref)
        band = _band(bq, wk)
        lane = lax.broadcasted_iota(jnp.int32, (bq, LANES), 1)
        lo = lane < HEAD_DIM

        def block(i, carry):
            q0, k0 = _window(i, bq, wk, length)
            valid = jnp.abs(band + (q0 - k0)) <= w
            rows, krows = pl.ds(q0, bq), pl.ds(k0, wk)
            lse_tile = jnp.zeros((bq, LANES), F32)
            for kv in range(N_KV):
                heads = [(kv * GRP + h, h % 2) for h in range(GRP)]
                qp = [qkv_ref[rows, (kv * 2 + j) * LANES:(kv * 2 + j + 1) * LANES] for j in range(GRP // 2)]
                k2 = jnp.concatenate([kk_ref[kv, 0, krows, :], kk_ref[kv, 1, krows, :]], axis=0)
                v2 = jnp.concatenate([vv_ref[kv, 0, krows, :], vv_ref[kv, 1, krows, :]], axis=0)
                sc2 = [lax.dot_general(q_, k2, NT, preferred_element_type=F32) for q_ in qp]
                sc = [jnp.where(valid, s_[:, half * wk:(half + 1) * wk], NEG_INF) for s_ in sc2 for half in range(2)]
                m = [jnp.max(s_, axis=-1, keepdims=True) for s_ in sc]
                if has_sink:
                    m = [jnp.maximum(m_, sink_ref[hd]) for m_, (hd, _) in zip(m, heads)]
                mb = [jnp.broadcast_to(m_, (bq, LANES)) for m_ in m]
                p = [jnp.exp(s_ - _over_keys(m_, wk)) for s_, m_ in zip(sc, m)]
                den = [jnp.sum(p_, axis=-1, keepdims=True) for p_ in p]
                if has_sink:
                    den = [d_ + jnp.exp(sink_ref[hd] - m_) for d_, m_, (hd, _) in zip(den, m, heads)]
                inv = [jnp.broadcast_to(1.0 / d_, (bq, LANES)) for d_ in den]
                pb = [p_.astype(BF16) for p_ in p]
                for j in range(GRP // 2):
                    o = jnp.dot(jnp.concatenate([pb[2 * j], pb[2 * j + 1]], axis=1), v2, preferred_element_type=F32)
                    o = o * jnp.where(lo, inv[2 * j], inv[2 * j + 1])
                    o_ref[rows, (kv * 2 + j) * LANES:(kv * 2 + j + 1) * LANES] = o.astype(BF16)
                for h, (hd, _) in enumerate(heads):
                    lse_tile = jnp.where(lane == hd, mb[h] - jnp.log(inv[h]), lse_tile)
            lse_ref[rows, :] = lse_tile
            return carry

        lax.fori_loop(0, nb, block, 0)

    args = [rows_all]
    in_specs = [pl.BlockSpec((None, length, QKV_W), lambda i: (i, 0, 0))]
    if has_sink:
        args.append(sink)
        in_specs.append(pl.BlockSpec(memory_space=pltpu.SMEM))
    (o, lse), exchanged = _hosted_call(
        body, exchange, name=f"attn_fwd_{tag}", grid=(nseq,), in_specs=in_specs,
        out_specs=[pl.BlockSpec((None, length, Q_W), lambda i: (i, 0, 0)), pl.BlockSpec((None, length, LANES), lambda i: (i, 0, 0))],
        out_shape=[jax.ShapeDtypeStruct((nseq, length, Q_W), BF16), jax.ShapeDtypeStruct((nseq, length, LANES), F32)],
        scratch_shapes=[pltpu.VMEM((N_KV, 2, length, LANES), BF16), pltpu.VMEM((N_KV, 2, length, LANES), BF16)],
        semantics=("parallel",), args=args)
    return o.reshape(shape[:-1] + (Q_W,)), lse.reshape(shape[:-1] + (LANES,)), exchanged


def _head_expand():
    return (jnp.arange(LANES)[:, None] == jnp.arange(Q_W)[None, :] // HEAD_DIM).astype(BF16)


def _out_proj(x, os, lses, dils, w, seq, tag):
    t, d = x.shape
    ts = _tile_rows(seq)
    ng = len(os)
    bl = t // seq
    if ng == 1:
        def body1(x_ref, o_ref, w_ref, y_ref):
            y_ref[...] = x_ref[...] + jnp.dot(o_ref[...], w_ref[...], preferred_element_type=F32)

        row = pl.BlockSpec((ts, d), lambda i: (i, 0))
        o = os[0].reshape(t, Q_W)
        y = pl.pallas_call(
            body1, name=f"out_proj_{tag}", grid=(t // ts,), in_specs=[row, row, _resident(w.shape)], out_specs=row,
            out_shape=jax.ShapeDtypeStruct((t, d), F32), compiler_params=_params("parallel"),
        )(x, o, w)
        return y, o, [lses[0]]

    def body(*refs):
        x_ref, w_ref, e_ref = refs[:3]
        o_refs, l_refs = refs[3:3 + ng], refs[3 + ng:3 + 2 * ng]
        y_ref, om_ref = refs[3 + 2 * ng:5 + 2 * ng]
        lt_refs = refs[5 + 2 * ng:5 + 3 * ng]
        wide_ref, narrow_ref = refs[5 + 3 * ng:]
        ls = [_merge_rows([l_refs[g][r] for r in range(dils[g])], narrow_ref, dils[g]) for g in range(ng)]
        mx = functools.reduce(jnp.maximum, ls)
        tot = mx + jnp.log(functools.reduce(lambda a, b: a + b, [jnp.exp(l_ - mx) for l_ in ls]))
        e = e_ref[...]
        o = None
        for g in range(ng):
            wt = jnp.exp(ls[g] - tot)
            hi = wt.astype(BF16)
            lo = (wt - hi.astype(F32)).astype(BF16)
            wide = jnp.dot(hi, e, preferred_element_type=F32) + jnp.dot(lo, e, preferred_element_type=F32)
            term = wide * _merge_rows([o_refs[g][r].astype(F32) for r in range(dils[g])], wide_ref, dils[g])
            o = term if o is None else o + term
        ob = o.astype(BF16)
        om_ref[...] = ob
        y_ref[...] = x_ref[...] + jnp.dot(ob, w_ref[...], preferred_element_type=F32)
        for g in range(ng):
            for r, part in enumerate(_split_rows(tot, narrow_ref, dils[g])):
                lt_refs[g][r] = part

    row = pl.BlockSpec((ts, d), lambda i: (i, 0))
    e = _head_expand()
    outs = pl.pallas_call(
        body, name=f"out_proj_{tag}", grid=(t // ts,),
        in_specs=[row, _resident(w.shape), _resident(e.shape)] + [_res_spec(seq, dl, Q_W) for dl in dils]
                 + [_res_spec(seq, dl, LANES) for dl in dils],
        out_specs=[row, pl.BlockSpec((ts, Q_W), lambda i: (i, 0))] + [_res_spec(seq, dl, LANES) for dl in dils],
        out_shape=[jax.ShapeDtypeStruct((t, d), F32), jax.ShapeDtypeStruct((t, Q_W), BF16)]
                  + [jax.ShapeDtypeStruct(_res_shape(bl, seq, dl, LANES), F32) for dl in dils],
        scratch_shapes=[_stage(ts, Q_W), _stage(ts, LANES)],
        compiler_params=_params("parallel"),
    )(x, w, e, *os, *lses)
    return outs[0], outs[1], list(outs[2:])


def _sigmoid(g):
    return 1.0 / (1.0 + jnp.exp(-g))


def _ffn_fwd(x, gain, wg, wu, wd, tag, exchange=()):
    t, d = x.shape
    f = wg.shape[1]
    tm = min(256, t)

    def body(x_ref, gain_ref, wg_ref, wu_ref, wd_ref, y_ref, g_ref, u_ref):
        xv = x_ref[...]
        h = (xv * _rms(xv) * gain_ref[...]).astype(BF16)
        g = jnp.dot(h, wg_ref[...], preferred_element_type=F32)
        u = jnp.dot(h, wu_ref[...], preferred_element_type=F32)
        g_ref[...] = g.astype(BF16)
        u_ref[...] = u.astype(BF16)
        a = (g * _sigmoid(g) * u).astype(BF16)
        y_ref[...] = xv + jnp.dot(a, wd_ref[...], preferred_element_type=F32)

    row = pl.BlockSpec((tm, d), lambda i: (i, 0))
    wide = pl.BlockSpec((tm, f), lambda i: (i, 0))
    outs, exchanged = _hosted_call(
        body, exchange, name=f"ffn_fwd_{tag}", grid=(t // tm,),
        in_specs=[row, _resident((1, d)), _resident(wg.shape), _resident(wu.shape), _resident(wd.shape)],
        out_specs=[row, wide, wide],
        out_shape=[jax.ShapeDtypeStruct((t, d), F32), jax.ShapeDtypeStruct((t, f), BF16), jax.ShapeDtypeStruct((t, f), BF16)],
        scratch_shapes=[], semantics=("parallel",), args=(x, gain, wg, wu, wd))
    return (*outs, exchanged)


def _loss_bwd(x, gain, target):
    t, d = x.shape
    tm = min(512, t)

    def body(x_ref, gain_ref, t_ref, dx_ref, loss_ref, dgain_ref):
        xv, gain_v = x_ref[...], gain_ref[...]
        xhat = xv * _rms(xv)
        err = xhat * gain_v - t_ref[...]
        dy = err * (1.0 / d)
        dx, _ = _rms_bwd(dy, xv, gain_v)
        dx_ref[...] = dx
        first = pl.program_id(0) == 0
        part = 0.5 * jnp.sum(jnp.mean(err * err, axis=-1, keepdims=True), axis=0, keepdims=True)
        _accumulate(loss_ref, jnp.broadcast_to(part, loss_ref.shape), first)
        _accumulate(dgain_ref, jnp.sum(dy * xhat, axis=0, keepdims=True), first)

    row = pl.BlockSpec((tm, d), lambda i: (i, 0))
    return pl.pallas_call(
        body, name="loss_bwd", grid=(t // tm,), in_specs=[row, _resident((1, d)), row],
        out_specs=[row, pl.BlockSpec((1, LANES), lambda i: (0, 0)), pl.BlockSpec((1, d), lambda i: (0, 0))],
        out_shape=[jax.ShapeDtypeStruct((t, d), F32), jax.ShapeDtypeStruct((1, LANES), F32), jax.ShapeDtypeStruct((1, d), F32)],
        compiler_params=_params("arbitrary"),
    )(x, gain, target)


def _ffn_bwd(dy, x, gain, g, u, wg, wu, wd, tag, exchange=()):
    t, d = x.shape
    f = wg.shape[1]
    tm = min(256, t)

    def body(dy_ref, x_ref, gain_ref, g_ref, u_ref, wg_ref, wu_ref, wd_ref, dx_ref, dg_ref, du_ref, a_ref, h_ref, dgain_ref):
        dyv = dy_ref[...]
        da = lax.dot_general(dyv.astype(BF16), wd_ref[...], NT, preferred_element_type=F32)
        gv, uv = g_ref[...].astype(F32), u_ref[...].astype(F32)
        sg = _sigmoid(gv)
        act = gv * sg
        a_ref[...] = (act * uv).astype(BF16)
        du = (da * act).astype(BF16)
        dg = (da * uv * (sg * (1.0 + gv * (1.0 - sg)))).astype(BF16)
        du_ref[...] = du
        dg_ref[...] = dg
        dh = (lax.dot_general(dg, wg_ref[...], NT, preferred_element_type=F32)
              + lax.dot_general(du, wu_ref[...], NT, preferred_element_type=F32))
        xv, gain_v = x_ref[...], gain_ref[...]
        dx, xhat = _rms_bwd(dh, xv, gain_v)
        dx_ref[...] = dyv + dx
        h_ref[...] = (xhat * gain_v).astype(BF16)
        _accumulate(dgain_ref, jnp.sum(dh * xhat, axis=0, keepdims=True), pl.program_id(0) == 0)

    row = pl.BlockSpec((tm, d), lambda i: (i, 0))
    wide = pl.BlockSpec((tm, f), lambda i: (i, 0))
    outs, exchanged = _hosted_call(
        body, exchange, name=f"ffn_bwd_{tag}", grid=(t // tm,),
        in_specs=[row, row, _resident((1, d)), wide, wide, _resident(wg.shape), _resident(wu.shape), _resident(wd.shape)],
        out_specs=[row, wide, wide, wide, row, pl.BlockSpec((1, d), lambda i: (0, 0))],
        out_shape=[jax.ShapeDtypeStruct((t, d), F32), jax.ShapeDtypeStruct((t, f), BF16), jax.ShapeDtypeStruct((t, f), BF16),
                   jax.ShapeDtypeStruct((t, f), BF16), jax.ShapeDtypeStruct((t, d), BF16), jax.ShapeDtypeStruct((1, d), F32)],
        scratch_shapes=[], semantics=("arbitrary",), args=(dy, x, gain, g, u, wg, wu, wd))
    return (*outs, exchanged)


def _tn_matmul(a, b, name, into=None, column=0, columns=1):
    t, k = a.shape
    n = b.shape[1]
    tk = k // 2 if (k // 2) % LANES == 0 else k
    tt = min(2048, t)

    def body(a_ref, b_ref, *rest):
        o_ref, acc_ref = rest[-2:]
        prod = lax.dot_general(a_ref[...].astype(BF16), b_ref[...].astype(BF16), TN, preferred_element_type=F32)
        j = pl.program_id(1)

        @pl.when(j == 0)
        def _():
            acc_ref[...] = prod

        @pl.when(j > 0)
        def _():
            acc_ref[...] += prod

        @pl.when(j == pl.num_programs(1) - 1)
        def _():
            o_ref[...] = acc_ref[...].astype(BF16)

    return pl.pallas_call(
        body, name=name, grid=(k // tk, t // tt),
        in_specs=[pl.BlockSpec((tt, tk), lambda i, j: (j, i)), pl.BlockSpec((tt, n), lambda i, j: (j, 0))]
                 + ([ANY] if into is not None else []),
        out_specs=pl.BlockSpec((tk, n), lambda i, j: (i, column)),
        out_shape=jax.ShapeDtypeStruct((k, columns * n), BF16),
        scratch_shapes=[pltpu.VMEM((tk, n), F32)],
        input_output_aliases={2: 0} if into is not None else {},
        compiler_params=_params("parallel", "arbitrary"),
    )(a, b, *([into] if into is not None else []))


def _attn_out_bwd(dx, w, o, dils, seq, tag, lse=None, sink=None):
    t, d = dx.shape
    ts = _tile_rows(seq)
    bl = t // seq
    ng = len(dils)
    has_sink = sink is not None
    expand = _head_expand().T

    def body(*refs):
        refs = list(refs)
        dx_ref, w_ref, o_ref, e_ref = refs[:4]
        refs = refs[4:]
        lse_ref, sink_ref = (refs.pop(0), refs.pop(0)) if has_sink else (None, None)
        do_refs, dl_refs = refs[:ng], refs[ng:2 * ng]
        refs = refs[2 * ng:]
        dsink_ref = refs.pop(0) if has_sink else None
        dof_ref, dlf_ref = refs
        do = lax.dot_general(dx_ref[...].astype(BF16), w_ref[...], NT, preferred_element_type=F32)
        prod = do * o_ref[...].astype(F32)
        hi = prod.astype(BF16)
        lo = (prod - hi.astype(F32)).astype(BF16)
        e = e_ref[...]
        dl = jnp.dot(hi, e, preferred_element_type=F32) + jnp.dot(lo, e, preferred_element_type=F32)
        for g in range(ng):
            for r, part in enumerate(_split_rows(do, dof_ref, dils[g])):
                do_refs[g][r] = part.astype(BF16)
            for r, part in enumerate(_split_rows(dl, dlf_ref, dils[g])):
                dl_refs[g][r] = part
        if has_sink:
            part = -jnp.exp(sink_ref[...] - lse_ref[...]) * dl
            _accumulate(dsink_ref, jnp.sum(part, axis=0, keepdims=True), pl.program_id(0) == 0)

    row = pl.BlockSpec((ts, d), lambda i: (i, 0))
    narrow = pl.BlockSpec((ts, LANES), lambda i: (i, 0))
    args = [dx, w, o, expand]
    in_specs = [row, _resident(w.shape), pl.BlockSpec((ts, Q_W), lambda i: (i, 0)), _resident(expand.shape)]
    if has_sink:
        args += [lse, jnp.pad(sink.reshape(1, N_HEADS), ((0, 0), (0, LANES - N_HEADS)))]
        in_specs += [narrow, _resident((1, LANES))]
    out_specs = [_res_spec(seq, dl, Q_W) for dl in dils] + [_res_spec(seq, dl, LANES) for dl in dils]
    out_shape = ([jax.ShapeDtypeStruct(_res_shape(bl, seq, dl, Q_W), BF16) for dl in dils]
                 + [jax.ShapeDtypeStruct(_res_shape(bl, seq, dl, LANES), F32) for dl in dils])
    if has_sink:
        out_specs.append(pl.BlockSpec((1, LANES), lambda i: (0, 0)))
        out_shape.append(jax.ShapeDtypeStruct((1, LANES), F32))
    outs = pl.pallas_call(
        body, name=f"attn_out_bwd_{tag}", grid=(t // ts,), in_specs=in_specs, out_specs=out_specs, out_shape=out_shape,
        scratch_shapes=[_stage(ts, Q_W), _stage(ts, LANES)],
        compiler_params=_params("arbitrary" if has_sink else "parallel"),
    )(*args)
    return list(outs[:ng]), list(outs[ng:2 * ng]), (outs[2 * ng] if has_sink else None)


def _attn_bwd(qkv, do, lse, delta, cos, sin, w, tag, exchange=()):
    shape = qkv.shape
    dil = shape[1]
    rows_all = _seq_view(qkv)
    nseq, length, _ = rows_all.shape
    bq = min(QUERY_BLOCK, length)
    wk = min(3 * bq, length)
    nb = length // bq

    def body(qkv_ref, do_ref, lse_ref, dl_ref, cos_ref, sin_ref, dp_ref, kk_ref, vv_ref, dk_ref, dv_ref):
        _pair_variants(qkv_ref, Q_W, kk_ref)
        _pair_variants(qkv_ref, Q_W + KV_W, vv_ref)
        dk_ref[...] = jnp.zeros_like(dk_ref)
        dv_ref[...] = jnp.zeros_like(dv_ref)
        band = _band(bq, wk)
        lo_q = lax.broadcasted_iota(jnp.int32, (bq, LANES), 1) < HEAD_DIM
        hi_q = jnp.logical_not(lo_q)

        def block(i, carry):
            q0, k0 = _window(i, bq, wk, length)
            valid = jnp.abs(band + (q0 - k0)) <= w
            rows, krows = pl.ds(q0, bq), pl.ds(k0, wk)
            c, sn = cos_ref[rows, :], -sin_ref[rows, :]
            lse_t, dl_t = lse_ref[rows, :], dl_ref[rows, :]
            for kv in range(N_KV):
                heads = [(kv * GRP + h, h % 2) for h in range(GRP)]
                cols = [slice((kv * 2 + j) * LANES, (kv * 2 + j + 1) * LANES) for j in range(GRP // 2)]
                qp = [qkv_ref[rows, cs] for cs in cols]
                dop = [do_ref[rows, cs] for cs in cols]
                k2 = jnp.concatenate([kk_ref[kv, 0, krows, :], kk_ref[kv, 1, krows, :]], axis=0)
                v2 = jnp.concatenate([vv_ref[kv, 0, krows, :], vv_ref[kv, 1, krows, :]], axis=0)
                sc2 = [lax.dot_general(q_, k2, NT, preferred_element_type=F32) for q_ in qp]
                dp2 = [lax.dot_general(d_, v2, NT, preferred_element_type=F32) for d_ in dop]
                sc = [s_[:, half * wk:(half + 1) * wk] for s_ in sc2 for half in range(2)]
                dp = [d_[:, half * wk:(half + 1) * wk] for d_ in dp2 for half in range(2)]
                p = [jnp.exp(jnp.where(valid, s_, NEG_INF) - _over_keys(lse_t[:, hd:hd + 1], wk))
                     for s_, (hd, _) in zip(sc, heads)]
                ds = [(p_ * (dp_ - _over_keys(dl_t[:, hd:hd + 1], wk))).astype(BF16) for p_, dp_, (hd, _) in zip(p, dp, heads)]
                pb = [p_.astype(BF16) for p_ in p]
                for j in range(GRP // 2):
                    dq = jnp.dot(jnp.concatenate([ds[2 * j], ds[2 * j + 1]], axis=1), k2, preferred_element_type=F32) * SCALE
                    dp_ref[rows, cols[j]] = _rope(dq, c, sn).astype(BF16)
                zero = jnp.zeros((bq, LANES), BF16)
                q4 = jnp.concatenate([jnp.where(lo_q if h % 2 == 0 else hi_q, qp[h // 2], zero) for h in range(GRP)], axis=0)
                do4 = jnp.concatenate([jnp.where(lo_q if h % 2 == 0 else hi_q, dop[h // 2], zero) for h in range(GRP)], axis=0)
                dk_ref[kv, krows, :] += lax.dot_general(jnp.concatenate(ds, axis=0), q4, TN, preferred_element_type=F32)
                dv_ref[kv, krows, :] += lax.dot_general(jnp.concatenate(pb, axis=0), do4, TN, preferred_element_type=F32)
            return carry

        lax.fori_loop(0, nb, block, 0)
        lo = lax.broadcasted_iota(jnp.int32, (length, LANES), 1) < HEAD_DIM
        c, sn = cos_ref[...], -sin_ref[...]
        for ch in range(KV_W // LANES):
            halves = []
            for acc_ref in (dk_ref, dv_ref):
                even, odd = acc_ref[2 * ch], acc_ref[2 * ch + 1]
                even = even + pltpu.roll(even, HEAD_DIM, 1)
                odd = odd + pltpu.roll(odd, HEAD_DIM, 1)
                halves.append(jnp.where(lo, even, odd))
            dp_ref[:, Q_W + ch * LANES:Q_W + (ch + 1) * LANES] = _rope(halves[0], c, sn).astype(BF16)
            dp_ref[:, Q_W + KV_W + ch * LANES:Q_W + KV_W + (ch + 1) * LANES] = halves[1].astype(BF16)

    mode = dict(pipeline_mode=pl.Buffered(1)) if dil == 1 else {}

    def seq_block(c):
        return pl.BlockSpec((None, length, c), lambda i: (i, 0, 0), **mode)

    table = pl.BlockSpec((None, length, LANES), lambda i: (i % dil, 0, 0), **mode)
    (out,), exchanged = _hosted_call(
        body, exchange, name=f"attn_bwd_{tag}", grid=(nseq,),
        in_specs=[seq_block(QKV_W), seq_block(Q_W), seq_block(LANES), seq_block(LANES), table, table],
        out_specs=[pl.BlockSpec((None, length, QKV_W), lambda i: (i, 0, 0))],
        out_shape=[jax.ShapeDtypeStruct((nseq, length, QKV_W), BF16)],
        scratch_shapes=[pltpu.VMEM((N_KV, 2, length, LANES), BF16), pltpu.VMEM((N_KV, 2, length, LANES), BF16),
                        pltpu.VMEM((N_KV, length, LANES), F32), pltpu.VMEM((N_KV, length, LANES), F32)],
        semantics=("parallel",), args=(rows_all, _seq_view(do), _seq_view(lse), _seq_view(delta), cos, sin))
    return out.reshape(shape), exchanged


def _qkv_bwd(dy, x, gain, w, dps, dils, seq, tag, exchange=()):
    t, d = x.shape
    ts = _tile_rows(seq)
    bl = t // seq
    ng = len(dps)

    def body(dy_ref, x_ref, gain_ref, w_ref, *refs):
        dp_refs, dx_ref = refs[:ng], refs[ng]
        h_refs = refs[ng + 1:2 * ng + 1]
        dgain_ref, stage_ref = refs[2 * ng + 1:]
        dh = None
        for gi in range(ng):
            dil = dils[gi]
            n = ts // dil
            dp = dp_refs[gi][0] if dil == 1 else jnp.concatenate([dp_refs[gi][r] for r in range(dil)], axis=0)
            part = lax.dot_general(dp, w_ref[:, gi * QKV_W:(gi + 1) * QKV_W], NT, preferred_element_type=F32)
            part = _merge_rows([part[r * n:(r + 1) * n] for r in range(dil)], stage_ref, dil)
            dh = part if dh is None else dh + part
        xv, gain_v = x_ref[...], gain_ref[...]
        dx, xhat = _rms_bwd(dh, xv, gain_v)
        dx_ref[...] = dy_ref[...] + dx
        h = xhat * gain_v
        for gi in range(ng):
            for r, part in enumerate(_split_rows(h, stage_ref, dils[gi])):
                h_refs[gi][r] = part.astype(BF16)
        _accumulate(dgain_ref, jnp.sum(dh * xhat, axis=0, keepdims=True), pl.program_id(0) == 0)

    row = pl.BlockSpec((ts, d), lambda i: (i, 0))
    outs, exchanged = _hosted_call(
        body, exchange, name=f"qkv_bwd_{tag}", grid=(t // ts,),
        in_specs=[row, row, _resident((1, d)), _resident(w.shape)] + [_res_spec(seq, dl, QKV_W) for dl in dils],
        out_specs=[row] + [_res_spec(seq, dl, d) for dl in dils] + [pl.BlockSpec((1, d), lambda i: (0, 0))],
        out_shape=[jax.ShapeDtypeStruct((t, d), F32)] + [jax.ShapeDtypeStruct(_res_shape(bl, seq, dl, d), BF16) for dl in dils]
                  + [jax.ShapeDtypeStruct((1, d), F32)],
        scratch_shapes=[_stage(ts, d)], semantics=("arbitrary",), args=(dy, x, gain, w, *dps))
    return outs[0], list(outs[1:1 + ng]), outs[1 + ng], exchanged


ANY = pl.BlockSpec(memory_space=pl.ANY)


def _place():
    x, y, c = lax.axis_index("x"), lax.axis_index("y"), lax.axis_index("c")
    return x, y, c


def _all_gather(shards):
    n = len(shards)

    def body(*refs):
        ins, outs = refs[:n], refs[n:2 * n]
        send_sems, recv_sems, local_sems = refs[2 * n:]
        x, y, c = _place()
        sibling = (x, y, 1 - c)
        chips = [(1 - x, y), (x, 1 - y), (1 - x, 1 - y)]

        def copy(a, k, block, to, src=None):
            px, py, pc = block
            rows = outs[a].at[4 * px + 2 * py + pc]
            return pltpu.make_async_remote_copy(
                src_ref=rows if src is None else src, dst_ref=rows, send_sem=send_sems.at[a, k], recv_sem=recv_sems.at[a, k],
                device_id=to, device_id_type=MESH)

        sent = []
        for a in range(n):
            mine = pltpu.make_async_copy(ins[a], outs[a].at[4 * x + 2 * y + c], local_sems.at[a])
            mine.start()
            sent.append(mine)
        for a in range(n):
            first = [copy(a, 0, (x, y, c), sibling, src=ins[a])]
            first += [copy(a, 1 + j, (x, y, c), (*chip, c), src=ins[a]) for j, chip in enumerate(chips)]
            for cp in first:
                cp.start()
            sent += first
        for a in range(n):
            for j, chip in enumerate(chips):
                copy(a, 1 + j, (*chip, c), (x, y, c)).wait_recv()
                passed = copy(a, 4 + j, (*chip, c), sibling)
                passed.start()
                sent.append(passed)
        for a in range(n):
            copy(a, 0, sibling, (x, y, c)).wait_recv()
            for j, chip in enumerate(chips):
                copy(a, 4 + j, (*chip, 1 - c), (x, y, c)).wait_recv()
        for cp in sent[n:]:
            cp.wait_send()
        for mine in sent[:n]:
            mine.wait()

    return pl.pallas_call(
        body, name="all_gather_weights", in_specs=[ANY] * n, out_specs=[ANY] * n,
        out_shape=[jax.ShapeDtypeStruct((N_DEV,) + s.shape, s.dtype) for s in shards],
        scratch_shapes=[pltpu.SemaphoreType.DMA((n, 7)), pltpu.SemaphoreType.DMA((n, 7)), pltpu.SemaphoreType.DMA((n,))],
    )(*shards)


def _exchange_copies(srcs, dsts, gather, send_sems, recv_sems, local_sems):
    x, y, c = _place()
    me = 4 * x + 2 * y + c
    copies = []
    for a, (src, dst) in enumerate(zip(srcs, dsts)):
        copies.append(pltpu.make_async_copy(src if gather[a] else src.at[me], dst.at[me], local_sems.at[a]))
        for k in range(1, N_DEV):
            peer = me ^ k
            copies.append(pltpu.make_async_remote_copy(
                src_ref=src if gather[a] else src.at[peer], dst_ref=dst.at[me], send_sem=send_sems.at[a, k - 1],
                recv_sem=recv_sems.at[a, k - 1], device_id=(peer // 4, (peer // 2) % 2, peer % 2), device_id_type=MESH))
    return copies


def _exchange_scratch(n):
    return [pltpu.SemaphoreType.DMA((n, N_DEV - 1)), pltpu.SemaphoreType.DMA((n, N_DEV - 1)), pltpu.SemaphoreType.DMA((n,))]


def _exchanged_shapes(exchange):
    return [jax.ShapeDtypeStruct(((N_DEV,) + a.shape) if g else a.shape, a.dtype) for a, g in exchange]


def _hosted_call(body, exchange, *, name, grid, in_specs, out_specs, out_shape, scratch_shapes, semantics, args):
    out_specs, out_shape, scratch = list(out_specs), list(out_shape), list(scratch_shapes)
    if not exchange:
        outs = pl.pallas_call(body, name=name, grid=grid, in_specs=in_specs, out_specs=out_specs, out_shape=out_shape,
                              scratch_shapes=scratch, compiler_params=_params(*semantics))(*args)
        return list(outs), []
    n, n_in, n_out, n_scr = len(exchange), len(in_specs), len(out_specs), len(scratch)
    gather = [g for _, g in exchange]

    def hosted(*refs):
        own_in, x_in = refs[:n_in], refs[n_in:n_in + n]
        own_out, x_out = refs[n_in + n:n_in + n + n_out], refs[n_in + n + n_out:n_in + 2 * n + n_out]
        own_scr, sems = refs[n_in + 2 * n + n_out:n_in + 2 * n + n_out + n_scr], refs[-3:]

        @pl.when(pl.program_id(0) == 0)
        def _():
            for cp in _exchange_copies(x_in, x_out, gather, *sems):
                cp.start()

        body(*own_in, *own_out, *own_scr)

        @pl.when(pl.program_id(0) == pl.num_programs(0) - 1)
        def _():
            for cp in _exchange_copies(x_in, x_out, gather, *sems):
                cp.wait()

    outs = pl.pallas_call(
        hosted, name=name, grid=grid, in_specs=list(in_specs) + [ANY] * n, out_specs=out_specs + [ANY] * n,
        out_shape=out_shape + _exchanged_shapes(exchange), scratch_shapes=scratch + _exchange_scratch(n),
        compiler_params=_params("arbitrary"),
    )(*args, *[a for a, _ in exchange])
    return list(outs[:n_out]), list(outs[n_out:])


def _exchange_now(exchange):
    n = len(exchange)
    gather = [g for _, g in exchange]

    def body(*refs):
        copies = _exchange_copies(refs[:n], refs[n:2 * n], gather, *refs[2 * n:])
        for cp in copies:
            cp.start()
        for cp in copies:
            cp.wait()

    return pl.pallas_call(
        body, name="exchange_last", in_specs=[ANY] * n, out_specs=[ANY] * n, out_shape=_exchanged_shapes(exchange),
        scratch_shapes=_exchange_scratch(n),
    )(*[a for a, _ in exchange])


def _all_reduce_small(v):
    def body(v_ref, o_ref, recv_ref, send_sems, recv_sems):
        x, y, c = _place()
        me = 4 * x + 2 * y + c
        copies = []
        for k in range(1, N_DEV):
            peer = me ^ k
            copies.append(pltpu.make_async_remote_copy(
                src_ref=v_ref, dst_ref=recv_ref.at[k], send_sem=send_sems.at[k - 1], recv_sem=recv_sems.at[k - 1],
                device_id=(peer // 4, (peer // 2) % 2, peer % 2), device_id_type=MESH))
        for cp in copies:
            cp.start()
        recv_ref[0] = v_ref[...]
        for cp in copies:
            cp.wait()
        acc = recv_ref[me]
        for src in range(1, N_DEV):
            acc = acc + recv_ref[me ^ src]
        o_ref[...] = acc

    vm = pl.BlockSpec(memory_space=pltpu.VMEM)
    return pl.pallas_call(
        body, name="all_reduce_small", in_specs=[vm], out_specs=vm, out_shape=jax.ShapeDtypeStruct(v.shape, F32),
        scratch_shapes=[pltpu.VMEM((N_DEV,) + v.shape, F32), pltpu.SemaphoreType.DMA((N_DEV - 1,)),
                        pltpu.SemaphoreType.DMA((N_DEV - 1,))],
    )(v)


def _adamw_math(w, g, m, v):
    m = ADAM_B1 * m + (1.0 - ADAM_B1) * g
    v = ADAM_B2 * v + (1.0 - ADAM_B2) * (g * g)
    m_hat = m / (1.0 - ADAM_B1 ** ADAM_STEP)
    v_hat = v / (1.0 - ADAM_B2 ** ADAM_STEP)
    delta = -ADAM_LR * (m_hat / (jnp.sqrt(v_hat) + ADAM_EPS) + ADAM_WD * w)
    return delta, m, v


def _adamw(parts, w, m, v, name, layer=None, into=None):
    r, c = w.shape[-2:]
    tr = r // 2 if r % 16 == 0 and r >= 256 else r
    n = len(parts)

    def body(*refs):
        w_ref, m_ref, v_ref = refs[n:n + 3]
        g_ref, d_ref, nm_ref, nv_ref = refs[-4:]
        g = refs[0][...].astype(F32)
        for p_ref in refs[1:n]:
            g = g + p_ref[...].astype(F32)
        g_ref[...] = g
        d_ref[...], nm_ref[...], nv_ref[...] = _adamw_math(w_ref[...], g, m_ref[...], v_ref[...])

    def slab(slot):
        return pl.BlockSpec((None, tr, c), lambda i: (slot, i, 0))

    tile = pl.BlockSpec((tr, c), lambda i: (i, 0)) if layer is None else slab(layer)
    arrays, in_specs = [], []
    for p in parts:
        if isinstance(p, tuple):
            arrays.append(p[0])
            in_specs.append(slab(p[1]))
        else:
            arrays.append(p)
            in_specs.append(tile)
    kept = list(into) if into is not None else []
    return pl.pallas_call(
        body, name=name, grid=(r // tr,), in_specs=in_specs + [tile] * 3 + [ANY] * len(kept), out_specs=[tile] * 4,
        out_shape=[jax.ShapeDtypeStruct(w.shape, F32)] * 4,
        input_output_aliases={n + 3 + k: k for k in range(len(kept))}, compiler_params=_params("parallel"),
    )(*arrays, w, m, v, *kept)


def _columns(g):
    return g.transpose(1, 0, 2).reshape(g.shape[1], -1)


def _rows(g):
    return g.reshape(-1, g.shape[-1])


def _column_blocks(dw):
    k, n = dw.shape
    return dw.reshape(k, N_DEV, n // N_DEV).transpose(1, 0, 2)


def _row_blocks(dw):
    k, n = dw.shape
    return dw.reshape(N_DEV, k // N_DEV, n)


def _pack_rows(rows, width):
    out = None
    for i, r in enumerate(rows):
        r = r.reshape(1, -1).astype(F32)
        r = jnp.pad(r, ((i, 8 - 1 - i), (0, width - r.shape[1])))
        out = r if out is None else out + r
    return out


def _mixer_fwd(x, gain, w_in, w_out, cos, sin, seq, groups, tag, sink=None, exchanges=None):
    qkvs, os, lses, got = [], [], [], {}
    for gi, (dil, w) in enumerate(groups):
        qkv, got["proj", gi] = _qkv_proj(x, gain, w_in, _tables_tiled(cos, seq, dil), _tables_tiled(sin, seq, dil), seq, dil, gi,
                                         f"{tag}{gi}", exchange=(exchanges or {}).get(("proj", gi), ()))
        o, lse, got[gi] = _attn_fwd(qkv, w, f"{tag}{gi}", sink=sink, exchange=(exchanges or {}).get(gi, ()))
        qkvs.append(qkv)
        os.append(o)
        lses.append(lse)
    y, o, lses = _out_proj(x, os, lses, [dl for dl, _ in groups], w_out, seq, tag)
    return y, (qkvs, o, lses), got


def _mixer_bwd(dy, x_in, gain, w_in, w_out, saved, cos, sin, seq, groups, tag, sink=None, exchange=(), scatter_dw_out=False):
    qkvs, o, lses = saved
    t, d = x_in.shape
    dils = [dl for dl, _ in groups]
    lse_tokens = lses[0].reshape(t, LANES) if sink is not None else None
    dos, dls, dsink = _attn_out_bwd(dy, w_out, o, dils, seq, tag, lse=lse_tokens, sink=sink)
    dw_out = _tn_matmul(o, dy, f"dw_out_{tag}")
    if scatter_dw_out:
        exchange = list(exchange) + _to_send([], [dw_out])
    dps, got = [], []
    for gi, (dil, w) in enumerate(groups):
        dp, brought = _attn_bwd(qkvs[gi], dos[gi], lses[gi], dls[gi], _tables_by_residue(cos, seq, dil),
                                _tables_by_residue(sin, seq, dil), w, f"{tag}{gi}", exchange=exchange if gi == 0 else ())
        dps.append(dp)
        got += brought
    if scatter_dw_out:
        dw_out = got.pop()
    dx, hs, dgain, _ = _qkv_bwd(dy, x_in, gain, w_in, dps, dils, seq, tag)
    dw_in = None
    for gi in range(len(groups)):
        dw_in = _tn_matmul(hs[gi].reshape(t, d), dps[gi].reshape(t, QKV_W), f"dw_in_{tag}{gi}", into=dw_in, column=gi,
                           columns=len(groups))
    return dx, dw_in, dw_out, dgain, dsink, got


def _ffn_layer_bwd(dy, x_in, gain, g, u, wg, wu, wd, tag, exchange=()):
    dx, dg, du, act, h, dgain, got = _ffn_bwd(dy, x_in, gain, g, u, wg, wu, wd, tag, exchange=exchange)
    dwd = _tn_matmul(act, dy, f"dw_down_{tag}")
    dwg = _tn_matmul(h, dg, f"dw_gate_{tag}")
    dwu = _tn_matmul(h, du, f"dw_up_{tag}")
    return dx, dwg, dwu, dwd, dgain, got


def _to_send(dws_by_columns, dws_by_rows):
    return [(_column_blocks(g), False) for g in dws_by_columns] + [(_row_blocks(g), False) for g in dws_by_rows]


def kernel(x, a_w_in, a_sink, a_w_out, b_w_in, b_w_out, norm_mix, norm_ffn, w_gate, w_up, w_down, final_norm, loss_target, m_a_w_in, m_a_sink, m_a_w_out, m_b_w_in, m_b_w_out, m_norm_mix, m_norm_ffn, m_w_gate, m_w_up, m_w_down, m_final_norm, v_a_w_in, v_a_sink, v_a_w_out, v_b_w_in, v_b_w_out, v_norm_mix, v_norm_ffn, v_w_gate, v_w_up, v_w_down, v_final_norm):
    bl, seq, d = x.shape
    t = bl * seq
    xf = x.reshape(t, d)
    target = loss_target.reshape(t, d)
    cos, sin = _rope_tables(seq)
    groups_a = [(1, ATTN_HALF_WINDOW)]
    groups_b = [(dil, window // 2 // dil) for window, dil in DILATED_GROUPS]

    def shard(w_):
        return w_.astype(BF16)

    wa_in, wa_out = _all_gather([shard(a_w_in[0]), shard(a_w_out[0])])
    wa_in, wa_out = _columns(wa_in), _rows(wa_out)

    x1_0, saved_a, got = _mixer_fwd(xf, norm_mix[0:1], wa_in, wa_out, cos, sin, seq, groups_a, "a", sink=a_sink[0],
                                    exchanges={("proj", 0): [(shard(w_down[0]), True)],
                                               0: [(shard(w_gate[0]), True), (shard(w_up[0]), True)]})
    wg0, wu0, wd0 = _columns(got[0][0]), _columns(got[0][1]), _rows(got["proj", 0][0])
    x2_0, g0, u0, got = _ffn_fwd(x1_0, norm_ffn[0:1], wg0, wu0, wd0, "0",
                                 exchange=[(shard(b_w_in[0]), True), (shard(b_w_out[0]), True)])
    wb_in, wb_out = _columns(got[0]), _rows(got[1])
    x1_1, saved_b, got = _mixer_fwd(x2_0, norm_mix[1:2], wb_in, wb_out, cos, sin, seq, groups_b, "b",
                                    exchanges={0: [(shard(w_gate[1]), True), (shard(w_up[1]), True)], 1: [(shard(w_down[1]), True)]})
    wg1, wu1, wd1 = _columns(got[0][0]), _columns(got[0][1]), _rows(got[1][0])
    x2_1, g1, u1, _ = _ffn_fwd(x1_1, norm_ffn[1:2], wg1, wu1, wd1, "1")

    dy, loss_part, d_final = _loss_bwd(x2_1, final_norm.reshape(1, d), target)
    dy, dwg1, dwu1, dwd1, d_nf1, _ = _ffn_layer_bwd(dy, x1_1, norm_ffn[1:2], g1, u1, wg1, wu1, wd1, "1")
    dy, dwb_in, dwb_out, d_nm1, _, (r_g1, r_u1, r_d1) = _mixer_bwd(
        dy, x2_0, norm_mix[1:2], wb_in, wb_out, saved_b, cos, sin, seq, groups_b, "b", exchange=_to_send([dwg1, dwu1], [dwd1]))
    dy, dwg0, dwu0, dwd0, d_nf0, (r_b_in, r_b_out) = _ffn_layer_bwd(
        dy, x1_0, norm_ffn[0:1], g0, u0, wg0, wu0, wd0, "0", exchange=_to_send([dwb_in], [dwb_out]))
    dy, dwa_in, r_a_out, d_nm0, d_sink, (r_g0, r_u0, r_d0) = _mixer_bwd(
        dy, xf, norm_mix[0:1], wa_in, wa_out, saved_a, cos, sin, seq, groups_a, "a", sink=a_sink[0],
        exchange=_to_send([dwg0, dwu0], [dwd0]), scatter_dw_out=True)
    (r_a_in,) = _exchange_now(_to_send([dwa_in], []))
    grad_x = dy.reshape(bl, seq, d)

    def update(received, w_, m_, v_, name):
        out = None
        for layer in reversed(range(len(received))):
            out = _adamw([(received[layer], src) for src in range(N_DEV)], w_, m_, v_, f"adamw_{name}{layer}", layer=layer, into=out)
        return out

    u_a_in = update([r_a_in], a_w_in, m_a_w_in, v_a_w_in, "a_in")
    u_a_out = update([r_a_out], a_w_out, m_a_w_out, v_a_w_out, "a_out")
    u_b_in = update([r_b_in], b_w_in, m_b_w_in, v_b_w_in, "b_in")
    u_b_out = update([r_b_out], b_w_out, m_b_w_out, v_b_w_out, "b_out")
    u_gate = update([r_g0, r_g1], w_gate, m_w_gate, v_w_gate, "gate")
    u_up = update([r_u0, r_u1], w_up, m_w_up, v_w_up, "up")
    u_down = update([r_d0, r_d1], w_down, m_w_down, v_w_down, "down")

    small = _pack_rows([d_nm0, d_nm1, d_nf0, d_nf1, d_final, d_sink, loss_part], d)
    total = _all_reduce_small(small)
    small_w = _pack_rows([norm_mix[0], norm_mix[1], norm_ffn[0], norm_ffn[1], final_norm, a_sink], d)
    small_m = _pack_rows([m_norm_mix[0], m_norm_mix[1], m_norm_ffn[0], m_norm_ffn[1], m_final_norm, m_a_sink], d)
    small_v = _pack_rows([v_norm_mix[0], v_norm_mix[1], v_norm_ffn[0], v_norm_ffn[1], v_final_norm, v_a_sink], d)
    u_small = _adamw([total], small_w, small_m, small_v, "adamw_small")
    loss = total[6, 0]

    outs = []
    for k in range(4):
        sm = u_small[k]
        outs += [u_a_in[k], sm[5:6, :N_HEADS], u_a_out[k], u_b_in[k], u_b_out[k], sm[0:2], sm[2:4],
                 u_gate[k], u_up[k], u_down[k], sm[4]]
    return (loss, grad_x, *outs)
```

```python
import functools
import math

import jax
import jax.numpy as jnp
from jax import lax
from jax.experimental import pallas as pl
from jax.experimental.pallas import tpu as pltpu

F32 = jnp.float32
BF16 = jnp.bfloat16

HEAD_DIM = 64
N_HEADS = 16
N_KV = 4
GRP = N_HEADS // N_KV
Q_W = N_HEADS * HEAD_DIM
KV_W = N_KV * HEAD_DIM
QKV_W = Q_W + 2 * KV_W
ATTN_HALF_WINDOW = 128
DILATED_GROUPS = ((128, 1), (512, 4), (2048, 16))
ROPE_THETA = 10000.0
RMS_EPS = 1e-6
NEG_INF = -1e30
SCALE = 1.0 / math.sqrt(HEAD_DIM)

ADAM_LR = 0.001
ADAM_B1 = 0.9
ADAM_B2 = 0.999
ADAM_EPS = 1e-08
ADAM_WD = 0.01
ADAM_STEP = 10

LANES = 128
VMEM_LIMIT = 56 * 1024 * 1024
QUERY_BLOCK = 128
N_DEV = 8
MESH = pl.DeviceIdType.MESH

NT = (((1,), (1,)), ((), ()))
TN = (((0,), (0,)), ((), ()))


def _params(*sem):
    return pltpu.CompilerParams(dimension_semantics=tuple(sem) if sem else None, vmem_limit_bytes=VMEM_LIMIT)


def _resident(shape):
    return pl.BlockSpec(shape, lambda *_: (0,) * len(shape), pipeline_mode=pl.Buffered(1))


def _rope_tables(seq):
    inv_freq = 1.0 / (ROPE_THETA ** (jnp.arange(0, HEAD_DIM, 2, dtype=F32) / HEAD_DIM))
    ang = jnp.arange(seq, dtype=F32)[:, None] * inv_freq[None, :]
    cos, sin = jnp.cos(ang), jnp.sin(ang)
    return jnp.tile(cos, (1, 4)), jnp.concatenate([-sin, sin, -sin, sin], axis=1)


def _rope(t, cos, sin_signed):
    lane = lax.broadcasted_iota(jnp.int32, t.shape, 1)
    first = (lane & (HEAD_DIM // 2)) == 0
    swapped = jnp.where(first, pltpu.roll(t, LANES - HEAD_DIM // 2, 1), pltpu.roll(t, HEAD_DIM // 2, 1))
    return t * cos + swapped * sin_signed


def _rms(x):
    return lax.rsqrt(jnp.mean(x * x, axis=-1, keepdims=True) + RMS_EPS)


def _rms_bwd(dh, x, gain):
    r = _rms(x)
    xhat = x * r
    dxh = dh * gain
    dx = r * (dxh - xhat * jnp.mean(dxh * xhat, axis=-1, keepdims=True))
    return dx, xhat


def _accumulate(ref, value, first):
    @pl.when(first)
    def _():
        ref[...] = jnp.zeros_like(ref)

    ref[...] += value


def _tile_rows(seq):
    return min(512, seq)


def _res_shape(bl, seq, dil, c):
    ts = _tile_rows(seq)
    return (bl, dil, seq // ts, ts // dil, c)


def _res_spec(seq, dil, c):
    ts = _tile_rows(seq)
    per_seq = seq // ts
    return pl.BlockSpec((None, dil, None, ts // dil, c), lambda i: (i // per_seq, 0, i % per_seq, 0, 0))


def _seq_view(a):
    bl, dil, tiles, n, c = a.shape
    return a.reshape(bl * dil, tiles * n, c)


def _stage(ts, c):
    return pltpu.VMEM((c // LANES, ts, LANES), F32)


def _split_rows(val, stage_ref, dil):
    if dil == 1:
        return [val]
    ts, c = val.shape
    n, nc = ts // dil, c // LANES
    for k in range(nc):
        stage_ref[k] = val[:, k * LANES:(k + 1) * LANES]
    return [jnp.concatenate([stage_ref[k, pl.ds(r, n, stride=dil), :] for k in range(nc)], axis=1) for r in range(dil)]


def _merge_rows(parts, stage_ref, dil):
    if dil == 1:
        return parts[0]
    n, c = parts[0].shape
    nc = c // LANES
    for r, part in enumerate(parts):
        for k in range(nc):
            stage_ref[k, pl.ds(r, n, stride=dil), :] = part[:, k * LANES:(k + 1) * LANES]
    return jnp.concatenate([stage_ref[k] for k in range(nc)], axis=1)


def _tables_tiled(table, seq, dil):
    ts = _tile_rows(seq)
    return table.reshape(seq // ts, ts // dil, dil, LANES).transpose(0, 2, 1, 3).reshape(seq, LANES)


def _tables_by_residue(table, seq, dil):
    return table.reshape(seq // dil, dil, LANES).transpose(1, 0, 2)


def _qkv_proj(x, gain, w, cos, sin, seq, dil, group, tag, exchange=()):
    t, d = x.shape
    ts = _tile_rows(seq)
    n = ts // dil
    per_seq = seq // ts

    def body(x_ref, g_ref, w_ref, cos_ref, sin_ref, o_ref, stage_ref):
        xv = jnp.concatenate(_split_rows(x_ref[...], stage_ref, dil), axis=0)
        h = (xv * _rms(xv) * g_ref[...]).astype(BF16)
        acc = jnp.dot(h, w_ref[...], preferred_element_type=F32)
        c, s = cos_ref[...], sin_ref[...]
        for j in range(QKV_W // LANES):
            cols = slice(j * LANES, (j + 1) * LANES)
            val = acc[:, cols]
            if j < (Q_W + KV_W) // LANES:
                val = _rope(val, c, s)
            if j < Q_W // LANES:
                val = val * SCALE
            val = val.astype(BF16)
            for r in range(dil):
                o_ref[r, :, cols] = val[r * n:(r + 1) * n]

    table = pl.BlockSpec((ts, LANES), lambda i: (i % per_seq, 0))
    (qkv,), exchanged = _hosted_call(
        body, exchange, name=f"qkv_proj_{tag}", grid=(t // ts,),
        in_specs=[pl.BlockSpec((ts, d), lambda i: (i, 0)), pl.BlockSpec((1, d), lambda i: (0, 0)),
                  pl.BlockSpec((d, QKV_W), lambda i: (0, group)), table, table],
        out_specs=[_res_spec(seq, dil, QKV_W)],
        out_shape=[jax.ShapeDtypeStruct(_res_shape(t // seq, seq, dil, QKV_W), BF16)],
        scratch_shapes=[_stage(ts, d)], semantics=("parallel",), args=(x, gain, w, cos, sin))
    return qkv, exchanged


def _band(bq, wk):
    return lax.broadcasted_iota(jnp.int32, (bq, wk), 0) - lax.broadcasted_iota(jnp.int32, (bq, wk), 1)


def _pair_variants(src_ref, base, dst_ref):
    lo = lax.broadcasted_iota(jnp.int32, (src_ref.shape[0], LANES), 1) < HEAD_DIM
    for c in range(KV_W // LANES):
        chunk = src_ref[:, base + c * LANES:base + (c + 1) * LANES]
        rolled = pltpu.roll(chunk, HEAD_DIM, 1)
        zero = jnp.zeros_like(chunk)
        dst_ref[2 * c, 0] = jnp.where(lo, chunk, zero)
        dst_ref[2 * c, 1] = jnp.where(lo, zero, rolled)
        dst_ref[2 * c + 1, 0] = jnp.where(lo, rolled, zero)
        dst_ref[2 * c + 1, 1] = jnp.where(lo, zero, chunk)


def _over_keys(col, wk):
    if wk % LANES:
        return jnp.broadcast_to(col, (col.shape[0], wk))
    wide = jnp.broadcast_to(col, (col.shape[0], LANES))
    return wide if wk == LANES else jnp.concatenate([wide] * (wk // LANES), axis=1)


def _key_rows(bq, w, length):
    return min(bq + 2 * w, length)


def _window(i, bq, w, wk, length):
    q0 = pl.multiple_of(i * bq, bq)
    k0 = pl.multiple_of(jnp.clip(q0 - w, 0, length - wk), min(w, bq))
    return q0, k0


def _attn_fwd(qkv, w, tag, sink=None, exchange=()):
    shape = qkv.shape
    rows_all = _seq_view(qkv)
    nseq, length, _ = rows_all.shape
    bq = min(QUERY_BLOCK, length)
    wk = _key_rows(bq, w, length)
    nb = length // bq
    has_sink = sink is not None

    def body(*refs):
        qkv_ref = refs[0]
        sink_ref = refs[1] if has_sink else None
        o_ref, lse_ref, kk_ref, vv_ref = refs[-4:]
        _pair_variants(qkv_ref, Q_W, kk_ref)
        _pair_variants(qkv_ref, Q_W + KV_W, vv_ref)
        band = _band(bq, wk)
        lane = lax.broadcasted_iota(jnp.int32, (bq, LANES), 1)
        lo = lane < HEAD_DIM

        def block(i, carry):
            q0, k0 = _window(i, bq, w, wk, length)
            valid = jnp.abs(band + (q0 - k0)) <= w
            rows, krows = pl.ds(q0, bq), pl.ds(k0, wk)
            lse_tile = jnp.zeros((bq, LANES), F32)
            for kv in range(N_KV):
                heads = [(kv * GRP + h, h % 2) for h in range(GRP)]
                qp = [qkv_ref[rows, (kv * 2 + j) * LANES:(kv * 2 + j + 1) * LANES] for j in range(GRP // 2)]
                k2 = jnp.concatenate([kk_ref[kv, 0, krows, :], kk_ref[kv, 1, krows, :]], axis=0)
                v2 = jnp.concatenate([vv_ref[kv, 0, krows, :], vv_ref[kv, 1, krows, :]], axis=0)
                sc2 = [lax.dot_general(q_, k2, NT, preferred_element_type=F32) for q_ in qp]
                sc = [jnp.where(valid, s_[:, half * wk:(half + 1) * wk], NEG_INF) for s_ in sc2 for half in range(2)]
                m = [jnp.max(s_, axis=-1, keepdims=True) for s_ in sc]
                if has_sink:
                    m = [jnp.maximum(m_, sink_ref[hd]) for m_, (hd, _) in zip(m, heads)]
                mb = [jnp.broadcast_to(m_, (bq, LANES)) for m_ in m]
                p = [jnp.exp(s_ - _over_keys(m_, wk)) for s_, m_ in zip(sc, m)]
                den = [jnp.sum(p_, axis=-1, keepdims=True) for p_ in p]
                if has_sink:
                    den = [d_ + jnp.exp(sink_ref[hd] - m_) for d_, m_, (hd, _) in zip(den, m, heads)]
                inv = [jnp.broadcast_to(1.0 / d_, (bq, LANES)) for d_ in den]
                pb = [p_.astype(BF16) for p_ in p]
                for j in range(GRP // 2):
                    o = jnp.dot(jnp.concatenate([pb[2 * j], pb[2 * j + 1]], axis=1), v2, preferred_element_type=F32)
                    o = o * jnp.where(lo, inv[2 * j], inv[2 * j + 1])
                    o_ref[rows, (kv * 2 + j) * LANES:(kv * 2 + j + 1) * LANES] = o.astype(BF16)
                for h, (hd, _) in enumerate(heads):
                    lse_tile = jnp.where(lane == hd, mb[h] - jnp.log(inv[h]), lse_tile)
            lse_ref[rows, :] = lse_tile
            return carry

        lax.fori_loop(0, nb, block, 0)

    args = [rows_all]
    in_specs = [pl.BlockSpec((None, length, QKV_W), lambda i: (i, 0, 0))]
    if has_sink:
        args.append(sink)
        in_specs.append(pl.BlockSpec(memory_space=pltpu.SMEM))
    (o, lse), exchanged = _hosted_call(
        body, exchange, name=f"attn_fwd_{tag}", grid=(nseq,), in_specs=in_specs,
        out_specs=[pl.BlockSpec((None, length, Q_W), lambda i: (i, 0, 0)), pl.BlockSpec((None, length, LANES), lambda i: (i, 0, 0))],
        out_shape=[jax.ShapeDtypeStruct((nseq, length, Q_W), BF16), jax.ShapeDtypeStruct((nseq, length, LANES), F32)],
        scratch_shapes=[pltpu.VMEM((N_KV, 2, length, LANES), BF16), pltpu.VMEM((N_KV, 2, length, LANES), BF16)],
        semantics=("parallel",), args=args)
    return o.reshape(shape[:-1] + (Q_W,)), lse.reshape(shape[:-1] + (LANES,)), exchanged


def _head_expand():
    return (jnp.arange(LANES)[:, None] == jnp.arange(Q_W)[None, :] // HEAD_DIM).astype(BF16)


def _out_proj(x, os, lses, dils, w, seq, tag):
    t, d = x.shape
    ts = _tile_rows(seq)
    ng = len(os)
    bl = t // seq
    if ng == 1:
        def body1(x_ref, o_ref, w_ref, y_ref):
            y_ref[...] = x_ref[...] + jnp.dot(o_ref[...], w_ref[...], preferred_element_type=F32)

        row = pl.BlockSpec((ts, d), lambda i: (i, 0))
        o = os[0].reshape(t, Q_W)
        y = pl.pallas_call(
            body1, name=f"out_proj_{tag}", grid=(t // ts,), in_specs=[row, row, _resident(w.shape)], out_specs=row,
            out_shape=jax.ShapeDtypeStruct((t, d), F32), compiler_params=_params("parallel"),
        )(x, o, w)
        return y, o, [lses[0]]

    def body(*refs):
        x_ref, w_ref, e_ref = refs[:3]
        o_refs, l_refs = refs[3:3 + ng], refs[3 + ng:3 + 2 * ng]
        y_ref, om_ref = refs[3 + 2 * ng:5 + 2 * ng]
        lt_refs = refs[5 + 2 * ng:5 + 3 * ng]
        wide_ref, narrow_ref = refs[5 + 3 * ng:]
        ls = [_merge_rows([l_refs[g][r] for r in range(dils[g])], narrow_ref, dils[g]) for g in range(ng)]
        mx = functools.reduce(jnp.maximum, ls)
        tot = mx + jnp.log(functools.reduce(lambda a, b: a + b, [jnp.exp(l_ - mx) for l_ in ls]))
        e = e_ref[...]
        o = None
        for g in range(ng):
            wt = jnp.exp(ls[g] - tot)
            hi = wt.astype(BF16)
            lo = (wt - hi.astype(F32)).astype(BF16)
            wide = jnp.dot(hi, e, preferred_element_type=F32) + jnp.dot(lo, e, preferred_element_type=F32)
            term = wide * _merge_rows([o_refs[g][r].astype(F32) for r in range(dils[g])], wide_ref, dils[g])
            o = term if o is None else o + term
        ob = o.astype(BF16)
        om_ref[...] = ob
        y_ref[...] = x_ref[...] + jnp.dot(ob, w_ref[...], preferred_element_type=F32)
        for g in range(ng):
            for r, part in enumerate(_split_rows(tot, narrow_ref, dils[g])):
                lt_refs[g][r] = part

    row = pl.BlockSpec((ts, d), lambda i: (i, 0))
    e = _head_expand()
    outs = pl.pallas_call(
        body, name=f"out_proj_{tag}", grid=(t // ts,),
        in_specs=[row, _resident(w.shape), _resident(e.shape)] + [_res_spec(seq, dl, Q_W) for dl in dils]
                 + [_res_spec(seq, dl, LANES) for dl in dils],
        out_specs=[row, pl.BlockSpec((ts, Q_W), lambda i: (i, 0))] + [_res_spec(seq, dl, LANES) for dl in dils],
        out_shape=[jax.ShapeDtypeStruct((t, d), F32), jax.ShapeDtypeStruct((t, Q_W), BF16)]
                  + [jax.ShapeDtypeStruct(_res_shape(bl, seq, dl, LANES), F32) for dl in dils],
        scratch_shapes=[_stage(ts, Q_W), _stage(ts, LANES)],
        compiler_params=_params("parallel"),
    )(x, w, e, *os, *lses)
    return outs[0], outs[1], list(outs[2:])


def _sigmoid(g):
    return 1.0 / (1.0 + jnp.exp(-g))


def _ffn_fwd(x, gain, wg, wu, wd, tag, exchange=()):
    t, d = x.shape
    f = wg.shape[1]
    tm = min(256, t)

    def body(x_ref, gain_ref, wg_ref, wu_ref, wd_ref, y_ref, g_ref, u_ref):
        xv = x_ref[...]
        h = (xv * _rms(xv) * gain_ref[...]).astype(BF16)
        g = jnp.dot(h, wg_ref[...], preferred_element_type=F32)
        u = jnp.dot(h, wu_ref[...], preferred_element_type=F32)
        g_ref[...] = g.astype(BF16)
        u_ref[...] = u.astype(BF16)
        a = (g * _sigmoid(g) * u).astype(BF16)
        y_ref[...] = xv + jnp.dot(a, wd_ref[...], preferred_element_type=F32)

    row = pl.BlockSpec((tm, d), lambda i: (i, 0))
    wide = pl.BlockSpec((tm, f), lambda i: (i, 0))
    outs, exchanged = _hosted_call(
        body, exchange, name=f"ffn_fwd_{tag}", grid=(t // tm,),
        in_specs=[row, _resident((1, d)), _resident(wg.shape), _resident(wu.shape), _resident(wd.shape)],
        out_specs=[row, wide, wide],
        out_shape=[jax.ShapeDtypeStruct((t, d), F32), jax.ShapeDtypeStruct((t, f), BF16), jax.ShapeDtypeStruct((t, f), BF16)],
        scratch_shapes=[], semantics=("parallel",), args=(x, gain, wg, wu, wd))
    return (*outs, exchanged)


def _loss_bwd(x, gain, target):
    t, d = x.shape
    tm = min(512, t)

    def body(x_ref, gain_ref, t_ref, dx_ref, loss_ref, dgain_ref):
        xv, gain_v = x_ref[...], gain_ref[...]
        xhat = xv * _rms(xv)
        err = xhat * gain_v - t_ref[...]
        dy = err * (1.0 / d)
        dx, _ = _rms_bwd(dy, xv, gain_v)
        dx_ref[...] = dx
        first = pl.program_id(0) == 0
        part = 0.5 * jnp.sum(jnp.mean(err * err, axis=-1, keepdims=True), axis=0, keepdims=True)
        _accumulate(loss_ref, jnp.broadcast_to(part, loss_ref.shape), first)
        _accumulate(dgain_ref, jnp.sum(dy * xhat, axis=0, keepdims=True), first)

    row = pl.BlockSpec((tm, d), lambda i: (i, 0))
    return pl.pallas_call(
        body, name="loss_bwd", grid=(t // tm,), in_specs=[row, _resident((1, d)), row],
        out_specs=[row, pl.BlockSpec((1, LANES), lambda i: (0, 0)), pl.BlockSpec((1, d), lambda i: (0, 0))],
        out_shape=[jax.ShapeDtypeStruct((t, d), F32), jax.ShapeDtypeStruct((1, LANES), F32), jax.ShapeDtypeStruct((1, d), F32)],
        compiler_params=_params("arbitrary"),
    )(x, gain, target)


def _ffn_bwd(dy, x, gain, g, u, wg, wu, wd, tag, exchange=()):
    t, d = x.shape
    f = wg.shape[1]
    tm = min(256, t)

    def body(dy_ref, x_ref, gain_ref, g_ref, u_ref, wg_ref, wu_ref, wd_ref, dx_ref, dg_ref, du_ref, a_ref, h_ref, dgain_ref):
        dyv = dy_ref[...]
        da = lax.dot_general(dyv.astype(BF16), wd_ref[...], NT, preferred_element_type=F32)
        gv, uv = g_ref[...].astype(F32), u_ref[...].astype(F32)
        sg = _sigmoid(gv)
        act = gv * sg
        a_ref[...] = (act * uv).astype(BF16)
        du = (da * act).astype(BF16)
        dg = (da * uv * (sg * (1.0 + gv * (1.0 - sg)))).astype(BF16)
        du_ref[...] = du
        dg_ref[...] = dg
        dh = (lax.dot_general(dg, wg_ref[...], NT, preferred_element_type=F32)
              + lax.dot_general(du, wu_ref[...], NT, preferred_element_type=F32))
        xv, gain_v = x_ref[...], gain_ref[...]
        dx, xhat = _rms_bwd(dh, xv, gain_v)
        dx_ref[...] = dyv + dx
        h_ref[...] = (xhat * gain_v).astype(BF16)
        _accumulate(dgain_ref, jnp.sum(dh * xhat, axis=0, keepdims=True), pl.program_id(0) == 0)

    row = pl.BlockSpec((tm, d), lambda i: (i, 0))
    wide = pl.BlockSpec((tm, f), lambda i: (i, 0))
    outs, exchanged = _hosted_call(
        body, exchange, name=f"ffn_bwd_{tag}", grid=(t // tm,),
        in_specs=[row, row, _resident((1, d)), wide, wide, _resident(wg.shape), _resident(wu.shape), _resident(wd.shape)],
        out_specs=[row, wide, wide, wide, row, pl.BlockSpec((1, d), lambda i: (0, 0))],
        out_shape=[jax.ShapeDtypeStruct((t, d), F32), jax.ShapeDtypeStruct((t, f), BF16), jax.ShapeDtypeStruct((t, f), BF16),
                   jax.ShapeDtypeStruct((t, f), BF16), jax.ShapeDtypeStruct((t, d), BF16), jax.ShapeDtypeStruct((1, d), F32)],
        scratch_shapes=[], semantics=("arbitrary",), args=(dy, x, gain, g, u, wg, wu, wd))
    return (*outs, exchanged)


def _tn_matmul(a, b, name, into=None, column=0, columns=1):
    t, k = a.shape
    n = b.shape[1]
    tk = k // 2 if (k // 2) % LANES == 0 else k
    tt = min(2048, t)

    def body(a_ref, b_ref, *rest):
        o_ref, acc_ref = rest[-2:]
        prod = lax.dot_general(a_ref[...].astype(BF16), b_ref[...].astype(BF16), TN, preferred_element_type=F32)
        j = pl.program_id(1)

        @pl.when(j == 0)
        def _():
            acc_ref[...] = prod

        @pl.when(j > 0)
        def _():
            acc_ref[...] += prod

        @pl.when(j == pl.num_programs(1) - 1)
        def _():
            o_ref[...] = acc_ref[...].astype(BF16)

    return pl.pallas_call(
        body, name=name, grid=(k // tk, t // tt),
        in_specs=[pl.BlockSpec((tt, tk), lambda i, j: (j, i)), pl.BlockSpec((tt, n), lambda i, j: (j, 0))]
                 + ([ANY] if into is not None else []),
        out_specs=pl.BlockSpec((tk, n), lambda i, j: (i, column)),
        out_shape=jax.ShapeDtypeStruct((k, columns * n), BF16),
        scratch_shapes=[pltpu.VMEM((tk, n), F32)],
        input_output_aliases={2: 0} if into is not None else {},
        compiler_params=_params("parallel", "arbitrary"),
    )(a, b, *([into] if into is not None else []))


def _attn_out_bwd(dx, w, o, dils, seq, tag, lse=None, sink=None):
    t, d = dx.shape
    ts = _tile_rows(seq)
    bl = t // seq
    ng = len(dils)
    has_sink = sink is not None
    expand = _head_expand().T

    def body(*refs):
        refs = list(refs)
        dx_ref, w_ref, o_ref, e_ref = refs[:4]
        refs = refs[4:]
        lse_ref, sink_ref = (refs.pop(0), refs.pop(0)) if has_sink else (None, None)
        do_refs, dl_refs = refs[:ng], refs[ng:2 * ng]
        refs = refs[2 * ng:]
        dsink_ref = refs.pop(0) if has_sink else None
        dof_ref, dlf_ref = refs
        do = lax.dot_general(dx_ref[...].astype(BF16), w_ref[...], NT, preferred_element_type=F32)
        prod = do * o_ref[...].astype(F32)
        hi = prod.astype(BF16)
        lo = (prod - hi.astype(F32)).astype(BF16)
        e = e_ref[...]
        dl = jnp.dot(hi, e, preferred_element_type=F32) + jnp.dot(lo, e, preferred_element_type=F32)
        for g in range(ng):
            for r, part in enumerate(_split_rows(do, dof_ref, dils[g])):
                do_refs[g][r] = part.astype(BF16)
            for r, part in enumerate(_split_rows(dl, dlf_ref, dils[g])):
                dl_refs[g][r] = part
        if has_sink:
            part = -jnp.exp(sink_ref[...] - lse_ref[...]) * dl
            _accumulate(dsink_ref, jnp.sum(part, axis=0, keepdims=True), pl.program_id(0) == 0)

    row = pl.BlockSpec((ts, d), lambda i: (i, 0))
    narrow = pl.BlockSpec((ts, LANES), lambda i: (i, 0))
    args = [dx, w, o, expand]
    in_specs = [row, _resident(w.shape), pl.BlockSpec((ts, Q_W), lambda i: (i, 0)), _resident(expand.shape)]
    if has_sink:
        args += [lse, jnp.pad(sink.reshape(1, N_HEADS), ((0, 0), (0, LANES - N_HEADS)))]
        in_specs += [narrow, _resident((1, LANES))]
    out_specs = [_res_spec(seq, dl, Q_W) for dl in dils] + [_res_spec(seq, dl, LANES) for dl in dils]
    out_shape = ([jax.ShapeDtypeStruct(_res_shape(bl, seq, dl, Q_W), BF16) for dl in dils]
                 + [jax.ShapeDtypeStruct(_res_shape(bl, seq, dl, LANES), F32) for dl in dils])
    if has_sink:
        out_specs.append(pl.BlockSpec((1, LANES), lambda i: (0, 0)))
        out_shape.append(jax.ShapeDtypeStruct((1, LANES), F32))
    outs = pl.pallas_call(
        body, name=f"attn_out_bwd_{tag}", grid=(t // ts,), in_specs=in_specs, out_specs=out_specs, out_shape=out_shape,
        scratch_shapes=[_stage(ts, Q_W), _stage(ts, LANES)],
        compiler_params=_params("arbitrary" if has_sink else "parallel"),
    )(*args)
    return list(outs[:ng]), list(outs[ng:2 * ng]), (outs[2 * ng] if has_sink else None)


def _attn_bwd(qkv, do, lse, delta, cos, sin, w, tag, exchange=()):
    shape = qkv.shape
    dil = shape[1]
    rows_all = _seq_view(qkv)
    nseq, length, _ = rows_all.shape
    bq = min(QUERY_BLOCK, length)
    wk = _key_rows(bq, w, length)
    nb = length // bq

    def body(qkv_ref, do_ref, lse_ref, dl_ref, cos_ref, sin_ref, dp_ref, kk_ref, vv_ref, dk_ref, dv_ref):
        _pair_variants(qkv_ref, Q_W, kk_ref)
        _pair_variants(qkv_ref, Q_W + KV_W, vv_ref)
        dk_ref[...] = jnp.zeros_like(dk_ref)
        dv_ref[...] = jnp.zeros_like(dv_ref)
        band = _band(bq, wk)
        lo_q = lax.broadcasted_iota(jnp.int32, (bq, LANES), 1) < HEAD_DIM
        hi_q = jnp.logical_not(lo_q)

        def block(i, carry):
            q0, k0 = _window(i, bq, w, wk, length)
            valid = jnp.abs(band + (q0 - k0)) <= w
            rows, krows = pl.ds(q0, bq), pl.ds(k0, wk)
            c, sn = cos_ref[rows, :], -sin_ref[rows, :]
            lse_t, dl_t = lse_ref[rows, :], dl_ref[rows, :]
            for kv in range(N_KV):
                heads = [(kv * GRP + h, h % 2) for h in range(GRP)]
                cols = [slice((kv * 2 + j) * LANES, (kv * 2 + j + 1) * LANES) for j in range(GRP // 2)]
                qp = [qkv_ref[rows, cs] for cs in cols]
                dop = [do_ref[rows, cs] for cs in cols]
                k2 = jnp.concatenate([kk_ref[kv, 0, krows, :], kk_ref[kv, 1, krows, :]], axis=0)
                v2 = jnp.concatenate([vv_ref[kv, 0, krows, :], vv_ref[kv, 1, krows, :]], axis=0)
                sc2 = [lax.dot_general(q_, k2, NT, preferred_element_type=F32) for q_ in qp]
                dp2 = [lax.dot_general(d_, v2, NT, preferred_element_type=F32) for d_ in dop]
                sc = [s_[:, half * wk:(half + 1) * wk] for s_ in sc2 for half in range(2)]
                dp = [d_[:, half * wk:(half + 1) * wk] for d_ in dp2 for half in range(2)]
                p = [jnp.exp(jnp.where(valid, s_, NEG_INF) - _over_keys(lse_t[:, hd:hd + 1], wk))
                     for s_, (hd, _) in zip(sc, heads)]
                ds = [(p_ * (dp_ - _over_keys(dl_t[:, hd:hd + 1], wk))).astype(BF16) for p_, dp_, (hd, _) in zip(p, dp, heads)]
                pb = [p_.astype(BF16) for p_ in p]
                for j in range(GRP // 2):
                    dq = jnp.dot(jnp.concatenate([ds[2 * j], ds[2 * j + 1]], axis=1), k2, preferred_element_type=F32) * SCALE
                    dp_ref[rows, cols[j]] = _rope(dq, c, sn).astype(BF16)
                zero = jnp.zeros((bq, LANES), BF16)
                q4 = jnp.concatenate([jnp.where(lo_q if h % 2 == 0 else hi_q, qp[h // 2], zero) for h in range(GRP)], axis=0)
                do4 = jnp.concatenate([jnp.where(lo_q if h % 2 == 0 else hi_q, dop[h // 2], zero) for h in range(GRP)], axis=0)
                dk_ref[kv, krows, :] += lax.dot_general(jnp.concatenate(ds, axis=0), q4, TN, preferred_element_type=F32)
                dv_ref[kv, krows, :] += lax.dot_general(jnp.concatenate(pb, axis=0), do4, TN, preferred_element_type=F32)
            return carry

        lax.fori_loop(0, nb, block, 0)
        lo = lax.broadcasted_iota(jnp.int32, (length, LANES), 1) < HEAD_DIM
        c, sn = cos_ref[...], -sin_ref[...]
        for ch in range(KV_W // LANES):
            halves = []
            for acc_ref in (dk_ref, dv_ref):
                even, odd = acc_ref[2 * ch], acc_ref[2 * ch + 1]
                even = even + pltpu.roll(even, HEAD_DIM, 1)
                odd = odd + pltpu.roll(odd, HEAD_DIM, 1)
                halves.append(jnp.where(lo, even, odd))
            dp_ref[:, Q_W + ch * LANES:Q_W + (ch + 1) * LANES] = _rope(halves[0], c, sn).astype(BF16)
            dp_ref[:, Q_W + KV_W + ch * LANES:Q_W + KV_W + (ch + 1) * LANES] = halves[1].astype(BF16)

    mode = dict(pipeline_mode=pl.Buffered(1)) if dil == 1 else {}

    def seq_block(c):
        return pl.BlockSpec((None, length, c), lambda i: (i, 0, 0), **mode)

    table = pl.BlockSpec((None, length, LANES), lambda i: (i % dil, 0, 0), **mode)
    (out,), exchanged = _hosted_call(
        body, exchange, name=f"attn_bwd_{tag}", grid=(nseq,),
        in_specs=[seq_block(QKV_W), seq_block(Q_W), seq_block(LANES), seq_block(LANES), table, table],
        out_specs=[pl.BlockSpec((None, length, QKV_W), lambda i: (i, 0, 0))],
        out_shape=[jax.ShapeDtypeStruct((nseq, length, QKV_W), BF16)],
        scratch_shapes=[pltpu.VMEM((N_KV, 2, length, LANES), BF16), pltpu.VMEM((N_KV, 2, length, LANES), BF16),
                        pltpu.VMEM((N_KV, length, LANES), F32), pltpu.VMEM((N_KV, length, LANES), F32)],
        semantics=("parallel",), args=(rows_all, _seq_view(do), _seq_view(lse), _seq_view(delta), cos, sin))
    return out.reshape(shape), exchanged


def _qkv_bwd(dy, x, gain, w, dps, dils, seq, tag, exchange=()):
    t, d = x.shape
    ts = _tile_rows(seq)
    bl = t // seq
    ng = len(dps)

    def body(dy_ref, x_ref, gain_ref, w_ref, *refs):
        dp_refs, dx_ref = refs[:ng], refs[ng]
        h_refs = refs[ng + 1:2 * ng + 1]
        dgain_ref, stage_ref = refs[2 * ng + 1:]
        dh = None
        for gi in range(ng):
            dil = dils[gi]
            n = ts // dil
            dp = dp_refs[gi][0] if dil == 1 else jnp.concatenate([dp_refs[gi][r] for r in range(dil)], axis=0)
            part = lax.dot_general(dp, w_ref[:, gi * QKV_W:(gi + 1) * QKV_W], NT, preferred_element_type=F32)
            part = _merge_rows([part[r * n:(r + 1) * n] for r in range(dil)], stage_ref, dil)
            dh = part if dh is None else dh + part
        xv, gain_v = x_ref[...], gain_ref[...]
        dx, xhat = _rms_bwd(dh, xv, gain_v)
        dx_ref[...] = dy_ref[...] + dx
        h = xhat * gain_v
        for gi in range(ng):
            for r, part in enumerate(_split_rows(h, stage_ref, dils[gi])):
                h_refs[gi][r] = part.astype(BF16)
        _accumulate(dgain_ref, jnp.sum(dh * xhat, axis=0, keepdims=True), pl.program_id(0) == 0)

    row = pl.BlockSpec((ts, d), lambda i: (i, 0))
    outs, exchanged = _hosted_call(
        body, exchange, name=f"qkv_bwd_{tag}", grid=(t // ts,),
        in_specs=[row, row, _resident((1, d)), _resident(w.shape)] + [_res_spec(seq, dl, QKV_W) for dl in dils],
        out_specs=[row] + [_res_spec(seq, dl, d) for dl in dils] + [pl.BlockSpec((1, d), lambda i: (0, 0))],
        out_shape=[jax.ShapeDtypeStruct((t, d), F32)] + [jax.ShapeDtypeStruct(_res_shape(bl, seq, dl, d), BF16) for dl in dils]
                  + [jax.ShapeDtypeStruct((1, d), F32)],
        scratch_shapes=[_stage(ts, d)], semantics=("arbitrary",), args=(dy, x, gain, w, *dps))
    return outs[0], list(outs[1:1 + ng]), outs[1 + ng], exchanged


ANY = pl.BlockSpec(memory_space=pl.ANY)


def _place():
    x, y, c = lax.axis_index("x"), lax.axis_index("y"), lax.axis_index("c")
    return x, y, c


def _all_gather(shards):
    n = len(shards)

    def body(*refs):
        ins, outs = refs[:n], refs[n:2 * n]
        send_sems, recv_sems, local_sems = refs[2 * n:]
        x, y, c = _place()
        sibling = (x, y, 1 - c)
        chips = [(1 - x, y), (x, 1 - y), (1 - x, 1 - y)]

        def copy(a, k, block, to, src=None):
            px, py, pc = block
            rows = outs[a].at[4 * px + 2 * py + pc]
            return pltpu.make_async_remote_copy(
                src_ref=rows if src is None else src, dst_ref=rows, send_sem=send_sems.at[a, k], recv_sem=recv_sems.at[a, k],
                device_id=to, device_id_type=MESH)

        sent = []
        for a in range(n):
            mine = pltpu.make_async_copy(ins[a], outs[a].at[4 * x + 2 * y + c], local_sems.at[a])
            mine.start()
            sent.append(mine)
        for a in range(n):
            first = [copy(a, 0, (x, y, c), sibling, src=ins[a])]
            first += [copy(a, 1 + j, (x, y, c), (*chip, c), src=ins[a]) for j, chip in enumerate(chips)]
            for cp in first:
                cp.start()
            sent += first
        for a in range(n):
            for j, chip in enumerate(chips):
                copy(a, 1 + j, (*chip, c), (x, y, c)).wait_recv()
                passed = copy(a, 4 + j, (*chip, c), sibling)
                passed.start()
                sent.append(passed)
        for a in range(n):
            copy(a, 0, sibling, (x, y, c)).wait_recv()
            for j, chip in enumerate(chips):
                copy(a, 4 + j, (*chip, 1 - c), (x, y, c)).wait_recv()
        for cp in sent[n:]:
            cp.wait_send()
        for mine in sent[:n]:
            mine.wait()

    return pl.pallas_call(
        body, name="all_gather_weights", in_specs=[ANY] * n, out_specs=[ANY] * n,
        out_shape=[jax.ShapeDtypeStruct((N_DEV,) + s.shape, s.dtype) for s in shards],
        scratch_shapes=[pltpu.SemaphoreType.DMA((n, 7)), pltpu.SemaphoreType.DMA((n, 7)), pltpu.SemaphoreType.DMA((n,))],
    )(*shards)


def _exchange_copies(srcs, dsts, gather, send_sems, recv_sems, local_sems):
    x, y, c = _place()
    me = 4 * x + 2 * y + c
    copies = []
    for a, (src, dst) in enumerate(zip(srcs, dsts)):
        copies.append(pltpu.make_async_copy(src if gather[a] else src.at[me], dst.at[me], local_sems.at[a]))
        for k in range(1, N_DEV):
            peer = me ^ k
            copies.append(pltpu.make_async_remote_copy(
                src_ref=src if gather[a] else src.at[peer], dst_ref=dst.at[me], send_sem=send_sems.at[a, k - 1],
                recv_sem=recv_sems.at[a, k - 1], device_id=(peer // 4, (peer // 2) % 2, peer % 2), device_id_type=MESH))
    return copies


def _exchange_scratch(n):
    return [pltpu.SemaphoreType.DMA((n, N_DEV - 1)), pltpu.SemaphoreType.DMA((n, N_DEV - 1)), pltpu.SemaphoreType.DMA((n,))]


def _exchanged_shapes(exchange):
    return [jax.ShapeDtypeStruct(((N_DEV,) + a.shape) if g else a.shape, a.dtype) for a, g in exchange]


def _hosted_call(body, exchange, *, name, grid, in_specs, out_specs, out_shape, scratch_shapes, semantics, args):
    out_specs, out_shape, scratch = list(out_specs), list(out_shape), list(scratch_shapes)
    if not exchange:
        outs = pl.pallas_call(body, name=name, grid=grid, in_specs=in_specs, out_specs=out_specs, out_shape=out_shape,
                              scratch_shapes=scratch, compiler_params=_params(*semantics))(*args)
        return list(outs), []
    n, n_in, n_out, n_scr = len(exchange), len(in_specs), len(out_specs), len(scratch)
    gather = [g for _, g in exchange]

    def hosted(*refs):
        own_in, x_in = refs[:n_in], refs[n_in:n_in + n]
        own_out, x_out = refs[n_in + n:n_in + n + n_out], refs[n_in + n + n_out:n_in + 2 * n + n_out]
        own_scr, sems = refs[n_in + 2 * n + n_out:n_in + 2 * n + n_out + n_scr], refs[-3:]

        @pl.when(pl.program_id(0) == 0)
        def _():
            for cp in _exchange_copies(x_in, x_out, gather, *sems):
                cp.start()

        body(*own_in, *own_out, *own_scr)

        @pl.when(pl.program_id(0) == pl.num_programs(0) - 1)
        def _():
            for cp in _exchange_copies(x_in, x_out, gather, *sems):
                cp.wait()

    outs = pl.pallas_call(
        hosted, name=name, grid=grid, in_specs=list(in_specs) + [ANY] * n, out_specs=out_specs + [ANY] * n,
        out_shape=out_shape + _exchanged_shapes(exchange), scratch_shapes=scratch + _exchange_scratch(n),
        compiler_params=_params("arbitrary"),
    )(*args, *[a for a, _ in exchange])
    return list(outs[:n_out]), list(outs[n_out:])


def _exchange_now(exchange):
    n = len(exchange)
    gather = [g for _, g in exchange]

    def body(*refs):
        copies = _exchange_copies(refs[:n], refs[n:2 * n], gather, *refs[2 * n:])
        for cp in copies:
            cp.start()
        for cp in copies:
            cp.wait()

    return pl.pallas_call(
        body, name="exchange_last", in_specs=[ANY] * n, out_specs=[ANY] * n, out_shape=_exchanged_shapes(exchange),
        scratch_shapes=_exchange_scratch(n),
    )(*[a for a, _ in exchange])


def _all_reduce_small(v):
    def body(v_ref, o_ref, recv_ref, send_sems, recv_sems):
        x, y, c = _place()
        me = 4 * x + 2 * y + c
        copies = []
        for k in range(1, N_DEV):
            peer = me ^ k
            copies.append(pltpu.make_async_remote_copy(
                src_ref=v_ref, dst_ref=recv_ref.at[k], send_sem=send_sems.at[k - 1], recv_sem=recv_sems.at[k - 1],
                device_id=(peer // 4, (peer // 2) % 2, peer % 2), device_id_type=MESH))
        for cp in copies:
            cp.start()
        recv_ref[0] = v_ref[...]
        for cp in copies:
            cp.wait()
        acc = recv_ref[me]
        for src in range(1, N_DEV):
            acc = acc + recv_ref[me ^ src]
        o_ref[...] = acc

    vm = pl.BlockSpec(memory_space=pltpu.VMEM)
    return pl.pallas_call(
        body, name="all_reduce_small", in_specs=[vm], out_specs=vm, out_shape=jax.ShapeDtypeStruct(v.shape, F32),
        scratch_shapes=[pltpu.VMEM((N_DEV,) + v.shape, F32), pltpu.SemaphoreType.DMA((N_DEV - 1,)),
                        pltpu.SemaphoreType.DMA((N_DEV - 1,))],
    )(v)


def _adamw_math(w, g, m, v):
    m = ADAM_B1 * m + (1.0 - ADAM_B1) * g
    v = ADAM_B2 * v + (1.0 - ADAM_B2) * (g * g)
    m_hat = m / (1.0 - ADAM_B1 ** ADAM_STEP)
    v_hat = v / (1.0 - ADAM_B2 ** ADAM_STEP)
    delta = -ADAM_LR * (m_hat / (jnp.sqrt(v_hat) + ADAM_EPS) + ADAM_WD * w)
    return delta, m, v


def _adamw(parts, w, m, v, name, layer=None, into=None):
    r, c = w.shape[-2:]
    tr = r // 2 if r % 16 == 0 and r >= 256 else r
    n = len(parts)

    def body(*refs):
        w_ref, m_ref, v_ref = refs[n:n + 3]
        g_ref, d_ref, nm_ref, nv_ref = refs[-4:]
        g = refs[0][...].astype(F32)
        for p_ref in refs[1:n]:
            g = g + p_ref[...].astype(F32)
        g_ref[...] = g
        d_ref[...], nm_ref[...], nv_ref[...] = _adamw_math(w_ref[...], g, m_ref[...], v_ref[...])

    def slab(slot):
        return pl.BlockSpec((None, tr, c), lambda i: (slot, i, 0))

    tile = pl.BlockSpec((tr, c), lambda i: (i, 0)) if layer is None else slab(layer)
    arrays, in_specs = [], []
    for p in parts:
        if isinstance(p, tuple):
            arrays.append(p[0])
            in_specs.append(slab(p[1]))
        else:
            arrays.append(p)
            in_specs.append(tile)
    kept = list(into) if into is not None else []
    return pl.pallas_call(
        body, name=name, grid=(r // tr,), in_specs=in_specs + [tile] * 3 + [ANY] * len(kept), out_specs=[tile] * 4,
        out_shape=[jax.ShapeDtypeStruct(w.shape, F32)] * 4,
        input_output_aliases={n + 3 + k: k for k in range(len(kept))}, compiler_params=_params("parallel"),
    )(*arrays, w, m, v, *kept)


def _columns(g):
    return g.transpose(1, 0, 2).reshape(g.shape[1], -1)


def _rows(g):
    return g.reshape(-1, g.shape[-1])


def _column_blocks(dw):
    k, n = dw.shape
    return dw.reshape(k, N_DEV, n // N_DEV).transpose(1, 0, 2)


def _row_blocks(dw):
    k, n = dw.shape
    return dw.reshape(N_DEV, k // N_DEV, n)


def _pack_rows(rows, width):
    out = None
    for i, r in enumerate(rows):
        r = r.reshape(1, -1).astype(F32)
        r = jnp.pad(r, ((i, 8 - 1 - i), (0, width - r.shape[1])))
        out = r if out is None else out + r
    return out


def _mixer_fwd(x, gain, w_in, w_out, cos, sin, seq, groups, tag, sink=None, exchanges=None):
    qkvs, os, lses, got = [], [], [], {}
    for gi, (dil, w) in enumerate(groups):
        qkv, got["proj", gi] = _qkv_proj(x, gain, w_in, _tables_tiled(cos, seq, dil), _tables_tiled(sin, seq, dil), seq, dil, gi,
                                         f"{tag}{gi}", exchange=(exchanges or {}).get(("proj", gi), ()))
        o, lse, got[gi] = _attn_fwd(qkv, w, f"{tag}{gi}", sink=sink, exchange=(exchanges or {}).get(gi, ()))
        qkvs.append(qkv)
        os.append(o)
        lses.append(lse)
    y, o, lses = _out_proj(x, os, lses, [dl for dl, _ in groups], w_out, seq, tag)
    return y, (qkvs, o, lses), got


def _mixer_bwd(dy, x_in, gain, w_in, w_out, saved, cos, sin, seq, groups, tag, sink=None, exchange=(), scatter_dw_out=False):
    qkvs, o, lses = saved
    t, d = x_in.shape
    dils = [dl for dl, _ in groups]
    lse_tokens = lses[0].reshape(t, LANES) if sink is not None else None
    dos, dls, dsink = _attn_out_bwd(dy, w_out, o, dils, seq, tag, lse=lse_tokens, sink=sink)
    dw_out = _tn_matmul(o, dy, f"dw_out_{tag}")
    if scatter_dw_out:
        exchange = list(exchange) + _to_send([], [dw_out])
    dps, got = [], []
    for gi, (dil, w) in enumerate(groups):
        dp, brought = _attn_bwd(qkvs[gi], dos[gi], lses[gi], dls[gi], _tables_by_residue(cos, seq, dil),
                                _tables_by_residue(sin, seq, dil), w, f"{tag}{gi}", exchange=exchange if gi == 0 else ())
        dps.append(dp)
        got += brought
    if scatter_dw_out:
        dw_out = got.pop()
    dx, hs, dgain, _ = _qkv_bwd(dy, x_in, gain, w_in, dps, dils, seq, tag)
    dw_in = None
    for gi in range(len(groups)):
        dw_in = _tn_matmul(hs[gi].reshape(t, d), dps[gi].reshape(t, QKV_W), f"dw_in_{tag}{gi}", into=dw_in, column=gi,
                           columns=len(groups))
    return dx, dw_in, dw_out, dgain, dsink, got


def _ffn_layer_bwd(dy, x_in, gain, g, u, wg, wu, wd, tag, exchange=()):
    dx, dg, du, act, h, dgain, got = _ffn_bwd(dy, x_in, gain, g, u, wg, wu, wd, tag, exchange=exchange)
    dwd = _tn_matmul(act, dy, f"dw_down_{tag}")
    dwg = _tn_matmul(h, dg, f"dw_gate_{tag}")
    dwu = _tn_matmul(h, du, f"dw_up_{tag}")
    return dx, dwg, dwu, dwd, dgain, got


def _to_send(dws_by_columns, dws_by_rows):
    return [(_column_blocks(g), False) for g in dws_by_columns] + [(_row_blocks(g), False) for g in dws_by_rows]


def kernel(x, a_w_in, a_sink, a_w_out, b_w_in, b_w_out, norm_mix, norm_ffn, w_gate, w_up, w_down, final_norm, loss_target, m_a_w_in, m_a_sink, m_a_w_out, m_b_w_in, m_b_w_out, m_norm_mix, m_norm_ffn, m_w_gate, m_w_up, m_w_down, m_final_norm, v_a_w_in, v_a_sink, v_a_w_out, v_b_w_in, v_b_w_out, v_norm_mix, v_norm_ffn, v_w_gate, v_w_up, v_w_down, v_final_norm):
    bl, seq, d = x.shape
    t = bl * seq
    xf = x.reshape(t, d)
    target = loss_target.reshape(t, d)
    cos, sin = _rope_tables(seq)
    groups_a = [(1, ATTN_HALF_WINDOW)]
    groups_b = [(dil, window // 2 // dil) for window, dil in DILATED_GROUPS]

    def shard(w_):
        return w_.astype(BF16)

    wa_in, wa_out = _all_gather([shard(a_w_in[0]), shard(a_w_out[0])])
    wa_in, wa_out = _columns(wa_in), _rows(wa_out)

    x1_0, saved_a, got = _mixer_fwd(xf, norm_mix[0:1], wa_in, wa_out, cos, sin, seq, groups_a, "a", sink=a_sink[0],
                                    exchanges={("proj", 0): [(shard(w_down[0]), True)],
                                               0: [(shard(w_gate[0]), True), (shard(w_up[0]), True)]})
    wg0, wu0, wd0 = _columns(got[0][0]), _columns(got[0][1]), _rows(got["proj", 0][0])
    x2_0, g0, u0, got = _ffn_fwd(x1_0, norm_ffn[0:1], wg0, wu0, wd0, "0",
                                 exchange=[(shard(b_w_in[0]), True), (shard(b_w_out[0]), True)])
    wb_in, wb_out = _columns(got[0]), _rows(got[1])
    x1_1, saved_b, got = _mixer_fwd(x2_0, norm_mix[1:2], wb_in, wb_out, cos, sin, seq, groups_b, "b",
                                    exchanges={0: [(shard(w_gate[1]), True), (shard(w_up[1]), True)], 1: [(shard(w_down[1]), True)]})
    wg1, wu1, wd1 = _columns(got[0][0]), _columns(got[0][1]), _rows(got[1][0])
    x2_1, g1, u1, _ = _ffn_fwd(x1_1, norm_ffn[1:2], wg1, wu1, wd1, "1")

    dy, loss_part, d_final = _loss_bwd(x2_1, final_norm.reshape(1, d), target)
    dy, dwg1, dwu1, dwd1, d_nf1, _ = _ffn_layer_bwd(dy, x1_1, norm_ffn[1:2], g1, u1, wg1, wu1, wd1, "1")
    dy, dwb_in, dwb_out, d_nm1, _, (r_g1, r_u1, r_d1) = _mixer_bwd(
        dy, x2_0, norm_mix[1:2], wb_in, wb_out, saved_b, cos, sin, seq, groups_b, "b", exchange=_to_send([dwg1, dwu1], [dwd1]))
    dy, dwg0, dwu0, dwd0, d_nf0, (r_b_in, r_b_out) = _ffn_layer_bwd(
        dy, x1_0, norm_ffn[0:1], g0, u0, wg0, wu0, wd0, "0", exchange=_to_send([dwb_in], [dwb_out]))
    dy, dwa_in, r_a_out, d_nm0, d_sink, (r_g0, r_u0, r_d0) = _mixer_bwd(
        dy, xf, norm_mix[0:1], wa_in, wa_out, saved_a, cos, sin, seq, groups_a, "a", sink=a_sink[0],
        exchange=_to_send([dwg0, dwu0], [dwd0]), scatter_dw_out=True)
    (r_a_in,) = _exchange_now(_to_send([dwa_in], []))
    grad_x = dy.reshape(bl, seq, d)

    def update(received, w_, m_, v_, name):
        out = None
        for layer in reversed(range(len(received))):
            out = _adamw([(received[layer], src) for src in range(N_DEV)], w_, m_, v_, f"adamw_{name}{layer}", layer=layer, into=out)
        return out

    u_a_in = update([r_a_in], a_w_in, m_a_w_in, v_a_w_in, "a_in")
    u_a_out = update([r_a_out], a_w_out, m_a_w_out, v_a_w_out, "a_out")
    u_b_in = update([r_b_in], b_w_in, m_b_w_in, v_b_w_in, "b_in")
    u_b_out = update([r_b_out], b_w_out, m_b_w_out, v_b_w_out, "b_out")
    u_gate = update([r_g0, r_g1], w_gate, m_w_gate, v_w_gate, "gate")
    u_up = update([r_u0, r_u1], w_up, m_w_up, v_w_up, "up")
    u_down = update([r_d0, r_d1], w_down, m_w_down, v_w_down, "down")

    small = _pack_rows([d_nm0, d_nm1, d_nf0, d_nf1, d_final, d_sink, loss_part], d)
    total = _all_reduce_small(small)
    small_w = _pack_rows([norm_mix[0], norm_mix[1], norm_ffn[0], norm_ffn[1], final_norm, a_sink], d)
    small_m = _pack_rows([m_norm_mix[0], m_norm_mix[1], m_norm_ffn[0], m_norm_ffn[1], m_final_norm, m_a_sink], d)
    small_v = _pack_rows([v_norm_mix[0], v_norm_mix[1], v_norm_ffn[0], v_norm_ffn[1], v_final_norm, v_a_sink], d)
    u_small = _adamw([total], small_w, small_m, small_v, "adamw_small")
    loss = total[6, 0]

    outs = []
    for k in range(4):
        sm = u_small[k]
        outs += [u_a_in[k], sm[5:6, :N_HEADS], u_a_out[k], u_b_in[k], u_b_out[k], sm[0:2], sm[2:4],
                 u_gate[k], u_up[k], u_down[k], sm[4]]
    return (loss, grad_x, *outs)
```

```python
import functools
import math

import jax
import jax.numpy as jnp
from jax import lax
from jax.experimental import pallas as pl
from jax.experimental.pallas import tpu as pltpu

F32 = jnp.float32
BF16 = jnp.bfloat16

HEAD_DIM = 64
N_HEADS = 16
N_KV = 4
GRP = N_HEADS // N_KV
Q_W = N_HEADS * HEAD_DIM
KV_W = N_KV * HEAD_DIM
QKV_W = Q_W + 2 * KV_W
ATTN_HALF_WINDOW = 128
DILATED_GROUPS = ((128, 1), (512, 4), (2048, 16))
ROPE_THETA = 10000.0
RMS_EPS = 1e-6
NEG_INF = -1e30
SCALE = 1.0 / math.sqrt(HEAD_DIM)

ADAM_LR = 0.001
ADAM_B1 = 0.9
ADAM_B2 = 0.999
ADAM_EPS = 1e-08
ADAM_WD = 0.01
ADAM_STEP = 10

LANES = 128
VMEM_LIMIT = 56 * 1024 * 1024
QUERY_BLOCK = 128
N_DEV = 8
MESH = pl.DeviceIdType.MESH

NT = (((1,), (1,)), ((), ()))
TN = (((0,), (0,)), ((), ()))


def _params(*sem):
    return pltpu.CompilerParams(dimension_semantics=tuple(sem) if sem else None, vmem_limit_bytes=VMEM_LIMIT)


def _resident(shape):
    return pl.BlockSpec(shape, lambda *_: (0,) * len(shape), pipeline_mode=pl.Buffered(1))


def _rope_tables(seq):
    inv_freq = 1.0 / (ROPE_THETA ** (jnp.arange(0, HEAD_DIM, 2, dtype=F32) / HEAD_DIM))
    ang = jnp.arange(seq, dtype=F32)[:, None] * inv_freq[None, :]
    cos, sin = jnp.cos(ang), jnp.sin(ang)
    return jnp.tile(cos, (1, 4)), jnp.concatenate([-sin, sin, -sin, sin], axis=1)


def _rope(t, cos, sin_signed):
    lane = lax.broadcasted_iota(jnp.int32, t.shape, 1)
    first = (lane & (HEAD_DIM // 2)) == 0
    swapped = jnp.where(first, pltpu.roll(t, LANES - HEAD_DIM // 2, 1), pltpu.roll(t, HEAD_DIM // 2, 1))
    return t * cos + swapped * sin_signed


def _rms(x):
    return lax.rsqrt(jnp.mean(x * x, axis=-1, keepdims=True) + RMS_EPS)


def _rms_bwd(dh, x, gain):
    r = _rms(x)
    xhat = x * r
    dxh = dh * gain
    dx = r * (dxh - xhat * jnp.mean(dxh * xhat, axis=-1, keepdims=True))
    return dx, xhat


def _accumulate(ref, value, first):
    @pl.when(first)
    def _():
        ref[...] = jnp.zeros_like(ref)

    ref[...] += value


def _tile_rows(seq):
    return min(512, seq)


def _res_shape(bl, seq, dil, c):
    ts = _tile_rows(seq)
    return (bl, dil, seq // ts, ts // dil, c)


def _res_spec(seq, dil, c):
    ts = _tile_rows(seq)
    per_seq = seq // ts
    return pl.BlockSpec((None, dil, None, ts // dil, c), lambda i: (i // per_seq, 0, i % per_seq, 0, 0))


def _seq_view(a):
    bl, dil, tiles, n, c = a.shape
    return a.reshape(bl * dil, tiles * n, c)


def _stage(ts, c):
    return pltpu.VMEM((c // LANES, ts, LANES), F32)


def _split_rows(val, stage_ref, dil):
    if dil == 1:
        return [val]
    ts, c = val.shape
    n, nc = ts // dil, c // LANES
    for k in range(nc):
        stage_ref[k] = val[:, k * LANES:(k + 1) * LANES]
    return [jnp.concatenate([stage_ref[k, pl.ds(r, n, stride=dil), :] for k in range(nc)], axis=1) for r in range(dil)]


def _merge_rows(parts, stage_ref, dil):
    if dil == 1:
        return parts[0]
    n, c = parts[0].shape
    nc = c // LANES
    for r, part in enumerate(parts):
        for k in range(nc):
            stage_ref[k, pl.ds(r, n, stride=dil), :] = part[:, k * LANES:(k + 1) * LANES]
    return jnp.concatenate([stage_ref[k] for k in range(nc)], axis=1)


def _tables_tiled(table, seq, dil):
    ts = _tile_rows(seq)
    return table.reshape(seq // ts, ts // dil, dil, LANES).transpose(0, 2, 1, 3).reshape(seq, LANES)


def _tables_by_residue(table, seq, dil):
    return table.reshape(seq // dil, dil, LANES).transpose(1, 0, 2)


def _qkv_proj(x, gain, w, cos, sin, seq, dil, group, tag, exchange=()):
    t, d = x.shape
    ts = _tile_rows(seq)
    n = ts // dil
    per_seq = seq // ts

    def body(x_ref, g_ref, w_ref, cos_ref, sin_ref, o_ref, stage_ref):
        xv = jnp.concatenate(_split_rows(x_ref[...], stage_ref, dil), axis=0)
        h = (xv * _rms(xv) * g_ref[...]).astype(BF16)
        acc = jnp.dot(h, w_ref[...], preferred_element_type=F32)
        c, s = cos_ref[...], sin_ref[...]
        for j in range(QKV_W // LANES):
            cols = slice(j * LANES, (j + 1) * LANES)
            val = acc[:, cols]
            if j < (Q_W + KV_W) // LANES:
                val = _rope(val, c, s)
            if j < Q_W // LANES:
                val = val * SCALE
            val = val.astype(BF16)
            for r in range(dil):
                o_ref[r, :, cols] = val[r * n:(r + 1) * n]

    table = pl.BlockSpec((ts, LANES), lambda i: (i % per_seq, 0))
    (qkv,), exchanged = _hosted_call(
        body, exchange, name=f"qkv_proj_{tag}", grid=(t // ts,),
        in_specs=[pl.BlockSpec((ts, d), lambda i: (i, 0)), pl.BlockSpec((1, d), lambda i: (0, 0)),
                  pl.BlockSpec((d, QKV_W), lambda i: (0, group)), table, table],
        out_specs=[_res_spec(seq, dil, QKV_W)],
        out_shape=[jax.ShapeDtypeStruct(_res_shape(t // seq, seq, dil, QKV_W), BF16)],
        scratch_shapes=[_stage(ts, d)], semantics=("parallel",), args=(x, gain, w, cos, sin))
    return qkv, exchanged


def _band(bq, wk):
    return lax.broadcasted_iota(jnp.int32, (bq, wk), 0) - lax.broadcasted_iota(jnp.int32, (bq, wk), 1)


def _pair_variants(src_ref, base, dst_ref):
    lo = lax.broadcasted_iota(jnp.int32, (src_ref.shape[0], LANES), 1) < HEAD_DIM
    for c in range(KV_W // LANES):
        chunk = src_ref[:, base + c * LANES:base + (c + 1) * LANES]
        rolled = pltpu.roll(chunk, HEAD_DIM, 1)
        zero = jnp.zeros_like(chunk)
        dst_ref[2 * c, 0] = jnp.where(lo, chunk, zero)
        dst_ref[2 * c, 1] = jnp.where(lo, zero, rolled)
        dst_ref[2 * c + 1, 0] = jnp.where(lo, rolled, zero)
        dst_ref[2 * c + 1, 1] = jnp.where(lo, zero, chunk)


def _over_keys(col, wk):
    if wk % LANES:
        return jnp.broadcast_to(col, (col.shape[0], wk))
    wide = jnp.broadcast_to(col, (col.shape[0], LANES))
    return wide if wk == LANES else jnp.concatenate([wide] * (wk // LANES), axis=1)


def _key_rows(bq, w, length):
    return min(bq + 2 * w, length)


def _window(i, bq, w, wk, length):
    q0 = pl.multiple_of(i * bq, bq)
    k0 = pl.multiple_of(jnp.clip(q0 - w, 0, length - wk), min(w, bq))
    return q0, k0


def _attn_fwd(qkv, w, tag, sink=None, exchange=()):
    shape = qkv.shape
    rows_all = _seq_view(qkv)
    nseq, length, _ = rows_all.shape
    bq = min(QUERY_BLOCK, length)
    wk = _key_rows(bq, w, length)
    nb = length // bq
    has_sink = sink is not None

    def body(*refs):
        qkv_ref = refs[0]
        sink_ref = refs[1] if has_sink else None
        o_ref, lse_ref, kk_ref, vv_ref = refs[-4:]
        _pair_variants(qkv_ref, Q_W, kk_ref)
        _pair_variants(qkv_ref, Q_W + KV_W, vv_ref)
        band = _band(bq, wk)
        lane = lax.broadcasted_iota(jnp.int32, (bq, LANES), 1)
        lo = lane < HEAD_DIM

        def block(i, carry):
            q0, k0 = _window(i, bq, w, wk, length)
            valid = jnp.abs(band + (q0 - k0)) <= w
            rows, krows = pl.ds(q0, bq), pl.ds(k0, wk)
            lse_tile = jnp.zeros((bq, LANES), F32)
            for kv in range(N_KV):
                heads = [(kv * GRP + h, h % 2) for h in range(GRP)]
                qp = [qkv_ref[rows, (kv * 2 + j) * LANES:(kv * 2 + j + 1) * LANES] for j in range(GRP // 2)]
                k2 = jnp.concatenate([kk_ref[kv, 0, krows, :], kk_ref[kv, 1, krows, :]], axis=0)
                v2 = jnp.concatenate([vv_ref[kv, 0, krows, :], vv_ref[kv, 1, krows, :]], axis=0)
                sc2 = [lax.dot_general(q_, k2, NT, preferred_element_type=F32) for q_ in qp]
                sc = [jnp.where(valid, s_[:, half * wk:(half + 1) * wk], NEG_INF) for s_ in sc2 for half in range(2)]
                m = [jnp.max(s_, axis=-1, keepdims=True) for s_ in sc]
                if has_sink:
                    m = [jnp.maximum(m_, sink_ref[hd]) for m_, (hd, _) in zip(m, heads)]
                mb = [jnp.broadcast_to(m_, (bq, LANES)) for m_ in m]
                p = [jnp.exp(s_ - _over_keys(m_, wk)) for s_, m_ in zip(sc, m)]
                den = [jnp.sum(p_, axis=-1, keepdims=True) for p_ in p]
                if has_sink:
                    den = [d_ + jnp.exp(sink_ref[hd] - m_) for d_, m_, (hd, _) in zip(den, m, heads)]
                inv = [jnp.broadcast_to(1.0 / d_, (bq, LANES)) for d_ in den]
                pb = [p_.astype(BF16) for p_ in p]
                for j in range(GRP // 2):
                    o = jnp.dot(jnp.concatenate([pb[2 * j], pb[2 * j + 1]], axis=1), v2, preferred_element_type=F32)
                    o = o * jnp.where(lo, inv[2 * j], inv[2 * j + 1])
                    o_ref[rows, (kv * 2 + j) * LANES:(kv * 2 + j + 1) * LANES] = o.astype(BF16)
                for h, (hd, _) in enumerate(heads):
                    lse_tile = jnp.where(lane == hd, mb[h] - jnp.log(inv[h]), lse_tile)
            lse_ref[rows, :] = lse_tile
            return carry

        lax.fori_loop(0, nb, block, 0)

    args = [rows_all]
    in_specs = [pl.BlockSpec((None, length, QKV_W), lambda i: (i, 0, 0))]
    if has_sink:
        args.append(sink)
        in_specs.append(pl.BlockSpec(memory_space=pltpu.SMEM))
    (o, lse), exchanged = _hosted_call(
        body, exchange, name=f"attn_fwd_{tag}", grid=(nseq,), in_specs=in_specs,
        out_specs=[pl.BlockSpec((None, length, Q_W), lambda i: (i, 0, 0)), pl.BlockSpec((None, length, LANES), lambda i: (i, 0, 0))],
        out_shape=[jax.ShapeDtypeStruct((nseq, length, Q_W), BF16), jax.ShapeDtypeStruct((nseq, length, LANES), F32)],
        scratch_shapes=[pltpu.VMEM((N_KV, 2, length, LANES), BF16), pltpu.VMEM((N_KV, 2, length, LANES), BF16)],
        semantics=("parallel",), args=args)
    return o.reshape(shape[:-1] + (Q_W,)), lse.reshape(shape[:-1] + (LANES,)), exchanged


def _head_expand():
    return (jnp.arange(LANES)[:, None] == jnp.arange(Q_W)[None, :] // HEAD_DIM).astype(BF16)


def _out_proj(x, os, lses, dils, w, seq, tag):
    t, d = x.shape
    ts = _tile_rows(seq)
    ng = len(os)
    bl = t // seq
    if ng == 1:
        def body1(x_ref, o_ref, w_ref, y_ref):
            y_ref[...] = x_ref[...] + jnp.dot(o_ref[...], w_ref[...], preferred_element_type=F32)

        row = pl.BlockSpec((ts, d), lambda i: (i, 0))
        o = os[0].reshape(t, Q_W)
        y = pl.pallas_call(
            body1, name=f"out_proj_{tag}", grid=(t // ts,), in_specs=[row, row, _resident(w.shape)], out_specs=row,
            out_shape=jax.ShapeDtypeStruct((t, d), F32), compiler_params=_params("parallel"),
        )(x, o, w)
        return y, o, [lses[0]]

    def body(*refs):
        x_ref, w_ref, e_ref = refs[:3]
        o_refs, l_refs = refs[3:3 + ng], refs[3 + ng:3 + 2 * ng]
        y_ref, om_ref = refs[3 + 2 * ng:5 + 2 * ng]
        lt_refs = refs[5 + 2 * ng:5 + 3 * ng]
        wide_ref, narrow_ref = refs[5 + 3 * ng:]
        ls = [_merge_rows([l_refs[g][r] for r in range(dils[g])], narrow_ref, dils[g]) for g in range(ng)]
        mx = functools.reduce(jnp.maximum, ls)
        tot = mx + jnp.log(functools.reduce(lambda a, b: a + b, [jnp.exp(l_ - mx) for l_ in ls]))
        e = e_ref[...]
        o = None
        for g in range(ng):
            wt = jnp.exp(ls[g] - tot)
            hi = wt.astype(BF16)
            lo = (wt - hi.astype(F32)).astype(BF16)
            wide = jnp.dot(hi, e, preferred_element_type=F32) + jnp.dot(lo, e, preferred_element_type=F32)
            term = wide * _merge_rows([o_refs[g][r].astype(F32) for r in range(dils[g])], wide_ref, dils[g])
            o = term if o is None else o + term
        ob = o.astype(BF16)
        om_ref[...] = ob
        y_ref[...] = x_ref[...] + jnp.dot(ob, w_ref[...], preferred_element_type=F32)
        for g in range(ng):
            for r, part in enumerate(_split_rows(tot, narrow_ref, dils[g])):
                lt_refs[g][r] = part

    row = pl.BlockSpec((ts, d), lambda i: (i, 0))
    e = _head_expand()
    outs = pl.pallas_call(
        body, name=f"out_proj_{tag}", grid=(t // ts,),
        in_specs=[row, _resident(w.shape), _resident(e.shape)] + [_res_spec(seq, dl, Q_W) for dl in dils]
                 + [_res_spec(seq, dl, LANES) for dl in dils],
        out_specs=[row, pl.BlockSpec((ts, Q_W), lambda i: (i, 0))] + [_res_spec(seq, dl, LANES) for dl in dils],
        out_shape=[jax.ShapeDtypeStruct((t, d), F32), jax.ShapeDtypeStruct((t, Q_W), BF16)]
                  + [jax.ShapeDtypeStruct(_res_shape(bl, seq, dl, LANES), F32) for dl in dils],
        scratch_shapes=[_stage(ts, Q_W), _stage(ts, LANES)],
        compiler_params=_params("parallel"),
    )(x, w, e, *os, *lses)
    return outs[0], outs[1], list(outs[2:])


def _sigmoid(g):
    return 1.0 / (1.0 + jnp.exp(-g))


def _ffn_fwd(x, gain, wg, wu, wd, tag, exchange=()):
    t, d = x.shape
    f = wg.shape[1]
    tm = min(256, t)

    def body(x_ref, gain_ref, wg_ref, wu_ref, wd_ref, y_ref, g_ref, u_ref):
        xv = x_ref[...]
        h = (xv * _rms(xv) * gain_ref[...]).astype(BF16)
        g = jnp.dot(h, wg_ref[...], preferred_element_type=F32)
        u = jnp.dot(h, wu_ref[...], preferred_element_type=F32)
        g_ref[...] = g.astype(BF16)
        u_ref[...] = u.astype(BF16)
        a = (g * _sigmoid(g) * u).astype(BF16)
        y_ref[...] = xv + jnp.dot(a, wd_ref[...], preferred_element_type=F32)

    row = pl.BlockSpec((tm, d), lambda i: (i, 0))
    wide = pl.BlockSpec((tm, f), lambda i: (i, 0))
    outs, exchanged = _hosted_call(
        body, exchange, name=f"ffn_fwd_{tag}", grid=(t // tm,),
        in_specs=[row, _resident((1, d)), _resident(wg.shape), _resident(wu.shape), _resident(wd.shape)],
        out_specs=[row, wide, wide],
        out_shape=[jax.ShapeDtypeStruct((t, d), F32), jax.ShapeDtypeStruct((t, f), BF16), jax.ShapeDtypeStruct((t, f), BF16)],
        scratch_shapes=[], semantics=("parallel",), args=(x, gain, wg, wu, wd))
    return (*outs, exchanged)


def _loss_bwd(x, gain, target):
    t, d = x.shape
    tm = min(512, t)

    def body(x_ref, gain_ref, t_ref, dx_ref, loss_ref, dgain_ref):
        xv, gain_v = x_ref[...], gain_ref[...]
        xhat = xv * _rms(xv)
        err = xhat * gain_v - t_ref[...]
        dy = err * (1.0 / d)
        dx, _ = _rms_bwd(dy, xv, gain_v)
        dx_ref[...] = dx
        first = pl.program_id(0) == 0
        part = 0.5 * jnp.sum(jnp.mean(err * err, axis=-1, keepdims=True), axis=0, keepdims=True)
        _accumulate(loss_ref, jnp.broadcast_to(part, loss_ref.shape), first)
        _accumulate(dgain_ref, jnp.sum(dy * xhat, axis=0, keepdims=True), first)

    row = pl.BlockSpec((tm, d), lambda i: (i, 0))
    return pl.pallas_call(
        body, name="loss_bwd", grid=(t // tm,), in_specs=[row, _resident((1, d)), row],
        out_specs=[row, pl.BlockSpec((1, LANES), lambda i: (0, 0)), pl.BlockSpec((1, d), lambda i: (0, 0))],
        out_shape=[jax.ShapeDtypeStruct((t, d), F32), jax.ShapeDtypeStruct((1, LANES), F32), jax.ShapeDtypeStruct((1, d), F32)],
        compiler_params=_params("arbitrary"),
    )(x, gain, target)


def _ffn_bwd(dy, x, gain, g, u, wg, wu, wd, tag, exchange=()):
    t, d = x.shape
    f = wg.shape[1]
    tm = min(256, t)

    def body(dy_ref, x_ref, gain_ref, g_ref, u_ref, wg_ref, wu_ref, wd_ref, dx_ref, dg_ref, du_ref, a_ref, h_ref, dgain_ref):
        dyv = dy_ref[...]
        da = lax.dot_general(dyv.astype(BF16), wd_ref[...], NT, preferred_element_type=F32)
        gv, uv = g_ref[...].astype(F32), u_ref[...].astype(F32)
        sg = _sigmoid(gv)
        act = gv * sg
        a_ref[...] = (act * uv).astype(BF16)
        du = (da * act).astype(BF16)
        dg = (da * uv * (sg * (1.0 + gv * (1.0 - sg)))).astype(BF16)
        du_ref[...] = du
        dg_ref[...] = dg
        dh = (lax.dot_general(dg, wg_ref[...], NT, preferred_element_type=F32)
              + lax.dot_general(du, wu_ref[...], NT, preferred_element_type=F32))
        xv, gain_v = x_ref[...], gain_ref[...]
        dx, xhat = _rms_bwd(dh, xv, gain_v)
        dx_ref[...] = dyv + dx
        h_ref[...] = (xhat * gain_v).astype(BF16)
        _accumulate(dgain_ref, jnp.sum(dh * xhat, axis=0, keepdims=True), pl.program_id(0) == 0)

    row = pl.BlockSpec((tm, d), lambda i: (i, 0))
    wide = pl.BlockSpec((tm, f), lambda i: (i, 0))
    outs, exchanged = _hosted_call(
        body, exchange, name=f"ffn_bwd_{tag}", grid=(t // tm,),
        in_specs=[row, row, _resident((1, d)), wide, wide, _resident(wg.shape), _resident(wu.shape), _resident(wd.shape)],
        out_specs=[row, wide, wide, wide, row, pl.BlockSpec((1, d), lambda i: (0, 0))],
        out_shape=[jax.ShapeDtypeStruct((t, d), F32), jax.ShapeDtypeStruct((t, f), BF16), jax.ShapeDtypeStruct((t, f), BF16),
                   jax.ShapeDtypeStruct((t, f), BF16), jax.ShapeDtypeStruct((t, d), BF16), jax.ShapeDtypeStruct((1, d), F32)],
        scratch_shapes=[], semantics=("arbitrary",), args=(dy, x, gain, g, u, wg, wu, wd))
    return (*outs, exchanged)


def _tn_matmul(a, b, name, into=None, column=0, columns=1, exchange=()):
    t, k = a.shape
    n = b.shape[1]
    tk = k // 2 if (k // 2) % LANES == 0 else k
    tt = min(2048, t)

    def body(a_ref, b_ref, *rest):
        o_ref, acc_ref = rest[-2:]
        prod = lax.dot_general(a_ref[...].astype(BF16), b_ref[...].astype(BF16), TN, preferred_element_type=F32)
        j = pl.program_id(1)

        @pl.when(j == 0)
        def _():
            acc_ref[...] = prod

        @pl.when(j > 0)
        def _():
            acc_ref[...] += prod

        @pl.when(j == pl.num_programs(1) - 1)
        def _():
            o_ref[...] = acc_ref[...].astype(BF16)

    (dw,), exchanged = _hosted_call(
        body, exchange, name=name, grid=(k // tk, t // tt),
        in_specs=[pl.BlockSpec((tt, tk), lambda i, j: (j, i)), pl.BlockSpec((tt, n), lambda i, j: (j, 0))]
                 + ([ANY] if into is not None else []),
        out_specs=pl.BlockSpec((tk, n), lambda i, j: (i, column)),
        out_shape=jax.ShapeDtypeStruct((k, columns * n), BF16),
        scratch_shapes=[pltpu.VMEM((tk, n), F32)],
        input_output_aliases={2: 0} if into is not None else {},
        semantics=("parallel", "arbitrary"), args=(a, b, *([into] if into is not None else [])))
    return (dw, exchanged) if exchange else dw


def _attn_out_bwd(dx, w, o, dils, seq, tag, lse=None, sink=None):
    t, d = dx.shape
    ts = _tile_rows(seq)
    bl = t // seq
    ng = len(dils)
    has_sink = sink is not None
    expand = _head_expand().T

    def body(*refs):
        refs = list(refs)
        dx_ref, w_ref, o_ref, e_ref = refs[:4]
        refs = refs[4:]
        lse_ref, sink_ref = (refs.pop(0), refs.pop(0)) if has_sink else (None, None)
        do_refs, dl_refs = refs[:ng], refs[ng:2 * ng]
        refs = refs[2 * ng:]
        dsink_ref = refs.pop(0) if has_sink else None
        dof_ref, dlf_ref = refs
        do = lax.dot_general(dx_ref[...].astype(BF16), w_ref[...], NT, preferred_element_type=F32)
        prod = do * o_ref[...].astype(F32)
        hi = prod.astype(BF16)
        lo = (prod - hi.astype(F32)).astype(BF16)
        e = e_ref[...]
        dl = jnp.dot(hi, e, preferred_element_type=F32) + jnp.dot(lo, e, preferred_element_type=F32)
        for g in range(ng):
            for r, part in enumerate(_split_rows(do, dof_ref, dils[g])):
                do_refs[g][r] = part.astype(BF16)
            for r, part in enumerate(_split_rows(dl, dlf_ref, dils[g])):
                dl_refs[g][r] = part
        if has_sink:
            part = -jnp.exp(sink_ref[...] - lse_ref[...]) * dl
            _accumulate(dsink_ref, jnp.sum(part, axis=0, keepdims=True), pl.program_id(0) == 0)

    row = pl.BlockSpec((ts, d), lambda i: (i, 0))
    narrow = pl.BlockSpec((ts, LANES), lambda i: (i, 0))
    args = [dx, w, o, expand]
    in_specs = [row, _resident(w.shape), pl.BlockSpec((ts, Q_W), lambda i: (i, 0)), _resident(expand.shape)]
    if has_sink:
        args += [lse, jnp.pad(sink.reshape(1, N_HEADS), ((0, 0), (0, LANES - N_HEADS)))]
        in_specs += [narrow, _resident((1, LANES))]
    out_specs = [_res_spec(seq, dl, Q_W) for dl in dils] + [_res_spec(seq, dl, LANES) for dl in dils]
    out_shape = ([jax.ShapeDtypeStruct(_res_shape(bl, seq, dl, Q_W), BF16) for dl in dils]
                 + [jax.ShapeDtypeStruct(_res_shape(bl, seq, dl, LANES), F32) for dl in dils])
    if has_sink:
        out_specs.append(pl.BlockSpec((1, LANES), lambda i: (0, 0)))
        out_shape.append(jax.ShapeDtypeStruct((1, LANES), F32))
    outs = pl.pallas_call(
        body, name=f"attn_out_bwd_{tag}", grid=(t // ts,), in_specs=in_specs, out_specs=out_specs, out_shape=out_shape,
        scratch_shapes=[_stage(ts, Q_W), _stage(ts, LANES)],
        compiler_params=_params("arbitrary" if has_sink else "parallel"),
    )(*args)
    return list(outs[:ng]), list(outs[ng:2 * ng]), (outs[2 * ng] if has_sink else None)


def _attn_bwd(qkv, do, lse, delta, cos, sin, w, tag, exchange=()):
    shape = qkv.shape
    dil = shape[1]
    rows_all = _seq_view(qkv)
    nseq, length, _ = rows_all.shape
    bq = min(QUERY_BLOCK, length)
    wk = _key_rows(bq, w, length)
    nb = length // bq

    def body(qkv_ref, do_ref, lse_ref, dl_ref, cos_ref, sin_ref, dp_ref, kk_ref, vv_ref, dk_ref, dv_ref):
        _pair_variants(qkv_ref, Q_W, kk_ref)
        _pair_variants(qkv_ref, Q_W + KV_W, vv_ref)
        dk_ref[...] = jnp.zeros_like(dk_ref)
        dv_ref[...] = jnp.zeros_like(dv_ref)
        band = _band(bq, wk)
        lo_q = lax.broadcasted_iota(jnp.int32, (bq, LANES), 1) < HEAD_DIM
        hi_q = jnp.logical_not(lo_q)

        def block(i, carry):
            q0, k0 = _window(i, bq, w, wk, length)
            valid = jnp.abs(band + (q0 - k0)) <= w
            rows, krows = pl.ds(q0, bq), pl.ds(k0, wk)
            c, sn = cos_ref[rows, :], -sin_ref[rows, :]
            lse_t, dl_t = lse_ref[rows, :], dl_ref[rows, :]
            for kv in range(N_KV):
                heads = [(kv * GRP + h, h % 2) for h in range(GRP)]
                cols = [slice((kv * 2 + j) * LANES, (kv * 2 + j + 1) * LANES) for j in range(GRP // 2)]
                qp = [qkv_ref[rows, cs] for cs in cols]
                dop = [do_ref[rows, cs] for cs in cols]
                k2 = jnp.concatenate([kk_ref[kv, 0, krows, :], kk_ref[kv, 1, krows, :]], axis=0)
                v2 = jnp.concatenate([vv_ref[kv, 0, krows, :], vv_ref[kv, 1, krows, :]], axis=0)
                sc2 = [lax.dot_general(q_, k2, NT, preferred_element_type=F32) for q_ in qp]
                dp2 = [lax.dot_general(d_, v2, NT, preferred_element_type=F32) for d_ in dop]
                sc = [s_[:, half * wk:(half + 1) * wk] for s_ in sc2 for half in range(2)]
                dp = [d_[:, half * wk:(half + 1) * wk] for d_ in dp2 for half in range(2)]
                p = [jnp.exp(jnp.where(valid, s_, NEG_INF) - _over_keys(lse_t[:, hd:hd + 1], wk))
                     for s_, (hd, _) in zip(sc, heads)]
                ds = [(p_ * (dp_ - _over_keys(dl_t[:, hd:hd + 1], wk))).astype(BF16) for p_, dp_, (hd, _) in zip(p, dp, heads)]
                pb = [p_.astype(BF16) for p_ in p]
                for j in range(GRP // 2):
                    dq = jnp.dot(jnp.concatenate([ds[2 * j], ds[2 * j + 1]], axis=1), k2, preferred_element_type=F32) * SCALE
                    dp_ref[rows, cols[j]] = _rope(dq, c, sn).astype(BF16)
                zero = jnp.zeros((bq, LANES), BF16)
                q4 = jnp.concatenate([jnp.where(lo_q if h % 2 == 0 else hi_q, qp[h // 2], zero) for h in range(GRP)], axis=0)
                do4 = jnp.concatenate([jnp.where(lo_q if h % 2 == 0 else hi_q, dop[h // 2], zero) for h in range(GRP)], axis=0)
                dk_ref[kv, krows, :] += lax.dot_general(jnp.concatenate(ds, axis=0), q4, TN, preferred_element_type=F32)
                dv_ref[kv, krows, :] += lax.dot_general(jnp.concatenate(pb, axis=0), do4, TN, preferred_element_type=F32)
            return carry

        lax.fori_loop(0, nb, block, 0)
        lo = lax.broadcasted_iota(jnp.int32, (length, LANES), 1) < HEAD_DIM
        c, sn = cos_ref[...], -sin_ref[...]
        for ch in range(KV_W // LANES):
            halves = []
            for acc_ref in (dk_ref, dv_ref):
                even, odd = acc_ref[2 * ch], acc_ref[2 * ch + 1]
                even = even + pltpu.roll(even, HEAD_DIM, 1)
                odd = odd + pltpu.roll(odd, HEAD_DIM, 1)
                halves.append(jnp.where(lo, even, odd))
            dp_ref[:, Q_W + ch * LANES:Q_W + (ch + 1) * LANES] = _rope(halves[0], c, sn).astype(BF16)
            dp_ref[:, Q_W + KV_W + ch * LANES:Q_W + KV_W + (ch + 1) * LANES] = halves[1].astype(BF16)

    mode = dict(pipeline_mode=pl.Buffered(1)) if dil == 1 else {}

    def seq_block(c):
        return pl.BlockSpec((None, length, c), lambda i: (i, 0, 0), **mode)

    table = pl.BlockSpec((None, length, LANES), lambda i: (i % dil, 0, 0), **mode)
    (out,), exchanged = _hosted_call(
        body, exchange, name=f"attn_bwd_{tag}", grid=(nseq,),
        in_specs=[seq_block(QKV_W), seq_block(Q_W), seq_block(LANES), seq_block(LANES), table, table],
        out_specs=[pl.BlockSpec((None, length, QKV_W), lambda i: (i, 0, 0))],
        out_shape=[jax.ShapeDtypeStruct((nseq, length, QKV_W), BF16)],
        scratch_shapes=[pltpu.VMEM((N_KV, 2, length, LANES), BF16), pltpu.VMEM((N_KV, 2, length, LANES), BF16),
                        pltpu.VMEM((N_KV, length, LANES), F32), pltpu.VMEM((N_KV, length, LANES), F32)],
        semantics=("parallel",), args=(rows_all, _seq_view(do), _seq_view(lse), _seq_view(delta), cos, sin))
    return out.reshape(shape), exchanged


def _qkv_bwd(dy, x, gain, w, dps, dils, seq, tag, exchange=()):
    t, d = x.shape
    ts = _tile_rows(seq)
    bl = t // seq
    ng = len(dps)

    def body(dy_ref, x_ref, gain_ref, w_ref, *refs):
        dp_refs, dx_ref = refs[:ng], refs[ng]
        h_refs = refs[ng + 1:2 * ng + 1]
        dgain_ref, stage_ref = refs[2 * ng + 1:]
        dh = None
        for gi in range(ng):
            dil = dils[gi]
            n = ts // dil
            dp = dp_refs[gi][0] if dil == 1 else jnp.concatenate([dp_refs[gi][r] for r in range(dil)], axis=0)
            part = lax.dot_general(dp, w_ref[:, gi * QKV_W:(gi + 1) * QKV_W], NT, preferred_element_type=F32)
            part = _merge_rows([part[r * n:(r + 1) * n] for r in range(dil)], stage_ref, dil)
            dh = part if dh is None else dh + part
        xv, gain_v = x_ref[...], gain_ref[...]
        dx, xhat = _rms_bwd(dh, xv, gain_v)
        dx_ref[...] = dy_ref[...] + dx
        h = xhat * gain_v
        for gi in range(ng):
            for r, part in enumerate(_split_rows(h, stage_ref, dils[gi])):
                h_refs[gi][r] = part.astype(BF16)
        _accumulate(dgain_ref, jnp.sum(dh * xhat, axis=0, keepdims=True), pl.program_id(0) == 0)

    row = pl.BlockSpec((ts, d), lambda i: (i, 0))
    outs, exchanged = _hosted_call(
        body, exchange, name=f"qkv_bwd_{tag}", grid=(t // ts,),
        in_specs=[row, row, _resident((1, d)), _resident(w.shape)] + [_res_spec(seq, dl, QKV_W) for dl in dils],
        out_specs=[row] + [_res_spec(seq, dl, d) for dl in dils] + [pl.BlockSpec((1, d), lambda i: (0, 0))],
        out_shape=[jax.ShapeDtypeStruct((t, d), F32)] + [jax.ShapeDtypeStruct(_res_shape(bl, seq, dl, d), BF16) for dl in dils]
                  + [jax.ShapeDtypeStruct((1, d), F32)],
        scratch_shapes=[_stage(ts, d)], semantics=("arbitrary",), args=(dy, x, gain, w, *dps))
    return outs[0], list(outs[1:1 + ng]), outs[1 + ng], exchanged


ANY = pl.BlockSpec(memory_space=pl.ANY)


def _place():
    x, y, c = lax.axis_index("x"), lax.axis_index("y"), lax.axis_index("c")
    return x, y, c


def _exchange_steps(srcs, dsts, gather, send_sems, recv_sems, local_sems):
    x, y, c = _place()
    me, sibling = (x, y, c), (x, y, 1 - c)
    chips = [(1 - x, y), (x, 1 - y), (1 - x, 1 - y)]
    mine = 4 * x + 2 * y + c

    def slot(a, device):
        px, py, pc = device
        return dsts[a].at[4 * px + 2 * py + pc]

    def passes(a, k, block, to, src=None):
        rows = slot(a, block)
        return pltpu.make_async_remote_copy(src_ref=rows if src is None else src, dst_ref=rows, send_sem=send_sems.at[a, k],
                                            recv_sem=recv_sems.at[a, k], device_id=to, device_id_type=MESH)

    def scatters(a, k):
        peer = mine ^ k
        return pltpu.make_async_remote_copy(
            src_ref=srcs[a].at[peer], dst_ref=dsts[a].at[mine], send_sem=send_sems.at[a, k - 1], recv_sem=recv_sems.at[a, k - 1],
            device_id=(peer // 4, (peer // 2) % 2, peer % 2), device_id_type=MESH)

    def local(a):
        return pltpu.make_async_copy(srcs[a] if gather[a] else srcs[a].at[mine], dsts[a].at[mine], local_sems.at[a])

    def first_copies(a):
        if not gather[a]:
            return [scatters(a, k) for k in range(1, N_DEV)]
        return [passes(a, 0, me, sibling, src=srcs[a])] + [passes(a, 1 + j, me, (*chip, c), src=srcs[a]) for j, chip in enumerate(chips)]

    def start():
        for a in range(len(srcs)):
            local(a).start()
            for cp in first_copies(a):
                cp.start()

    def forward():
        for a in range(len(srcs)):
            if gather[a]:
                for j, chip in enumerate(chips):
                    passes(a, 1 + j, (*chip, c), me).wait_recv()
                    passes(a, 4 + j, (*chip, c), sibling).start()

    def finish():
        for a in range(len(srcs)):
            if gather[a]:
                passes(a, 0, sibling, me).wait_recv()
                for j, chip in enumerate(chips):
                    passes(a, 4 + j, (*chip, 1 - c), me).wait_recv()
                    passes(a, 4 + j, (*chip, c), sibling).wait_send()
                for cp in first_copies(a):
                    cp.wait_send()
            else:
                for cp in first_copies(a):
                    cp.wait()
            local(a).wait()

    return start, forward, finish


def _exchange_scratch(n):
    return [pltpu.SemaphoreType.DMA((n, N_DEV - 1)), pltpu.SemaphoreType.DMA((n, N_DEV - 1)), pltpu.SemaphoreType.DMA((n,))]


def _exchanged_shapes(exchange):
    return [jax.ShapeDtypeStruct(((N_DEV,) + a.shape) if g else a.shape, a.dtype) for a, g in exchange]


def _hosted_call(body, exchange, *, name, grid, in_specs, out_specs, out_shape, scratch_shapes, semantics, args,
                 input_output_aliases=None):
    single = not isinstance(out_shape, (list, tuple))
    out_specs, out_shape = ([out_specs], [out_shape]) if single else (list(out_specs), list(out_shape))
    scratch, aliases = list(scratch_shapes), dict(input_output_aliases or {})
    if not exchange:
        outs = pl.pallas_call(body, name=name, grid=grid, in_specs=in_specs, out_specs=out_specs, out_shape=out_shape,
                              scratch_shapes=scratch, input_output_aliases=aliases, compiler_params=_params(*semantics))(*args)
        return list(outs), []
    n, n_in, n_out, n_scr = len(exchange), len(in_specs), len(out_specs), len(scratch)
    gather = [g for _, g in exchange]
    steps = math.prod(grid)

    def hosted(*refs):
        own_in, x_in = refs[:n_in], refs[n_in:n_in + n]
        own_out, x_out = refs[n_in + n:n_in + n + n_out], refs[n_in + n + n_out:n_in + 2 * n + n_out]
        own_scr, sems = refs[n_in + 2 * n + n_out:n_in + 2 * n + n_out + n_scr], refs[-3:]
        step = pl.program_id(0)
        for axis in range(1, len(grid)):
            step = step * grid[axis] + pl.program_id(axis)
        start, forward, finish = _exchange_steps(x_in, x_out, gather, *sems)
        pl.when(step == 0)(start)
        body(*own_in, *own_out, *own_scr)
        pl.when(step == steps // 2)(forward)
        pl.when(step == steps - 1)(finish)

    outs = pl.pallas_call(
        hosted, name=name, grid=grid, in_specs=list(in_specs) + [ANY] * n, out_specs=out_specs + [ANY] * n,
        out_shape=out_shape + _exchanged_shapes(exchange), scratch_shapes=scratch + _exchange_scratch(n),
        input_output_aliases=aliases, compiler_params=_params(*["arbitrary"] * len(grid)),
    )(*args, *[a for a, _ in exchange])
    return list(outs[:n_out]), list(outs[n_out:])


def _exchange_now(exchange, name):
    n = len(exchange)
    gather = [g for _, g in exchange]

    def body(*refs):
        for step in _exchange_steps(refs[:n], refs[n:2 * n], gather, *refs[2 * n:]):
            step()

    return pl.pallas_call(
        body, name=name, in_specs=[ANY] * n, out_specs=[ANY] * n, out_shape=_exchanged_shapes(exchange),
        scratch_shapes=_exchange_scratch(n),
    )(*[a for a, _ in exchange])


def _all_reduce_small(v):
    def body(v_ref, o_ref, recv_ref, send_sems, recv_sems):
        x, y, c = _place()
        me = 4 * x + 2 * y + c
        copies = []
        for k in range(1, N_DEV):
            peer = me ^ k
            copies.append(pltpu.make_async_remote_copy(
                src_ref=v_ref, dst_ref=recv_ref.at[k], send_sem=send_sems.at[k - 1], recv_sem=recv_sems.at[k - 1],
                device_id=(peer // 4, (peer // 2) % 2, peer % 2), device_id_type=MESH))
        for cp in copies:
            cp.start()
        recv_ref[0] = v_ref[...]
        for cp in copies:
            cp.wait()
        acc = recv_ref[me]
        for src in range(1, N_DEV):
            acc = acc + recv_ref[me ^ src]
        o_ref[...] = acc

    vm = pl.BlockSpec(memory_space=pltpu.VMEM)
    return pl.pallas_call(
        body, name="all_reduce_small", in_specs=[vm], out_specs=vm, out_shape=jax.ShapeDtypeStruct(v.shape, F32),
        scratch_shapes=[pltpu.VMEM((N_DEV,) + v.shape, F32), pltpu.SemaphoreType.DMA((N_DEV - 1,)),
                        pltpu.SemaphoreType.DMA((N_DEV - 1,))],
    )(v)


def _adamw_math(w, g, m, v):
    m = ADAM_B1 * m + (1.0 - ADAM_B1) * g
    v = ADAM_B2 * v + (1.0 - ADAM_B2) * (g * g)
    m_hat = m / (1.0 - ADAM_B1 ** ADAM_STEP)
    v_hat = v / (1.0 - ADAM_B2 ** ADAM_STEP)
    delta = -ADAM_LR * (m_hat / (jnp.sqrt(v_hat) + ADAM_EPS) + ADAM_WD * w)
    return delta, m, v


def _adamw(parts, w, m, v, name, layer=None, into=None):
    r, c = w.shape[-2:]
    tr = r // 2 if r % 16 == 0 and r >= 256 else r
    n = len(parts)

    def body(*refs):
        w_ref, m_ref, v_ref = refs[n:n + 3]
        g_ref, d_ref, nm_ref, nv_ref = refs[-4:]
        g = refs[0][...].astype(F32)
        for p_ref in refs[1:n]:
            g = g + p_ref[...].astype(F32)
        g_ref[...] = g
        d_ref[...], nm_ref[...], nv_ref[...] = _adamw_math(w_ref[...], g, m_ref[...], v_ref[...])

    def slab(slot):
        return pl.BlockSpec((None, tr, c), lambda i: (slot, i, 0))

    tile = pl.BlockSpec((tr, c), lambda i: (i, 0)) if layer is None else slab(layer)
    arrays, in_specs = [], []
    for p in parts:
        if isinstance(p, tuple):
            arrays.append(p[0])
            in_specs.append(slab(p[1]))
        else:
            arrays.append(p)
            in_specs.append(tile)
    kept = list(into) if into is not None else []
    return pl.pallas_call(
        body, name=name, grid=(r // tr,), in_specs=in_specs + [tile] * 3 + [ANY] * len(kept), out_specs=[tile] * 4,
        out_shape=[jax.ShapeDtypeStruct(w.shape, F32)] * 4,
        input_output_aliases={n + 3 + k: k for k in range(len(kept))}, compiler_params=_params("parallel"),
    )(*arrays, w, m, v, *kept)


def _columns(g):
    return g.transpose(1, 0, 2).reshape(g.shape[1], -1)


def _rows(g):
    return g.reshape(-1, g.shape[-1])


def _column_blocks(dw):
    k, n = dw.shape
    return dw.reshape(k, N_DEV, n // N_DEV).transpose(1, 0, 2)


def _row_blocks(dw):
    k, n = dw.shape
    return dw.reshape(N_DEV, k // N_DEV, n)


def _pack_rows(rows, width):
    out = None
    for i, r in enumerate(rows):
        r = r.reshape(1, -1).astype(F32)
        r = jnp.pad(r, ((i, 8 - 1 - i), (0, width - r.shape[1])))
        out = r if out is None else out + r
    return out


def _mixer_fwd(x, gain, w_in, w_out, cos, sin, seq, groups, tag, sink=None, exchanges=None):
    qkvs, os, lses, got = [], [], [], {}
    for gi, (dil, w) in enumerate(groups):
        qkv, got["proj", gi] = _qkv_proj(x, gain, w_in, _tables_tiled(cos, seq, dil), _tables_tiled(sin, seq, dil), seq, dil, gi,
                                         f"{tag}{gi}", exchange=(exchanges or {}).get(("proj", gi), ()))
        o, lse, got[gi] = _attn_fwd(qkv, w, f"{tag}{gi}", sink=sink, exchange=(exchanges or {}).get(gi, ()))
        qkvs.append(qkv)
        os.append(o)
        lses.append(lse)
    y, o, lses = _out_proj(x, os, lses, [dl for dl, _ in groups], w_out, seq, tag)
    return y, (qkvs, o, lses), got


def _mixer_bwd(dy, x_in, gain, w_in, w_out, saved, cos, sin, seq, groups, tag, sink=None, exchange=(), scatter_dw_out=False):
    qkvs, o, lses = saved
    t, d = x_in.shape
    dils = [dl for dl, _ in groups]
    lse_tokens = lses[0].reshape(t, LANES) if sink is not None else None
    dos, dls, dsink = _attn_out_bwd(dy, w_out, o, dils, seq, tag, lse=lse_tokens, sink=sink)
    dw_out = _tn_matmul(o, dy, f"dw_out_{tag}")
    if scatter_dw_out:
        exchange = list(exchange) + _to_send([], [dw_out])
    dps, got = [], []
    for gi, (dil, w) in enumerate(groups):
        dp, brought = _attn_bwd(qkvs[gi], dos[gi], lses[gi], dls[gi], _tables_by_residue(cos, seq, dil),
                                _tables_by_residue(sin, seq, dil), w, f"{tag}{gi}", exchange=exchange if gi == 0 else ())
        dps.append(dp)
        got += brought
    if scatter_dw_out:
        dw_out = got.pop()
    dx, hs, dgain, _ = _qkv_bwd(dy, x_in, gain, w_in, dps, dils, seq, tag)
    dw_in = None
    for gi in range(len(groups)):
        dw_in = _tn_matmul(hs[gi].reshape(t, d), dps[gi].reshape(t, QKV_W), f"dw_in_{tag}{gi}", into=dw_in, column=gi,
                           columns=len(groups))
    return dx, dw_in, dw_out, dgain, dsink, got


def _ffn_layer_bwd(dy, x_in, gain, g, u, wg, wu, wd, tag, exchange=()):
    dx, dg, du, act, h, dgain, got = _ffn_bwd(dy, x_in, gain, g, u, wg, wu, wd, tag, exchange=exchange)
    dwd = _tn_matmul(act, dy, f"dw_down_{tag}")
    dwg, (r_down,) = _tn_matmul(h, dg, f"dw_gate_{tag}", exchange=_to_send([], [dwd]))
    dwu, (r_gate,) = _tn_matmul(h, du, f"dw_up_{tag}", exchange=_to_send([dwg], []))
    return dx, r_gate, dwu, r_down, dgain, got


def _to_send(dws_by_columns, dws_by_rows):
    return [(_column_blocks(g), False) for g in dws_by_columns] + [(_row_blocks(g), False) for g in dws_by_rows]


def kernel(x, a_w_in, a_sink, a_w_out, b_w_in, b_w_out, norm_mix, norm_ffn, w_gate, w_up, w_down, final_norm, loss_target, m_a_w_in, m_a_sink, m_a_w_out, m_b_w_in, m_b_w_out, m_norm_mix, m_norm_ffn, m_w_gate, m_w_up, m_w_down, m_final_norm, v_a_w_in, v_a_sink, v_a_w_out, v_b_w_in, v_b_w_out, v_norm_mix, v_norm_ffn, v_w_gate, v_w_up, v_w_down, v_final_norm):
    bl, seq, d = x.shape
    t = bl * seq
    xf = x.reshape(t, d)
    target = loss_target.reshape(t, d)
    cos, sin = _rope_tables(seq)
    groups_a = [(1, ATTN_HALF_WINDOW)]
    groups_b = [(dil, window // 2 // dil) for window, dil in DILATED_GROUPS]

    def shard(w_):
        return w_.astype(BF16)

    wa_in, wa_out = _exchange_now([(shard(a_w_in[0]), True), (shard(a_w_out[0]), True)], "gather_first")
    wa_in, wa_out = _columns(wa_in), _rows(wa_out)

    x1_0, saved_a, got = _mixer_fwd(xf, norm_mix[0:1], wa_in, wa_out, cos, sin, seq, groups_a, "a", sink=a_sink[0],
                                    exchanges={("proj", 0): [(shard(w_down[0]), True)],
                                               0: [(shard(w_gate[0]), True), (shard(w_up[0]), True)]})
    wg0, wu0, wd0 = _columns(got[0][0]), _columns(got[0][1]), _rows(got["proj", 0][0])
    x2_0, g0, u0, got = _ffn_fwd(x1_0, norm_ffn[0:1], wg0, wu0, wd0, "0",
                                 exchange=[(shard(b_w_in[0]), True), (shard(b_w_out[0]), True)])
    wb_in, wb_out = _columns(got[0]), _rows(got[1])
    x1_1, saved_b, got = _mixer_fwd(x2_0, norm_mix[1:2], wb_in, wb_out, cos, sin, seq, groups_b, "b",
                                    exchanges={0: [(shard(w_gate[1]), True)], 1: [(shard(w_up[1]), True)],
                                               2: [(shard(w_down[1]), True)]})
    wg1, wu1, wd1 = _columns(got[0][0]), _columns(got[1][0]), _rows(got[2][0])
    x2_1, g1, u1, _ = _ffn_fwd(x1_1, norm_ffn[1:2], wg1, wu1, wd1, "1")

    dy, loss_part, d_final = _loss_bwd(x2_1, final_norm.reshape(1, d), target)
    dy, r_g1, dwu1, r_d1, d_nf1, _ = _ffn_layer_bwd(dy, x1_1, norm_ffn[1:2], g1, u1, wg1, wu1, wd1, "1")
    dy, dwb_in, dwb_out, d_nm1, _, (r_u1,) = _mixer_bwd(
        dy, x2_0, norm_mix[1:2], wb_in, wb_out, saved_b, cos, sin, seq, groups_b, "b", exchange=_to_send([dwu1], []))
    dy, r_g0, dwu0, r_d0, d_nf0, (r_b_in, r_b_out) = _ffn_layer_bwd(
        dy, x1_0, norm_ffn[0:1], g0, u0, wg0, wu0, wd0, "0", exchange=_to_send([dwb_in], [dwb_out]))
    dy, dwa_in, r_a_out, d_nm0, d_sink, (r_u0,) = _mixer_bwd(
        dy, xf, norm_mix[0:1], wa_in, wa_out, saved_a, cos, sin, seq, groups_a, "a", sink=a_sink[0],
        exchange=_to_send([dwu0], []), scatter_dw_out=True)
    (r_a_in,) = _exchange_now(_to_send([dwa_in], []), "scatter_last")
    grad_x = dy.reshape(bl, seq, d)

    def update(received, w_, m_, v_, name):
        out = None
        for layer in reversed(range(len(received))):
            out = _adamw([(received[layer], src) for src in range(N_DEV)], w_, m_, v_, f"adamw_{name}{layer}", layer=layer, into=out)
        return out

    u_a_in = update([r_a_in], a_w_in, m_a_w_in, v_a_w_in, "a_in")
    u_a_out = update([r_a_out], a_w_out, m_a_w_out, v_a_w_out, "a_out")
    u_b_in = update([r_b_in], b_w_in, m_b_w_in, v_b_w_in, "b_in")
    u_b_out = update([r_b_out], b_w_out, m_b_w_out, v_b_w_out, "b_out")
    u_gate = update([r_g0, r_g1], w_gate, m_w_gate, v_w_gate, "gate")
    u_up = update([r_u0, r_u1], w_up, m_w_up, v_w_up, "up")
    u_down = update([r_d0, r_d1], w_down, m_w_down, v_w_down, "down")

    small = _pack_rows([d_nm0, d_nm1, d_nf0, d_nf1, d_final, d_sink, loss_part], d)
    total = _all_reduce_small(small)
    small_w = _pack_rows([norm_mix[0], norm_mix[1], norm_ffn[0], norm_ffn[1], final_norm, a_sink], d)
    small_m = _pack_rows([m_norm_mix[0], m_norm_mix[1], m_norm_ffn[0], m_norm_ffn[1], m_final_norm, m_a_sink], d)
    small_v = _pack_rows([v_norm_mix[0], v_norm_mix[1], v_norm_ffn[0], v_norm_ffn[1], v_final_norm, v_a_sink], d)
    u_small = _adamw([total], small_w, small_m, small_v, "adamw_small")
    loss = total[6, 0]

    outs = []
    for k in range(4):
        sm = u_small[k]
        outs += [u_a_in[k], sm[5:6, :N_HEADS], u_a_out[k], u_b_in[k], u_b_out[k], sm[0:2], sm[2:4],
                 u_gate[k], u_up[k], u_down[k], sm[4]]
    return (loss, grad_x, *outs)
```

```python
import functools
import math

import jax
import jax.numpy as jnp
from jax import lax
from jax.experimental import pallas as pl
from jax.experimental.pallas import tpu as pltpu

F32 = jnp.float32
BF16 = jnp.bfloat16

HEAD_DIM = 64
N_HEADS = 16
N_KV = 4
GRP = N_HEADS // N_KV
Q_W = N_HEADS * HEAD_DIM
KV_W = N_KV * HEAD_DIM
QKV_W = Q_W + 2 * KV_W
ATTN_HALF_WINDOW = 128
DILATED_GROUPS = ((128, 1), (512, 4), (2048, 16))
ROPE_THETA = 10000.0
RMS_EPS = 1e-6
NEG_INF = -1e30
SCALE = 1.0 / math.sqrt(HEAD_DIM)

ADAM_LR = 0.001
ADAM_B1 = 0.9
ADAM_B2 = 0.999
ADAM_EPS = 1e-08
ADAM_WD = 0.01
ADAM_STEP = 10

LANES = 128
VMEM_LIMIT = 56 * 1024 * 1024
QUERY_BLOCK = 128
N_DEV = 8
MESH = pl.DeviceIdType.MESH

NT = (((1,), (1,)), ((), ()))
TN = (((0,), (0,)), ((), ()))


def _params(*sem):
    return pltpu.CompilerParams(dimension_semantics=tuple(sem) if sem else None, vmem_limit_bytes=VMEM_LIMIT)


def _resident(shape):
    return pl.BlockSpec(shape, lambda *_: (0,) * len(shape), pipeline_mode=pl.Buffered(1))


def _rope_tables(seq):
    inv_freq = 1.0 / (ROPE_THETA ** (jnp.arange(0, HEAD_DIM, 2, dtype=F32) / HEAD_DIM))
    ang = jnp.arange(seq, dtype=F32)[:, None] * inv_freq[None, :]
    cos, sin = jnp.cos(ang), jnp.sin(ang)
    return jnp.tile(cos, (1, 4)), jnp.concatenate([-sin, sin, -sin, sin], axis=1)


def _rope(t, cos, sin_signed):
    lane = lax.broadcasted_iota(jnp.int32, t.shape, 1)
    first = (lane & (HEAD_DIM // 2)) == 0
    swapped = jnp.where(first, pltpu.roll(t, LANES - HEAD_DIM // 2, 1), pltpu.roll(t, HEAD_DIM // 2, 1))
    return t * cos + swapped * sin_signed


def _rms(x):
    return lax.rsqrt(jnp.mean(x * x, axis=-1, keepdims=True) + RMS_EPS)


def _rms_bwd(dh, x, gain):
    r = _rms(x)
    xhat = x * r
    dxh = dh * gain
    dx = r * (dxh - xhat * jnp.mean(dxh * xhat, axis=-1, keepdims=True))
    return dx, xhat


def _accumulate(ref, value, first):
    @pl.when(first)
    def _():
        ref[...] = jnp.zeros_like(ref)

    ref[...] += value


def _tile_rows(seq):
    return min(512, seq)


def _res_shape(bl, seq, dil, c):
    ts = _tile_rows(seq)
    return (bl, dil, seq // ts, ts // dil, c)


def _res_spec(seq, dil, c):
    ts = _tile_rows(seq)
    per_seq = seq // ts
    return pl.BlockSpec((None, dil, None, ts // dil, c), lambda i: (i // per_seq, 0, i % per_seq, 0, 0))


def _seq_view(a):
    bl, dil, tiles, n, c = a.shape
    return a.reshape(bl * dil, tiles * n, c)


def _stage(ts, c):
    return pltpu.VMEM((c // LANES, ts, LANES), F32)


def _split_rows(val, stage_ref, dil):
    if dil == 1:
        return [val]
    ts, c = val.shape
    n, nc = ts // dil, c // LANES
    for k in range(nc):
        stage_ref[k] = val[:, k * LANES:(k + 1) * LANES]
    return [jnp.concatenate([stage_ref[k, pl.ds(r, n, stride=dil), :] for k in range(nc)], axis=1) for r in range(dil)]


def _merge_rows(parts, stage_ref, dil):
    if dil == 1:
        return parts[0]
    n, c = parts[0].shape
    nc = c // LANES
    for r, part in enumerate(parts):
        for k in range(nc):
            stage_ref[k, pl.ds(r, n, stride=dil), :] = part[:, k * LANES:(k + 1) * LANES]
    return jnp.concatenate([stage_ref[k] for k in range(nc)], axis=1)


def _tables_tiled(table, seq, dil):
    ts = _tile_rows(seq)
    return table.reshape(seq // ts, ts // dil, dil, LANES).transpose(0, 2, 1, 3).reshape(seq, LANES)


def _tables_by_residue(table, seq, dil):
    return table.reshape(seq // dil, dil, LANES).transpose(1, 0, 2)


def _qkv_proj(x, gain, w, cos, sin, seq, dil, group, tag, exchange=()):
    t, d = x.shape
    ts = _tile_rows(seq)
    n = ts // dil
    per_seq = seq // ts

    def body(x_ref, g_ref, w_ref, cos_ref, sin_ref, o_ref, stage_ref):
        xv = jnp.concatenate(_split_rows(x_ref[...], stage_ref, dil), axis=0)
        h = (xv * _rms(xv) * g_ref[...]).astype(BF16)
        acc = jnp.dot(h, w_ref[...], preferred_element_type=F32)
        c, s = cos_ref[...], sin_ref[...]
        for j in range(QKV_W // LANES):
            cols = slice(j * LANES, (j + 1) * LANES)
            val = acc[:, cols]
            if j < (Q_W + KV_W) // LANES:
                val = _rope(val, c, s)
            if j < Q_W // LANES:
                val = val * SCALE
            val = val.astype(BF16)
            for r in range(dil):
                o_ref[r, :, cols] = val[r * n:(r + 1) * n]

    table = pl.BlockSpec((ts, LANES), lambda i: (i % per_seq, 0))
    (qkv,), exchanged = _hosted_call(
        body, exchange, name=f"qkv_proj_{tag}", grid=(t // ts,),
        in_specs=[pl.BlockSpec((ts, d), lambda i: (i, 0)), pl.BlockSpec((1, d), lambda i: (0, 0)),
                  pl.BlockSpec((d, QKV_W), lambda i: (0, group)), table, table],
        out_specs=[_res_spec(seq, dil, QKV_W)],
        out_shape=[jax.ShapeDtypeStruct(_res_shape(t // seq, seq, dil, QKV_W), BF16)],
        scratch_shapes=[_stage(ts, d)], semantics=("parallel",), args=(x, gain, w, cos, sin))
    return qkv, exchanged


def _band(bq, wk):
    return lax.broadcasted_iota(jnp.int32, (bq, wk), 0) - lax.broadcasted_iota(jnp.int32, (bq, wk), 1)


def _pair_variants(src_ref, base, dst_ref):
    lo = lax.broadcasted_iota(jnp.int32, (src_ref.shape[0], LANES), 1) < HEAD_DIM
    for c in range(KV_W // LANES):
        chunk = src_ref[:, base + c * LANES:base + (c + 1) * LANES]
        rolled = pltpu.roll(chunk, HEAD_DIM, 1)
        zero = jnp.zeros_like(chunk)
        dst_ref[2 * c, 0] = jnp.where(lo, chunk, zero)
        dst_ref[2 * c, 1] = jnp.where(lo, zero, rolled)
        dst_ref[2 * c + 1, 0] = jnp.where(lo, rolled, zero)
        dst_ref[2 * c + 1, 1] = jnp.where(lo, zero, chunk)


def _over_keys(col, wk):
    if wk % LANES:
        return jnp.broadcast_to(col, (col.shape[0], wk))
    wide = jnp.broadcast_to(col, (col.shape[0], LANES))
    return wide if wk == LANES else jnp.concatenate([wide] * (wk // LANES), axis=1)


def _key_rows(bq, w, length):
    return min(bq + 2 * w, length)


def _window(i, bq, w, wk, length):
    q0 = pl.multiple_of(i * bq, bq)
    k0 = pl.multiple_of(jnp.clip(q0 - w, 0, length - wk), min(w, bq))
    return q0, k0


def _attn_fwd(qkv, w, tag, sink=None, exchange=()):
    shape = qkv.shape
    rows_all = _seq_view(qkv)
    nseq, length, _ = rows_all.shape
    bq = min(QUERY_BLOCK, length)
    wk = _key_rows(bq, w, length)
    nb = length // bq
    has_sink = sink is not None

    def body(*refs):
        qkv_ref = refs[0]
        sink_ref = refs[1] if has_sink else None
        o_ref, lse_ref, kk_ref, vv_ref = refs[-4:]
        _pair_variants(qkv_ref, Q_W, kk_ref)
        _pair_variants(qkv_ref, Q_W + KV_W, vv_ref)
        band = _band(bq, wk)
        lane = lax.broadcasted_iota(jnp.int32, (bq, LANES), 1)
        lo = lane < HEAD_DIM

        def block(i, carry):
            q0, k0 = _window(i, bq, w, wk, length)
            valid = jnp.abs(band + (q0 - k0)) <= w
            rows, krows = pl.ds(q0, bq), pl.ds(k0, wk)
            lse_tile = jnp.zeros((bq, LANES), F32)
            for kv in range(N_KV):
                heads = [(kv * GRP + h, h % 2) for h in range(GRP)]
                qp = [qkv_ref[rows, (kv * 2 + j) * LANES:(kv * 2 + j + 1) * LANES] for j in range(GRP // 2)]
                k2 = jnp.concatenate([kk_ref[kv, 0, krows, :], kk_ref[kv, 1, krows, :]], axis=0)
                v2 = jnp.concatenate([vv_ref[kv, 0, krows, :], vv_ref[kv, 1, krows, :]], axis=0)
                sc2 = [lax.dot_general(q_, k2, NT, preferred_element_type=F32) for q_ in qp]
                sc = [jnp.where(valid, s_[:, half * wk:(half + 1) * wk], NEG_INF) for s_ in sc2 for half in range(2)]
                m = [jnp.max(s_, axis=-1, keepdims=True) for s_ in sc]
                if has_sink:
                    m = [jnp.maximum(m_, sink_ref[hd]) for m_, (hd, _) in zip(m, heads)]
                mb = [jnp.broadcast_to(m_, (bq, LANES)) for m_ in m]
                p = [jnp.exp(s_ - _over_keys(m_, wk)) for s_, m_ in zip(sc, m)]
                den = [jnp.sum(p_, axis=-1, keepdims=True) for p_ in p]
                if has_sink:
                    den = [d_ + jnp.exp(sink_ref[hd] - m_) for d_, m_, (hd, _) in zip(den, m, heads)]
                inv = [jnp.broadcast_to(1.0 / d_, (bq, LANES)) for d_ in den]
                pb = [p_.astype(BF16) for p_ in p]
                for j in range(GRP // 2):
                    o = jnp.dot(jnp.concatenate([pb[2 * j], pb[2 * j + 1]], axis=1), v2, preferred_element_type=F32)
                    o = o * jnp.where(lo, inv[2 * j], inv[2 * j + 1])
                    o_ref[rows, (kv * 2 + j) * LANES:(kv * 2 + j + 1) * LANES] = o.astype(BF16)
                for h, (hd, _) in enumerate(heads):
                    lse_tile = jnp.where(lane == hd, mb[h] - jnp.log(inv[h]), lse_tile)
            lse_ref[rows, :] = lse_tile
            return carry

        lax.fori_loop(0, nb, block, 0)

    args = [rows_all]
    in_specs = [pl.BlockSpec((None, length, QKV_W), lambda i: (i, 0, 0))]
    if has_sink:
        args.append(sink)
        in_specs.append(pl.BlockSpec(memory_space=pltpu.SMEM))
    (o, lse), exchanged = _hosted_call(
        body, exchange, name=f"attn_fwd_{tag}", grid=(nseq,), in_specs=in_specs,
        out_specs=[pl.BlockSpec((None, length, Q_W), lambda i: (i, 0, 0)), pl.BlockSpec((None, length, LANES), lambda i: (i, 0, 0))],
        out_shape=[jax.ShapeDtypeStruct((nseq, length, Q_W), BF16), jax.ShapeDtypeStruct((nseq, length, LANES), F32)],
        scratch_shapes=[pltpu.VMEM((N_KV, 2, length, LANES), BF16), pltpu.VMEM((N_KV, 2, length, LANES), BF16)],
        semantics=("parallel",), args=args)
    return o.reshape(shape[:-1] + (Q_W,)), lse.reshape(shape[:-1] + (LANES,)), exchanged


def _head_expand():
    return (jnp.arange(LANES)[:, None] == jnp.arange(Q_W)[None, :] // HEAD_DIM).astype(BF16)


def _out_proj(x, os, lses, dils, w, seq, tag):
    t, d = x.shape
    ts = _tile_rows(seq)
    ng = len(os)
    bl = t // seq
    if ng == 1:
        def body1(x_ref, o_ref, w_ref, y_ref):
            y_ref[...] = x_ref[...] + jnp.dot(o_ref[...], w_ref[...], preferred_element_type=F32)

        row = pl.BlockSpec((ts, d), lambda i: (i, 0))
        o = os[0].reshape(t, Q_W)
        y = pl.pallas_call(
            body1, name=f"out_proj_{tag}", grid=(t // ts,), in_specs=[row, row, _resident(w.shape)], out_specs=row,
            out_shape=jax.ShapeDtypeStruct((t, d), F32), compiler_params=_params("parallel"),
        )(x, o, w)
        return y, o, [lses[0]]

    def body(*refs):
        x_ref, w_ref, e_ref = refs[:3]
        o_refs, l_refs = refs[3:3 + ng], refs[3 + ng:3 + 2 * ng]
        y_ref, om_ref = refs[3 + 2 * ng:5 + 2 * ng]
        lt_refs = refs[5 + 2 * ng:5 + 3 * ng]
        wide_ref, narrow_ref = refs[5 + 3 * ng:]
        ls = [_merge_rows([l_refs[g][r] for r in range(dils[g])], narrow_ref, dils[g]) for g in range(ng)]
        mx = functools.reduce(jnp.maximum, ls)
        tot = mx + jnp.log(functools.reduce(lambda a, b: a + b, [jnp.exp(l_ - mx) for l_ in ls]))
        e = e_ref[...]
        o = None
        for g in range(ng):
            wt = jnp.exp(ls[g] - tot)
            hi = wt.astype(BF16)
            lo = (wt - hi.astype(F32)).astype(BF16)
            wide = jnp.dot(hi, e, preferred_element_type=F32) + jnp.dot(lo, e, preferred_element_type=F32)
            term = wide * _merge_rows([o_refs[g][r].astype(F32) for r in range(dils[g])], wide_ref, dils[g])
            o = term if o is None else o + term
        ob = o.astype(BF16)
        om_ref[...] = ob
        y_ref[...] = x_ref[...] + jnp.dot(ob, w_ref[...], preferred_element_type=F32)
        for g in range(ng):
            for r, part in enumerate(_split_rows(tot, narrow_ref, dils[g])):
                lt_refs[g][r] = part

    row = pl.BlockSpec((ts, d), lambda i: (i, 0))
    e = _head_expand()
    outs = pl.pallas_call(
        body, name=f"out_proj_{tag}", grid=(t // ts,),
        in_specs=[row, _resident(w.shape), _resident(e.shape)] + [_res_spec(seq, dl, Q_W) for dl in dils]
                 + [_res_spec(seq, dl, LANES) for dl in dils],
        out_specs=[row, pl.BlockSpec((ts, Q_W), lambda i: (i, 0))] + [_res_spec(seq, dl, LANES) for dl in dils],
        out_shape=[jax.ShapeDtypeStruct((t, d), F32), jax.ShapeDtypeStruct((t, Q_W), BF16)]
                  + [jax.ShapeDtypeStruct(_res_shape(bl, seq, dl, LANES), F32) for dl in dils],
        scratch_shapes=[_stage(ts, Q_W), _stage(ts, LANES)],
        compiler_params=_params("parallel"),
    )(x, w, e, *os, *lses)
    return outs[0], outs[1], list(outs[2:])


def _sigmoid(g):
    return 1.0 / (1.0 + jnp.exp(-g))


def _ffn_fwd(x, gain, wg, wu, wd, tag, exchange=()):
    t, d = x.shape
    f = wg.shape[1]
    tm = min(256, t)

    def body(x_ref, gain_ref, wg_ref, wu_ref, wd_ref, y_ref, g_ref, u_ref):
        xv = x_ref[...]
        h = (xv * _rms(xv) * gain_ref[...]).astype(BF16)
        g = jnp.dot(h, wg_ref[...], preferred_element_type=F32)
        u = jnp.dot(h, wu_ref[...], preferred_element_type=F32)
        g_ref[...] = g.astype(BF16)
        u_ref[...] = u.astype(BF16)
        a = (g * _sigmoid(g) * u).astype(BF16)
        y_ref[...] = xv + jnp.dot(a, wd_ref[...], preferred_element_type=F32)

    row = pl.BlockSpec((tm, d), lambda i: (i, 0))
    wide = pl.BlockSpec((tm, f), lambda i: (i, 0))
    outs, exchanged = _hosted_call(
        body, exchange, name=f"ffn_fwd_{tag}", grid=(t // tm,),
        in_specs=[row, _resident((1, d)), _resident(wg.shape), _resident(wu.shape), _resident(wd.shape)],
        out_specs=[row, wide, wide],
        out_shape=[jax.ShapeDtypeStruct((t, d), F32), jax.ShapeDtypeStruct((t, f), BF16), jax.ShapeDtypeStruct((t, f), BF16)],
        scratch_shapes=[], semantics=("parallel",), args=(x, gain, wg, wu, wd))
    return (*outs, exchanged)


def _loss_bwd(x, gain, target):
    t, d = x.shape
    tm = min(512, t)

    def body(x_ref, gain_ref, t_ref, dx_ref, loss_ref, dgain_ref):
        xv, gain_v = x_ref[...], gain_ref[...]
        xhat = xv * _rms(xv)
        err = xhat * gain_v - t_ref[...]
        dy = err * (1.0 / d)
        dx, _ = _rms_bwd(dy, xv, gain_v)
        dx_ref[...] = dx
        first = pl.program_id(0) == 0
        part = 0.5 * jnp.sum(jnp.mean(err * err, axis=-1, keepdims=True), axis=0, keepdims=True)
        _accumulate(loss_ref, jnp.broadcast_to(part, loss_ref.shape), first)
        _accumulate(dgain_ref, jnp.sum(dy * xhat, axis=0, keepdims=True), first)

    row = pl.BlockSpec((tm, d), lambda i: (i, 0))
    return pl.pallas_call(
        body, name="loss_bwd", grid=(t // tm,), in_specs=[row, _resident((1, d)), row],
        out_specs=[row, pl.BlockSpec((1, LANES), lambda i: (0, 0)), pl.BlockSpec((1, d), lambda i: (0, 0))],
        out_shape=[jax.ShapeDtypeStruct((t, d), F32), jax.ShapeDtypeStruct((1, LANES), F32), jax.ShapeDtypeStruct((1, d), F32)],
        compiler_params=_params("arbitrary"),
    )(x, gain, target)


def _ffn_bwd(dy, x, gain, g, u, wg, wu, wd, tag, exchange=()):
    t, d = x.shape
    f = wg.shape[1]
    tm = min(256, t)

    def body(dy_ref, x_ref, gain_ref, g_ref, u_ref, wg_ref, wu_ref, wd_ref, dx_ref, dg_ref, du_ref, a_ref, h_ref, dgain_ref):
        dyv = dy_ref[...]
        da = lax.dot_general(dyv.astype(BF16), wd_ref[...], NT, preferred_element_type=F32)
        gv, uv = g_ref[...].astype(F32), u_ref[...].astype(F32)
        sg = _sigmoid(gv)
        act = gv * sg
        a_ref[...] = (act * uv).astype(BF16)
        du = (da * act).astype(BF16)
        dg = (da * uv * (sg * (1.0 + gv * (1.0 - sg)))).astype(BF16)
        du_ref[...] = du
        dg_ref[...] = dg
        dh = (lax.dot_general(dg, wg_ref[...], NT, preferred_element_type=F32)
              + lax.dot_general(du, wu_ref[...], NT, preferred_element_type=F32))
        xv, gain_v = x_ref[...], gain_ref[...]
        dx, xhat = _rms_bwd(dh, xv, gain_v)
        dx_ref[...] = dyv + dx
        h_ref[...] = (xhat * gain_v).astype(BF16)
        _accumulate(dgain_ref, jnp.sum(dh * xhat, axis=0, keepdims=True), pl.program_id(0) == 0)

    row = pl.BlockSpec((tm, d), lambda i: (i, 0))
    wide = pl.BlockSpec((tm, f), lambda i: (i, 0))
    outs, exchanged = _hosted_call(
        body, exchange, name=f"ffn_bwd_{tag}", grid=(t // tm,),
        in_specs=[row, row, _resident((1, d)), wide, wide, _resident(wg.shape), _resident(wu.shape), _resident(wd.shape)],
        out_specs=[row, wide, wide, wide, row, pl.BlockSpec((1, d), lambda i: (0, 0))],
        out_shape=[jax.ShapeDtypeStruct((t, d), F32), jax.ShapeDtypeStruct((t, f), BF16), jax.ShapeDtypeStruct((t, f), BF16),
                   jax.ShapeDtypeStruct((t, f), BF16), jax.ShapeDtypeStruct((t, d), BF16), jax.ShapeDtypeStruct((1, d), F32)],
        scratch_shapes=[], semantics=("arbitrary",), args=(dy, x, gain, g, u, wg, wu, wd))
    return (*outs, exchanged)


def _tn_matmul(a, b, name, into=None, column=0, columns=1, exchange=()):
    t, k = a.shape
    n = b.shape[1]
    tk = k // 2 if (k // 2) % LANES == 0 else k
    tt = min(2048, t)

    def body(a_ref, b_ref, *rest):
        o_ref, acc_ref = rest[-2:]
        prod = lax.dot_general(a_ref[...].astype(BF16), b_ref[...].astype(BF16), TN, preferred_element_type=F32)
        j = pl.program_id(1)

        @pl.when(j == 0)
        def _():
            acc_ref[...] = prod

        @pl.when(j > 0)
        def _():
            acc_ref[...] += prod

        @pl.when(j == pl.num_programs(1) - 1)
        def _():
            o_ref[...] = acc_ref[...].astype(BF16)

    (dw,), exchanged = _hosted_call(
        body, exchange, name=name, grid=(k // tk, t // tt),
        in_specs=[pl.BlockSpec((tt, tk), lambda i, j: (j, i)), pl.BlockSpec((tt, n), lambda i, j: (j, 0))]
                 + ([ANY] if into is not None else []),
        out_specs=pl.BlockSpec((tk, n), lambda i, j: (i, column)),
        out_shape=jax.ShapeDtypeStruct((k, columns * n), BF16),
        scratch_shapes=[pltpu.VMEM((tk, n), F32)],
        input_output_aliases={2: 0} if into is not None else {},
        semantics=("parallel", "arbitrary"), args=(a, b, *([into] if into is not None else [])))
    return (dw, exchanged) if exchange else dw


def _attn_out_bwd(dx, w, o, dils, seq, tag, lse=None, sink=None):
    t, d = dx.shape
    ts = _tile_rows(seq)
    bl = t // seq
    ng = len(dils)
    has_sink = sink is not None
    expand = _head_expand().T

    def body(*refs):
        refs = list(refs)
        dx_ref, w_ref, o_ref, e_ref = refs[:4]
        refs = refs[4:]
        lse_ref, sink_ref = (refs.pop(0), refs.pop(0)) if has_sink else (None, None)
        do_refs, dl_refs = refs[:ng], refs[ng:2 * ng]
        refs = refs[2 * ng:]
        dsink_ref = refs.pop(0) if has_sink else None
        dof_ref, dlf_ref = refs
        do = lax.dot_general(dx_ref[...].astype(BF16), w_ref[...], NT, preferred_element_type=F32)
        prod = do * o_ref[...].astype(F32)
        hi = prod.astype(BF16)
        lo = (prod - hi.astype(F32)).astype(BF16)
        e = e_ref[...]
        dl = jnp.dot(hi, e, preferred_element_type=F32) + jnp.dot(lo, e, preferred_element_type=F32)
        for g in range(ng):
            for r, part in enumerate(_split_rows(do, dof_ref, dils[g])):
                do_refs[g][r] = part.astype(BF16)
            for r, part in enumerate(_split_rows(dl, dlf_ref, dils[g])):
                dl_refs[g][r] = part
        if has_sink:
            part = -jnp.exp(sink_ref[...] - lse_ref[...]) * dl
            _accumulate(dsink_ref, jnp.sum(part, axis=0, keepdims=True), pl.program_id(0) == 0)

    row = pl.BlockSpec((ts, d), lambda i: (i, 0))
    narrow = pl.BlockSpec((ts, LANES), lambda i: (i, 0))
    args = [dx, w, o, expand]
    in_specs = [row, _resident(w.shape), pl.BlockSpec((ts, Q_W), lambda i: (i, 0)), _resident(expand.shape)]
    if has_sink:
        args += [lse, jnp.pad(sink.reshape(1, N_HEADS), ((0, 0), (0, LANES - N_HEADS)))]
        in_specs += [narrow, _resident((1, LANES))]
    out_specs = [_res_spec(seq, dl, Q_W) for dl in dils] + [_res_spec(seq, dl, LANES) for dl in dils]
    out_shape = ([jax.ShapeDtypeStruct(_res_shape(bl, seq, dl, Q_W), BF16) for dl in dils]
                 + [jax.ShapeDtypeStruct(_res_shape(bl, seq, dl, LANES), F32) for dl in dils])
    if has_sink:
        out_specs.append(pl.BlockSpec((1, LANES), lambda i: (0, 0)))
        out_shape.append(jax.ShapeDtypeStruct((1, LANES), F32))
    outs = pl.pallas_call(
        body, name=f"attn_out_bwd_{tag}", grid=(t // ts,), in_specs=in_specs, out_specs=out_specs, out_shape=out_shape,
        scratch_shapes=[_stage(ts, Q_W), _stage(ts, LANES)],
        compiler_params=_params("arbitrary" if has_sink else "parallel"),
    )(*args)
    return list(outs[:ng]), list(outs[ng:2 * ng]), (outs[2 * ng] if has_sink else None)


def _attn_bwd(qkv, do, lse, delta, cos, sin, w, tag, exchange=()):
    shape = qkv.shape
    dil = shape[1]
    rows_all = _seq_view(qkv)
    nseq, length, _ = rows_all.shape
    bq = min(QUERY_BLOCK, length)
    wk = _key_rows(bq, w, length)
    nb = length // bq

    def body(qkv_ref, do_ref, lse_ref, dl_ref, cos_ref, sin_ref, dp_ref, kk_ref, vv_ref, dk_ref, dv_ref):
        _pair_variants(qkv_ref, Q_W, kk_ref)
        _pair_variants(qkv_ref, Q_W + KV_W, vv_ref)
        dk_ref[...] = jnp.zeros_like(dk_ref)
        dv_ref[...] = jnp.zeros_like(dv_ref)
        band = _band(bq, wk)
        lo_q = lax.broadcasted_iota(jnp.int32, (bq, LANES), 1) < HEAD_DIM
        hi_q = jnp.logical_not(lo_q)

        def block(i, carry):
            q0, k0 = _window(i, bq, w, wk, length)
            valid = jnp.abs(band + (q0 - k0)) <= w
            rows, krows = pl.ds(q0, bq), pl.ds(k0, wk)
            c, sn = cos_ref[rows, :], -sin_ref[rows, :]
            lse_t, dl_t = lse_ref[rows, :], dl_ref[rows, :]
            for kv in range(N_KV):
                heads = [(kv * GRP + h, h % 2) for h in range(GRP)]
                cols = [slice((kv * 2 + j) * LANES, (kv * 2 + j + 1) * LANES) for j in range(GRP // 2)]
                qp = [qkv_ref[rows, cs] for cs in cols]
                dop = [do_ref[rows, cs] for cs in cols]
                k2 = jnp.concatenate([kk_ref[kv, 0, krows, :], kk_ref[kv, 1, krows, :]], axis=0)
                v2 = jnp.concatenate([vv_ref[kv, 0, krows, :], vv_ref[kv, 1, krows, :]], axis=0)
                sc2 = [lax.dot_general(q_, k2, NT, preferred_element_type=F32) for q_ in qp]
                dp2 = [lax.dot_general(d_, v2, NT, preferred_element_type=F32) for d_ in dop]
                sc = [s_[:, half * wk:(half + 1) * wk] for s_ in sc2 for half in range(2)]
                dp = [d_[:, half * wk:(half + 1) * wk] for d_ in dp2 for half in range(2)]
                p = [jnp.exp(jnp.where(valid, s_, NEG_INF) - _over_keys(lse_t[:, hd:hd + 1], wk))
                     for s_, (hd, _) in zip(sc, heads)]
                ds = [(p_ * (dp_ - _over_keys(dl_t[:, hd:hd + 1], wk))).astype(BF16) for p_, dp_, (hd, _) in zip(p, dp, heads)]
                pb = [p_.astype(BF16) for p_ in p]
                for j in range(GRP // 2):
                    dq = jnp.dot(jnp.concatenate([ds[2 * j], ds[2 * j + 1]], axis=1), k2, preferred_element_type=F32) * SCALE
                    dp_ref[rows, cols[j]] = _rope(dq, c, sn).astype(BF16)
                zero = jnp.zeros((bq, LANES), BF16)
                q4 = jnp.concatenate([jnp.where(lo_q if h % 2 == 0 else hi_q, qp[h // 2], zero) for h in range(GRP)], axis=0)
                do4 = jnp.concatenate([jnp.where(lo_q if h % 2 == 0 else hi_q, dop[h // 2], zero) for h in range(GRP)], axis=0)
                dk_ref[kv, krows, :] += lax.dot_general(jnp.concatenate(ds, axis=0), q4, TN, preferred_element_type=F32)
                dv_ref[kv, krows, :] += lax.dot_general(jnp.concatenate(pb, axis=0), do4, TN, preferred_element_type=F32)
            return carry

        lax.fori_loop(0, nb, block, 0)
        lo = lax.broadcasted_iota(jnp.int32, (length, LANES), 1) < HEAD_DIM
        c, sn = cos_ref[...], -sin_ref[...]
        for ch in range(KV_W // LANES):
            halves = []
            for acc_ref in (dk_ref, dv_ref):
                even, odd = acc_ref[2 * ch], acc_ref[2 * ch + 1]
                even = even + pltpu.roll(even, HEAD_DIM, 1)
                odd = odd + pltpu.roll(odd, HEAD_DIM, 1)
                halves.append(jnp.where(lo, even, odd))
            dp_ref[:, Q_W + ch * LANES:Q_W + (ch + 1) * LANES] = _rope(halves[0], c, sn).astype(BF16)
            dp_ref[:, Q_W + KV_W + ch * LANES:Q_W + KV_W + (ch + 1) * LANES] = halves[1].astype(BF16)

    mode = dict(pipeline_mode=pl.Buffered(1)) if dil == 1 else {}

    def seq_block(c):
        return pl.BlockSpec((None, length, c), lambda i: (i, 0, 0), **mode)

    table = pl.BlockSpec((None, length, LANES), lambda i: (i % dil, 0, 0), **mode)
    (out,), exchanged = _hosted_call(
        body, exchange, name=f"attn_bwd_{tag}", grid=(nseq,),
        in_specs=[seq_block(QKV_W), seq_block(Q_W), seq_block(LANES), seq_block(LANES), table, table],
        out_specs=[pl.BlockSpec((None, length, QKV_W), lambda i: (i, 0, 0))],
        out_shape=[jax.ShapeDtypeStruct((nseq, length, QKV_W), BF16)],
        scratch_shapes=[pltpu.VMEM((N_KV, 2, length, LANES), BF16), pltpu.VMEM((N_KV, 2, length, LANES), BF16),
                        pltpu.VMEM((N_KV, length, LANES), F32), pltpu.VMEM((N_KV, length, LANES), F32)],
        semantics=("parallel",), args=(rows_all, _seq_view(do), _seq_view(lse), _seq_view(delta), cos, sin))
    return out.reshape(shape), exchanged


def _qkv_bwd(dy, x, gain, w, dps, dils, seq, tag, exchange=()):
    t, d = x.shape
    ts = _tile_rows(seq)
    bl = t // seq
    ng = len(dps)

    def body(dy_ref, x_ref, gain_ref, w_ref, *refs):
        dp_refs, dx_ref = refs[:ng], refs[ng]
        h_refs = refs[ng + 1:2 * ng + 1]
        dgain_ref, stage_ref = refs[2 * ng + 1:]
        dh = None
        for gi in range(ng):
            dil = dils[gi]
            n = ts // dil
            dp = dp_refs[gi][0] if dil == 1 else jnp.concatenate([dp_refs[gi][r] for r in range(dil)], axis=0)
            part = lax.dot_general(dp, w_ref[:, gi * QKV_W:(gi + 1) * QKV_W], NT, preferred_element_type=F32)
            part = _merge_rows([part[r * n:(r + 1) * n] for r in range(dil)], stage_ref, dil)
            dh = part if dh is None else dh + part
        xv, gain_v = x_ref[...], gain_ref[...]
        dx, xhat = _rms_bwd(dh, xv, gain_v)
        dx_ref[...] = dy_ref[...] + dx
        h = xhat * gain_v
        for gi in range(ng):
            for r, part in enumerate(_split_rows(h, stage_ref, dils[gi])):
                h_refs[gi][r] = part.astype(BF16)
        _accumulate(dgain_ref, jnp.sum(dh * xhat, axis=0, keepdims=True), pl.program_id(0) == 0)

    row = pl.BlockSpec((ts, d), lambda i: (i, 0))
    outs, exchanged = _hosted_call(
        body, exchange, name=f"qkv_bwd_{tag}", grid=(t // ts,),
        in_specs=[row, row, _resident((1, d)), _resident(w.shape)] + [_res_spec(seq, dl, QKV_W) for dl in dils],
        out_specs=[row] + [_res_spec(seq, dl, d) for dl in dils] + [pl.BlockSpec((1, d), lambda i: (0, 0))],
        out_shape=[jax.ShapeDtypeStruct((t, d), F32)] + [jax.ShapeDtypeStruct(_res_shape(bl, seq, dl, d), BF16) for dl in dils]
                  + [jax.ShapeDtypeStruct((1, d), F32)],
        scratch_shapes=[_stage(ts, d)], semantics=("arbitrary",), args=(dy, x, gain, w, *dps))
    return outs[0], list(outs[1:1 + ng]), outs[1 + ng], exchanged


ANY = pl.BlockSpec(memory_space=pl.ANY)


def _place():
    x, y, c = lax.axis_index("x"), lax.axis_index("y"), lax.axis_index("c")
    return x, y, c


def _exchange_steps(srcs, dsts, gather, send_sems, recv_sems, local_sems):
    x, y, c = _place()
    me, sibling = (x, y, c), (x, y, 1 - c)
    chips = [(1 - x, y), (x, 1 - y), (1 - x, 1 - y)]
    mine = 4 * x + 2 * y + c

    def slot(a, device):
        px, py, pc = device
        return dsts[a].at[4 * px + 2 * py + pc]

    def passes(a, k, block, to, src=None):
        rows = slot(a, block)
        return pltpu.make_async_remote_copy(src_ref=rows if src is None else src, dst_ref=rows, send_sem=send_sems.at[a, k],
                                            recv_sem=recv_sems.at[a, k], device_id=to, device_id_type=MESH)

    def scatters(a, k):
        peer = mine ^ k
        return pltpu.make_async_remote_copy(
            src_ref=srcs[a].at[peer], dst_ref=dsts[a].at[mine], send_sem=send_sems.at[a, k - 1], recv_sem=recv_sems.at[a, k - 1],
            device_id=(peer // 4, (peer // 2) % 2, peer % 2), device_id_type=MESH)

    def local(a):
        return pltpu.make_async_copy(srcs[a] if gather[a] else srcs[a].at[mine], dsts[a].at[mine], local_sems.at[a])

    def first_copies(a):
        if not gather[a]:
            return [scatters(a, k) for k in range(1, N_DEV)]
        return [passes(a, 0, me, sibling, src=srcs[a])] + [passes(a, 1 + j, me, (*chip, c), src=srcs[a]) for j, chip in enumerate(chips)]

    def start():
        for a in range(len(srcs)):
            local(a).start()
            for cp in first_copies(a):
                cp.start()

    def forward():
        for a in range(len(srcs)):
            if gather[a]:
                for j, chip in enumerate(chips):
                    passes(a, 1 + j, (*chip, c), me).wait_recv()
                    passes(a, 4 + j, (*chip, c), sibling).start()

    def finish():
        for a in range(len(srcs)):
            if gather[a]:
                passes(a, 0, sibling, me).wait_recv()
                for j, chip in enumerate(chips):
                    passes(a, 4 + j, (*chip, 1 - c), me).wait_recv()
                    passes(a, 4 + j, (*chip, c), sibling).wait_send()
                for cp in first_copies(a):
                    cp.wait_send()
            else:
                for cp in first_copies(a):
                    cp.wait()
            local(a).wait()

    return start, forward, finish


def _exchange_scratch(n):
    return [pltpu.SemaphoreType.DMA((n, N_DEV - 1)), pltpu.SemaphoreType.DMA((n, N_DEV - 1)), pltpu.SemaphoreType.DMA((n,))]


def _exchanged_shapes(exchange):
    return [jax.ShapeDtypeStruct(((N_DEV,) + a.shape) if g else a.shape, a.dtype) for a, g in exchange]


def _hosted_call(body, exchange, *, name, grid, in_specs, out_specs, out_shape, scratch_shapes, semantics, args,
                 input_output_aliases=None):
    single = not isinstance(out_shape, (list, tuple))
    out_specs, out_shape = ([out_specs], [out_shape]) if single else (list(out_specs), list(out_shape))
    scratch, aliases = list(scratch_shapes), dict(input_output_aliases or {})
    if not exchange:
        outs = pl.pallas_call(body, name=name, grid=grid, in_specs=in_specs, out_specs=out_specs, out_shape=out_shape,
                              scratch_shapes=scratch, input_output_aliases=aliases, compiler_params=_params(*semantics))(*args)
        return list(outs), []
    n, n_in, n_out, n_scr = len(exchange), len(in_specs), len(out_specs), len(scratch)
    gather = [g for _, g in exchange]
    steps = math.prod(grid)

    def hosted(*refs):
        own_in, x_in = refs[:n_in], refs[n_in:n_in + n]
        own_out, x_out = refs[n_in + n:n_in + n + n_out], refs[n_in + n + n_out:n_in + 2 * n + n_out]
        own_scr, sems = refs[n_in + 2 * n + n_out:n_in + 2 * n + n_out + n_scr], refs[-3:]
        step = pl.program_id(0)
        for axis in range(1, len(grid)):
            step = step * grid[axis] + pl.program_id(axis)
        start, forward, finish = _exchange_steps(x_in, x_out, gather, *sems)
        pl.when(step == 0)(start)
        body(*own_in, *own_out, *own_scr)
        pl.when(step == steps // 2)(forward)
        pl.when(step == steps - 1)(finish)

    outs = pl.pallas_call(
        hosted, name=name, grid=grid, in_specs=list(in_specs) + [ANY] * n, out_specs=out_specs + [ANY] * n,
        out_shape=out_shape + _exchanged_shapes(exchange), scratch_shapes=scratch + _exchange_scratch(n),
        input_output_aliases=aliases, compiler_params=_params(*["arbitrary"] * len(grid)),
    )(*args, *[a for a, _ in exchange])
    return list(outs[:n_out]), list(outs[n_out:])


def _exchange_now(exchange, name):
    n = len(exchange)
    gather = [g for _, g in exchange]

    def body(*refs):
        for step in _exchange_steps(refs[:n], refs[n:2 * n], gather, *refs[2 * n:]):
            step()

    return pl.pallas_call(
        body, name=name, in_specs=[ANY] * n, out_specs=[ANY] * n, out_shape=_exchanged_shapes(exchange),
        scratch_shapes=_exchange_scratch(n),
    )(*[a for a, _ in exchange])


def _all_reduce_small(v):
    def body(v_ref, o_ref, recv_ref, send_sems, recv_sems):
        x, y, c = _place()
        me = 4 * x + 2 * y + c
        copies = []
        for k in range(1, N_DEV):
            peer = me ^ k
            copies.append(pltpu.make_async_remote_copy(
                src_ref=v_ref, dst_ref=recv_ref.at[k], send_sem=send_sems.at[k - 1], recv_sem=recv_sems.at[k - 1],
                device_id=(peer // 4, (peer // 2) % 2, peer % 2), device_id_type=MESH))
        for cp in copies:
            cp.start()
        recv_ref[0] = v_ref[...]
        for cp in copies:
            cp.wait()
        acc = recv_ref[me]
        for src in range(1, N_DEV):
            acc = acc + recv_ref[me ^ src]
        o_ref[...] = acc

    vm = pl.BlockSpec(memory_space=pltpu.VMEM)
    return pl.pallas_call(
        body, name="all_reduce_small", in_specs=[vm], out_specs=vm, out_shape=jax.ShapeDtypeStruct(v.shape, F32),
        scratch_shapes=[pltpu.VMEM((N_DEV,) + v.shape, F32), pltpu.SemaphoreType.DMA((N_DEV - 1,)),
                        pltpu.SemaphoreType.DMA((N_DEV - 1,))],
    )(v)


def _adamw_math(w, g, m, v):
    m = ADAM_B1 * m + (1.0 - ADAM_B1) * g
    v = ADAM_B2 * v + (1.0 - ADAM_B2) * (g * g)
    m_hat = m / (1.0 - ADAM_B1 ** ADAM_STEP)
    v_hat = v / (1.0 - ADAM_B2 ** ADAM_STEP)
    delta = -ADAM_LR * (m_hat / (jnp.sqrt(v_hat) + ADAM_EPS) + ADAM_WD * w)
    return delta, m, v


def _adamw(parts, w, m, v, name, layer=None, into=None):
    r, c = w.shape[-2:]
    tr = r // 2 if r % 16 == 0 and r >= 256 else r
    n = len(parts)

    def body(*refs):
        w_ref, m_ref, v_ref = refs[n:n + 3]
        g_ref, d_ref, nm_ref, nv_ref = refs[-4:]
        g = refs[0][...].astype(F32)
        for p_ref in refs[1:n]:
            g = g + p_ref[...].astype(F32)
        g_ref[...] = g
        d_ref[...], nm_ref[...], nv_ref[...] = _adamw_math(w_ref[...], g, m_ref[...], v_ref[...])

    def slab(slot):
        return pl.BlockSpec((None, tr, c), lambda i: (slot, i, 0))

    tile = pl.BlockSpec((tr, c), lambda i: (i, 0)) if layer is None else slab(layer)
    arrays, in_specs = [], []
    for p in parts:
        if isinstance(p, tuple):
            arrays.append(p[0])
            in_specs.append(slab(p[1]))
        else:
            arrays.append(p)
            in_specs.append(tile)
    kept = list(into) if into is not None else []
    return pl.pallas_call(
        body, name=name, grid=(r // tr,), in_specs=in_specs + [tile] * 3 + [ANY] * len(kept), out_specs=[tile] * 4,
        out_shape=[jax.ShapeDtypeStruct(w.shape, F32)] * 4,
        input_output_aliases={n + 3 + k: k for k in range(len(kept))}, compiler_params=_params("parallel"),
    )(*arrays, w, m, v, *kept)


def _columns(g):
    return g.transpose(1, 0, 2).reshape(g.shape[1], -1)


def _rows(g):
    return g.reshape(-1, g.shape[-1])


def _column_blocks(dw):
    k, n = dw.shape
    return dw.reshape(k, N_DEV, n // N_DEV).transpose(1, 0, 2)


def _row_blocks(dw):
    k, n = dw.shape
    return dw.reshape(N_DEV, k // N_DEV, n)


def _pack_rows(rows, width):
    out = None
    for i, r in enumerate(rows):
        r = r.reshape(1, -1).astype(F32)
        r = jnp.pad(r, ((i, 8 - 1 - i), (0, width - r.shape[1])))
        out = r if out is None else out + r
    return out


def _mixer_fwd(x, gain, w_in, w_out, cos, sin, seq, groups, tag, sink=None, exchanges=None):
    qkvs, os, lses, got = [], [], [], {}
    for gi, (dil, w) in enumerate(groups):
        qkv, got["proj", gi] = _qkv_proj(x, gain, w_in, _tables_tiled(cos, seq, dil), _tables_tiled(sin, seq, dil), seq, dil, gi,
                                         f"{tag}{gi}", exchange=(exchanges or {}).get(("proj", gi), ()))
        o, lse, got[gi] = _attn_fwd(qkv, w, f"{tag}{gi}", sink=sink, exchange=(exchanges or {}).get(gi, ()))
        qkvs.append(qkv)
        os.append(o)
        lses.append(lse)
    y, o, lses = _out_proj(x, os, lses, [dl for dl, _ in groups], w_out, seq, tag)
    return y, (qkvs, o, lses), got


def _mixer_bwd(dy, x_in, gain, w_in, w_out, saved, cos, sin, seq, groups, tag, sink=None, exchanges=None, scatter_dw_out=False):
    qkvs, o, lses = saved
    t, d = x_in.shape
    dils = [dl for dl, _ in groups]
    lse_tokens = lses[0].reshape(t, LANES) if sink is not None else None
    dos, dls, dsink = _attn_out_bwd(dy, w_out, o, dils, seq, tag, lse=lse_tokens, sink=sink)
    dw_out = _tn_matmul(o, dy, f"dw_out_{tag}")
    exchanges = {gi: list(e) for gi, e in (exchanges or {}).items()}
    if scatter_dw_out:
        exchanges[0] = exchanges.get(0, []) + _to_send([], [dw_out])
    dps, got = [], {}
    for gi, (dil, w) in enumerate(groups):
        dp, got[gi] = _attn_bwd(qkvs[gi], dos[gi], lses[gi], dls[gi], _tables_by_residue(cos, seq, dil),
                                _tables_by_residue(sin, seq, dil), w, f"{tag}{gi}", exchange=exchanges.get(gi, ()))
        dps.append(dp)
    if scatter_dw_out:
        dw_out = got[0].pop()
    dx, hs, dgain, _ = _qkv_bwd(dy, x_in, gain, w_in, dps, dils, seq, tag)
    dw_in = None
    for gi in range(len(groups)):
        dw_in = _tn_matmul(hs[gi].reshape(t, d), dps[gi].reshape(t, QKV_W), f"dw_in_{tag}{gi}", into=dw_in, column=gi,
                           columns=len(groups))
    return dx, dw_in, dw_out, dgain, dsink, got


def _ffn_layer_bwd(dy, x_in, gain, g, u, wg, wu, wd, tag, exchange=()):
    dx, dg, du, act, h, dgain, got = _ffn_bwd(dy, x_in, gain, g, u, wg, wu, wd, tag, exchange=exchange)
    dwd = _tn_matmul(act, dy, f"dw_down_{tag}")
    dwg = _tn_matmul(h, dg, f"dw_gate_{tag}")
    dwu = _tn_matmul(h, du, f"dw_up_{tag}")
    return dx, dwg, dwu, dwd, dgain, got


def _to_send(dws_by_columns, dws_by_rows):
    return [(_column_blocks(g), False) for g in dws_by_columns] + [(_row_blocks(g), False) for g in dws_by_rows]


def kernel(x, a_w_in, a_sink, a_w_out, b_w_in, b_w_out, norm_mix, norm_ffn, w_gate, w_up, w_down, final_norm, loss_target, m_a_w_in, m_a_sink, m_a_w_out, m_b_w_in, m_b_w_out, m_norm_mix, m_norm_ffn, m_w_gate, m_w_up, m_w_down, m_final_norm, v_a_w_in, v_a_sink, v_a_w_out, v_b_w_in, v_b_w_out, v_norm_mix, v_norm_ffn, v_w_gate, v_w_up, v_w_down, v_final_norm):
    bl, seq, d = x.shape
    t = bl * seq
    xf = x.reshape(t, d)
    target = loss_target.reshape(t, d)
    cos, sin = _rope_tables(seq)
    groups_a = [(1, ATTN_HALF_WINDOW)]
    groups_b = [(dil, window // 2 // dil) for window, dil in DILATED_GROUPS]

    def shard(w_):
        return w_.astype(BF16)

    wa_in, wa_out = _exchange_now([(shard(a_w_in[0]), True), (shard(a_w_out[0]), True)], "gather_first")
    wa_in, wa_out = _columns(wa_in), _rows(wa_out)

    x1_0, saved_a, got = _mixer_fwd(xf, norm_mix[0:1], wa_in, wa_out, cos, sin, seq, groups_a, "a", sink=a_sink[0],
                                    exchanges={("proj", 0): [(shard(w_down[0]), True)],
                                               0: [(shard(w_gate[0]), True), (shard(w_up[0]), True)]})
    wg0, wu0, wd0 = _columns(got[0][0]), _columns(got[0][1]), _rows(got["proj", 0][0])
    x2_0, g0, u0, got = _ffn_fwd(x1_0, norm_ffn[0:1], wg0, wu0, wd0, "0",
                                 exchange=[(shard(b_w_in[0]), True), (shard(b_w_out[0]), True)])
    wb_in, wb_out = _columns(got[0]), _rows(got[1])
    x1_1, saved_b, got = _mixer_fwd(x2_0, norm_mix[1:2], wb_in, wb_out, cos, sin, seq, groups_b, "b",
                                    exchanges={0: [(shard(w_gate[1]), True)], 1: [(shard(w_up[1]), True)],
                                               2: [(shard(w_down[1]), True)]})
    wg1, wu1, wd1 = _columns(got[0][0]), _columns(got[1][0]), _rows(got[2][0])
    x2_1, g1, u1, _ = _ffn_fwd(x1_1, norm_ffn[1:2], wg1, wu1, wd1, "1")

    dy, loss_part, d_final = _loss_bwd(x2_1, final_norm.reshape(1, d), target)
    dy, dwg1, dwu1, dwd1, d_nf1, _ = _ffn_layer_bwd(dy, x1_1, norm_ffn[1:2], g1, u1, wg1, wu1, wd1, "1")
    dy, dwb_in, dwb_out, d_nm1, _, got = _mixer_bwd(
        dy, x2_0, norm_mix[1:2], wb_in, wb_out, saved_b, cos, sin, seq, groups_b, "b",
        exchanges={0: _to_send([dwg1], [dwd1]), 1: _to_send([dwu1], [])})
    (r_g1, r_d1), (r_u1,) = got[0], got[1]
    dy, dwg0, dwu0, dwd0, d_nf0, (r_b_in, r_b_out) = _ffn_layer_bwd(
        dy, x1_0, norm_ffn[0:1], g0, u0, wg0, wu0, wd0, "0", exchange=_to_send([dwb_in], [dwb_out]))
    dy, dwa_in, r_a_out, d_nm0, d_sink, got = _mixer_bwd(
        dy, xf, norm_mix[0:1], wa_in, wa_out, saved_a, cos, sin, seq, groups_a, "a", sink=a_sink[0],
        exchanges={0: _to_send([dwg0, dwu0], [dwd0])}, scatter_dw_out=True)
    r_g0, r_u0, r_d0 = got[0]
    (r_a_in,) = _exchange_now(_to_send([dwa_in], []), "scatter_last")
    grad_x = dy.reshape(bl, seq, d)

    def update(received, w_, m_, v_, name):
        out = None
        for layer in reversed(range(len(received))):
            out = _adamw([(received[layer], src) for src in range(N_DEV)], w_, m_, v_, f"adamw_{name}{layer}", layer=layer, into=out)
        return out

    u_a_in = update([r_a_in], a_w_in, m_a_w_in, v_a_w_in, "a_in")
    u_a_out = update([r_a_out], a_w_out, m_a_w_out, v_a_w_out, "a_out")
    u_b_in = update([r_b_in], b_w_in, m_b_w_in, v_b_w_in, "b_in")
    u_b_out = update([r_b_out], b_w_out, m_b_w_out, v_b_w_out, "b_out")
    u_gate = update([r_g0, r_g1], w_gate, m_w_gate, v_w_gate, "gate")
    u_up = update([r_u0, r_u1], w_up, m_w_up, v_w_up, "up")
    u_down = update([r_d0, r_d1], w_down, m_w_down, v_w_down, "down")

    small = _pack_rows([d_nm0, d_nm1, d_nf0, d_nf1, d_final, d_sink, loss_part], d)
    total = _all_reduce_small(small)
    small_w = _pack_rows([norm_mix[0], norm_mix[1], norm_ffn[0], norm_ffn[1], final_norm, a_sink], d)
    small_m = _pack_rows([m_norm_mix[0], m_norm_mix[1], m_norm_ffn[0], m_norm_ffn[1], m_final_norm, m_a_sink], d)
    small_v = _pack_rows([v_norm_mix[0], v_norm_mix[1], v_norm_ffn[0], v_norm_ffn[1], v_final_norm, v_a_sink], d)
    u_small = _adamw([total], small_w, small_m, small_v, "adamw_small")
    loss = total[6, 0]

    outs = []
    for k in range(4):
        sm = u_small[k]
        outs += [u_a_in[k], sm[5:6, :N_HEADS], u_a_out[k], u_b_in[k], u_b_out[k], sm[0:2], sm[2:4],
                 u_gate[k], u_up[k], u_down[k], sm[4]]
    return (loss, grad_x, *outs)
```

```python
import functools
import math

import jax
import jax.numpy as jnp
from jax import lax
from jax.experimental import pallas as pl
from jax.experimental.pallas import tpu as pltpu

F32 = jnp.float32
BF16 = jnp.bfloat16

HEAD_DIM = 64
N_HEADS = 16
N_KV = 4
GRP = N_HEADS // N_KV
Q_W = N_HEADS * HEAD_DIM
KV_W = N_KV * HEAD_DIM
QKV_W = Q_W + 2 * KV_W
ATTN_HALF_WINDOW = 128
DILATED_GROUPS = ((128, 1), (512, 4), (2048, 16))
ROPE_THETA = 10000.0
RMS_EPS = 1e-6
NEG_INF = -1e30
SCALE = 1.0 / math.sqrt(HEAD_DIM)

ADAM_LR = 0.001
ADAM_B1 = 0.9
ADAM_B2 = 0.999
ADAM_EPS = 1e-08
ADAM_WD = 0.01
ADAM_STEP = 10

LANES = 128
VMEM_LIMIT = 56 * 1024 * 1024
QUERY_BLOCK = 128
N_DEV = 8
MESH = pl.DeviceIdType.MESH

NT = (((1,), (1,)), ((), ()))
TN = (((0,), (0,)), ((), ()))


def _params(*sem):
    return pltpu.CompilerParams(dimension_semantics=tuple(sem) if sem else None, vmem_limit_bytes=VMEM_LIMIT)


def _resident(shape):
    return pl.BlockSpec(shape, lambda *_: (0,) * len(shape), pipeline_mode=pl.Buffered(1))


def _rope_tables(seq):
    inv_freq = 1.0 / (ROPE_THETA ** (jnp.arange(0, HEAD_DIM, 2, dtype=F32) / HEAD_DIM))
    ang = jnp.arange(seq, dtype=F32)[:, None] * inv_freq[None, :]
    cos, sin = jnp.cos(ang), jnp.sin(ang)
    return jnp.tile(cos, (1, 4)), jnp.concatenate([-sin, sin, -sin, sin], axis=1)


def _rope(t, cos, sin_signed):
    lane = lax.broadcasted_iota(jnp.int32, t.shape, 1)
    first = (lane & (HEAD_DIM // 2)) == 0
    swapped = jnp.where(first, pltpu.roll(t, LANES - HEAD_DIM // 2, 1), pltpu.roll(t, HEAD_DIM // 2, 1))
    return t * cos + swapped * sin_signed


def _rms(x):
    return lax.rsqrt(jnp.mean(x * x, axis=-1, keepdims=True) + RMS_EPS)


def _rms_bwd(dh, x, gain):
    r = _rms(x)
    xhat = x * r
    dxh = dh * gain
    dx = r * (dxh - xhat * jnp.mean(dxh * xhat, axis=-1, keepdims=True))
    return dx, xhat


def _accumulate(ref, value, first):
    @pl.when(first)
    def _():
        ref[...] = jnp.zeros_like(ref)

    ref[...] += value


def _tile_rows(seq):
    return min(512, seq)


def _res_shape(bl, seq, dil, c):
    ts = _tile_rows(seq)
    return (bl, dil, seq // ts, ts // dil, c)


def _res_spec(seq, dil, c):
    ts = _tile_rows(seq)
    per_seq = seq // ts
    return pl.BlockSpec((None, dil, None, ts // dil, c), lambda i: (i // per_seq, 0, i % per_seq, 0, 0))


def _seq_view(a):
    bl, dil, tiles, n, c = a.shape
    return a.reshape(bl * dil, tiles * n, c)


def _stage(ts, c):
    return pltpu.VMEM((c // LANES, ts, LANES), F32)


def _split_rows(val, stage_ref, dil):
    if dil == 1:
        return [val]
    ts, c = val.shape
    n, nc = ts // dil, c // LANES
    for k in range(nc):
        stage_ref[k] = val[:, k * LANES:(k + 1) * LANES]
    return [jnp.concatenate([stage_ref[k, pl.ds(r, n, stride=dil), :] for k in range(nc)], axis=1) for r in range(dil)]


def _merge_rows(parts, stage_ref, dil):
    if dil == 1:
        return parts[0]
    n, c = parts[0].shape
    nc = c // LANES
    for r, part in enumerate(parts):
        for k in range(nc):
            stage_ref[k, pl.ds(r, n, stride=dil), :] = part[:, k * LANES:(k + 1) * LANES]
    return jnp.concatenate([stage_ref[k] for k in range(nc)], axis=1)


def _tables_tiled(table, seq, dil):
    ts = _tile_rows(seq)
    return table.reshape(seq // ts, ts // dil, dil, LANES).transpose(0, 2, 1, 3).reshape(seq, LANES)


def _tables_by_residue(table, seq, dil):
    return table.reshape(seq // dil, dil, LANES).transpose(1, 0, 2)


def _qkv_proj(x, gain, w, cos, sin, seq, dil, group, tag, exchange=()):
    t, d = x.shape
    ts = _tile_rows(seq)
    n = ts // dil
    per_seq = seq // ts

    def body(x_ref, g_ref, w_ref, cos_ref, sin_ref, o_ref, stage_ref):
        xv = jnp.concatenate(_split_rows(x_ref[...], stage_ref, dil), axis=0)
        h = (xv * _rms(xv) * g_ref[...]).astype(BF16)
        acc = lax.dot_general(h, w_ref[...], NT, preferred_element_type=F32)
        c, s = cos_ref[...], sin_ref[...]
        for j in range(QKV_W // LANES):
            cols = slice(j * LANES, (j + 1) * LANES)
            val = acc[:, cols]
            if j < (Q_W + KV_W) // LANES:
                val = _rope(val, c, s)
            if j < Q_W // LANES:
                val = val * SCALE
            val = val.astype(BF16)
            for r in range(dil):
                o_ref[r, :, cols] = val[r * n:(r + 1) * n]

    table = pl.BlockSpec((ts, LANES), lambda i: (i % per_seq, 0))
    (qkv,), exchanged = _hosted_call(
        body, exchange, name=f"qkv_proj_{tag}", grid=(t // ts,),
        in_specs=[pl.BlockSpec((ts, d), lambda i: (i, 0)), pl.BlockSpec((1, d), lambda i: (0, 0)),
                  pl.BlockSpec((QKV_W, d), lambda i: (group, 0)), table, table],
        out_specs=[_res_spec(seq, dil, QKV_W)],
        out_shape=[jax.ShapeDtypeStruct(_res_shape(t // seq, seq, dil, QKV_W), BF16)],
        scratch_shapes=[_stage(ts, d)], semantics=("parallel",), args=(x, gain, w, cos, sin))
    return qkv, exchanged


def _band(bq, wk):
    return lax.broadcasted_iota(jnp.int32, (bq, wk), 0) - lax.broadcasted_iota(jnp.int32, (bq, wk), 1)


def _pair_variants(src_ref, base, dst_ref):
    lo = lax.broadcasted_iota(jnp.int32, (src_ref.shape[0], LANES), 1) < HEAD_DIM
    for c in range(KV_W // LANES):
        chunk = src_ref[:, base + c * LANES:base + (c + 1) * LANES]
        rolled = pltpu.roll(chunk, HEAD_DIM, 1)
        zero = jnp.zeros_like(chunk)
        dst_ref[2 * c, 0] = jnp.where(lo, chunk, zero)
        dst_ref[2 * c, 1] = jnp.where(lo, zero, rolled)
        dst_ref[2 * c + 1, 0] = jnp.where(lo, rolled, zero)
        dst_ref[2 * c + 1, 1] = jnp.where(lo, zero, chunk)


def _over_keys(col, wk):
    if wk % LANES:
        return jnp.broadcast_to(col, (col.shape[0], wk))
    wide = jnp.broadcast_to(col, (col.shape[0], LANES))
    return wide if wk == LANES else jnp.concatenate([wide] * (wk // LANES), axis=1)


def _key_rows(bq, w, length):
    return min(bq + 2 * w, length)


def _window(i, bq, w, wk, length):
    q0 = pl.multiple_of(i * bq, bq)
    k0 = pl.multiple_of(jnp.clip(q0 - w, 0, length - wk), min(w, bq))
    return q0, k0


def _attn_fwd(qkv, w, tag, sink=None, exchange=()):
    shape = qkv.shape
    rows_all = _seq_view(qkv)
    nseq, length, _ = rows_all.shape
    bq = min(QUERY_BLOCK, length)
    wk = _key_rows(bq, w, length)
    nb = length // bq
    has_sink = sink is not None

    def body(*refs):
        qkv_ref = refs[0]
        sink_ref = refs[1] if has_sink else None
        o_ref, lse_ref, kk_ref, vv_ref = refs[-4:]
        _pair_variants(qkv_ref, Q_W, kk_ref)
        _pair_variants(qkv_ref, Q_W + KV_W, vv_ref)
        band = _band(bq, wk)
        lane = lax.broadcasted_iota(jnp.int32, (bq, LANES), 1)
        lo = lane < HEAD_DIM

        def block(i, carry):
            q0, k0 = _window(i, bq, w, wk, length)
            valid = jnp.abs(band + (q0 - k0)) <= w
            rows, krows = pl.ds(q0, bq), pl.ds(k0, wk)
            lse_tile = jnp.zeros((bq, LANES), F32)
            for kv in range(N_KV):
                heads = [(kv * GRP + h, h % 2) for h in range(GRP)]
                qp = [qkv_ref[rows, (kv * 2 + j) * LANES:(kv * 2 + j + 1) * LANES] for j in range(GRP // 2)]
                k2 = jnp.concatenate([kk_ref[kv, 0, krows, :], kk_ref[kv, 1, krows, :]], axis=0)
                v2 = jnp.concatenate([vv_ref[kv, 0, krows, :], vv_ref[kv, 1, krows, :]], axis=0)
                sc2 = [lax.dot_general(q_, k2, NT, preferred_element_type=F32) for q_ in qp]
                sc = [jnp.where(valid, s_[:, half * wk:(half + 1) * wk], NEG_INF) for s_ in sc2 for half in range(2)]
                m = [jnp.max(s_, axis=-1, keepdims=True) for s_ in sc]
                if has_sink:
                    m = [jnp.maximum(m_, sink_ref[hd]) for m_, (hd, _) in zip(m, heads)]
                mb = [jnp.broadcast_to(m_, (bq, LANES)) for m_ in m]
                p = [jnp.exp(s_ - _over_keys(m_, wk)) for s_, m_ in zip(sc, m)]
                den = [jnp.sum(p_, axis=-1, keepdims=True) for p_ in p]
                if has_sink:
                    den = [d_ + jnp.exp(sink_ref[hd] - m_) for d_, m_, (hd, _) in zip(den, m, heads)]
                inv = [jnp.broadcast_to(1.0 / d_, (bq, LANES)) for d_ in den]
                pb = [p_.astype(BF16) for p_ in p]
                for j in range(GRP // 2):
                    o = jnp.dot(jnp.concatenate([pb[2 * j], pb[2 * j + 1]], axis=1), v2, preferred_element_type=F32)
                    o = o * jnp.where(lo, inv[2 * j], inv[2 * j + 1])
                    o_ref[rows, (kv * 2 + j) * LANES:(kv * 2 + j + 1) * LANES] = o.astype(BF16)
                for h, (hd, _) in enumerate(heads):
                    lse_tile = jnp.where(lane == hd, mb[h] - jnp.log(inv[h]), lse_tile)
            lse_ref[rows, :] = lse_tile
            return carry

        lax.fori_loop(0, nb, block, 0)

    args = [rows_all]
    in_specs = [pl.BlockSpec((None, length, QKV_W), lambda i: (i, 0, 0))]
    if has_sink:
        args.append(sink)
        in_specs.append(pl.BlockSpec(memory_space=pltpu.SMEM))
    (o, lse), exchanged = _hosted_call(
        body, exchange, name=f"attn_fwd_{tag}", grid=(nseq,), in_specs=in_specs,
        out_specs=[pl.BlockSpec((None, length, Q_W), lambda i: (i, 0, 0)), pl.BlockSpec((None, length, LANES), lambda i: (i, 0, 0))],
        out_shape=[jax.ShapeDtypeStruct((nseq, length, Q_W), BF16), jax.ShapeDtypeStruct((nseq, length, LANES), F32)],
        scratch_shapes=[pltpu.VMEM((N_KV, 2, length, LANES), BF16), pltpu.VMEM((N_KV, 2, length, LANES), BF16)],
        semantics=("parallel",), args=args)
    return o.reshape(shape[:-1] + (Q_W,)), lse.reshape(shape[:-1] + (LANES,)), exchanged


def _head_expand():
    return (jnp.arange(LANES)[:, None] == jnp.arange(Q_W)[None, :] // HEAD_DIM).astype(BF16)


def _out_proj(x, os, lses, dils, w, seq, tag):
    t, d = x.shape
    ts = _tile_rows(seq)
    ng = len(os)
    bl = t // seq
    if ng == 1:
        def body1(x_ref, o_ref, w_ref, y_ref):
            y_ref[...] = x_ref[...] + jnp.dot(o_ref[...], w_ref[...], preferred_element_type=F32)

        row = pl.BlockSpec((ts, d), lambda i: (i, 0))
        o = os[0].reshape(t, Q_W)
        y = pl.pallas_call(
            body1, name=f"out_proj_{tag}", grid=(t // ts,), in_specs=[row, row, _resident(w.shape)], out_specs=row,
            out_shape=jax.ShapeDtypeStruct((t, d), F32), compiler_params=_params("parallel"),
        )(x, o, w)
        return y, o, [lses[0]]

    def body(*refs):
        x_ref, w_ref, e_ref = refs[:3]
        o_refs, l_refs = refs[3:3 + ng], refs[3 + ng:3 + 2 * ng]
        y_ref, om_ref = refs[3 + 2 * ng:5 + 2 * ng]
        lt_refs = refs[5 + 2 * ng:5 + 3 * ng]
        wide_ref, narrow_ref = refs[5 + 3 * ng:]
        ls = [_merge_rows([l_refs[g][r] for r in range(dils[g])], narrow_ref, dils[g]) for g in range(ng)]
        mx = functools.reduce(jnp.maximum, ls)
        tot = mx + jnp.log(functools.reduce(lambda a, b: a + b, [jnp.exp(l_ - mx) for l_ in ls]))
        e = e_ref[...]
        o = None
        for g in range(ng):
            wt = jnp.exp(ls[g] - tot)
            hi = wt.astype(BF16)
            lo = (wt - hi.astype(F32)).astype(BF16)
            wide = jnp.dot(hi, e, preferred_element_type=F32) + jnp.dot(lo, e, preferred_element_type=F32)
            term = wide * _merge_rows([o_refs[g][r].astype(F32) for r in range(dils[g])], wide_ref, dils[g])
            o = term if o is None else o + term
        ob = o.astype(BF16)
        om_ref[...] = ob
        y_ref[...] = x_ref[...] + jnp.dot(ob, w_ref[...], preferred_element_type=F32)
        for g in range(ng):
            for r, part in enumerate(_split_rows(tot, narrow_ref, dils[g])):
                lt_refs[g][r] = part

    row = pl.BlockSpec((ts, d), lambda i: (i, 0))
    e = _head_expand()
    outs = pl.pallas_call(
        body, name=f"out_proj_{tag}", grid=(t // ts,),
        in_specs=[row, _resident(w.shape), _resident(e.shape)] + [_res_spec(seq, dl, Q_W) for dl in dils]
                 + [_res_spec(seq, dl, LANES) for dl in dils],
        out_specs=[row, pl.BlockSpec((ts, Q_W), lambda i: (i, 0))] + [_res_spec(seq, dl, LANES) for dl in dils],
        out_shape=[jax.ShapeDtypeStruct((t, d), F32), jax.ShapeDtypeStruct((t, Q_W), BF16)]
                  + [jax.ShapeDtypeStruct(_res_shape(bl, seq, dl, LANES), F32) for dl in dils],
        scratch_shapes=[_stage(ts, Q_W), _stage(ts, LANES)],
        compiler_params=_params("parallel"),
    )(x, w, e, *os, *lses)
    return outs[0], outs[1], list(outs[2:])


def _sigmoid(g):
    return 1.0 / (1.0 + jnp.exp(-g))


def _ffn_fwd(x, gain, wg, wu, wd, tag, exchange=()):
    t, d = x.shape
    f = wd.shape[0]
    tm = min(256, t)

    def body(x_ref, gain_ref, wg_ref, wu_ref, wd_ref, y_ref, g_ref, u_ref):
        xv = x_ref[...]
        h = (xv * _rms(xv) * gain_ref[...]).astype(BF16)
        g = lax.dot_general(h, wg_ref[...], NT, preferred_element_type=F32)
        u = lax.dot_general(h, wu_ref[...], NT, preferred_element_type=F32)
        g_ref[...] = g.astype(BF16)
        u_ref[...] = u.astype(BF16)
        a = (g * _sigmoid(g) * u).astype(BF16)
        y_ref[...] = xv + jnp.dot(a, wd_ref[...], preferred_element_type=F32)

    row = pl.BlockSpec((tm, d), lambda i: (i, 0))
    wide = pl.BlockSpec((tm, f), lambda i: (i, 0))
    outs, exchanged = _hosted_call(
        body, exchange, name=f"ffn_fwd_{tag}", grid=(t // tm,),
        in_specs=[row, _resident((1, d)), _resident(wg.shape), _resident(wu.shape), _resident(wd.shape)],
        out_specs=[row, wide, wide],
        out_shape=[jax.ShapeDtypeStruct((t, d), F32), jax.ShapeDtypeStruct((t, f), BF16), jax.ShapeDtypeStruct((t, f), BF16)],
        scratch_shapes=[], semantics=("parallel",), args=(x, gain, wg, wu, wd))
    return (*outs, exchanged)


def _loss_bwd(x, gain, target):
    t, d = x.shape
    tm = min(512, t)

    def body(x_ref, gain_ref, t_ref, dx_ref, loss_ref, dgain_ref):
        xv, gain_v = x_ref[...], gain_ref[...]
        xhat = xv * _rms(xv)
        err = xhat * gain_v - t_ref[...]
        dy = err * (1.0 / d)
        dx, _ = _rms_bwd(dy, xv, gain_v)
        dx_ref[...] = dx
        first = pl.program_id(0) == 0
        part = 0.5 * jnp.sum(jnp.mean(err * err, axis=-1, keepdims=True), axis=0, keepdims=True)
        _accumulate(loss_ref, jnp.broadcast_to(part, loss_ref.shape), first)
        _accumulate(dgain_ref, jnp.sum(dy * xhat, axis=0, keepdims=True), first)

    row = pl.BlockSpec((tm, d), lambda i: (i, 0))
    return pl.pallas_call(
        body, name="loss_bwd", grid=(t // tm,), in_specs=[row, _resident((1, d)), row],
        out_specs=[row, pl.BlockSpec((1, LANES), lambda i: (0, 0)), pl.BlockSpec((1, d), lambda i: (0, 0))],
        out_shape=[jax.ShapeDtypeStruct((t, d), F32), jax.ShapeDtypeStruct((1, LANES), F32), jax.ShapeDtypeStruct((1, d), F32)],
        compiler_params=_params("arbitrary"),
    )(x, gain, target)


def _ffn_bwd(dy, x, gain, g, u, wg, wu, wd, tag, exchange=()):
    t, d = x.shape
    f = wd.shape[0]
    tm = min(256, t)

    def body(dy_ref, x_ref, gain_ref, g_ref, u_ref, wg_ref, wu_ref, wd_ref, dx_ref, dg_ref, du_ref, a_ref, h_ref, dgain_ref):
        dyv = dy_ref[...]
        da = lax.dot_general(dyv.astype(BF16), wd_ref[...], NT, preferred_element_type=F32)
        gv, uv = g_ref[...].astype(F32), u_ref[...].astype(F32)
        sg = _sigmoid(gv)
        act = gv * sg
        a_ref[...] = (act * uv).astype(BF16)
        du = (da * act).astype(BF16)
        dg = (da * uv * (sg * (1.0 + gv * (1.0 - sg)))).astype(BF16)
        du_ref[...] = du
        dg_ref[...] = dg
        dh = jnp.dot(dg, wg_ref[...], preferred_element_type=F32) + jnp.dot(du, wu_ref[...], preferred_element_type=F32)
        xv, gain_v = x_ref[...], gain_ref[...]
        dx, xhat = _rms_bwd(dh, xv, gain_v)
        dx_ref[...] = dyv + dx
        h_ref[...] = (xhat * gain_v).astype(BF16)
        _accumulate(dgain_ref, jnp.sum(dh * xhat, axis=0, keepdims=True), pl.program_id(0) == 0)

    row = pl.BlockSpec((tm, d), lambda i: (i, 0))
    wide = pl.BlockSpec((tm, f), lambda i: (i, 0))
    outs, exchanged = _hosted_call(
        body, exchange, name=f"ffn_bwd_{tag}", grid=(t // tm,),
        in_specs=[row, row, _resident((1, d)), wide, wide, _resident(wg.shape), _resident(wu.shape), _resident(wd.shape)],
        out_specs=[row, wide, wide, wide, row, pl.BlockSpec((1, d), lambda i: (0, 0))],
        out_shape=[jax.ShapeDtypeStruct((t, d), F32), jax.ShapeDtypeStruct((t, f), BF16), jax.ShapeDtypeStruct((t, f), BF16),
                   jax.ShapeDtypeStruct((t, f), BF16), jax.ShapeDtypeStruct((t, d), BF16), jax.ShapeDtypeStruct((1, d), F32)],
        scratch_shapes=[], semantics=("arbitrary",), args=(dy, x, gain, g, u, wg, wu, wd))
    return (*outs, exchanged)


def _tn_matmul(a, b, name, into=None, row_block=0, row_blocks=1):
    t, k = a.shape
    n = b.shape[1]
    tk = k // 2 if (k // 2) % LANES == 0 else k
    tt = min(2048, t)
    first = row_block * (k // tk)

    def body(a_ref, b_ref, *rest):
        o_ref, acc_ref = rest[-2:]
        prod = lax.dot_general(a_ref[...].astype(BF16), b_ref[...].astype(BF16), TN, preferred_element_type=F32)
        j = pl.program_id(1)

        @pl.when(j == 0)
        def _():
            acc_ref[...] = prod

        @pl.when(j > 0)
        def _():
            acc_ref[...] += prod

        @pl.when(j == pl.num_programs(1) - 1)
        def _():
            o_ref[...] = acc_ref[...].astype(BF16)

    return pl.pallas_call(
        body, name=name, grid=(k // tk, t // tt),
        in_specs=[pl.BlockSpec((tt, tk), lambda i, j: (j, i)), pl.BlockSpec((tt, n), lambda i, j: (j, 0))]
                 + ([ANY] if into is not None else []),
        out_specs=pl.BlockSpec((tk, n), lambda i, j: (first + i, 0)),
        out_shape=jax.ShapeDtypeStruct((row_blocks * k, n), BF16),
        scratch_shapes=[pltpu.VMEM((tk, n), F32)],
        input_output_aliases={2: 0} if into is not None else {},
        compiler_params=_params("parallel", "arbitrary"),
    )(a, b, *([into] if into is not None else []))


def _attn_out_bwd(dx, w, o, dils, seq, tag, lse=None, sink=None):
    t, d = dx.shape
    ts = _tile_rows(seq)
    bl = t // seq
    ng = len(dils)
    has_sink = sink is not None
    expand = _head_expand().T

    def body(*refs):
        refs = list(refs)
        dx_ref, w_ref, o_ref, e_ref = refs[:4]
        refs = refs[4:]
        lse_ref, sink_ref = (refs.pop(0), refs.pop(0)) if has_sink else (None, None)
        do_refs, dl_refs = refs[:ng], refs[ng:2 * ng]
        refs = refs[2 * ng:]
        dsink_ref = refs.pop(0) if has_sink else None
        dof_ref, dlf_ref = refs
        do = lax.dot_general(dx_ref[...].astype(BF16), w_ref[...], NT, preferred_element_type=F32)
        prod = do * o_ref[...].astype(F32)
        hi = prod.astype(BF16)
        lo = (prod - hi.astype(F32)).astype(BF16)
        e = e_ref[...]
        dl = jnp.dot(hi, e, preferred_element_type=F32) + jnp.dot(lo, e, preferred_element_type=F32)
        for g in range(ng):
            for r, part in enumerate(_split_rows(do, dof_ref, dils[g])):
                do_refs[g][r] = part.astype(BF16)
            for r, part in enumerate(_split_rows(dl, dlf_ref, dils[g])):
                dl_refs[g][r] = part
        if has_sink:
            part = -jnp.exp(sink_ref[...] - lse_ref[...]) * dl
            _accumulate(dsink_ref, jnp.sum(part, axis=0, keepdims=True), pl.program_id(0) == 0)

    row = pl.BlockSpec((ts, d), lambda i: (i, 0))
    narrow = pl.BlockSpec((ts, LANES), lambda i: (i, 0))
    args = [dx, w, o, expand]
    in_specs = [row, _resident(w.shape), pl.BlockSpec((ts, Q_W), lambda i: (i, 0)), _resident(expand.shape)]
    if has_sink:
        args += [lse, jnp.pad(sink.reshape(1, N_HEADS), ((0, 0), (0, LANES - N_HEADS)))]
        in_specs += [narrow, _resident((1, LANES))]
    out_specs = [_res_spec(seq, dl, Q_W) for dl in dils] + [_res_spec(seq, dl, LANES) for dl in dils]
    out_shape = ([jax.ShapeDtypeStruct(_res_shape(bl, seq, dl, Q_W), BF16) for dl in dils]
                 + [jax.ShapeDtypeStruct(_res_shape(bl, seq, dl, LANES), F32) for dl in dils])
    if has_sink:
        out_specs.append(pl.BlockSpec((1, LANES), lambda i: (0, 0)))
        out_shape.append(jax.ShapeDtypeStruct((1, LANES), F32))
    outs = pl.pallas_call(
        body, name=f"attn_out_bwd_{tag}", grid=(t // ts,), in_specs=in_specs, out_specs=out_specs, out_shape=out_shape,
        scratch_shapes=[_stage(ts, Q_W), _stage(ts, LANES)],
        compiler_params=_params("arbitrary" if has_sink else "parallel"),
    )(*args)
    return list(outs[:ng]), list(outs[ng:2 * ng]), (outs[2 * ng] if has_sink else None)


def _attn_bwd(qkv, do, lse, delta, cos, sin, w, tag, exchange=()):
    shape = qkv.shape
    dil = shape[1]
    rows_all = _seq_view(qkv)
    nseq, length, _ = rows_all.shape
    bq = min(QUERY_BLOCK, length)
    wk = _key_rows(bq, w, length)
    nb = length // bq

    def body(qkv_ref, do_ref, lse_ref, dl_ref, cos_ref, sin_ref, dp_ref, kk_ref, vv_ref, dk_ref, dv_ref):
        _pair_variants(qkv_ref, Q_W, kk_ref)
        _pair_variants(qkv_ref, Q_W + KV_W, vv_ref)
        dk_ref[...] = jnp.zeros_like(dk_ref)
        dv_ref[...] = jnp.zeros_like(dv_ref)
        band = _band(bq, wk)
        lo_q = lax.broadcasted_iota(jnp.int32, (bq, LANES), 1) < HEAD_DIM
        hi_q = jnp.logical_not(lo_q)

        def block(i, carry):
            q0, k0 = _window(i, bq, w, wk, length)
            valid = jnp.abs(band + (q0 - k0)) <= w
            rows, krows = pl.ds(q0, bq), pl.ds(k0, wk)
            c, sn = cos_ref[rows, :], -sin_ref[rows, :]
            lse_t, dl_t = lse_ref[rows, :], dl_ref[rows, :]
            for kv in range(N_KV):
                heads = [(kv * GRP + h, h % 2) for h in range(GRP)]
                cols = [slice((kv * 2 + j) * LANES, (kv * 2 + j + 1) * LANES) for j in range(GRP // 2)]
                qp = [qkv_ref[rows, cs] for cs in cols]
                dop = [do_ref[rows, cs] for cs in cols]
                k2 = jnp.concatenate([kk_ref[kv, 0, krows, :], kk_ref[kv, 1, krows, :]], axis=0)
                v2 = jnp.concatenate([vv_ref[kv, 0, krows, :], vv_ref[kv, 1, krows, :]], axis=0)
                sc2 = [lax.dot_general(q_, k2, NT, preferred_element_type=F32) for q_ in qp]
                dp2 = [lax.dot_general(d_, v2, NT, preferred_element_type=F32) for d_ in dop]
                sc = [s_[:, half * wk:(half + 1) * wk] for s_ in sc2 for half in range(2)]
                dp = [d_[:, half * wk:(half + 1) * wk] for d_ in dp2 for half in range(2)]
                p = [jnp.exp(jnp.where(valid, s_, NEG_INF) - _over_keys(lse_t[:, hd:hd + 1], wk))
                     for s_, (hd, _) in zip(sc, heads)]
                ds = [(p_ * (dp_ - _over_keys(dl_t[:, hd:hd + 1], wk))).astype(BF16) for p_, dp_, (hd, _) in zip(p, dp, heads)]
                pb = [p_.astype(BF16) for p_ in p]
                for j in range(GRP // 2):
                    dq = jnp.dot(jnp.concatenate([ds[2 * j], ds[2 * j + 1]], axis=1), k2, preferred_element_type=F32) * SCALE
                    dp_ref[rows, cols[j]] = _rope(dq, c, sn).astype(BF16)
                zero = jnp.zeros((bq, LANES), BF16)
                q4 = jnp.concatenate([jnp.where(lo_q if h % 2 == 0 else hi_q, qp[h // 2], zero) for h in range(GRP)], axis=0)
                do4 = jnp.concatenate([jnp.where(lo_q if h % 2 == 0 else hi_q, dop[h // 2], zero) for h in range(GRP)], axis=0)
                dk_ref[kv, krows, :] += lax.dot_general(jnp.concatenate(ds, axis=0), q4, TN, preferred_element_type=F32)
                dv_ref[kv, krows, :] += lax.dot_general(jnp.concatenate(pb, axis=0), do4, TN, preferred_element_type=F32)
            return carry

        lax.fori_loop(0, nb, block, 0)
        lo = lax.broadcasted_iota(jnp.int32, (length, LANES), 1) < HEAD_DIM
        c, sn = cos_ref[...], -sin_ref[...]
        for ch in range(KV_W // LANES):
            halves = []
            for acc_ref in (dk_ref, dv_ref):
                even, odd = acc_ref[2 * ch], acc_ref[2 * ch + 1]
                even = even + pltpu.roll(even, HEAD_DIM, 1)
                odd = odd + pltpu.roll(odd, HEAD_DIM, 1)
                halves.append(jnp.where(lo, even, odd))
            dp_ref[:, Q_W + ch * LANES:Q_W + (ch + 1) * LANES] = _rope(halves[0], c, sn).astype(BF16)
            dp_ref[:, Q_W + KV_W + ch * LANES:Q_W + KV_W + (ch + 1) * LANES] = halves[1].astype(BF16)

    mode = dict(pipeline_mode=pl.Buffered(1)) if dil == 1 else {}

    def seq_block(c):
        return pl.BlockSpec((None, length, c), lambda i: (i, 0, 0), **mode)

    table = pl.BlockSpec((None, length, LANES), lambda i: (i % dil, 0, 0), **mode)
    (out,), exchanged = _hosted_call(
        body, exchange, name=f"attn_bwd_{tag}", grid=(nseq,),
        in_specs=[seq_block(QKV_W), seq_block(Q_W), seq_block(LANES), seq_block(LANES), table, table],
        out_specs=[pl.BlockSpec((None, length, QKV_W), lambda i: (i, 0, 0))],
        out_shape=[jax.ShapeDtypeStruct((nseq, length, QKV_W), BF16)],
        scratch_shapes=[pltpu.VMEM((N_KV, 2, length, LANES), BF16), pltpu.VMEM((N_KV, 2, length, LANES), BF16),
                        pltpu.VMEM((N_KV, length, LANES), F32), pltpu.VMEM((N_KV, length, LANES), F32)],
        semantics=("parallel",), args=(rows_all, _seq_view(do), _seq_view(lse), _seq_view(delta), cos, sin))
    return out.reshape(shape), exchanged


def _qkv_bwd(dy, x, gain, w, dps, dils, seq, tag, exchange=()):
    t, d = x.shape
    ts = _tile_rows(seq)
    bl = t // seq
    ng = len(dps)

    def body(dy_ref, x_ref, gain_ref, w_ref, *refs):
        dp_refs, dx_ref = refs[:ng], refs[ng]
        h_refs = refs[ng + 1:2 * ng + 1]
        dgain_ref, stage_ref = refs[2 * ng + 1:]
        dh = None
        for gi in range(ng):
            dil = dils[gi]
            n = ts // dil
            dp = dp_refs[gi][0] if dil == 1 else jnp.concatenate([dp_refs[gi][r] for r in range(dil)], axis=0)
            part = jnp.dot(dp, w_ref[gi * QKV_W:(gi + 1) * QKV_W, :], preferred_element_type=F32)
            part = _merge_rows([part[r * n:(r + 1) * n] for r in range(dil)], stage_ref, dil)
            dh = part if dh is None else dh + part
        xv, gain_v = x_ref[...], gain_ref[...]
        dx, xhat = _rms_bwd(dh, xv, gain_v)
        dx_ref[...] = dy_ref[...] + dx
        h = xhat * gain_v
        for gi in range(ng):
            for r, part in enumerate(_split_rows(h, stage_ref, dils[gi])):
                h_refs[gi][r] = part.astype(BF16)
        _accumulate(dgain_ref, jnp.sum(dh * xhat, axis=0, keepdims=True), pl.program_id(0) == 0)

    row = pl.BlockSpec((ts, d), lambda i: (i, 0))
    outs, exchanged = _hosted_call(
        body, exchange, name=f"qkv_bwd_{tag}", grid=(t // ts,),
        in_specs=[row, row, _resident((1, d)), _resident(w.shape)] + [_res_spec(seq, dl, QKV_W) for dl in dils],
        out_specs=[row] + [_res_spec(seq, dl, d) for dl in dils] + [pl.BlockSpec((1, d), lambda i: (0, 0))],
        out_shape=[jax.ShapeDtypeStruct((t, d), F32)] + [jax.ShapeDtypeStruct(_res_shape(bl, seq, dl, d), BF16) for dl in dils]
                  + [jax.ShapeDtypeStruct((1, d), F32)],
        scratch_shapes=[_stage(ts, d)], semantics=("arbitrary",), args=(dy, x, gain, w, *dps))
    return outs[0], list(outs[1:1 + ng]), outs[1 + ng], exchanged


ANY = pl.BlockSpec(memory_space=pl.ANY)


def _place():
    x, y, c = lax.axis_index("x"), lax.axis_index("y"), lax.axis_index("c")
    return x, y, c


def _exchange_steps(srcs, dsts, gather, send_sems, recv_sems, local_sems):
    x, y, c = _place()
    me, sibling = (x, y, c), (x, y, 1 - c)
    chips = [(1 - x, y), (x, 1 - y), (1 - x, 1 - y)]
    mine = 4 * x + 2 * y + c

    def slot(a, device):
        px, py, pc = device
        return dsts[a].at[4 * px + 2 * py + pc]

    def passes(a, k, block, to, src=None):
        rows = slot(a, block)
        return pltpu.make_async_remote_copy(src_ref=rows if src is None else src, dst_ref=rows, send_sem=send_sems.at[a, k],
                                            recv_sem=recv_sems.at[a, k], device_id=to, device_id_type=MESH)

    def scatters(a, k):
        peer = mine ^ k
        return pltpu.make_async_remote_copy(
            src_ref=srcs[a].at[peer], dst_ref=dsts[a].at[mine], send_sem=send_sems.at[a, k - 1], recv_sem=recv_sems.at[a, k - 1],
            device_id=(peer // 4, (peer // 2) % 2, peer % 2), device_id_type=MESH)

    def local(a):
        return pltpu.make_async_copy(srcs[a] if gather[a] else srcs[a].at[mine], dsts[a].at[mine], local_sems.at[a])

    def first_copies(a):
        if not gather[a]:
            return [scatters(a, k) for k in range(1, N_DEV)]
        return [passes(a, 0, me, sibling, src=srcs[a])] + [passes(a, 1 + j, me, (*chip, c), src=srcs[a]) for j, chip in enumerate(chips)]

    def start():
        for a in range(len(srcs)):
            local(a).start()
            for cp in first_copies(a):
                cp.start()

    def forward():
        for a in range(len(srcs)):
            if gather[a]:
                for j, chip in enumerate(chips):
                    passes(a, 1 + j, (*chip, c), me).wait_recv()
                    passes(a, 4 + j, (*chip, c), sibling).start()

    def finish():
        for a in range(len(srcs)):
            if gather[a]:
                passes(a, 0, sibling, me).wait_recv()
                for j, chip in enumerate(chips):
                    passes(a, 4 + j, (*chip, 1 - c), me).wait_recv()
                    passes(a, 4 + j, (*chip, c), sibling).wait_send()
                for cp in first_copies(a):
                    cp.wait_send()
            else:
                for cp in first_copies(a):
                    cp.wait()
            local(a).wait()

    return start, forward, finish


def _exchange_scratch(n):
    return [pltpu.SemaphoreType.DMA((n, N_DEV - 1)), pltpu.SemaphoreType.DMA((n, N_DEV - 1)), pltpu.SemaphoreType.DMA((n,))]


def _exchanged_shapes(exchange):
    return [jax.ShapeDtypeStruct(((N_DEV,) + a.shape) if g else a.shape, a.dtype) for a, g in exchange]


def _hosted_call(body, exchange, *, name, grid, in_specs, out_specs, out_shape, scratch_shapes, semantics, args,
                 input_output_aliases=None):
    single = not isinstance(out_shape, (list, tuple))
    out_specs, out_shape = ([out_specs], [out_shape]) if single else (list(out_specs), list(out_shape))
    scratch, aliases = list(scratch_shapes), dict(input_output_aliases or {})
    if not exchange:
        outs = pl.pallas_call(body, name=name, grid=grid, in_specs=in_specs, out_specs=out_specs, out_shape=out_shape,
                              scratch_shapes=scratch, input_output_aliases=aliases, compiler_params=_params(*semantics))(*args)
        return list(outs), []
    n, n_in, n_out, n_scr = len(exchange), len(in_specs), len(out_specs), len(scratch)
    gather = [g for _, g in exchange]
    steps = math.prod(grid)

    def hosted(*refs):
        own_in, x_in = refs[:n_in], refs[n_in:n_in + n]
        own_out, x_out = refs[n_in + n:n_in + n + n_out], refs[n_in + n + n_out:n_in + 2 * n + n_out]
        own_scr, sems = refs[n_in + 2 * n + n_out:n_in + 2 * n + n_out + n_scr], refs[-3:]
        step = pl.program_id(0)
        for axis in range(1, len(grid)):
            step = step * grid[axis] + pl.program_id(axis)
        start, forward, finish = _exchange_steps(x_in, x_out, gather, *sems)
        pl.when(step == 0)(start)
        body(*own_in, *own_out, *own_scr)
        pl.when(step == steps // 2)(forward)
        pl.when(step == steps - 1)(finish)

    outs = pl.pallas_call(
        hosted, name=name, grid=grid, in_specs=list(in_specs) + [ANY] * n, out_specs=out_specs + [ANY] * n,
        out_shape=out_shape + _exchanged_shapes(exchange), scratch_shapes=scratch + _exchange_scratch(n),
        input_output_aliases=aliases, compiler_params=_params(*["arbitrary"] * len(grid)),
    )(*args, *[a for a, _ in exchange])
    return list(outs[:n_out]), list(outs[n_out:])


def _exchange_now(exchange, name):
    n = len(exchange)
    gather = [g for _, g in exchange]

    def body(*refs):
        for step in _exchange_steps(refs[:n], refs[n:2 * n], gather, *refs[2 * n:]):
            step()

    return pl.pallas_call(
        body, name=name, in_specs=[ANY] * n, out_specs=[ANY] * n, out_shape=_exchanged_shapes(exchange),
        scratch_shapes=_exchange_scratch(n),
    )(*[a for a, _ in exchange])


def _all_reduce_small(v):
    def body(v_ref, o_ref, recv_ref, send_sems, recv_sems):
        x, y, c = _place()
        me = 4 * x + 2 * y + c
        copies = []
        for k in range(1, N_DEV):
            peer = me ^ k
            copies.append(pltpu.make_async_remote_copy(
                src_ref=v_ref, dst_ref=recv_ref.at[k], send_sem=send_sems.at[k - 1], recv_sem=recv_sems.at[k - 1],
                device_id=(peer // 4, (peer // 2) % 2, peer % 2), device_id_type=MESH))
        for cp in copies:
            cp.start()
        recv_ref[0] = v_ref[...]
        for cp in copies:
            cp.wait()
        acc = recv_ref[me]
        for src in range(1, N_DEV):
            acc = acc + recv_ref[me ^ src]
        o_ref[...] = acc

    vm = pl.BlockSpec(memory_space=pltpu.VMEM)
    return pl.pallas_call(
        body, name="all_reduce_small", in_specs=[vm], out_specs=vm, out_shape=jax.ShapeDtypeStruct(v.shape, F32),
        scratch_shapes=[pltpu.VMEM((N_DEV,) + v.shape, F32), pltpu.SemaphoreType.DMA((N_DEV - 1,)),
                        pltpu.SemaphoreType.DMA((N_DEV - 1,))],
    )(v)


def _adamw_math(w, g, m, v):
    m = ADAM_B1 * m + (1.0 - ADAM_B1) * g
    v = ADAM_B2 * v + (1.0 - ADAM_B2) * (g * g)
    m_hat = m / (1.0 - ADAM_B1 ** ADAM_STEP)
    v_hat = v / (1.0 - ADAM_B2 ** ADAM_STEP)
    delta = -ADAM_LR * (m_hat / (jnp.sqrt(v_hat) + ADAM_EPS) + ADAM_WD * w)
    return delta, m, v


def _adamw(parts, w, m, v, name, layer=None, into=None):
    r, c = w.shape[-2:]
    tr = r // 2 if r % 16 == 0 and r >= 256 else r
    n = len(parts)

    def body(*refs):
        w_ref, m_ref, v_ref = refs[n:n + 3]
        g_ref, d_ref, nm_ref, nv_ref = refs[-4:]
        g = refs[0][...].astype(F32)
        for p_ref in refs[1:n]:
            g = g + p_ref[...].astype(F32)
        g_ref[...] = g
        d_ref[...], nm_ref[...], nv_ref[...] = _adamw_math(w_ref[...], g, m_ref[...], v_ref[...])

    def slab(slot):
        return pl.BlockSpec((None, tr, c), lambda i: (slot, i, 0))

    tile = pl.BlockSpec((tr, c), lambda i: (i, 0)) if layer is None else slab(layer)
    arrays, in_specs = [], []
    for p in parts:
        if isinstance(p, tuple):
            arrays.append(p[0])
            in_specs.append(slab(p[1]))
        else:
            arrays.append(p)
            in_specs.append(tile)
    kept = list(into) if into is not None else []
    return pl.pallas_call(
        body, name=name, grid=(r // tr,), in_specs=in_specs + [tile] * 3 + [ANY] * len(kept), out_specs=[tile] * 4,
        out_shape=[jax.ShapeDtypeStruct(w.shape, F32)] * 4,
        input_output_aliases={n + 3 + k: k for k in range(len(kept))}, compiler_params=_params("parallel"),
    )(*arrays, w, m, v, *kept)


def _rows(g):
    return g.reshape(-1, g.shape[-1])


def _row_blocks(dw):
    k, n = dw.shape
    return dw.reshape(N_DEV, k // N_DEV, n)


def _pack_rows(rows, width):
    out = None
    for i, r in enumerate(rows):
        r = r.reshape(1, -1).astype(F32)
        r = jnp.pad(r, ((i, 8 - 1 - i), (0, width - r.shape[1])))
        out = r if out is None else out + r
    return out


def _mixer_fwd(x, gain, w_in, w_out, cos, sin, seq, groups, tag, sink=None, exchanges=None):
    qkvs, os, lses, got = [], [], [], {}
    for gi, (dil, w) in enumerate(groups):
        qkv, got["proj", gi] = _qkv_proj(x, gain, w_in, _tables_tiled(cos, seq, dil), _tables_tiled(sin, seq, dil), seq, dil, gi,
                                         f"{tag}{gi}", exchange=(exchanges or {}).get(("proj", gi), ()))
        o, lse, got[gi] = _attn_fwd(qkv, w, f"{tag}{gi}", sink=sink, exchange=(exchanges or {}).get(gi, ()))
        qkvs.append(qkv)
        os.append(o)
        lses.append(lse)
    y, o, lses = _out_proj(x, os, lses, [dl for dl, _ in groups], w_out, seq, tag)
    return y, (qkvs, o, lses), got


def _mixer_bwd(dy, x_in, gain, w_in, w_out, saved, cos, sin, seq, groups, tag, sink=None, exchanges=None, scatter_dw_out=False):
    qkvs, o, lses = saved
    t, d = x_in.shape
    dils = [dl for dl, _ in groups]
    lse_tokens = lses[0].reshape(t, LANES) if sink is not None else None
    dos, dls, dsink = _attn_out_bwd(dy, w_out, o, dils, seq, tag, lse=lse_tokens, sink=sink)
    dw_out = _tn_matmul(o, dy, f"dw_out_{tag}")
    exchanges = {gi: list(e) for gi, e in (exchanges or {}).items()}
    if scatter_dw_out:
        exchanges[0] = exchanges.get(0, []) + _to_send([dw_out])
    dps, got = [], {}
    for gi, (dil, w) in enumerate(groups):
        dp, got[gi] = _attn_bwd(qkvs[gi], dos[gi], lses[gi], dls[gi], _tables_by_residue(cos, seq, dil),
                                _tables_by_residue(sin, seq, dil), w, f"{tag}{gi}", exchange=exchanges.get(gi, ()))
        dps.append(dp)
    if scatter_dw_out:
        dw_out = got[0].pop()
    dx, hs, dgain, _ = _qkv_bwd(dy, x_in, gain, w_in, dps, dils, seq, tag)
    dw_in = None
    for gi in range(len(groups)):
        dw_in = _tn_matmul(dps[gi].reshape(t, QKV_W), hs[gi].reshape(t, d), f"dw_in_{tag}{gi}", into=dw_in, row_block=gi,
                           row_blocks=len(groups))
    return dx, dw_in, dw_out, dgain, dsink, got


def _ffn_layer_bwd(dy, x_in, gain, g, u, wg, wu, wd, tag, exchange=()):
    dx, dg, du, act, h, dgain, got = _ffn_bwd(dy, x_in, gain, g, u, wg, wu, wd, tag, exchange=exchange)
    dwd = _tn_matmul(act, dy, f"dw_down_{tag}")
    dwg = _tn_matmul(dg, h, f"dw_gate_{tag}")
    dwu = _tn_matmul(du, h, f"dw_up_{tag}")
    return dx, dwg, dwu, dwd, dgain, got


def _to_send(dws):
    return [(_row_blocks(g), False) for g in dws]


def kernel(x, a_w_in, a_sink, a_w_out, b_w_in, b_w_out, norm_mix, norm_ffn, w_gate, w_up, w_down, final_norm, loss_target, m_a_w_in, m_a_sink, m_a_w_out, m_b_w_in, m_b_w_out, m_norm_mix, m_norm_ffn, m_w_gate, m_w_up, m_w_down, m_final_norm, v_a_w_in, v_a_sink, v_a_w_out, v_b_w_in, v_b_w_out, v_norm_mix, v_norm_ffn, v_w_gate, v_w_up, v_w_down, v_final_norm):
    bl, seq, d = x.shape
    t = bl * seq
    xf = x.reshape(t, d)
    target = loss_target.reshape(t, d)
    cos, sin = _rope_tables(seq)
    groups_a = [(1, ATTN_HALF_WINDOW)]
    groups_b = [(dil, window // 2 // dil) for window, dil in DILATED_GROUPS]

    def flip(w_):
        return jnp.swapaxes(w_, -1, -2)

    a_w_in, m_a_w_in, v_a_w_in, b_w_in, m_b_w_in, v_b_w_in = map(flip, (a_w_in, m_a_w_in, v_a_w_in, b_w_in, m_b_w_in, v_b_w_in))
    w_gate, m_w_gate, v_w_gate, w_up, m_w_up, v_w_up = map(flip, (w_gate, m_w_gate, v_w_gate, w_up, m_w_up, v_w_up))

    def shard(w_, layer):
        return (w_[layer].astype(BF16), True)

    wa_in, wa_out = map(_rows, _exchange_now([shard(a_w_in, 0), shard(a_w_out, 0)], "gather_first"))

    x1_0, saved_a, got = _mixer_fwd(xf, norm_mix[0:1], wa_in, wa_out, cos, sin, seq, groups_a, "a", sink=a_sink[0],
                                    exchanges={("proj", 0): [shard(w_down, 0)], 0: [shard(w_gate, 0), shard(w_up, 0)]})
    wg0, wu0, wd0 = map(_rows, got[0] + got["proj", 0])
    x2_0, g0, u0, got = _ffn_fwd(x1_0, norm_ffn[0:1], wg0, wu0, wd0, "0", exchange=[shard(b_w_in, 0), shard(b_w_out, 0)])
    wb_in, wb_out = map(_rows, got)
    x1_1, saved_b, got = _mixer_fwd(x2_0, norm_mix[1:2], wb_in, wb_out, cos, sin, seq, groups_b, "b",
                                    exchanges={0: [shard(w_gate, 1)], 1: [shard(w_up, 1)], 2: [shard(w_down, 1)]})
    wg1, wu1, wd1 = map(_rows, got[0] + got[1] + got[2])
    x2_1, g1, u1, _ = _ffn_fwd(x1_1, norm_ffn[1:2], wg1, wu1, wd1, "1")

    dy, loss_part, d_final = _loss_bwd(x2_1, final_norm.reshape(1, d), target)
    dy, dwg1, dwu1, dwd1, d_nf1, _ = _ffn_layer_bwd(dy, x1_1, norm_ffn[1:2], g1, u1, wg1, wu1, wd1, "1")
    dy, dwb_in, dwb_out, d_nm1, _, got = _mixer_bwd(
        dy, x2_0, norm_mix[1:2], wb_in, wb_out, saved_b, cos, sin, seq, groups_b, "b",
        exchanges={0: _to_send([dwg1, dwd1]), 1: _to_send([dwu1])})
    (r_g1, r_d1), (r_u1,) = got[0], got[1]
    dy, dwg0, dwu0, dwd0, d_nf0, (r_b_in, r_b_out) = _ffn_layer_bwd(
        dy, x1_0, norm_ffn[0:1], g0, u0, wg0, wu0, wd0, "0", exchange=_to_send([dwb_in, dwb_out]))
    dy, dwa_in, r_a_out, d_nm0, d_sink, got = _mixer_bwd(
        dy, xf, norm_mix[0:1], wa_in, wa_out, saved_a, cos, sin, seq, groups_a, "a", sink=a_sink[0],
        exchanges={0: _to_send([dwg0, dwu0, dwd0])}, scatter_dw_out=True)
    r_g0, r_u0, r_d0 = got[0]
    (r_a_in,) = _exchange_now(_to_send([dwa_in]), "scatter_last")
    grad_x = dy.reshape(bl, seq, d)

    def update(received, w_, m_, v_, name):
        out = None
        for layer in reversed(range(len(received))):
            out = _adamw([(received[layer], src) for src in range(N_DEV)], w_, m_, v_, f"adamw_{name}{layer}", layer=layer, into=out)
        return out

    u_a_in = update([r_a_in], a_w_in, m_a_w_in, v_a_w_in, "a_in")
    u_a_out = update([r_a_out], a_w_out, m_a_w_out, v_a_w_out, "a_out")
    u_b_in = update([r_b_in], b_w_in, m_b_w_in, v_b_w_in, "b_in")
    u_b_out = update([r_b_out], b_w_out, m_b_w_out, v_b_w_out, "b_out")
    u_gate = update([r_g0, r_g1], w_gate, m_w_gate, v_w_gate, "gate")
    u_up = update([r_u0, r_u1], w_up, m_w_up, v_w_up, "up")
    u_down = update([r_d0, r_d1], w_down, m_w_down, v_w_down, "down")

    small = _pack_rows([d_nm0, d_nm1, d_nf0, d_nf1, d_final, d_sink, loss_part], d)
    total = _all_reduce_small(small)
    small_w = _pack_rows([norm_mix[0], norm_mix[1], norm_ffn[0], norm_ffn[1], final_norm, a_sink], d)
    small_m = _pack_rows([m_norm_mix[0], m_norm_mix[1], m_norm_ffn[0], m_norm_ffn[1], m_final_norm, m_a_sink], d)
    small_v = _pack_rows([v_norm_mix[0], v_norm_mix[1], v_norm_ffn[0], v_norm_ffn[1], v_final_norm, v_a_sink], d)
    u_small = _adamw([total], small_w, small_m, small_v, "adamw_small")
    loss = total[6, 0]

    outs = []
    for k in range(4):
        sm = u_small[k]
        outs += [flip(u_a_in[k]), sm[5:6, :N_HEADS], u_a_out[k], flip(u_b_in[k]), u_b_out[k], sm[0:2], sm[2:4],
                 flip(u_gate[k]), flip(u_up[k]), u_down[k], sm[4]]
    return (loss, grad_x, *outs)
```

```python
import functools
import math

import jax
import jax.numpy as jnp
from jax import lax
from jax.experimental import pallas as pl
from jax.experimental.pallas import tpu as pltpu

F32 = jnp.float32
BF16 = jnp.bfloat16

HEAD_DIM = 64
N_HEADS = 16
N_KV = 4
GRP = N_HEADS // N_KV
Q_W = N_HEADS * HEAD_DIM
KV_W = N_KV * HEAD_DIM
QKV_W = Q_W + 2 * KV_W
ATTN_HALF_WINDOW = 128
DILATED_GROUPS = ((128, 1), (512, 4), (2048, 16))
ROPE_THETA = 10000.0
RMS_EPS = 1e-6
NEG_INF = -1e30
SCALE = 1.0 / math.sqrt(HEAD_DIM)

ADAM_LR = 0.001
ADAM_B1 = 0.9
ADAM_B2 = 0.999
ADAM_EPS = 1e-08
ADAM_WD = 0.01
ADAM_STEP = 10

LANES = 128
VMEM_LIMIT = 56 * 1024 * 1024
QUERY_BLOCK = 128
N_DEV = 8
MESH = pl.DeviceIdType.MESH

NT = (((1,), (1,)), ((), ()))
TN = (((0,), (0,)), ((), ()))


def _params(*sem):
    return pltpu.CompilerParams(dimension_semantics=tuple(sem) if sem else None, vmem_limit_bytes=VMEM_LIMIT)


def _resident(shape):
    return pl.BlockSpec(shape, lambda *_: (0,) * len(shape), pipeline_mode=pl.Buffered(1))


def _rope_tables(seq):
    inv_freq = 1.0 / (ROPE_THETA ** (jnp.arange(0, HEAD_DIM, 2, dtype=F32) / HEAD_DIM))
    ang = jnp.arange(seq, dtype=F32)[:, None] * inv_freq[None, :]
    cos, sin = jnp.cos(ang), jnp.sin(ang)
    return jnp.tile(cos, (1, 4)), jnp.concatenate([-sin, sin, -sin, sin], axis=1)


def _rope(t, cos, sin_signed):
    lane = lax.broadcasted_iota(jnp.int32, t.shape, 1)
    first = (lane & (HEAD_DIM // 2)) == 0
    swapped = jnp.where(first, pltpu.roll(t, LANES - HEAD_DIM // 2, 1), pltpu.roll(t, HEAD_DIM // 2, 1))
    return t * cos + swapped * sin_signed


def _rms(x):
    return lax.rsqrt(jnp.mean(x * x, axis=-1, keepdims=True) + RMS_EPS)


def _rms_bwd(dh, x, gain):
    r = _rms(x)
    xhat = x * r
    dxh = dh * gain
    dx = r * (dxh - xhat * jnp.mean(dxh * xhat, axis=-1, keepdims=True))
    return dx, xhat


def _accumulate(ref, value, first):
    @pl.when(first)
    def _():
        ref[...] = jnp.zeros_like(ref)

    ref[...] += value


def _tile_rows(seq):
    return min(512, seq)


def _res_shape(bl, seq, dil, c):
    ts = _tile_rows(seq)
    return (bl, dil, seq // ts, ts // dil, c)


def _res_spec(seq, dil, c):
    ts = _tile_rows(seq)
    per_seq = seq // ts
    return pl.BlockSpec((None, dil, None, ts // dil, c), lambda i: (i // per_seq, 0, i % per_seq, 0, 0))


def _seq_view(a):
    bl, dil, tiles, n, c = a.shape
    return a.reshape(bl * dil, tiles * n, c)


def _stage(ts, c):
    return pltpu.VMEM((c // LANES, ts, LANES), F32)


def _split_rows(val, stage_ref, dil):
    if dil == 1:
        return [val]
    ts, c = val.shape
    n, nc = ts // dil, c // LANES
    for k in range(nc):
        stage_ref[k] = val[:, k * LANES:(k + 1) * LANES]
    return [jnp.concatenate([stage_ref[k, pl.ds(r, n, stride=dil), :] for k in range(nc)], axis=1) for r in range(dil)]


def _merge_rows(parts, stage_ref, dil):
    if dil == 1:
        return parts[0]
    n, c = parts[0].shape
    nc = c // LANES
    for r, part in enumerate(parts):
        for k in range(nc):
            stage_ref[k, pl.ds(r, n, stride=dil), :] = part[:, k * LANES:(k + 1) * LANES]
    return jnp.concatenate([stage_ref[k] for k in range(nc)], axis=1)


def _tables_tiled(table, seq, dil):
    ts = _tile_rows(seq)
    return table.reshape(seq // ts, ts // dil, dil, LANES).transpose(0, 2, 1, 3).reshape(seq, LANES)


def _tables_by_residue(table, seq, dil):
    return table.reshape(seq // dil, dil, LANES).transpose(1, 0, 2)


def _qkv_proj(x, gain, w, cos, sin, seq, dil, group, tag, exchange=()):
    t, d = x.shape
    ts = _tile_rows(seq)
    n = ts // dil
    per_seq = seq // ts

    def body(x_ref, g_ref, w_ref, cos_ref, sin_ref, o_ref, stage_ref):
        xv = jnp.concatenate(_split_rows(x_ref[...], stage_ref, dil), axis=0)
        h = (xv * _rms(xv) * g_ref[...]).astype(BF16)
        acc = lax.dot_general(h, w_ref[...], NT, preferred_element_type=F32)
        c, s = cos_ref[...], sin_ref[...]
        for j in range(QKV_W // LANES):
            cols = slice(j * LANES, (j + 1) * LANES)
            val = acc[:, cols]
            if j < (Q_W + KV_W) // LANES:
                val = _rope(val, c, s)
            if j < Q_W // LANES:
                val = val * SCALE
            val = val.astype(BF16)
            for r in range(dil):
                o_ref[r, :, cols] = val[r * n:(r + 1) * n]

    table = pl.BlockSpec((ts, LANES), lambda i: (i % per_seq, 0))
    (qkv,), exchanged = _hosted_call(
        body, exchange, name=f"qkv_proj_{tag}", grid=(t // ts,),
        in_specs=[pl.BlockSpec((ts, d), lambda i: (i, 0)), pl.BlockSpec((1, d), lambda i: (0, 0)),
                  pl.BlockSpec((QKV_W, d), lambda i: (group, 0)), table, table],
        out_specs=[_res_spec(seq, dil, QKV_W)],
        out_shape=[jax.ShapeDtypeStruct(_res_shape(t // seq, seq, dil, QKV_W), BF16)],
        scratch_shapes=[_stage(ts, d)], semantics=("parallel",), args=(x, gain, w, cos, sin))
    return qkv, exchanged


def _band(bq, wk):
    return lax.broadcasted_iota(jnp.int32, (bq, wk), 0) - lax.broadcasted_iota(jnp.int32, (bq, wk), 1)


def _swap_halves(src_ref, base, dst_ref):
    for c in range(KV_W // LANES):
        dst_ref[c] = pltpu.roll(src_ref[:, base + c * LANES:base + (c + 1) * LANES], HEAD_DIM, 1)


def _pair_operand(src_ref, swapped_ref, base, kv, rows):
    c = kv // 2
    chunk, swapped = src_ref[rows, base + c * LANES:base + (c + 1) * LANES], swapped_ref[c, rows, :]
    lo = lax.broadcasted_iota(jnp.int32, chunk.shape, 1) < HEAD_DIM
    zero = jnp.zeros_like(chunk)
    if kv % 2 == 0:
        return jnp.concatenate([jnp.where(lo, chunk, zero), jnp.where(lo, zero, swapped)], axis=0)
    return jnp.concatenate([jnp.where(lo, swapped, zero), jnp.where(lo, zero, chunk)], axis=0)


def _over_keys(col, wk):
    if wk % LANES:
        return jnp.broadcast_to(col, (col.shape[0], wk))
    wide = jnp.broadcast_to(col, (col.shape[0], LANES))
    return wide if wk == LANES else jnp.concatenate([wide] * (wk // LANES), axis=1)


def _key_rows(bq, w, length):
    return min(bq + 2 * w, length)


def _window(i, bq, w, wk, length):
    q0 = pl.multiple_of(i * bq, bq)
    k0 = pl.multiple_of(jnp.clip(q0 - w, 0, length - wk), min(w, bq))
    return q0, k0


def _attn_fwd(qkv, w, tag, sink=None, exchange=()):
    shape = qkv.shape
    rows_all = _seq_view(qkv)
    nseq, length, _ = rows_all.shape
    bq = min(QUERY_BLOCK, length)
    wk = _key_rows(bq, w, length)
    nb = length // bq
    has_sink = sink is not None

    def body(*refs):
        qkv_ref = refs[0]
        sink_ref = refs[1] if has_sink else None
        o_ref, lse_ref, kk_ref, vv_ref = refs[-4:]
        _swap_halves(qkv_ref, Q_W, kk_ref)
        _swap_halves(qkv_ref, Q_W + KV_W, vv_ref)
        band = _band(bq, wk)
        lane = lax.broadcasted_iota(jnp.int32, (bq, LANES), 1)
        lo = lane < HEAD_DIM

        def block(i, carry):
            q0, k0 = _window(i, bq, w, wk, length)
            valid = jnp.abs(band + (q0 - k0)) <= w
            rows, krows = pl.ds(q0, bq), pl.ds(k0, wk)
            lse_tile = jnp.zeros((bq, LANES), F32)
            for kv in range(N_KV):
                heads = [(kv * GRP + h, h % 2) for h in range(GRP)]
                qp = [qkv_ref[rows, (kv * 2 + j) * LANES:(kv * 2 + j + 1) * LANES] for j in range(GRP // 2)]
                k2 = _pair_operand(qkv_ref, kk_ref, Q_W, kv, krows)
                v2 = _pair_operand(qkv_ref, vv_ref, Q_W + KV_W, kv, krows)
                sc2 = [lax.dot_general(q_, k2, NT, preferred_element_type=F32) for q_ in qp]
                sc = [jnp.where(valid, s_[:, half * wk:(half + 1) * wk], NEG_INF) for s_ in sc2 for half in range(2)]
                m = [jnp.max(s_, axis=-1, keepdims=True) for s_ in sc]
                if has_sink:
                    m = [jnp.maximum(m_, sink_ref[hd]) for m_, (hd, _) in zip(m, heads)]
                mb = [jnp.broadcast_to(m_, (bq, LANES)) for m_ in m]
                p = [jnp.exp(s_ - _over_keys(m_, wk)) for s_, m_ in zip(sc, m)]
                den = [jnp.sum(p_, axis=-1, keepdims=True) for p_ in p]
                if has_sink:
                    den = [d_ + jnp.exp(sink_ref[hd] - m_) for d_, m_, (hd, _) in zip(den, m, heads)]
                inv = [jnp.broadcast_to(1.0 / d_, (bq, LANES)) for d_ in den]
                pb = [p_.astype(BF16) for p_ in p]
                for j in range(GRP // 2):
                    o = jnp.dot(jnp.concatenate([pb[2 * j], pb[2 * j + 1]], axis=1), v2, preferred_element_type=F32)
                    o = o * jnp.where(lo, inv[2 * j], inv[2 * j + 1])
                    o_ref[rows, (kv * 2 + j) * LANES:(kv * 2 + j + 1) * LANES] = o.astype(BF16)
                for h, (hd, _) in enumerate(heads):
                    lse_tile = jnp.where(lane == hd, mb[h] - jnp.log(inv[h]), lse_tile)
            lse_ref[rows, :] = lse_tile
            return carry

        lax.fori_loop(0, nb, block, 0)

    args = [rows_all]
    in_specs = [pl.BlockSpec((None, length, QKV_W), lambda i: (i, 0, 0))]
    if has_sink:
        args.append(sink)
        in_specs.append(pl.BlockSpec(memory_space=pltpu.SMEM))
    (o, lse), exchanged = _hosted_call(
        body, exchange, name=f"attn_fwd_{tag}", grid=(nseq,), in_specs=in_specs,
        out_specs=[pl.BlockSpec((None, length, Q_W), lambda i: (i, 0, 0)), pl.BlockSpec((None, length, LANES), lambda i: (i, 0, 0))],
        out_shape=[jax.ShapeDtypeStruct((nseq, length, Q_W), BF16), jax.ShapeDtypeStruct((nseq, length, LANES), F32)],
        scratch_shapes=[pltpu.VMEM((KV_W // LANES, length, LANES), BF16), pltpu.VMEM((KV_W // LANES, length, LANES), BF16)],
        semantics=("parallel",), args=args)
    return o.reshape(shape[:-1] + (Q_W,)), lse.reshape(shape[:-1] + (LANES,)), exchanged


def _head_expand():
    return (jnp.arange(LANES)[:, None] == jnp.arange(Q_W)[None, :] // HEAD_DIM).astype(BF16)


def _out_proj(x, os, lses, dils, w, seq, tag):
    t, d = x.shape
    ts = _tile_rows(seq)
    ng = len(os)
    bl = t // seq
    if ng == 1:
        def body1(x_ref, o_ref, w_ref, y_ref):
            y_ref[...] = x_ref[...] + jnp.dot(o_ref[...], w_ref[...], preferred_element_type=F32)

        row = pl.BlockSpec((ts, d), lambda i: (i, 0))
        o = os[0].reshape(t, Q_W)
        y = pl.pallas_call(
            body1, name=f"out_proj_{tag}", grid=(t // ts,), in_specs=[row, row, _resident(w.shape)], out_specs=row,
            out_shape=jax.ShapeDtypeStruct((t, d), F32), compiler_params=_params("parallel"),
        )(x, o, w)
        return y, o, [lses[0]]

    def body(*refs):
        x_ref, w_ref, e_ref = refs[:3]
        o_refs, l_refs = refs[3:3 + ng], refs[3 + ng:3 + 2 * ng]
        y_ref, om_ref = refs[3 + 2 * ng:5 + 2 * ng]
        lt_refs = refs[5 + 2 * ng:5 + 3 * ng]
        wide_ref, narrow_ref = refs[5 + 3 * ng:]
        ls = [_merge_rows([l_refs[g][r] for r in range(dils[g])], narrow_ref, dils[g]) for g in range(ng)]
        mx = functools.reduce(jnp.maximum, ls)
        tot = mx + jnp.log(functools.reduce(lambda a, b: a + b, [jnp.exp(l_ - mx) for l_ in ls]))
        e = e_ref[...]
        o = None
        for g in range(ng):
            wt = jnp.exp(ls[g] - tot)
            hi = wt.astype(BF16)
            lo = (wt - hi.astype(F32)).astype(BF16)
            wide = jnp.dot(hi, e, preferred_element_type=F32) + jnp.dot(lo, e, preferred_element_type=F32)
            term = wide * _merge_rows([o_refs[g][r].astype(F32) for r in range(dils[g])], wide_ref, dils[g])
            o = term if o is None else o + term
        ob = o.astype(BF16)
        om_ref[...] = ob
        y_ref[...] = x_ref[...] + jnp.dot(ob, w_ref[...], preferred_element_type=F32)
        for g in range(ng):
            for r, part in enumerate(_split_rows(tot, narrow_ref, dils[g])):
                lt_refs[g][r] = part

    row = pl.BlockSpec((ts, d), lambda i: (i, 0))
    e = _head_expand()
    outs = pl.pallas_call(
        body, name=f"out_proj_{tag}", grid=(t // ts,),
        in_specs=[row, _resident(w.shape), _resident(e.shape)] + [_res_spec(seq, dl, Q_W) for dl in dils]
                 + [_res_spec(seq, dl, LANES) for dl in dils],
        out_specs=[row, pl.BlockSpec((ts, Q_W), lambda i: (i, 0))] + [_res_spec(seq, dl, LANES) for dl in dils],
        out_shape=[jax.ShapeDtypeStruct((t, d), F32), jax.ShapeDtypeStruct((t, Q_W), BF16)]
                  + [jax.ShapeDtypeStruct(_res_shape(bl, seq, dl, LANES), F32) for dl in dils],
        scratch_shapes=[_stage(ts, Q_W), _stage(ts, LANES)],
        compiler_params=_params("parallel"),
    )(x, w, e, *os, *lses)
    return outs[0], outs[1], list(outs[2:])


def _sigmoid(g):
    return 1.0 / (1.0 + jnp.exp(-g))


def _ffn_fwd(x, gain, wg, wu, wd, tag, exchange=()):
    t, d = x.shape
    f = wd.shape[0]
    tm = min(256, t)

    def body(x_ref, gain_ref, wg_ref, wu_ref, wd_ref, y_ref, g_ref, u_ref):
        xv = x_ref[...]
        h = (xv * _rms(xv) * gain_ref[...]).astype(BF16)
        g = lax.dot_general(h, wg_ref[...], NT, preferred_element_type=F32)
        u = lax.dot_general(h, wu_ref[...], NT, preferred_element_type=F32)
        g_ref[...] = g.astype(BF16)
        u_ref[...] = u.astype(BF16)
        a = (g * _sigmoid(g) * u).astype(BF16)
        y_ref[...] = xv + jnp.dot(a, wd_ref[...], preferred_element_type=F32)

    row = pl.BlockSpec((tm, d), lambda i: (i, 0))
    wide = pl.BlockSpec((tm, f), lambda i: (i, 0))
    outs, exchanged = _hosted_call(
        body, exchange, name=f"ffn_fwd_{tag}", grid=(t // tm,),
        in_specs=[row, _resident((1, d)), _resident(wg.shape), _resident(wu.shape), _resident(wd.shape)],
        out_specs=[row, wide, wide],
        out_shape=[jax.ShapeDtypeStruct((t, d), F32), jax.ShapeDtypeStruct((t, f), BF16), jax.ShapeDtypeStruct((t, f), BF16)],
        scratch_shapes=[], semantics=("parallel",), args=(x, gain, wg, wu, wd))
    return (*outs, exchanged)


def _loss_bwd(x, gain, target):
    t, d = x.shape
    tm = min(512, t)

    def body(x_ref, gain_ref, t_ref, dx_ref, loss_ref, dgain_ref):
        xv, gain_v = x_ref[...], gain_ref[...]
        xhat = xv * _rms(xv)
        err = xhat * gain_v - t_ref[...]
        dy = err * (1.0 / d)
        dx, _ = _rms_bwd(dy, xv, gain_v)
        dx_ref[...] = dx
        first = pl.program_id(0) == 0
        part = 0.5 * jnp.sum(jnp.mean(err * err, axis=-1, keepdims=True), axis=0, keepdims=True)
        _accumulate(loss_ref, jnp.broadcast_to(part, loss_ref.shape), first)
        _accumulate(dgain_ref, jnp.sum(dy * xhat, axis=0, keepdims=True), first)

    row = pl.BlockSpec((tm, d), lambda i: (i, 0))
    return pl.pallas_call(
        body, name="loss_bwd", grid=(t // tm,), in_specs=[row, _resident((1, d)), row],
        out_specs=[row, pl.BlockSpec((1, LANES), lambda i: (0, 0)), pl.BlockSpec((1, d), lambda i: (0, 0))],
        out_shape=[jax.ShapeDtypeStruct((t, d), F32), jax.ShapeDtypeStruct((1, LANES), F32), jax.ShapeDtypeStruct((1, d), F32)],
        compiler_params=_params("arbitrary"),
    )(x, gain, target)


def _ffn_bwd(dy, x, gain, g, u, wg, wu, wd, tag, exchange=()):
    t, d = x.shape
    f = wd.shape[0]
    tm = min(256, t)

    def body(dy_ref, x_ref, gain_ref, g_ref, u_ref, wg_ref, wu_ref, wd_ref, dx_ref, dg_ref, du_ref, a_ref, h_ref, dgain_ref):
        dyv = dy_ref[...]
        da = lax.dot_general(dyv.astype(BF16), wd_ref[...], NT, preferred_element_type=F32)
        gv, uv = g_ref[...].astype(F32), u_ref[...].astype(F32)
        sg = _sigmoid(gv)
        act = gv * sg
        a_ref[...] = (act * uv).astype(BF16)
        du = (da * act).astype(BF16)
        dg = (da * uv * (sg * (1.0 + gv * (1.0 - sg)))).astype(BF16)
        du_ref[...] = du
        dg_ref[...] = dg
        dh = jnp.dot(dg, wg_ref[...], preferred_element_type=F32) + jnp.dot(du, wu_ref[...], preferred_element_type=F32)
        xv, gain_v = x_ref[...], gain_ref[...]
        dx, xhat = _rms_bwd(dh, xv, gain_v)
        dx_ref[...] = dyv + dx
        h_ref[...] = (xhat * gain_v).astype(BF16)
        _accumulate(dgain_ref, jnp.sum(dh * xhat, axis=0, keepdims=True), pl.program_id(0) == 0)

    row = pl.BlockSpec((tm, d), lambda i: (i, 0))
    wide = pl.BlockSpec((tm, f), lambda i: (i, 0))
    outs, exchanged = _hosted_call(
        body, exchange, name=f"ffn_bwd_{tag}", grid=(t // tm,),
        in_specs=[row, row, _resident((1, d)), wide, wide, _resident(wg.shape), _resident(wu.shape), _resident(wd.shape)],
        out_specs=[row, wide, wide, wide, row, pl.BlockSpec((1, d), lambda i: (0, 0))],
        out_shape=[jax.ShapeDtypeStruct((t, d), F32), jax.ShapeDtypeStruct((t, f), BF16), jax.ShapeDtypeStruct((t, f), BF16),
                   jax.ShapeDtypeStruct((t, f), BF16), jax.ShapeDtypeStruct((t, d), BF16), jax.ShapeDtypeStruct((1, d), F32)],
        scratch_shapes=[], semantics=("arbitrary",), args=(dy, x, gain, g, u, wg, wu, wd))
    return (*outs, exchanged)


def _tn_matmul(a, b, name, into=None, row_block=0, row_blocks=1):
    t, k = a.shape
    n = b.shape[1]
    tk = k // 2 if (k // 2) % LANES == 0 else k
    tt = min(2048, t)
    first = row_block * (k // tk)

    def body(a_ref, b_ref, *rest):
        o_ref, acc_ref = rest[-2:]
        prod = lax.dot_general(a_ref[...].astype(BF16), b_ref[...].astype(BF16), TN, preferred_element_type=F32)
        j = pl.program_id(1)

        @pl.when(j == 0)
        def _():
            acc_ref[...] = prod

        @pl.when(j > 0)
        def _():
            acc_ref[...] += prod

        @pl.when(j == pl.num_programs(1) - 1)
        def _():
            o_ref[...] = acc_ref[...].astype(BF16)

    return pl.pallas_call(
        body, name=name, grid=(k // tk, t // tt),
        in_specs=[pl.BlockSpec((tt, tk), lambda i, j: (j, i)), pl.BlockSpec((tt, n), lambda i, j: (j, 0))]
                 + ([ANY] if into is not None else []),
        out_specs=pl.BlockSpec((tk, n), lambda i, j: (first + i, 0)),
        out_shape=jax.ShapeDtypeStruct((row_blocks * k, n), BF16),
        scratch_shapes=[pltpu.VMEM((tk, n), F32)],
        input_output_aliases={2: 0} if into is not None else {},
        compiler_params=_params("parallel", "arbitrary"),
    )(a, b, *([into] if into is not None else []))


def _attn_out_bwd(dx, w, o, dils, seq, tag, lse=None, sink=None):
    t, d = dx.shape
    ts = _tile_rows(seq)
    bl = t // seq
    ng = len(dils)
    has_sink = sink is not None
    expand = _head_expand().T

    def body(*refs):
        refs = list(refs)
        dx_ref, w_ref, o_ref, e_ref = refs[:4]
        refs = refs[4:]
        lse_ref, sink_ref = (refs.pop(0), refs.pop(0)) if has_sink else (None, None)
        do_refs, dl_refs = refs[:ng], refs[ng:2 * ng]
        refs = refs[2 * ng:]
        dsink_ref = refs.pop(0) if has_sink else None
        dof_ref, dlf_ref = refs
        do = lax.dot_general(dx_ref[...].astype(BF16), w_ref[...], NT, preferred_element_type=F32)
        prod = do * o_ref[...].astype(F32)
        hi = prod.astype(BF16)
        lo = (prod - hi.astype(F32)).astype(BF16)
        e = e_ref[...]
        dl = jnp.dot(hi, e, preferred_element_type=F32) + jnp.dot(lo, e, preferred_element_type=F32)
        for g in range(ng):
            for r, part in enumerate(_split_rows(do, dof_ref, dils[g])):
                do_refs[g][r] = part.astype(BF16)
            for r, part in enumerate(_split_rows(dl, dlf_ref, dils[g])):
                dl_refs[g][r] = part
        if has_sink:
            part = -jnp.exp(sink_ref[...] - lse_ref[...]) * dl
            _accumulate(dsink_ref, jnp.sum(part, axis=0, keepdims=True), pl.program_id(0) == 0)

    row = pl.BlockSpec((ts, d), lambda i: (i, 0))
    narrow = pl.BlockSpec((ts, LANES), lambda i: (i, 0))
    args = [dx, w, o, expand]
    in_specs = [row, _resident(w.shape), pl.BlockSpec((ts, Q_W), lambda i: (i, 0)), _resident(expand.shape)]
    if has_sink:
        args += [lse, jnp.pad(sink.reshape(1, N_HEADS), ((0, 0), (0, LANES - N_HEADS)))]
        in_specs += [narrow, _resident((1, LANES))]
    out_specs = [_res_spec(seq, dl, Q_W) for dl in dils] + [_res_spec(seq, dl, LANES) for dl in dils]
    out_shape = ([jax.ShapeDtypeStruct(_res_shape(bl, seq, dl, Q_W), BF16) for dl in dils]
                 + [jax.ShapeDtypeStruct(_res_shape(bl, seq, dl, LANES), F32) for dl in dils])
    if has_sink:
        out_specs.append(pl.BlockSpec((1, LANES), lambda i: (0, 0)))
        out_shape.append(jax.ShapeDtypeStruct((1, LANES), F32))
    outs = pl.pallas_call(
        body, name=f"attn_out_bwd_{tag}", grid=(t // ts,), in_specs=in_specs, out_specs=out_specs, out_shape=out_shape,
        scratch_shapes=[_stage(ts, Q_W), _stage(ts, LANES)],
        compiler_params=_params("arbitrary" if has_sink else "parallel"),
    )(*args)
    return list(outs[:ng]), list(outs[ng:2 * ng]), (outs[2 * ng] if has_sink else None)


def _attn_bwd(qkv, do, lse, delta, cos, sin, w, tag, exchange=()):
    shape = qkv.shape
    dil = shape[1]
    rows_all = _seq_view(qkv)
    nseq, length, _ = rows_all.shape
    bq = min(QUERY_BLOCK, length)
    wk = _key_rows(bq, w, length)
    nb = length // bq

    def body(qkv_ref, do_ref, lse_ref, dl_ref, cos_ref, sin_ref, dp_ref, kk_ref, vv_ref, dk_ref, dv_ref):
        _swap_halves(qkv_ref, Q_W, kk_ref)
        _swap_halves(qkv_ref, Q_W + KV_W, vv_ref)
        dk_ref[...] = jnp.zeros_like(dk_ref)
        dv_ref[...] = jnp.zeros_like(dv_ref)
        band = _band(bq, wk)
        lo_q = lax.broadcasted_iota(jnp.int32, (bq, LANES), 1) < HEAD_DIM
        hi_q = jnp.logical_not(lo_q)

        def block(i, carry):
            q0, k0 = _window(i, bq, w, wk, length)
            valid = jnp.abs(band + (q0 - k0)) <= w
            rows, krows = pl.ds(q0, bq), pl.ds(k0, wk)
            c, sn = cos_ref[rows, :], -sin_ref[rows, :]
            lse_t, dl_t = lse_ref[rows, :], dl_ref[rows, :]
            for kv in range(N_KV):
                heads = [(kv * GRP + h, h % 2) for h in range(GRP)]
                cols = [slice((kv * 2 + j) * LANES, (kv * 2 + j + 1) * LANES) for j in range(GRP // 2)]
                qp = [qkv_ref[rows, cs] for cs in cols]
                dop = [do_ref[rows, cs] for cs in cols]
                k2 = _pair_operand(qkv_ref, kk_ref, Q_W, kv, krows)
                v2 = _pair_operand(qkv_ref, vv_ref, Q_W + KV_W, kv, krows)
                sc2 = [lax.dot_general(q_, k2, NT, preferred_element_type=F32) for q_ in qp]
                dp2 = [lax.dot_general(d_, v2, NT, preferred_element_type=F32) for d_ in dop]
                sc = [s_[:, half * wk:(half + 1) * wk] for s_ in sc2 for half in range(2)]
                dp = [d_[:, half * wk:(half + 1) * wk] for d_ in dp2 for half in range(2)]
                p = [jnp.exp(jnp.where(valid, s_, NEG_INF) - _over_keys(lse_t[:, hd:hd + 1], wk))
                     for s_, (hd, _) in zip(sc, heads)]
                ds = [(p_ * (dp_ - _over_keys(dl_t[:, hd:hd + 1], wk))).astype(BF16) for p_, dp_, (hd, _) in zip(p, dp, heads)]
                pb = [p_.astype(BF16) for p_ in p]
                for j in range(GRP // 2):
                    dq = jnp.dot(jnp.concatenate([ds[2 * j], ds[2 * j + 1]], axis=1), k2, preferred_element_type=F32) * SCALE
                    dp_ref[rows, cols[j]] = _rope(dq, c, sn).astype(BF16)
                zero = jnp.zeros((bq, LANES), BF16)
                q4 = jnp.concatenate([jnp.where(lo_q if h % 2 == 0 else hi_q, qp[h // 2], zero) for h in range(GRP)], axis=0)
                do4 = jnp.concatenate([jnp.where(lo_q if h % 2 == 0 else hi_q, dop[h // 2], zero) for h in range(GRP)], axis=0)
                dk_ref[kv, krows, :] += lax.dot_general(jnp.concatenate(ds, axis=0), q4, TN, preferred_element_type=F32)
                dv_ref[kv, krows, :] += lax.dot_general(jnp.concatenate(pb, axis=0), do4, TN, preferred_element_type=F32)
            return carry

        lax.fori_loop(0, nb, block, 0)
        lo = lax.broadcasted_iota(jnp.int32, (length, LANES), 1) < HEAD_DIM
        c, sn = cos_ref[...], -sin_ref[...]
        for ch in range(KV_W // LANES):
            halves = []
            for acc_ref in (dk_ref, dv_ref):
                even, odd = acc_ref[2 * ch], acc_ref[2 * ch + 1]
                even = even + pltpu.roll(even, HEAD_DIM, 1)
                odd = odd + pltpu.roll(odd, HEAD_DIM, 1)
                halves.append(jnp.where(lo, even, odd))
            dp_ref[:, Q_W + ch * LANES:Q_W + (ch + 1) * LANES] = _rope(halves[0], c, sn).astype(BF16)
            dp_ref[:, Q_W + KV_W + ch * LANES:Q_W + KV_W + (ch + 1) * LANES] = halves[1].astype(BF16)

    def seq_block(c):
        return pl.BlockSpec((None, length, c), lambda i: (i, 0, 0))

    table = pl.BlockSpec((None, length, LANES), lambda i: (i % dil, 0, 0))
    (out,), exchanged = _hosted_call(
        body, exchange, name=f"attn_bwd_{tag}", grid=(nseq,),
        in_specs=[seq_block(QKV_W), seq_block(Q_W), seq_block(LANES), seq_block(LANES), table, table],
        out_specs=[pl.BlockSpec((None, length, QKV_W), lambda i: (i, 0, 0))],
        out_shape=[jax.ShapeDtypeStruct((nseq, length, QKV_W), BF16)],
        scratch_shapes=[pltpu.VMEM((KV_W // LANES, length, LANES), BF16), pltpu.VMEM((KV_W // LANES, length, LANES), BF16),
                        pltpu.VMEM((N_KV, length, LANES), F32), pltpu.VMEM((N_KV, length, LANES), F32)],
        semantics=("parallel",), args=(rows_all, _seq_view(do), _seq_view(lse), _seq_view(delta), cos, sin))
    return out.reshape(shape), exchanged


def _qkv_bwd(dy, x, gain, w, dps, dils, seq, tag, exchange=()):
    t, d = x.shape
    ts = _tile_rows(seq)
    bl = t // seq
    ng = len(dps)

    def body(dy_ref, x_ref, gain_ref, w_ref, *refs):
        dp_refs, dx_ref = refs[:ng], refs[ng]
        h_refs = refs[ng + 1:2 * ng + 1]
        dgain_ref, stage_ref = refs[2 * ng + 1:]
        dh = None
        for gi in range(ng):
            dil = dils[gi]
            n = ts // dil
            dp = dp_refs[gi][0] if dil == 1 else jnp.concatenate([dp_refs[gi][r] for r in range(dil)], axis=0)
            part = jnp.dot(dp, w_ref[gi * QKV_W:(gi + 1) * QKV_W, :], preferred_element_type=F32)
            part = _merge_rows([part[r * n:(r + 1) * n] for r in range(dil)], stage_ref, dil)
            dh = part if dh is None else dh + part
        xv, gain_v = x_ref[...], gain_ref[...]
        dx, xhat = _rms_bwd(dh, xv, gain_v)
        dx_ref[...] = dy_ref[...] + dx
        h = xhat * gain_v
        for gi in range(ng):
            for r, part in enumerate(_split_rows(h, stage_ref, dils[gi])):
                h_refs[gi][r] = part.astype(BF16)
        _accumulate(dgain_ref, jnp.sum(dh * xhat, axis=0, keepdims=True), pl.program_id(0) == 0)

    row = pl.BlockSpec((ts, d), lambda i: (i, 0))
    outs, exchanged = _hosted_call(
        body, exchange, name=f"qkv_bwd_{tag}", grid=(t // ts,),
        in_specs=[row, row, _resident((1, d)), _resident(w.shape)] + [_res_spec(seq, dl, QKV_W) for dl in dils],
        out_specs=[row] + [_res_spec(seq, dl, d) for dl in dils] + [pl.BlockSpec((1, d), lambda i: (0, 0))],
        out_shape=[jax.ShapeDtypeStruct((t, d), F32)] + [jax.ShapeDtypeStruct(_res_shape(bl, seq, dl, d), BF16) for dl in dils]
                  + [jax.ShapeDtypeStruct((1, d), F32)],
        scratch_shapes=[_stage(ts, d)], semantics=("arbitrary",), args=(dy, x, gain, w, *dps))
    return outs[0], list(outs[1:1 + ng]), outs[1 + ng], exchanged


ANY = pl.BlockSpec(memory_space=pl.ANY)


def _place():
    x, y, c = lax.axis_index("x"), lax.axis_index("y"), lax.axis_index("c")
    return x, y, c


def _exchange_steps(srcs, dsts, gather, send_sems, recv_sems, local_sems):
    x, y, c = _place()
    me, sibling = (x, y, c), (x, y, 1 - c)
    chips = [(1 - x, y), (x, 1 - y), (1 - x, 1 - y)]
    mine = 4 * x + 2 * y + c

    def slot(a, device):
        px, py, pc = device
        return dsts[a].at[4 * px + 2 * py + pc]

    def passes(a, k, block, to, src=None):
        rows = slot(a, block)
        return pltpu.make_async_remote_copy(src_ref=rows if src is None else src, dst_ref=rows, send_sem=send_sems.at[a, k],
                                            recv_sem=recv_sems.at[a, k], device_id=to, device_id_type=MESH)

    def scatters(a, k):
        peer = mine ^ k
        return pltpu.make_async_remote_copy(
            src_ref=srcs[a].at[peer], dst_ref=dsts[a].at[mine], send_sem=send_sems.at[a, k - 1], recv_sem=recv_sems.at[a, k - 1],
            device_id=(peer // 4, (peer // 2) % 2, peer % 2), device_id_type=MESH)

    def local(a):
        return pltpu.make_async_copy(srcs[a] if gather[a] else srcs[a].at[mine], dsts[a].at[mine], local_sems.at[a])

    def first_copies(a):
        if not gather[a]:
            return [scatters(a, k) for k in range(1, N_DEV)]
        return [passes(a, 0, me, sibling, src=srcs[a])] + [passes(a, 1 + j, me, (*chip, c), src=srcs[a]) for j, chip in enumerate(chips)]

    def start():
        for a in range(len(srcs)):
            local(a).start()
            for cp in first_copies(a):
                cp.start()

    def forward():
        for a in range(len(srcs)):
            if gather[a]:
                for j, chip in enumerate(chips):
                    passes(a, 1 + j, (*chip, c), me).wait_recv()
                    passes(a, 4 + j, (*chip, c), sibling).start()

    def finish():
        for a in range(len(srcs)):
            if gather[a]:
                passes(a, 0, sibling, me).wait_recv()
                for j, chip in enumerate(chips):
                    passes(a, 4 + j, (*chip, 1 - c), me).wait_recv()
                    passes(a, 4 + j, (*chip, c), sibling).wait_send()
                for cp in first_copies(a):
                    cp.wait_send()
            else:
                for cp in first_copies(a):
                    cp.wait()
            local(a).wait()

    return start, forward, finish


def _exchange_scratch(n):
    return [pltpu.SemaphoreType.DMA((n, N_DEV - 1)), pltpu.SemaphoreType.DMA((n, N_DEV - 1)), pltpu.SemaphoreType.DMA((n,))]


def _exchanged_shapes(exchange):
    return [jax.ShapeDtypeStruct(((N_DEV,) + a.shape) if g else a.shape, a.dtype) for a, g in exchange]


def _hosted_call(body, exchange, *, name, grid, in_specs, out_specs, out_shape, scratch_shapes, semantics, args,
                 input_output_aliases=None):
    single = not isinstance(out_shape, (list, tuple))
    out_specs, out_shape = ([out_specs], [out_shape]) if single else (list(out_specs), list(out_shape))
    scratch, aliases = list(scratch_shapes), dict(input_output_aliases or {})
    if not exchange:
        outs = pl.pallas_call(body, name=name, grid=grid, in_specs=in_specs, out_specs=out_specs, out_shape=out_shape,
                              scratch_shapes=scratch, input_output_aliases=aliases, compiler_params=_params(*semantics))(*args)
        return list(outs), []
    n, n_in, n_out, n_scr = len(exchange), len(in_specs), len(out_specs), len(scratch)
    gather = [g for _, g in exchange]
    steps = math.prod(grid)

    def hosted(*refs):
        own_in, x_in = refs[:n_in], refs[n_in:n_in + n]
        own_out, x_out = refs[n_in + n:n_in + n + n_out], refs[n_in + n + n_out:n_in + 2 * n + n_out]
        own_scr, sems = refs[n_in + 2 * n + n_out:n_in + 2 * n + n_out + n_scr], refs[-3:]
        step = pl.program_id(0)
        for axis in range(1, len(grid)):
            step = step * grid[axis] + pl.program_id(axis)
        start, forward, finish = _exchange_steps(x_in, x_out, gather, *sems)
        pl.when(step == 0)(start)
        body(*own_in, *own_out, *own_scr)
        pl.when(step == steps // 2)(forward)
        pl.when(step == steps - 1)(finish)

    outs = pl.pallas_call(
        hosted, name=name, grid=grid, in_specs=list(in_specs) + [ANY] * n, out_specs=out_specs + [ANY] * n,
        out_shape=out_shape + _exchanged_shapes(exchange), scratch_shapes=scratch + _exchange_scratch(n),
        input_output_aliases=aliases, compiler_params=_params(*["arbitrary"] * len(grid)),
    )(*args, *[a for a, _ in exchange])
    return list(outs[:n_out]), list(outs[n_out:])


def _exchange_now(exchange, name):
    n = len(exchange)
    gather = [g for _, g in exchange]

    def body(*refs):
        for step in _exchange_steps(refs[:n], refs[n:2 * n], gather, *refs[2 * n:]):
            step()

    return pl.pallas_call(
        body, name=name, in_specs=[ANY] * n, out_specs=[ANY] * n, out_shape=_exchanged_shapes(exchange),
        scratch_shapes=_exchange_scratch(n),
    )(*[a for a, _ in exchange])


def _all_reduce_small(v):
    def body(v_ref, o_ref, recv_ref, send_sems, recv_sems):
        x, y, c = _place()
        me = 4 * x + 2 * y + c
        copies = []
        for k in range(1, N_DEV):
            peer = me ^ k
            copies.append(pltpu.make_async_remote_copy(
                src_ref=v_ref, dst_ref=recv_ref.at[k], send_sem=send_sems.at[k - 1], recv_sem=recv_sems.at[k - 1],
                device_id=(peer // 4, (peer // 2) % 2, peer % 2), device_id_type=MESH))
        for cp in copies:
            cp.start()
        recv_ref[0] = v_ref[...]
        for cp in copies:
            cp.wait()
        acc = recv_ref[me]
        for src in range(1, N_DEV):
            acc = acc + recv_ref[me ^ src]
        o_ref[...] = acc

    vm = pl.BlockSpec(memory_space=pltpu.VMEM)
    return pl.pallas_call(
        body, name="all_reduce_small", in_specs=[vm], out_specs=vm, out_shape=jax.ShapeDtypeStruct(v.shape, F32),
        scratch_shapes=[pltpu.VMEM((N_DEV,) + v.shape, F32), pltpu.SemaphoreType.DMA((N_DEV - 1,)),
                        pltpu.SemaphoreType.DMA((N_DEV - 1,))],
    )(v)


def _adamw_math(w, g, m, v):
    m = ADAM_B1 * m + (1.0 - ADAM_B1) * g
    v = ADAM_B2 * v + (1.0 - ADAM_B2) * (g * g)
    m_hat = m / (1.0 - ADAM_B1 ** ADAM_STEP)
    v_hat = v / (1.0 - ADAM_B2 ** ADAM_STEP)
    delta = -ADAM_LR * (m_hat / (jnp.sqrt(v_hat) + ADAM_EPS) + ADAM_WD * w)
    return delta, m, v


def _adamw(parts, w, m, v, name, layer=None, into=None):
    r, c = w.shape[-2:]
    tr = r // 2 if r % 16 == 0 and r >= 256 else r
    n = len(parts)

    def body(*refs):
        w_ref, m_ref, v_ref = refs[n:n + 3]
        g_ref, d_ref, nm_ref, nv_ref = refs[-4:]
        g = refs[0][...].astype(F32)
        for p_ref in refs[1:n]:
            g = g + p_ref[...].astype(F32)
        g_ref[...] = g
        d_ref[...], nm_ref[...], nv_ref[...] = _adamw_math(w_ref[...], g, m_ref[...], v_ref[...])

    def slab(slot):
        return pl.BlockSpec((None, tr, c), lambda i: (slot, i, 0))

    tile = pl.BlockSpec((tr, c), lambda i: (i, 0)) if layer is None else slab(layer)
    arrays, in_specs = [], []
    for p in parts:
        if isinstance(p, tuple):
            arrays.append(p[0])
            in_specs.append(slab(p[1]))
        else:
            arrays.append(p)
            in_specs.append(tile)
    kept = list(into) if into is not None else []
    return pl.pallas_call(
        body, name=name, grid=(r // tr,), in_specs=in_specs + [tile] * 3 + [ANY] * len(kept), out_specs=[tile] * 4,
        out_shape=[jax.ShapeDtypeStruct(w.shape, F32)] * 4,
        input_output_aliases={n + 3 + k: k for k in range(len(kept))}, compiler_params=_params("parallel"),
    )(*arrays, w, m, v, *kept)


def _rows(g):
    return g.reshape(-1, g.shape[-1])


def _row_blocks(dw):
    k, n = dw.shape
    return dw.reshape(N_DEV, k // N_DEV, n)


def _pack_rows(rows, width):
    out = None
    for i, r in enumerate(rows):
        r = r.reshape(1, -1).astype(F32)
        r = jnp.pad(r, ((i, 8 - 1 - i), (0, width - r.shape[1])))
        out = r if out is None else out + r
    return out


def _mixer_fwd(x, gain, w_in, w_out, cos, sin, seq, groups, tag, sink=None, exchanges=None):
    qkvs, os, lses, got = [], [], [], {}
    for gi, (dil, w) in enumerate(groups):
        qkv, got["proj", gi] = _qkv_proj(x, gain, w_in, _tables_tiled(cos, seq, dil), _tables_tiled(sin, seq, dil), seq, dil, gi,
                                         f"{tag}{gi}", exchange=(exchanges or {}).get(("proj", gi), ()))
        o, lse, got[gi] = _attn_fwd(qkv, w, f"{tag}{gi}", sink=sink, exchange=(exchanges or {}).get(gi, ()))
        qkvs.append(qkv)
        os.append(o)
        lses.append(lse)
    y, o, lses = _out_proj(x, os, lses, [dl for dl, _ in groups], w_out, seq, tag)
    return y, (qkvs, o, lses), got


def _mixer_bwd(dy, x_in, gain, w_in, w_out, saved, cos, sin, seq, groups, tag, sink=None, exchanges=None, scatter_dw_out=False):
    qkvs, o, lses = saved
    t, d = x_in.shape
    dils = [dl for dl, _ in groups]
    lse_tokens = lses[0].reshape(t, LANES) if sink is not None else None
    dos, dls, dsink = _attn_out_bwd(dy, w_out, o, dils, seq, tag, lse=lse_tokens, sink=sink)
    dw_out = _tn_matmul(o, dy, f"dw_out_{tag}")
    exchanges = {gi: list(e) for gi, e in (exchanges or {}).items()}
    if scatter_dw_out:
        exchanges[0] = exchanges.get(0, []) + _to_send([dw_out])
    dps, got = [], {}
    for gi, (dil, w) in enumerate(groups):
        dp, got[gi] = _attn_bwd(qkvs[gi], dos[gi], lses[gi], dls[gi], _tables_by_residue(cos, seq, dil),
                                _tables_by_residue(sin, seq, dil), w, f"{tag}{gi}", exchange=exchanges.get(gi, ()))
        dps.append(dp)
    if scatter_dw_out:
        dw_out = got[0].pop()
    dx, hs, dgain, _ = _qkv_bwd(dy, x_in, gain, w_in, dps, dils, seq, tag)
    dw_in = None
    for gi in range(len(groups)):
        dw_in = _tn_matmul(dps[gi].reshape(t, QKV_W), hs[gi].reshape(t, d), f"dw_in_{tag}{gi}", into=dw_in, row_block=gi,
                           row_blocks=len(groups))
    return dx, dw_in, dw_out, dgain, dsink, got


def _ffn_layer_bwd(dy, x_in, gain, g, u, wg, wu, wd, tag, exchange=()):
    dx, dg, du, act, h, dgain, got = _ffn_bwd(dy, x_in, gain, g, u, wg, wu, wd, tag, exchange=exchange)
    dwd = _tn_matmul(act, dy, f"dw_down_{tag}")
    dwg = _tn_matmul(dg, h, f"dw_gate_{tag}")
    dwu = _tn_matmul(du, h, f"dw_up_{tag}")
    return dx, dwg, dwu, dwd, dgain, got


def _to_send(dws):
    return [(_row_blocks(g), False) for g in dws]


def kernel(x, a_w_in, a_sink, a_w_out, b_w_in, b_w_out, norm_mix, norm_ffn, w_gate, w_up, w_down, final_norm, loss_target, m_a_w_in, m_a_sink, m_a_w_out, m_b_w_in, m_b_w_out, m_norm_mix, m_norm_ffn, m_w_gate, m_w_up, m_w_down, m_final_norm, v_a_w_in, v_a_sink, v_a_w_out, v_b_w_in, v_b_w_out, v_norm_mix, v_norm_ffn, v_w_gate, v_w_up, v_w_down, v_final_norm):
    bl, seq, d = x.shape
    t = bl * seq
    xf = x.reshape(t, d)
    target = loss_target.reshape(t, d)
    cos, sin = _rope_tables(seq)
    groups_a = [(1, ATTN_HALF_WINDOW)]
    groups_b = [(dil, window // 2 // dil) for window, dil in DILATED_GROUPS]

    def flip(w_):
        return jnp.swapaxes(w_, -1, -2)

    a_w_in, m_a_w_in, v_a_w_in, b_w_in, m_b_w_in, v_b_w_in = map(flip, (a_w_in, m_a_w_in, v_a_w_in, b_w_in, m_b_w_in, v_b_w_in))
    w_gate, m_w_gate, v_w_gate, w_up, m_w_up, v_w_up = map(flip, (w_gate, m_w_gate, v_w_gate, w_up, m_w_up, v_w_up))

    def shard(w_, layer):
        return (w_[layer].astype(BF16), True)

    wa_in, wa_out = map(_rows, _exchange_now([shard(a_w_in, 0), shard(a_w_out, 0)], "gather_first"))

    x1_0, saved_a, got = _mixer_fwd(xf, norm_mix[0:1], wa_in, wa_out, cos, sin, seq, groups_a, "a", sink=a_sink[0],
                                    exchanges={("proj", 0): [shard(w_down, 0)], 0: [shard(w_gate, 0), shard(w_up, 0)]})
    wg0, wu0, wd0 = map(_rows, got[0] + got["proj", 0])
    x2_0, g0, u0, got = _ffn_fwd(x1_0, norm_ffn[0:1], wg0, wu0, wd0, "0", exchange=[shard(b_w_in, 0), shard(b_w_out, 0)])
    wb_in, wb_out = map(_rows, got)
    x1_1, saved_b, got = _mixer_fwd(x2_0, norm_mix[1:2], wb_in, wb_out, cos, sin, seq, groups_b, "b",
                                    exchanges={0: [shard(w_gate, 1)], 1: [shard(w_up, 1)], 2: [shard(w_down, 1)]})
    wg1, wu1, wd1 = map(_rows, got[0] + got[1] + got[2])
    x2_1, g1, u1, _ = _ffn_fwd(x1_1, norm_ffn[1:2], wg1, wu1, wd1, "1")

    dy, loss_part, d_final = _loss_bwd(x2_1, final_norm.reshape(1, d), target)
    dy, dwg1, dwu1, dwd1, d_nf1, _ = _ffn_layer_bwd(dy, x1_1, norm_ffn[1:2], g1, u1, wg1, wu1, wd1, "1")
    dy, dwb_in, dwb_out, d_nm1, _, got = _mixer_bwd(
        dy, x2_0, norm_mix[1:2], wb_in, wb_out, saved_b, cos, sin, seq, groups_b, "b",
        exchanges={0: _to_send([dwg1, dwd1]), 1: _to_send([dwu1])})
    (r_g1, r_d1), (r_u1,) = got[0], got[1]
    dy, dwg0, dwu0, dwd0, d_nf0, (r_b_in, r_b_out) = _ffn_layer_bwd(
        dy, x1_0, norm_ffn[0:1], g0, u0, wg0, wu0, wd0, "0", exchange=_to_send([dwb_in, dwb_out]))
    dy, dwa_in, r_a_out, d_nm0, d_sink, got = _mixer_bwd(
        dy, xf, norm_mix[0:1], wa_in, wa_out, saved_a, cos, sin, seq, groups_a, "a", sink=a_sink[0],
        exchanges={0: _to_send([dwg0, dwu0, dwd0])}, scatter_dw_out=True)
    r_g0, r_u0, r_d0 = got[0]
    (r_a_in,) = _exchange_now(_to_send([dwa_in]), "scatter_last")
    grad_x = dy.reshape(bl, seq, d)

    def update(received, w_, m_, v_, name):
        out = None
        for layer in reversed(range(len(received))):
            out = _adamw([(received[layer], src) for src in range(N_DEV)], w_, m_, v_, f"adamw_{name}{layer}", layer=layer, into=out)
        return out

    u_a_in = update([r_a_in], a_w_in, m_a_w_in, v_a_w_in, "a_in")
    u_a_out = update([r_a_out], a_w_out, m_a_w_out, v_a_w_out, "a_out")
    u_b_in = update([r_b_in], b_w_in, m_b_w_in, v_b_w_in, "b_in")
    u_b_out = update([r_b_out], b_w_out, m_b_w_out, v_b_w_out, "b_out")
    u_gate = update([r_g0, r_g1], w_gate, m_w_gate, v_w_gate, "gate")
    u_up = update([r_u0, r_u1], w_up, m_w_up, v_w_up, "up")
    u_down = update([r_d0, r_d1], w_down, m_w_down, v_w_down, "down")

    small = _pack_rows([d_nm0, d_nm1, d_nf0, d_nf1, d_final, d_sink, loss_part], d)
    total = _all_reduce_small(small)
    small_w = _pack_rows([norm_mix[0], norm_mix[1], norm_ffn[0], norm_ffn[1], final_norm, a_sink], d)
    small_m = _pack_rows([m_norm_mix[0], m_norm_mix[1], m_norm_ffn[0], m_norm_ffn[1], m_final_norm, m_a_sink], d)
    small_v = _pack_rows([v_norm_mix[0], v_norm_mix[1], v_norm_ffn[0], v_norm_ffn[1], v_final_norm, v_a_sink], d)
    u_small = _adamw([total], small_w, small_m, small_v, "adamw_small")
    loss = total[6, 0]

    outs = []
    for k in range(4):
        sm = u_small[k]
        outs += [flip(u_a_in[k]), sm[5:6, :N_HEADS], u_a_out[k], flip(u_b_in[k]), u_b_out[k], sm[0:2], sm[2:4],
                 flip(u_gate[k]), flip(u_up[k]), u_down[k], sm[4]]
    return (loss, grad_x, *outs)
```

```python
import functools
import math

import jax
import jax.numpy as jnp
from jax import lax
from jax.experimental import pallas as pl
from jax.experimental.pallas import tpu as pltpu

F32 = jnp.float32
BF16 = jnp.bfloat16

HEAD_DIM = 64
N_HEADS = 16
N_KV = 4
GRP = N_HEADS // N_KV
Q_W = N_HEADS * HEAD_DIM
KV_W = N_KV * HEAD_DIM
QKV_W = Q_W + 2 * KV_W
ATTN_HALF_WINDOW = 128
DILATED_GROUPS = ((128, 1), (512, 4), (2048, 16))
ROPE_THETA = 10000.0
RMS_EPS = 1e-6
NEG_INF = -1e30
SCALE = 1.0 / math.sqrt(HEAD_DIM)

ADAM_LR = 0.001
ADAM_B1 = 0.9
ADAM_B2 = 0.999
ADAM_EPS = 1e-08
ADAM_WD = 0.01
ADAM_STEP = 10

LANES = 128
VMEM_LIMIT = 56 * 1024 * 1024
QUERY_BLOCK = 128
N_DEV = 8
MESH = pl.DeviceIdType.MESH

NT = (((1,), (1,)), ((), ()))
TN = (((0,), (0,)), ((), ()))


def _params(*sem):
    return pltpu.CompilerParams(dimension_semantics=tuple(sem) if sem else None, vmem_limit_bytes=VMEM_LIMIT)


def _resident(shape):
    return pl.BlockSpec(shape, lambda *_: (0,) * len(shape), pipeline_mode=pl.Buffered(1))


def _rope_tables(seq):
    inv_freq = 1.0 / (ROPE_THETA ** (jnp.arange(0, HEAD_DIM, 2, dtype=F32) / HEAD_DIM))
    ang = jnp.arange(seq, dtype=F32)[:, None] * inv_freq[None, :]
    cos, sin = jnp.cos(ang), jnp.sin(ang)
    return jnp.tile(cos, (1, 4)), jnp.concatenate([-sin, sin, -sin, sin], axis=1)


def _rope(t, cos, sin_signed):
    lane = lax.broadcasted_iota(jnp.int32, t.shape, 1)
    first = (lane & (HEAD_DIM // 2)) == 0
    swapped = jnp.where(first, pltpu.roll(t, LANES - HEAD_DIM // 2, 1), pltpu.roll(t, HEAD_DIM // 2, 1))
    return t * cos + swapped * sin_signed


def _rms(x):
    return lax.rsqrt(jnp.mean(x * x, axis=-1, keepdims=True) + RMS_EPS)


def _rms_bwd(dh, x, gain):
    r = _rms(x)
    xhat = x * r
    dxh = dh * gain
    dx = r * (dxh - xhat * jnp.mean(dxh * xhat, axis=-1, keepdims=True))
    return dx, xhat


def _accumulate(ref, value, first):
    @pl.when(first)
    def _():
        ref[...] = jnp.zeros_like(ref)

    ref[...] += value


def _tile_rows(seq):
    return min(512, seq)


def _res_shape(bl, seq, dil, c):
    ts = _tile_rows(seq)
    return (bl, dil, seq // ts, ts // dil, c)


def _res_spec(seq, dil, c):
    ts = _tile_rows(seq)
    per_seq = seq // ts
    return pl.BlockSpec((None, dil, None, ts // dil, c), lambda i: (i // per_seq, 0, i % per_seq, 0, 0))


def _seq_view(a):
    bl, dil, tiles, n, c = a.shape
    return a.reshape(bl * dil, tiles * n, c)


def _stage(ts, c):
    return pltpu.VMEM((c // LANES, ts, LANES), F32)


def _split_rows(val, stage_ref, dil):
    if dil == 1:
        return [val]
    ts, c = val.shape
    n, nc = ts // dil, c // LANES
    for k in range(nc):
        stage_ref[k] = val[:, k * LANES:(k + 1) * LANES]
    return [jnp.concatenate([stage_ref[k, pl.ds(r, n, stride=dil), :] for k in range(nc)], axis=1) for r in range(dil)]


def _merge_rows(parts, stage_ref, dil):
    if dil == 1:
        return parts[0]
    n, c = parts[0].shape
    nc = c // LANES
    for r, part in enumerate(parts):
        for k in range(nc):
            stage_ref[k, pl.ds(r, n, stride=dil), :] = part[:, k * LANES:(k + 1) * LANES]
    return jnp.concatenate([stage_ref[k] for k in range(nc)], axis=1)


def _tables_tiled(table, seq, dil):
    ts = _tile_rows(seq)
    return table.reshape(seq // ts, ts // dil, dil, LANES).transpose(0, 2, 1, 3).reshape(seq, LANES)


def _tables_by_residue(table, seq, dil):
    return table.reshape(seq // dil, dil, LANES).transpose(1, 0, 2)


def _qkv_proj(x, gain, w, cos, sin, seq, dil, group, tag, exchange=()):
    t, d = x.shape
    ts = _tile_rows(seq)
    n = ts // dil
    per_seq = seq // ts

    def body(x_ref, g_ref, w_ref, cos_ref, sin_ref, o_ref, stage_ref):
        xv = jnp.concatenate(_split_rows(x_ref[...], stage_ref, dil), axis=0)
        h = (xv * _rms(xv) * g_ref[...]).astype(BF16)
        acc = lax.dot_general(h, w_ref[...], NT, preferred_element_type=F32)
        c, s = cos_ref[...], sin_ref[...]
        for j in range(QKV_W // LANES):
            cols = slice(j * LANES, (j + 1) * LANES)
            val = acc[:, cols]
            if j < (Q_W + KV_W) // LANES:
                val = _rope(val, c, s)
            if j < Q_W // LANES:
                val = val * SCALE
            val = val.astype(BF16)
            for r in range(dil):
                o_ref[r, :, cols] = val[r * n:(r + 1) * n]

    table = pl.BlockSpec((ts, LANES), lambda i: (i % per_seq, 0))
    (qkv,), exchanged = _hosted_call(
        body, exchange, name=f"qkv_proj_{tag}", grid=(t // ts,),
        in_specs=[pl.BlockSpec((ts, d), lambda i: (i, 0)), pl.BlockSpec((1, d), lambda i: (0, 0)),
                  pl.BlockSpec((QKV_W, d), lambda i: (group, 0)), table, table],
        out_specs=[_res_spec(seq, dil, QKV_W)],
        out_shape=[jax.ShapeDtypeStruct(_res_shape(t // seq, seq, dil, QKV_W), BF16)],
        scratch_shapes=[_stage(ts, d)], semantics=("parallel",), args=(x, gain, w, cos, sin))
    return qkv, exchanged


def _band(bq, wk):
    return lax.broadcasted_iota(jnp.int32, (bq, wk), 0) - lax.broadcasted_iota(jnp.int32, (bq, wk), 1)


def _swap_halves(src_ref, base, dst_ref):
    for c in range(KV_W // LANES):
        dst_ref[c] = pltpu.roll(src_ref[:, base + c * LANES:base + (c + 1) * LANES], HEAD_DIM, 1)


def _pair_operand(src_ref, swapped_ref, base, kv, rows):
    c = kv // 2
    chunk, swapped = src_ref[rows, base + c * LANES:base + (c + 1) * LANES], swapped_ref[c, rows, :]
    lo = lax.broadcasted_iota(jnp.int32, chunk.shape, 1) < HEAD_DIM
    zero = jnp.zeros_like(chunk)
    if kv % 2 == 0:
        return jnp.concatenate([jnp.where(lo, chunk, zero), jnp.where(lo, zero, swapped)], axis=0)
    return jnp.concatenate([jnp.where(lo, swapped, zero), jnp.where(lo, zero, chunk)], axis=0)


def _over_keys(col, wk):
    if wk % LANES:
        return jnp.broadcast_to(col, (col.shape[0], wk))
    wide = jnp.broadcast_to(col, (col.shape[0], LANES))
    return wide if wk == LANES else jnp.concatenate([wide] * (wk // LANES), axis=1)


def _key_rows(bq, w, length):
    return min(bq + 2 * w, length)


def _window(i, bq, w, wk, length):
    q0 = pl.multiple_of(i * bq, bq)
    k0 = pl.multiple_of(jnp.clip(q0 - w, 0, length - wk), min(w, bq))
    return q0, k0


def _attn_fwd(qkv, w, tag, sink=None, exchange=()):
    shape = qkv.shape
    rows_all = _seq_view(qkv)
    nseq, length, _ = rows_all.shape
    bq = min(QUERY_BLOCK, length)
    wk = _key_rows(bq, w, length)
    nb = length // bq
    has_sink = sink is not None

    def body(*refs):
        qkv_ref = refs[0]
        sink_ref = refs[1] if has_sink else None
        o_ref, lse_ref, kk_ref, vv_ref = refs[-4:]
        _swap_halves(qkv_ref, Q_W, kk_ref)
        _swap_halves(qkv_ref, Q_W + KV_W, vv_ref)
        band = _band(bq, wk)
        lane = lax.broadcasted_iota(jnp.int32, (bq, LANES), 1)
        lo = lane < HEAD_DIM

        def block(i, carry):
            q0, k0 = _window(i, bq, w, wk, length)
            valid = jnp.abs(band + (q0 - k0)) <= w
            rows, krows = pl.ds(q0, bq), pl.ds(k0, wk)
            lse_tile = jnp.zeros((bq, LANES), F32)
            for kv in range(N_KV):
                heads = [(kv * GRP + h, h % 2) for h in range(GRP)]
                qp = [qkv_ref[rows, (kv * 2 + j) * LANES:(kv * 2 + j + 1) * LANES] for j in range(GRP // 2)]
                k2 = _pair_operand(qkv_ref, kk_ref, Q_W, kv, krows)
                v2 = _pair_operand(qkv_ref, vv_ref, Q_W + KV_W, kv, krows)
                sc2 = [lax.dot_general(q_, k2, NT, preferred_element_type=F32) for q_ in qp]
                sc = [jnp.where(valid, s_[:, half * wk:(half + 1) * wk], NEG_INF) for s_ in sc2 for half in range(2)]
                m = [jnp.max(s_, axis=-1, keepdims=True) for s_ in sc]
                if has_sink:
                    m = [jnp.maximum(m_, sink_ref[hd]) for m_, (hd, _) in zip(m, heads)]
                mb = [jnp.broadcast_to(m_, (bq, LANES)) for m_ in m]
                p = [jnp.exp(s_ - _over_keys(m_, wk)) for s_, m_ in zip(sc, m)]
                den = [jnp.sum(p_, axis=-1, keepdims=True) for p_ in p]
                if has_sink:
                    den = [d_ + jnp.exp(sink_ref[hd] - m_) for d_, m_, (hd, _) in zip(den, m, heads)]
                inv = [jnp.broadcast_to(1.0 / d_, (bq, LANES)) for d_ in den]
                pb = [p_.astype(BF16) for p_ in p]
                for j in range(GRP // 2):
                    o = jnp.dot(jnp.concatenate([pb[2 * j], pb[2 * j + 1]], axis=1), v2, preferred_element_type=F32)
                    o = o * jnp.where(lo, inv[2 * j], inv[2 * j + 1])
                    o_ref[rows, (kv * 2 + j) * LANES:(kv * 2 + j + 1) * LANES] = o.astype(BF16)
                for h, (hd, _) in enumerate(heads):
                    lse_tile = jnp.where(lane == hd, mb[h] - jnp.log(inv[h]), lse_tile)
            lse_ref[rows, :] = lse_tile
            return carry

        lax.fori_loop(0, nb, block, 0)

    args = [rows_all]
    in_specs = [pl.BlockSpec((None, length, QKV_W), lambda i: (i, 0, 0))]
    if has_sink:
        args.append(sink)
        in_specs.append(pl.BlockSpec(memory_space=pltpu.SMEM))
    (o, lse), exchanged = _hosted_call(
        body, exchange, name=f"attn_fwd_{tag}", grid=(nseq,), in_specs=in_specs,
        out_specs=[pl.BlockSpec((None, length, Q_W), lambda i: (i, 0, 0)), pl.BlockSpec((None, length, LANES), lambda i: (i, 0, 0))],
        out_shape=[jax.ShapeDtypeStruct((nseq, length, Q_W), BF16), jax.ShapeDtypeStruct((nseq, length, LANES), F32)],
        scratch_shapes=[pltpu.VMEM((KV_W // LANES, length, LANES), BF16), pltpu.VMEM((KV_W // LANES, length, LANES), BF16)],
        semantics=("parallel",), args=args)
    return o.reshape(shape[:-1] + (Q_W,)), lse.reshape(shape[:-1] + (LANES,)), exchanged


def _head_expand():
    return (jnp.arange(LANES)[:, None] == jnp.arange(Q_W)[None, :] // HEAD_DIM).astype(BF16)


def _out_proj(x, os, lses, dils, w, seq, tag):
    t, d = x.shape
    ts = _tile_rows(seq)
    ng = len(os)
    bl = t // seq
    if ng == 1:
        def body1(x_ref, o_ref, w_ref, y_ref):
            y_ref[...] = x_ref[...] + jnp.dot(o_ref[...], w_ref[...], preferred_element_type=F32)

        row = pl.BlockSpec((ts, d), lambda i: (i, 0))
        o = os[0].reshape(t, Q_W)
        y = pl.pallas_call(
            body1, name=f"out_proj_{tag}", grid=(t // ts,), in_specs=[row, row, _resident(w.shape)], out_specs=row,
            out_shape=jax.ShapeDtypeStruct((t, d), F32), compiler_params=_params("parallel"),
        )(x, o, w)
        return y, o, [lses[0]]

    def body(*refs):
        x_ref, w_ref, e_ref = refs[:3]
        o_refs, l_refs = refs[3:3 + ng], refs[3 + ng:3 + 2 * ng]
        y_ref, om_ref = refs[3 + 2 * ng:5 + 2 * ng]
        lt_refs = refs[5 + 2 * ng:5 + 3 * ng]
        wide_ref, narrow_ref = refs[5 + 3 * ng:]
        ls = [_merge_rows([l_refs[g][r] for r in range(dils[g])], narrow_ref, dils[g]) for g in range(ng)]
        mx = functools.reduce(jnp.maximum, ls)
        tot = mx + jnp.log(functools.reduce(lambda a, b: a + b, [jnp.exp(l_ - mx) for l_ in ls]))
        e = e_ref[...]
        o = None
        for g in range(ng):
            wt = jnp.exp(ls[g] - tot)
            hi = wt.astype(BF16)
            lo = (wt - hi.astype(F32)).astype(BF16)
            wide = jnp.dot(hi, e, preferred_element_type=F32) + jnp.dot(lo, e, preferred_element_type=F32)
            term = wide * _merge_rows([o_refs[g][r].astype(F32) for r in range(dils[g])], wide_ref, dils[g])
            o = term if o is None else o + term
        ob = o.astype(BF16)
        om_ref[...] = ob
        y_ref[...] = x_ref[...] + jnp.dot(ob, w_ref[...], preferred_element_type=F32)
        for g in range(ng):
            for r, part in enumerate(_split_rows(tot, narrow_ref, dils[g])):
                lt_refs[g][r] = part

    row = pl.BlockSpec((ts, d), lambda i: (i, 0))
    e = _head_expand()
    outs = pl.pallas_call(
        body, name=f"out_proj_{tag}", grid=(t // ts,),
        in_specs=[row, _resident(w.shape), _resident(e.shape)] + [_res_spec(seq, dl, Q_W) for dl in dils]
                 + [_res_spec(seq, dl, LANES) for dl in dils],
        out_specs=[row, pl.BlockSpec((ts, Q_W), lambda i: (i, 0))] + [_res_spec(seq, dl, LANES) for dl in dils],
        out_shape=[jax.ShapeDtypeStruct((t, d), F32), jax.ShapeDtypeStruct((t, Q_W), BF16)]
                  + [jax.ShapeDtypeStruct(_res_shape(bl, seq, dl, LANES), F32) for dl in dils],
        scratch_shapes=[_stage(ts, Q_W), _stage(ts, LANES)],
        compiler_params=_params("parallel"),
    )(x, w, e, *os, *lses)
    return outs[0], outs[1], list(outs[2:])


def _sigmoid(g):
    return 1.0 / (1.0 + jnp.exp(-g))


def _ffn_fwd(x, gain, wg, wu, wd, tag, exchange=(), loss_head=None):
    t, d = x.shape
    f = wd.shape[0]
    tm = min(256, t)
    has_loss = loss_head is not None

    def body(*refs):
        x_ref, gain_ref, wg_ref, wu_ref, wd_ref = refs[:5]
        y_ref, g_ref, u_ref = refs[-5:-2] if has_loss else refs[-3:]
        xv = x_ref[...]
        h = (xv * _rms(xv) * gain_ref[...]).astype(BF16)
        g = lax.dot_general(h, wg_ref[...], NT, preferred_element_type=F32)
        u = lax.dot_general(h, wu_ref[...], NT, preferred_element_type=F32)
        g_ref[...] = g.astype(BF16)
        u_ref[...] = u.astype(BF16)
        a = (g * _sigmoid(g) * u).astype(BF16)
        y = xv + jnp.dot(a, wd_ref[...], preferred_element_type=F32)
        if not has_loss:
            y_ref[...] = y
            return
        head_ref, target_ref, loss_ref, dhead_ref = refs[5], refs[6], refs[-2], refs[-1]
        head = head_ref[...]
        yhat = y * _rms(y)
        err = yhat * head - target_ref[...]
        dout = err * (1.0 / d)
        y_ref[...] = _rms_bwd(dout, y, head)[0]
        first = pl.program_id(0) == 0
        part = 0.5 * jnp.sum(jnp.mean(err * err, axis=-1, keepdims=True), axis=0, keepdims=True)
        _accumulate(loss_ref, jnp.broadcast_to(part, loss_ref.shape), first)
        _accumulate(dhead_ref, jnp.sum(dout * yhat, axis=0, keepdims=True), first)

    row = pl.BlockSpec((tm, d), lambda i: (i, 0))
    wide = pl.BlockSpec((tm, f), lambda i: (i, 0))
    in_specs = [row, _resident((1, d)), _resident(wg.shape), _resident(wu.shape), _resident(wd.shape)]
    out_specs = [row, wide, wide]
    out_shape = [jax.ShapeDtypeStruct((t, d), F32), jax.ShapeDtypeStruct((t, f), BF16), jax.ShapeDtypeStruct((t, f), BF16)]
    if has_loss:
        in_specs += [_resident((1, d)), row]
        out_specs += [pl.BlockSpec((1, LANES), lambda i: (0, 0)), pl.BlockSpec((1, d), lambda i: (0, 0))]
        out_shape += [jax.ShapeDtypeStruct((1, LANES), F32), jax.ShapeDtypeStruct((1, d), F32)]
    outs, exchanged = _hosted_call(
        body, exchange, name=f"ffn_fwd_{tag}", grid=(t // tm,), in_specs=in_specs, out_specs=out_specs, out_shape=out_shape,
        scratch_shapes=[], semantics=("arbitrary" if has_loss else "parallel",),
        args=(x, gain, wg, wu, wd) + (tuple(loss_head) if has_loss else ()))
    return (*outs, exchanged)


def _ffn_bwd(dy, x, gain, g, u, wg, wu, wd, tag, exchange=()):
    t, d = x.shape
    f = wd.shape[0]
    tm = min(256, t)

    def body(dy_ref, x_ref, gain_ref, g_ref, u_ref, wg_ref, wu_ref, wd_ref, dx_ref, dg_ref, du_ref, a_ref, h_ref, dgain_ref):
        dyv = dy_ref[...]
        da = lax.dot_general(dyv.astype(BF16), wd_ref[...], NT, preferred_element_type=F32)
        gv, uv = g_ref[...].astype(F32), u_ref[...].astype(F32)
        sg = _sigmoid(gv)
        act = gv * sg
        a_ref[...] = (act * uv).astype(BF16)
        du = (da * act).astype(BF16)
        dg = (da * uv * (sg * (1.0 + gv * (1.0 - sg)))).astype(BF16)
        du_ref[...] = du
        dg_ref[...] = dg
        dh = jnp.dot(dg, wg_ref[...], preferred_element_type=F32) + jnp.dot(du, wu_ref[...], preferred_element_type=F32)
        xv, gain_v = x_ref[...], gain_ref[...]
        dx, xhat = _rms_bwd(dh, xv, gain_v)
        dx_ref[...] = dyv + dx
        h_ref[...] = (xhat * gain_v).astype(BF16)
        _accumulate(dgain_ref, jnp.sum(dh * xhat, axis=0, keepdims=True), pl.program_id(0) == 0)

    row = pl.BlockSpec((tm, d), lambda i: (i, 0))
    wide = pl.BlockSpec((tm, f), lambda i: (i, 0))
    outs, exchanged = _hosted_call(
        body, exchange, name=f"ffn_bwd_{tag}", grid=(t // tm,),
        in_specs=[row, row, _resident((1, d)), wide, wide, _resident(wg.shape), _resident(wu.shape), _resident(wd.shape)],
        out_specs=[row, wide, wide, wide, row, pl.BlockSpec((1, d), lambda i: (0, 0))],
        out_shape=[jax.ShapeDtypeStruct((t, d), F32), jax.ShapeDtypeStruct((t, f), BF16), jax.ShapeDtypeStruct((t, f), BF16),
                   jax.ShapeDtypeStruct((t, f), BF16), jax.ShapeDtypeStruct((t, d), BF16), jax.ShapeDtypeStruct((1, d), F32)],
        scratch_shapes=[], semantics=("arbitrary",), args=(dy, x, gain, g, u, wg, wu, wd))
    return (*outs, exchanged)


def _tn_matmul(a, b, name, into=None, row_block=0, row_blocks=1):
    t, k = a.shape
    n = b.shape[1]
    tk = k // 2 if (k // 2) % LANES == 0 else k
    tt = min(2048, t)
    first = row_block * (k // tk)

    def body(a_ref, b_ref, *rest):
        o_ref, acc_ref = rest[-2:]
        prod = lax.dot_general(a_ref[...].astype(BF16), b_ref[...].astype(BF16), TN, preferred_element_type=F32)
        j = pl.program_id(1)

        @pl.when(j == 0)
        def _():
            acc_ref[...] = prod

        @pl.when(j > 0)
        def _():
            acc_ref[...] += prod

        @pl.when(j == pl.num_programs(1) - 1)
        def _():
            o_ref[...] = acc_ref[...].astype(BF16)

    return pl.pallas_call(
        body, name=name, grid=(k // tk, t // tt),
        in_specs=[pl.BlockSpec((tt, tk), lambda i, j: (j, i)), pl.BlockSpec((tt, n), lambda i, j: (j, 0))]
                 + ([ANY] if into is not None else []),
        out_specs=pl.BlockSpec((tk, n), lambda i, j: (first + i, 0)),
        out_shape=jax.ShapeDtypeStruct((row_blocks * k, n), BF16),
        scratch_shapes=[pltpu.VMEM((tk, n), F32)],
        input_output_aliases={2: 0} if into is not None else {},
        compiler_params=_params("parallel", "arbitrary"),
    )(a, b, *([into] if into is not None else []))


def _attn_out_bwd(dx, w, o, dils, seq, tag, lse=None, sink=None, exchange=()):
    t, d = dx.shape
    ts = _tile_rows(seq)
    bl = t // seq
    ng = len(dils)
    has_sink = sink is not None
    expand = _head_expand().T

    def body(*refs):
        refs = list(refs)
        dx_ref, w_ref, o_ref, e_ref = refs[:4]
        refs = refs[4:]
        lse_ref, sink_ref = (refs.pop(0), refs.pop(0)) if has_sink else (None, None)
        do_refs, dl_refs = refs[:ng], refs[ng:2 * ng]
        refs = refs[2 * ng:]
        dsink_ref = refs.pop(0) if has_sink else None
        dof_ref, dlf_ref = refs
        do = lax.dot_general(dx_ref[...].astype(BF16), w_ref[...], NT, preferred_element_type=F32)
        prod = do * o_ref[...].astype(F32)
        hi = prod.astype(BF16)
        lo = (prod - hi.astype(F32)).astype(BF16)
        e = e_ref[...]
        dl = jnp.dot(hi, e, preferred_element_type=F32) + jnp.dot(lo, e, preferred_element_type=F32)
        for g in range(ng):
            for r, part in enumerate(_split_rows(do, dof_ref, dils[g])):
                do_refs[g][r] = part.astype(BF16)
            for r, part in enumerate(_split_rows(dl, dlf_ref, dils[g])):
                dl_refs[g][r] = part
        if has_sink:
            part = -jnp.exp(sink_ref[...] - lse_ref[...]) * dl
            _accumulate(dsink_ref, jnp.sum(part, axis=0, keepdims=True), pl.program_id(0) == 0)

    row = pl.BlockSpec((ts, d), lambda i: (i, 0))
    narrow = pl.BlockSpec((ts, LANES), lambda i: (i, 0))
    args = [dx, w, o, expand]
    in_specs = [row, _resident(w.shape), pl.BlockSpec((ts, Q_W), lambda i: (i, 0)), _resident(expand.shape)]
    if has_sink:
        args += [lse, jnp.pad(sink.reshape(1, N_HEADS), ((0, 0), (0, LANES - N_HEADS)))]
        in_specs += [narrow, _resident((1, LANES))]
    out_specs = [_res_spec(seq, dl, Q_W) for dl in dils] + [_res_spec(seq, dl, LANES) for dl in dils]
    out_shape = ([jax.ShapeDtypeStruct(_res_shape(bl, seq, dl, Q_W), BF16) for dl in dils]
                 + [jax.ShapeDtypeStruct(_res_shape(bl, seq, dl, LANES), F32) for dl in dils])
    if has_sink:
        out_specs.append(pl.BlockSpec((1, LANES), lambda i: (0, 0)))
        out_shape.append(jax.ShapeDtypeStruct((1, LANES), F32))
    outs, exchanged = _hosted_call(
        body, exchange, name=f"attn_out_bwd_{tag}", grid=(t // ts,), in_specs=in_specs, out_specs=out_specs, out_shape=out_shape,
        scratch_shapes=[_stage(ts, Q_W), _stage(ts, LANES)], semantics=("arbitrary" if has_sink else "parallel",), args=args)
    return list(outs[:ng]), list(outs[ng:2 * ng]), (outs[2 * ng] if has_sink else None), exchanged


def _attn_bwd(qkv, do, lse, delta, cos, sin, w, tag, exchange=()):
    shape = qkv.shape
    dil = shape[1]
    rows_all = _seq_view(qkv)
    nseq, length, _ = rows_all.shape
    bq = min(QUERY_BLOCK, length)
    wk = _key_rows(bq, w, length)
    nb = length // bq

    def body(qkv_ref, do_ref, lse_ref, dl_ref, cos_ref, sin_ref, dp_ref, kk_ref, vv_ref, dk_ref, dv_ref):
        _swap_halves(qkv_ref, Q_W, kk_ref)
        _swap_halves(qkv_ref, Q_W + KV_W, vv_ref)
        dk_ref[...] = jnp.zeros_like(dk_ref)
        dv_ref[...] = jnp.zeros_like(dv_ref)
        band = _band(bq, wk)
        lo_q = lax.broadcasted_iota(jnp.int32, (bq, LANES), 1) < HEAD_DIM
        hi_q = jnp.logical_not(lo_q)

        def block(i, carry):
            q0, k0 = _window(i, bq, w, wk, length)
            valid = jnp.abs(band + (q0 - k0)) <= w
            rows, krows = pl.ds(q0, bq), pl.ds(k0, wk)
            c, sn = cos_ref[rows, :], -sin_ref[rows, :]
            lse_t, dl_t = lse_ref[rows, :], dl_ref[rows, :]
            for kv in range(N_KV):
                heads = [(kv * GRP + h, h % 2) for h in range(GRP)]
                cols = [slice((kv * 2 + j) * LANES, (kv * 2 + j + 1) * LANES) for j in range(GRP // 2)]
                qp = [qkv_ref[rows, cs] for cs in cols]
                dop = [do_ref[rows, cs] for cs in cols]
                k2 = _pair_operand(qkv_ref, kk_ref, Q_W, kv, krows)
                v2 = _pair_operand(qkv_ref, vv_ref, Q_W + KV_W, kv, krows)
                sc2 = [lax.dot_general(q_, k2, NT, preferred_element_type=F32) for q_ in qp]
                dp2 = [lax.dot_general(d_, v2, NT, preferred_element_type=F32) for d_ in dop]
                sc = [s_[:, half * wk:(half + 1) * wk] for s_ in sc2 for half in range(2)]
                dp = [d_[:, half * wk:(half + 1) * wk] for d_ in dp2 for half in range(2)]
                p = [jnp.exp(jnp.where(valid, s_, NEG_INF) - _over_keys(lse_t[:, hd:hd + 1], wk))
                     for s_, (hd, _) in zip(sc, heads)]
                ds = [(p_ * (dp_ - _over_keys(dl_t[:, hd:hd + 1], wk))).astype(BF16) for p_, dp_, (hd, _) in zip(p, dp, heads)]
                pb = [p_.astype(BF16) for p_ in p]
                for j in range(GRP // 2):
                    dq = jnp.dot(jnp.concatenate([ds[2 * j], ds[2 * j + 1]], axis=1), k2, preferred_element_type=F32) * SCALE
                    dp_ref[rows, cols[j]] = _rope(dq, c, sn).astype(BF16)
                zero = jnp.zeros((bq, LANES), BF16)
                q4 = jnp.concatenate([jnp.where(lo_q if h % 2 == 0 else hi_q, qp[h // 2], zero) for h in range(GRP)], axis=0)
                do4 = jnp.concatenate([jnp.where(lo_q if h % 2 == 0 else hi_q, dop[h // 2], zero) for h in range(GRP)], axis=0)
                dk_ref[kv, krows, :] += lax.dot_general(jnp.concatenate(ds, axis=0), q4, TN, preferred_element_type=F32)
                dv_ref[kv, krows, :] += lax.dot_general(jnp.concatenate(pb, axis=0), do4, TN, preferred_element_type=F32)
            return carry

        lax.fori_loop(0, nb, block, 0)
        lo = lax.broadcasted_iota(jnp.int32, (length, LANES), 1) < HEAD_DIM
        c, sn = cos_ref[...], -sin_ref[...]
        for ch in range(KV_W // LANES):
            halves = []
            for acc_ref in (dk_ref, dv_ref):
                even, odd = acc_ref[2 * ch], acc_ref[2 * ch + 1]
                even = even + pltpu.roll(even, HEAD_DIM, 1)
                odd = odd + pltpu.roll(odd, HEAD_DIM, 1)
                halves.append(jnp.where(lo, even, odd))
            dp_ref[:, Q_W + ch * LANES:Q_W + (ch + 1) * LANES] = _rope(halves[0], c, sn).astype(BF16)
            dp_ref[:, Q_W + KV_W + ch * LANES:Q_W + KV_W + (ch + 1) * LANES] = halves[1].astype(BF16)

    def seq_block(c):
        return pl.BlockSpec((None, length, c), lambda i: (i, 0, 0))

    table = pl.BlockSpec((None, length, LANES), lambda i: (i % dil, 0, 0))
    (out,), exchanged = _hosted_call(
        body, exchange, name=f"attn_bwd_{tag}", grid=(nseq,),
        in_specs=[seq_block(QKV_W), seq_block(Q_W), seq_block(LANES), seq_block(LANES), table, table],
        out_specs=[pl.BlockSpec((None, length, QKV_W), lambda i: (i, 0, 0))],
        out_shape=[jax.ShapeDtypeStruct((nseq, length, QKV_W), BF16)],
        scratch_shapes=[pltpu.VMEM((KV_W // LANES, length, LANES), BF16), pltpu.VMEM((KV_W // LANES, length, LANES), BF16),
                        pltpu.VMEM((N_KV, length, LANES), F32), pltpu.VMEM((N_KV, length, LANES), F32)],
        semantics=("parallel",), args=(rows_all, _seq_view(do), _seq_view(lse), _seq_view(delta), cos, sin))
    return out.reshape(shape), exchanged


def _qkv_bwd(dy, x, gain, w, dps, dils, seq, tag, exchange=()):
    t, d = x.shape
    ts = _tile_rows(seq)
    bl = t // seq
    ng = len(dps)

    def body(dy_ref, x_ref, gain_ref, w_ref, *refs):
        dp_refs, dx_ref = refs[:ng], refs[ng]
        h_refs = refs[ng + 1:2 * ng + 1]
        dgain_ref, stage_ref = refs[2 * ng + 1:]
        dh = None
        for gi in range(ng):
            dil = dils[gi]
            n = ts // dil
            dp = dp_refs[gi][0] if dil == 1 else jnp.concatenate([dp_refs[gi][r] for r in range(dil)], axis=0)
            part = jnp.dot(dp, w_ref[gi * QKV_W:(gi + 1) * QKV_W, :], preferred_element_type=F32)
            part = _merge_rows([part[r * n:(r + 1) * n] for r in range(dil)], stage_ref, dil)
            dh = part if dh is None else dh + part
        xv, gain_v = x_ref[...], gain_ref[...]
        dx, xhat = _rms_bwd(dh, xv, gain_v)
        dx_ref[...] = dy_ref[...] + dx
        h = xhat * gain_v
        for gi in range(ng):
            for r, part in enumerate(_split_rows(h, stage_ref, dils[gi])):
                h_refs[gi][r] = part.astype(BF16)
        _accumulate(dgain_ref, jnp.sum(dh * xhat, axis=0, keepdims=True), pl.program_id(0) == 0)

    row = pl.BlockSpec((ts, d), lambda i: (i, 0))
    outs, exchanged = _hosted_call(
        body, exchange, name=f"qkv_bwd_{tag}", grid=(t // ts,),
        in_specs=[row, row, _resident((1, d)), _resident(w.shape)] + [_res_spec(seq, dl, QKV_W) for dl in dils],
        out_specs=[row] + [_res_spec(seq, dl, d) for dl in dils] + [pl.BlockSpec((1, d), lambda i: (0, 0))],
        out_shape=[jax.ShapeDtypeStruct((t, d), F32)] + [jax.ShapeDtypeStruct(_res_shape(bl, seq, dl, d), BF16) for dl in dils]
                  + [jax.ShapeDtypeStruct((1, d), F32)],
        scratch_shapes=[_stage(ts, d)], semantics=("arbitrary",), args=(dy, x, gain, w, *dps))
    return outs[0], list(outs[1:1 + ng]), outs[1 + ng], exchanged


ANY = pl.BlockSpec(memory_space=pl.ANY)


def _place():
    x, y, c = lax.axis_index("x"), lax.axis_index("y"), lax.axis_index("c")
    return x, y, c


def _exchange_steps(srcs, dsts, gather, send_sems, recv_sems, local_sems):
    x, y, c = _place()
    me, sibling = (x, y, c), (x, y, 1 - c)
    chips = [(1 - x, y), (x, 1 - y), (1 - x, 1 - y)]
    mine = 4 * x + 2 * y + c

    def slot(a, device):
        px, py, pc = device
        return dsts[a].at[4 * px + 2 * py + pc]

    def passes(a, k, block, to, src=None):
        rows = slot(a, block)
        return pltpu.make_async_remote_copy(src_ref=rows if src is None else src, dst_ref=rows, send_sem=send_sems.at[a, k],
                                            recv_sem=recv_sems.at[a, k], device_id=to, device_id_type=MESH)

    def scatters(a, k):
        peer = mine ^ k
        return pltpu.make_async_remote_copy(
            src_ref=srcs[a].at[peer], dst_ref=dsts[a].at[mine], send_sem=send_sems.at[a, k - 1], recv_sem=recv_sems.at[a, k - 1],
            device_id=(peer // 4, (peer // 2) % 2, peer % 2), device_id_type=MESH)

    def local(a):
        return pltpu.make_async_copy(srcs[a] if gather[a] else srcs[a].at[mine], dsts[a].at[mine], local_sems.at[a])

    def first_copies(a):
        if not gather[a]:
            return [scatters(a, k) for k in range(1, N_DEV)]
        return [passes(a, 0, me, sibling, src=srcs[a])] + [passes(a, 1 + j, me, (*chip, c), src=srcs[a]) for j, chip in enumerate(chips)]

    def start():
        for a in range(len(srcs)):
            local(a).start()
            for cp in first_copies(a):
                cp.start()

    def forward():
        for a in range(len(srcs)):
            if gather[a]:
                for j, chip in enumerate(chips):
                    passes(a, 1 + j, (*chip, c), me).wait_recv()
                    passes(a, 4 + j, (*chip, c), sibling).start()

    def finish():
        for a in range(len(srcs)):
            if gather[a]:
                passes(a, 0, sibling, me).wait_recv()
                for j, chip in enumerate(chips):
                    passes(a, 4 + j, (*chip, 1 - c), me).wait_recv()
                    passes(a, 4 + j, (*chip, c), sibling).wait_send()
                for cp in first_copies(a):
                    cp.wait_send()
            else:
                for cp in first_copies(a):
                    cp.wait()
            local(a).wait()

    return start, forward, finish


def _exchange_scratch(n):
    return [pltpu.SemaphoreType.DMA((n, N_DEV - 1)), pltpu.SemaphoreType.DMA((n, N_DEV - 1)), pltpu.SemaphoreType.DMA((n,))]


def _exchanged_shapes(exchange):
    return [jax.ShapeDtypeStruct(((N_DEV,) + a.shape) if g else a.shape, a.dtype) for a, g in exchange]


def _hosted_call(body, exchange, *, name, grid, in_specs, out_specs, out_shape, scratch_shapes, semantics, args,
                 input_output_aliases=None):
    single = not isinstance(out_shape, (list, tuple))
    out_specs, out_shape = ([out_specs], [out_shape]) if single else (list(out_specs), list(out_shape))
    scratch, aliases = list(scratch_shapes), dict(input_output_aliases or {})
    if not exchange:
        outs = pl.pallas_call(body, name=name, grid=grid, in_specs=in_specs, out_specs=out_specs, out_shape=out_shape,
                              scratch_shapes=scratch, input_output_aliases=aliases, compiler_params=_params(*semantics))(*args)
        return list(outs), []
    n, n_in, n_out, n_scr = len(exchange), len(in_specs), len(out_specs), len(scratch)
    gather = [g for _, g in exchange]
    steps = math.prod(grid)

    def hosted(*refs):
        own_in, x_in = refs[:n_in], refs[n_in:n_in + n]
        own_out, x_out = refs[n_in + n:n_in + n + n_out], refs[n_in + n + n_out:n_in + 2 * n + n_out]
        own_scr, sems = refs[n_in + 2 * n + n_out:n_in + 2 * n + n_out + n_scr], refs[-3:]
        step = pl.program_id(0)
        for axis in range(1, len(grid)):
            step = step * grid[axis] + pl.program_id(axis)
        start, forward, finish = _exchange_steps(x_in, x_out, gather, *sems)
        pl.when(step == 0)(start)
        body(*own_in, *own_out, *own_scr)
        pl.when(step == steps // 2)(forward)
        pl.when(step == steps - 1)(finish)

    outs = pl.pallas_call(
        hosted, name=name, grid=grid, in_specs=list(in_specs) + [ANY] * n, out_specs=out_specs + [ANY] * n,
        out_shape=out_shape + _exchanged_shapes(exchange), scratch_shapes=scratch + _exchange_scratch(n),
        input_output_aliases=aliases, compiler_params=_params(*["arbitrary"] * len(grid)),
    )(*args, *[a for a, _ in exchange])
    return list(outs[:n_out]), list(outs[n_out:])


def _exchange_now(exchange, name):
    n = len(exchange)
    gather = [g for _, g in exchange]

    def body(*refs):
        for step in _exchange_steps(refs[:n], refs[n:2 * n], gather, *refs[2 * n:]):
            step()

    return pl.pallas_call(
        body, name=name, in_specs=[ANY] * n, out_specs=[ANY] * n, out_shape=_exchanged_shapes(exchange),
        scratch_shapes=_exchange_scratch(n),
    )(*[a for a, _ in exchange])


def _all_reduce_small(v):
    def body(v_ref, o_ref, recv_ref, send_sems, recv_sems):
        x, y, c = _place()
        me = 4 * x + 2 * y + c
        copies = []
        for k in range(1, N_DEV):
            peer = me ^ k
            copies.append(pltpu.make_async_remote_copy(
                src_ref=v_ref, dst_ref=recv_ref.at[k], send_sem=send_sems.at[k - 1], recv_sem=recv_sems.at[k - 1],
                device_id=(peer // 4, (peer // 2) % 2, peer % 2), device_id_type=MESH))
        for cp in copies:
            cp.start()
        recv_ref[0] = v_ref[...]
        for cp in copies:
            cp.wait()
        acc = recv_ref[me]
        for src in range(1, N_DEV):
            acc = acc + recv_ref[me ^ src]
        o_ref[...] = acc

    vm = pl.BlockSpec(memory_space=pltpu.VMEM)
    return pl.pallas_call(
        body, name="all_reduce_small", in_specs=[vm], out_specs=vm, out_shape=jax.ShapeDtypeStruct(v.shape, F32),
        scratch_shapes=[pltpu.VMEM((N_DEV,) + v.shape, F32), pltpu.SemaphoreType.DMA((N_DEV - 1,)),
                        pltpu.SemaphoreType.DMA((N_DEV - 1,))],
    )(v)


def _adamw_math(w, g, m, v):
    m = ADAM_B1 * m + (1.0 - ADAM_B1) * g
    v = ADAM_B2 * v + (1.0 - ADAM_B2) * (g * g)
    m_hat = m / (1.0 - ADAM_B1 ** ADAM_STEP)
    v_hat = v / (1.0 - ADAM_B2 ** ADAM_STEP)
    delta = -ADAM_LR * (m_hat / (jnp.sqrt(v_hat) + ADAM_EPS) + ADAM_WD * w)
    return delta, m, v


def _adamw(parts, w, m, v, name, layer=None, into=None):
    r, c = w.shape[-2:]
    tr = r // 2 if r % 16 == 0 and r >= 256 else r
    n = len(parts)

    def body(*refs):
        w_ref, m_ref, v_ref = refs[n:n + 3]
        g_ref, d_ref, nm_ref, nv_ref = refs[-4:]
        g = refs[0][...].astype(F32)
        for p_ref in refs[1:n]:
            g = g + p_ref[...].astype(F32)
        g_ref[...] = g
        d_ref[...], nm_ref[...], nv_ref[...] = _adamw_math(w_ref[...], g, m_ref[...], v_ref[...])

    def slab(slot):
        return pl.BlockSpec((None, tr, c), lambda i: (slot, i, 0))

    tile = pl.BlockSpec((tr, c), lambda i: (i, 0)) if layer is None else slab(layer)
    arrays, in_specs = [], []
    for p in parts:
        if isinstance(p, tuple):
            arrays.append(p[0])
            in_specs.append(slab(p[1]))
        else:
            arrays.append(p)
            in_specs.append(tile)
    kept = list(into) if into is not None else []
    return pl.pallas_call(
        body, name=name, grid=(r // tr,), in_specs=in_specs + [tile] * 3 + [ANY] * len(kept), out_specs=[tile] * 4,
        out_shape=[jax.ShapeDtypeStruct(w.shape, F32)] * 4,
        input_output_aliases={n + 3 + k: k for k in range(len(kept))}, compiler_params=_params("parallel"),
    )(*arrays, w, m, v, *kept)


def _rows(g):
    return g.reshape(-1, g.shape[-1])


def _row_blocks(dw):
    k, n = dw.shape
    return dw.reshape(N_DEV, k // N_DEV, n)


def _pack_rows(rows, width):
    out = None
    for i, r in enumerate(rows):
        r = r.reshape(1, -1).astype(F32)
        r = jnp.pad(r, ((i, 8 - 1 - i), (0, width - r.shape[1])))
        out = r if out is None else out + r
    return out


def _mixer_fwd(x, gain, w_in, w_out, cos, sin, seq, groups, tag, sink=None, exchanges=None):
    qkvs, os, lses, got = [], [], [], {}
    for gi, (dil, w) in enumerate(groups):
        qkv, got["proj", gi] = _qkv_proj(x, gain, w_in, _tables_tiled(cos, seq, dil), _tables_tiled(sin, seq, dil), seq, dil, gi,
                                         f"{tag}{gi}", exchange=(exchanges or {}).get(("proj", gi), ()))
        o, lse, got[gi] = _attn_fwd(qkv, w, f"{tag}{gi}", sink=sink, exchange=(exchanges or {}).get(gi, ()))
        qkvs.append(qkv)
        os.append(o)
        lses.append(lse)
    y, o, lses = _out_proj(x, os, lses, [dl for dl, _ in groups], w_out, seq, tag)
    return y, (qkvs, o, lses), got


def _mixer_bwd(dy, x_in, gain, w_in, w_out, saved, cos, sin, seq, groups, tag, sink=None, exchanges=None, scatter_dw_out=False):
    qkvs, o, lses = saved
    t, d = x_in.shape
    dils = [dl for dl, _ in groups]
    lse_tokens = lses[0].reshape(t, LANES) if sink is not None else None
    dw_out = _tn_matmul(o, dy, f"dw_out_{tag}")
    dos, dls, dsink, early = _attn_out_bwd(dy, w_out, o, dils, seq, tag, lse=lse_tokens, sink=sink,
                                           exchange=_to_send([dw_out]) if scatter_dw_out else ())
    if scatter_dw_out:
        (dw_out,) = early
    exchanges = exchanges or {}
    dps, got = [], {}
    for gi, (dil, w) in enumerate(groups):
        dp, got[gi] = _attn_bwd(qkvs[gi], dos[gi], lses[gi], dls[gi], _tables_by_residue(cos, seq, dil),
                                _tables_by_residue(sin, seq, dil), w, f"{tag}{gi}", exchange=exchanges.get(gi, ()))
        dps.append(dp)
    dx, hs, dgain, _ = _qkv_bwd(dy, x_in, gain, w_in, dps, dils, seq, tag)
    dw_in = None
    for gi in range(len(groups)):
        dw_in = _tn_matmul(dps[gi].reshape(t, QKV_W), hs[gi].reshape(t, d), f"dw_in_{tag}{gi}", into=dw_in, row_block=gi,
                           row_blocks=len(groups))
    return dx, dw_in, dw_out, dgain, dsink, got


def _ffn_layer_bwd(dy, x_in, gain, g, u, wg, wu, wd, tag, exchange=()):
    dx, dg, du, act, h, dgain, got = _ffn_bwd(dy, x_in, gain, g, u, wg, wu, wd, tag, exchange=exchange)
    dwd = _tn_matmul(act, dy, f"dw_down_{tag}")
    dwg = _tn_matmul(dg, h, f"dw_gate_{tag}")
    dwu = _tn_matmul(du, h, f"dw_up_{tag}")
    return dx, dwg, dwu, dwd, dgain, got


def _to_send(dws):
    return [(_row_blocks(g), False) for g in dws]


def kernel(x, a_w_in, a_sink, a_w_out, b_w_in, b_w_out, norm_mix, norm_ffn, w_gate, w_up, w_down, final_norm, loss_target, m_a_w_in, m_a_sink, m_a_w_out, m_b_w_in, m_b_w_out, m_norm_mix, m_norm_ffn, m_w_gate, m_w_up, m_w_down, m_final_norm, v_a_w_in, v_a_sink, v_a_w_out, v_b_w_in, v_b_w_out, v_norm_mix, v_norm_ffn, v_w_gate, v_w_up, v_w_down, v_final_norm):
    bl, seq, d = x.shape
    t = bl * seq
    xf = x.reshape(t, d)
    target = loss_target.reshape(t, d)
    cos, sin = _rope_tables(seq)
    groups_a = [(1, ATTN_HALF_WINDOW)]
    groups_b = [(dil, window // 2 // dil) for window, dil in DILATED_GROUPS]

    def flip(w_):
        return jnp.swapaxes(w_, -1, -2)

    a_w_in, m_a_w_in, v_a_w_in, b_w_in, m_b_w_in, v_b_w_in = map(flip, (a_w_in, m_a_w_in, v_a_w_in, b_w_in, m_b_w_in, v_b_w_in))
    w_gate, m_w_gate, v_w_gate, w_up, m_w_up, v_w_up = map(flip, (w_gate, m_w_gate, v_w_gate, w_up, m_w_up, v_w_up))

    def shard(w_, layer):
        return (w_[layer].astype(BF16), True)

    wa_in, wa_out = map(_rows, _exchange_now([shard(a_w_in, 0), shard(a_w_out, 0)], "gather_first"))

    x1_0, saved_a, got = _mixer_fwd(xf, norm_mix[0:1], wa_in, wa_out, cos, sin, seq, groups_a, "a", sink=a_sink[0],
                                    exchanges={("proj", 0): [shard(w_down, 0)], 0: [shard(w_gate, 0), shard(w_up, 0)]})
    wg0, wu0, wd0 = map(_rows, got[0] + got["proj", 0])
    x2_0, g0, u0, got = _ffn_fwd(x1_0, norm_ffn[0:1], wg0, wu0, wd0, "0", exchange=[shard(b_w_in, 0), shard(b_w_out, 0)])
    wb_in, wb_out = map(_rows, got)
    x1_1, saved_b, got = _mixer_fwd(x2_0, norm_mix[1:2], wb_in, wb_out, cos, sin, seq, groups_b, "b",
                                    exchanges={0: [shard(w_gate, 1)], 1: [shard(w_up, 1)], 2: [shard(w_down, 1)]})
    wg1, wu1, wd1 = map(_rows, got[0] + got[1] + got[2])
    dy, g1, u1, loss_part, d_final, _ = _ffn_fwd(x1_1, norm_ffn[1:2], wg1, wu1, wd1, "1",
                                                 loss_head=(final_norm.reshape(1, d), target))

    dy, dwg1, dwu1, dwd1, d_nf1, _ = _ffn_layer_bwd(dy, x1_1, norm_ffn[1:2], g1, u1, wg1, wu1, wd1, "1")
    dy, dwb_in, dwb_out, d_nm1, _, got = _mixer_bwd(
        dy, x2_0, norm_mix[1:2], wb_in, wb_out, saved_b, cos, sin, seq, groups_b, "b",
        exchanges={0: _to_send([dwg1, dwd1]), 1: _to_send([dwu1])})
    (r_g1, r_d1), (r_u1,) = got[0], got[1]
    dy, dwg0, dwu0, dwd0, d_nf0, (r_b_in, r_b_out) = _ffn_layer_bwd(
        dy, x1_0, norm_ffn[0:1], g0, u0, wg0, wu0, wd0, "0", exchange=_to_send([dwb_in, dwb_out]))
    dy, dwa_in, r_a_out, d_nm0, d_sink, got = _mixer_bwd(
        dy, xf, norm_mix[0:1], wa_in, wa_out, saved_a, cos, sin, seq, groups_a, "a", sink=a_sink[0],
        exchanges={0: _to_send([dwg0, dwu0, dwd0])}, scatter_dw_out=True)
    r_g0, r_u0, r_d0 = got[0]
    (r_a_in,) = _exchange_now(_to_send([dwa_in]), "scatter_last")
    grad_x = dy.reshape(bl, seq, d)

    def update(received, w_, m_, v_, name):
        out = None
        for layer in reversed(range(len(received))):
            out = _adamw([(received[layer], src) for src in range(N_DEV)], w_, m_, v_, f"adamw_{name}{layer}", layer=layer, into=out)
        return out

    u_a_in = update([r_a_in], a_w_in, m_a_w_in, v_a_w_in, "a_in")
    u_a_out = update([r_a_out], a_w_out, m_a_w_out, v_a_w_out, "a_out")
    u_b_in = update([r_b_in], b_w_in, m_b_w_in, v_b_w_in, "b_in")
    u_b_out = update([r_b_out], b_w_out, m_b_w_out, v_b_w_out, "b_out")
    u_gate = update([r_g0, r_g1], w_gate, m_w_gate, v_w_gate, "gate")
    u_up = update([r_u0, r_u1], w_up, m_w_up, v_w_up, "up")
    u_down = update([r_d0, r_d1], w_down, m_w_down, v_w_down, "down")

    small = _pack_rows([d_nm0, d_nm1, d_nf0, d_nf1, d_final, d_sink, loss_part], d)
    total = _all_reduce_small(small)
    small_w = _pack_rows([norm_mix[0], norm_mix[1], norm_ffn[0], norm_ffn[1], final_norm, a_sink], d)
    small_m = _pack_rows([m_norm_mix[0], m_norm_mix[1], m_norm_ffn[0], m_norm_ffn[1], m_final_norm, m_a_sink], d)
    small_v = _pack_rows([v_norm_mix[0], v_norm_mix[1], v_norm_ffn[0], v_norm_ffn[1], v_final_norm, v_a_sink], d)
    u_small = _adamw([total], small_w, small_m, small_v, "adamw_small")
    loss = total[6, 0]

    outs = []
    for k in range(4):
        sm = u_small[k]
        outs += [flip(u_a_in[k]), sm[5:6, :N_HEADS], u_a_out[k], flip(u_b_in[k]), u_b_out[k], sm[0:2], sm[2:4],
                 flip(u_gate[k]), flip(u_up[k]), u_down[k], sm[4]]
    return (loss, grad_x, *outs)
```

```python
import functools
import math

import jax
import jax.numpy as jnp
from jax import lax
from jax.experimental import pallas as pl
from jax.experimental.pallas import tpu as pltpu

F32 = jnp.float32
BF16 = jnp.bfloat16

HEAD_DIM = 64
N_HEADS = 16
N_KV = 4
GRP = N_HEADS // N_KV
Q_W = N_HEADS * HEAD_DIM
KV_W = N_KV * HEAD_DIM
QKV_W = Q_W + 2 * KV_W
ATTN_HALF_WINDOW = 128
DILATED_GROUPS = ((128, 1), (512, 4), (2048, 16))
ROPE_THETA = 10000.0
RMS_EPS = 1e-6
NEG_INF = -1e30
SCALE = 1.0 / math.sqrt(HEAD_DIM)

ADAM_LR = 0.001
ADAM_B1 = 0.9
ADAM_B2 = 0.999
ADAM_EPS = 1e-08
ADAM_WD = 0.01
ADAM_STEP = 10

LANES = 128
VMEM_LIMIT = 56 * 1024 * 1024
QUERY_BLOCK = 128
N_DEV = 8
MESH = pl.DeviceIdType.MESH

NT = (((1,), (1,)), ((), ()))
TN = (((0,), (0,)), ((), ()))


def _params(*sem):
    return pltpu.CompilerParams(dimension_semantics=tuple(sem) if sem else None, vmem_limit_bytes=VMEM_LIMIT)


def _resident(shape):
    return pl.BlockSpec(shape, lambda *_: (0,) * len(shape), pipeline_mode=pl.Buffered(1))


def _rope_tables(seq):
    inv_freq = 1.0 / (ROPE_THETA ** (jnp.arange(0, HEAD_DIM, 2, dtype=F32) / HEAD_DIM))
    ang = jnp.arange(seq, dtype=F32)[:, None] * inv_freq[None, :]
    cos, sin = jnp.cos(ang), jnp.sin(ang)
    return jnp.tile(cos, (1, 4)), jnp.concatenate([-sin, sin, -sin, sin], axis=1)


def _rope(t, cos, sin_signed):
    lane = lax.broadcasted_iota(jnp.int32, t.shape, 1)
    first = (lane & (HEAD_DIM // 2)) == 0
    swapped = jnp.where(first, pltpu.roll(t, LANES - HEAD_DIM // 2, 1), pltpu.roll(t, HEAD_DIM // 2, 1))
    return t * cos + swapped * sin_signed


def _rms(x):
    return lax.rsqrt(jnp.mean(x * x, axis=-1, keepdims=True) + RMS_EPS)


def _rms_bwd(dh, x, gain):
    r = _rms(x)
    xhat = x * r
    dxh = dh * gain
    dx = r * (dxh - xhat * jnp.mean(dxh * xhat, axis=-1, keepdims=True))
    return dx, xhat


def _accumulate(ref, value, first):
    @pl.when(first)
    def _():
        ref[...] = jnp.zeros_like(ref)

    ref[...] += value


def _tile_rows(seq):
    return min(512, seq)


def _res_shape(bl, seq, dil, c):
    ts = _tile_rows(seq)
    return (bl, dil, seq // ts, ts // dil, c)


def _res_spec(seq, dil, c):
    ts = _tile_rows(seq)
    per_seq = seq // ts
    return pl.BlockSpec((None, dil, None, ts // dil, c), lambda i: (i // per_seq, 0, i % per_seq, 0, 0))


def _seq_view(a):
    bl, dil, tiles, n, c = a.shape
    return a.reshape(bl * dil, tiles * n, c)


def _stage(ts, c):
    return pltpu.VMEM((c // LANES, ts, LANES), F32)


def _split_rows(val, stage_ref, dil):
    if dil == 1:
        return [val]
    ts, c = val.shape
    n, nc = ts // dil, c // LANES
    for k in range(nc):
        stage_ref[k] = val[:, k * LANES:(k + 1) * LANES]
    return [jnp.concatenate([stage_ref[k, pl.ds(r, n, stride=dil), :] for k in range(nc)], axis=1) for r in range(dil)]


def _merge_rows(parts, stage_ref, dil):
    if dil == 1:
        return parts[0]
    n, c = parts[0].shape
    nc = c // LANES
    for r, part in enumerate(parts):
        for k in range(nc):
            stage_ref[k, pl.ds(r, n, stride=dil), :] = part[:, k * LANES:(k + 1) * LANES]
    return jnp.concatenate([stage_ref[k] for k in range(nc)], axis=1)


def _tables_tiled(table, seq, dil):
    ts = _tile_rows(seq)
    return table.reshape(seq // ts, ts // dil, dil, LANES).transpose(0, 2, 1, 3).reshape(seq, LANES)


def _tables_by_residue(table, seq, dil):
    return table.reshape(seq // dil, dil, LANES).transpose(1, 0, 2)


def _qkv_proj(x, gain, w, cos, sin, seq, dil, group, tag, exchange=()):
    t, d = x.shape
    ts = _tile_rows(seq)
    n = ts // dil
    per_seq = seq // ts

    def body(x_ref, g_ref, w_ref, cos_ref, sin_ref, o_ref, h_ref, stage_ref):
        xv = jnp.concatenate(_split_rows(x_ref[...], stage_ref, dil), axis=0)
        h = (xv * _rms(xv) * g_ref[...]).astype(BF16)
        for r in range(dil):
            h_ref[r] = h[r * n:(r + 1) * n]
        acc = lax.dot_general(h, w_ref[...], NT, preferred_element_type=F32)
        c, s = cos_ref[...], sin_ref[...]
        for j in range(QKV_W // LANES):
            cols = slice(j * LANES, (j + 1) * LANES)
            val = acc[:, cols]
            if j < (Q_W + KV_W) // LANES:
                val = _rope(val, c, s)
            if j < Q_W // LANES:
                val = val * SCALE
            val = val.astype(BF16)
            for r in range(dil):
                o_ref[r, :, cols] = val[r * n:(r + 1) * n]

    table = pl.BlockSpec((ts, LANES), lambda i: (i % per_seq, 0))
    (qkv, h), exchanged = _hosted_call(
        body, exchange, name=f"qkv_proj_{tag}", grid=(t // ts,),
        in_specs=[pl.BlockSpec((ts, d), lambda i: (i, 0)), pl.BlockSpec((1, d), lambda i: (0, 0)),
                  pl.BlockSpec((QKV_W, d), lambda i: (group, 0)), table, table],
        out_specs=[_res_spec(seq, dil, QKV_W), _res_spec(seq, dil, d)],
        out_shape=[jax.ShapeDtypeStruct(_res_shape(t // seq, seq, dil, QKV_W), BF16),
                   jax.ShapeDtypeStruct(_res_shape(t // seq, seq, dil, d), BF16)],
        scratch_shapes=[_stage(ts, d)], semantics=("parallel",), args=(x, gain, w, cos, sin))
    return qkv, h, exchanged


def _band(bq, wk):
    return lax.broadcasted_iota(jnp.int32, (bq, wk), 0) - lax.broadcasted_iota(jnp.int32, (bq, wk), 1)


def _swap_halves(src_ref, base, dst_ref):
    for c in range(KV_W // LANES):
        dst_ref[c] = pltpu.roll(src_ref[:, base + c * LANES:base + (c + 1) * LANES], HEAD_DIM, 1)


def _pair_operand(src_ref, swapped_ref, base, kv, rows):
    c = kv // 2
    chunk, swapped = src_ref[rows, base + c * LANES:base + (c + 1) * LANES], swapped_ref[c, rows, :]
    lo = lax.broadcasted_iota(jnp.int32, chunk.shape, 1) < HEAD_DIM
    zero = jnp.zeros_like(chunk)
    if kv % 2 == 0:
        return jnp.concatenate([jnp.where(lo, chunk, zero), jnp.where(lo, zero, swapped)], axis=0)
    return jnp.concatenate([jnp.where(lo, swapped, zero), jnp.where(lo, zero, chunk)], axis=0)


def _over_keys(col, wk):
    if wk % LANES:
        return jnp.broadcast_to(col, (col.shape[0], wk))
    wide = jnp.broadcast_to(col, (col.shape[0], LANES))
    return wide if wk == LANES else jnp.concatenate([wide] * (wk // LANES), axis=1)


def _key_rows(bq, w, length):
    return min(bq + 2 * w, length)


def _window(i, bq, w, wk, length):
    q0 = pl.multiple_of(i * bq, bq)
    k0 = pl.multiple_of(jnp.clip(q0 - w, 0, length - wk), min(w, bq))
    return q0, k0


def _attn_fwd(qkv, w, tag, sink=None, exchange=()):
    shape = qkv.shape
    rows_all = _seq_view(qkv)
    nseq, length, _ = rows_all.shape
    bq = min(QUERY_BLOCK, length)
    wk = _key_rows(bq, w, length)
    nb = length // bq
    has_sink = sink is not None

    def body(*refs):
        qkv_ref = refs[0]
        sink_ref = refs[1] if has_sink else None
        o_ref, lse_ref, kk_ref, vv_ref = refs[-4:]
        _swap_halves(qkv_ref, Q_W, kk_ref)
        _swap_halves(qkv_ref, Q_W + KV_W, vv_ref)
        band = _band(bq, wk)
        lane = lax.broadcasted_iota(jnp.int32, (bq, LANES), 1)
        lo = lane < HEAD_DIM

        def block(i, carry):
            q0, k0 = _window(i, bq, w, wk, length)
            valid = jnp.abs(band + (q0 - k0)) <= w
            rows, krows = pl.ds(q0, bq), pl.ds(k0, wk)
            lse_tile = jnp.zeros((bq, LANES), F32)
            for kv in range(N_KV):
                heads = [(kv * GRP + h, h % 2) for h in range(GRP)]
                qp = [qkv_ref[rows, (kv * 2 + j) * LANES:(kv * 2 + j + 1) * LANES] for j in range(GRP // 2)]
                k2 = _pair_operand(qkv_ref, kk_ref, Q_W, kv, krows)
                v2 = _pair_operand(qkv_ref, vv_ref, Q_W + KV_W, kv, krows)
                sc2 = [lax.dot_general(q_, k2, NT, preferred_element_type=F32) for q_ in qp]
                sc = [jnp.where(valid, s_[:, half * wk:(half + 1) * wk], NEG_INF) for s_ in sc2 for half in range(2)]
                m = [jnp.max(s_, axis=-1, keepdims=True) for s_ in sc]
                if has_sink:
                    m = [jnp.maximum(m_, sink_ref[hd]) for m_, (hd, _) in zip(m, heads)]
                mb = [jnp.broadcast_to(m_, (bq, LANES)) for m_ in m]
                p = [jnp.exp(s_ - _over_keys(m_, wk)) for s_, m_ in zip(sc, m)]
                den = [jnp.sum(p_, axis=-1, keepdims=True) for p_ in p]
                if has_sink:
                    den = [d_ + jnp.exp(sink_ref[hd] - m_) for d_, m_, (hd, _) in zip(den, m, heads)]
                inv = [jnp.broadcast_to(1.0 / d_, (bq, LANES)) for d_ in den]
                pb = [p_.astype(BF16) for p_ in p]
                for j in range(GRP // 2):
                    o = jnp.dot(jnp.concatenate([pb[2 * j], pb[2 * j + 1]], axis=1), v2, preferred_element_type=F32)
                    o = o * jnp.where(lo, inv[2 * j], inv[2 * j + 1])
                    o_ref[rows, (kv * 2 + j) * LANES:(kv * 2 + j + 1) * LANES] = o.astype(BF16)
                for h, (hd, _) in enumerate(heads):
                    lse_tile = jnp.where(lane == hd, mb[h] - jnp.log(inv[h]), lse_tile)
            lse_ref[rows, :] = lse_tile
            return carry

        lax.fori_loop(0, nb, block, 0)

    args = [rows_all]
    in_specs = [pl.BlockSpec((None, length, QKV_W), lambda i: (i, 0, 0))]
    if has_sink:
        args.append(sink)
        in_specs.append(pl.BlockSpec(memory_space=pltpu.SMEM))
    (o, lse), exchanged = _hosted_call(
        body, exchange, name=f"attn_fwd_{tag}", grid=(nseq,), in_specs=in_specs,
        out_specs=[pl.BlockSpec((None, length, Q_W), lambda i: (i, 0, 0)), pl.BlockSpec((None, length, LANES), lambda i: (i, 0, 0))],
        out_shape=[jax.ShapeDtypeStruct((nseq, length, Q_W), BF16), jax.ShapeDtypeStruct((nseq, length, LANES), F32)],
        scratch_shapes=[pltpu.VMEM((KV_W // LANES, length, LANES), BF16), pltpu.VMEM((KV_W // LANES, length, LANES), BF16)],
        semantics=("parallel",), args=args)
    return o.reshape(shape[:-1] + (Q_W,)), lse.reshape(shape[:-1] + (LANES,)), exchanged


def _head_expand():
    return (jnp.arange(LANES)[:, None] == jnp.arange(Q_W)[None, :] // HEAD_DIM).astype(BF16)


def _out_proj(x, os, lses, dils, w, seq, tag):
    t, d = x.shape
    ts = _tile_rows(seq)
    ng = len(os)
    bl = t // seq
    if ng == 1:
        def body1(x_ref, o_ref, w_ref, y_ref):
            y_ref[...] = x_ref[...] + jnp.dot(o_ref[...], w_ref[...], preferred_element_type=F32)

        row = pl.BlockSpec((ts, d), lambda i: (i, 0))
        o = os[0].reshape(t, Q_W)
        y = pl.pallas_call(
            body1, name=f"out_proj_{tag}", grid=(t // ts,), in_specs=[row, row, _resident(w.shape)], out_specs=row,
            out_shape=jax.ShapeDtypeStruct((t, d), F32), compiler_params=_params("parallel"),
        )(x, o, w)
        return y, o, [lses[0]]

    def body(*refs):
        x_ref, w_ref, e_ref = refs[:3]
        o_refs, l_refs = refs[3:3 + ng], refs[3 + ng:3 + 2 * ng]
        y_ref, om_ref = refs[3 + 2 * ng:5 + 2 * ng]
        lt_refs = refs[5 + 2 * ng:5 + 3 * ng]
        wide_ref, narrow_ref = refs[5 + 3 * ng:]
        ls = [_merge_rows([l_refs[g][r] for r in range(dils[g])], narrow_ref, dils[g]) for g in range(ng)]
        mx = functools.reduce(jnp.maximum, ls)
        tot = mx + jnp.log(functools.reduce(lambda a, b: a + b, [jnp.exp(l_ - mx) for l_ in ls]))
        e = e_ref[...]
        o = None
        for g in range(ng):
            wt = jnp.exp(ls[g] - tot)
            hi = wt.astype(BF16)
            lo = (wt - hi.astype(F32)).astype(BF16)
            wide = jnp.dot(hi, e, preferred_element_type=F32) + jnp.dot(lo, e, preferred_element_type=F32)
            term = wide * _merge_rows([o_refs[g][r].astype(F32) for r in range(dils[g])], wide_ref, dils[g])
            o = term if o is None else o + term
        ob = o.astype(BF16)
        om_ref[...] = ob
        y_ref[...] = x_ref[...] + jnp.dot(ob, w_ref[...], preferred_element_type=F32)
        for g in range(ng):
            for r, part in enumerate(_split_rows(tot, narrow_ref, dils[g])):
                lt_refs[g][r] = part

    row = pl.BlockSpec((ts, d), lambda i: (i, 0))
    e = _head_expand()
    outs = pl.pallas_call(
        body, name=f"out_proj_{tag}", grid=(t // ts,),
        in_specs=[row, _resident(w.shape), _resident(e.shape)] + [_res_spec(seq, dl, Q_W) for dl in dils]
                 + [_res_spec(seq, dl, LANES) for dl in dils],
        out_specs=[row, pl.BlockSpec((ts, Q_W), lambda i: (i, 0))] + [_res_spec(seq, dl, LANES) for dl in dils],
        out_shape=[jax.ShapeDtypeStruct((t, d), F32), jax.ShapeDtypeStruct((t, Q_W), BF16)]
                  + [jax.ShapeDtypeStruct(_res_shape(bl, seq, dl, LANES), F32) for dl in dils],
        scratch_shapes=[_stage(ts, Q_W), _stage(ts, LANES)],
        compiler_params=_params("parallel"),
    )(x, w, e, *os, *lses)
    return outs[0], outs[1], list(outs[2:])


def _sigmoid(g):
    return 1.0 / (1.0 + jnp.exp(-g))


def _ffn_fwd(x, gain, wg, wu, wd, tag, exchange=(), loss_head=None):
    t, d = x.shape
    f = wd.shape[0]
    tm = min(256, t)
    has_loss = loss_head is not None

    def body(*refs):
        x_ref, gain_ref, wg_ref, wu_ref, wd_ref = refs[:5]
        y_ref, g_ref, u_ref, a_ref, h_ref = refs[-7:-2] if has_loss else refs[-5:]
        xv = x_ref[...]
        h = (xv * _rms(xv) * gain_ref[...]).astype(BF16)
        h_ref[...] = h
        g = lax.dot_general(h, wg_ref[...], NT, preferred_element_type=F32)
        u = lax.dot_general(h, wu_ref[...], NT, preferred_element_type=F32)
        g_ref[...] = g.astype(BF16)
        u_ref[...] = u.astype(BF16)
        a = (g * _sigmoid(g) * u).astype(BF16)
        a_ref[...] = a
        y = xv + jnp.dot(a, wd_ref[...], preferred_element_type=F32)
        if not has_loss:
            y_ref[...] = y
            return
        head_ref, target_ref, loss_ref, dhead_ref = refs[5], refs[6], refs[-2], refs[-1]
        head = head_ref[...]
        yhat = y * _rms(y)
        err = yhat * head - target_ref[...]
        dout = err * (1.0 / d)
        y_ref[...] = _rms_bwd(dout, y, head)[0]
        first = pl.program_id(0) == 0
        part = 0.5 * jnp.sum(jnp.mean(err * err, axis=-1, keepdims=True), axis=0, keepdims=True)
        _accumulate(loss_ref, jnp.broadcast_to(part, loss_ref.shape), first)
        _accumulate(dhead_ref, jnp.sum(dout * yhat, axis=0, keepdims=True), first)

    row = pl.BlockSpec((tm, d), lambda i: (i, 0))
    wide = pl.BlockSpec((tm, f), lambda i: (i, 0))
    in_specs = [row, _resident((1, d)), _resident(wg.shape), _resident(wu.shape), _resident(wd.shape)]
    out_specs = [row, wide, wide, wide, row]
    out_shape = [jax.ShapeDtypeStruct((t, d), F32)] + [jax.ShapeDtypeStruct((t, f), BF16)] * 3 + [jax.ShapeDtypeStruct((t, d), BF16)]
    if has_loss:
        in_specs += [_resident((1, d)), row]
        out_specs += [pl.BlockSpec((1, LANES), lambda i: (0, 0)), pl.BlockSpec((1, d), lambda i: (0, 0))]
        out_shape += [jax.ShapeDtypeStruct((1, LANES), F32), jax.ShapeDtypeStruct((1, d), F32)]
    outs, exchanged = _hosted_call(
        body, exchange, name=f"ffn_fwd_{tag}", grid=(t // tm,), in_specs=in_specs, out_specs=out_specs, out_shape=out_shape,
        scratch_shapes=[], semantics=("arbitrary" if has_loss else "parallel",),
        args=(x, gain, wg, wu, wd) + (tuple(loss_head) if has_loss else ()))
    return (*outs, exchanged)


def _ffn_bwd(dy, x, gain, g, u, wg, wu, wd, tag, exchange=()):
    t, d = x.shape
    f = wd.shape[0]
    tm = min(256, t)

    def body(dy_ref, x_ref, gain_ref, g_ref, u_ref, wg_ref, wu_ref, wd_ref, dx_ref, dg_ref, du_ref, dgain_ref):
        dyv = dy_ref[...]
        da = lax.dot_general(dyv.astype(BF16), wd_ref[...], NT, preferred_element_type=F32)
        gv, uv = g_ref[...].astype(F32), u_ref[...].astype(F32)
        sg = _sigmoid(gv)
        act = gv * sg
        du = (da * act).astype(BF16)
        dg = (da * uv * (sg * (1.0 + gv * (1.0 - sg)))).astype(BF16)
        du_ref[...] = du
        dg_ref[...] = dg
        dh = jnp.dot(dg, wg_ref[...], preferred_element_type=F32) + jnp.dot(du, wu_ref[...], preferred_element_type=F32)
        xv, gain_v = x_ref[...], gain_ref[...]
        dx, xhat = _rms_bwd(dh, xv, gain_v)
        dx_ref[...] = dyv + dx
        _accumulate(dgain_ref, jnp.sum(dh * xhat, axis=0, keepdims=True), pl.program_id(0) == 0)

    row = pl.BlockSpec((tm, d), lambda i: (i, 0))
    wide = pl.BlockSpec((tm, f), lambda i: (i, 0))
    outs, exchanged = _hosted_call(
        body, exchange, name=f"ffn_bwd_{tag}", grid=(t // tm,),
        in_specs=[row, row, _resident((1, d)), wide, wide, _resident(wg.shape), _resident(wu.shape), _resident(wd.shape)],
        out_specs=[row, wide, wide, pl.BlockSpec((1, d), lambda i: (0, 0))],
        out_shape=[jax.ShapeDtypeStruct((t, d), F32), jax.ShapeDtypeStruct((t, f), BF16), jax.ShapeDtypeStruct((t, f), BF16),
                   jax.ShapeDtypeStruct((1, d), F32)],
        scratch_shapes=[], semantics=("arbitrary",), args=(dy, x, gain, g, u, wg, wu, wd))
    return (*outs, exchanged)


def _tn_matmul(a, b, name, into=None, row_block=0, row_blocks=1):
    t, k = a.shape
    n = b.shape[1]
    tk = k // 2 if (k // 2) % LANES == 0 else k
    tt = min(2048, t)
    first = row_block * (k // tk)

    def body(a_ref, b_ref, *rest):
        o_ref, acc_ref = rest[-2:]
        prod = lax.dot_general(a_ref[...].astype(BF16), b_ref[...].astype(BF16), TN, preferred_element_type=F32)
        j = pl.program_id(1)

        @pl.when(j == 0)
        def _():
            acc_ref[...] = prod

        @pl.when(j > 0)
        def _():
            acc_ref[...] += prod

        @pl.when(j == pl.num_programs(1) - 1)
        def _():
            o_ref[...] = acc_ref[...].astype(BF16)

    return pl.pallas_call(
        body, name=name, grid=(k // tk, t // tt),
        in_specs=[pl.BlockSpec((tt, tk), lambda i, j: (j, i)), pl.BlockSpec((tt, n), lambda i, j: (j, 0))]
                 + ([ANY] if into is not None else []),
        out_specs=pl.BlockSpec((tk, n), lambda i, j: (first + i, 0)),
        out_shape=jax.ShapeDtypeStruct((row_blocks * k, n), BF16),
        scratch_shapes=[pltpu.VMEM((tk, n), F32)],
        input_output_aliases={2: 0} if into is not None else {},
        compiler_params=_params("parallel", "arbitrary"),
    )(a, b, *([into] if into is not None else []))


def _attn_out_bwd(dx, w, o, dils, seq, tag, lse=None, sink=None, exchange=()):
    t, d = dx.shape
    ts = _tile_rows(seq)
    bl = t // seq
    ng = len(dils)
    has_sink = sink is not None
    expand = _head_expand().T

    def body(*refs):
        refs = list(refs)
        dx_ref, w_ref, o_ref, e_ref = refs[:4]
        refs = refs[4:]
        lse_ref, sink_ref = (refs.pop(0), refs.pop(0)) if has_sink else (None, None)
        do_refs, dl_refs = refs[:ng], refs[ng:2 * ng]
        refs = refs[2 * ng:]
        dsink_ref = refs.pop(0) if has_sink else None
        dof_ref, dlf_ref = refs
        do = lax.dot_general(dx_ref[...].astype(BF16), w_ref[...], NT, preferred_element_type=F32)
        prod = do * o_ref[...].astype(F32)
        hi = prod.astype(BF16)
        lo = (prod - hi.astype(F32)).astype(BF16)
        e = e_ref[...]
        dl = jnp.dot(hi, e, preferred_element_type=F32) + jnp.dot(lo, e, preferred_element_type=F32)
        for g in range(ng):
            for r, part in enumerate(_split_rows(do, dof_ref, dils[g])):
                do_refs[g][r] = part.astype(BF16)
            for r, part in enumerate(_split_rows(dl, dlf_ref, dils[g])):
                dl_refs[g][r] = part
        if has_sink:
            part = -jnp.exp(sink_ref[...] - lse_ref[...]) * dl
            _accumulate(dsink_ref, jnp.sum(part, axis=0, keepdims=True), pl.program_id(0) == 0)

    row = pl.BlockSpec((ts, d), lambda i: (i, 0))
    narrow = pl.BlockSpec((ts, LANES), lambda i: (i, 0))
    args = [dx, w, o, expand]
    in_specs = [row, _resident(w.shape), pl.BlockSpec((ts, Q_W), lambda i: (i, 0)), _resident(expand.shape)]
    if has_sink:
        args += [lse, jnp.pad(sink.reshape(1, N_HEADS), ((0, 0), (0, LANES - N_HEADS)))]
        in_specs += [narrow, _resident((1, LANES))]
    out_specs = [_res_spec(seq, dl, Q_W) for dl in dils] + [_res_spec(seq, dl, LANES) for dl in dils]
    out_shape = ([jax.ShapeDtypeStruct(_res_shape(bl, seq, dl, Q_W), BF16) for dl in dils]
                 + [jax.ShapeDtypeStruct(_res_shape(bl, seq, dl, LANES), F32) for dl in dils])
    if has_sink:
        out_specs.append(pl.BlockSpec((1, LANES), lambda i: (0, 0)))
        out_shape.append(jax.ShapeDtypeStruct((1, LANES), F32))
    outs, exchanged = _hosted_call(
        body, exchange, name=f"attn_out_bwd_{tag}", grid=(t // ts,), in_specs=in_specs, out_specs=out_specs, out_shape=out_shape,
        scratch_shapes=[_stage(ts, Q_W), _stage(ts, LANES)], semantics=("arbitrary" if has_sink else "parallel",), args=args)
    return list(outs[:ng]), list(outs[ng:2 * ng]), (outs[2 * ng] if has_sink else None), exchanged


def _attn_bwd(qkv, do, lse, delta, cos, sin, w, tag, exchange=()):
    shape = qkv.shape
    dil = shape[1]
    rows_all = _seq_view(qkv)
    nseq, length, _ = rows_all.shape
    bq = min(QUERY_BLOCK, length)
    wk = _key_rows(bq, w, length)
    nb = length // bq

    def body(qkv_ref, do_ref, lse_ref, dl_ref, cos_ref, sin_ref, dp_ref, kk_ref, vv_ref, dk_ref, dv_ref):
        _swap_halves(qkv_ref, Q_W, kk_ref)
        _swap_halves(qkv_ref, Q_W + KV_W, vv_ref)
        dk_ref[...] = jnp.zeros_like(dk_ref)
        dv_ref[...] = jnp.zeros_like(dv_ref)
        band = _band(bq, wk)
        lo_q = lax.broadcasted_iota(jnp.int32, (bq, LANES), 1) < HEAD_DIM
        hi_q = jnp.logical_not(lo_q)

        def block(i, carry):
            q0, k0 = _window(i, bq, w, wk, length)
            valid = jnp.abs(band + (q0 - k0)) <= w
            rows, krows = pl.ds(q0, bq), pl.ds(k0, wk)
            c, sn = cos_ref[rows, :], -sin_ref[rows, :]
            lse_t, dl_t = lse_ref[rows, :], dl_ref[rows, :]
            for kv in range(N_KV):
                heads = [(kv * GRP + h, h % 2) for h in range(GRP)]
                cols = [slice((kv * 2 + j) * LANES, (kv * 2 + j + 1) * LANES) for j in range(GRP // 2)]
                qp = [qkv_ref[rows, cs] for cs in cols]
                dop = [do_ref[rows, cs] for cs in cols]
                k2 = _pair_operand(qkv_ref, kk_ref, Q_W, kv, krows)
                v2 = _pair_operand(qkv_ref, vv_ref, Q_W + KV_W, kv, krows)
                sc2 = [lax.dot_general(q_, k2, NT, preferred_element_type=F32) for q_ in qp]
                dp2 = [lax.dot_general(d_, v2, NT, preferred_element_type=F32) for d_ in dop]
                sc = [s_[:, half * wk:(half + 1) * wk] for s_ in sc2 for half in range(2)]
                dp = [d_[:, half * wk:(half + 1) * wk] for d_ in dp2 for half in range(2)]
                p = [jnp.exp(jnp.where(valid, s_, NEG_INF) - _over_keys(lse_t[:, hd:hd + 1], wk))
                     for s_, (hd, _) in zip(sc, heads)]
                ds = [(p_ * (dp_ - _over_keys(dl_t[:, hd:hd + 1], wk))).astype(BF16) for p_, dp_, (hd, _) in zip(p, dp, heads)]
                pb = [p_.astype(BF16) for p_ in p]
                for j in range(GRP // 2):
                    dq = jnp.dot(jnp.concatenate([ds[2 * j], ds[2 * j + 1]], axis=1), k2, preferred_element_type=F32) * SCALE
                    dp_ref[rows, cols[j]] = _rope(dq, c, sn).astype(BF16)
                zero = jnp.zeros((bq, LANES), BF16)
                q4 = jnp.concatenate([jnp.where(lo_q if h % 2 == 0 else hi_q, qp[h // 2], zero) for h in range(GRP)], axis=0)
                do4 = jnp.concatenate([jnp.where(lo_q if h % 2 == 0 else hi_q, dop[h // 2], zero) for h in range(GRP)], axis=0)
                dk_ref[kv, krows, :] += lax.dot_general(jnp.concatenate(ds, axis=0), q4, TN, preferred_element_type=F32)
                dv_ref[kv, krows, :] += lax.dot_general(jnp.concatenate(pb, axis=0), do4, TN, preferred_element_type=F32)
            return carry

        lax.fori_loop(0, nb, block, 0)
        lo = lax.broadcasted_iota(jnp.int32, (length, LANES), 1) < HEAD_DIM
        c, sn = cos_ref[...], -sin_ref[...]
        for ch in range(KV_W // LANES):
            halves = []
            for acc_ref in (dk_ref, dv_ref):
                even, odd = acc_ref[2 * ch], acc_ref[2 * ch + 1]
                even = even + pltpu.roll(even, HEAD_DIM, 1)
                odd = odd + pltpu.roll(odd, HEAD_DIM, 1)
                halves.append(jnp.where(lo, even, odd))
            dp_ref[:, Q_W + ch * LANES:Q_W + (ch + 1) * LANES] = _rope(halves[0], c, sn).astype(BF16)
            dp_ref[:, Q_W + KV_W + ch * LANES:Q_W + KV_W + (ch + 1) * LANES] = halves[1].astype(BF16)

    def seq_block(c):
        return pl.BlockSpec((None, length, c), lambda i: (i, 0, 0))

    table = pl.BlockSpec((None, length, LANES), lambda i: (i % dil, 0, 0))
    (out,), exchanged = _hosted_call(
        body, exchange, name=f"attn_bwd_{tag}", grid=(nseq,),
        in_specs=[seq_block(QKV_W), seq_block(Q_W), seq_block(LANES), seq_block(LANES), table, table],
        out_specs=[pl.BlockSpec((None, length, QKV_W), lambda i: (i, 0, 0))],
        out_shape=[jax.ShapeDtypeStruct((nseq, length, QKV_W), BF16)],
        scratch_shapes=[pltpu.VMEM((KV_W // LANES, length, LANES), BF16), pltpu.VMEM((KV_W // LANES, length, LANES), BF16),
                        pltpu.VMEM((N_KV, length, LANES), F32), pltpu.VMEM((N_KV, length, LANES), F32)],
        semantics=("parallel",), args=(rows_all, _seq_view(do), _seq_view(lse), _seq_view(delta), cos, sin))
    return out.reshape(shape), exchanged


def _qkv_bwd(dy, x, gain, w, dps, dils, seq, tag):
    t, d = x.shape
    ts = _tile_rows(seq)
    ng = len(dps)

    def body(dy_ref, x_ref, gain_ref, w_ref, *refs):
        dp_refs, (dx_ref, dgain_ref, stage_ref) = refs[:ng], refs[ng:]
        dh = None
        for gi in range(ng):
            dil = dils[gi]
            n = ts // dil
            dp = dp_refs[gi][0] if dil == 1 else jnp.concatenate([dp_refs[gi][r] for r in range(dil)], axis=0)
            part = jnp.dot(dp, w_ref[gi * QKV_W:(gi + 1) * QKV_W, :], preferred_element_type=F32)
            part = _merge_rows([part[r * n:(r + 1) * n] for r in range(dil)], stage_ref, dil)
            dh = part if dh is None else dh + part
        xv, gain_v = x_ref[...], gain_ref[...]
        dx, xhat = _rms_bwd(dh, xv, gain_v)
        dx_ref[...] = dy_ref[...] + dx
        _accumulate(dgain_ref, jnp.sum(dh * xhat, axis=0, keepdims=True), pl.program_id(0) == 0)

    row = pl.BlockSpec((ts, d), lambda i: (i, 0))
    return pl.pallas_call(
        body, name=f"qkv_bwd_{tag}", grid=(t // ts,),
        in_specs=[row, row, _resident((1, d)), _resident(w.shape)] + [_res_spec(seq, dl, QKV_W) for dl in dils],
        out_specs=[row, pl.BlockSpec((1, d), lambda i: (0, 0))],
        out_shape=[jax.ShapeDtypeStruct((t, d), F32), jax.ShapeDtypeStruct((1, d), F32)],
        scratch_shapes=[_stage(ts, d)], compiler_params=_params("arbitrary"),
    )(dy, x, gain, w, *dps)


ANY = pl.BlockSpec(memory_space=pl.ANY)


def _place():
    x, y, c = lax.axis_index("x"), lax.axis_index("y"), lax.axis_index("c")
    return x, y, c


def _exchange_steps(srcs, dsts, gather, send_sems, recv_sems, local_sems):
    x, y, c = _place()
    me, sibling = (x, y, c), (x, y, 1 - c)
    chips = [(1 - x, y), (x, 1 - y), (1 - x, 1 - y)]
    mine = 4 * x + 2 * y + c

    def slot(a, device):
        px, py, pc = device
        return dsts[a].at[4 * px + 2 * py + pc]

    def passes(a, k, block, to, src=None):
        rows = slot(a, block)
        return pltpu.make_async_remote_copy(src_ref=rows if src is None else src, dst_ref=rows, send_sem=send_sems.at[a, k],
                                            recv_sem=recv_sems.at[a, k], device_id=to, device_id_type=MESH)

    def scatters(a, k):
        peer = mine ^ k
        return pltpu.make_async_remote_copy(
            src_ref=srcs[a].at[peer], dst_ref=dsts[a].at[mine], send_sem=send_sems.at[a, k - 1], recv_sem=recv_sems.at[a, k - 1],
            device_id=(peer // 4, (peer // 2) % 2, peer % 2), device_id_type=MESH)

    def local(a):
        return pltpu.make_async_copy(srcs[a] if gather[a] else srcs[a].at[mine], dsts[a].at[mine], local_sems.at[a])

    def first_copies(a):
        if not gather[a]:
            return [scatters(a, k) for k in range(1, N_DEV)]
        return [passes(a, 0, me, sibling, src=srcs[a])] + [passes(a, 1 + j, me, (*chip, c), src=srcs[a]) for j, chip in enumerate(chips)]

    def start():
        for a in range(len(srcs)):
            local(a).start()
            for cp in first_copies(a):
                cp.start()

    def forward():
        for a in range(len(srcs)):
            if gather[a]:
                for j, chip in enumerate(chips):
                    passes(a, 1 + j, (*chip, c), me).wait_recv()
                    passes(a, 4 + j, (*chip, c), sibling).start()

    def finish():
        for a in range(len(srcs)):
            if gather[a]:
                passes(a, 0, sibling, me).wait_recv()
                for j, chip in enumerate(chips):
                    passes(a, 4 + j, (*chip, 1 - c), me).wait_recv()
                    passes(a, 4 + j, (*chip, c), sibling).wait_send()
                for cp in first_copies(a):
                    cp.wait_send()
            else:
                for cp in first_copies(a):
                    cp.wait()
            local(a).wait()

    return start, forward, finish


def _exchange_scratch(n):
    return [pltpu.SemaphoreType.DMA((n, N_DEV - 1)), pltpu.SemaphoreType.DMA((n, N_DEV - 1)), pltpu.SemaphoreType.DMA((n,))]


def _exchanged_shapes(exchange):
    return [jax.ShapeDtypeStruct(((N_DEV,) + a.shape) if g else a.shape, a.dtype) for a, g in exchange]


def _hosted_call(body, exchange, *, name, grid, in_specs, out_specs, out_shape, scratch_shapes, semantics, args,
                 input_output_aliases=None):
    single = not isinstance(out_shape, (list, tuple))
    out_specs, out_shape = ([out_specs], [out_shape]) if single else (list(out_specs), list(out_shape))
    scratch, aliases = list(scratch_shapes), dict(input_output_aliases or {})
    if not exchange:
        outs = pl.pallas_call(body, name=name, grid=grid, in_specs=in_specs, out_specs=out_specs, out_shape=out_shape,
                              scratch_shapes=scratch, input_output_aliases=aliases, compiler_params=_params(*semantics))(*args)
        return list(outs), []
    n, n_in, n_out, n_scr = len(exchange), len(in_specs), len(out_specs), len(scratch)
    gather = [g for _, g in exchange]
    steps = math.prod(grid)

    def hosted(*refs):
        own_in, x_in = refs[:n_in], refs[n_in:n_in + n]
        own_out, x_out = refs[n_in + n:n_in + n + n_out], refs[n_in + n + n_out:n_in + 2 * n + n_out]
        own_scr, sems = refs[n_in + 2 * n + n_out:n_in + 2 * n + n_out + n_scr], refs[-3:]
        step = pl.program_id(0)
        for axis in range(1, len(grid)):
            step = step * grid[axis] + pl.program_id(axis)
        start, forward, finish = _exchange_steps(x_in, x_out, gather, *sems)
        pl.when(step == 0)(start)
        body(*own_in, *own_out, *own_scr)
        pl.when(step == steps // 2)(forward)
        pl.when(step == steps - 1)(finish)

    outs = pl.pallas_call(
        hosted, name=name, grid=grid, in_specs=list(in_specs) + [ANY] * n, out_specs=out_specs + [ANY] * n,
        out_shape=out_shape + _exchanged_shapes(exchange), scratch_shapes=scratch + _exchange_scratch(n),
        input_output_aliases=aliases, compiler_params=_params(*["arbitrary"] * len(grid)),
    )(*args, *[a for a, _ in exchange])
    return list(outs[:n_out]), list(outs[n_out:])


def _exchange_now(exchange, name):
    n = len(exchange)
    gather = [g for _, g in exchange]

    def body(*refs):
        for step in _exchange_steps(refs[:n], refs[n:2 * n], gather, *refs[2 * n:]):
            step()

    return pl.pallas_call(
        body, name=name, in_specs=[ANY] * n, out_specs=[ANY] * n, out_shape=_exchanged_shapes(exchange),
        scratch_shapes=_exchange_scratch(n),
    )(*[a for a, _ in exchange])


def _all_reduce_small(v):
    def body(v_ref, o_ref, recv_ref, send_sems, recv_sems):
        x, y, c = _place()
        me = 4 * x + 2 * y + c
        copies = []
        for k in range(1, N_DEV):
            peer = me ^ k
            copies.append(pltpu.make_async_remote_copy(
                src_ref=v_ref, dst_ref=recv_ref.at[k], send_sem=send_sems.at[k - 1], recv_sem=recv_sems.at[k - 1],
                device_id=(peer // 4, (peer // 2) % 2, peer % 2), device_id_type=MESH))
        for cp in copies:
            cp.start()
        recv_ref[0] = v_ref[...]
        for cp in copies:
            cp.wait()
        acc = recv_ref[me]
        for src in range(1, N_DEV):
            acc = acc + recv_ref[me ^ src]
        o_ref[...] = acc

    vm = pl.BlockSpec(memory_space=pltpu.VMEM)
    return pl.pallas_call(
        body, name="all_reduce_small", in_specs=[vm], out_specs=vm, out_shape=jax.ShapeDtypeStruct(v.shape, F32),
        scratch_shapes=[pltpu.VMEM((N_DEV,) + v.shape, F32), pltpu.SemaphoreType.DMA((N_DEV - 1,)),
                        pltpu.SemaphoreType.DMA((N_DEV - 1,))],
    )(v)


def _adamw_math(w, g, m, v):
    m = ADAM_B1 * m + (1.0 - ADAM_B1) * g
    v = ADAM_B2 * v + (1.0 - ADAM_B2) * (g * g)
    m_hat = m / (1.0 - ADAM_B1 ** ADAM_STEP)
    v_hat = v / (1.0 - ADAM_B2 ** ADAM_STEP)
    delta = -ADAM_LR * (m_hat / (jnp.sqrt(v_hat) + ADAM_EPS) + ADAM_WD * w)
    return delta, m, v


def _adamw(parts, w, m, v, name, layer=None, into=None):
    r, c = w.shape[-2:]
    tr = r // 2 if r % 16 == 0 and r >= 256 else r
    n = len(parts)

    def body(*refs):
        w_ref, m_ref, v_ref = refs[n:n + 3]
        g_ref, d_ref, nm_ref, nv_ref = refs[-4:]
        g = refs[0][...].astype(F32)
        for p_ref in refs[1:n]:
            g = g + p_ref[...].astype(F32)
        g_ref[...] = g
        d_ref[...], nm_ref[...], nv_ref[...] = _adamw_math(w_ref[...], g, m_ref[...], v_ref[...])

    def slab(slot):
        return pl.BlockSpec((None, tr, c), lambda i: (slot, i, 0))

    tile = pl.BlockSpec((tr, c), lambda i: (i, 0)) if layer is None else slab(layer)
    arrays, in_specs = [], []
    for p in parts:
        if isinstance(p, tuple):
            arrays.append(p[0])
            in_specs.append(slab(p[1]))
        else:
            arrays.append(p)
            in_specs.append(tile)
    kept = list(into) if into is not None else []
    return pl.pallas_call(
        body, name=name, grid=(r // tr,), in_specs=in_specs + [tile] * 3 + [ANY] * len(kept), out_specs=[tile] * 4,
        out_shape=[jax.ShapeDtypeStruct(w.shape, F32)] * 4,
        input_output_aliases={n + 3 + k: k for k in range(len(kept))}, compiler_params=_params("parallel"),
    )(*arrays, w, m, v, *kept)


def _rows(g):
    return g.reshape(-1, g.shape[-1])


def _row_blocks(dw):
    k, n = dw.shape
    return dw.reshape(N_DEV, k // N_DEV, n)


def _pack_rows(rows, width):
    out = None
    for i, r in enumerate(rows):
        r = r.reshape(1, -1).astype(F32)
        r = jnp.pad(r, ((i, 8 - 1 - i), (0, width - r.shape[1])))
        out = r if out is None else out + r
    return out


def _mixer_fwd(x, gain, w_in, w_out, cos, sin, seq, groups, tag, sink=None, exchanges=None):
    qkvs, hs, os, lses, got = [], [], [], [], {}
    for gi, (dil, w) in enumerate(groups):
        qkv, h, got["proj", gi] = _qkv_proj(x, gain, w_in, _tables_tiled(cos, seq, dil), _tables_tiled(sin, seq, dil), seq, dil,
                                            gi, f"{tag}{gi}", exchange=(exchanges or {}).get(("proj", gi), ()))
        o, lse, got[gi] = _attn_fwd(qkv, w, f"{tag}{gi}", sink=sink, exchange=(exchanges or {}).get(gi, ()))
        qkvs.append(qkv)
        hs.append(h)
        os.append(o)
        lses.append(lse)
    y, o, lses = _out_proj(x, os, lses, [dl for dl, _ in groups], w_out, seq, tag)
    return y, (qkvs, hs, o, lses), got


def _mixer_bwd(dy, x_in, gain, w_in, w_out, saved, cos, sin, seq, groups, tag, sink=None, exchanges=None, scatter_dw_out=False):
    qkvs, hs, o, lses = saved
    t, d = x_in.shape
    dils = [dl for dl, _ in groups]
    lse_tokens = lses[0].reshape(t, LANES) if sink is not None else None
    dw_out = _tn_matmul(o, dy, f"dw_out_{tag}")
    dos, dls, dsink, early = _attn_out_bwd(dy, w_out, o, dils, seq, tag, lse=lse_tokens, sink=sink,
                                           exchange=_to_send([dw_out]) if scatter_dw_out else ())
    if scatter_dw_out:
        (dw_out,) = early
    exchanges = exchanges or {}
    dps, got = [], {}
    for gi, (dil, w) in enumerate(groups):
        dp, got[gi] = _attn_bwd(qkvs[gi], dos[gi], lses[gi], dls[gi], _tables_by_residue(cos, seq, dil),
                                _tables_by_residue(sin, seq, dil), w, f"{tag}{gi}", exchange=exchanges.get(gi, ()))
        dps.append(dp)
    dx, dgain = _qkv_bwd(dy, x_in, gain, w_in, dps, dils, seq, tag)
    dw_in = None
    for gi in range(len(groups)):
        dw_in = _tn_matmul(dps[gi].reshape(t, QKV_W), hs[gi].reshape(t, d), f"dw_in_{tag}{gi}", into=dw_in, row_block=gi,
                           row_blocks=len(groups))
    return dx, dw_in, dw_out, dgain, dsink, got


def _ffn_layer_bwd(dy, x_in, gain, saved, wg, wu, wd, tag, exchange=()):
    g, u, act, h = saved
    dx, dg, du, dgain, got = _ffn_bwd(dy, x_in, gain, g, u, wg, wu, wd, tag, exchange=exchange)
    dwd = _tn_matmul(act, dy, f"dw_down_{tag}")
    dwg = _tn_matmul(dg, h, f"dw_gate_{tag}")
    dwu = _tn_matmul(du, h, f"dw_up_{tag}")
    return dx, dwg, dwu, dwd, dgain, got


def _to_send(dws):
    return [(_row_blocks(g), False) for g in dws]


def kernel(x, a_w_in, a_sink, a_w_out, b_w_in, b_w_out, norm_mix, norm_ffn, w_gate, w_up, w_down, final_norm, loss_target, m_a_w_in, m_a_sink, m_a_w_out, m_b_w_in, m_b_w_out, m_norm_mix, m_norm_ffn, m_w_gate, m_w_up, m_w_down, m_final_norm, v_a_w_in, v_a_sink, v_a_w_out, v_b_w_in, v_b_w_out, v_norm_mix, v_norm_ffn, v_w_gate, v_w_up, v_w_down, v_final_norm):
    bl, seq, d = x.shape
    t = bl * seq
    xf = x.reshape(t, d)
    target = loss_target.reshape(t, d)
    cos, sin = _rope_tables(seq)
    groups_a = [(1, ATTN_HALF_WINDOW)]
    groups_b = [(dil, window // 2 // dil) for window, dil in DILATED_GROUPS]

    def flip(w_):
        return jnp.swapaxes(w_, -1, -2)

    a_w_in, m_a_w_in, v_a_w_in, b_w_in, m_b_w_in, v_b_w_in = map(flip, (a_w_in, m_a_w_in, v_a_w_in, b_w_in, m_b_w_in, v_b_w_in))
    w_gate, m_w_gate, v_w_gate, w_up, m_w_up, v_w_up = map(flip, (w_gate, m_w_gate, v_w_gate, w_up, m_w_up, v_w_up))

    def shard(w_, layer):
        return (w_[layer].astype(BF16), True)

    wa_in, wa_out = map(_rows, _exchange_now([shard(a_w_in, 0), shard(a_w_out, 0)], "gather_first"))

    x1_0, saved_a, got = _mixer_fwd(xf, norm_mix[0:1], wa_in, wa_out, cos, sin, seq, groups_a, "a", sink=a_sink[0],
                                    exchanges={("proj", 0): [shard(w_down, 0)], 0: [shard(w_gate, 0), shard(w_up, 0)]})
    wg0, wu0, wd0 = map(_rows, got[0] + got["proj", 0])
    x2_0, *saved_0, got = _ffn_fwd(x1_0, norm_ffn[0:1], wg0, wu0, wd0, "0", exchange=[shard(b_w_in, 0), shard(b_w_out, 0)])
    wb_in, wb_out = map(_rows, got)
    x1_1, saved_b, got = _mixer_fwd(x2_0, norm_mix[1:2], wb_in, wb_out, cos, sin, seq, groups_b, "b",
                                    exchanges={0: [shard(w_gate, 1)], 1: [shard(w_up, 1)], 2: [shard(w_down, 1)]})
    wg1, wu1, wd1 = map(_rows, got[0] + got[1] + got[2])
    dy, *saved_1, loss_part, d_final, _ = _ffn_fwd(x1_1, norm_ffn[1:2], wg1, wu1, wd1, "1",
                                                   loss_head=(final_norm.reshape(1, d), target))

    dy, dwg1, dwu1, dwd1, d_nf1, _ = _ffn_layer_bwd(dy, x1_1, norm_ffn[1:2], saved_1, wg1, wu1, wd1, "1")
    dy, dwb_in, dwb_out, d_nm1, _, got = _mixer_bwd(
        dy, x2_0, norm_mix[1:2], wb_in, wb_out, saved_b, cos, sin, seq, groups_b, "b",
        exchanges={0: _to_send([dwg1, dwd1]), 1: _to_send([dwu1])})
    (r_g1, r_d1), (r_u1,) = got[0], got[1]
    dy, dwg0, dwu0, dwd0, d_nf0, (r_b_in, r_b_out) = _ffn_layer_bwd(
        dy, x1_0, norm_ffn[0:1], saved_0, wg0, wu0, wd0, "0", exchange=_to_send([dwb_in, dwb_out]))
    dy, dwa_in, r_a_out, d_nm0, d_sink, got = _mixer_bwd(
        dy, xf, norm_mix[0:1], wa_in, wa_out, saved_a, cos, sin, seq, groups_a, "a", sink=a_sink[0],
        exchanges={0: _to_send([dwg0, dwu0, dwd0])}, scatter_dw_out=True)
    r_g0, r_u0, r_d0 = got[0]
    (r_a_in,) = _exchange_now(_to_send([dwa_in]), "scatter_last")
    grad_x = dy.reshape(bl, seq, d)

    def update(received, w_, m_, v_, name):
        out = None
        for layer in reversed(range(len(received))):
            out = _adamw([(received[layer], src) for src in range(N_DEV)], w_, m_, v_, f"adamw_{name}{layer}", layer=layer, into=out)
        return out

    u_a_in = update([r_a_in], a_w_in, m_a_w_in, v_a_w_in, "a_in")
    u_a_out = update([r_a_out], a_w_out, m_a_w_out, v_a_w_out, "a_out")
    u_b_in = update([r_b_in], b_w_in, m_b_w_in, v_b_w_in, "b_in")
    u_b_out = update([r_b_out], b_w_out, m_b_w_out, v_b_w_out, "b_out")
    u_gate = update([r_g0, r_g1], w_gate, m_w_gate, v_w_gate, "gate")
    u_up = update([r_u0, r_u1], w_up, m_w_up, v_w_up, "up")
    u_down = update([r_d0, r_d1], w_down, m_w_down, v_w_down, "down")

    small = _pack_rows([d_nm0, d_nm1, d_nf0, d_nf1, d_final, d_sink, loss_part], d)
    total = _all_reduce_small(small)
    small_w = _pack_rows([norm_mix[0], norm_mix[1], norm_ffn[0], norm_ffn[1], final_norm, a_sink], d)
    small_m = _pack_rows([m_norm_mix[0], m_norm_mix[1], m_norm_ffn[0], m_norm_ffn[1], m_final_norm, m_a_sink], d)
    small_v = _pack_rows([v_norm_mix[0], v_norm_mix[1], v_norm_ffn[0], v_norm_ffn[1], v_final_norm, v_a_sink], d)
    u_small = _adamw([total], small_w, small_m, small_v, "adamw_small")
    loss = total[6, 0]

    outs = []
    for k in range(4):
        sm = u_small[k]
        outs += [flip(u_a_in[k]), sm[5:6, :N_HEADS], u_a_out[k], flip(u_b_in[k]), u_b_out[k], sm[0:2], sm[2:4],
                 flip(u_gate[k]), flip(u_up[k]), u_down[k], sm[4]]
    return (loss, grad_x, *outs)
```

```python
import functools
import math

import jax
import jax.numpy as jnp
from jax import lax
from jax.experimental import pallas as pl
from jax.experimental.pallas import tpu as pltpu

F32 = jnp.float32
BF16 = jnp.bfloat16

HEAD_DIM = 64
N_HEADS = 16
N_KV = 4
GRP = N_HEADS // N_KV
Q_W = N_HEADS * HEAD_DIM
KV_W = N_KV * HEAD_DIM
QKV_W = Q_W + 2 * KV_W
ATTN_HALF_WINDOW = 128
DILATED_GROUPS = ((128, 1), (512, 4), (2048, 16))
ROPE_THETA = 10000.0
RMS_EPS = 1e-6
NEG_INF = -1e30
SCALE = 1.0 / math.sqrt(HEAD_DIM)

ADAM_LR = 0.001
ADAM_B1 = 0.9
ADAM_B2 = 0.999
ADAM_EPS = 1e-08
ADAM_WD = 0.01
ADAM_STEP = 10

LANES = 128
VMEM_LIMIT = 56 * 1024 * 1024
QUERY_BLOCK = 128
N_DEV = 8
MESH = pl.DeviceIdType.MESH

NT = (((1,), (1,)), ((), ()))
TN = (((0,), (0,)), ((), ()))


def _params(*sem):
    return pltpu.CompilerParams(dimension_semantics=tuple(sem) if sem else None, vmem_limit_bytes=VMEM_LIMIT)


def _resident(shape):
    return pl.BlockSpec(shape, lambda *_: (0,) * len(shape), pipeline_mode=pl.Buffered(1))


def _rope_tables(seq):
    inv_freq = 1.0 / (ROPE_THETA ** (jnp.arange(0, HEAD_DIM, 2, dtype=F32) / HEAD_DIM))
    ang = jnp.arange(seq, dtype=F32)[:, None] * inv_freq[None, :]
    cos, sin = jnp.cos(ang), jnp.sin(ang)
    return jnp.tile(cos, (1, 4)), jnp.concatenate([-sin, sin, -sin, sin], axis=1)


def _rope(t, cos, sin_signed):
    lane = lax.broadcasted_iota(jnp.int32, t.shape, 1)
    first = (lane & (HEAD_DIM // 2)) == 0
    swapped = jnp.where(first, pltpu.roll(t, LANES - HEAD_DIM // 2, 1), pltpu.roll(t, HEAD_DIM // 2, 1))
    return t * cos + swapped * sin_signed


def _rms(x):
    return lax.rsqrt(jnp.mean(x * x, axis=-1, keepdims=True) + RMS_EPS)


def _rms_bwd(dh, x, gain):
    r = _rms(x)
    xhat = x * r
    dxh = dh * gain
    dx = r * (dxh - xhat * jnp.mean(dxh * xhat, axis=-1, keepdims=True))
    return dx, xhat


def _accumulate(ref, value, first):
    @pl.when(first)
    def _():
        ref[...] = jnp.zeros_like(ref)

    ref[...] += value


def _tile_rows(seq):
    return min(512, seq)


def _res_shape(bl, seq, dil, c):
    ts = _tile_rows(seq)
    return (bl, dil, seq // ts, ts // dil, c)


def _res_spec(seq, dil, c):
    ts = _tile_rows(seq)
    per_seq = seq // ts
    return pl.BlockSpec((None, dil, None, ts // dil, c), lambda i: (i // per_seq, 0, i % per_seq, 0, 0))


def _seq_view(a):
    bl, dil, tiles, n, c = a.shape
    return a.reshape(bl * dil, tiles * n, c)


def _stage(ts, c):
    return pltpu.VMEM((c // LANES, ts, LANES), F32)


def _split_rows(val, stage_ref, dil):
    if dil == 1:
        return [val]
    ts, c = val.shape
    n, nc = ts // dil, c // LANES
    for k in range(nc):
        stage_ref[k] = val[:, k * LANES:(k + 1) * LANES]
    return [jnp.concatenate([stage_ref[k, pl.ds(r, n, stride=dil), :] for k in range(nc)], axis=1) for r in range(dil)]


def _merge_rows(parts, stage_ref, dil):
    if dil == 1:
        return parts[0]
    n, c = parts[0].shape
    nc = c // LANES
    for r, part in enumerate(parts):
        for k in range(nc):
            stage_ref[k, pl.ds(r, n, stride=dil), :] = part[:, k * LANES:(k + 1) * LANES]
    return jnp.concatenate([stage_ref[k] for k in range(nc)], axis=1)


def _tables_tiled(table, seq, dil):
    ts = _tile_rows(seq)
    return table.reshape(seq // ts, ts // dil, dil, LANES).transpose(0, 2, 1, 3).reshape(seq, LANES)


def _tables_by_residue(table, seq, dil):
    return table.reshape(seq // dil, dil, LANES).transpose(1, 0, 2)


def _qkv_proj(x, gain, w, cos, sin, seq, dils, tag, exchange=()):
    t, d = x.shape
    ts = _tile_rows(seq)
    per_seq = seq // ts
    ng = len(dils)
    tables = [t_ for dil in dils for t_ in (_tables_tiled(cos, seq, dil), _tables_tiled(sin, seq, dil))]

    def body(x_ref, g_ref, w_ref, *refs):
        table_refs, o_refs, h_refs, stage_ref = refs[:2 * ng], refs[2 * ng:3 * ng], refs[3 * ng:4 * ng], refs[4 * ng]
        xv = x_ref[...]
        h_tokens = xv * _rms(xv) * g_ref[...]
        for gi, dil in enumerate(dils):
            n = ts // dil
            h = jnp.concatenate(_split_rows(h_tokens, stage_ref, dil), axis=0).astype(BF16)
            for r in range(dil):
                h_refs[gi][r] = h[r * n:(r + 1) * n]
            acc = lax.dot_general(h, w_ref[gi * QKV_W:(gi + 1) * QKV_W, :], NT, preferred_element_type=F32)
            c, s = table_refs[2 * gi][...], table_refs[2 * gi + 1][...]
            for j in range(QKV_W // LANES):
                cols = slice(j * LANES, (j + 1) * LANES)
                val = acc[:, cols]
                if j < (Q_W + KV_W) // LANES:
                    val = _rope(val, c, s)
                if j < Q_W // LANES:
                    val = val * SCALE
                val = val.astype(BF16)
                for r in range(dil):
                    o_refs[gi][r, :, cols] = val[r * n:(r + 1) * n]

    table = pl.BlockSpec((ts, LANES), lambda i: (i % per_seq, 0))
    outs, exchanged = _hosted_call(
        body, exchange, name=f"qkv_proj_{tag}", grid=(t // ts,),
        in_specs=[pl.BlockSpec((ts, d), lambda i: (i, 0)), _resident((1, d)), _resident(w.shape)] + [table] * (2 * ng),
        out_specs=[_res_spec(seq, dil, QKV_W) for dil in dils] + [_res_spec(seq, dil, d) for dil in dils],
        out_shape=[jax.ShapeDtypeStruct(_res_shape(t // seq, seq, dil, QKV_W), BF16) for dil in dils]
                  + [jax.ShapeDtypeStruct(_res_shape(t // seq, seq, dil, d), BF16) for dil in dils],
        scratch_shapes=[_stage(ts, d)], semantics=("parallel",), args=(x, gain, w, *tables))
    return outs[:ng], outs[ng:], exchanged


def _band(bq, wk):
    return lax.broadcasted_iota(jnp.int32, (bq, wk), 0) - lax.broadcasted_iota(jnp.int32, (bq, wk), 1)


def _swap_halves(src_ref, base, dst_ref):
    for c in range(KV_W // LANES):
        dst_ref[c] = pltpu.roll(src_ref[:, base + c * LANES:base + (c + 1) * LANES], HEAD_DIM, 1)


def _pair_operand(src_ref, swapped_ref, base, kv, rows):
    c = kv // 2
    chunk, swapped = src_ref[rows, base + c * LANES:base + (c + 1) * LANES], swapped_ref[c, rows, :]
    lo = lax.broadcasted_iota(jnp.int32, chunk.shape, 1) < HEAD_DIM
    zero = jnp.zeros_like(chunk)
    if kv % 2 == 0:
        return jnp.concatenate([jnp.where(lo, chunk, zero), jnp.where(lo, zero, swapped)], axis=0)
    return jnp.concatenate([jnp.where(lo, swapped, zero), jnp.where(lo, zero, chunk)], axis=0)


def _over_keys(col, wk):
    if wk % LANES:
        return jnp.broadcast_to(col, (col.shape[0], wk))
    wide = jnp.broadcast_to(col, (col.shape[0], LANES))
    return wide if wk == LANES else jnp.concatenate([wide] * (wk // LANES), axis=1)


def _key_rows(bq, w, length):
    return min(bq + 2 * w, length)


def _window(i, bq, w, wk, length):
    q0 = pl.multiple_of(i * bq, bq)
    k0 = pl.multiple_of(jnp.clip(q0 - w, 0, length - wk), min(w, bq))
    return q0, k0


def _attn_fwd(qkv, w, tag, sink=None, exchange=()):
    shape = qkv.shape
    rows_all = _seq_view(qkv)
    nseq, length, _ = rows_all.shape
    bq = min(QUERY_BLOCK, length)
    wk = _key_rows(bq, w, length)
    nb = length // bq
    has_sink = sink is not None

    def body(*refs):
        qkv_ref = refs[0]
        sink_ref = refs[1] if has_sink else None
        o_ref, lse_ref, kk_ref, vv_ref = refs[-4:]
        _swap_halves(qkv_ref, Q_W, kk_ref)
        _swap_halves(qkv_ref, Q_W + KV_W, vv_ref)
        band = _band(bq, wk)
        lane = lax.broadcasted_iota(jnp.int32, (bq, LANES), 1)
        lo = lane < HEAD_DIM

        def block(i, carry):
            q0, k0 = _window(i, bq, w, wk, length)
            valid = jnp.abs(band + (q0 - k0)) <= w
            rows, krows = pl.ds(q0, bq), pl.ds(k0, wk)
            lse_tile = jnp.zeros((bq, LANES), F32)
            for kv in range(N_KV):
                heads = [(kv * GRP + h, h % 2) for h in range(GRP)]
                qp = [qkv_ref[rows, (kv * 2 + j) * LANES:(kv * 2 + j + 1) * LANES] for j in range(GRP // 2)]
                k2 = _pair_operand(qkv_ref, kk_ref, Q_W, kv, krows)
                v2 = _pair_operand(qkv_ref, vv_ref, Q_W + KV_W, kv, krows)
                sc2 = [lax.dot_general(q_, k2, NT, preferred_element_type=F32) for q_ in qp]
                sc = [jnp.where(valid, s_[:, half * wk:(half + 1) * wk], NEG_INF) for s_ in sc2 for half in range(2)]
                m = [jnp.max(s_, axis=-1, keepdims=True) for s_ in sc]
                if has_sink:
                    m = [jnp.maximum(m_, sink_ref[hd]) for m_, (hd, _) in zip(m, heads)]
                mb = [jnp.broadcast_to(m_, (bq, LANES)) for m_ in m]
                p = [jnp.exp(s_ - _over_keys(m_, wk)) for s_, m_ in zip(sc, m)]
                den = [jnp.sum(p_, axis=-1, keepdims=True) for p_ in p]
                if has_sink:
                    den = [d_ + jnp.exp(sink_ref[hd] - m_) for d_, m_, (hd, _) in zip(den, m, heads)]
                inv = [jnp.broadcast_to(1.0 / d_, (bq, LANES)) for d_ in den]
                pb = [p_.astype(BF16) for p_ in p]
                for j in range(GRP // 2):
                    o = jnp.dot(jnp.concatenate([pb[2 * j], pb[2 * j + 1]], axis=1), v2, preferred_element_type=F32)
                    o = o * jnp.where(lo, inv[2 * j], inv[2 * j + 1])
                    o_ref[rows, (kv * 2 + j) * LANES:(kv * 2 + j + 1) * LANES] = o.astype(BF16)
                for h, (hd, _) in enumerate(heads):
                    lse_tile = jnp.where(lane == hd, mb[h] - jnp.log(inv[h]), lse_tile)
            lse_ref[rows, :] = lse_tile
            return carry

        lax.fori_loop(0, nb, block, 0)

    args = [rows_all]
    in_specs = [pl.BlockSpec((None, length, QKV_W), lambda i: (i, 0, 0))]
    if has_sink:
        args.append(sink)
        in_specs.append(pl.BlockSpec(memory_space=pltpu.SMEM))
    (o, lse), exchanged = _hosted_call(
        body, exchange, name=f"attn_fwd_{tag}", grid=(nseq,), in_specs=in_specs,
        out_specs=[pl.BlockSpec((None, length, Q_W), lambda i: (i, 0, 0)), pl.BlockSpec((None, length, LANES), lambda i: (i, 0, 0))],
        out_shape=[jax.ShapeDtypeStruct((nseq, length, Q_W), BF16), jax.ShapeDtypeStruct((nseq, length, LANES), F32)],
        scratch_shapes=[pltpu.VMEM((KV_W // LANES, length, LANES), BF16), pltpu.VMEM((KV_W // LANES, length, LANES), BF16)],
        semantics=("parallel",), args=args)
    return o.reshape(shape[:-1] + (Q_W,)), lse.reshape(shape[:-1] + (LANES,)), exchanged


def _head_expand():
    return (jnp.arange(LANES)[:, None] == jnp.arange(Q_W)[None, :] // HEAD_DIM).astype(BF16)


def _out_proj(x, os, lses, dils, w, seq, tag):
    t, d = x.shape
    ts = _tile_rows(seq)
    ng = len(os)
    bl = t // seq
    if ng == 1:
        def body1(x_ref, o_ref, w_ref, y_ref):
            y_ref[...] = x_ref[...] + jnp.dot(o_ref[...], w_ref[...], preferred_element_type=F32)

        row = pl.BlockSpec((ts, d), lambda i: (i, 0))
        o = os[0].reshape(t, Q_W)
        y = pl.pallas_call(
            body1, name=f"out_proj_{tag}", grid=(t // ts,), in_specs=[row, row, _resident(w.shape)], out_specs=row,
            out_shape=jax.ShapeDtypeStruct((t, d), F32), compiler_params=_params("parallel"),
        )(x, o, w)
        return y, o, [lses[0]]

    def body(*refs):
        x_ref, w_ref, e_ref = refs[:3]
        o_refs, l_refs = refs[3:3 + ng], refs[3 + ng:3 + 2 * ng]
        y_ref, om_ref = refs[3 + 2 * ng:5 + 2 * ng]
        lt_refs = refs[5 + 2 * ng:5 + 3 * ng]
        wide_ref, narrow_ref = refs[5 + 3 * ng:]
        ls = [_merge_rows([l_refs[g][r] for r in range(dils[g])], narrow_ref, dils[g]) for g in range(ng)]
        mx = functools.reduce(jnp.maximum, ls)
        tot = mx + jnp.log(functools.reduce(lambda a, b: a + b, [jnp.exp(l_ - mx) for l_ in ls]))
        e = e_ref[...]
        o = None
        for g in range(ng):
            wt = jnp.exp(ls[g] - tot)
            hi = wt.astype(BF16)
            lo = (wt - hi.astype(F32)).astype(BF16)
            wide = jnp.dot(hi, e, preferred_element_type=F32) + jnp.dot(lo, e, preferred_element_type=F32)
            term = wide * _merge_rows([o_refs[g][r].astype(F32) for r in range(dils[g])], wide_ref, dils[g])
            o = term if o is None else o + term
        ob = o.astype(BF16)
        om_ref[...] = ob
        y_ref[...] = x_ref[...] + jnp.dot(ob, w_ref[...], preferred_element_type=F32)
        for g in range(ng):
            for r, part in enumerate(_split_rows(tot, narrow_ref, dils[g])):
                lt_refs[g][r] = part

    row = pl.BlockSpec((ts, d), lambda i: (i, 0))
    e = _head_expand()
    outs = pl.pallas_call(
        body, name=f"out_proj_{tag}", grid=(t // ts,),
        in_specs=[row, _resident(w.shape), _resident(e.shape)] + [_res_spec(seq, dl, Q_W) for dl in dils]
                 + [_res_spec(seq, dl, LANES) for dl in dils],
        out_specs=[row, pl.BlockSpec((ts, Q_W), lambda i: (i, 0))] + [_res_spec(seq, dl, LANES) for dl in dils],
        out_shape=[jax.ShapeDtypeStruct((t, d), F32), jax.ShapeDtypeStruct((t, Q_W), BF16)]
                  + [jax.ShapeDtypeStruct(_res_shape(bl, seq, dl, LANES), F32) for dl in dils],
        scratch_shapes=[_stage(ts, Q_W), _stage(ts, LANES)],
        compiler_params=_params("parallel"),
    )(x, w, e, *os, *lses)
    return outs[0], outs[1], list(outs[2:])


def _sigmoid(g):
    return 1.0 / (1.0 + jnp.exp(-g))


def _ffn_fwd(x, gain, wg, wu, wd, tag, exchange=(), loss_head=None):
    t, d = x.shape
    f = wd.shape[0]
    tm = min(256, t)
    has_loss = loss_head is not None

    def body(*refs):
        x_ref, gain_ref, wg_ref, wu_ref, wd_ref = refs[:5]
        y_ref, g_ref, u_ref, a_ref, h_ref = refs[-7:-2] if has_loss else refs[-5:]
        xv = x_ref[...]
        h = (xv * _rms(xv) * gain_ref[...]).astype(BF16)
        h_ref[...] = h
        g = lax.dot_general(h, wg_ref[...], NT, preferred_element_type=F32)
        u = lax.dot_general(h, wu_ref[...], NT, preferred_element_type=F32)
        g_ref[...] = g.astype(BF16)
        u_ref[...] = u.astype(BF16)
        a = (g * _sigmoid(g) * u).astype(BF16)
        a_ref[...] = a
        y = xv + jnp.dot(a, wd_ref[...], preferred_element_type=F32)
        if not has_loss:
            y_ref[...] = y
            return
        head_ref, target_ref, loss_ref, dhead_ref = refs[5], refs[6], refs[-2], refs[-1]
        head = head_ref[...]
        yhat = y * _rms(y)
        err = yhat * head - target_ref[...]
        dout = err * (1.0 / d)
        y_ref[...] = _rms_bwd(dout, y, head)[0]
        first = pl.program_id(0) == 0
        part = 0.5 * jnp.sum(jnp.mean(err * err, axis=-1, keepdims=True), axis=0, keepdims=True)
        _accumulate(loss_ref, jnp.broadcast_to(part, loss_ref.shape), first)
        _accumulate(dhead_ref, jnp.sum(dout * yhat, axis=0, keepdims=True), first)

    row = pl.BlockSpec((tm, d), lambda i: (i, 0))
    wide = pl.BlockSpec((tm, f), lambda i: (i, 0))
    in_specs = [row, _resident((1, d)), _resident(wg.shape), _resident(wu.shape), _resident(wd.shape)]
    out_specs = [row, wide, wide, wide, row]
    out_shape = [jax.ShapeDtypeStruct((t, d), F32)] + [jax.ShapeDtypeStruct((t, f), BF16)] * 3 + [jax.ShapeDtypeStruct((t, d), BF16)]
    if has_loss:
        in_specs += [_resident((1, d)), row]
        out_specs += [pl.BlockSpec((1, LANES), lambda i: (0, 0)), pl.BlockSpec((1, d), lambda i: (0, 0))]
        out_shape += [jax.ShapeDtypeStruct((1, LANES), F32), jax.ShapeDtypeStruct((1, d), F32)]
    outs, exchanged = _hosted_call(
        body, exchange, name=f"ffn_fwd_{tag}", grid=(t // tm,), in_specs=in_specs, out_specs=out_specs, out_shape=out_shape,
        scratch_shapes=[], semantics=("arbitrary" if has_loss else "parallel",),
        args=(x, gain, wg, wu, wd) + (tuple(loss_head) if has_loss else ()))
    return (*outs, exchanged)


def _ffn_bwd(dy, x, gain, g, u, wg, wu, wd, tag, exchange=()):
    t, d = x.shape
    f = wd.shape[0]
    tm = min(256, t)

    def body(dy_ref, x_ref, gain_ref, g_ref, u_ref, wg_ref, wu_ref, wd_ref, dx_ref, dg_ref, du_ref, dgain_ref):
        dyv = dy_ref[...]
        da = lax.dot_general(dyv.astype(BF16), wd_ref[...], NT, preferred_element_type=F32)
        gv, uv = g_ref[...].astype(F32), u_ref[...].astype(F32)
        sg = _sigmoid(gv)
        act = gv * sg
        du = (da * act).astype(BF16)
        dg = (da * uv * (sg * (1.0 + gv * (1.0 - sg)))).astype(BF16)
        du_ref[...] = du
        dg_ref[...] = dg
        dh = jnp.dot(dg, wg_ref[...], preferred_element_type=F32) + jnp.dot(du, wu_ref[...], preferred_element_type=F32)
        xv, gain_v = x_ref[...], gain_ref[...]
        dx, xhat = _rms_bwd(dh, xv, gain_v)
        dx_ref[...] = dyv + dx
        _accumulate(dgain_ref, jnp.sum(dh * xhat, axis=0, keepdims=True), pl.program_id(0) == 0)

    row = pl.BlockSpec((tm, d), lambda i: (i, 0))
    wide = pl.BlockSpec((tm, f), lambda i: (i, 0))
    outs, exchanged = _hosted_call(
        body, exchange, name=f"ffn_bwd_{tag}", grid=(t // tm,),
        in_specs=[row, row, _resident((1, d)), wide, wide, _resident(wg.shape), _resident(wu.shape), _resident(wd.shape)],
        out_specs=[row, wide, wide, pl.BlockSpec((1, d), lambda i: (0, 0))],
        out_shape=[jax.ShapeDtypeStruct((t, d), F32), jax.ShapeDtypeStruct((t, f), BF16), jax.ShapeDtypeStruct((t, f), BF16),
                   jax.ShapeDtypeStruct((1, d), F32)],
        scratch_shapes=[], semantics=("arbitrary",), args=(dy, x, gain, g, u, wg, wu, wd))
    return (*outs, exchanged)


def _tn_matmul(a, b, name, into=None, row_block=0, row_blocks=1):
    t, k = a.shape
    n = b.shape[1]
    tk = k // 2 if (k // 2) % LANES == 0 else k
    tt = min(2048, t)
    first = row_block * (k // tk)

    def body(a_ref, b_ref, *rest):
        o_ref, acc_ref = rest[-2:]
        prod = lax.dot_general(a_ref[...].astype(BF16), b_ref[...].astype(BF16), TN, preferred_element_type=F32)
        j = pl.program_id(1)

        @pl.when(j == 0)
        def _():
            acc_ref[...] = prod

        @pl.when(j > 0)
        def _():
            acc_ref[...] += prod

        @pl.when(j == pl.num_programs(1) - 1)
        def _():
            o_ref[...] = acc_ref[...].astype(BF16)

    return pl.pallas_call(
        body, name=name, grid=(k // tk, t // tt),
        in_specs=[pl.BlockSpec((tt, tk), lambda i, j: (j, i)), pl.BlockSpec((tt, n), lambda i, j: (j, 0))]
                 + ([ANY] if into is not None else []),
        out_specs=pl.BlockSpec((tk, n), lambda i, j: (first + i, 0)),
        out_shape=jax.ShapeDtypeStruct((row_blocks * k, n), BF16),
        scratch_shapes=[pltpu.VMEM((tk, n), F32)],
        input_output_aliases={2: 0} if into is not None else {},
        compiler_params=_params("parallel", "arbitrary"),
    )(a, b, *([into] if into is not None else []))


def _attn_out_bwd(dx, w, o, dils, seq, tag, lse=None, sink=None, exchange=()):
    t, d = dx.shape
    ts = _tile_rows(seq)
    bl = t // seq
    ng = len(dils)
    has_sink = sink is not None
    expand = _head_expand().T

    def body(*refs):
        refs = list(refs)
        dx_ref, w_ref, o_ref, e_ref = refs[:4]
        refs = refs[4:]
        lse_ref, sink_ref = (refs.pop(0), refs.pop(0)) if has_sink else (None, None)
        do_refs, dl_refs = refs[:ng], refs[ng:2 * ng]
        refs = refs[2 * ng:]
        dsink_ref = refs.pop(0) if has_sink else None
        dof_ref, dlf_ref = refs
        do = lax.dot_general(dx_ref[...].astype(BF16), w_ref[...], NT, preferred_element_type=F32)
        prod = do * o_ref[...].astype(F32)
        hi = prod.astype(BF16)
        lo = (prod - hi.astype(F32)).astype(BF16)
        e = e_ref[...]
        dl = jnp.dot(hi, e, preferred_element_type=F32) + jnp.dot(lo, e, preferred_element_type=F32)
        for g in range(ng):
            for r, part in enumerate(_split_rows(do, dof_ref, dils[g])):
                do_refs[g][r] = part.astype(BF16)
            for r, part in enumerate(_split_rows(dl, dlf_ref, dils[g])):
                dl_refs[g][r] = part
        if has_sink:
            part = -jnp.exp(sink_ref[...] - lse_ref[...]) * dl
            _accumulate(dsink_ref, jnp.sum(part, axis=0, keepdims=True), pl.program_id(0) == 0)

    row = pl.BlockSpec((ts, d), lambda i: (i, 0))
    narrow = pl.BlockSpec((ts, LANES), lambda i: (i, 0))
    args = [dx, w, o, expand]
    in_specs = [row, _resident(w.shape), pl.BlockSpec((ts, Q_W), lambda i: (i, 0)), _resident(expand.shape)]
    if has_sink:
        args += [lse, jnp.pad(sink.reshape(1, N_HEADS), ((0, 0), (0, LANES - N_HEADS)))]
        in_specs += [narrow, _resident((1, LANES))]
    out_specs = [_res_spec(seq, dl, Q_W) for dl in dils] + [_res_spec(seq, dl, LANES) for dl in dils]
    out_shape = ([jax.ShapeDtypeStruct(_res_shape(bl, seq, dl, Q_W), BF16) for dl in dils]
                 + [jax.ShapeDtypeStruct(_res_shape(bl, seq, dl, LANES), F32) for dl in dils])
    if has_sink:
        out_specs.append(pl.BlockSpec((1, LANES), lambda i: (0, 0)))
        out_shape.append(jax.ShapeDtypeStruct((1, LANES), F32))
    outs, exchanged = _hosted_call(
        body, exchange, name=f"attn_out_bwd_{tag}", grid=(t // ts,), in_specs=in_specs, out_specs=out_specs, out_shape=out_shape,
        scratch_shapes=[_stage(ts, Q_W), _stage(ts, LANES)], semantics=("arbitrary" if has_sink else "parallel",), args=args)
    return list(outs[:ng]), list(outs[ng:2 * ng]), (outs[2 * ng] if has_sink else None), exchanged


def _attn_bwd(qkv, do, lse, delta, cos, sin, w, tag, exchange=()):
    shape = qkv.shape
    dil = shape[1]
    rows_all = _seq_view(qkv)
    nseq, length, _ = rows_all.shape
    bq = min(QUERY_BLOCK, length)
    wk = _key_rows(bq, w, length)
    nb = length // bq

    def body(qkv_ref, do_ref, lse_ref, dl_ref, cos_ref, sin_ref, dp_ref, kk_ref, vv_ref, dk_ref, dv_ref):
        _swap_halves(qkv_ref, Q_W, kk_ref)
        _swap_halves(qkv_ref, Q_W + KV_W, vv_ref)
        dk_ref[...] = jnp.zeros_like(dk_ref)
        dv_ref[...] = jnp.zeros_like(dv_ref)
        band = _band(bq, wk)
        lo_q = lax.broadcasted_iota(jnp.int32, (bq, LANES), 1) < HEAD_DIM
        hi_q = jnp.logical_not(lo_q)

        def block(i, carry):
            q0, k0 = _window(i, bq, w, wk, length)
            valid = jnp.abs(band + (q0 - k0)) <= w
            rows, krows = pl.ds(q0, bq), pl.ds(k0, wk)
            c, sn = cos_ref[rows, :], -sin_ref[rows, :]
            lse_t, dl_t = lse_ref[rows, :], dl_ref[rows, :]
            for kv in range(N_KV):
                heads = [(kv * GRP + h, h % 2) for h in range(GRP)]
                cols = [slice((kv * 2 + j) * LANES, (kv * 2 + j + 1) * LANES) for j in range(GRP // 2)]
                qp = [qkv_ref[rows, cs] for cs in cols]
                dop = [do_ref[rows, cs] for cs in cols]
                k2 = _pair_operand(qkv_ref, kk_ref, Q_W, kv, krows)
                v2 = _pair_operand(qkv_ref, vv_ref, Q_W + KV_W, kv, krows)
                sc2 = [lax.dot_general(q_, k2, NT, preferred_element_type=F32) for q_ in qp]
                dp2 = [lax.dot_general(d_, v2, NT, preferred_element_type=F32) for d_ in dop]
                sc = [s_[:, half * wk:(half + 1) * wk] for s_ in sc2 for half in range(2)]
                dp = [d_[:, half * wk:(half + 1) * wk] for d_ in dp2 for half in range(2)]
                p = [jnp.exp(jnp.where(valid, s_, NEG_INF) - _over_keys(lse_t[:, hd:hd + 1], wk))
                     for s_, (hd, _) in zip(sc, heads)]
                ds = [(p_ * (dp_ - _over_keys(dl_t[:, hd:hd + 1], wk))).astype(BF16) for p_, dp_, (hd, _) in zip(p, dp, heads)]
                pb = [p_.astype(BF16) for p_ in p]
                for j in range(GRP // 2):
                    dq = jnp.dot(jnp.concatenate([ds[2 * j], ds[2 * j + 1]], axis=1), k2, preferred_element_type=F32) * SCALE
                    dp_ref[rows, cols[j]] = _rope(dq, c, sn).astype(BF16)
                zero = jnp.zeros((bq, LANES), BF16)
                q4 = jnp.concatenate([jnp.where(lo_q if h % 2 == 0 else hi_q, qp[h // 2], zero) for h in range(GRP)], axis=0)
                do4 = jnp.concatenate([jnp.where(lo_q if h % 2 == 0 else hi_q, dop[h // 2], zero) for h in range(GRP)], axis=0)
                dk_ref[kv, krows, :] += lax.dot_general(jnp.concatenate(ds, axis=0), q4, TN, preferred_element_type=F32)
                dv_ref[kv, krows, :] += lax.dot_general(jnp.concatenate(pb, axis=0), do4, TN, preferred_element_type=F32)
            return carry

        lax.fori_loop(0, nb, block, 0)
        lo = lax.broadcasted_iota(jnp.int32, (length, LANES), 1) < HEAD_DIM
        c, sn = cos_ref[...], -sin_ref[...]
        for ch in range(KV_W // LANES):
            halves = []
            for acc_ref in (dk_ref, dv_ref):
                even, odd = acc_ref[2 * ch], acc_ref[2 * ch + 1]
                even = even + pltpu.roll(even, HEAD_DIM, 1)
                odd = odd + pltpu.roll(odd, HEAD_DIM, 1)
                halves.append(jnp.where(lo, even, odd))
            dp_ref[:, Q_W + ch * LANES:Q_W + (ch + 1) * LANES] = _rope(halves[0], c, sn).astype(BF16)
            dp_ref[:, Q_W + KV_W + ch * LANES:Q_W + KV_W + (ch + 1) * LANES] = halves[1].astype(BF16)

    def seq_block(c):
        return pl.BlockSpec((None, length, c), lambda i: (i, 0, 0))

    table = pl.BlockSpec((None, length, LANES), lambda i: (i % dil, 0, 0))
    (out,), exchanged = _hosted_call(
        body, exchange, name=f"attn_bwd_{tag}", grid=(nseq,),
        in_specs=[seq_block(QKV_W), seq_block(Q_W), seq_block(LANES), seq_block(LANES), table, table],
        out_specs=[pl.BlockSpec((None, length, QKV_W), lambda i: (i, 0, 0))],
        out_shape=[jax.ShapeDtypeStruct((nseq, length, QKV_W), BF16)],
        scratch_shapes=[pltpu.VMEM((KV_W // LANES, length, LANES), BF16), pltpu.VMEM((KV_W // LANES, length, LANES), BF16),
                        pltpu.VMEM((N_KV, length, LANES), F32), pltpu.VMEM((N_KV, length, LANES), F32)],
        semantics=("parallel",), args=(rows_all, _seq_view(do), _seq_view(lse), _seq_view(delta), cos, sin))
    return out.reshape(shape), exchanged


def _qkv_bwd(dy, x, gain, w, dps, dils, seq, tag):
    t, d = x.shape
    ts = _tile_rows(seq)
    ng = len(dps)

    def body(dy_ref, x_ref, gain_ref, w_ref, *refs):
        dp_refs, (dx_ref, dgain_ref, stage_ref) = refs[:ng], refs[ng:]
        dh = None
        for gi in range(ng):
            dil = dils[gi]
            n = ts // dil
            dp = dp_refs[gi][0] if dil == 1 else jnp.concatenate([dp_refs[gi][r] for r in range(dil)], axis=0)
            part = jnp.dot(dp, w_ref[gi * QKV_W:(gi + 1) * QKV_W, :], preferred_element_type=F32)
            part = _merge_rows([part[r * n:(r + 1) * n] for r in range(dil)], stage_ref, dil)
            dh = part if dh is None else dh + part
        xv, gain_v = x_ref[...], gain_ref[...]
        dx, xhat = _rms_bwd(dh, xv, gain_v)
        dx_ref[...] = dy_ref[...] + dx
        _accumulate(dgain_ref, jnp.sum(dh * xhat, axis=0, keepdims=True), pl.program_id(0) == 0)

    row = pl.BlockSpec((ts, d), lambda i: (i, 0))
    return pl.pallas_call(
        body, name=f"qkv_bwd_{tag}", grid=(t // ts,),
        in_specs=[row, row, _resident((1, d)), _resident(w.shape)] + [_res_spec(seq, dl, QKV_W) for dl in dils],
        out_specs=[row, pl.BlockSpec((1, d), lambda i: (0, 0))],
        out_shape=[jax.ShapeDtypeStruct((t, d), F32), jax.ShapeDtypeStruct((1, d), F32)],
        scratch_shapes=[_stage(ts, d)], compiler_params=_params("arbitrary"),
    )(dy, x, gain, w, *dps)


ANY = pl.BlockSpec(memory_space=pl.ANY)


def _place():
    x, y, c = lax.axis_index("x"), lax.axis_index("y"), lax.axis_index("c")
    return x, y, c


def _exchange_steps(srcs, dsts, gather, send_sems, recv_sems, local_sems):
    x, y, c = _place()
    me, sibling = (x, y, c), (x, y, 1 - c)
    chips = [(1 - x, y), (x, 1 - y), (1 - x, 1 - y)]
    mine = 4 * x + 2 * y + c

    def slot(a, device):
        px, py, pc = device
        return dsts[a].at[4 * px + 2 * py + pc]

    def passes(a, k, block, to, src=None):
        rows = slot(a, block)
        return pltpu.make_async_remote_copy(src_ref=rows if src is None else src, dst_ref=rows, send_sem=send_sems.at[a, k],
                                            recv_sem=recv_sems.at[a, k], device_id=to, device_id_type=MESH)

    def scatters(a, k):
        peer = mine ^ k
        return pltpu.make_async_remote_copy(
            src_ref=srcs[a].at[peer], dst_ref=dsts[a].at[mine], send_sem=send_sems.at[a, k - 1], recv_sem=recv_sems.at[a, k - 1],
            device_id=(peer // 4, (peer // 2) % 2, peer % 2), device_id_type=MESH)

    def local(a):
        return pltpu.make_async_copy(srcs[a] if gather[a] else srcs[a].at[mine], dsts[a].at[mine], local_sems.at[a])

    def first_copies(a):
        if not gather[a]:
            return [scatters(a, k) for k in range(1, N_DEV)]
        return [passes(a, 0, me, sibling, src=srcs[a])] + [passes(a, 1 + j, me, (*chip, c), src=srcs[a]) for j, chip in enumerate(chips)]

    def start():
        for a in range(len(srcs)):
            local(a).start()
            for cp in first_copies(a):
                cp.start()

    def forward():
        for a in range(len(srcs)):
            if gather[a]:
                for j, chip in enumerate(chips):
                    passes(a, 1 + j, (*chip, c), me).wait_recv()
                    passes(a, 4 + j, (*chip, c), sibling).start()

    def finish():
        for a in range(len(srcs)):
            if gather[a]:
                passes(a, 0, sibling, me).wait_recv()
                for j, chip in enumerate(chips):
                    passes(a, 4 + j, (*chip, 1 - c), me).wait_recv()
                    passes(a, 4 + j, (*chip, c), sibling).wait_send()
                for cp in first_copies(a):
                    cp.wait_send()
            else:
                for cp in first_copies(a):
                    cp.wait()
            local(a).wait()

    return start, forward, finish


def _exchange_scratch(n):
    return [pltpu.SemaphoreType.DMA((n, N_DEV - 1)), pltpu.SemaphoreType.DMA((n, N_DEV - 1)), pltpu.SemaphoreType.DMA((n,))]


def _exchanged_shapes(exchange):
    return [jax.ShapeDtypeStruct(((N_DEV,) + a.shape) if g else a.shape, a.dtype) for a, g in exchange]


def _hosted_call(body, exchange, *, name, grid, in_specs, out_specs, out_shape, scratch_shapes, semantics, args,
                 input_output_aliases=None):
    single = not isinstance(out_shape, (list, tuple))
    out_specs, out_shape = ([out_specs], [out_shape]) if single else (list(out_specs), list(out_shape))
    scratch, aliases = list(scratch_shapes), dict(input_output_aliases or {})
    if not exchange:
        outs = pl.pallas_call(body, name=name, grid=grid, in_specs=in_specs, out_specs=out_specs, out_shape=out_shape,
                              scratch_shapes=scratch, input_output_aliases=aliases, compiler_params=_params(*semantics))(*args)
        return list(outs), []
    n, n_in, n_out, n_scr = len(exchange), len(in_specs), len(out_specs), len(scratch)
    gather = [g for _, g in exchange]
    steps = math.prod(grid)

    def hosted(*refs):
        own_in, x_in = refs[:n_in], refs[n_in:n_in + n]
        own_out, x_out = refs[n_in + n:n_in + n + n_out], refs[n_in + n + n_out:n_in + 2 * n + n_out]
        own_scr, sems = refs[n_in + 2 * n + n_out:n_in + 2 * n + n_out + n_scr], refs[-3:]
        step = pl.program_id(0)
        for axis in range(1, len(grid)):
            step = step * grid[axis] + pl.program_id(axis)
        start, forward, finish = _exchange_steps(x_in, x_out, gather, *sems)
        pl.when(step == 0)(start)
        body(*own_in, *own_out, *own_scr)
        pl.when(step == steps // 2)(forward)
        pl.when(step == steps - 1)(finish)

    outs = pl.pallas_call(
        hosted, name=name, grid=grid, in_specs=list(in_specs) + [ANY] * n, out_specs=out_specs + [ANY] * n,
        out_shape=out_shape + _exchanged_shapes(exchange), scratch_shapes=scratch + _exchange_scratch(n),
        input_output_aliases=aliases, compiler_params=_params(*["arbitrary"] * len(grid)),
    )(*args, *[a for a, _ in exchange])
    return list(outs[:n_out]), list(outs[n_out:])


def _exchange_now(exchange, name):
    n = len(exchange)
    gather = [g for _, g in exchange]

    def body(*refs):
        for step in _exchange_steps(refs[:n], refs[n:2 * n], gather, *refs[2 * n:]):
            step()

    return pl.pallas_call(
        body, name=name, in_specs=[ANY] * n, out_specs=[ANY] * n, out_shape=_exchanged_shapes(exchange),
        scratch_shapes=_exchange_scratch(n),
    )(*[a for a, _ in exchange])


def _all_reduce_small(v):
    def body(v_ref, o_ref, recv_ref, send_sems, recv_sems):
        x, y, c = _place()
        me = 4 * x + 2 * y + c
        copies = []
        for k in range(1, N_DEV):
            peer = me ^ k
            copies.append(pltpu.make_async_remote_copy(
                src_ref=v_ref, dst_ref=recv_ref.at[k], send_sem=send_sems.at[k - 1], recv_sem=recv_sems.at[k - 1],
                device_id=(peer // 4, (peer // 2) % 2, peer % 2), device_id_type=MESH))
        for cp in copies:
            cp.start()
        recv_ref[0] = v_ref[...]
        for cp in copies:
            cp.wait()
        acc = recv_ref[me]
        for src in range(1, N_DEV):
            acc = acc + recv_ref[me ^ src]
        o_ref[...] = acc

    vm = pl.BlockSpec(memory_space=pltpu.VMEM)
    return pl.pallas_call(
        body, name="all_reduce_small", in_specs=[vm], out_specs=vm, out_shape=jax.ShapeDtypeStruct(v.shape, F32),
        scratch_shapes=[pltpu.VMEM((N_DEV,) + v.shape, F32), pltpu.SemaphoreType.DMA((N_DEV - 1,)),
                        pltpu.SemaphoreType.DMA((N_DEV - 1,))],
    )(v)


def _adamw_math(w, g, m, v):
    m = ADAM_B1 * m + (1.0 - ADAM_B1) * g
    v = ADAM_B2 * v + (1.0 - ADAM_B2) * (g * g)
    m_hat = m / (1.0 - ADAM_B1 ** ADAM_STEP)
    v_hat = v / (1.0 - ADAM_B2 ** ADAM_STEP)
    delta = -ADAM_LR * (m_hat / (jnp.sqrt(v_hat) + ADAM_EPS) + ADAM_WD * w)
    return delta, m, v


def _adamw(parts, w, m, v, name, layer=None, into=None):
    r, c = w.shape[-2:]
    tr = r // 2 if r % 16 == 0 and r >= 256 else r
    n = len(parts)

    def body(*refs):
        w_ref, m_ref, v_ref = refs[n:n + 3]
        g_ref, d_ref, nm_ref, nv_ref = refs[-4:]
        g = refs[0][...].astype(F32)
        for p_ref in refs[1:n]:
            g = g + p_ref[...].astype(F32)
        g_ref[...] = g
        d_ref[...], nm_ref[...], nv_ref[...] = _adamw_math(w_ref[...], g, m_ref[...], v_ref[...])

    def slab(slot):
        return pl.BlockSpec((None, tr, c), lambda i: (slot, i, 0))

    tile = pl.BlockSpec((tr, c), lambda i: (i, 0)) if layer is None else slab(layer)
    arrays, in_specs = [], []
    for p in parts:
        if isinstance(p, tuple):
            arrays.append(p[0])
            in_specs.append(slab(p[1]))
        else:
            arrays.append(p)
            in_specs.append(tile)
    kept = list(into) if into is not None else []
    return pl.pallas_call(
        body, name=name, grid=(r // tr,), in_specs=in_specs + [tile] * 3 + [ANY] * len(kept), out_specs=[tile] * 4,
        out_shape=[jax.ShapeDtypeStruct(w.shape, F32)] * 4,
        input_output_aliases={n + 3 + k: k for k in range(len(kept))}, compiler_params=_params("parallel"),
    )(*arrays, w, m, v, *kept)


def _rows(g):
    return g.reshape(-1, g.shape[-1])


def _row_blocks(dw):
    k, n = dw.shape
    return dw.reshape(N_DEV, k // N_DEV, n)


def _pack_rows(rows, width):
    out = None
    for i, r in enumerate(rows):
        r = r.reshape(1, -1).astype(F32)
        r = jnp.pad(r, ((i, 8 - 1 - i), (0, width - r.shape[1])))
        out = r if out is None else out + r
    return out


def _mixer_fwd(x, gain, w_in, w_out, cos, sin, seq, groups, tag, sink=None, exchanges=None):
    exchanges = exchanges or {}
    os, lses, got = [], [], {}
    qkvs, hs, got["proj"] = _qkv_proj(x, gain, w_in, cos, sin, seq, [dil for dil, _ in groups], tag,
                                      exchange=exchanges.get("proj", ()))
    for gi, (dil, w) in enumerate(groups):
        o, lse, got[gi] = _attn_fwd(qkvs[gi], w, f"{tag}{gi}", sink=sink, exchange=exchanges.get(gi, ()))
        os.append(o)
        lses.append(lse)
    y, o, lses = _out_proj(x, os, lses, [dl for dl, _ in groups], w_out, seq, tag)
    return y, (qkvs, hs, o, lses), got


def _mixer_bwd(dy, x_in, gain, w_in, w_out, saved, cos, sin, seq, groups, tag, sink=None, exchanges=None, scatter_dw_out=False):
    qkvs, hs, o, lses = saved
    t, d = x_in.shape
    dils = [dl for dl, _ in groups]
    lse_tokens = lses[0].reshape(t, LANES) if sink is not None else None
    dw_out = _tn_matmul(o, dy, f"dw_out_{tag}")
    dos, dls, dsink, early = _attn_out_bwd(dy, w_out, o, dils, seq, tag, lse=lse_tokens, sink=sink,
                                           exchange=_to_send([dw_out]) if scatter_dw_out else ())
    if scatter_dw_out:
        (dw_out,) = early
    exchanges = exchanges or {}
    dps, got = [], {}
    for gi, (dil, w) in enumerate(groups):
        dp, got[gi] = _attn_bwd(qkvs[gi], dos[gi], lses[gi], dls[gi], _tables_by_residue(cos, seq, dil),
                                _tables_by_residue(sin, seq, dil), w, f"{tag}{gi}", exchange=exchanges.get(gi, ()))
        dps.append(dp)
    dx, dgain = _qkv_bwd(dy, x_in, gain, w_in, dps, dils, seq, tag)
    dw_in = None
    for gi in range(len(groups)):
        dw_in = _tn_matmul(dps[gi].reshape(t, QKV_W), hs[gi].reshape(t, d), f"dw_in_{tag}{gi}", into=dw_in, row_block=gi,
                           row_blocks=len(groups))
    return dx, dw_in, dw_out, dgain, dsink, got


def _ffn_layer_bwd(dy, x_in, gain, saved, wg, wu, wd, tag, exchange=()):
    g, u, act, h = saved
    dx, dg, du, dgain, got = _ffn_bwd(dy, x_in, gain, g, u, wg, wu, wd, tag, exchange=exchange)
    dwd = _tn_matmul(act, dy, f"dw_down_{tag}")
    dwg = _tn_matmul(dg, h, f"dw_gate_{tag}")
    dwu = _tn_matmul(du, h, f"dw_up_{tag}")
    return dx, dwg, dwu, dwd, dgain, got


def _to_send(dws):
    return [(_row_blocks(g), False) for g in dws]


def kernel(x, a_w_in, a_sink, a_w_out, b_w_in, b_w_out, norm_mix, norm_ffn, w_gate, w_up, w_down, final_norm, loss_target, m_a_w_in, m_a_sink, m_a_w_out, m_b_w_in, m_b_w_out, m_norm_mix, m_norm_ffn, m_w_gate, m_w_up, m_w_down, m_final_norm, v_a_w_in, v_a_sink, v_a_w_out, v_b_w_in, v_b_w_out, v_norm_mix, v_norm_ffn, v_w_gate, v_w_up, v_w_down, v_final_norm):
    bl, seq, d = x.shape
    t = bl * seq
    xf = x.reshape(t, d)
    target = loss_target.reshape(t, d)
    cos, sin = _rope_tables(seq)
    groups_a = [(1, ATTN_HALF_WINDOW)]
    groups_b = [(dil, window // 2 // dil) for window, dil in DILATED_GROUPS]

    def flip(w_):
        return jnp.swapaxes(w_, -1, -2)

    a_w_in, m_a_w_in, v_a_w_in, b_w_in, m_b_w_in, v_b_w_in = map(flip, (a_w_in, m_a_w_in, v_a_w_in, b_w_in, m_b_w_in, v_b_w_in))
    w_gate, m_w_gate, v_w_gate, w_up, m_w_up, v_w_up = map(flip, (w_gate, m_w_gate, v_w_gate, w_up, m_w_up, v_w_up))

    def shard(w_, layer):
        return (w_[layer].astype(BF16), True)

    wa_in, wa_out = map(_rows, _exchange_now([shard(a_w_in, 0), shard(a_w_out, 0)], "gather_first"))

    x1_0, saved_a, got = _mixer_fwd(xf, norm_mix[0:1], wa_in, wa_out, cos, sin, seq, groups_a, "a", sink=a_sink[0],
                                    exchanges={"proj": [shard(w_down, 0)], 0: [shard(w_gate, 0), shard(w_up, 0)]})
    wg0, wu0, wd0 = map(_rows, got[0] + got["proj"])
    x2_0, *saved_0, got = _ffn_fwd(x1_0, norm_ffn[0:1], wg0, wu0, wd0, "0", exchange=[shard(b_w_in, 0), shard(b_w_out, 0)])
    wb_in, wb_out = map(_rows, got)
    x1_1, saved_b, got = _mixer_fwd(x2_0, norm_mix[1:2], wb_in, wb_out, cos, sin, seq, groups_b, "b",
                                    exchanges={0: [shard(w_gate, 1)], 1: [shard(w_up, 1)], 2: [shard(w_down, 1)]})
    wg1, wu1, wd1 = map(_rows, got[0] + got[1] + got[2])
    dy, *saved_1, loss_part, d_final, _ = _ffn_fwd(x1_1, norm_ffn[1:2], wg1, wu1, wd1, "1",
                                                   loss_head=(final_norm.reshape(1, d), target))

    dy, dwg1, dwu1, dwd1, d_nf1, _ = _ffn_layer_bwd(dy, x1_1, norm_ffn[1:2], saved_1, wg1, wu1, wd1, "1")
    dy, dwb_in, dwb_out, d_nm1, _, got = _mixer_bwd(
        dy, x2_0, norm_mix[1:2], wb_in, wb_out, saved_b, cos, sin, seq, groups_b, "b",
        exchanges={0: _to_send([dwg1, dwd1]), 1: _to_send([dwu1])})
    (r_g1, r_d1), (r_u1,) = got[0], got[1]
    dy, dwg0, dwu0, dwd0, d_nf0, (r_b_in, r_b_out) = _ffn_layer_bwd(
        dy, x1_0, norm_ffn[0:1], saved_0, wg0, wu0, wd0, "0", exchange=_to_send([dwb_in, dwb_out]))
    dy, dwa_in, r_a_out, d_nm0, d_sink, got = _mixer_bwd(
        dy, xf, norm_mix[0:1], wa_in, wa_out, saved_a, cos, sin, seq, groups_a, "a", sink=a_sink[0],
        exchanges={0: _to_send([dwg0, dwu0, dwd0])}, scatter_dw_out=True)
    r_g0, r_u0, r_d0 = got[0]
    (r_a_in,) = _exchange_now(_to_send([dwa_in]), "scatter_last")
    grad_x = dy.reshape(bl, seq, d)

    def update(received, w_, m_, v_, name):
        out = None
        for layer in reversed(range(len(received))):
            out = _adamw([(received[layer], src) for src in range(N_DEV)], w_, m_, v_, f"adamw_{name}{layer}", layer=layer, into=out)
        return out

    u_a_in = update([r_a_in], a_w_in, m_a_w_in, v_a_w_in, "a_in")
    u_a_out = update([r_a_out], a_w_out, m_a_w_out, v_a_w_out, "a_out")
    u_b_in = update([r_b_in], b_w_in, m_b_w_in, v_b_w_in, "b_in")
    u_b_out = update([r_b_out], b_w_out, m_b_w_out, v_b_w_out, "b_out")
    u_gate = update([r_g0, r_g1], w_gate, m_w_gate, v_w_gate, "gate")
    u_up = update([r_u0, r_u1], w_up, m_w_up, v_w_up, "up")
    u_down = update([r_d0, r_d1], w_down, m_w_down, v_w_down, "down")

    small = _pack_rows([d_nm0, d_nm1, d_nf0, d_nf1, d_final, d_sink, loss_part], d)
    total = _all_reduce_small(small)
    small_w = _pack_rows([norm_mix[0], norm_mix[1], norm_ffn[0], norm_ffn[1], final_norm, a_sink], d)
    small_m = _pack_rows([m_norm_mix[0], m_norm_mix[1], m_norm_ffn[0], m_norm_ffn[1], m_final_norm, m_a_sink], d)
    small_v = _pack_rows([v_norm_mix[0], v_norm_mix[1], v_norm_ffn[0], v_norm_ffn[1], v_final_norm, v_a_sink], d)
    u_small = _adamw([total], small_w, small_m, small_v, "adamw_small")
    loss = total[6, 0]

    outs = []
    for k in range(4):
        sm = u_small[k]
        outs += [flip(u_a_in[k]), sm[5:6, :N_HEADS], u_a_out[k], flip(u_b_in[k]), u_b_out[k], sm[0:2], sm[2:4],
                 flip(u_gate[k]), flip(u_up[k]), u_down[k], sm[4]]
    return (loss, grad_x, *outs)
```

```python
import functools
import math

import jax
import jax.numpy as jnp
from jax import lax
from jax.experimental import pallas as pl
from jax.experimental.pallas import tpu as pltpu

F32 = jnp.float32
BF16 = jnp.bfloat16

HEAD_DIM = 64
N_HEADS = 16
N_KV = 4
GRP = N_HEADS // N_KV
Q_W = N_HEADS * HEAD_DIM
KV_W = N_KV * HEAD_DIM
QKV_W = Q_W + 2 * KV_W
ATTN_HALF_WINDOW = 128
DILATED_GROUPS = ((128, 1), (512, 4), (2048, 16))
ROPE_THETA = 10000.0
RMS_EPS = 1e-6
NEG_INF = -1e30
SCALE = 1.0 / math.sqrt(HEAD_DIM)

ADAM_LR = 0.001
ADAM_B1 = 0.9
ADAM_B2 = 0.999
ADAM_EPS = 1e-08
ADAM_WD = 0.01
ADAM_STEP = 10

LANES = 128
VMEM_LIMIT = 56 * 1024 * 1024
QUERY_BLOCK = 128
N_DEV = 8
MESH = pl.DeviceIdType.MESH

NT = (((1,), (1,)), ((), ()))
TN = (((0,), (0,)), ((), ()))


def _params(*sem):
    return pltpu.CompilerParams(dimension_semantics=tuple(sem) if sem else None, vmem_limit_bytes=VMEM_LIMIT)


def _resident(shape):
    return pl.BlockSpec(shape, lambda *_: (0,) * len(shape), pipeline_mode=pl.Buffered(1))


def _rope_tables(seq):
    inv_freq = 1.0 / (ROPE_THETA ** (jnp.arange(0, HEAD_DIM, 2, dtype=F32) / HEAD_DIM))
    ang = jnp.arange(seq, dtype=F32)[:, None] * inv_freq[None, :]
    cos, sin = jnp.cos(ang), jnp.sin(ang)
    return jnp.tile(cos, (1, 4)), jnp.concatenate([-sin, sin, -sin, sin], axis=1)


def _rope(t, cos, sin_signed):
    lane = lax.broadcasted_iota(jnp.int32, t.shape, 1)
    first = (lane & (HEAD_DIM // 2)) == 0
    swapped = jnp.where(first, pltpu.roll(t, LANES - HEAD_DIM // 2, 1), pltpu.roll(t, HEAD_DIM // 2, 1))
    return t * cos + swapped * sin_signed


def _rms(x):
    return lax.rsqrt(jnp.mean(x * x, axis=-1, keepdims=True) + RMS_EPS)


def _rms_bwd(dh, x, gain):
    r = _rms(x)
    xhat = x * r
    dxh = dh * gain
    dx = r * (dxh - xhat * jnp.mean(dxh * xhat, axis=-1, keepdims=True))
    return dx, xhat


def _accumulate(ref, value, first):
    @pl.when(first)
    def _():
        ref[...] = jnp.zeros_like(ref)

    ref[...] += value


def _tile_rows(seq):
    return min(512, seq)


def _res_shape(bl, seq, dil, c):
    ts = _tile_rows(seq)
    return (bl, dil, seq // ts, ts // dil, c)


def _res_spec(seq, dil, c):
    ts = _tile_rows(seq)
    per_seq = seq // ts
    return pl.BlockSpec((None, dil, None, ts // dil, c), lambda i: (i // per_seq, 0, i % per_seq, 0, 0))


def _seq_view(a):
    bl, dil, tiles, n, c = a.shape
    return a.reshape(bl * dil, tiles * n, c)


def _stage(ts, c):
    return pltpu.VMEM((c // LANES, ts, LANES), F32)


def _split_rows(val, stage_ref, dil):
    if dil == 1:
        return [val]
    ts, c = val.shape
    n, nc = ts // dil, c // LANES
    for k in range(nc):
        stage_ref[k] = val[:, k * LANES:(k + 1) * LANES]
    return [jnp.concatenate([stage_ref[k, pl.ds(r, n, stride=dil), :] for k in range(nc)], axis=1) for r in range(dil)]


def _merge_rows(parts, stage_ref, dil):
    if dil == 1:
        return parts[0]
    n, c = parts[0].shape
    nc = c // LANES
    for r, part in enumerate(parts):
        for k in range(nc):
            stage_ref[k, pl.ds(r, n, stride=dil), :] = part[:, k * LANES:(k + 1) * LANES]
    return jnp.concatenate([stage_ref[k] for k in range(nc)], axis=1)


def _tables_tiled(table, seq, dil):
    ts = _tile_rows(seq)
    return table.reshape(seq // ts, ts // dil, dil, LANES).transpose(0, 2, 1, 3).reshape(seq, LANES)


def _tables_by_residue(table, seq, dil):
    return table.reshape(seq // dil, dil, LANES).transpose(1, 0, 2)


def _qkv_proj(x, gain, w, cos, sin, seq, dils, tag, exchange=()):
    t, d = x.shape
    ts = _tile_rows(seq)
    per_seq = seq // ts
    ng = len(dils)
    tables = [t_ for dil in dils for t_ in (_tables_tiled(cos, seq, dil), _tables_tiled(sin, seq, dil))]

    def body(x_ref, g_ref, w_ref, *refs):
        table_refs, o_refs, h_refs, stage_ref = refs[:2 * ng], refs[2 * ng:3 * ng], refs[3 * ng:4 * ng], refs[4 * ng]
        xv = x_ref[...]
        h_tokens = xv * _rms(xv) * g_ref[...]
        for gi, dil in enumerate(dils):
            n = ts // dil
            h = jnp.concatenate(_split_rows(h_tokens, stage_ref, dil), axis=0).astype(BF16)
            for r in range(dil):
                h_refs[gi][r] = h[r * n:(r + 1) * n]
            acc = lax.dot_general(h, w_ref[gi * QKV_W:(gi + 1) * QKV_W, :], NT, preferred_element_type=F32)
            c, s = table_refs[2 * gi][...], table_refs[2 * gi + 1][...]
            for j in range(QKV_W // LANES):
                cols = slice(j * LANES, (j + 1) * LANES)
                val = acc[:, cols]
                if j < (Q_W + KV_W) // LANES:
                    val = _rope(val, c, s)
                if j < Q_W // LANES:
                    val = val * SCALE
                val = val.astype(BF16)
                for r in range(dil):
                    o_refs[gi][r, :, cols] = val[r * n:(r + 1) * n]

    table = pl.BlockSpec((ts, LANES), lambda i: (i % per_seq, 0))
    outs, exchanged = _hosted_call(
        body, exchange, name=f"qkv_proj_{tag}", grid=(t // ts,),
        in_specs=[pl.BlockSpec((ts, d), lambda i: (i, 0)), _resident((1, d)), _resident(w.shape)] + [table] * (2 * ng),
        out_specs=[_res_spec(seq, dil, QKV_W) for dil in dils] + [_res_spec(seq, dil, d) for dil in dils],
        out_shape=[jax.ShapeDtypeStruct(_res_shape(t // seq, seq, dil, QKV_W), BF16) for dil in dils]
                  + [jax.ShapeDtypeStruct(_res_shape(t // seq, seq, dil, d), BF16) for dil in dils],
        scratch_shapes=[_stage(ts, d)], semantics=("parallel",), args=(x, gain, w, *tables))
    return outs[:ng], outs[ng:], exchanged


def _band(bq, wk):
    return lax.broadcasted_iota(jnp.int32, (bq, wk), 0) - lax.broadcasted_iota(jnp.int32, (bq, wk), 1)


def _swap_halves(src_ref, base, dst_ref):
    for c in range(KV_W // LANES):
        dst_ref[c] = pltpu.roll(src_ref[:, base + c * LANES:base + (c + 1) * LANES], HEAD_DIM, 1)


def _pair_operand(src_ref, swapped_ref, base, kv, rows):
    c = kv // 2
    chunk, swapped = src_ref[rows, base + c * LANES:base + (c + 1) * LANES], swapped_ref[c, rows, :]
    lo = lax.broadcasted_iota(jnp.int32, chunk.shape, 1) < HEAD_DIM
    zero = jnp.zeros_like(chunk)
    if kv % 2 == 0:
        return jnp.concatenate([jnp.where(lo, chunk, zero), jnp.where(lo, zero, swapped)], axis=0)
    return jnp.concatenate([jnp.where(lo, swapped, zero), jnp.where(lo, zero, chunk)], axis=0)


def _over_keys(col, wk):
    if wk % LANES:
        return jnp.broadcast_to(col, (col.shape[0], wk))
    wide = jnp.broadcast_to(col, (col.shape[0], LANES))
    return wide if wk == LANES else jnp.concatenate([wide] * (wk // LANES), axis=1)


def _per_step(dil, length):
    return max(1, min(dil, 512 // length))


def _key_rows(bq, w, length):
    return min(bq + 2 * w, length)


def _window(i, bq, w, wk, length):
    q0 = pl.multiple_of(i * bq, bq)
    k0 = pl.multiple_of(jnp.clip(q0 - w, 0, length - wk), min(w, bq))
    return q0, k0


def _attn_fwd(qkv, w, tag, sink=None, exchange=()):
    shape = qkv.shape
    rows_all = _seq_view(qkv)
    nseq, length, _ = rows_all.shape
    bq = min(QUERY_BLOCK, length)
    wk = _key_rows(bq, w, length)
    nb = length // bq
    has_sink = sink is not None
    per_step = _per_step(shape[1], length)

    def body(*refs):
        sink_ref = refs[1] if has_sink else None
        kk_ref, vv_ref = refs[-2:]
        for sub in range(per_step):
            one(refs[0].at[sub], refs[-4].at[sub], refs[-3].at[sub], sink_ref, kk_ref, vv_ref)

    def one(qkv_ref, o_ref, lse_ref, sink_ref, kk_ref, vv_ref):
        _swap_halves(qkv_ref, Q_W, kk_ref)
        _swap_halves(qkv_ref, Q_W + KV_W, vv_ref)
        band = _band(bq, wk)
        lane = lax.broadcasted_iota(jnp.int32, (bq, LANES), 1)
        lo = lane < HEAD_DIM

        def block(i, carry):
            q0, k0 = _window(i, bq, w, wk, length)
            valid = jnp.abs(band + (q0 - k0)) <= w
            rows, krows = pl.ds(q0, bq), pl.ds(k0, wk)
            lse_tile = jnp.zeros((bq, LANES), F32)
            for kv in range(N_KV):
                heads = [(kv * GRP + h, h % 2) for h in range(GRP)]
                qp = [qkv_ref[rows, (kv * 2 + j) * LANES:(kv * 2 + j + 1) * LANES] for j in range(GRP // 2)]
                k2 = _pair_operand(qkv_ref, kk_ref, Q_W, kv, krows)
                v2 = _pair_operand(qkv_ref, vv_ref, Q_W + KV_W, kv, krows)
                sc2 = [lax.dot_general(q_, k2, NT, preferred_element_type=F32) for q_ in qp]
                sc = [jnp.where(valid, s_[:, half * wk:(half + 1) * wk], NEG_INF) for s_ in sc2 for half in range(2)]
                m = [jnp.max(s_, axis=-1, keepdims=True) for s_ in sc]
                if has_sink:
                    m = [jnp.maximum(m_, sink_ref[hd]) for m_, (hd, _) in zip(m, heads)]
                mb = [jnp.broadcast_to(m_, (bq, LANES)) for m_ in m]
                p = [jnp.exp(s_ - _over_keys(m_, wk)) for s_, m_ in zip(sc, m)]
                den = [jnp.sum(p_, axis=-1, keepdims=True) for p_ in p]
                if has_sink:
                    den = [d_ + jnp.exp(sink_ref[hd] - m_) for d_, m_, (hd, _) in zip(den, m, heads)]
                inv = [jnp.broadcast_to(1.0 / d_, (bq, LANES)) for d_ in den]
                pb = [p_.astype(BF16) for p_ in p]
                for j in range(GRP // 2):
                    o = jnp.dot(jnp.concatenate([pb[2 * j], pb[2 * j + 1]], axis=1), v2, preferred_element_type=F32)
                    o = o * jnp.where(lo, inv[2 * j], inv[2 * j + 1])
                    o_ref[rows, (kv * 2 + j) * LANES:(kv * 2 + j + 1) * LANES] = o.astype(BF16)
                for h, (hd, _) in enumerate(heads):
                    lse_tile = jnp.where(lane == hd, mb[h] - jnp.log(inv[h]), lse_tile)
            lse_ref[rows, :] = lse_tile
            return carry

        lax.fori_loop(0, nb, block, 0)

    def seq_block(c):
        return pl.BlockSpec((per_step, length, c), lambda i: (i, 0, 0))

    args = [rows_all]
    in_specs = [seq_block(QKV_W)]
    if has_sink:
        args.append(sink)
        in_specs.append(pl.BlockSpec(memory_space=pltpu.SMEM))
    (o, lse), exchanged = _hosted_call(
        body, exchange, name=f"attn_fwd_{tag}", grid=(nseq // per_step,), in_specs=in_specs,
        out_specs=[seq_block(Q_W), seq_block(LANES)],
        out_shape=[jax.ShapeDtypeStruct((nseq, length, Q_W), BF16), jax.ShapeDtypeStruct((nseq, length, LANES), F32)],
        scratch_shapes=[pltpu.VMEM((KV_W // LANES, length, LANES), BF16), pltpu.VMEM((KV_W // LANES, length, LANES), BF16)],
        semantics=("parallel",), args=args)
    return o.reshape(shape[:-1] + (Q_W,)), lse.reshape(shape[:-1] + (LANES,)), exchanged


def _head_expand():
    return (jnp.arange(LANES)[:, None] == jnp.arange(Q_W)[None, :] // HEAD_DIM).astype(BF16)


def _out_proj(x, os, lses, dils, w, seq, tag):
    t, d = x.shape
    ts = _tile_rows(seq)
    ng = len(os)
    bl = t // seq
    if ng == 1:
        def body1(x_ref, o_ref, w_ref, y_ref):
            y_ref[...] = x_ref[...] + jnp.dot(o_ref[...], w_ref[...], preferred_element_type=F32)

        row = pl.BlockSpec((ts, d), lambda i: (i, 0))
        o = os[0].reshape(t, Q_W)
        y = pl.pallas_call(
            body1, name=f"out_proj_{tag}", grid=(t // ts,), in_specs=[row, row, _resident(w.shape)], out_specs=row,
            out_shape=jax.ShapeDtypeStruct((t, d), F32), compiler_params=_params("parallel"),
        )(x, o, w)
        return y, o, [lses[0]]

    def body(*refs):
        x_ref, w_ref, e_ref = refs[:3]
        o_refs, l_refs = refs[3:3 + ng], refs[3 + ng:3 + 2 * ng]
        y_ref, om_ref = refs[3 + 2 * ng:5 + 2 * ng]
        lt_refs = refs[5 + 2 * ng:5 + 3 * ng]
        wide_ref, narrow_ref = refs[5 + 3 * ng:]
        ls = [_merge_rows([l_refs[g][r] for r in range(dils[g])], narrow_ref, dils[g]) for g in range(ng)]
        mx = functools.reduce(jnp.maximum, ls)
        tot = mx + jnp.log(functools.reduce(lambda a, b: a + b, [jnp.exp(l_ - mx) for l_ in ls]))
        e = e_ref[...]
        o = None
        for g in range(ng):
            wt = jnp.exp(ls[g] - tot)
            hi = wt.astype(BF16)
            lo = (wt - hi.astype(F32)).astype(BF16)
            wide = jnp.dot(hi, e, preferred_element_type=F32) + jnp.dot(lo, e, preferred_element_type=F32)
            term = wide * _merge_rows([o_refs[g][r].astype(F32) for r in range(dils[g])], wide_ref, dils[g])
            o = term if o is None else o + term
        ob = o.astype(BF16)
        om_ref[...] = ob
        y_ref[...] = x_ref[...] + jnp.dot(ob, w_ref[...], preferred_element_type=F32)
        for g in range(ng):
            for r, part in enumerate(_split_rows(tot, narrow_ref, dils[g])):
                lt_refs[g][r] = part

    row = pl.BlockSpec((ts, d), lambda i: (i, 0))
    e = _head_expand()
    outs = pl.pallas_call(
        body, name=f"out_proj_{tag}", grid=(t // ts,),
        in_specs=[row, _resident(w.shape), _resident(e.shape)] + [_res_spec(seq, dl, Q_W) for dl in dils]
                 + [_res_spec(seq, dl, LANES) for dl in dils],
        out_specs=[row, pl.BlockSpec((ts, Q_W), lambda i: (i, 0))] + [_res_spec(seq, dl, LANES) for dl in dils],
        out_shape=[jax.ShapeDtypeStruct((t, d), F32), jax.ShapeDtypeStruct((t, Q_W), BF16)]
                  + [jax.ShapeDtypeStruct(_res_shape(bl, seq, dl, LANES), F32) for dl in dils],
        scratch_shapes=[_stage(ts, Q_W), _stage(ts, LANES)],
        compiler_params=_params("parallel"),
    )(x, w, e, *os, *lses)
    return outs[0], outs[1], list(outs[2:])


def _sigmoid(g):
    return 1.0 / (1.0 + jnp.exp(-g))


def _ffn_fwd(x, gain, wg, wu, wd, tag, exchange=(), loss_head=None):
    t, d = x.shape
    f = wd.shape[0]
    tm = min(256, t)
    has_loss = loss_head is not None

    def body(*refs):
        x_ref, gain_ref, wg_ref, wu_ref, wd_ref = refs[:5]
        y_ref, g_ref, u_ref, a_ref, h_ref = refs[-7:-2] if has_loss else refs[-5:]
        xv = x_ref[...]
        h = (xv * _rms(xv) * gain_ref[...]).astype(BF16)
        h_ref[...] = h
        g = lax.dot_general(h, wg_ref[...], NT, preferred_element_type=F32)
        u = lax.dot_general(h, wu_ref[...], NT, preferred_element_type=F32)
        g_ref[...] = g.astype(BF16)
        u_ref[...] = u.astype(BF16)
        a = (g * _sigmoid(g) * u).astype(BF16)
        a_ref[...] = a
        y = xv + jnp.dot(a, wd_ref[...], preferred_element_type=F32)
        if not has_loss:
            y_ref[...] = y
            return
        head_ref, target_ref, loss_ref, dhead_ref = refs[5], refs[6], refs[-2], refs[-1]
        head = head_ref[...]
        yhat = y * _rms(y)
        err = yhat * head - target_ref[...]
        dout = err * (1.0 / d)
        y_ref[...] = _rms_bwd(dout, y, head)[0]
        first = pl.program_id(0) == 0
        part = 0.5 * jnp.sum(jnp.mean(err * err, axis=-1, keepdims=True), axis=0, keepdims=True)
        _accumulate(loss_ref, jnp.broadcast_to(part, loss_ref.shape), first)
        _accumulate(dhead_ref, jnp.sum(dout * yhat, axis=0, keepdims=True), first)

    row = pl.BlockSpec((tm, d), lambda i: (i, 0))
    wide = pl.BlockSpec((tm, f), lambda i: (i, 0))
    in_specs = [row, _resident((1, d)), _resident(wg.shape), _resident(wu.shape), _resident(wd.shape)]
    out_specs = [row, wide, wide, wide, row]
    out_shape = [jax.ShapeDtypeStruct((t, d), F32)] + [jax.ShapeDtypeStruct((t, f), BF16)] * 3 + [jax.ShapeDtypeStruct((t, d), BF16)]
    if has_loss:
        in_specs += [_resident((1, d)), row]
        out_specs += [pl.BlockSpec((1, LANES), lambda i: (0, 0)), pl.BlockSpec((1, d), lambda i: (0, 0))]
        out_shape += [jax.ShapeDtypeStruct((1, LANES), F32), jax.ShapeDtypeStruct((1, d), F32)]
    outs, exchanged = _hosted_call(
        body, exchange, name=f"ffn_fwd_{tag}", grid=(t // tm,), in_specs=in_specs, out_specs=out_specs, out_shape=out_shape,
        scratch_shapes=[], semantics=("arbitrary" if has_loss else "parallel",),
        args=(x, gain, wg, wu, wd) + (tuple(loss_head) if has_loss else ()))
    return (*outs, exchanged)


def _ffn_bwd(dy, x, gain, g, u, wg, wu, wd, tag, exchange=()):
    t, d = x.shape
    f = wd.shape[0]
    tm = min(256, t)

    def body(dy_ref, x_ref, gain_ref, g_ref, u_ref, wg_ref, wu_ref, wd_ref, dx_ref, dg_ref, du_ref, dgain_ref):
        dyv = dy_ref[...]
        da = lax.dot_general(dyv.astype(BF16), wd_ref[...], NT, preferred_element_type=F32)
        gv, uv = g_ref[...].astype(F32), u_ref[...].astype(F32)
        sg = _sigmoid(gv)
        act = gv * sg
        du = (da * act).astype(BF16)
        dg = (da * uv * (sg * (1.0 + gv * (1.0 - sg)))).astype(BF16)
        du_ref[...] = du
        dg_ref[...] = dg
        dh = jnp.dot(dg, wg_ref[...], preferred_element_type=F32) + jnp.dot(du, wu_ref[...], preferred_element_type=F32)
        xv, gain_v = x_ref[...], gain_ref[...]
        dx, xhat = _rms_bwd(dh, xv, gain_v)
        dx_ref[...] = dyv + dx
        _accumulate(dgain_ref, jnp.sum(dh * xhat, axis=0, keepdims=True), pl.program_id(0) == 0)

    row = pl.BlockSpec((tm, d), lambda i: (i, 0))
    wide = pl.BlockSpec((tm, f), lambda i: (i, 0))
    outs, exchanged = _hosted_call(
        body, exchange, name=f"ffn_bwd_{tag}", grid=(t // tm,),
        in_specs=[row, row, _resident((1, d)), wide, wide, _resident(wg.shape), _resident(wu.shape), _resident(wd.shape)],
        out_specs=[row, wide, wide, pl.BlockSpec((1, d), lambda i: (0, 0))],
        out_shape=[jax.ShapeDtypeStruct((t, d), F32), jax.ShapeDtypeStruct((t, f), BF16), jax.ShapeDtypeStruct((t, f), BF16),
                   jax.ShapeDtypeStruct((1, d), F32)],
        scratch_shapes=[], semantics=("arbitrary",), args=(dy, x, gain, g, u, wg, wu, wd))
    return (*outs, exchanged)


def _tn_matmul(a, b, name, into=None, row_block=0, row_blocks=1):
    t, k = a.shape
    n = b.shape[1]
    tk = k // 2 if (k // 2) % LANES == 0 else k
    tt = min(2048, t)
    first = row_block * (k // tk)

    def body(a_ref, b_ref, *rest):
        o_ref, acc_ref = rest[-2:]
        prod = lax.dot_general(a_ref[...].astype(BF16), b_ref[...].astype(BF16), TN, preferred_element_type=F32)
        j = pl.program_id(1)

        @pl.when(j == 0)
        def _():
            acc_ref[...] = prod

        @pl.when(j > 0)
        def _():
            acc_ref[...] += prod

        @pl.when(j == pl.num_programs(1) - 1)
        def _():
            o_ref[...] = acc_ref[...].astype(BF16)

    return pl.pallas_call(
        body, name=name, grid=(k // tk, t // tt),
        in_specs=[pl.BlockSpec((tt, tk), lambda i, j: (j, i)), pl.BlockSpec((tt, n), lambda i, j: (j, 0))]
                 + ([ANY] if into is not None else []),
        out_specs=pl.BlockSpec((tk, n), lambda i, j: (first + i, 0)),
        out_shape=jax.ShapeDtypeStruct((row_blocks * k, n), BF16),
        scratch_shapes=[pltpu.VMEM((tk, n), F32)],
        input_output_aliases={2: 0} if into is not None else {},
        compiler_params=_params("parallel", "arbitrary"),
    )(a, b, *([into] if into is not None else []))


def _attn_out_bwd(dx, w, o, dils, seq, tag, lse=None, sink=None, exchange=()):
    t, d = dx.shape
    ts = _tile_rows(seq)
    bl = t // seq
    ng = len(dils)
    has_sink = sink is not None
    expand = _head_expand().T

    def body(*refs):
        refs = list(refs)
        dx_ref, w_ref, o_ref, e_ref = refs[:4]
        refs = refs[4:]
        lse_ref, sink_ref = (refs.pop(0), refs.pop(0)) if has_sink else (None, None)
        do_refs, dl_refs = refs[:ng], refs[ng:2 * ng]
        refs = refs[2 * ng:]
        dsink_ref = refs.pop(0) if has_sink else None
        dof_ref, dlf_ref = refs
        do = lax.dot_general(dx_ref[...].astype(BF16), w_ref[...], NT, preferred_element_type=F32)
        prod = do * o_ref[...].astype(F32)
        hi = prod.astype(BF16)
        lo = (prod - hi.astype(F32)).astype(BF16)
        e = e_ref[...]
        dl = jnp.dot(hi, e, preferred_element_type=F32) + jnp.dot(lo, e, preferred_element_type=F32)
        for g in range(ng):
            for r, part in enumerate(_split_rows(do, dof_ref, dils[g])):
                do_refs[g][r] = part.astype(BF16)
            for r, part in enumerate(_split_rows(dl, dlf_ref, dils[g])):
                dl_refs[g][r] = part
        if has_sink:
            part = -jnp.exp(sink_ref[...] - lse_ref[...]) * dl
            _accumulate(dsink_ref, jnp.sum(part, axis=0, keepdims=True), pl.program_id(0) == 0)

    row = pl.BlockSpec((ts, d), lambda i: (i, 0))
    narrow = pl.BlockSpec((ts, LANES), lambda i: (i, 0))
    args = [dx, w, o, expand]
    in_specs = [row, _resident(w.shape), pl.BlockSpec((ts, Q_W), lambda i: (i, 0)), _resident(expand.shape)]
    if has_sink:
        args += [lse, jnp.pad(sink.reshape(1, N_HEADS), ((0, 0), (0, LANES - N_HEADS)))]
        in_specs += [narrow, _resident((1, LANES))]
    out_specs = [_res_spec(seq, dl, Q_W) for dl in dils] + [_res_spec(seq, dl, LANES) for dl in dils]
    out_shape = ([jax.ShapeDtypeStruct(_res_shape(bl, seq, dl, Q_W), BF16) for dl in dils]
                 + [jax.ShapeDtypeStruct(_res_shape(bl, seq, dl, LANES), F32) for dl in dils])
    if has_sink:
        out_specs.append(pl.BlockSpec((1, LANES), lambda i: (0, 0)))
        out_shape.append(jax.ShapeDtypeStruct((1, LANES), F32))
    outs, exchanged = _hosted_call(
        body, exchange, name=f"attn_out_bwd_{tag}", grid=(t // ts,), in_specs=in_specs, out_specs=out_specs, out_shape=out_shape,
        scratch_shapes=[_stage(ts, Q_W), _stage(ts, LANES)], semantics=("arbitrary" if has_sink else "parallel",), args=args)
    return list(outs[:ng]), list(outs[ng:2 * ng]), (outs[2 * ng] if has_sink else None), exchanged


def _attn_bwd(qkv, do, lse, delta, cos, sin, w, tag, exchange=()):
    shape = qkv.shape
    dil = shape[1]
    rows_all = _seq_view(qkv)
    nseq, length, _ = rows_all.shape
    bq = min(QUERY_BLOCK, length)
    wk = _key_rows(bq, w, length)
    nb = length // bq
    per_step = _per_step(dil, length)

    def body(*refs):
        for sub in range(per_step):
            one(*[ref.at[sub] for ref in refs[:7]], *refs[7:])

    def one(qkv_ref, do_ref, lse_ref, dl_ref, cos_ref, sin_ref, dp_ref, kk_ref, vv_ref, dk_ref, dv_ref):
        _swap_halves(qkv_ref, Q_W, kk_ref)
        _swap_halves(qkv_ref, Q_W + KV_W, vv_ref)
        dk_ref[...] = jnp.zeros_like(dk_ref)
        dv_ref[...] = jnp.zeros_like(dv_ref)
        band = _band(bq, wk)
        lo_q = lax.broadcasted_iota(jnp.int32, (bq, LANES), 1) < HEAD_DIM
        hi_q = jnp.logical_not(lo_q)

        def block(i, carry):
            q0, k0 = _window(i, bq, w, wk, length)
            valid = jnp.abs(band + (q0 - k0)) <= w
            rows, krows = pl.ds(q0, bq), pl.ds(k0, wk)
            c, sn = cos_ref[rows, :], -sin_ref[rows, :]
            lse_t, dl_t = lse_ref[rows, :], dl_ref[rows, :]
            for kv in range(N_KV):
                heads = [(kv * GRP + h, h % 2) for h in range(GRP)]
                cols = [slice((kv * 2 + j) * LANES, (kv * 2 + j + 1) * LANES) for j in range(GRP // 2)]
                qp = [qkv_ref[rows, cs] for cs in cols]
                dop = [do_ref[rows, cs] for cs in cols]
                k2 = _pair_operand(qkv_ref, kk_ref, Q_W, kv, krows)
                v2 = _pair_operand(qkv_ref, vv_ref, Q_W + KV_W, kv, krows)
                sc2 = [lax.dot_general(q_, k2, NT, preferred_element_type=F32) for q_ in qp]
                dp2 = [lax.dot_general(d_, v2, NT, preferred_element_type=F32) for d_ in dop]
                sc = [s_[:, half * wk:(half + 1) * wk] for s_ in sc2 for half in range(2)]
                dp = [d_[:, half * wk:(half + 1) * wk] for d_ in dp2 for half in range(2)]
                p = [jnp.exp(jnp.where(valid, s_, NEG_INF) - _over_keys(lse_t[:, hd:hd + 1], wk))
                     for s_, (hd, _) in zip(sc, heads)]
                ds = [(p_ * (dp_ - _over_keys(dl_t[:, hd:hd + 1], wk))).astype(BF16) for p_, dp_, (hd, _) in zip(p, dp, heads)]
                pb = [p_.astype(BF16) for p_ in p]
                for j in range(GRP // 2):
                    dq = jnp.dot(jnp.concatenate([ds[2 * j], ds[2 * j + 1]], axis=1), k2, preferred_element_type=F32) * SCALE
                    dp_ref[rows, cols[j]] = _rope(dq, c, sn).astype(BF16)
                zero = jnp.zeros((bq, LANES), BF16)
                q4 = jnp.concatenate([jnp.where(lo_q if h % 2 == 0 else hi_q, qp[h // 2], zero) for h in range(GRP)], axis=0)
                do4 = jnp.concatenate([jnp.where(lo_q if h % 2 == 0 else hi_q, dop[h // 2], zero) for h in range(GRP)], axis=0)
                dk_ref[kv, krows, :] += lax.dot_general(jnp.concatenate(ds, axis=0), q4, TN, preferred_element_type=F32)
                dv_ref[kv, krows, :] += lax.dot_general(jnp.concatenate(pb, axis=0), do4, TN, preferred_element_type=F32)
            return carry

        lax.fori_loop(0, nb, block, 0)
        lo = lax.broadcasted_iota(jnp.int32, (length, LANES), 1) < HEAD_DIM
        c, sn = cos_ref[...], -sin_ref[...]
        for ch in range(KV_W // LANES):
            halves = []
            for acc_ref in (dk_ref, dv_ref):
                even, odd = acc_ref[2 * ch], acc_ref[2 * ch + 1]
                even = even + pltpu.roll(even, HEAD_DIM, 1)
                odd = odd + pltpu.roll(odd, HEAD_DIM, 1)
                halves.append(jnp.where(lo, even, odd))
            dp_ref[:, Q_W + ch * LANES:Q_W + (ch + 1) * LANES] = _rope(halves[0], c, sn).astype(BF16)
            dp_ref[:, Q_W + KV_W + ch * LANES:Q_W + KV_W + (ch + 1) * LANES] = halves[1].astype(BF16)

    def seq_block(c):
        return pl.BlockSpec((per_step, length, c), lambda i: (i, 0, 0))

    table = pl.BlockSpec((per_step, length, LANES), lambda i: (i % (dil // per_step), 0, 0))
    (out,), exchanged = _hosted_call(
        body, exchange, name=f"attn_bwd_{tag}", grid=(nseq // per_step,),
        in_specs=[seq_block(QKV_W), seq_block(Q_W), seq_block(LANES), seq_block(LANES), table, table],
        out_specs=[seq_block(QKV_W)],
        out_shape=[jax.ShapeDtypeStruct((nseq, length, QKV_W), BF16)],
        scratch_shapes=[pltpu.VMEM((KV_W // LANES, length, LANES), BF16), pltpu.VMEM((KV_W // LANES, length, LANES), BF16),
                        pltpu.VMEM((N_KV, length, LANES), F32), pltpu.VMEM((N_KV, length, LANES), F32)],
        semantics=("parallel",), args=(rows_all, _seq_view(do), _seq_view(lse), _seq_view(delta), cos, sin))
    return out.reshape(shape), exchanged


def _qkv_bwd(dy, x, gain, w, dps, dils, seq, tag):
    t, d = x.shape
    ts = _tile_rows(seq)
    ng = len(dps)

    def body(dy_ref, x_ref, gain_ref, w_ref, *refs):
        dp_refs, (dx_ref, dgain_ref, stage_ref) = refs[:ng], refs[ng:]
        dh = None
        for gi in range(ng):
            dil = dils[gi]
            n = ts // dil
            dp = dp_refs[gi][0] if dil == 1 else jnp.concatenate([dp_refs[gi][r] for r in range(dil)], axis=0)
            part = jnp.dot(dp, w_ref[gi * QKV_W:(gi + 1) * QKV_W, :], preferred_element_type=F32)
            part = _merge_rows([part[r * n:(r + 1) * n] for r in range(dil)], stage_ref, dil)
            dh = part if dh is None else dh + part
        xv, gain_v = x_ref[...], gain_ref[...]
        dx, xhat = _rms_bwd(dh, xv, gain_v)
        dx_ref[...] = dy_ref[...] + dx
        _accumulate(dgain_ref, jnp.sum(dh * xhat, axis=0, keepdims=True), pl.program_id(0) == 0)

    row = pl.BlockSpec((ts, d), lambda i: (i, 0))
    return pl.pallas_call(
        body, name=f"qkv_bwd_{tag}", grid=(t // ts,),
        in_specs=[row, row, _resident((1, d)), _resident(w.shape)] + [_res_spec(seq, dl, QKV_W) for dl in dils],
        out_specs=[row, pl.BlockSpec((1, d), lambda i: (0, 0))],
        out_shape=[jax.ShapeDtypeStruct((t, d), F32), jax.ShapeDtypeStruct((1, d), F32)],
        scratch_shapes=[_stage(ts, d)], compiler_params=_params("arbitrary"),
    )(dy, x, gain, w, *dps)


ANY = pl.BlockSpec(memory_space=pl.ANY)


def _place():
    x, y, c = lax.axis_index("x"), lax.axis_index("y"), lax.axis_index("c")
    return x, y, c


def _exchange_steps(srcs, dsts, gather, send_sems, recv_sems, local_sems):
    x, y, c = _place()
    me, sibling = (x, y, c), (x, y, 1 - c)
    chips = [(1 - x, y), (x, 1 - y), (1 - x, 1 - y)]
    mine = 4 * x + 2 * y + c

    def slot(a, device):
        px, py, pc = device
        return dsts[a].at[4 * px + 2 * py + pc]

    def passes(a, k, block, to, src=None):
        rows = slot(a, block)
        return pltpu.make_async_remote_copy(src_ref=rows if src is None else src, dst_ref=rows, send_sem=send_sems.at[a, k],
                                            recv_sem=recv_sems.at[a, k], device_id=to, device_id_type=MESH)

    def scatters(a, k):
        peer = mine ^ k
        return pltpu.make_async_remote_copy(
            src_ref=srcs[a].at[peer], dst_ref=dsts[a].at[mine], send_sem=send_sems.at[a, k - 1], recv_sem=recv_sems.at[a, k - 1],
            device_id=(peer // 4, (peer // 2) % 2, peer % 2), device_id_type=MESH)

    def local(a):
        return pltpu.make_async_copy(srcs[a] if gather[a] else srcs[a].at[mine], dsts[a].at[mine], local_sems.at[a])

    def first_copies(a):
        if not gather[a]:
            return [scatters(a, k) for k in range(1, N_DEV)]
        return [passes(a, 0, me, sibling, src=srcs[a])] + [passes(a, 1 + j, me, (*chip, c), src=srcs[a]) for j, chip in enumerate(chips)]

    def start():
        for a in range(len(srcs)):
            local(a).start()
            for cp in first_copies(a):
                cp.start()

    def forward():
        for a in range(len(srcs)):
            if gather[a]:
                for j, chip in enumerate(chips):
                    passes(a, 1 + j, (*chip, c), me).wait_recv()
                    passes(a, 4 + j, (*chip, c), sibling).start()

    def finish():
        for a in range(len(srcs)):
            if gather[a]:
                passes(a, 0, sibling, me).wait_recv()
                for j, chip in enumerate(chips):
                    passes(a, 4 + j, (*chip, 1 - c), me).wait_recv()
                    passes(a, 4 + j, (*chip, c), sibling).wait_send()
                for cp in first_copies(a):
                    cp.wait_send()
            else:
                for cp in first_copies(a):
                    cp.wait()
            local(a).wait()

    return start, forward, finish


def _exchange_scratch(n):
    return [pltpu.SemaphoreType.DMA((n, N_DEV - 1)), pltpu.SemaphoreType.DMA((n, N_DEV - 1)), pltpu.SemaphoreType.DMA((n,))]


def _exchanged_shapes(exchange):
    return [jax.ShapeDtypeStruct(((N_DEV,) + a.shape) if g else a.shape, a.dtype) for a, g in exchange]


def _hosted_call(body, exchange, *, name, grid, in_specs, out_specs, out_shape, scratch_shapes, semantics, args,
                 input_output_aliases=None):
    single = not isinstance(out_shape, (list, tuple))
    out_specs, out_shape = ([out_specs], [out_shape]) if single else (list(out_specs), list(out_shape))
    scratch, aliases = list(scratch_shapes), dict(input_output_aliases or {})
    if not exchange:
        outs = pl.pallas_call(body, name=name, grid=grid, in_specs=in_specs, out_specs=out_specs, out_shape=out_shape,
                              scratch_shapes=scratch, input_output_aliases=aliases, compiler_params=_params(*semantics))(*args)
        return list(outs), []
    n, n_in, n_out, n_scr = len(exchange), len(in_specs), len(out_specs), len(scratch)
    gather = [g for _, g in exchange]
    steps = math.prod(grid)

    def hosted(*refs):
        own_in, x_in = refs[:n_in], refs[n_in:n_in + n]
        own_out, x_out = refs[n_in + n:n_in + n + n_out], refs[n_in + n + n_out:n_in + 2 * n + n_out]
        own_scr, sems = refs[n_in + 2 * n + n_out:n_in + 2 * n + n_out + n_scr], refs[-3:]
        step = pl.program_id(0)
        for axis in range(1, len(grid)):
            step = step * grid[axis] + pl.program_id(axis)
        start, forward, finish = _exchange_steps(x_in, x_out, gather, *sems)
        pl.when(step == 0)(start)
        body(*own_in, *own_out, *own_scr)
        pl.when(step == steps // 2)(forward)
        pl.when(step == steps - 1)(finish)

    outs = pl.pallas_call(
        hosted, name=name, grid=grid, in_specs=list(in_specs) + [ANY] * n, out_specs=out_specs + [ANY] * n,
        out_shape=out_shape + _exchanged_shapes(exchange), scratch_shapes=scratch + _exchange_scratch(n),
        input_output_aliases=aliases, compiler_params=_params(*["arbitrary"] * len(grid)),
    )(*args, *[a for a, _ in exchange])
    return list(outs[:n_out]), list(outs[n_out:])


def _exchange_now(exchange, name):
    n = len(exchange)
    gather = [g for _, g in exchange]

    def body(*refs):
        for step in _exchange_steps(refs[:n], refs[n:2 * n], gather, *refs[2 * n:]):
            step()

    return pl.pallas_call(
        body, name=name, in_specs=[ANY] * n, out_specs=[ANY] * n, out_shape=_exchanged_shapes(exchange),
        scratch_shapes=_exchange_scratch(n),
    )(*[a for a, _ in exchange])


def _all_reduce_small(v):
    def body(v_ref, o_ref, recv_ref, send_sems, recv_sems):
        x, y, c = _place()
        me = 4 * x + 2 * y + c
        copies = []
        for k in range(1, N_DEV):
            peer = me ^ k
            copies.append(pltpu.make_async_remote_copy(
                src_ref=v_ref, dst_ref=recv_ref.at[k], send_sem=send_sems.at[k - 1], recv_sem=recv_sems.at[k - 1],
                device_id=(peer // 4, (peer // 2) % 2, peer % 2), device_id_type=MESH))
        for cp in copies:
            cp.start()
        recv_ref[0] = v_ref[...]
        for cp in copies:
            cp.wait()
        acc = recv_ref[me]
        for src in range(1, N_DEV):
            acc = acc + recv_ref[me ^ src]
        o_ref[...] = acc

    vm = pl.BlockSpec(memory_space=pltpu.VMEM)
    return pl.pallas_call(
        body, name="all_reduce_small", in_specs=[vm], out_specs=vm, out_shape=jax.ShapeDtypeStruct(v.shape, F32),
        scratch_shapes=[pltpu.VMEM((N_DEV,) + v.shape, F32), pltpu.SemaphoreType.DMA((N_DEV - 1,)),
                        pltpu.SemaphoreType.DMA((N_DEV - 1,))],
    )(v)


def _adamw_math(w, g, m, v):
    m = ADAM_B1 * m + (1.0 - ADAM_B1) * g
    v = ADAM_B2 * v + (1.0 - ADAM_B2) * (g * g)
    m_hat = m / (1.0 - ADAM_B1 ** ADAM_STEP)
    v_hat = v / (1.0 - ADAM_B2 ** ADAM_STEP)
    delta = -ADAM_LR * (m_hat / (jnp.sqrt(v_hat) + ADAM_EPS) + ADAM_WD * w)
    return delta, m, v


def _adamw(parts, w, m, v, name, layer=None, into=None):
    r, c = w.shape[-2:]
    tr = r // 2 if r % 16 == 0 and r >= 256 else r
    n = len(parts)

    def body(*refs):
        w_ref, m_ref, v_ref = refs[n:n + 3]
        g_ref, d_ref, nm_ref, nv_ref = refs[-4:]
        g = refs[0][...].astype(F32)
        for p_ref in refs[1:n]:
            g = g + p_ref[...].astype(F32)
        g_ref[...] = g
        d_ref[...], nm_ref[...], nv_ref[...] = _adamw_math(w_ref[...], g, m_ref[...], v_ref[...])

    def slab(slot):
        return pl.BlockSpec((None, tr, c), lambda i: (slot, i, 0))

    tile = pl.BlockSpec((tr, c), lambda i: (i, 0)) if layer is None else slab(layer)
    arrays, in_specs = [], []
    for p in parts:
        if isinstance(p, tuple):
            arrays.append(p[0])
            in_specs.append(slab(p[1]))
        else:
            arrays.append(p)
            in_specs.append(tile)
    kept = list(into) if into is not None else []
    return pl.pallas_call(
        body, name=name, grid=(r // tr,), in_specs=in_specs + [tile] * 3 + [ANY] * len(kept), out_specs=[tile] * 4,
        out_shape=[jax.ShapeDtypeStruct(w.shape, F32)] * 4,
        input_output_aliases={n + 3 + k: k for k in range(len(kept))}, compiler_params=_params("parallel"),
    )(*arrays, w, m, v, *kept)


def _rows(g):
    return g.reshape(-1, g.shape[-1])


def _row_blocks(dw):
    k, n = dw.shape
    return dw.reshape(N_DEV, k // N_DEV, n)


def _pack_rows(rows, width):
    out = None
    for i, r in enumerate(rows):
        r = r.reshape(1, -1).astype(F32)
        r = jnp.pad(r, ((i, 8 - 1 - i), (0, width - r.shape[1])))
        out = r if out is None else out + r
    return out


def _mixer_fwd(x, gain, w_in, w_out, cos, sin, seq, groups, tag, sink=None, exchanges=None):
    exchanges = exchanges or {}
    os, lses, got = [], [], {}
    qkvs, hs, got["proj"] = _qkv_proj(x, gain, w_in, cos, sin, seq, [dil for dil, _ in groups], tag,
                                      exchange=exchanges.get("proj", ()))
    for gi, (dil, w) in enumerate(groups):
        o, lse, got[gi] = _attn_fwd(qkvs[gi], w, f"{tag}{gi}", sink=sink, exchange=exchanges.get(gi, ()))
        os.append(o)
        lses.append(lse)
    y, o, lses = _out_proj(x, os, lses, [dl for dl, _ in groups], w_out, seq, tag)
    return y, (qkvs, hs, o, lses), got


def _mixer_bwd(dy, x_in, gain, w_in, w_out, saved, cos, sin, seq, groups, tag, sink=None, exchanges=None, scatter_dw_out=False):
    qkvs, hs, o, lses = saved
    t, d = x_in.shape
    dils = [dl for dl, _ in groups]
    lse_tokens = lses[0].reshape(t, LANES) if sink is not None else None
    dw_out = _tn_matmul(o, dy, f"dw_out_{tag}")
    dos, dls, dsink, early = _attn_out_bwd(dy, w_out, o, dils, seq, tag, lse=lse_tokens, sink=sink,
                                           exchange=_to_send([dw_out]) if scatter_dw_out else ())
    if scatter_dw_out:
        (dw_out,) = early
    exchanges = exchanges or {}
    dps, got = [], {}
    for gi, (dil, w) in enumerate(groups):
        dp, got[gi] = _attn_bwd(qkvs[gi], dos[gi], lses[gi], dls[gi], _tables_by_residue(cos, seq, dil),
                                _tables_by_residue(sin, seq, dil), w, f"{tag}{gi}", exchange=exchanges.get(gi, ()))
        dps.append(dp)
    dx, dgain = _qkv_bwd(dy, x_in, gain, w_in, dps, dils, seq, tag)
    dw_in = None
    for gi in range(len(groups)):
        dw_in = _tn_matmul(dps[gi].reshape(t, QKV_W), hs[gi].reshape(t, d), f"dw_in_{tag}{gi}", into=dw_in, row_block=gi,
                           row_blocks=len(groups))
    return dx, dw_in, dw_out, dgain, dsink, got


def _ffn_layer_bwd(dy, x_in, gain, saved, wg, wu, wd, tag, exchange=()):
    g, u, act, h = saved
    dx, dg, du, dgain, got = _ffn_bwd(dy, x_in, gain, g, u, wg, wu, wd, tag, exchange=exchange)
    dwd = _tn_matmul(act, dy, f"dw_down_{tag}")
    dwg = _tn_matmul(dg, h, f"dw_gate_{tag}")
    dwu = _tn_matmul(du, h, f"dw_up_{tag}")
    return dx, dwg, dwu, dwd, dgain, got


def _to_send(dws):
    return [(_row_blocks(g), False) for g in dws]


def kernel(x, a_w_in, a_sink, a_w_out, b_w_in, b_w_out, norm_mix, norm_ffn, w_gate, w_up, w_down, final_norm, loss_target, m_a_w_in, m_a_sink, m_a_w_out, m_b_w_in, m_b_w_out, m_norm_mix, m_norm_ffn, m_w_gate, m_w_up, m_w_down, m_final_norm, v_a_w_in, v_a_sink, v_a_w_out, v_b_w_in, v_b_w_out, v_norm_mix, v_norm_ffn, v_w_gate, v_w_up, v_w_down, v_final_norm):
    bl, seq, d = x.shape
    t = bl * seq
    xf = x.reshape(t, d)
    target = loss_target.reshape(t, d)
    cos, sin = _rope_tables(seq)
    groups_a = [(1, ATTN_HALF_WINDOW)]
    groups_b = [(dil, window // 2 // dil) for window, dil in DILATED_GROUPS]

    def flip(w_):
        return jnp.swapaxes(w_, -1, -2)

    a_w_in, m_a_w_in, v_a_w_in, b_w_in, m_b_w_in, v_b_w_in = map(flip, (a_w_in, m_a_w_in, v_a_w_in, b_w_in, m_b_w_in, v_b_w_in))
    w_gate, m_w_gate, v_w_gate, w_up, m_w_up, v_w_up = map(flip, (w_gate, m_w_gate, v_w_gate, w_up, m_w_up, v_w_up))

    def shard(w_, layer):
        return (w_[layer].astype(BF16), True)

    wa_in, wa_out = map(_rows, _exchange_now([shard(a_w_in, 0), shard(a_w_out, 0)], "gather_first"))

    x1_0, saved_a, got = _mixer_fwd(xf, norm_mix[0:1], wa_in, wa_out, cos, sin, seq, groups_a, "a", sink=a_sink[0],
                                    exchanges={"proj": [shard(w_down, 0)], 0: [shard(w_gate, 0), shard(w_up, 0)]})
    wg0, wu0, wd0 = map(_rows, got[0] + got["proj"])
    x2_0, *saved_0, got = _ffn_fwd(x1_0, norm_ffn[0:1], wg0, wu0, wd0, "0", exchange=[shard(b_w_in, 0), shard(b_w_out, 0)])
    wb_in, wb_out = map(_rows, got)
    x1_1, saved_b, got = _mixer_fwd(x2_0, norm_mix[1:2], wb_in, wb_out, cos, sin, seq, groups_b, "b",
                                    exchanges={0: [shard(w_gate, 1)], 1: [shard(w_up, 1)], 2: [shard(w_down, 1)]})
    wg1, wu1, wd1 = map(_rows, got[0] + got[1] + got[2])
    dy, *saved_1, loss_part, d_final, _ = _ffn_fwd(x1_1, norm_ffn[1:2], wg1, wu1, wd1, "1",
                                                   loss_head=(final_norm.reshape(1, d), target))

    dy, dwg1, dwu1, dwd1, d_nf1, _ = _ffn_layer_bwd(dy, x1_1, norm_ffn[1:2], saved_1, wg1, wu1, wd1, "1")
    dy, dwb_in, dwb_out, d_nm1, _, got = _mixer_bwd(
        dy, x2_0, norm_mix[1:2], wb_in, wb_out, saved_b, cos, sin, seq, groups_b, "b",
        exchanges={0: _to_send([dwg1, dwd1]), 1: _to_send([dwu1])})
    (r_g1, r_d1), (r_u1,) = got[0], got[1]
    dy, dwg0, dwu0, dwd0, d_nf0, (r_b_in, r_b_out) = _ffn_layer_bwd(
        dy, x1_0, norm_ffn[0:1], saved_0, wg0, wu0, wd0, "0", exchange=_to_send([dwb_in, dwb_out]))
    dy, dwa_in, r_a_out, d_nm0, d_sink, got = _mixer_bwd(
        dy, xf, norm_mix[0:1], wa_in, wa_out, saved_a, cos, sin, seq, groups_a, "a", sink=a_sink[0],
        exchanges={0: _to_send([dwg0, dwu0, dwd0])}, scatter_dw_out=True)
    r_g0, r_u0, r_d0 = got[0]
    (r_a_in,) = _exchange_now(_to_send([dwa_in]), "scatter_last")
    grad_x = dy.reshape(bl, seq, d)

    def update(received, w_, m_, v_, name):
        out = None
        for layer in reversed(range(len(received))):
            out = _adamw([(received[layer], src) for src in range(N_DEV)], w_, m_, v_, f"adamw_{name}{layer}", layer=layer, into=out)
        return out

    u_a_in = update([r_a_in], a_w_in, m_a_w_in, v_a_w_in, "a_in")
    u_a_out = update([r_a_out], a_w_out, m_a_w_out, v_a_w_out, "a_out")
    u_b_in = update([r_b_in], b_w_in, m_b_w_in, v_b_w_in, "b_in")
    u_b_out = update([r_b_out], b_w_out, m_b_w_out, v_b_w_out, "b_out")
    u_gate = update([r_g0, r_g1], w_gate, m_w_gate, v_w_gate, "gate")
    u_up = update([r_u0, r_u1], w_up, m_w_up, v_w_up, "up")
    u_down = update([r_d0, r_d1], w_down, m_w_down, v_w_down, "down")

    small = _pack_rows([d_nm0, d_nm1, d_nf0, d_nf1, d_final, d_sink, loss_part], d)
    total = _all_reduce_small(small)
    small_w = _pack_rows([norm_mix[0], norm_mix[1], norm_ffn[0], norm_ffn[1], final_norm, a_sink], d)
    small_m = _pack_rows([m_norm_mix[0], m_norm_mix[1], m_norm_ffn[0], m_norm_ffn[1], m_final_norm, m_a_sink], d)
    small_v = _pack_rows([v_norm_mix[0], v_norm_mix[1], v_norm_ffn[0], v_norm_ffn[1], v_final_norm, v_a_sink], d)
    u_small = _adamw([total], small_w, small_m, small_v, "adamw_small")
    loss = total[6, 0]

    outs = []
    for k in range(4):
        sm = u_small[k]
        outs += [flip(u_a_in[k]), sm[5:6, :N_HEADS], u_a_out[k], flip(u_b_in[k]), u_b_out[k], sm[0:2], sm[2:4],
                 flip(u_gate[k]), flip(u_up[k]), u_down[k], sm[4]]
    return (loss, grad_x, *outs)
```

```python
import functools
import math

import jax
import jax.numpy as jnp
from jax import lax
from jax.experimental import pallas as pl
from jax.experimental.pallas import tpu as pltpu

F32 = jnp.float32
BF16 = jnp.bfloat16

HEAD_DIM = 64
N_HEADS = 16
N_KV = 4
GRP = N_HEADS // N_KV
Q_W = N_HEADS * HEAD_DIM
KV_W = N_KV * HEAD_DIM
QKV_W = Q_W + 2 * KV_W
ATTN_HALF_WINDOW = 128
DILATED_GROUPS = ((128, 1), (512, 4), (2048, 16))
ROPE_THETA = 10000.0
RMS_EPS = 1e-6
NEG_INF = -1e30
SCALE = 1.0 / math.sqrt(HEAD_DIM)

ADAM_LR = 0.001
ADAM_B1 = 0.9
ADAM_B2 = 0.999
ADAM_EPS = 1e-08
ADAM_WD = 0.01
ADAM_STEP = 10

LANES = 128
VMEM_LIMIT = 56 * 1024 * 1024
QUERY_BLOCK = 128
N_DEV = 8
MESH = pl.DeviceIdType.MESH

NT = (((1,), (1,)), ((), ()))
TN = (((0,), (0,)), ((), ()))


def _params(*sem):
    return pltpu.CompilerParams(dimension_semantics=tuple(sem) if sem else None, vmem_limit_bytes=VMEM_LIMIT)


def _resident(shape):
    return pl.BlockSpec(shape, lambda *_: (0,) * len(shape), pipeline_mode=pl.Buffered(1))


def _rope_tables(seq):
    inv_freq = 1.0 / (ROPE_THETA ** (jnp.arange(0, HEAD_DIM, 2, dtype=F32) / HEAD_DIM))
    ang = jnp.arange(seq, dtype=F32)[:, None] * inv_freq[None, :]
    cos, sin = jnp.cos(ang), jnp.sin(ang)
    return jnp.tile(cos, (1, 4)), jnp.concatenate([-sin, sin, -sin, sin], axis=1)


def _rope(t, cos, sin_signed):
    lane = lax.broadcasted_iota(jnp.int32, t.shape, 1)
    first = (lane & (HEAD_DIM // 2)) == 0
    swapped = jnp.where(first, pltpu.roll(t, LANES - HEAD_DIM // 2, 1), pltpu.roll(t, HEAD_DIM // 2, 1))
    return t * cos + swapped * sin_signed


def _rms(x):
    return lax.rsqrt(jnp.mean(x * x, axis=-1, keepdims=True) + RMS_EPS)


def _rms_bwd(dh, x, gain):
    r = _rms(x)
    xhat = x * r
    dxh = dh * gain
    dx = r * (dxh - xhat * jnp.mean(dxh * xhat, axis=-1, keepdims=True))
    return dx, xhat


def _accumulate(ref, value, first):
    @pl.when(first)
    def _():
        ref[...] = jnp.zeros_like(ref)

    ref[...] += value


def _tile_rows(seq):
    return min(512, seq)


def _res_shape(bl, seq, dil, c):
    ts = _tile_rows(seq)
    return (bl, dil, seq // ts, ts // dil, c)


def _res_spec(seq, dil, c):
    ts = _tile_rows(seq)
    per_seq = seq // ts
    return pl.BlockSpec((None, dil, None, ts // dil, c), lambda i: (i // per_seq, 0, i % per_seq, 0, 0))


def _seq_view(a):
    bl, dil, tiles, n, c = a.shape
    return a.reshape(bl * dil, tiles * n, c)


def _stage(ts, c):
    return pltpu.VMEM((c // LANES, ts, LANES), F32)


def _split_rows(val, stage_ref, dil):
    if dil == 1:
        return [val]
    ts, c = val.shape
    n, nc = ts // dil, c // LANES
    for k in range(nc):
        stage_ref[k] = val[:, k * LANES:(k + 1) * LANES]
    return [jnp.concatenate([stage_ref[k, pl.ds(r, n, stride=dil), :] for k in range(nc)], axis=1) for r in range(dil)]


def _merge_rows(parts, stage_ref, dil):
    if dil == 1:
        return parts[0]
    n, c = parts[0].shape
    nc = c // LANES
    for r, part in enumerate(parts):
        for k in range(nc):
            stage_ref[k, pl.ds(r, n, stride=dil), :] = part[:, k * LANES:(k + 1) * LANES]
    return jnp.concatenate([stage_ref[k] for k in range(nc)], axis=1)


def _tables_tiled(table, seq, dil):
    ts = _tile_rows(seq)
    return table.reshape(seq // ts, ts // dil, dil, LANES).transpose(0, 2, 1, 3).reshape(seq, LANES)


def _tables_by_residue(table, seq, dil):
    return table.reshape(seq // dil, dil, LANES).transpose(1, 0, 2)


def _qkv_proj(x, gain, w, cos, sin, seq, dils, tag, exchange=()):
    t, d = x.shape
    ts = _tile_rows(seq)
    per_seq = seq // ts
    ng = len(dils)
    tables = [t_ for dil in dils for t_ in (_tables_tiled(cos, seq, dil), _tables_tiled(sin, seq, dil))]

    def body(x_ref, g_ref, w_ref, *refs):
        table_refs, o_refs, h_refs, stage_ref = refs[:2 * ng], refs[2 * ng:3 * ng], refs[3 * ng:4 * ng], refs[4 * ng]
        xv = x_ref[...]
        h_tokens = xv * _rms(xv) * g_ref[...]
        for gi, dil in enumerate(dils):
            n = ts // dil
            h = jnp.concatenate(_split_rows(h_tokens, stage_ref, dil), axis=0).astype(BF16)
            for r in range(dil):
                h_refs[gi][r] = h[r * n:(r + 1) * n]
            acc = lax.dot_general(h, w_ref[gi * QKV_W:(gi + 1) * QKV_W, :], NT, preferred_element_type=F32)
            c, s = table_refs[2 * gi][...], table_refs[2 * gi + 1][...]
            for j in range(QKV_W // LANES):
                cols = slice(j * LANES, (j + 1) * LANES)
                val = acc[:, cols]
                if j < (Q_W + KV_W) // LANES:
                    val = _rope(val, c, s)
                if j < Q_W // LANES:
                    val = val * SCALE
                val = val.astype(BF16)
                for r in range(dil):
                    o_refs[gi][r, :, cols] = val[r * n:(r + 1) * n]

    table = pl.BlockSpec((ts, LANES), lambda i: (i % per_seq, 0))
    outs, exchanged = _hosted_call(
        body, exchange, name=f"qkv_proj_{tag}", grid=(t // ts,),
        in_specs=[pl.BlockSpec((ts, d), lambda i: (i, 0)), _resident((1, d)), _resident(w.shape)] + [table] * (2 * ng),
        out_specs=[_res_spec(seq, dil, QKV_W) for dil in dils] + [_res_spec(seq, dil, d) for dil in dils],
        out_shape=[jax.ShapeDtypeStruct(_res_shape(t // seq, seq, dil, QKV_W), BF16) for dil in dils]
                  + [jax.ShapeDtypeStruct(_res_shape(t // seq, seq, dil, d), BF16) for dil in dils],
        scratch_shapes=[_stage(ts, d)], semantics=("parallel",), args=(x, gain, w, *tables))
    return outs[:ng], outs[ng:], exchanged


def _band(bq, wk):
    return lax.broadcasted_iota(jnp.int32, (bq, wk), 0) - lax.broadcasted_iota(jnp.int32, (bq, wk), 1)


def _swap_halves(src_ref, base, dst_ref):
    for c in range(KV_W // LANES):
        dst_ref[c] = pltpu.roll(src_ref[:, base + c * LANES:base + (c + 1) * LANES], HEAD_DIM, 1)


def _pair_operand(src_ref, swapped_ref, base, kv, rows):
    c = kv // 2
    chunk, swapped = src_ref[rows, base + c * LANES:base + (c + 1) * LANES], swapped_ref[c, rows, :]
    lo = lax.broadcasted_iota(jnp.int32, chunk.shape, 1) < HEAD_DIM
    zero = jnp.zeros_like(chunk)
    if kv % 2 == 0:
        return jnp.concatenate([jnp.where(lo, chunk, zero), jnp.where(lo, zero, swapped)], axis=0)
    return jnp.concatenate([jnp.where(lo, swapped, zero), jnp.where(lo, zero, chunk)], axis=0)


def _over_keys(col, wk):
    if wk % LANES:
        return jnp.broadcast_to(col, (col.shape[0], wk))
    wide = jnp.broadcast_to(col, (col.shape[0], LANES))
    return wide if wk == LANES else jnp.concatenate([wide] * (wk // LANES), axis=1)


def _per_step(dil, length):
    return max(1, min(dil, 512 // length))


def _key_rows(bq, w, length):
    return min(bq + 2 * w, length)


def _window(i, bq, w, wk, length):
    q0 = pl.multiple_of(i * bq, bq)
    k0 = pl.multiple_of(jnp.clip(q0 - w, 0, length - wk), min(w, bq))
    return q0, k0


def _attn_fwd(qkv, w, tag, sink=None, exchange=()):
    shape = qkv.shape
    rows_all = _seq_view(qkv)
    nseq, length, _ = rows_all.shape
    bq = min(QUERY_BLOCK, length)
    wk = _key_rows(bq, w, length)
    nb = length // bq
    has_sink = sink is not None
    per_step = _per_step(shape[1], length)

    def body(*refs):
        sink_ref = refs[1] if has_sink else None
        kk_ref, vv_ref = refs[-2:]
        for sub in range(per_step):
            one(refs[0].at[sub], refs[-4].at[sub], refs[-3].at[sub], sink_ref, kk_ref, vv_ref)

    def one(qkv_ref, o_ref, lse_ref, sink_ref, kk_ref, vv_ref):
        _swap_halves(qkv_ref, Q_W, kk_ref)
        _swap_halves(qkv_ref, Q_W + KV_W, vv_ref)
        band = _band(bq, wk)
        lane = lax.broadcasted_iota(jnp.int32, (bq, LANES), 1)
        lo = lane < HEAD_DIM

        def block(i, carry):
            q0, k0 = _window(i, bq, w, wk, length)
            valid = jnp.abs(band + (q0 - k0)) <= w
            rows, krows = pl.ds(q0, bq), pl.ds(k0, wk)
            lse_tile = jnp.zeros((bq, LANES), F32)
            for kv in range(N_KV):
                heads = [(kv * GRP + h, h % 2) for h in range(GRP)]
                qp = [qkv_ref[rows, (kv * 2 + j) * LANES:(kv * 2 + j + 1) * LANES] for j in range(GRP // 2)]
                k2 = _pair_operand(qkv_ref, kk_ref, Q_W, kv, krows)
                v2 = _pair_operand(qkv_ref, vv_ref, Q_W + KV_W, kv, krows)
                sc2 = [lax.dot_general(q_, k2, NT, preferred_element_type=F32) for q_ in qp]
                sc = [jnp.where(valid, s_[:, half * wk:(half + 1) * wk], NEG_INF) for s_ in sc2 for half in range(2)]
                m = [jnp.max(s_, axis=-1, keepdims=True) for s_ in sc]
                if has_sink:
                    m = [jnp.maximum(m_, sink_ref[hd]) for m_, (hd, _) in zip(m, heads)]
                mb = [jnp.broadcast_to(m_, (bq, LANES)) for m_ in m]
                p = [jnp.exp(s_ - _over_keys(m_, wk)) for s_, m_ in zip(sc, m)]
                den = [jnp.sum(p_, axis=-1, keepdims=True) for p_ in p]
                if has_sink:
                    den = [d_ + jnp.exp(sink_ref[hd] - m_) for d_, m_, (hd, _) in zip(den, m, heads)]
                inv = [jnp.broadcast_to(1.0 / d_, (bq, LANES)) for d_ in den]
                pb = [p_.astype(BF16) for p_ in p]
                for j in range(GRP // 2):
                    o = jnp.dot(jnp.concatenate([pb[2 * j], pb[2 * j + 1]], axis=1), v2, preferred_element_type=F32)
                    o = o * jnp.where(lo, inv[2 * j], inv[2 * j + 1])
                    o_ref[rows, (kv * 2 + j) * LANES:(kv * 2 + j + 1) * LANES] = o.astype(BF16)
                for h, (hd, _) in enumerate(heads):
                    lse_tile = jnp.where(lane == hd, mb[h] - jnp.log(inv[h]), lse_tile)
            lse_ref[rows, :] = lse_tile
            return carry

        lax.fori_loop(0, nb, block, 0)

    def seq_block(c):
        return pl.BlockSpec((per_step, length, c), lambda i: (i, 0, 0))

    args = [rows_all]
    in_specs = [seq_block(QKV_W)]
    if has_sink:
        args.append(sink)
        in_specs.append(pl.BlockSpec(memory_space=pltpu.SMEM))
    (o, lse), exchanged = _hosted_call(
        body, exchange, name=f"attn_fwd_{tag}", grid=(nseq // per_step,), in_specs=in_specs,
        out_specs=[seq_block(Q_W), seq_block(LANES)],
        out_shape=[jax.ShapeDtypeStruct((nseq, length, Q_W), BF16), jax.ShapeDtypeStruct((nseq, length, LANES), F32)],
        scratch_shapes=[pltpu.VMEM((KV_W // LANES, length, LANES), BF16), pltpu.VMEM((KV_W // LANES, length, LANES), BF16)],
        semantics=("parallel",), args=args)
    return o.reshape(shape[:-1] + (Q_W,)), lse.reshape(shape[:-1] + (LANES,)), exchanged


def _head_expand():
    return (jnp.arange(LANES)[:, None] == jnp.arange(Q_W)[None, :] // HEAD_DIM).astype(BF16)


def _out_proj(x, os, lses, dils, w, seq, tag):
    t, d = x.shape
    ts = _tile_rows(seq)
    ng = len(os)
    bl = t // seq
    if ng == 1:
        def body1(x_ref, o_ref, w_ref, y_ref):
            y_ref[...] = x_ref[...] + jnp.dot(o_ref[...], w_ref[...], preferred_element_type=F32)

        row = pl.BlockSpec((ts, d), lambda i: (i, 0))
        o = os[0].reshape(t, Q_W)
        y = pl.pallas_call(
            body1, name=f"out_proj_{tag}", grid=(t // ts,), in_specs=[row, row, _resident(w.shape)], out_specs=row,
            out_shape=jax.ShapeDtypeStruct((t, d), F32), compiler_params=_params("parallel"),
        )(x, o, w)
        return y, o, [lses[0]]

    def body(*refs):
        x_ref, w_ref, e_ref = refs[:3]
        o_refs, l_refs = refs[3:3 + ng], refs[3 + ng:3 + 2 * ng]
        y_ref, om_ref = refs[3 + 2 * ng:5 + 2 * ng]
        lt_refs = refs[5 + 2 * ng:5 + 3 * ng]
        wide_ref, narrow_ref = refs[5 + 3 * ng:]
        ls = [_merge_rows([l_refs[g][r] for r in range(dils[g])], narrow_ref, dils[g]) for g in range(ng)]
        mx = functools.reduce(jnp.maximum, ls)
        tot = mx + jnp.log(functools.reduce(lambda a, b: a + b, [jnp.exp(l_ - mx) for l_ in ls]))
        e = e_ref[...]
        o = None
        for g in range(ng):
            wt = jnp.exp(ls[g] - tot)
            hi = wt.astype(BF16)
            lo = (wt - hi.astype(F32)).astype(BF16)
            wide = jnp.dot(hi, e, preferred_element_type=F32) + jnp.dot(lo, e, preferred_element_type=F32)
            term = wide * _merge_rows([o_refs[g][r].astype(F32) for r in range(dils[g])], wide_ref, dils[g])
            o = term if o is None else o + term
        ob = o.astype(BF16)
        om_ref[...] = ob
        y_ref[...] = x_ref[...] + jnp.dot(ob, w_ref[...], preferred_element_type=F32)
        for g in range(ng):
            for r, part in enumerate(_split_rows(tot, narrow_ref, dils[g])):
                lt_refs[g][r] = part

    row = pl.BlockSpec((ts, d), lambda i: (i, 0))
    e = _head_expand()
    outs = pl.pallas_call(
        body, name=f"out_proj_{tag}", grid=(t // ts,),
        in_specs=[row, _resident(w.shape), _resident(e.shape)] + [_res_spec(seq, dl, Q_W) for dl in dils]
                 + [_res_spec(seq, dl, LANES) for dl in dils],
        out_specs=[row, pl.BlockSpec((ts, Q_W), lambda i: (i, 0))] + [_res_spec(seq, dl, LANES) for dl in dils],
        out_shape=[jax.ShapeDtypeStruct((t, d), F32), jax.ShapeDtypeStruct((t, Q_W), BF16)]
                  + [jax.ShapeDtypeStruct(_res_shape(bl, seq, dl, LANES), F32) for dl in dils],
        scratch_shapes=[_stage(ts, Q_W), _stage(ts, LANES)],
        compiler_params=_params("parallel"),
    )(x, w, e, *os, *lses)
    return outs[0], outs[1], list(outs[2:])


def _sigmoid(g):
    return 1.0 / (1.0 + jnp.exp(-g))


def _ffn_fwd(x, gain, wg, wu, wd, tag, exchange=(), loss_head=None):
    t, d = x.shape
    f = wd.shape[0]
    tm = min(256, t)
    has_loss = loss_head is not None

    def body(*refs):
        x_ref, gain_ref, wg_ref, wu_ref, wd_ref = refs[:5]
        y_ref, g_ref, u_ref, a_ref, h_ref = refs[-7:-2] if has_loss else refs[-5:]
        xv = x_ref[...]
        h = (xv * _rms(xv) * gain_ref[...]).astype(BF16)
        h_ref[...] = h
        g = lax.dot_general(h, wg_ref[...], NT, preferred_element_type=F32)
        u = lax.dot_general(h, wu_ref[...], NT, preferred_element_type=F32)
        g_ref[...] = g.astype(BF16)
        u_ref[...] = u.astype(BF16)
        a = (g * _sigmoid(g) * u).astype(BF16)
        a_ref[...] = a
        y = xv + jnp.dot(a, wd_ref[...], preferred_element_type=F32)
        if not has_loss:
            y_ref[...] = y
            return
        head_ref, target_ref, loss_ref, dhead_ref = refs[5], refs[6], refs[-2], refs[-1]
        head = head_ref[...]
        yhat = y * _rms(y)
        err = yhat * head - target_ref[...]
        dout = err * (1.0 / d)
        y_ref[...] = _rms_bwd(dout, y, head)[0]
        first = pl.program_id(0) == 0
        part = 0.5 * jnp.sum(jnp.mean(err * err, axis=-1, keepdims=True), axis=0, keepdims=True)
        _accumulate(loss_ref, jnp.broadcast_to(part, loss_ref.shape), first)
        _accumulate(dhead_ref, jnp.sum(dout * yhat, axis=0, keepdims=True), first)

    row = pl.BlockSpec((tm, d), lambda i: (i, 0))
    wide = pl.BlockSpec((tm, f), lambda i: (i, 0))
    in_specs = [row, _resident((1, d)), _resident(wg.shape), _resident(wu.shape), _resident(wd.shape)]
    out_specs = [row, wide, wide, wide, row]
    out_shape = [jax.ShapeDtypeStruct((t, d), F32)] + [jax.ShapeDtypeStruct((t, f), BF16)] * 3 + [jax.ShapeDtypeStruct((t, d), BF16)]
    if has_loss:
        in_specs += [_resident((1, d)), row]
        out_specs += [pl.BlockSpec((1, LANES), lambda i: (0, 0)), pl.BlockSpec((1, d), lambda i: (0, 0))]
        out_shape += [jax.ShapeDtypeStruct((1, LANES), F32), jax.ShapeDtypeStruct((1, d), F32)]
    outs, exchanged = _hosted_call(
        body, exchange, name=f"ffn_fwd_{tag}", grid=(t // tm,), in_specs=in_specs, out_specs=out_specs, out_shape=out_shape,
        scratch_shapes=[], semantics=("arbitrary" if has_loss else "parallel",),
        args=(x, gain, wg, wu, wd) + (tuple(loss_head) if has_loss else ()))
    return (*outs, exchanged)


def _ffn_bwd(dy, x, gain, g, u, wg, wu, wd, tag, exchange=()):
    t, d = x.shape
    f = wd.shape[0]
    tm = min(256, t)

    def body(dy_ref, x_ref, gain_ref, g_ref, u_ref, wg_ref, wu_ref, wd_ref, dx_ref, dg_ref, du_ref, dgain_ref):
        dyv = dy_ref[...]
        da = lax.dot_general(dyv.astype(BF16), wd_ref[...], NT, preferred_element_type=F32)
        gv, uv = g_ref[...].astype(F32), u_ref[...].astype(F32)
        sg = _sigmoid(gv)
        act = gv * sg
        du = (da * act).astype(BF16)
        dg = (da * uv * (sg * (1.0 + gv * (1.0 - sg)))).astype(BF16)
        du_ref[...] = du
        dg_ref[...] = dg
        dh = jnp.dot(dg, wg_ref[...], preferred_element_type=F32) + jnp.dot(du, wu_ref[...], preferred_element_type=F32)
        xv, gain_v = x_ref[...], gain_ref[...]
        dx, xhat = _rms_bwd(dh, xv, gain_v)
        dx_ref[...] = dyv + dx
        _accumulate(dgain_ref, jnp.sum(dh * xhat, axis=0, keepdims=True), pl.program_id(0) == 0)

    row = pl.BlockSpec((tm, d), lambda i: (i, 0))
    wide = pl.BlockSpec((tm, f), lambda i: (i, 0))
    outs, exchanged = _hosted_call(
        body, exchange, name=f"ffn_bwd_{tag}", grid=(t // tm,),
        in_specs=[row, row, _resident((1, d)), wide, wide, _resident(wg.shape), _resident(wu.shape), _resident(wd.shape)],
        out_specs=[row, wide, wide, pl.BlockSpec((1, d), lambda i: (0, 0))],
        out_shape=[jax.ShapeDtypeStruct((t, d), F32), jax.ShapeDtypeStruct((t, f), BF16), jax.ShapeDtypeStruct((t, f), BF16),
                   jax.ShapeDtypeStruct((1, d), F32)],
        scratch_shapes=[], semantics=("arbitrary",), args=(dy, x, gain, g, u, wg, wu, wd))
    return (*outs, exchanged)


def _tn_matmul(a, b, name, into=None, row_block=0, row_blocks=1):
    t, k = a.shape
    n = b.shape[1]
    tk = k // 2 if (k // 2) % LANES == 0 else k
    tt = min(2048, t)
    first = row_block * (k // tk)

    def body(a_ref, b_ref, *rest):
        o_ref, acc_ref = rest[-2:]
        prod = lax.dot_general(a_ref[...].astype(BF16), b_ref[...].astype(BF16), TN, preferred_element_type=F32)
        j = pl.program_id(1)

        @pl.when(j == 0)
        def _():
            acc_ref[...] = prod

        @pl.when(j > 0)
        def _():
            acc_ref[...] += prod

        @pl.when(j == pl.num_programs(1) - 1)
        def _():
            o_ref[...] = acc_ref[...].astype(BF16)

    return pl.pallas_call(
        body, name=name, grid=(k // tk, t // tt),
        in_specs=[pl.BlockSpec((tt, tk), lambda i, j: (j, i)), pl.BlockSpec((tt, n), lambda i, j: (j, 0))]
                 + ([ANY] if into is not None else []),
        out_specs=pl.BlockSpec((tk, n), lambda i, j: (first + i, 0)),
        out_shape=jax.ShapeDtypeStruct((row_blocks * k, n), BF16),
        scratch_shapes=[pltpu.VMEM((tk, n), F32)],
        input_output_aliases={2: 0} if into is not None else {},
        compiler_params=_params("parallel", "arbitrary"),
    )(a, b, *([into] if into is not None else []))


def _attn_out_bwd(dx, w, o, dils, seq, tag, lse=None, sink=None, exchange=()):
    t, d = dx.shape
    ts = _tile_rows(seq)
    bl = t // seq
    ng = len(dils)
    has_sink = sink is not None
    expand = _head_expand().T

    def body(*refs):
        refs = list(refs)
        dx_ref, w_ref, o_ref, e_ref = refs[:4]
        refs = refs[4:]
        lse_ref, sink_ref = (refs.pop(0), refs.pop(0)) if has_sink else (None, None)
        do_refs, dl_refs = refs[:ng], refs[ng:2 * ng]
        refs = refs[2 * ng:]
        dsink_ref = refs.pop(0) if has_sink else None
        dof_ref, dlf_ref = refs
        do = lax.dot_general(dx_ref[...].astype(BF16), w_ref[...], NT, preferred_element_type=F32)
        prod = do * o_ref[...].astype(F32)
        hi = prod.astype(BF16)
        lo = (prod - hi.astype(F32)).astype(BF16)
        e = e_ref[...]
        dl = jnp.dot(hi, e, preferred_element_type=F32) + jnp.dot(lo, e, preferred_element_type=F32)
        for g in range(ng):
            for r, part in enumerate(_split_rows(do, dof_ref, dils[g])):
                do_refs[g][r] = part.astype(BF16)
            for r, part in enumerate(_split_rows(dl, dlf_ref, dils[g])):
                dl_refs[g][r] = part
        if has_sink:
            part = -jnp.exp(sink_ref[...] - lse_ref[...]) * dl
            _accumulate(dsink_ref, jnp.sum(part, axis=0, keepdims=True), pl.program_id(0) == 0)

    row = pl.BlockSpec((ts, d), lambda i: (i, 0))
    narrow = pl.BlockSpec((ts, LANES), lambda i: (i, 0))
    args = [dx, w, o, expand]
    in_specs = [row, _resident(w.shape), pl.BlockSpec((ts, Q_W), lambda i: (i, 0)), _resident(expand.shape)]
    if has_sink:
        args += [lse, jnp.pad(sink.reshape(1, N_HEADS), ((0, 0), (0, LANES - N_HEADS)))]
        in_specs += [narrow, _resident((1, LANES))]
    out_specs = [_res_spec(seq, dl, Q_W) for dl in dils] + [_res_spec(seq, dl, LANES) for dl in dils]
    out_shape = ([jax.ShapeDtypeStruct(_res_shape(bl, seq, dl, Q_W), BF16) for dl in dils]
                 + [jax.ShapeDtypeStruct(_res_shape(bl, seq, dl, LANES), F32) for dl in dils])
    if has_sink:
        out_specs.append(pl.BlockSpec((1, LANES), lambda i: (0, 0)))
        out_shape.append(jax.ShapeDtypeStruct((1, LANES), F32))
    outs, exchanged = _hosted_call(
        body, exchange, name=f"attn_out_bwd_{tag}", grid=(t // ts,), in_specs=in_specs, out_specs=out_specs, out_shape=out_shape,
        scratch_shapes=[_stage(ts, Q_W), _stage(ts, LANES)], semantics=("arbitrary" if has_sink else "parallel",), args=args)
    return list(outs[:ng]), list(outs[ng:2 * ng]), (outs[2 * ng] if has_sink else None), exchanged


def _attn_bwd(qkv, do, lse, delta, cos, sin, w, tag, exchange=()):
    shape = qkv.shape
    dil = shape[1]
    rows_all = _seq_view(qkv)
    nseq, length, _ = rows_all.shape
    bq = min(QUERY_BLOCK, length)
    wk = _key_rows(bq, w, length)
    nb = length // bq
    per_step = _per_step(dil, length)

    def body(*refs):
        def sub(i, carry):
            one(*[ref.at[i] for ref in refs[:7]], *refs[7:])
            return carry

        if per_step == 1:
            sub(0, 0)
        else:
            lax.fori_loop(0, per_step, sub, 0)

    def one(qkv_ref, do_ref, lse_ref, dl_ref, cos_ref, sin_ref, dp_ref, kk_ref, vv_ref, dk_ref, dv_ref):
        _swap_halves(qkv_ref, Q_W, kk_ref)
        _swap_halves(qkv_ref, Q_W + KV_W, vv_ref)
        dk_ref[...] = jnp.zeros_like(dk_ref)
        dv_ref[...] = jnp.zeros_like(dv_ref)
        band = _band(bq, wk)
        lo_q = lax.broadcasted_iota(jnp.int32, (bq, LANES), 1) < HEAD_DIM
        hi_q = jnp.logical_not(lo_q)

        def block(i, carry):
            q0, k0 = _window(i, bq, w, wk, length)
            valid = jnp.abs(band + (q0 - k0)) <= w
            rows, krows = pl.ds(q0, bq), pl.ds(k0, wk)
            c, sn = cos_ref[rows, :], -sin_ref[rows, :]
            lse_t, dl_t = lse_ref[rows, :], dl_ref[rows, :]
            for kv in range(N_KV):
                heads = [(kv * GRP + h, h % 2) for h in range(GRP)]
                cols = [slice((kv * 2 + j) * LANES, (kv * 2 + j + 1) * LANES) for j in range(GRP // 2)]
                qp = [qkv_ref[rows, cs] for cs in cols]
                dop = [do_ref[rows, cs] for cs in cols]
                k2 = _pair_operand(qkv_ref, kk_ref, Q_W, kv, krows)
                v2 = _pair_operand(qkv_ref, vv_ref, Q_W + KV_W, kv, krows)
                sc2 = [lax.dot_general(q_, k2, NT, preferred_element_type=F32) for q_ in qp]
                dp2 = [lax.dot_general(d_, v2, NT, preferred_element_type=F32) for d_ in dop]
                sc = [s_[:, half * wk:(half + 1) * wk] for s_ in sc2 for half in range(2)]
                dp = [d_[:, half * wk:(half + 1) * wk] for d_ in dp2 for half in range(2)]
                p = [jnp.exp(jnp.where(valid, s_, NEG_INF) - _over_keys(lse_t[:, hd:hd + 1], wk))
                     for s_, (hd, _) in zip(sc, heads)]
                ds = [(p_ * (dp_ - _over_keys(dl_t[:, hd:hd + 1], wk))).astype(BF16) for p_, dp_, (hd, _) in zip(p, dp, heads)]
                pb = [p_.astype(BF16) for p_ in p]
                for j in range(GRP // 2):
                    dq = jnp.dot(jnp.concatenate([ds[2 * j], ds[2 * j + 1]], axis=1), k2, preferred_element_type=F32) * SCALE
                    dp_ref[rows, cols[j]] = _rope(dq, c, sn).astype(BF16)
                zero = jnp.zeros((bq, LANES), BF16)
                q4 = jnp.concatenate([jnp.where(lo_q if h % 2 == 0 else hi_q, qp[h // 2], zero) for h in range(GRP)], axis=0)
                do4 = jnp.concatenate([jnp.where(lo_q if h % 2 == 0 else hi_q, dop[h // 2], zero) for h in range(GRP)], axis=0)
                dk_ref[kv, krows, :] += lax.dot_general(jnp.concatenate(ds, axis=0), q4, TN, preferred_element_type=F32)
                dv_ref[kv, krows, :] += lax.dot_general(jnp.concatenate(pb, axis=0), do4, TN, preferred_element_type=F32)
            return carry

        lax.fori_loop(0, nb, block, 0)
        lo = lax.broadcasted_iota(jnp.int32, (length, LANES), 1) < HEAD_DIM
        c, sn = cos_ref[...], -sin_ref[...]
        for ch in range(KV_W // LANES):
            halves = []
            for acc_ref in (dk_ref, dv_ref):
                even, odd = acc_ref[2 * ch], acc_ref[2 * ch + 1]
                even = even + pltpu.roll(even, HEAD_DIM, 1)
                odd = odd + pltpu.roll(odd, HEAD_DIM, 1)
                halves.append(jnp.where(lo, even, odd))
            dp_ref[:, Q_W + ch * LANES:Q_W + (ch + 1) * LANES] = _rope(halves[0], c, sn).astype(BF16)
            dp_ref[:, Q_W + KV_W + ch * LANES:Q_W + KV_W + (ch + 1) * LANES] = halves[1].astype(BF16)

    def seq_block(c):
        return pl.BlockSpec((per_step, length, c), lambda i: (i, 0, 0))

    table = pl.BlockSpec((per_step, length, LANES), lambda i: (i % (dil // per_step), 0, 0))
    (out,), exchanged = _hosted_call(
        body, exchange, name=f"attn_bwd_{tag}", grid=(nseq // per_step,),
        in_specs=[seq_block(QKV_W), seq_block(Q_W), seq_block(LANES), seq_block(LANES), table, table],
        out_specs=[seq_block(QKV_W)],
        out_shape=[jax.ShapeDtypeStruct((nseq, length, QKV_W), BF16)],
        scratch_shapes=[pltpu.VMEM((KV_W // LANES, length, LANES), BF16), pltpu.VMEM((KV_W // LANES, length, LANES), BF16),
                        pltpu.VMEM((N_KV, length, LANES), F32), pltpu.VMEM((N_KV, length, LANES), F32)],
        semantics=("parallel",), args=(rows_all, _seq_view(do), _seq_view(lse), _seq_view(delta), cos, sin))
    return out.reshape(shape), exchanged


def _qkv_bwd(dy, x, gain, w, dps, dils, seq, tag):
    t, d = x.shape
    ts = _tile_rows(seq)
    ng = len(dps)

    def body(dy_ref, x_ref, gain_ref, w_ref, *refs):
        dp_refs, (dx_ref, dgain_ref, stage_ref) = refs[:ng], refs[ng:]
        dh = None
        for gi in range(ng):
            dil = dils[gi]
            n = ts // dil
            dp = dp_refs[gi][0] if dil == 1 else jnp.concatenate([dp_refs[gi][r] for r in range(dil)], axis=0)
            part = jnp.dot(dp, w_ref[gi * QKV_W:(gi + 1) * QKV_W, :], preferred_element_type=F32)
            part = _merge_rows([part[r * n:(r + 1) * n] for r in range(dil)], stage_ref, dil)
            dh = part if dh is None else dh + part
        xv, gain_v = x_ref[...], gain_ref[...]
        dx, xhat = _rms_bwd(dh, xv, gain_v)
        dx_ref[...] = dy_ref[...] + dx
        _accumulate(dgain_ref, jnp.sum(dh * xhat, axis=0, keepdims=True), pl.program_id(0) == 0)

    row = pl.BlockSpec((ts, d), lambda i: (i, 0))
    return pl.pallas_call(
        body, name=f"qkv_bwd_{tag}", grid=(t // ts,),
        in_specs=[row, row, _resident((1, d)), _resident(w.shape)] + [_res_spec(seq, dl, QKV_W) for dl in dils],
        out_specs=[row, pl.BlockSpec((1, d), lambda i: (0, 0))],
        out_shape=[jax.ShapeDtypeStruct((t, d), F32), jax.ShapeDtypeStruct((1, d), F32)],
        scratch_shapes=[_stage(ts, d)], compiler_params=_params("arbitrary"),
    )(dy, x, gain, w, *dps)


ANY = pl.BlockSpec(memory_space=pl.ANY)


def _place():
    x, y, c = lax.axis_index("x"), lax.axis_index("y"), lax.axis_index("c")
    return x, y, c


def _exchange_steps(srcs, dsts, gather, send_sems, recv_sems, local_sems):
    x, y, c = _place()
    me, sibling = (x, y, c), (x, y, 1 - c)
    chips = [(1 - x, y), (x, 1 - y), (1 - x, 1 - y)]
    mine = 4 * x + 2 * y + c

    def slot(a, device):
        px, py, pc = device
        return dsts[a].at[4 * px + 2 * py + pc]

    def passes(a, k, block, to, src=None):
        rows = slot(a, block)
        return pltpu.make_async_remote_copy(src_ref=rows if src is None else src, dst_ref=rows, send_sem=send_sems.at[a, k],
                                            recv_sem=recv_sems.at[a, k], device_id=to, device_id_type=MESH)

    def scatters(a, k):
        peer = mine ^ k
        return pltpu.make_async_remote_copy(
            src_ref=srcs[a].at[peer], dst_ref=dsts[a].at[mine], send_sem=send_sems.at[a, k - 1], recv_sem=recv_sems.at[a, k - 1],
            device_id=(peer // 4, (peer // 2) % 2, peer % 2), device_id_type=MESH)

    def local(a):
        return pltpu.make_async_copy(srcs[a] if gather[a] else srcs[a].at[mine], dsts[a].at[mine], local_sems.at[a])

    def first_copies(a):
        if not gather[a]:
            return [scatters(a, k) for k in range(1, N_DEV)]
        return [passes(a, 0, me, sibling, src=srcs[a])] + [passes(a, 1 + j, me, (*chip, c), src=srcs[a]) for j, chip in enumerate(chips)]

    def start():
        for a in range(len(srcs)):
            local(a).start()
            for cp in first_copies(a):
                cp.start()

    def forward():
        for a in range(len(srcs)):
            if gather[a]:
                for j, chip in enumerate(chips):
                    passes(a, 1 + j, (*chip, c), me).wait_recv()
                    passes(a, 4 + j, (*chip, c), sibling).start()

    def finish():
        for a in range(len(srcs)):
            if gather[a]:
                passes(a, 0, sibling, me).wait_recv()
                for j, chip in enumerate(chips):
                    passes(a, 4 + j, (*chip, 1 - c), me).wait_recv()
                    passes(a, 4 + j, (*chip, c), sibling).wait_send()
                for cp in first_copies(a):
                    cp.wait_send()
            else:
                for cp in first_copies(a):
                    cp.wait()
            local(a).wait()

    return start, forward, finish


def _exchange_scratch(n):
    return [pltpu.SemaphoreType.DMA((n, N_DEV - 1)), pltpu.SemaphoreType.DMA((n, N_DEV - 1)), pltpu.SemaphoreType.DMA((n,))]


def _exchanged_shapes(exchange):
    return [jax.ShapeDtypeStruct(((N_DEV,) + a.shape) if g else a.shape, a.dtype) for a, g in exchange]


def _hosted_call(body, exchange, *, name, grid, in_specs, out_specs, out_shape, scratch_shapes, semantics, args,
                 input_output_aliases=None):
    single = not isinstance(out_shape, (list, tuple))
    out_specs, out_shape = ([out_specs], [out_shape]) if single else (list(out_specs), list(out_shape))
    scratch, aliases = list(scratch_shapes), dict(input_output_aliases or {})
    if not exchange:
        outs = pl.pallas_call(body, name=name, grid=grid, in_specs=in_specs, out_specs=out_specs, out_shape=out_shape,
                              scratch_shapes=scratch, input_output_aliases=aliases, compiler_params=_params(*semantics))(*args)
        return list(outs), []
    n, n_in, n_out, n_scr = len(exchange), len(in_specs), len(out_specs), len(scratch)
    gather = [g for _, g in exchange]
    steps = math.prod(grid)

    def hosted(*refs):
        own_in, x_in = refs[:n_in], refs[n_in:n_in + n]
        own_out, x_out = refs[n_in + n:n_in + n + n_out], refs[n_in + n + n_out:n_in + 2 * n + n_out]
        own_scr, sems = refs[n_in + 2 * n + n_out:n_in + 2 * n + n_out + n_scr], refs[-3:]
        step = pl.program_id(0)
        for axis in range(1, len(grid)):
            step = step * grid[axis] + pl.program_id(axis)
        start, forward, finish = _exchange_steps(x_in, x_out, gather, *sems)
        pl.when(step == 0)(start)
        body(*own_in, *own_out, *own_scr)
        pl.when(step == steps // 2)(forward)
        pl.when(step == steps - 1)(finish)

    outs = pl.pallas_call(
        hosted, name=name, grid=grid, in_specs=list(in_specs) + [ANY] * n, out_specs=out_specs + [ANY] * n,
        out_shape=out_shape + _exchanged_shapes(exchange), scratch_shapes=scratch + _exchange_scratch(n),
        input_output_aliases=aliases, compiler_params=_params(*["arbitrary"] * len(grid)),
    )(*args, *[a for a, _ in exchange])
    return list(outs[:n_out]), list(outs[n_out:])


def _exchange_now(exchange, name):
    n = len(exchange)
    gather = [g for _, g in exchange]

    def body(*refs):
        for step in _exchange_steps(refs[:n], refs[n:2 * n], gather, *refs[2 * n:]):
            step()

    return pl.pallas_call(
        body, name=name, in_specs=[ANY] * n, out_specs=[ANY] * n, out_shape=_exchanged_shapes(exchange),
        scratch_shapes=_exchange_scratch(n),
    )(*[a for a, _ in exchange])


def _all_reduce_small(v):
    def body(v_ref, o_ref, recv_ref, send_sems, recv_sems):
        x, y, c = _place()
        me = 4 * x + 2 * y + c
        copies = []
        for k in range(1, N_DEV):
            peer = me ^ k
            copies.append(pltpu.make_async_remote_copy(
                src_ref=v_ref, dst_ref=recv_ref.at[k], send_sem=send_sems.at[k - 1], recv_sem=recv_sems.at[k - 1],
                device_id=(peer // 4, (peer // 2) % 2, peer % 2), device_id_type=MESH))
        for cp in copies:
            cp.start()
        recv_ref[0] = v_ref[...]
        for cp in copies:
            cp.wait()
        acc = recv_ref[me]
        for src in range(1, N_DEV):
            acc = acc + recv_ref[me ^ src]
        o_ref[...] = acc

    vm = pl.BlockSpec(memory_space=pltpu.VMEM)
    return pl.pallas_call(
        body, name="all_reduce_small", in_specs=[vm], out_specs=vm, out_shape=jax.ShapeDtypeStruct(v.shape, F32),
        scratch_shapes=[pltpu.VMEM((N_DEV,) + v.shape, F32), pltpu.SemaphoreType.DMA((N_DEV - 1,)),
                        pltpu.SemaphoreType.DMA((N_DEV - 1,))],
    )(v)


def _adamw_math(w, g, m, v):
    m = ADAM_B1 * m + (1.0 - ADAM_B1) * g
    v = ADAM_B2 * v + (1.0 - ADAM_B2) * (g * g)
    m_hat = m / (1.0 - ADAM_B1 ** ADAM_STEP)
    v_hat = v / (1.0 - ADAM_B2 ** ADAM_STEP)
    delta = -ADAM_LR * (m_hat / (jnp.sqrt(v_hat) + ADAM_EPS) + ADAM_WD * w)
    return delta, m, v


def _adamw(parts, w, m, v, name, layer=None, into=None):
    r, c = w.shape[-2:]
    tr = r // 2 if r % 16 == 0 and r >= 256 else r
    n = len(parts)

    def body(*refs):
        w_ref, m_ref, v_ref = refs[n:n + 3]
        g_ref, d_ref, nm_ref, nv_ref = refs[-4:]
        g = refs[0][...].astype(F32)
        for p_ref in refs[1:n]:
            g = g + p_ref[...].astype(F32)
        g_ref[...] = g
        d_ref[...], nm_ref[...], nv_ref[...] = _adamw_math(w_ref[...], g, m_ref[...], v_ref[...])

    def slab(slot):
        return pl.BlockSpec((None, tr, c), lambda i: (slot, i, 0))

    tile = pl.BlockSpec((tr, c), lambda i: (i, 0)) if layer is None else slab(layer)
    arrays, in_specs = [], []
    for p in parts:
        if isinstance(p, tuple):
            arrays.append(p[0])
            in_specs.append(slab(p[1]))
        else:
            arrays.append(p)
            in_specs.append(tile)
    kept = list(into) if into is not None else []
    return pl.pallas_call(
        body, name=name, grid=(r // tr,), in_specs=in_specs + [tile] * 3 + [ANY] * len(kept), out_specs=[tile] * 4,
        out_shape=[jax.ShapeDtypeStruct(w.shape, F32)] * 4,
        input_output_aliases={n + 3 + k: k for k in range(len(kept))}, compiler_params=_params("parallel"),
    )(*arrays, w, m, v, *kept)


def _rows(g):
    return g.reshape(-1, g.shape[-1])


def _row_blocks(dw):
    k, n = dw.shape
    return dw.reshape(N_DEV, k // N_DEV, n)


def _pack_rows(rows, width):
    out = None
    for i, r in enumerate(rows):
        r = r.reshape(1, -1).astype(F32)
        r = jnp.pad(r, ((i, 8 - 1 - i), (0, width - r.shape[1])))
        out = r if out is None else out + r
    return out


def _mixer_fwd(x, gain, w_in, w_out, cos, sin, seq, groups, tag, sink=None, exchanges=None):
    exchanges = exchanges or {}
    os, lses, got = [], [], {}
    qkvs, hs, got["proj"] = _qkv_proj(x, gain, w_in, cos, sin, seq, [dil for dil, _ in groups], tag,
                                      exchange=exchanges.get("proj", ()))
    for gi, (dil, w) in enumerate(groups):
        o, lse, got[gi] = _attn_fwd(qkvs[gi], w, f"{tag}{gi}", sink=sink, exchange=exchanges.get(gi, ()))
        os.append(o)
        lses.append(lse)
    y, o, lses = _out_proj(x, os, lses, [dl for dl, _ in groups], w_out, seq, tag)
    return y, (qkvs, hs, o, lses), got


def _mixer_bwd(dy, x_in, gain, w_in, w_out, saved, cos, sin, seq, groups, tag, sink=None, exchanges=None, scatter_dw_out=False):
    qkvs, hs, o, lses = saved
    t, d = x_in.shape
    dils = [dl for dl, _ in groups]
    lse_tokens = lses[0].reshape(t, LANES) if sink is not None else None
    dw_out = _tn_matmul(o, dy, f"dw_out_{tag}")
    dos, dls, dsink, early = _attn_out_bwd(dy, w_out, o, dils, seq, tag, lse=lse_tokens, sink=sink,
                                           exchange=_to_send([dw_out]) if scatter_dw_out else ())
    if scatter_dw_out:
        (dw_out,) = early
    exchanges = exchanges or {}
    dps, got = [], {}
    for gi, (dil, w) in enumerate(groups):
        dp, got[gi] = _attn_bwd(qkvs[gi], dos[gi], lses[gi], dls[gi], _tables_by_residue(cos, seq, dil),
                                _tables_by_residue(sin, seq, dil), w, f"{tag}{gi}", exchange=exchanges.get(gi, ()))
        dps.append(dp)
    dx, dgain = _qkv_bwd(dy, x_in, gain, w_in, dps, dils, seq, tag)
    dw_in = None
    for gi in range(len(groups)):
        dw_in = _tn_matmul(dps[gi].reshape(t, QKV_W), hs[gi].reshape(t, d), f"dw_in_{tag}{gi}", into=dw_in, row_block=gi,
                           row_blocks=len(groups))
    return dx, dw_in, dw_out, dgain, dsink, got


def _ffn_layer_bwd(dy, x_in, gain, saved, wg, wu, wd, tag, exchange=()):
    g, u, act, h = saved
    dx, dg, du, dgain, got = _ffn_bwd(dy, x_in, gain, g, u, wg, wu, wd, tag, exchange=exchange)
    dwd = _tn_matmul(act, dy, f"dw_down_{tag}")
    dwg = _tn_matmul(dg, h, f"dw_gate_{tag}")
    dwu = _tn_matmul(du, h, f"dw_up_{tag}")
    return dx, dwg, dwu, dwd, dgain, got


def _to_send(dws):
    return [(_row_blocks(g), False) for g in dws]


def kernel(x, a_w_in, a_sink, a_w_out, b_w_in, b_w_out, norm_mix, norm_ffn, w_gate, w_up, w_down, final_norm, loss_target, m_a_w_in, m_a_sink, m_a_w_out, m_b_w_in, m_b_w_out, m_norm_mix, m_norm_ffn, m_w_gate, m_w_up, m_w_down, m_final_norm, v_a_w_in, v_a_sink, v_a_w_out, v_b_w_in, v_b_w_out, v_norm_mix, v_norm_ffn, v_w_gate, v_w_up, v_w_down, v_final_norm):
    bl, seq, d = x.shape
    t = bl * seq
    xf = x.reshape(t, d)
    target = loss_target.reshape(t, d)
    cos, sin = _rope_tables(seq)
    groups_a = [(1, ATTN_HALF_WINDOW)]
    groups_b = [(dil, window // 2 // dil) for window, dil in DILATED_GROUPS]

    def flip(w_):
        return jnp.swapaxes(w_, -1, -2)

    a_w_in, m_a_w_in, v_a_w_in, b_w_in, m_b_w_in, v_b_w_in = map(flip, (a_w_in, m_a_w_in, v_a_w_in, b_w_in, m_b_w_in, v_b_w_in))
    w_gate, m_w_gate, v_w_gate, w_up, m_w_up, v_w_up = map(flip, (w_gate, m_w_gate, v_w_gate, w_up, m_w_up, v_w_up))

    def shard(w_, layer):
        return (w_[layer].astype(BF16), True)

    wa_in, wa_out = map(_rows, _exchange_now([shard(a_w_in, 0), shard(a_w_out, 0)], "gather_first"))

    x1_0, saved_a, got = _mixer_fwd(xf, norm_mix[0:1], wa_in, wa_out, cos, sin, seq, groups_a, "a", sink=a_sink[0],
                                    exchanges={"proj": [shard(w_down, 0)], 0: [shard(w_gate, 0), shard(w_up, 0)]})
    wg0, wu0, wd0 = map(_rows, got[0] + got["proj"])
    x2_0, *saved_0, got = _ffn_fwd(x1_0, norm_ffn[0:1], wg0, wu0, wd0, "0", exchange=[shard(b_w_in, 0), shard(b_w_out, 0)])
    wb_in, wb_out = map(_rows, got)
    x1_1, saved_b, got = _mixer_fwd(x2_0, norm_mix[1:2], wb_in, wb_out, cos, sin, seq, groups_b, "b",
                                    exchanges={0: [shard(w_gate, 1)], 1: [shard(w_up, 1)], 2: [shard(w_down, 1)]})
    wg1, wu1, wd1 = map(_rows, got[0] + got[1] + got[2])
    dy, *saved_1, loss_part, d_final, _ = _ffn_fwd(x1_1, norm_ffn[1:2], wg1, wu1, wd1, "1",
                                                   loss_head=(final_norm.reshape(1, d), target))

    dy, dwg1, dwu1, dwd1, d_nf1, _ = _ffn_layer_bwd(dy, x1_1, norm_ffn[1:2], saved_1, wg1, wu1, wd1, "1")
    dy, dwb_in, dwb_out, d_nm1, _, got = _mixer_bwd(
        dy, x2_0, norm_mix[1:2], wb_in, wb_out, saved_b, cos, sin, seq, groups_b, "b",
        exchanges={0: _to_send([dwg1, dwd1]), 1: _to_send([dwu1])})
    (r_g1, r_d1), (r_u1,) = got[0], got[1]
    dy, dwg0, dwu0, dwd0, d_nf0, (r_b_in, r_b_out) = _ffn_layer_bwd(
        dy, x1_0, norm_ffn[0:1], saved_0, wg0, wu0, wd0, "0", exchange=_to_send([dwb_in, dwb_out]))
    dy, dwa_in, r_a_out, d_nm0, d_sink, got = _mixer_bwd(
        dy, xf, norm_mix[0:1], wa_in, wa_out, saved_a, cos, sin, seq, groups_a, "a", sink=a_sink[0],
        exchanges={0: _to_send([dwg0, dwu0, dwd0])}, scatter_dw_out=True)
    r_g0, r_u0, r_d0 = got[0]
    (r_a_in,) = _exchange_now(_to_send([dwa_in]), "scatter_last")
    grad_x = dy.reshape(bl, seq, d)

    def update(received, w_, m_, v_, name):
        out = None
        for layer in reversed(range(len(received))):
            out = _adamw([(received[layer], src) for src in range(N_DEV)], w_, m_, v_, f"adamw_{name}{layer}", layer=layer, into=out)
        return out

    u_a_in = update([r_a_in], a_w_in, m_a_w_in, v_a_w_in, "a_in")
    u_a_out = update([r_a_out], a_w_out, m_a_w_out, v_a_w_out, "a_out")
    u_b_in = update([r_b_in], b_w_in, m_b_w_in, v_b_w_in, "b_in")
    u_b_out = update([r_b_out], b_w_out, m_b_w_out, v_b_w_out, "b_out")
    u_gate = update([r_g0, r_g1], w_gate, m_w_gate, v_w_gate, "gate")
    u_up = update([r_u0, r_u1], w_up, m_w_up, v_w_up, "up")
    u_down = update([r_d0, r_d1], w_down, m_w_down, v_w_down, "down")

    small = _pack_rows([d_nm0, d_nm1, d_nf0, d_nf1, d_final, d_sink, loss_part], d)
    total = _all_reduce_small(small)
    small_w = _pack_rows([norm_mix[0], norm_mix[1], norm_ffn[0], norm_ffn[1], final_norm, a_sink], d)
    small_m = _pack_rows([m_norm_mix[0], m_norm_mix[1], m_norm_ffn[0], m_norm_ffn[1], m_final_norm, m_a_sink], d)
    small_v = _pack_rows([v_norm_mix[0], v_norm_mix[1], v_norm_ffn[0], v_norm_ffn[1], v_final_norm, v_a_sink], d)
    u_small = _adamw([total], small_w, small_m, small_v, "adamw_small")
    loss = total[6, 0]

    outs = []
    for k in range(4):
        sm = u_small[k]
        outs += [flip(u_a_in[k]), sm[5:6, :N_HEADS], u_a_out[k], flip(u_b_in[k]), u_b_out[k], sm[0:2], sm[2:4],
                 flip(u_gate[k]), flip(u_up[k]), u_down[k], sm[4]]
    return (loss, grad_x, *outs)
```

```python
import functools
import math

import jax
import jax.numpy as jnp
from jax import lax
from jax.experimental import pallas as pl
from jax.experimental.pallas import tpu as pltpu

F32 = jnp.float32
BF16 = jnp.bfloat16

HEAD_DIM = 64
N_HEADS = 16
N_KV = 4
GRP = N_HEADS // N_KV
Q_W = N_HEADS * HEAD_DIM
KV_W = N_KV * HEAD_DIM
QKV_W = Q_W + 2 * KV_W
ATTN_HALF_WINDOW = 128
DILATED_GROUPS = ((128, 1), (512, 4), (2048, 16))
ROPE_THETA = 10000.0
RMS_EPS = 1e-6
NEG_INF = -1e30
SCALE = 1.0 / math.sqrt(HEAD_DIM)

ADAM_LR = 0.001
ADAM_B1 = 0.9
ADAM_B2 = 0.999
ADAM_EPS = 1e-08
ADAM_WD = 0.01
ADAM_STEP = 10

LANES = 128
VMEM_LIMIT = 56 * 1024 * 1024
QUERY_BLOCK = 128
N_DEV = 8
MESH = pl.DeviceIdType.MESH

NT = (((1,), (1,)), ((), ()))
TN = (((0,), (0,)), ((), ()))


def _params(*sem):
    return pltpu.CompilerParams(dimension_semantics=tuple(sem) if sem else None, vmem_limit_bytes=VMEM_LIMIT)


def _resident(shape):
    return pl.BlockSpec(shape, lambda *_: (0,) * len(shape), pipeline_mode=pl.Buffered(1))


def _rope_tables(seq):
    inv_freq = 1.0 / (ROPE_THETA ** (jnp.arange(0, HEAD_DIM, 2, dtype=F32) / HEAD_DIM))
    ang = jnp.arange(seq, dtype=F32)[:, None] * inv_freq[None, :]
    cos, sin = jnp.cos(ang), jnp.sin(ang)
    return jnp.tile(cos, (1, 4)), jnp.concatenate([-sin, sin, -sin, sin], axis=1)


def _rope(t, cos, sin_signed):
    lane = lax.broadcasted_iota(jnp.int32, t.shape, 1)
    first = (lane & (HEAD_DIM // 2)) == 0
    swapped = jnp.where(first, pltpu.roll(t, LANES - HEAD_DIM // 2, 1), pltpu.roll(t, HEAD_DIM // 2, 1))
    return t * cos + swapped * sin_signed


def _rms(x):
    return lax.rsqrt(jnp.mean(x * x, axis=-1, keepdims=True) + RMS_EPS)


def _rms_bwd(dh, x, gain):
    r = _rms(x)
    xhat = x * r
    dxh = dh * gain
    dx = r * (dxh - xhat * jnp.mean(dxh * xhat, axis=-1, keepdims=True))
    return dx, xhat


def _accumulate(ref, value, first):
    @pl.when(first)
    def _():
        ref[...] = jnp.zeros_like(ref)

    ref[...] += value


def _tile_rows(seq):
    return min(512, seq)


def _res_shape(bl, seq, dil, c):
    ts = _tile_rows(seq)
    return (bl, dil, seq // ts, ts // dil, c)


def _res_spec(seq, dil, c):
    ts = _tile_rows(seq)
    per_seq = seq // ts
    return pl.BlockSpec((None, dil, None, ts // dil, c), lambda i: (i // per_seq, 0, i % per_seq, 0, 0))


def _seq_view(a):
    bl, dil, tiles, n, c = a.shape
    return a.reshape(bl * dil, tiles * n, c)


def _stage(ts, c):
    return pltpu.VMEM((c // LANES, ts, LANES), F32)


def _split_rows(val, stage_ref, dil):
    if dil == 1:
        return [val]
    ts, c = val.shape
    n, nc = ts // dil, c // LANES
    for k in range(nc):
        stage_ref[k] = val[:, k * LANES:(k + 1) * LANES]
    return [jnp.concatenate([stage_ref[k, pl.ds(r, n, stride=dil), :] for k in range(nc)], axis=1) for r in range(dil)]


def _merge_rows(parts, stage_ref, dil):
    if dil == 1:
        return parts[0]
    n, c = parts[0].shape
    nc = c // LANES
    for r, part in enumerate(parts):
        for k in range(nc):
            stage_ref[k, pl.ds(r, n, stride=dil), :] = part[:, k * LANES:(k + 1) * LANES]
    return jnp.concatenate([stage_ref[k] for k in range(nc)], axis=1)


def _tables_tiled(table, seq, dil):
    ts = _tile_rows(seq)
    return table.reshape(seq // ts, ts // dil, dil, LANES).transpose(0, 2, 1, 3).reshape(seq, LANES)


def _tables_by_residue(table, seq, dil):
    return table.reshape(seq // dil, dil, LANES).transpose(1, 0, 2)


def _qkv_proj(x, gain, w, cos, sin, seq, dils, tag, exchange=()):
    t, d = x.shape
    ts = _tile_rows(seq)
    per_seq = seq // ts
    ng = len(dils)
    tables = [t_ for dil in dils for t_ in (_tables_tiled(cos, seq, dil), _tables_tiled(sin, seq, dil))]

    def body(x_ref, g_ref, w_ref, *refs):
        table_refs, o_refs, h_refs, stage_ref = refs[:2 * ng], refs[2 * ng:3 * ng], refs[3 * ng:4 * ng], refs[4 * ng]
        xv = x_ref[...]
        h_tokens = xv * _rms(xv) * g_ref[...]
        for gi, dil in enumerate(dils):
            n = ts // dil
            h = jnp.concatenate(_split_rows(h_tokens, stage_ref, dil), axis=0).astype(BF16)
            for r in range(dil):
                h_refs[gi][r] = h[r * n:(r + 1) * n]
            acc = lax.dot_general(h, w_ref[gi * QKV_W:(gi + 1) * QKV_W, :], NT, preferred_element_type=F32)
            c, s = table_refs[2 * gi][...], table_refs[2 * gi + 1][...]
            for j in range(QKV_W // LANES):
                cols = slice(j * LANES, (j + 1) * LANES)
                val = acc[:, cols]
                if j < (Q_W + KV_W) // LANES:
                    val = _rope(val, c, s)
                if j < Q_W // LANES:
                    val = val * SCALE
                val = val.astype(BF16)
                for r in range(dil):
                    o_refs[gi][r, :, cols] = val[r * n:(r + 1) * n]

    table = pl.BlockSpec((ts, LANES), lambda i: (i % per_seq, 0))
    outs, exchanged = _hosted_call(
        body, exchange, name=f"qkv_proj_{tag}", grid=(t // ts,),
        in_specs=[pl.BlockSpec((ts, d), lambda i: (i, 0)), _resident((1, d)), _resident(w.shape)] + [table] * (2 * ng),
        out_specs=[_res_spec(seq, dil, QKV_W) for dil in dils] + [_res_spec(seq, dil, d) for dil in dils],
        out_shape=[jax.ShapeDtypeStruct(_res_shape(t // seq, seq, dil, QKV_W), BF16) for dil in dils]
                  + [jax.ShapeDtypeStruct(_res_shape(t // seq, seq, dil, d), BF16) for dil in dils],
        scratch_shapes=[_stage(ts, d)], semantics=("parallel",), args=(x, gain, w, *tables))
    return outs[:ng], outs[ng:], exchanged


def _band(bq, wk):
    return lax.broadcasted_iota(jnp.int32, (bq, wk), 0) - lax.broadcasted_iota(jnp.int32, (bq, wk), 1)


def _swap_halves(src_ref, base, dst_ref):
    for c in range(KV_W // LANES):
        dst_ref[c] = pltpu.roll(src_ref[:, base + c * LANES:base + (c + 1) * LANES], HEAD_DIM, 1)


def _pair_operand(src_ref, swapped_ref, base, kv, rows):
    c = kv // 2
    chunk, swapped = src_ref[rows, base + c * LANES:base + (c + 1) * LANES], swapped_ref[c, rows, :]
    lo = lax.broadcasted_iota(jnp.int32, chunk.shape, 1) < HEAD_DIM
    zero = jnp.zeros_like(chunk)
    if kv % 2 == 0:
        return jnp.concatenate([jnp.where(lo, chunk, zero), jnp.where(lo, zero, swapped)], axis=0)
    return jnp.concatenate([jnp.where(lo, swapped, zero), jnp.where(lo, zero, chunk)], axis=0)


def _over_keys(col, wk):
    if wk % LANES:
        return jnp.broadcast_to(col, (col.shape[0], wk))
    wide = jnp.broadcast_to(col, (col.shape[0], LANES))
    return wide if wk == LANES else jnp.concatenate([wide] * (wk // LANES), axis=1)


def _per_step(dil, length):
    return max(1, min(dil, 512 // length))


def _key_rows(bq, w, length):
    return min(bq + 2 * w, length)


def _window(i, bq, w, wk, length):
    q0 = pl.multiple_of(i * bq, bq)
    k0 = pl.multiple_of(jnp.clip(q0 - w, 0, length - wk), min(w, bq))
    return q0, k0


def _attn_fwd(qkv, w, tag, sink=None, exchange=()):
    shape = qkv.shape
    rows_all = _seq_view(qkv)
    nseq, length, _ = rows_all.shape
    bq = min(QUERY_BLOCK, length)
    wk = _key_rows(bq, w, length)
    nb = length // bq
    has_sink = sink is not None
    per_step = _per_step(shape[1], length)

    def body(*refs):
        sink_ref = refs[1] if has_sink else None
        kk_ref, vv_ref = refs[-2:]
        for sub in range(per_step):
            one(refs[0].at[sub], refs[-4].at[sub], refs[-3].at[sub], sink_ref, kk_ref, vv_ref)

    def one(qkv_ref, o_ref, lse_ref, sink_ref, kk_ref, vv_ref):
        _swap_halves(qkv_ref, Q_W, kk_ref)
        _swap_halves(qkv_ref, Q_W + KV_W, vv_ref)
        band = _band(bq, wk)
        lane = lax.broadcasted_iota(jnp.int32, (bq, LANES), 1)
        lo = lane < HEAD_DIM

        def block(i, carry):
            q0, k0 = _window(i, bq, w, wk, length)
            valid = jnp.abs(band + (q0 - k0)) <= w
            rows, krows = pl.ds(q0, bq), pl.ds(k0, wk)
            lse_tile = jnp.zeros((bq, LANES), F32)
            for kv in range(N_KV):
                heads = [(kv * GRP + h, h % 2) for h in range(GRP)]
                qp = [qkv_ref[rows, (kv * 2 + j) * LANES:(kv * 2 + j + 1) * LANES] for j in range(GRP // 2)]
                k2 = _pair_operand(qkv_ref, kk_ref, Q_W, kv, krows)
                v2 = _pair_operand(qkv_ref, vv_ref, Q_W + KV_W, kv, krows)
                sc2 = [lax.dot_general(q_, k2, NT, preferred_element_type=F32) for q_ in qp]
                sc = [jnp.where(valid, s_[:, half * wk:(half + 1) * wk], NEG_INF) for s_ in sc2 for half in range(2)]
                m = [jnp.max(s_, axis=-1, keepdims=True) for s_ in sc]
                if has_sink:
                    m = [jnp.maximum(m_, sink_ref[hd]) for m_, (hd, _) in zip(m, heads)]
                mb = [jnp.broadcast_to(m_, (bq, LANES)) for m_ in m]
                p = [jnp.exp(s_ - _over_keys(m_, wk)) for s_, m_ in zip(sc, m)]
                den = [jnp.sum(p_, axis=-1, keepdims=True) for p_ in p]
                if has_sink:
                    den = [d_ + jnp.exp(sink_ref[hd] - m_) for d_, m_, (hd, _) in zip(den, m, heads)]
                inv = [jnp.broadcast_to(1.0 / d_, (bq, LANES)) for d_ in den]
                pb = [p_.astype(BF16) for p_ in p]
                for j in range(GRP // 2):
                    o = jnp.dot(jnp.concatenate([pb[2 * j], pb[2 * j + 1]], axis=1), v2, preferred_element_type=F32)
                    o = o * jnp.where(lo, inv[2 * j], inv[2 * j + 1])
                    o_ref[rows, (kv * 2 + j) * LANES:(kv * 2 + j + 1) * LANES] = o.astype(BF16)
                for h, (hd, _) in enumerate(heads):
                    lse_tile = jnp.where(lane == hd, mb[h] - jnp.log(inv[h]), lse_tile)
            lse_ref[rows, :] = lse_tile
            return carry

        lax.fori_loop(0, nb, block, 0)

    def seq_block(c):
        return pl.BlockSpec((per_step, length, c), lambda i: (i, 0, 0))

    args = [rows_all]
    in_specs = [seq_block(QKV_W)]
    if has_sink:
        args.append(sink)
        in_specs.append(pl.BlockSpec(memory_space=pltpu.SMEM))
    (o, lse), exchanged = _hosted_call(
        body, exchange, name=f"attn_fwd_{tag}", grid=(nseq // per_step,), in_specs=in_specs,
        out_specs=[seq_block(Q_W), seq_block(LANES)],
        out_shape=[jax.ShapeDtypeStruct((nseq, length, Q_W), BF16), jax.ShapeDtypeStruct((nseq, length, LANES), F32)],
        scratch_shapes=[pltpu.VMEM((KV_W // LANES, length, LANES), BF16), pltpu.VMEM((KV_W // LANES, length, LANES), BF16)],
        semantics=("parallel",), args=args)
    return o.reshape(shape[:-1] + (Q_W,)), lse.reshape(shape[:-1] + (LANES,)), exchanged


def _head_expand():
    return (jnp.arange(LANES)[:, None] == jnp.arange(Q_W)[None, :] // HEAD_DIM).astype(BF16)


def _mix_groups(os, lses, dils, seq, tag):
    bl = os[0].shape[0]
    ts = _tile_rows(seq)
    t = bl * seq
    ng = len(os)
    if ng == 1 and dils[0] == 1:
        return os[0].reshape(t, Q_W), [lses[0]]

    def body(*refs):
        e_ref = refs[0]
        o_refs, l_refs = refs[1:1 + ng], refs[1 + ng:1 + 2 * ng]
        om_ref = refs[1 + 2 * ng]
        lt_refs = refs[2 + 2 * ng:2 + 3 * ng]
        wide_ref, narrow_ref = refs[2 + 3 * ng:]
        ls = [_merge_rows([l_refs[g][r] for r in range(dils[g])], narrow_ref, dils[g]) for g in range(ng)]
        mx = functools.reduce(jnp.maximum, ls)
        tot = mx + jnp.log(functools.reduce(lambda a, b: a + b, [jnp.exp(l_ - mx) for l_ in ls]))
        e = e_ref[...]
        o = None
        for g in range(ng):
            wt = jnp.exp(ls[g] - tot)
            hi = wt.astype(BF16)
            lo = (wt - hi.astype(F32)).astype(BF16)
            wide = jnp.dot(hi, e, preferred_element_type=F32) + jnp.dot(lo, e, preferred_element_type=F32)
            term = wide * _merge_rows([o_refs[g][r].astype(F32) for r in range(dils[g])], wide_ref, dils[g])
            o = term if o is None else o + term
        om_ref[...] = o.astype(BF16)
        for g in range(ng):
            for r, part in enumerate(_split_rows(tot, narrow_ref, dils[g])):
                lt_refs[g][r] = part

    e = _head_expand()
    outs = pl.pallas_call(
        body, name=f"mix_groups_{tag}", grid=(t // ts,),
        in_specs=[_resident(e.shape)] + [_res_spec(seq, dl, Q_W) for dl in dils] + [_res_spec(seq, dl, LANES) for dl in dils],
        out_specs=[pl.BlockSpec((ts, Q_W), lambda i: (i, 0))] + [_res_spec(seq, dl, LANES) for dl in dils],
        out_shape=[jax.ShapeDtypeStruct((t, Q_W), BF16)]
                  + [jax.ShapeDtypeStruct(_res_shape(bl, seq, dl, LANES), F32) for dl in dils],
        scratch_shapes=[_stage(ts, Q_W), _stage(ts, LANES)],
        compiler_params=_params("parallel"),
    )(e, *os, *lses)
    return outs[0], list(outs[1:])


def _sigmoid(g):
    return 1.0 / (1.0 + jnp.exp(-g))


def _ffn_fwd(x0, o, w_out, gain, wg, wu, wd, tag, exchange=(), loss_head=None):
    t, d = x0.shape
    f = wd.shape[0]
    tm = min(256, t)
    has_loss = loss_head is not None

    def body(*refs):
        x0_ref, o_ref, wo_ref, gain_ref, wg_ref, wu_ref, wd_ref = refs[:7]
        x_ref, y_ref, g_ref, u_ref, a_ref, h_ref = refs[-8:-2] if has_loss else refs[-6:]
        xv = x0_ref[...] + jnp.dot(o_ref[...], wo_ref[...], preferred_element_type=F32)
        x_ref[...] = xv
        h = (xv * _rms(xv) * gain_ref[...]).astype(BF16)
        h_ref[...] = h
        g = lax.dot_general(h, wg_ref[...], NT, preferred_element_type=F32)
        u = lax.dot_general(h, wu_ref[...], NT, preferred_element_type=F32)
        g_ref[...] = g.astype(BF16)
        u_ref[...] = u.astype(BF16)
        a = (g * _sigmoid(g) * u).astype(BF16)
        a_ref[...] = a
        y = xv + jnp.dot(a, wd_ref[...], preferred_element_type=F32)
        if not has_loss:
            y_ref[...] = y
            return
        head_ref, target_ref, loss_ref, dhead_ref = refs[7], refs[8], refs[-2], refs[-1]
        head = head_ref[...]
        yhat = y * _rms(y)
        err = yhat * head - target_ref[...]
        dout = err * (1.0 / d)
        y_ref[...] = _rms_bwd(dout, y, head)[0]
        first = pl.program_id(0) == 0
        part = 0.5 * jnp.sum(jnp.mean(err * err, axis=-1, keepdims=True), axis=0, keepdims=True)
        _accumulate(loss_ref, jnp.broadcast_to(part, loss_ref.shape), first)
        _accumulate(dhead_ref, jnp.sum(dout * yhat, axis=0, keepdims=True), first)

    row = pl.BlockSpec((tm, d), lambda i: (i, 0))
    wide = pl.BlockSpec((tm, f), lambda i: (i, 0))
    in_specs = [row, pl.BlockSpec((tm, Q_W), lambda i: (i, 0)), _resident(w_out.shape), _resident((1, d)), _resident(wg.shape),
                _resident(wu.shape), _resident(wd.shape)]
    out_specs = [row, row, wide, wide, wide, row]
    out_shape = ([jax.ShapeDtypeStruct((t, d), F32)] * 2 + [jax.ShapeDtypeStruct((t, f), BF16)] * 3
                 + [jax.ShapeDtypeStruct((t, d), BF16)])
    if has_loss:
        in_specs += [_resident((1, d)), row]
        out_specs += [pl.BlockSpec((1, LANES), lambda i: (0, 0)), pl.BlockSpec((1, d), lambda i: (0, 0))]
        out_shape += [jax.ShapeDtypeStruct((1, LANES), F32), jax.ShapeDtypeStruct((1, d), F32)]
    outs, exchanged = _hosted_call(
        body, exchange, name=f"ffn_fwd_{tag}", grid=(t // tm,), in_specs=in_specs, out_specs=out_specs, out_shape=out_shape,
        scratch_shapes=[], semantics=("arbitrary" if has_loss else "parallel",),
        args=(x0, o, w_out, gain, wg, wu, wd) + (tuple(loss_head) if has_loss else ()))
    return (*outs, exchanged)


def _ffn_bwd(dy, x, gain, g, u, wg, wu, wd, tag, exchange=()):
    t, d = x.shape
    f = wd.shape[0]
    tm = min(256, t)

    def body(dy_ref, x_ref, gain_ref, g_ref, u_ref, wg_ref, wu_ref, wd_ref, dx_ref, dg_ref, du_ref, dgain_ref):
        dyv = dy_ref[...]
        da = lax.dot_general(dyv.astype(BF16), wd_ref[...], NT, preferred_element_type=F32)
        gv, uv = g_ref[...].astype(F32), u_ref[...].astype(F32)
        sg = _sigmoid(gv)
        act = gv * sg
        du = (da * act).astype(BF16)
        dg = (da * uv * (sg * (1.0 + gv * (1.0 - sg)))).astype(BF16)
        du_ref[...] = du
        dg_ref[...] = dg
        dh = jnp.dot(dg, wg_ref[...], preferred_element_type=F32) + jnp.dot(du, wu_ref[...], preferred_element_type=F32)
        xv, gain_v = x_ref[...], gain_ref[...]
        dx, xhat = _rms_bwd(dh, xv, gain_v)
        dx_ref[...] = dyv + dx
        _accumulate(dgain_ref, jnp.sum(dh * xhat, axis=0, keepdims=True), pl.program_id(0) == 0)

    row = pl.BlockSpec((tm, d), lambda i: (i, 0))
    wide = pl.BlockSpec((tm, f), lambda i: (i, 0))
    outs, exchanged = _hosted_call(
        body, exchange, name=f"ffn_bwd_{tag}", grid=(t // tm,),
        in_specs=[row, row, _resident((1, d)), wide, wide, _resident(wg.shape), _resident(wu.shape), _resident(wd.shape)],
        out_specs=[row, wide, wide, pl.BlockSpec((1, d), lambda i: (0, 0))],
        out_shape=[jax.ShapeDtypeStruct((t, d), F32), jax.ShapeDtypeStruct((t, f), BF16), jax.ShapeDtypeStruct((t, f), BF16),
                   jax.ShapeDtypeStruct((1, d), F32)],
        scratch_shapes=[], semantics=("arbitrary",), args=(dy, x, gain, g, u, wg, wu, wd))
    return (*outs, exchanged)


def _tn_matmul(a, b, name, into=None, row_block=0, row_blocks=1):
    t, k = a.shape
    n = b.shape[1]
    tk = k // 2 if (k // 2) % LANES == 0 else k
    tt = min(2048, t)
    first = row_block * (k // tk)

    def body(a_ref, b_ref, *rest):
        o_ref, acc_ref = rest[-2:]
        prod = lax.dot_general(a_ref[...].astype(BF16), b_ref[...].astype(BF16), TN, preferred_element_type=F32)
        j = pl.program_id(1)

        @pl.when(j == 0)
        def _():
            acc_ref[...] = prod

        @pl.when(j > 0)
        def _():
            acc_ref[...] += prod

        @pl.when(j == pl.num_programs(1) - 1)
        def _():
            o_ref[...] = acc_ref[...].astype(BF16)

    return pl.pallas_call(
        body, name=name, grid=(k // tk, t // tt),
        in_specs=[pl.BlockSpec((tt, tk), lambda i, j: (j, i)), pl.BlockSpec((tt, n), lambda i, j: (j, 0))]
                 + ([ANY] if into is not None else []),
        out_specs=pl.BlockSpec((tk, n), lambda i, j: (first + i, 0)),
        out_shape=jax.ShapeDtypeStruct((row_blocks * k, n), BF16),
        scratch_shapes=[pltpu.VMEM((tk, n), F32)],
        input_output_aliases={2: 0} if into is not None else {},
        compiler_params=_params("parallel", "arbitrary"),
    )(a, b, *([into] if into is not None else []))


def _attn_out_bwd(dx, w, o, dils, seq, tag, lse=None, sink=None, exchange=()):
    t, d = dx.shape
    ts = _tile_rows(seq)
    bl = t // seq
    ng = len(dils)
    has_sink = sink is not None
    expand = _head_expand().T

    def body(*refs):
        refs = list(refs)
        dx_ref, w_ref, o_ref, e_ref = refs[:4]
        refs = refs[4:]
        lse_ref, sink_ref = (refs.pop(0), refs.pop(0)) if has_sink else (None, None)
        do_refs, dl_refs = refs[:ng], refs[ng:2 * ng]
        refs = refs[2 * ng:]
        dsink_ref = refs.pop(0) if has_sink else None
        dof_ref, dlf_ref = refs
        do = lax.dot_general(dx_ref[...].astype(BF16), w_ref[...], NT, preferred_element_type=F32)
        prod = do * o_ref[...].astype(F32)
        hi = prod.astype(BF16)
        lo = (prod - hi.astype(F32)).astype(BF16)
        e = e_ref[...]
        dl = jnp.dot(hi, e, preferred_element_type=F32) + jnp.dot(lo, e, preferred_element_type=F32)
        for g in range(ng):
            for r, part in enumerate(_split_rows(do, dof_ref, dils[g])):
                do_refs[g][r] = part.astype(BF16)
            for r, part in enumerate(_split_rows(dl, dlf_ref, dils[g])):
                dl_refs[g][r] = part
        if has_sink:
            part = -jnp.exp(sink_ref[...] - lse_ref[...]) * dl
            _accumulate(dsink_ref, jnp.sum(part, axis=0, keepdims=True), pl.program_id(0) == 0)

    row = pl.BlockSpec((ts, d), lambda i: (i, 0))
    narrow = pl.BlockSpec((ts, LANES), lambda i: (i, 0))
    args = [dx, w, o, expand]
    in_specs = [row, _resident(w.shape), pl.BlockSpec((ts, Q_W), lambda i: (i, 0)), _resident(expand.shape)]
    if has_sink:
        args += [lse, jnp.pad(sink.reshape(1, N_HEADS), ((0, 0), (0, LANES - N_HEADS)))]
        in_specs += [narrow, _resident((1, LANES))]
    out_specs = [_res_spec(seq, dl, Q_W) for dl in dils] + [_res_spec(seq, dl, LANES) for dl in dils]
    out_shape = ([jax.ShapeDtypeStruct(_res_shape(bl, seq, dl, Q_W), BF16) for dl in dils]
                 + [jax.ShapeDtypeStruct(_res_shape(bl, seq, dl, LANES), F32) for dl in dils])
    if has_sink:
        out_specs.append(pl.BlockSpec((1, LANES), lambda i: (0, 0)))
        out_shape.append(jax.ShapeDtypeStruct((1, LANES), F32))
    outs, exchanged = _hosted_call(
        body, exchange, name=f"attn_out_bwd_{tag}", grid=(t // ts,), in_specs=in_specs, out_specs=out_specs, out_shape=out_shape,
        scratch_shapes=[_stage(ts, Q_W), _stage(ts, LANES)], semantics=("arbitrary" if has_sink else "parallel",), args=args)
    return list(outs[:ng]), list(outs[ng:2 * ng]), (outs[2 * ng] if has_sink else None), exchanged


def _attn_bwd(qkv, do, lse, delta, cos, sin, w, tag, exchange=()):
    shape = qkv.shape
    dil = shape[1]
    rows_all = _seq_view(qkv)
    nseq, length, _ = rows_all.shape
    bq = min(QUERY_BLOCK, length)
    wk = _key_rows(bq, w, length)
    nb = length // bq
    per_step = _per_step(dil, length)

    def body(*refs):
        def sub(i, carry):
            one(*[ref.at[i] for ref in refs[:7]], *refs[7:])
            return carry

        if per_step == 1:
            sub(0, 0)
        else:
            lax.fori_loop(0, per_step, sub, 0)

    def one(qkv_ref, do_ref, lse_ref, dl_ref, cos_ref, sin_ref, dp_ref, kk_ref, vv_ref, dk_ref, dv_ref):
        _swap_halves(qkv_ref, Q_W, kk_ref)
        _swap_halves(qkv_ref, Q_W + KV_W, vv_ref)
        dk_ref[...] = jnp.zeros_like(dk_ref)
        dv_ref[...] = jnp.zeros_like(dv_ref)
        band = _band(bq, wk)
        lo_q = lax.broadcasted_iota(jnp.int32, (bq, LANES), 1) < HEAD_DIM
        hi_q = jnp.logical_not(lo_q)

        def block(i, carry):
            q0, k0 = _window(i, bq, w, wk, length)
            valid = jnp.abs(band + (q0 - k0)) <= w
            rows, krows = pl.ds(q0, bq), pl.ds(k0, wk)
            c, sn = cos_ref[rows, :], -sin_ref[rows, :]
            lse_t, dl_t = lse_ref[rows, :], dl_ref[rows, :]
            for kv in range(N_KV):
                heads = [(kv * GRP + h, h % 2) for h in range(GRP)]
                cols = [slice((kv * 2 + j) * LANES, (kv * 2 + j + 1) * LANES) for j in range(GRP // 2)]
                qp = [qkv_ref[rows, cs] for cs in cols]
                dop = [do_ref[rows, cs] for cs in cols]
                k2 = _pair_operand(qkv_ref, kk_ref, Q_W, kv, krows)
                v2 = _pair_operand(qkv_ref, vv_ref, Q_W + KV_W, kv, krows)
                sc2 = [lax.dot_general(q_, k2, NT, preferred_element_type=F32) for q_ in qp]
                dp2 = [lax.dot_general(d_, v2, NT, preferred_element_type=F32) for d_ in dop]
                sc = [s_[:, half * wk:(half + 1) * wk] for s_ in sc2 for half in range(2)]
                dp = [d_[:, half * wk:(half + 1) * wk] for d_ in dp2 for half in range(2)]
                p = [jnp.exp(jnp.where(valid, s_, NEG_INF) - _over_keys(lse_t[:, hd:hd + 1], wk))
                     for s_, (hd, _) in zip(sc, heads)]
                ds = [(p_ * (dp_ - _over_keys(dl_t[:, hd:hd + 1], wk))).astype(BF16) for p_, dp_, (hd, _) in zip(p, dp, heads)]
                pb = [p_.astype(BF16) for p_ in p]
                for j in range(GRP // 2):
                    dq = jnp.dot(jnp.concatenate([ds[2 * j], ds[2 * j + 1]], axis=1), k2, preferred_element_type=F32) * SCALE
                    dp_ref[rows, cols[j]] = _rope(dq, c, sn).astype(BF16)
                zero = jnp.zeros((bq, LANES), BF16)
                q4 = jnp.concatenate([jnp.where(lo_q if h % 2 == 0 else hi_q, qp[h // 2], zero) for h in range(GRP)], axis=0)
                do4 = jnp.concatenate([jnp.where(lo_q if h % 2 == 0 else hi_q, dop[h // 2], zero) for h in range(GRP)], axis=0)
                dk_ref[kv, krows, :] += lax.dot_general(jnp.concatenate(ds, axis=0), q4, TN, preferred_element_type=F32)
                dv_ref[kv, krows, :] += lax.dot_general(jnp.concatenate(pb, axis=0), do4, TN, preferred_element_type=F32)
            return carry

        lax.fori_loop(0, nb, block, 0)
        lo = lax.broadcasted_iota(jnp.int32, (length, LANES), 1) < HEAD_DIM
        c, sn = cos_ref[...], -sin_ref[...]
        for ch in range(KV_W // LANES):
            halves = []
            for acc_ref in (dk_ref, dv_ref):
                even, odd = acc_ref[2 * ch], acc_ref[2 * ch + 1]
                even = even + pltpu.roll(even, HEAD_DIM, 1)
                odd = odd + pltpu.roll(odd, HEAD_DIM, 1)
                halves.append(jnp.where(lo, even, odd))
            dp_ref[:, Q_W + ch * LANES:Q_W + (ch + 1) * LANES] = _rope(halves[0], c, sn).astype(BF16)
            dp_ref[:, Q_W + KV_W + ch * LANES:Q_W + KV_W + (ch + 1) * LANES] = halves[1].astype(BF16)

    def seq_block(c):
        return pl.BlockSpec((per_step, length, c), lambda i: (i, 0, 0))

    table = pl.BlockSpec((per_step, length, LANES), lambda i: (i % (dil // per_step), 0, 0))
    (out,), exchanged = _hosted_call(
        body, exchange, name=f"attn_bwd_{tag}", grid=(nseq // per_step,),
        in_specs=[seq_block(QKV_W), seq_block(Q_W), seq_block(LANES), seq_block(LANES), table, table],
        out_specs=[seq_block(QKV_W)],
        out_shape=[jax.ShapeDtypeStruct((nseq, length, QKV_W), BF16)],
        scratch_shapes=[pltpu.VMEM((KV_W // LANES, length, LANES), BF16), pltpu.VMEM((KV_W // LANES, length, LANES), BF16),
                        pltpu.VMEM((N_KV, length, LANES), F32), pltpu.VMEM((N_KV, length, LANES), F32)],
        semantics=("parallel",), args=(rows_all, _seq_view(do), _seq_view(lse), _seq_view(delta), cos, sin))
    return out.reshape(shape), exchanged


def _qkv_bwd(dy, x, gain, w, dps, dils, seq, tag):
    t, d = x.shape
    ts = _tile_rows(seq)
    ng = len(dps)

    def body(dy_ref, x_ref, gain_ref, w_ref, *refs):
        dp_refs, (dx_ref, dgain_ref, stage_ref) = refs[:ng], refs[ng:]
        dh = None
        for gi in range(ng):
            dil = dils[gi]
            n = ts // dil
            dp = dp_refs[gi][0] if dil == 1 else jnp.concatenate([dp_refs[gi][r] for r in range(dil)], axis=0)
            part = jnp.dot(dp, w_ref[gi * QKV_W:(gi + 1) * QKV_W, :], preferred_element_type=F32)
            part = _merge_rows([part[r * n:(r + 1) * n] for r in range(dil)], stage_ref, dil)
            dh = part if dh is None else dh + part
        xv, gain_v = x_ref[...], gain_ref[...]
        dx, xhat = _rms_bwd(dh, xv, gain_v)
        dx_ref[...] = dy_ref[...] + dx
        _accumulate(dgain_ref, jnp.sum(dh * xhat, axis=0, keepdims=True), pl.program_id(0) == 0)

    row = pl.BlockSpec((ts, d), lambda i: (i, 0))
    return pl.pallas_call(
        body, name=f"qkv_bwd_{tag}", grid=(t // ts,),
        in_specs=[row, row, _resident((1, d)), _resident(w.shape)] + [_res_spec(seq, dl, QKV_W) for dl in dils],
        out_specs=[row, pl.BlockSpec((1, d), lambda i: (0, 0))],
        out_shape=[jax.ShapeDtypeStruct((t, d), F32), jax.ShapeDtypeStruct((1, d), F32)],
        scratch_shapes=[_stage(ts, d)], compiler_params=_params("arbitrary"),
    )(dy, x, gain, w, *dps)


ANY = pl.BlockSpec(memory_space=pl.ANY)


def _place():
    x, y, c = lax.axis_index("x"), lax.axis_index("y"), lax.axis_index("c")
    return x, y, c


def _exchange_steps(srcs, dsts, gather, send_sems, recv_sems, local_sems):
    x, y, c = _place()
    me, sibling = (x, y, c), (x, y, 1 - c)
    chips = [(1 - x, y), (x, 1 - y), (1 - x, 1 - y)]
    mine = 4 * x + 2 * y + c

    def slot(a, device):
        px, py, pc = device
        return dsts[a].at[4 * px + 2 * py + pc]

    def passes(a, k, block, to, src=None):
        rows = slot(a, block)
        return pltpu.make_async_remote_copy(src_ref=rows if src is None else src, dst_ref=rows, send_sem=send_sems.at[a, k],
                                            recv_sem=recv_sems.at[a, k], device_id=to, device_id_type=MESH)

    def scatters(a, k):
        peer = mine ^ k
        return pltpu.make_async_remote_copy(
            src_ref=srcs[a].at[peer], dst_ref=dsts[a].at[mine], send_sem=send_sems.at[a, k - 1], recv_sem=recv_sems.at[a, k - 1],
            device_id=(peer // 4, (peer // 2) % 2, peer % 2), device_id_type=MESH)

    def local(a):
        return pltpu.make_async_copy(srcs[a] if gather[a] else srcs[a].at[mine], dsts[a].at[mine], local_sems.at[a])

    def first_copies(a):
        if not gather[a]:
            return [scatters(a, k) for k in range(1, N_DEV)]
        return [passes(a, 0, me, sibling, src=srcs[a])] + [passes(a, 1 + j, me, (*chip, c), src=srcs[a]) for j, chip in enumerate(chips)]

    def start():
        for a in range(len(srcs)):
            local(a).start()
            for cp in first_copies(a):
                cp.start()

    def forward():
        for a in range(len(srcs)):
            if gather[a]:
                for j, chip in enumerate(chips):
                    passes(a, 1 + j, (*chip, c), me).wait_recv()
                    passes(a, 4 + j, (*chip, c), sibling).start()

    def finish():
        for a in range(len(srcs)):
            if gather[a]:
                passes(a, 0, sibling, me).wait_recv()
                for j, chip in enumerate(chips):
                    passes(a, 4 + j, (*chip, 1 - c), me).wait_recv()
                    passes(a, 4 + j, (*chip, c), sibling).wait_send()
                for cp in first_copies(a):
                    cp.wait_send()
            else:
                for cp in first_copies(a):
                    cp.wait()
            local(a).wait()

    return start, forward, finish


def _exchange_scratch(n):
    return [pltpu.SemaphoreType.DMA((n, N_DEV - 1)), pltpu.SemaphoreType.DMA((n, N_DEV - 1)), pltpu.SemaphoreType.DMA((n,))]


def _exchanged_shapes(exchange):
    return [jax.ShapeDtypeStruct(((N_DEV,) + a.shape) if g else a.shape, a.dtype) for a, g in exchange]


def _hosted_call(body, exchange, *, name, grid, in_specs, out_specs, out_shape, scratch_shapes, semantics, args):
    out_specs, out_shape, scratch = list(out_specs), list(out_shape), list(scratch_shapes)
    if not exchange:
        outs = pl.pallas_call(body, name=name, grid=grid, in_specs=in_specs, out_specs=out_specs, out_shape=out_shape,
                              scratch_shapes=scratch, compiler_params=_params(*semantics))(*args)
        return list(outs), []
    n, n_in, n_out, n_scr = len(exchange), len(in_specs), len(out_specs), len(scratch)
    gather = [g for _, g in exchange]
    steps = math.prod(grid)

    def hosted(*refs):
        own_in, x_in = refs[:n_in], refs[n_in:n_in + n]
        own_out, x_out = refs[n_in + n:n_in + n + n_out], refs[n_in + n + n_out:n_in + 2 * n + n_out]
        own_scr, sems = refs[n_in + 2 * n + n_out:n_in + 2 * n + n_out + n_scr], refs[-3:]
        step = pl.program_id(0)
        for axis in range(1, len(grid)):
            step = step * grid[axis] + pl.program_id(axis)
        start, forward, finish = _exchange_steps(x_in, x_out, gather, *sems)
        pl.when(step == 0)(start)
        body(*own_in, *own_out, *own_scr)
        pl.when(step == steps // 2)(forward)
        pl.when(step == steps - 1)(finish)

    outs = pl.pallas_call(
        hosted, name=name, grid=grid, in_specs=list(in_specs) + [ANY] * n, out_specs=out_specs + [ANY] * n,
        out_shape=out_shape + _exchanged_shapes(exchange), scratch_shapes=scratch + _exchange_scratch(n),
        compiler_params=_params(*["arbitrary"] * len(grid)),
    )(*args, *[a for a, _ in exchange])
    return list(outs[:n_out]), list(outs[n_out:])


def _exchange_now(exchange, name):
    n = len(exchange)
    gather = [g for _, g in exchange]

    def body(*refs):
        for step in _exchange_steps(refs[:n], refs[n:2 * n], gather, *refs[2 * n:]):
            step()

    return pl.pallas_call(
        body, name=name, in_specs=[ANY] * n, out_specs=[ANY] * n, out_shape=_exchanged_shapes(exchange),
        scratch_shapes=_exchange_scratch(n),
    )(*[a for a, _ in exchange])


def _all_reduce_small(v):
    def body(v_ref, o_ref, recv_ref, send_sems, recv_sems):
        x, y, c = _place()
        me = 4 * x + 2 * y + c
        copies = []
        for k in range(1, N_DEV):
            peer = me ^ k
            copies.append(pltpu.make_async_remote_copy(
                src_ref=v_ref, dst_ref=recv_ref.at[k], send_sem=send_sems.at[k - 1], recv_sem=recv_sems.at[k - 1],
                device_id=(peer // 4, (peer // 2) % 2, peer % 2), device_id_type=MESH))
        for cp in copies:
            cp.start()
        recv_ref[0] = v_ref[...]
        for cp in copies:
            cp.wait()
        acc = recv_ref[me]
        for src in range(1, N_DEV):
            acc = acc + recv_ref[me ^ src]
        o_ref[...] = acc

    vm = pl.BlockSpec(memory_space=pltpu.VMEM)
    return pl.pallas_call(
        body, name="all_reduce_small", in_specs=[vm], out_specs=vm, out_shape=jax.ShapeDtypeStruct(v.shape, F32),
        scratch_shapes=[pltpu.VMEM((N_DEV,) + v.shape, F32), pltpu.SemaphoreType.DMA((N_DEV - 1,)),
                        pltpu.SemaphoreType.DMA((N_DEV - 1,))],
    )(v)


def _adamw_math(w, g, m, v):
    m = ADAM_B1 * m + (1.0 - ADAM_B1) * g
    v = ADAM_B2 * v + (1.0 - ADAM_B2) * (g * g)
    m_hat = m / (1.0 - ADAM_B1 ** ADAM_STEP)
    v_hat = v / (1.0 - ADAM_B2 ** ADAM_STEP)
    delta = -ADAM_LR * (m_hat / (jnp.sqrt(v_hat) + ADAM_EPS) + ADAM_WD * w)
    return delta, m, v


def _adamw(parts, w, m, v, name, layer=None, into=None):
    r, c = w.shape[-2:]
    tr = r // 2 if r % 16 == 0 and r >= 256 else r
    n = len(parts)

    def body(*refs):
        w_ref, m_ref, v_ref = refs[n:n + 3]
        g_ref, d_ref, nm_ref, nv_ref = refs[-4:]
        g = refs[0][...].astype(F32)
        for p_ref in refs[1:n]:
            g = g + p_ref[...].astype(F32)
        g_ref[...] = g
        d_ref[...], nm_ref[...], nv_ref[...] = _adamw_math(w_ref[...], g, m_ref[...], v_ref[...])

    def slab(slot):
        return pl.BlockSpec((None, tr, c), lambda i: (slot, i, 0))

    tile = pl.BlockSpec((tr, c), lambda i: (i, 0)) if layer is None else slab(layer)
    arrays, in_specs = [], []
    for p in parts:
        if isinstance(p, tuple):
            arrays.append(p[0])
            in_specs.append(slab(p[1]))
        else:
            arrays.append(p)
            in_specs.append(tile)
    kept = list(into) if into is not None else []
    return pl.pallas_call(
        body, name=name, grid=(r // tr,), in_specs=in_specs + [tile] * 3 + [ANY] * len(kept), out_specs=[tile] * 4,
        out_shape=[jax.ShapeDtypeStruct(w.shape, F32)] * 4,
        input_output_aliases={n + 3 + k: k for k in range(len(kept))}, compiler_params=_params("parallel"),
    )(*arrays, w, m, v, *kept)


def _rows(g):
    return g.reshape(-1, g.shape[-1])


def _row_blocks(dw):
    k, n = dw.shape
    return dw.reshape(N_DEV, k // N_DEV, n)


def _pack_rows(rows, width):
    out = None
    for i, r in enumerate(rows):
        r = r.reshape(1, -1).astype(F32)
        r = jnp.pad(r, ((i, 8 - 1 - i), (0, width - r.shape[1])))
        out = r if out is None else out + r
    return out


def _mixer_fwd(x, gain, w_in, cos, sin, seq, groups, tag, sink=None, exchanges=None):
    exchanges = exchanges or {}
    os, lses, got = [], [], {}
    qkvs, hs, got["proj"] = _qkv_proj(x, gain, w_in, cos, sin, seq, [dil for dil, _ in groups], tag,
                                      exchange=exchanges.get("proj", ()))
    for gi, (dil, w) in enumerate(groups):
        o, lse, got[gi] = _attn_fwd(qkvs[gi], w, f"{tag}{gi}", sink=sink, exchange=exchanges.get(gi, ()))
        os.append(o)
        lses.append(lse)
    o, lses = _mix_groups(os, lses, [dl for dl, _ in groups], seq, tag)
    return (qkvs, hs, o, lses), got


def _mixer_bwd(dy, x_in, gain, w_in, w_out, saved, cos, sin, seq, groups, tag, sink=None, exchanges=None, scatter_dw_out=False):
    qkvs, hs, o, lses = saved
    t, d = x_in.shape
    dils = [dl for dl, _ in groups]
    lse_tokens = lses[0].reshape(t, LANES) if sink is not None else None
    dw_out = _tn_matmul(o, dy, f"dw_out_{tag}")
    dos, dls, dsink, early = _attn_out_bwd(dy, w_out, o, dils, seq, tag, lse=lse_tokens, sink=sink,
                                           exchange=_to_send([dw_out]) if scatter_dw_out else ())
    if scatter_dw_out:
        (dw_out,) = early
    exchanges = exchanges or {}
    dps, got = [], {}
    for gi, (dil, w) in enumerate(groups):
        dp, got[gi] = _attn_bwd(qkvs[gi], dos[gi], lses[gi], dls[gi], _tables_by_residue(cos, seq, dil),
                                _tables_by_residue(sin, seq, dil), w, f"{tag}{gi}", exchange=exchanges.get(gi, ()))
        dps.append(dp)
    dx, dgain = _qkv_bwd(dy, x_in, gain, w_in, dps, dils, seq, tag)
    dw_in = None
    for gi in range(len(groups)):
        dw_in = _tn_matmul(dps[gi].reshape(t, QKV_W), hs[gi].reshape(t, d), f"dw_in_{tag}{gi}", into=dw_in, row_block=gi,
                           row_blocks=len(groups))
    return dx, dw_in, dw_out, dgain, dsink, got


def _ffn_layer_bwd(dy, x_in, gain, saved, wg, wu, wd, tag, exchange=()):
    g, u, act, h = saved
    dx, dg, du, dgain, got = _ffn_bwd(dy, x_in, gain, g, u, wg, wu, wd, tag, exchange=exchange)
    dwd = _tn_matmul(act, dy, f"dw_down_{tag}")
    dwg = _tn_matmul(dg, h, f"dw_gate_{tag}")
    dwu = _tn_matmul(du, h, f"dw_up_{tag}")
    return dx, dwg, dwu, dwd, dgain, got


def _to_send(dws):
    return [(_row_blocks(g), False) for g in dws]


def kernel(x, a_w_in, a_sink, a_w_out, b_w_in, b_w_out, norm_mix, norm_ffn, w_gate, w_up, w_down, final_norm, loss_target, m_a_w_in, m_a_sink, m_a_w_out, m_b_w_in, m_b_w_out, m_norm_mix, m_norm_ffn, m_w_gate, m_w_up, m_w_down, m_final_norm, v_a_w_in, v_a_sink, v_a_w_out, v_b_w_in, v_b_w_out, v_norm_mix, v_norm_ffn, v_w_gate, v_w_up, v_w_down, v_final_norm):
    bl, seq, d = x.shape
    t = bl * seq
    xf = x.reshape(t, d)
    target = loss_target.reshape(t, d)
    cos, sin = _rope_tables(seq)
    groups_a = [(1, ATTN_HALF_WINDOW)]
    groups_b = [(dil, window // 2 // dil) for window, dil in DILATED_GROUPS]

    def flip(w_):
        return jnp.swapaxes(w_, -1, -2)

    a_w_in, m_a_w_in, v_a_w_in, b_w_in, m_b_w_in, v_b_w_in = map(flip, (a_w_in, m_a_w_in, v_a_w_in, b_w_in, m_b_w_in, v_b_w_in))
    w_gate, m_w_gate, v_w_gate, w_up, m_w_up, v_w_up = map(flip, (w_gate, m_w_gate, v_w_gate, w_up, m_w_up, v_w_up))

    def shard(w_, layer):
        return (w_[layer].astype(BF16), True)

    (wa_in,) = map(_rows, _exchange_now([shard(a_w_in, 0)], "gather_first"))

    saved_a, got = _mixer_fwd(xf, norm_mix[0:1], wa_in, cos, sin, seq, groups_a, "a", sink=a_sink[0],
                              exchanges={"proj": [shard(w_down, 0), shard(a_w_out, 0)], 0: [shard(w_gate, 0), shard(w_up, 0)]})
    wg0, wu0, wd0, wa_out = map(_rows, got[0] + got["proj"])
    x1_0, x2_0, *saved_0, got = _ffn_fwd(xf, saved_a[2], wa_out, norm_ffn[0:1], wg0, wu0, wd0, "0",
                                         exchange=[shard(b_w_in, 0), shard(b_w_out, 0)])
    wb_in, wb_out = map(_rows, got)
    saved_b, got = _mixer_fwd(x2_0, norm_mix[1:2], wb_in, cos, sin, seq, groups_b, "b",
                              exchanges={0: [shard(w_gate, 1)], 1: [shard(w_up, 1)], 2: [shard(w_down, 1)]})
    wg1, wu1, wd1 = map(_rows, got[0] + got[1] + got[2])
    x1_1, dy, *saved_1, loss_part, d_final, _ = _ffn_fwd(x2_0, saved_b[2], wb_out, norm_ffn[1:2], wg1, wu1, wd1, "1",
                                                         loss_head=(final_norm.reshape(1, d), target))

    dy, dwg1, dwu1, dwd1, d_nf1, _ = _ffn_layer_bwd(dy, x1_1, norm_ffn[1:2], saved_1, wg1, wu1, wd1, "1")
    dy, dwb_in, dwb_out, d_nm1, _, got = _mixer_bwd(
        dy, x2_0, norm_mix[1:2], wb_in, wb_out, saved_b, cos, sin, seq, groups_b, "b",
        exchanges={0: _to_send([dwg1, dwd1]), 1: _to_send([dwu1])})
    (r_g1, r_d1), (r_u1,) = got[0], got[1]
    dy, dwg0, dwu0, dwd0, d_nf0, (r_b_in, r_b_out) = _ffn_layer_bwd(
        dy, x1_0, norm_ffn[0:1], saved_0, wg0, wu0, wd0, "0", exchange=_to_send([dwb_in, dwb_out]))
    dy, dwa_in, r_a_out, d_nm0, d_sink, got = _mixer_bwd(
        dy, xf, norm_mix[0:1], wa_in, wa_out, saved_a, cos, sin, seq, groups_a, "a", sink=a_sink[0],
        exchanges={0: _to_send([dwg0, dwu0, dwd0])}, scatter_dw_out=True)
    r_g0, r_u0, r_d0 = got[0]
    (r_a_in,) = _exchange_now(_to_send([dwa_in]), "scatter_last")
    grad_x = dy.reshape(bl, seq, d)

    def update(received, w_, m_, v_, name):
        out = None
        for layer in reversed(range(len(received))):
            out = _adamw([(received[layer], src) for src in range(N_DEV)], w_, m_, v_, f"adamw_{name}{layer}", layer=layer, into=out)
        return out

    u_a_in = update([r_a_in], a_w_in, m_a_w_in, v_a_w_in, "a_in")
    u_a_out = update([r_a_out], a_w_out, m_a_w_out, v_a_w_out, "a_out")
    u_b_in = update([r_b_in], b_w_in, m_b_w_in, v_b_w_in, "b_in")
    u_b_out = update([r_b_out], b_w_out, m_b_w_out, v_b_w_out, "b_out")
    u_gate = update([r_g0, r_g1], w_gate, m_w_gate, v_w_gate, "gate")
    u_up = update([r_u0, r_u1], w_up, m_w_up, v_w_up, "up")
    u_down = update([r_d0, r_d1], w_down, m_w_down, v_w_down, "down")

    small = _pack_rows([d_nm0, d_nm1, d_nf0, d_nf1, d_final, d_sink, loss_part], d)
    total = _all_reduce_small(small)
    small_w = _pack_rows([norm_mix[0], norm_mix[1], norm_ffn[0], norm_ffn[1], final_norm, a_sink], d)
    small_m = _pack_rows([m_norm_mix[0], m_norm_mix[1], m_norm_ffn[0], m_norm_ffn[1], m_final_norm, m_a_sink], d)
    small_v = _pack_rows([v_norm_mix[0], v_norm_mix[1], v_norm_ffn[0], v_norm_ffn[1], v_final_norm, v_a_sink], d)
    u_small = _adamw([total], small_w, small_m, small_v, "adamw_small")
    loss = total[6, 0]

    outs = []
    for k in range(4):
        sm = u_small[k]
        outs += [flip(u_a_in[k]), sm[5:6, :N_HEADS], u_a_out[k], flip(u_b_in[k]), u_b_out[k], sm[0:2], sm[2:4],
                 flip(u_gate[k]), flip(u_up[k]), u_down[k], sm[4]]
    return (loss, grad_x, *outs)
```

```python
import functools
import math

import jax
import jax.numpy as jnp
from jax import lax
from jax.experimental import pallas as pl
from jax.experimental.pallas import tpu as pltpu

F32 = jnp.float32
BF16 = jnp.bfloat16

HEAD_DIM = 64
N_HEADS = 16
N_KV = 4
GRP = N_HEADS // N_KV
Q_W = N_HEADS * HEAD_DIM
KV_W = N_KV * HEAD_DIM
QKV_W = Q_W + 2 * KV_W
ATTN_HALF_WINDOW = 128
DILATED_GROUPS = ((128, 1), (512, 4), (2048, 16))
ROPE_THETA = 10000.0
RMS_EPS = 1e-6
NEG_INF = -1e30
SCALE = 1.0 / math.sqrt(HEAD_DIM)
LOG2E = 1.0 / math.log(2.0)
LN2 = math.log(2.0)

ADAM_LR = 0.001
ADAM_B1 = 0.9
ADAM_B2 = 0.999
ADAM_EPS = 1e-08
ADAM_WD = 0.01
ADAM_STEP = 10

LANES = 128
VMEM_LIMIT = 56 * 1024 * 1024
QUERY_BLOCK = 128
N_DEV = 8
MESH = pl.DeviceIdType.MESH

NT = (((1,), (1,)), ((), ()))
TN = (((0,), (0,)), ((), ()))


def _params(*sem):
    return pltpu.CompilerParams(dimension_semantics=tuple(sem) if sem else None, vmem_limit_bytes=VMEM_LIMIT)


def _resident(shape):
    return pl.BlockSpec(shape, lambda *_: (0,) * len(shape), pipeline_mode=pl.Buffered(1))


def _rope_tables(seq):
    inv_freq = 1.0 / (ROPE_THETA ** (jnp.arange(0, HEAD_DIM, 2, dtype=F32) / HEAD_DIM))
    ang = jnp.arange(seq, dtype=F32)[:, None] * inv_freq[None, :]
    cos, sin = jnp.cos(ang), jnp.sin(ang)
    return jnp.tile(cos, (1, 4)), jnp.concatenate([-sin, sin, -sin, sin], axis=1)


def _rope(t, cos, sin_signed):
    lane = lax.broadcasted_iota(jnp.int32, t.shape, 1)
    first = (lane & (HEAD_DIM // 2)) == 0
    swapped = jnp.where(first, pltpu.roll(t, LANES - HEAD_DIM // 2, 1), pltpu.roll(t, HEAD_DIM // 2, 1))
    return t * cos + swapped * sin_signed


def _rms(x):
    return lax.rsqrt(jnp.mean(x * x, axis=-1, keepdims=True) + RMS_EPS)


def _rms_bwd(dh, x, gain):
    r = _rms(x)
    xhat = x * r
    dxh = dh * gain
    dx = r * (dxh - xhat * jnp.mean(dxh * xhat, axis=-1, keepdims=True))
    return dx, xhat


def _accumulate(ref, value, first):
    @pl.when(first)
    def _():
        ref[...] = jnp.zeros_like(ref)

    ref[...] += value


def _tile_rows(seq):
    return min(512, seq)


def _res_shape(bl, seq, dil, c):
    ts = _tile_rows(seq)
    return (bl, dil, seq // ts, ts // dil, c)


def _res_spec(seq, dil, c):
    ts = _tile_rows(seq)
    per_seq = seq // ts
    return pl.BlockSpec((None, dil, None, ts // dil, c), lambda i: (i // per_seq, 0, i % per_seq, 0, 0))


def _seq_view(a):
    bl, dil, tiles, n, c = a.shape
    return a.reshape(bl * dil, tiles * n, c)


def _stage(ts, c):
    return pltpu.VMEM((c // LANES, ts, LANES), F32)


def _split_rows(val, stage_ref, dil):
    if dil == 1:
        return [val]
    ts, c = val.shape
    n, nc = ts // dil, c // LANES
    for k in range(nc):
        stage_ref[k] = val[:, k * LANES:(k + 1) * LANES]
    return [jnp.concatenate([stage_ref[k, pl.ds(r, n, stride=dil), :] for k in range(nc)], axis=1) for r in range(dil)]


def _merge_rows(parts, stage_ref, dil):
    if dil == 1:
        return parts[0]
    n, c = parts[0].shape
    nc = c // LANES
    for r, part in enumerate(parts):
        for k in range(nc):
            stage_ref[k, pl.ds(r, n, stride=dil), :] = part[:, k * LANES:(k + 1) * LANES]
    return jnp.concatenate([stage_ref[k] for k in range(nc)], axis=1)


def _tables_tiled(table, seq, dil):
    ts = _tile_rows(seq)
    return table.reshape(seq // ts, ts // dil, dil, LANES).transpose(0, 2, 1, 3).reshape(seq, LANES)


def _tables_by_residue(table, seq, dil):
    return table.reshape(seq // dil, dil, LANES).transpose(1, 0, 2)


def _qkv_proj(x, gain, w, cos, sin, seq, dils, tag, exchange=()):
    t, d = x.shape
    ts = _tile_rows(seq)
    per_seq = seq // ts
    ng = len(dils)
    tables = [t_ for dil in dils for t_ in (_tables_tiled(cos, seq, dil), _tables_tiled(sin, seq, dil))]

    def body(x_ref, g_ref, w_ref, *refs):
        table_refs, o_refs, h_refs, stage_ref = refs[:2 * ng], refs[2 * ng:3 * ng], refs[3 * ng:4 * ng], refs[4 * ng]
        xv = x_ref[...]
        h_tokens = xv * _rms(xv) * g_ref[...]
        for gi, dil in enumerate(dils):
            n = ts // dil
            h = jnp.concatenate(_split_rows(h_tokens, stage_ref, dil), axis=0).astype(BF16)
            for r in range(dil):
                h_refs[gi][r] = h[r * n:(r + 1) * n]
            acc = lax.dot_general(h, w_ref[gi * QKV_W:(gi + 1) * QKV_W, :], NT, preferred_element_type=F32)
            c, s = table_refs[2 * gi][...], table_refs[2 * gi + 1][...]
            for j in range(QKV_W // LANES):
                cols = slice(j * LANES, (j + 1) * LANES)
                val = acc[:, cols]
                if j < (Q_W + KV_W) // LANES:
                    val = _rope(val, c, s)
                if j < Q_W // LANES:
                    val = val * (SCALE * LOG2E)
                val = val.astype(BF16)
                for r in range(dil):
                    o_refs[gi][r, :, cols] = val[r * n:(r + 1) * n]

    table = pl.BlockSpec((ts, LANES), lambda i: (i % per_seq, 0))
    outs, exchanged = _hosted_call(
        body, exchange, name=f"qkv_proj_{tag}", grid=(t // ts,),
        in_specs=[pl.BlockSpec((ts, d), lambda i: (i, 0)), _resident((1, d)), _resident(w.shape)] + [table] * (2 * ng),
        out_specs=[_res_spec(seq, dil, QKV_W) for dil in dils] + [_res_spec(seq, dil, d) for dil in dils],
        out_shape=[jax.ShapeDtypeStruct(_res_shape(t // seq, seq, dil, QKV_W), BF16) for dil in dils]
                  + [jax.ShapeDtypeStruct(_res_shape(t // seq, seq, dil, d), BF16) for dil in dils],
        scratch_shapes=[_stage(ts, d)], semantics=("parallel",), args=(x, gain, w, *tables))
    return outs[:ng], outs[ng:], exchanged


def _band(bq, wk):
    return lax.broadcasted_iota(jnp.int32, (bq, wk), 0) - lax.broadcasted_iota(jnp.int32, (bq, wk), 1)


def _swap_halves(src_ref, base, dst_ref):
    for c in range(KV_W // LANES):
        dst_ref[c] = pltpu.roll(src_ref[:, base + c * LANES:base + (c + 1) * LANES], HEAD_DIM, 1)


def _pair_operand(src_ref, swapped_ref, base, kv, rows):
    c = kv // 2
    chunk, swapped = src_ref[rows, base + c * LANES:base + (c + 1) * LANES], swapped_ref[c, rows, :]
    lo = lax.broadcasted_iota(jnp.int32, chunk.shape, 1) < HEAD_DIM
    zero = jnp.zeros_like(chunk)
    if kv % 2 == 0:
        return jnp.concatenate([jnp.where(lo, chunk, zero), jnp.where(lo, zero, swapped)], axis=0)
    return jnp.concatenate([jnp.where(lo, swapped, zero), jnp.where(lo, zero, chunk)], axis=0)


def _over_keys(col, wk):
    if wk % LANES:
        return jnp.broadcast_to(col, (col.shape[0], wk))
    wide = jnp.broadcast_to(col, (col.shape[0], LANES))
    return wide if wk == LANES else jnp.concatenate([wide] * (wk // LANES), axis=1)


def _per_step(dil, length):
    return max(1, min(dil, 512 // length))


def _key_rows(bq, w, length):
    return min(bq + 2 * w, length)


def _window(i, bq, w, wk, length):
    q0 = pl.multiple_of(i * bq, bq)
    k0 = pl.multiple_of(jnp.clip(q0 - w, 0, length - wk), min(w, bq))
    return q0, k0


def _attn_fwd(qkv, w, tag, sink=None, exchange=()):
    shape = qkv.shape
    rows_all = _seq_view(qkv)
    nseq, length, _ = rows_all.shape
    bq = min(QUERY_BLOCK, length)
    wk = _key_rows(bq, w, length)
    nb = length // bq
    has_sink = sink is not None
    per_step = _per_step(shape[1], length)

    def body(*refs):
        sink_ref = refs[1] if has_sink else None
        kk_ref, vv_ref = refs[-2:]
        for sub in range(per_step):
            one(refs[0].at[sub], refs[-4].at[sub], refs[-3].at[sub], sink_ref, kk_ref, vv_ref)

    def one(qkv_ref, o_ref, lse_ref, sink_ref, kk_ref, vv_ref):
        _swap_halves(qkv_ref, Q_W, kk_ref)
        _swap_halves(qkv_ref, Q_W + KV_W, vv_ref)
        band = _band(bq, wk)
        lane = lax.broadcasted_iota(jnp.int32, (bq, LANES), 1)
        lo = lane < HEAD_DIM

        def block(i, carry):
            q0, k0 = _window(i, bq, w, wk, length)
            valid = jnp.abs(band + (q0 - k0)) <= w
            rows, krows = pl.ds(q0, bq), pl.ds(k0, wk)
            lse_tile = jnp.zeros((bq, LANES), F32)
            for kv in range(N_KV):
                heads = [(kv * GRP + h, h % 2) for h in range(GRP)]
                qp = [qkv_ref[rows, (kv * 2 + j) * LANES:(kv * 2 + j + 1) * LANES] for j in range(GRP // 2)]
                k2 = _pair_operand(qkv_ref, kk_ref, Q_W, kv, krows)
                v2 = _pair_operand(qkv_ref, vv_ref, Q_W + KV_W, kv, krows)
                sc2 = [lax.dot_general(q_, k2, NT, preferred_element_type=F32) for q_ in qp]
                sc = [jnp.where(valid, s_[:, half * wk:(half + 1) * wk], NEG_INF) for s_ in sc2 for half in range(2)]
                m = [jnp.max(s_, axis=-1, keepdims=True) for s_ in sc]
                if has_sink:
                    m = [jnp.maximum(m_, sink_ref[hd]) for m_, (hd, _) in zip(m, heads)]
                mb = [jnp.broadcast_to(m_, (bq, LANES)) for m_ in m]
                p = [jnp.exp2(s_ - _over_keys(m_, wk)) for s_, m_ in zip(sc, m)]
                den = [jnp.sum(p_, axis=-1, keepdims=True) for p_ in p]
                if has_sink:
                    den = [d_ + jnp.exp2(sink_ref[hd] - m_) for d_, m_, (hd, _) in zip(den, m, heads)]
                inv = [jnp.broadcast_to(1.0 / d_, (bq, LANES)) for d_ in den]
                pb = [p_.astype(BF16) for p_ in p]
                for j in range(GRP // 2):
                    o = jnp.dot(jnp.concatenate([pb[2 * j], pb[2 * j + 1]], axis=1), v2, preferred_element_type=F32)
                    o = o * jnp.where(lo, inv[2 * j], inv[2 * j + 1])
                    o_ref[rows, (kv * 2 + j) * LANES:(kv * 2 + j + 1) * LANES] = o.astype(BF16)
                for h, (hd, _) in enumerate(heads):
                    lse_tile = jnp.where(lane == hd, mb[h] - jnp.log(inv[h]) * LOG2E, lse_tile)
            lse_ref[rows, :] = lse_tile
            return carry

        lax.fori_loop(0, nb, block, 0)

    def seq_block(c):
        return pl.BlockSpec((per_step, length, c), lambda i: (i, 0, 0))

    args = [rows_all]
    in_specs = [seq_block(QKV_W)]
    if has_sink:
        args.append(sink * LOG2E)
        in_specs.append(pl.BlockSpec(memory_space=pltpu.SMEM))
    (o, lse), exchanged = _hosted_call(
        body, exchange, name=f"attn_fwd_{tag}", grid=(nseq // per_step,), in_specs=in_specs,
        out_specs=[seq_block(Q_W), seq_block(LANES)],
        out_shape=[jax.ShapeDtypeStruct((nseq, length, Q_W), BF16), jax.ShapeDtypeStruct((nseq, length, LANES), F32)],
        scratch_shapes=[pltpu.VMEM((KV_W // LANES, length, LANES), BF16), pltpu.VMEM((KV_W // LANES, length, LANES), BF16)],
        semantics=("parallel",), args=args)
    return o.reshape(shape[:-1] + (Q_W,)), lse.reshape(shape[:-1] + (LANES,)), exchanged


def _head_expand():
    return (jnp.arange(LANES)[:, None] == jnp.arange(Q_W)[None, :] // HEAD_DIM).astype(BF16)


def _mix_groups(os, lses, dils, seq, tag):
    bl = os[0].shape[0]
    ts = _tile_rows(seq)
    t = bl * seq
    ng = len(os)
    if ng == 1 and dils[0] == 1:
        return os[0].reshape(t, Q_W), [lses[0]]

    def body(*refs):
        e_ref = refs[0]
        o_refs, l_refs = refs[1:1 + ng], refs[1 + ng:1 + 2 * ng]
        om_ref = refs[1 + 2 * ng]
        lt_refs = refs[2 + 2 * ng:2 + 3 * ng]
        wide_ref, narrow_ref = refs[2 + 3 * ng:]
        ls = [_merge_rows([l_refs[g][r] for r in range(dils[g])], narrow_ref, dils[g]) for g in range(ng)]
        mx = functools.reduce(jnp.maximum, ls)
        tot = mx + jnp.log(functools.reduce(lambda a, b: a + b, [jnp.exp2(l_ - mx) for l_ in ls])) * LOG2E
        e = e_ref[...]
        o = None
        for g in range(ng):
            wt = jnp.exp2(ls[g] - tot)
            hi = wt.astype(BF16)
            lo = (wt - hi.astype(F32)).astype(BF16)
            wide = jnp.dot(hi, e, preferred_element_type=F32) + jnp.dot(lo, e, preferred_element_type=F32)
            term = wide * _merge_rows([o_refs[g][r].astype(F32) for r in range(dils[g])], wide_ref, dils[g])
            o = term if o is None else o + term
        om_ref[...] = o.astype(BF16)
        for g in range(ng):
            for r, part in enumerate(_split_rows(tot, narrow_ref, dils[g])):
                lt_refs[g][r] = part

    e = _head_expand()
    outs = pl.pallas_call(
        body, name=f"mix_groups_{tag}", grid=(t // ts,),
        in_specs=[_resident(e.shape)] + [_res_spec(seq, dl, Q_W) for dl in dils] + [_res_spec(seq, dl, LANES) for dl in dils],
        out_specs=[pl.BlockSpec((ts, Q_W), lambda i: (i, 0))] + [_res_spec(seq, dl, LANES) for dl in dils],
        out_shape=[jax.ShapeDtypeStruct((t, Q_W), BF16)]
                  + [jax.ShapeDtypeStruct(_res_shape(bl, seq, dl, LANES), F32) for dl in dils],
        scratch_shapes=[_stage(ts, Q_W), _stage(ts, LANES)],
        compiler_params=_params("parallel"),
    )(e, *os, *lses)
    return outs[0], list(outs[1:])


def _sigmoid(g):
    return 1.0 / (1.0 + jnp.exp(-g))


def _ffn_fwd(x0, o, w_out, gain, wg, wu, wd, tag, exchange=(), loss_head=None):
    t, d = x0.shape
    f = wd.shape[0]
    tm = min(256, t)
    has_loss = loss_head is not None

    def body(*refs):
        x0_ref, o_ref, wo_ref, gain_ref, wg_ref, wu_ref, wd_ref = refs[:7]
        x_ref, y_ref, g_ref, u_ref, a_ref, h_ref = refs[-8:-2] if has_loss else refs[-6:]
        xv = x0_ref[...] + jnp.dot(o_ref[...], wo_ref[...], preferred_element_type=F32)
        x_ref[...] = xv
        h = (xv * _rms(xv) * gain_ref[...]).astype(BF16)
        h_ref[...] = h
        g = lax.dot_general(h, wg_ref[...], NT, preferred_element_type=F32)
        u = lax.dot_general(h, wu_ref[...], NT, preferred_element_type=F32)
        g_ref[...] = g.astype(BF16)
        u_ref[...] = u.astype(BF16)
        a = (g * _sigmoid(g) * u).astype(BF16)
        a_ref[...] = a
        y = xv + jnp.dot(a, wd_ref[...], preferred_element_type=F32)
        if not has_loss:
            y_ref[...] = y
            return
        head_ref, target_ref, loss_ref, dhead_ref = refs[7], refs[8], refs[-2], refs[-1]
        head = head_ref[...]
        yhat = y * _rms(y)
        err = yhat * head - target_ref[...]
        dout = err * (1.0 / d)
        y_ref[...] = _rms_bwd(dout, y, head)[0]
        first = pl.program_id(0) == 0
        part = 0.5 * jnp.sum(jnp.mean(err * err, axis=-1, keepdims=True), axis=0, keepdims=True)
        _accumulate(loss_ref, jnp.broadcast_to(part, loss_ref.shape), first)
        _accumulate(dhead_ref, jnp.sum(dout * yhat, axis=0, keepdims=True), first)

    row = pl.BlockSpec((tm, d), lambda i: (i, 0))
    wide = pl.BlockSpec((tm, f), lambda i: (i, 0))
    in_specs = [row, pl.BlockSpec((tm, Q_W), lambda i: (i, 0)), _resident(w_out.shape), _resident((1, d)), _resident(wg.shape),
                _resident(wu.shape), _resident(wd.shape)]
    out_specs = [row, row, wide, wide, wide, row]
    out_shape = ([jax.ShapeDtypeStruct((t, d), F32)] * 2 + [jax.ShapeDtypeStruct((t, f), BF16)] * 3
                 + [jax.ShapeDtypeStruct((t, d), BF16)])
    if has_loss:
        in_specs += [_resident((1, d)), row]
        out_specs += [pl.BlockSpec((1, LANES), lambda i: (0, 0)), pl.BlockSpec((1, d), lambda i: (0, 0))]
        out_shape += [jax.ShapeDtypeStruct((1, LANES), F32), jax.ShapeDtypeStruct((1, d), F32)]
    outs, exchanged = _hosted_call(
        body, exchange, name=f"ffn_fwd_{tag}", grid=(t // tm,), in_specs=in_specs, out_specs=out_specs, out_shape=out_shape,
        scratch_shapes=[], semantics=("arbitrary" if has_loss else "parallel",),
        args=(x0, o, w_out, gain, wg, wu, wd) + (tuple(loss_head) if has_loss else ()))
    return (*outs, exchanged)


def _ffn_bwd(dy, x, gain, g, u, wg, wu, wd, tag, exchange=()):
    t, d = x.shape
    f = wd.shape[0]
    tm = min(256, t)

    def body(dy_ref, x_ref, gain_ref, g_ref, u_ref, wg_ref, wu_ref, wd_ref, dx_ref, dg_ref, du_ref, dgain_ref):
        dyv = dy_ref[...]
        da = lax.dot_general(dyv.astype(BF16), wd_ref[...], NT, preferred_element_type=F32)
        gv, uv = g_ref[...].astype(F32), u_ref[...].astype(F32)
        sg = _sigmoid(gv)
        act = gv * sg
        du = (da * act).astype(BF16)
        dg = (da * uv * (sg * (1.0 + gv * (1.0 - sg)))).astype(BF16)
        du_ref[...] = du
        dg_ref[...] = dg
        dh = jnp.dot(dg, wg_ref[...], preferred_element_type=F32) + jnp.dot(du, wu_ref[...], preferred_element_type=F32)
        xv, gain_v = x_ref[...], gain_ref[...]
        dx, xhat = _rms_bwd(dh, xv, gain_v)
        dx_ref[...] = dyv + dx
        _accumulate(dgain_ref, jnp.sum(dh * xhat, axis=0, keepdims=True), pl.program_id(0) == 0)

    row = pl.BlockSpec((tm, d), lambda i: (i, 0))
    wide = pl.BlockSpec((tm, f), lambda i: (i, 0))
    outs, exchanged = _hosted_call(
        body, exchange, name=f"ffn_bwd_{tag}", grid=(t // tm,),
        in_specs=[row, row, _resident((1, d)), wide, wide, _resident(wg.shape), _resident(wu.shape), _resident(wd.shape)],
        out_specs=[row, wide, wide, pl.BlockSpec((1, d), lambda i: (0, 0))],
        out_shape=[jax.ShapeDtypeStruct((t, d), F32), jax.ShapeDtypeStruct((t, f), BF16), jax.ShapeDtypeStruct((t, f), BF16),
                   jax.ShapeDtypeStruct((1, d), F32)],
        scratch_shapes=[], semantics=("arbitrary",), args=(dy, x, gain, g, u, wg, wu, wd))
    return (*outs, exchanged)


def _tn_matmul(a, b, name, into=None, row_block=0, row_blocks=1):
    t, k = a.shape
    n = b.shape[1]
    tk = k // 2 if (k // 2) % LANES == 0 else k
    tt = min(2048, t)
    first = row_block * (k // tk)

    def body(a_ref, b_ref, *rest):
        o_ref, acc_ref = rest[-2:]
        prod = lax.dot_general(a_ref[...].astype(BF16), b_ref[...].astype(BF16), TN, preferred_element_type=F32)
        j = pl.program_id(1)

        @pl.when(j == 0)
        def _():
            acc_ref[...] = prod

        @pl.when(j > 0)
        def _():
            acc_ref[...] += prod

        @pl.when(j == pl.num_programs(1) - 1)
        def _():
            o_ref[...] = acc_ref[...].astype(BF16)

    return pl.pallas_call(
        body, name=name, grid=(k // tk, t // tt),
        in_specs=[pl.BlockSpec((tt, tk), lambda i, j: (j, i)), pl.BlockSpec((tt, n), lambda i, j: (j, 0))]
                 + ([ANY] if into is not None else []),
        out_specs=pl.BlockSpec((tk, n), lambda i, j: (first + i, 0)),
        out_shape=jax.ShapeDtypeStruct((row_blocks * k, n), BF16),
        scratch_shapes=[pltpu.VMEM((tk, n), F32)],
        input_output_aliases={2: 0} if into is not None else {},
        compiler_params=_params("parallel", "arbitrary"),
    )(a, b, *([into] if into is not None else []))


def _attn_out_bwd(dx, w, o, dils, seq, tag, lse=None, sink=None, exchange=()):
    t, d = dx.shape
    ts = _tile_rows(seq)
    bl = t // seq
    ng = len(dils)
    has_sink = sink is not None
    expand = _head_expand().T

    def body(*refs):
        refs = list(refs)
        dx_ref, w_ref, o_ref, e_ref = refs[:4]
        refs = refs[4:]
        lse_ref, sink_ref = (refs.pop(0), refs.pop(0)) if has_sink else (None, None)
        do_refs, dl_refs = refs[:ng], refs[ng:2 * ng]
        refs = refs[2 * ng:]
        dsink_ref = refs.pop(0) if has_sink else None
        dof_ref, dlf_ref = refs
        do = lax.dot_general(dx_ref[...].astype(BF16), w_ref[...], NT, preferred_element_type=F32)
        prod = do * o_ref[...].astype(F32)
        hi = prod.astype(BF16)
        lo = (prod - hi.astype(F32)).astype(BF16)
        e = e_ref[...]
        dl = jnp.dot(hi, e, preferred_element_type=F32) + jnp.dot(lo, e, preferred_element_type=F32)
        for g in range(ng):
            for r, part in enumerate(_split_rows(do, dof_ref, dils[g])):
                do_refs[g][r] = part.astype(BF16)
            for r, part in enumerate(_split_rows(dl, dlf_ref, dils[g])):
                dl_refs[g][r] = part
        if has_sink:
            part = -jnp.exp2(sink_ref[...] - lse_ref[...]) * dl
            _accumulate(dsink_ref, jnp.sum(part, axis=0, keepdims=True), pl.program_id(0) == 0)

    row = pl.BlockSpec((ts, d), lambda i: (i, 0))
    narrow = pl.BlockSpec((ts, LANES), lambda i: (i, 0))
    args = [dx, w, o, expand]
    in_specs = [row, _resident(w.shape), pl.BlockSpec((ts, Q_W), lambda i: (i, 0)), _resident(expand.shape)]
    if has_sink:
        args += [lse, jnp.pad(sink.reshape(1, N_HEADS) * LOG2E, ((0, 0), (0, LANES - N_HEADS)))]
        in_specs += [narrow, _resident((1, LANES))]
    out_specs = [_res_spec(seq, dl, Q_W) for dl in dils] + [_res_spec(seq, dl, LANES) for dl in dils]
    out_shape = ([jax.ShapeDtypeStruct(_res_shape(bl, seq, dl, Q_W), BF16) for dl in dils]
                 + [jax.ShapeDtypeStruct(_res_shape(bl, seq, dl, LANES), F32) for dl in dils])
    if has_sink:
        out_specs.append(pl.BlockSpec((1, LANES), lambda i: (0, 0)))
        out_shape.append(jax.ShapeDtypeStruct((1, LANES), F32))
    outs, exchanged = _hosted_call(
        body, exchange, name=f"attn_out_bwd_{tag}", grid=(t // ts,), in_specs=in_specs, out_specs=out_specs, out_shape=out_shape,
        scratch_shapes=[_stage(ts, Q_W), _stage(ts, LANES)], semantics=("arbitrary" if has_sink else "parallel",), args=args)
    return list(outs[:ng]), list(outs[ng:2 * ng]), (outs[2 * ng] if has_sink else None), exchanged


def _attn_bwd(qkv, do, lse, delta, cos, sin, w, tag, exchange=()):
    shape = qkv.shape
    dil = shape[1]
    rows_all = _seq_view(qkv)
    nseq, length, _ = rows_all.shape
    bq = min(QUERY_BLOCK, length)
    wk = _key_rows(bq, w, length)
    nb = length // bq
    per_step = _per_step(dil, length)

    def body(*refs):
        def sub(i, carry):
            one(*[ref.at[i] for ref in refs[:7]], *refs[7:])
            return carry

        if per_step == 1:
            sub(0, 0)
        else:
            lax.fori_loop(0, per_step, sub, 0)

    def one(qkv_ref, do_ref, lse_ref, dl_ref, cos_ref, sin_ref, dp_ref, kk_ref, vv_ref, dk_ref, dv_ref):
        _swap_halves(qkv_ref, Q_W, kk_ref)
        _swap_halves(qkv_ref, Q_W + KV_W, vv_ref)
        dk_ref[...] = jnp.zeros_like(dk_ref)
        dv_ref[...] = jnp.zeros_like(dv_ref)
        band = _band(bq, wk)
        lo_q = lax.broadcasted_iota(jnp.int32, (bq, LANES), 1) < HEAD_DIM
        hi_q = jnp.logical_not(lo_q)

        def block(i, carry):
            q0, k0 = _window(i, bq, w, wk, length)
            valid = jnp.abs(band + (q0 - k0)) <= w
            rows, krows = pl.ds(q0, bq), pl.ds(k0, wk)
            c, sn = cos_ref[rows, :], -sin_ref[rows, :]
            lse_t, dl_t = lse_ref[rows, :], dl_ref[rows, :]
            for kv in range(N_KV):
                heads = [(kv * GRP + h, h % 2) for h in range(GRP)]
                cols = [slice((kv * 2 + j) * LANES, (kv * 2 + j + 1) * LANES) for j in range(GRP // 2)]
                qp = [qkv_ref[rows, cs] for cs in cols]
                dop = [do_ref[rows, cs] for cs in cols]
                k2 = _pair_operand(qkv_ref, kk_ref, Q_W, kv, krows)
                v2 = _pair_operand(qkv_ref, vv_ref, Q_W + KV_W, kv, krows)
                sc2 = [lax.dot_general(q_, k2, NT, preferred_element_type=F32) for q_ in qp]
                dp2 = [lax.dot_general(d_, v2, NT, preferred_element_type=F32) for d_ in dop]
                sc = [s_[:, half * wk:(half + 1) * wk] for s_ in sc2 for half in range(2)]
                dp = [d_[:, half * wk:(half + 1) * wk] for d_ in dp2 for half in range(2)]
                p = [jnp.exp2(jnp.where(valid, s_, NEG_INF) - _over_keys(lse_t[:, hd:hd + 1], wk))
                     for s_, (hd, _) in zip(sc, heads)]
                ds = [(p_ * (dp_ - _over_keys(dl_t[:, hd:hd + 1], wk))).astype(BF16) for p_, dp_, (hd, _) in zip(p, dp, heads)]
                pb = [p_.astype(BF16) for p_ in p]
                for j in range(GRP // 2):
                    dq = jnp.dot(jnp.concatenate([ds[2 * j], ds[2 * j + 1]], axis=1), k2, preferred_element_type=F32) * SCALE
                    dp_ref[rows, cols[j]] = _rope(dq, c, sn).astype(BF16)
                zero = jnp.zeros((bq, LANES), BF16)
                q4 = jnp.concatenate([jnp.where(lo_q if h % 2 == 0 else hi_q, qp[h // 2], zero) for h in range(GRP)], axis=0)
                do4 = jnp.concatenate([jnp.where(lo_q if h % 2 == 0 else hi_q, dop[h // 2], zero) for h in range(GRP)], axis=0)
                dk_ref[kv, krows, :] += lax.dot_general(jnp.concatenate(ds, axis=0), q4, TN, preferred_element_type=F32)
                dv_ref[kv, krows, :] += lax.dot_general(jnp.concatenate(pb, axis=0), do4, TN, preferred_element_type=F32)
            return carry

        lax.fori_loop(0, nb, block, 0)
        lo = lax.broadcasted_iota(jnp.int32, (length, LANES), 1) < HEAD_DIM
        c, sn = cos_ref[...], -sin_ref[...]
        for ch in range(KV_W // LANES):
            halves = []
            for acc_ref in (dk_ref, dv_ref):
                even, odd = acc_ref[2 * ch], acc_ref[2 * ch + 1]
                even = even + pltpu.roll(even, HEAD_DIM, 1)
                odd = odd + pltpu.roll(odd, HEAD_DIM, 1)
                halves.append(jnp.where(lo, even, odd))
            dp_ref[:, Q_W + ch * LANES:Q_W + (ch + 1) * LANES] = _rope(halves[0] * LN2, c, sn).astype(BF16)
            dp_ref[:, Q_W + KV_W + ch * LANES:Q_W + KV_W + (ch + 1) * LANES] = halves[1].astype(BF16)

    def seq_block(c):
        return pl.BlockSpec((per_step, length, c), lambda i: (i, 0, 0))

    table = pl.BlockSpec((per_step, length, LANES), lambda i: (i % (dil // per_step), 0, 0))
    (out,), exchanged = _hosted_call(
        body, exchange, name=f"attn_bwd_{tag}", grid=(nseq // per_step,),
        in_specs=[seq_block(QKV_W), seq_block(Q_W), seq_block(LANES), seq_block(LANES), table, table],
        out_specs=[seq_block(QKV_W)],
        out_shape=[jax.ShapeDtypeStruct((nseq, length, QKV_W), BF16)],
        scratch_shapes=[pltpu.VMEM((KV_W // LANES, length, LANES), BF16), pltpu.VMEM((KV_W // LANES, length, LANES), BF16),
                        pltpu.VMEM((N_KV, length, LANES), F32), pltpu.VMEM((N_KV, length, LANES), F32)],
        semantics=("parallel",), args=(rows_all, _seq_view(do), _seq_view(lse), _seq_view(delta), cos, sin))
    return out.reshape(shape), exchanged


def _qkv_bwd(dy, x, gain, w, dps, dils, seq, tag):
    t, d = x.shape
    ts = _tile_rows(seq)
    ng = len(dps)

    def body(dy_ref, x_ref, gain_ref, w_ref, *refs):
        dp_refs, (dx_ref, dgain_ref, stage_ref) = refs[:ng], refs[ng:]
        dh = None
        for gi in range(ng):
            dil = dils[gi]
            n = ts // dil
            dp = dp_refs[gi][0] if dil == 1 else jnp.concatenate([dp_refs[gi][r] for r in range(dil)], axis=0)
            part = jnp.dot(dp, w_ref[gi * QKV_W:(gi + 1) * QKV_W, :], preferred_element_type=F32)
            part = _merge_rows([part[r * n:(r + 1) * n] for r in range(dil)], stage_ref, dil)
            dh = part if dh is None else dh + part
        xv, gain_v = x_ref[...], gain_ref[...]
        dx, xhat = _rms_bwd(dh, xv, gain_v)
        dx_ref[...] = dy_ref[...] + dx
        _accumulate(dgain_ref, jnp.sum(dh * xhat, axis=0, keepdims=True), pl.program_id(0) == 0)

    row = pl.BlockSpec((ts, d), lambda i: (i, 0))
    return pl.pallas_call(
        body, name=f"qkv_bwd_{tag}", grid=(t // ts,),
        in_specs=[row, row, _resident((1, d)), _resident(w.shape)] + [_res_spec(seq, dl, QKV_W) for dl in dils],
        out_specs=[row, pl.BlockSpec((1, d), lambda i: (0, 0))],
        out_shape=[jax.ShapeDtypeStruct((t, d), F32), jax.ShapeDtypeStruct((1, d), F32)],
        scratch_shapes=[_stage(ts, d)], compiler_params=_params("arbitrary"),
    )(dy, x, gain, w, *dps)


ANY = pl.BlockSpec(memory_space=pl.ANY)


def _place():
    x, y, c = lax.axis_index("x"), lax.axis_index("y"), lax.axis_index("c")
    return x, y, c


def _exchange_steps(srcs, dsts, gather, send_sems, recv_sems, local_sems):
    x, y, c = _place()
    me, sibling = (x, y, c), (x, y, 1 - c)
    chips = [(1 - x, y), (x, 1 - y), (1 - x, 1 - y)]
    mine = 4 * x + 2 * y + c

    def slot(a, device):
        px, py, pc = device
        return dsts[a].at[4 * px + 2 * py + pc]

    def passes(a, k, block, to, src=None):
        rows = slot(a, block)
        return pltpu.make_async_remote_copy(src_ref=rows if src is None else src, dst_ref=rows, send_sem=send_sems.at[a, k],
                                            recv_sem=recv_sems.at[a, k], device_id=to, device_id_type=MESH)

    def scatters(a, k):
        peer = mine ^ k
        return pltpu.make_async_remote_copy(
            src_ref=srcs[a].at[peer], dst_ref=dsts[a].at[mine], send_sem=send_sems.at[a, k - 1], recv_sem=recv_sems.at[a, k - 1],
            device_id=(peer // 4, (peer // 2) % 2, peer % 2), device_id_type=MESH)

    def local(a):
        return pltpu.make_async_copy(srcs[a] if gather[a] else srcs[a].at[mine], dsts[a].at[mine], local_sems.at[a])

    def first_copies(a):
        if not gather[a]:
            return [scatters(a, k) for k in range(1, N_DEV)]
        return [passes(a, 0, me, sibling, src=srcs[a])] + [passes(a, 1 + j, me, (*chip, c), src=srcs[a]) for j, chip in enumerate(chips)]

    def start():
        for a in range(len(srcs)):
            local(a).start()
            for cp in first_copies(a):
                cp.start()

    def forward():
        for a in range(len(srcs)):
            if gather[a]:
                for j, chip in enumerate(chips):
                    passes(a, 1 + j, (*chip, c), me).wait_recv()
                    passes(a, 4 + j, (*chip, c), sibling).start()

    def finish():
        for a in range(len(srcs)):
            if gather[a]:
                passes(a, 0, sibling, me).wait_recv()
                for j, chip in enumerate(chips):
                    passes(a, 4 + j, (*chip, 1 - c), me).wait_recv()
                    passes(a, 4 + j, (*chip, c), sibling).wait_send()
                for cp in first_copies(a):
                    cp.wait_send()
            else:
                for cp in first_copies(a):
                    cp.wait()
            local(a).wait()

    return start, forward, finish


def _exchange_scratch(n):
    return [pltpu.SemaphoreType.DMA((n, N_DEV - 1)), pltpu.SemaphoreType.DMA((n, N_DEV - 1)), pltpu.SemaphoreType.DMA((n,))]


def _exchanged_shapes(exchange):
    return [jax.ShapeDtypeStruct(((N_DEV,) + a.shape) if g else a.shape, a.dtype) for a, g in exchange]


def _hosted_call(body, exchange, *, name, grid, in_specs, out_specs, out_shape, scratch_shapes, semantics, args):
    out_specs, out_shape, scratch = list(out_specs), list(out_shape), list(scratch_shapes)
    if not exchange:
        outs = pl.pallas_call(body, name=name, grid=grid, in_specs=in_specs, out_specs=out_specs, out_shape=out_shape,
                              scratch_shapes=scratch, compiler_params=_params(*semantics))(*args)
        return list(outs), []
    n, n_in, n_out, n_scr = len(exchange), len(in_specs), len(out_specs), len(scratch)
    gather = [g for _, g in exchange]
    steps = math.prod(grid)

    def hosted(*refs):
        own_in, x_in = refs[:n_in], refs[n_in:n_in + n]
        own_out, x_out = refs[n_in + n:n_in + n + n_out], refs[n_in + n + n_out:n_in + 2 * n + n_out]
        own_scr, sems = refs[n_in + 2 * n + n_out:n_in + 2 * n + n_out + n_scr], refs[-3:]
        step = pl.program_id(0)
        for axis in range(1, len(grid)):
            step = step * grid[axis] + pl.program_id(axis)
        start, forward, finish = _exchange_steps(x_in, x_out, gather, *sems)
        pl.when(step == 0)(start)
        body(*own_in, *own_out, *own_scr)
        pl.when(step == steps // 2)(forward)
        pl.when(step == steps - 1)(finish)

    outs = pl.pallas_call(
        hosted, name=name, grid=grid, in_specs=list(in_specs) + [ANY] * n, out_specs=out_specs + [ANY] * n,
        out_shape=out_shape + _exchanged_shapes(exchange), scratch_shapes=scratch + _exchange_scratch(n),
        compiler_params=_params(*["arbitrary"] * len(grid)),
    )(*args, *[a for a, _ in exchange])
    return list(outs[:n_out]), list(outs[n_out:])


def _exchange_now(exchange, name):
    n = len(exchange)
    gather = [g for _, g in exchange]

    def body(*refs):
        for step in _exchange_steps(refs[:n], refs[n:2 * n], gather, *refs[2 * n:]):
            step()

    return pl.pallas_call(
        body, name=name, in_specs=[ANY] * n, out_specs=[ANY] * n, out_shape=_exchanged_shapes(exchange),
        scratch_shapes=_exchange_scratch(n),
    )(*[a for a, _ in exchange])


def _all_reduce_small(v):
    def body(v_ref, o_ref, recv_ref, send_sems, recv_sems):
        x, y, c = _place()
        me = 4 * x + 2 * y + c
        copies = []
        for k in range(1, N_DEV):
            peer = me ^ k
            copies.append(pltpu.make_async_remote_copy(
                src_ref=v_ref, dst_ref=recv_ref.at[k], send_sem=send_sems.at[k - 1], recv_sem=recv_sems.at[k - 1],
                device_id=(peer // 4, (peer // 2) % 2, peer % 2), device_id_type=MESH))
        for cp in copies:
            cp.start()
        recv_ref[0] = v_ref[...]
        for cp in copies:
            cp.wait()
        acc = recv_ref[me]
        for src in range(1, N_DEV):
            acc = acc + recv_ref[me ^ src]
        o_ref[...] = acc

    vm = pl.BlockSpec(memory_space=pltpu.VMEM)
    return pl.pallas_call(
        body, name="all_reduce_small", in_specs=[vm], out_specs=vm, out_shape=jax.ShapeDtypeStruct(v.shape, F32),
        scratch_shapes=[pltpu.VMEM((N_DEV,) + v.shape, F32), pltpu.SemaphoreType.DMA((N_DEV - 1,)),
                        pltpu.SemaphoreType.DMA((N_DEV - 1,))],
    )(v)


def _adamw_math(w, g, m, v):
    m = ADAM_B1 * m + (1.0 - ADAM_B1) * g
    v = ADAM_B2 * v + (1.0 - ADAM_B2) * (g * g)
    m_hat = m / (1.0 - ADAM_B1 ** ADAM_STEP)
    v_hat = v / (1.0 - ADAM_B2 ** ADAM_STEP)
    delta = -ADAM_LR * (m_hat / (jnp.sqrt(v_hat) + ADAM_EPS) + ADAM_WD * w)
    return delta, m, v


def _adamw(parts, w, m, v, name, layer=None, into=None):
    r, c = w.shape[-2:]
    tr = r // 2 if r % 16 == 0 and r >= 256 else r
    n = len(parts)

    def body(*refs):
        w_ref, m_ref, v_ref = refs[n:n + 3]
        g_ref, d_ref, nm_ref, nv_ref = refs[-4:]
        g = refs[0][...].astype(F32)
        for p_ref in refs[1:n]:
            g = g + p_ref[...].astype(F32)
        g_ref[...] = g
        d_ref[...], nm_ref[...], nv_ref[...] = _adamw_math(w_ref[...], g, m_ref[...], v_ref[...])

    def slab(slot):
        return pl.BlockSpec((None, tr, c), lambda i: (slot, i, 0))

    tile = pl.BlockSpec((tr, c), lambda i: (i, 0)) if layer is None else slab(layer)
    arrays, in_specs = [], []
    for p in parts:
        if isinstance(p, tuple):
            arrays.append(p[0])
            in_specs.append(slab(p[1]))
        else:
            arrays.append(p)
            in_specs.append(tile)
    kept = list(into) if into is not None else []
    return pl.pallas_call(
        body, name=name, grid=(r // tr,), in_specs=in_specs + [tile] * 3 + [ANY] * len(kept), out_specs=[tile] * 4,
        out_shape=[jax.ShapeDtypeStruct(w.shape, F32)] * 4,
        input_output_aliases={n + 3 + k: k for k in range(len(kept))}, compiler_params=_params("parallel"),
    )(*arrays, w, m, v, *kept)


def _rows(g):
    return g.reshape(-1, g.shape[-1])


def _row_blocks(dw):
    k, n = dw.shape
    return dw.reshape(N_DEV, k // N_DEV, n)


def _pack_rows(rows, width):
    out = None
    for i, r in enumerate(rows):
        r = r.reshape(1, -1).astype(F32)
        r = jnp.pad(r, ((i, 8 - 1 - i), (0, width - r.shape[1])))
        out = r if out is None else out + r
    return out


def _mixer_fwd(x, gain, w_in, cos, sin, seq, groups, tag, sink=None, exchanges=None):
    exchanges = exchanges or {}
    os, lses, got = [], [], {}
    qkvs, hs, got["proj"] = _qkv_proj(x, gain, w_in, cos, sin, seq, [dil for dil, _ in groups], tag,
                                      exchange=exchanges.get("proj", ()))
    for gi, (dil, w) in enumerate(groups):
        o, lse, got[gi] = _attn_fwd(qkvs[gi], w, f"{tag}{gi}", sink=sink, exchange=exchanges.get(gi, ()))
        os.append(o)
        lses.append(lse)
    o, lses = _mix_groups(os, lses, [dl for dl, _ in groups], seq, tag)
    return (qkvs, hs, o, lses), got


def _mixer_bwd(dy, x_in, gain, w_in, w_out, saved, cos, sin, seq, groups, tag, sink=None, exchanges=None, scatter_dw_out=False):
    qkvs, hs, o, lses = saved
    t, d = x_in.shape
    dils = [dl for dl, _ in groups]
    lse_tokens = lses[0].reshape(t, LANES) if sink is not None else None
    dw_out = _tn_matmul(o, dy, f"dw_out_{tag}")
    dos, dls, dsink, early = _attn_out_bwd(dy, w_out, o, dils, seq, tag, lse=lse_tokens, sink=sink,
                                           exchange=_to_send([dw_out]) if scatter_dw_out else ())
    if scatter_dw_out:
        (dw_out,) = early
    exchanges = exchanges or {}
    dps, got = [], {}
    for gi, (dil, w) in enumerate(groups):
        dp, got[gi] = _attn_bwd(qkvs[gi], dos[gi], lses[gi], dls[gi], _tables_by_residue(cos, seq, dil),
                                _tables_by_residue(sin, seq, dil), w, f"{tag}{gi}", exchange=exchanges.get(gi, ()))
        dps.append(dp)
    dx, dgain = _qkv_bwd(dy, x_in, gain, w_in, dps, dils, seq, tag)
    dw_in = None
    for gi in range(len(groups)):
        dw_in = _tn_matmul(dps[gi].reshape(t, QKV_W), hs[gi].reshape(t, d), f"dw_in_{tag}{gi}", into=dw_in, row_block=gi,
                           row_blocks=len(groups))
    return dx, dw_in, dw_out, dgain, dsink, got


def _ffn_layer_bwd(dy, x_in, gain, saved, wg, wu, wd, tag, exchange=()):
    g, u, act, h = saved
    dx, dg, du, dgain, got = _ffn_bwd(dy, x_in, gain, g, u, wg, wu, wd, tag, exchange=exchange)
    dwd = _tn_matmul(act, dy, f"dw_down_{tag}")
    dwg = _tn_matmul(dg, h, f"dw_gate_{tag}")
    dwu = _tn_matmul(du, h, f"dw_up_{tag}")
    return dx, dwg, dwu, dwd, dgain, got


def _to_send(dws):
    return [(_row_blocks(g), False) for g in dws]


def kernel(x, a_w_in, a_sink, a_w_out, b_w_in, b_w_out, norm_mix, norm_ffn, w_gate, w_up, w_down, final_norm, loss_target, m_a_w_in, m_a_sink, m_a_w_out, m_b_w_in, m_b_w_out, m_norm_mix, m_norm_ffn, m_w_gate, m_w_up, m_w_down, m_final_norm, v_a_w_in, v_a_sink, v_a_w_out, v_b_w_in, v_b_w_out, v_norm_mix, v_norm_ffn, v_w_gate, v_w_up, v_w_down, v_final_norm):
    bl, seq, d = x.shape
    t = bl * seq
    xf = x.reshape(t, d)
    target = loss_target.reshape(t, d)
    cos, sin = _rope_tables(seq)
    groups_a = [(1, ATTN_HALF_WINDOW)]
    groups_b = [(dil, window // 2 // dil) for window, dil in DILATED_GROUPS]

    def flip(w_):
        return jnp.swapaxes(w_, -1, -2)

    a_w_in, m_a_w_in, v_a_w_in, b_w_in, m_b_w_in, v_b_w_in = map(flip, (a_w_in, m_a_w_in, v_a_w_in, b_w_in, m_b_w_in, v_b_w_in))
    w_gate, m_w_gate, v_w_gate, w_up, m_w_up, v_w_up = map(flip, (w_gate, m_w_gate, v_w_gate, w_up, m_w_up, v_w_up))

    def shard(w_, layer):
        return (w_[layer].astype(BF16), True)

    (wa_in,) = map(_rows, _exchange_now([shard(a_w_in, 0)], "gather_first"))

    saved_a, got = _mixer_fwd(xf, norm_mix[0:1], wa_in, cos, sin, seq, groups_a, "a", sink=a_sink[0],
                              exchanges={"proj": [shard(w_down, 0), shard(a_w_out, 0)], 0: [shard(w_gate, 0), shard(w_up, 0)]})
    wg0, wu0, wd0, wa_out = map(_rows, got[0] + got["proj"])
    x1_0, x2_0, *saved_0, got = _ffn_fwd(xf, saved_a[2], wa_out, norm_ffn[0:1], wg0, wu0, wd0, "0",
                                         exchange=[shard(b_w_in, 0), shard(b_w_out, 0)])
    wb_in, wb_out = map(_rows, got)
    saved_b, got = _mixer_fwd(x2_0, norm_mix[1:2], wb_in, cos, sin, seq, groups_b, "b",
                              exchanges={0: [shard(w_gate, 1)], 1: [shard(w_up, 1)], 2: [shard(w_down, 1)]})
    wg1, wu1, wd1 = map(_rows, got[0] + got[1] + got[2])
    x1_1, dy, *saved_1, loss_part, d_final, _ = _ffn_fwd(x2_0, saved_b[2], wb_out, norm_ffn[1:2], wg1, wu1, wd1, "1",
                                                         loss_head=(final_norm.reshape(1, d), target))

    dy, dwg1, dwu1, dwd1, d_nf1, _ = _ffn_layer_bwd(dy, x1_1, norm_ffn[1:2], saved_1, wg1, wu1, wd1, "1")
    dy, dwb_in, dwb_out, d_nm1, _, got = _mixer_bwd(
        dy, x2_0, norm_mix[1:2], wb_in, wb_out, saved_b, cos, sin, seq, groups_b, "b",
        exchanges={0: _to_send([dwg1, dwd1]), 1: _to_send([dwu1])})
    (r_g1, r_d1), (r_u1,) = got[0], got[1]
    dy, dwg0, dwu0, dwd0, d_nf0, (r_b_in, r_b_out) = _ffn_layer_bwd(
        dy, x1_0, norm_ffn[0:1], saved_0, wg0, wu0, wd0, "0", exchange=_to_send([dwb_in, dwb_out]))
    dy, dwa_in, r_a_out, d_nm0, d_sink, got = _mixer_bwd(
        dy, xf, norm_mix[0:1], wa_in, wa_out, saved_a, cos, sin, seq, groups_a, "a", sink=a_sink[0],
        exchanges={0: _to_send([dwg0, dwu0, dwd0])}, scatter_dw_out=True)
    r_g0, r_u0, r_d0 = got[0]
    (r_a_in,) = _exchange_now(_to_send([dwa_in]), "scatter_last")
    grad_x = dy.reshape(bl, seq, d)

    def update(received, w_, m_, v_, name):
        out = None
        for layer in reversed(range(len(received))):
            out = _adamw([(received[layer], src) for src in range(N_DEV)], w_, m_, v_, f"adamw_{name}{layer}", layer=layer, into=out)
        return out

    u_a_in = update([r_a_in], a_w_in, m_a_w_in, v_a_w_in, "a_in")
    u_a_out = update([r_a_out], a_w_out, m_a_w_out, v_a_w_out, "a_out")
    u_b_in = update([r_b_in], b_w_in, m_b_w_in, v_b_w_in, "b_in")
    u_b_out = update([r_b_out], b_w_out, m_b_w_out, v_b_w_out, "b_out")
    u_gate = update([r_g0, r_g1], w_gate, m_w_gate, v_w_gate, "gate")
    u_up = update([r_u0, r_u1], w_up, m_w_up, v_w_up, "up")
    u_down = update([r_d0, r_d1], w_down, m_w_down, v_w_down, "down")

    small = _pack_rows([d_nm0, d_nm1, d_nf0, d_nf1, d_final, d_sink, loss_part], d)
    total = _all_reduce_small(small)
    small_w = _pack_rows([norm_mix[0], norm_mix[1], norm_ffn[0], norm_ffn[1], final_norm, a_sink], d)
    small_m = _pack_rows([m_norm_mix[0], m_norm_mix[1], m_norm_ffn[0], m_norm_ffn[1], m_final_norm, m_a_sink], d)
    small_v = _pack_rows([v_norm_mix[0], v_norm_mix[1], v_norm_ffn[0], v_norm_ffn[1], v_final_norm, v_a_sink], d)
    u_small = _adamw([total], small_w, small_m, small_v, "adamw_small")
    loss = total[6, 0]

    outs = []
    for k in range(4):
        sm = u_small[k]
        outs += [flip(u_a_in[k]), sm[5:6, :N_HEADS], u_a_out[k], flip(u_b_in[k]), u_b_out[k], sm[0:2], sm[2:4],
                 flip(u_gate[k]), flip(u_up[k]), u_down[k], sm[4]]
    return (loss, grad_x, *outs)
```

```python
import functools
import math

import jax
import jax.numpy as jnp
from jax import lax
from jax.experimental import pallas as pl
from jax.experimental.pallas import tpu as pltpu

F32 = jnp.float32
BF16 = jnp.bfloat16

HEAD_DIM = 64
N_HEADS = 16
N_KV = 4
GRP = N_HEADS // N_KV
Q_W = N_HEADS * HEAD_DIM
KV_W = N_KV * HEAD_DIM
QKV_W = Q_W + 2 * KV_W
ATTN_HALF_WINDOW = 128
DILATED_GROUPS = ((128, 1), (512, 4), (2048, 16))
ROPE_THETA = 10000.0
RMS_EPS = 1e-6
NEG_INF = -1e30
SCALE = 1.0 / math.sqrt(HEAD_DIM)
LOG2E = 1.0 / math.log(2.0)
LN2 = math.log(2.0)

ADAM_LR = 0.001
ADAM_B1 = 0.9
ADAM_B2 = 0.999
ADAM_EPS = 1e-08
ADAM_WD = 0.01
ADAM_STEP = 10

LANES = 128
VMEM_LIMIT = 56 * 1024 * 1024
QUERY_BLOCK = 128
N_DEV = 8
MESH = pl.DeviceIdType.MESH

NT = (((1,), (1,)), ((), ()))
TN = (((0,), (0,)), ((), ()))


def _params(*sem):
    return pltpu.CompilerParams(dimension_semantics=tuple(sem) if sem else None, vmem_limit_bytes=VMEM_LIMIT)


def _resident(shape):
    return pl.BlockSpec(shape, lambda *_: (0,) * len(shape), pipeline_mode=pl.Buffered(1))


def _rope_tables(seq):
    inv_freq = 1.0 / (ROPE_THETA ** (jnp.arange(0, HEAD_DIM, 2, dtype=F32) / HEAD_DIM))
    ang = jnp.arange(seq, dtype=F32)[:, None] * inv_freq[None, :]
    cos, sin = jnp.cos(ang), jnp.sin(ang)
    return jnp.tile(cos, (1, 4)), jnp.concatenate([-sin, sin, -sin, sin], axis=1)


def _rope(t, cos, sin_signed):
    lane = lax.broadcasted_iota(jnp.int32, t.shape, 1)
    first = (lane & (HEAD_DIM // 2)) == 0
    swapped = jnp.where(first, pltpu.roll(t, LANES - HEAD_DIM // 2, 1), pltpu.roll(t, HEAD_DIM // 2, 1))
    return t * cos + swapped * sin_signed


def _rms(x):
    return lax.rsqrt(jnp.mean(x * x, axis=-1, keepdims=True) + RMS_EPS)


def _rms_bwd(dh, x, gain):
    r = _rms(x)
    xhat = x * r
    dxh = dh * gain
    dx = r * (dxh - xhat * jnp.mean(dxh * xhat, axis=-1, keepdims=True))
    return dx, xhat


def _accumulate(ref, value, first):
    @pl.when(first)
    def _():
        ref[...] = jnp.zeros_like(ref)

    ref[...] += value


def _tile_rows(seq):
    return min(512, seq)


def _res_shape(bl, seq, dil, c):
    ts = _tile_rows(seq)
    return (bl, dil, seq // ts, ts // dil, c)


def _res_spec(seq, dil, c):
    ts = _tile_rows(seq)
    per_seq = seq // ts
    return pl.BlockSpec((None, dil, None, ts // dil, c), lambda i: (i // per_seq, 0, i % per_seq, 0, 0))


def _seq_view(a):
    bl, dil, tiles, n, c = a.shape
    return a.reshape(bl * dil, tiles * n, c)


def _stage(ts, c):
    return pltpu.VMEM((c // LANES, ts, LANES), F32)


def _split_rows(val, stage_ref, dil):
    if dil == 1:
        return [val]
    ts, c = val.shape
    n, nc = ts // dil, c // LANES
    for k in range(nc):
        stage_ref[k] = val[:, k * LANES:(k + 1) * LANES]
    return [jnp.concatenate([stage_ref[k, pl.ds(r, n, stride=dil), :] for k in range(nc)], axis=1) for r in range(dil)]


def _merge_rows(parts, stage_ref, dil):
    if dil == 1:
        return parts[0]
    n, c = parts[0].shape
    nc = c // LANES
    for r, part in enumerate(parts):
        for k in range(nc):
            stage_ref[k, pl.ds(r, n, stride=dil), :] = part[:, k * LANES:(k + 1) * LANES]
    return jnp.concatenate([stage_ref[k] for k in range(nc)], axis=1)


def _tables_tiled(table, seq, dil):
    ts = _tile_rows(seq)
    return table.reshape(seq // ts, ts // dil, dil, LANES).transpose(0, 2, 1, 3).reshape(seq, LANES)


def _tables_by_residue(table, seq, dil):
    return table.reshape(seq // dil, dil, LANES).transpose(1, 0, 2)


def _qkv_proj(x, gain, w, cos, sin, seq, dils, tag, exchange=()):
    t, d = x.shape
    ts = _tile_rows(seq)
    per_seq = seq // ts
    ng = len(dils)
    tables = [t_ for dil in dils for t_ in (_tables_tiled(cos, seq, dil), _tables_tiled(sin, seq, dil))]

    def body(x_ref, g_ref, w_ref, *refs):
        table_refs, o_refs, h_refs, stage_ref = refs[:2 * ng], refs[2 * ng:3 * ng], refs[3 * ng:4 * ng], refs[4 * ng]
        xv = x_ref[...]
        h_tokens = xv * _rms(xv) * g_ref[...]
        for gi, dil in enumerate(dils):
            n = ts // dil
            h = jnp.concatenate(_split_rows(h_tokens, stage_ref, dil), axis=0).astype(BF16)
            for r in range(dil):
                h_refs[gi][r] = h[r * n:(r + 1) * n]
            acc = lax.dot_general(h, w_ref[gi * QKV_W:(gi + 1) * QKV_W, :], NT, preferred_element_type=F32)
            c, s = table_refs[2 * gi][...], table_refs[2 * gi + 1][...]
            for j in range(QKV_W // LANES):
                cols = slice(j * LANES, (j + 1) * LANES)
                val = acc[:, cols]
                if j < (Q_W + KV_W) // LANES:
                    val = _rope(val, c, s)
                if j < Q_W // LANES:
                    val = val * (SCALE * LOG2E)
                val = val.astype(BF16)
                for r in range(dil):
                    o_refs[gi][r, :, cols] = val[r * n:(r + 1) * n]

    table = pl.BlockSpec((ts, LANES), lambda i: (i % per_seq, 0))
    outs, exchanged = _hosted_call(
        body, exchange, name=f"qkv_proj_{tag}", grid=(t // ts,),
        in_specs=[pl.BlockSpec((ts, d), lambda i: (i, 0)), _resident((1, d)), _resident(w.shape)] + [table] * (2 * ng),
        out_specs=[_res_spec(seq, dil, QKV_W) for dil in dils] + [_res_spec(seq, dil, d) for dil in dils],
        out_shape=[jax.ShapeDtypeStruct(_res_shape(t // seq, seq, dil, QKV_W), BF16) for dil in dils]
                  + [jax.ShapeDtypeStruct(_res_shape(t // seq, seq, dil, d), BF16) for dil in dils],
        scratch_shapes=[_stage(ts, d)], semantics=("parallel",), args=(x, gain, w, *tables))
    return outs[:ng], outs[ng:], exchanged


def _band(bq, wk):
    return lax.broadcasted_iota(jnp.int32, (bq, wk), 0) - lax.broadcasted_iota(jnp.int32, (bq, wk), 1)


def _swap_halves(src_ref, base, dst_ref):
    for c in range(KV_W // LANES):
        dst_ref[c] = pltpu.roll(src_ref[:, base + c * LANES:base + (c + 1) * LANES], HEAD_DIM, 1)


def _pair_operand(src_ref, swapped_ref, base, kv, rows):
    c = kv // 2
    chunk, swapped = src_ref[rows, base + c * LANES:base + (c + 1) * LANES], swapped_ref[c, rows, :]
    lo = lax.broadcasted_iota(jnp.int32, chunk.shape, 1) < HEAD_DIM
    zero = jnp.zeros_like(chunk)
    if kv % 2 == 0:
        return jnp.concatenate([jnp.where(lo, chunk, zero), jnp.where(lo, zero, swapped)], axis=0)
    return jnp.concatenate([jnp.where(lo, swapped, zero), jnp.where(lo, zero, chunk)], axis=0)


def _over_keys(col, wk):
    if wk % LANES:
        return jnp.broadcast_to(col, (col.shape[0], wk))
    wide = jnp.broadcast_to(col, (col.shape[0], LANES))
    return wide if wk == LANES else jnp.concatenate([wide] * (wk // LANES), axis=1)


def _per_step(dil, length):
    return max(1, min(dil, 512 // length))


def _key_rows(bq, w, length):
    return min(bq + 2 * w, length)


def _window(i, bq, w, wk, length):
    q0 = pl.multiple_of(i * bq, bq)
    k0 = pl.multiple_of(jnp.clip(q0 - w, 0, length - wk), min(w, bq))
    return q0, k0


def _attn_fwd(qkv, w, tag, sink=None, exchange=()):
    shape = qkv.shape
    rows_all = _seq_view(qkv)
    nseq, length, _ = rows_all.shape
    bq = min(QUERY_BLOCK, length)
    wk = _key_rows(bq, w, length)
    nb = length // bq
    has_sink = sink is not None
    per_step = _per_step(shape[1], length)

    def body(*refs):
        sink_ref = refs[1] if has_sink else None
        kk_ref, vv_ref = refs[-2:]
        for sub in range(per_step):
            one(refs[0].at[sub], refs[-4].at[sub], refs[-3].at[sub], sink_ref, kk_ref, vv_ref)

    def one(qkv_ref, o_ref, lse_ref, sink_ref, kk_ref, vv_ref):
        _swap_halves(qkv_ref, Q_W, kk_ref)
        _swap_halves(qkv_ref, Q_W + KV_W, vv_ref)
        band = _band(bq, wk)
        lane = lax.broadcasted_iota(jnp.int32, (bq, LANES), 1)
        lo = lane < HEAD_DIM

        def block(i, carry):
            q0, k0 = _window(i, bq, w, wk, length)
            valid = jnp.abs(band + (q0 - k0)) <= w
            rows, krows = pl.ds(q0, bq), pl.ds(k0, wk)
            lse_tile = jnp.zeros((bq, LANES), F32)
            for kv in range(N_KV):
                heads = [(kv * GRP + h, h % 2) for h in range(GRP)]
                qp = [qkv_ref[rows, (kv * 2 + j) * LANES:(kv * 2 + j + 1) * LANES] for j in range(GRP // 2)]
                k2 = _pair_operand(qkv_ref, kk_ref, Q_W, kv, krows)
                v2 = _pair_operand(qkv_ref, vv_ref, Q_W + KV_W, kv, krows)
                sc2 = [lax.dot_general(q_, k2, NT, preferred_element_type=F32) for q_ in qp]
                sc = [jnp.where(valid, s_[:, half * wk:(half + 1) * wk], NEG_INF) for s_ in sc2 for half in range(2)]
                m = [jnp.max(s_, axis=-1, keepdims=True) for s_ in sc]
                if has_sink:
                    m = [jnp.maximum(m_, sink_ref[hd]) for m_, (hd, _) in zip(m, heads)]
                mb = [jnp.broadcast_to(m_, (bq, LANES)) for m_ in m]
                p = [jnp.exp2(s_ - _over_keys(m_, wk)) for s_, m_ in zip(sc, m)]
                den = [jnp.sum(p_, axis=-1, keepdims=True) for p_ in p]
                if has_sink:
                    den = [d_ + jnp.exp2(sink_ref[hd] - m_) for d_, m_, (hd, _) in zip(den, m, heads)]
                inv = [jnp.broadcast_to(1.0 / d_, (bq, LANES)) for d_ in den]
                pb = [p_.astype(BF16) for p_ in p]
                for j in range(GRP // 2):
                    o = jnp.dot(jnp.concatenate([pb[2 * j], pb[2 * j + 1]], axis=1), v2, preferred_element_type=F32)
                    o = o * jnp.where(lo, inv[2 * j], inv[2 * j + 1])
                    o_ref[rows, (kv * 2 + j) * LANES:(kv * 2 + j + 1) * LANES] = o.astype(BF16)
                for h, (hd, _) in enumerate(heads):
                    lse_tile = jnp.where(lane == hd, mb[h] - jnp.log(inv[h]) * LOG2E, lse_tile)
            lse_ref[rows, :] = lse_tile
            return carry

        lax.fori_loop(0, nb, block, 0)

    def seq_block(c):
        return pl.BlockSpec((per_step, length, c), lambda i: (i, 0, 0))

    args = [rows_all]
    in_specs = [seq_block(QKV_W)]
    if has_sink:
        args.append(sink * LOG2E)
        in_specs.append(pl.BlockSpec(memory_space=pltpu.SMEM))
    (o, lse), exchanged = _hosted_call(
        body, exchange, name=f"attn_fwd_{tag}", grid=(nseq // per_step,), in_specs=in_specs,
        out_specs=[seq_block(Q_W), seq_block(LANES)],
        out_shape=[jax.ShapeDtypeStruct((nseq, length, Q_W), BF16), jax.ShapeDtypeStruct((nseq, length, LANES), F32)],
        scratch_shapes=[pltpu.VMEM((KV_W // LANES, length, LANES), BF16), pltpu.VMEM((KV_W // LANES, length, LANES), BF16)],
        semantics=("parallel",), args=args)
    return o.reshape(shape[:-1] + (Q_W,)), lse.reshape(shape[:-1] + (LANES,)), exchanged


def _head_expand():
    return (jnp.arange(LANES)[:, None] == jnp.arange(Q_W)[None, :] // HEAD_DIM).astype(BF16)


def _mix_groups(os, lses, dils, seq, tag):
    bl = os[0].shape[0]
    ts = _tile_rows(seq)
    t = bl * seq
    ng = len(os)
    if ng == 1 and dils[0] == 1:
        return os[0].reshape(t, Q_W), [lses[0]]

    def body(*refs):
        e_ref = refs[0]
        o_refs, l_refs = refs[1:1 + ng], refs[1 + ng:1 + 2 * ng]
        om_ref = refs[1 + 2 * ng]
        lt_refs = refs[2 + 2 * ng:2 + 3 * ng]
        wide_ref, narrow_ref = refs[2 + 3 * ng:]
        ls = [_merge_rows([l_refs[g][r] for r in range(dils[g])], narrow_ref, dils[g]) for g in range(ng)]
        mx = functools.reduce(jnp.maximum, ls)
        tot = mx + jnp.log(functools.reduce(lambda a, b: a + b, [jnp.exp2(l_ - mx) for l_ in ls])) * LOG2E
        e = e_ref[...]
        o = None
        for g in range(ng):
            wt = jnp.exp2(ls[g] - tot)
            hi = wt.astype(BF16)
            lo = (wt - hi.astype(F32)).astype(BF16)
            wide = jnp.dot(hi, e, preferred_element_type=F32) + jnp.dot(lo, e, preferred_element_type=F32)
            term = wide * _merge_rows([o_refs[g][r].astype(F32) for r in range(dils[g])], wide_ref, dils[g])
            o = term if o is None else o + term
        om_ref[...] = o.astype(BF16)
        for g in range(ng):
            for r, part in enumerate(_split_rows(tot, narrow_ref, dils[g])):
                lt_refs[g][r] = part

    e = _head_expand()
    outs = pl.pallas_call(
        body, name=f"mix_groups_{tag}", grid=(t // ts,),
        in_specs=[_resident(e.shape)] + [_res_spec(seq, dl, Q_W) for dl in dils] + [_res_spec(seq, dl, LANES) for dl in dils],
        out_specs=[pl.BlockSpec((ts, Q_W), lambda i: (i, 0))] + [_res_spec(seq, dl, LANES) for dl in dils],
        out_shape=[jax.ShapeDtypeStruct((t, Q_W), BF16)]
                  + [jax.ShapeDtypeStruct(_res_shape(bl, seq, dl, LANES), F32) for dl in dils],
        scratch_shapes=[_stage(ts, Q_W), _stage(ts, LANES)],
        compiler_params=_params("parallel"),
    )(e, *os, *lses)
    return outs[0], list(outs[1:])


def _sigmoid(g):
    return 1.0 / (1.0 + jnp.exp(-g))


def _ffn_fwd(x0, o, w_out, gain, wg, wu, wd, tag, exchange=(), loss_head=None):
    t, d = x0.shape
    f = wd.shape[0]
    tm = min(256, t)
    has_loss = loss_head is not None

    def body(*refs):
        x0_ref, o_ref, wo_ref, gain_ref, wg_ref, wu_ref, wd_ref = refs[:7]
        x_ref, y_ref, g_ref, u_ref, a_ref, h_ref = refs[-8:-2] if has_loss else refs[-6:]
        xv = x0_ref[...] + jnp.dot(o_ref[...], wo_ref[...], preferred_element_type=F32)
        x_ref[...] = xv
        h = (xv * _rms(xv) * gain_ref[...]).astype(BF16)
        h_ref[...] = h
        g = lax.dot_general(h, wg_ref[...], NT, preferred_element_type=F32)
        u = lax.dot_general(h, wu_ref[...], NT, preferred_element_type=F32)
        g_ref[...] = g.astype(BF16)
        u_ref[...] = u.astype(BF16)
        a = (g * _sigmoid(g) * u).astype(BF16)
        a_ref[...] = a
        y = xv + jnp.dot(a, wd_ref[...], preferred_element_type=F32)
        if not has_loss:
            y_ref[...] = y
            return
        head_ref, target_ref, loss_ref, dhead_ref = refs[7], refs[8], refs[-2], refs[-1]
        head = head_ref[...]
        yhat = y * _rms(y)
        err = yhat * head - target_ref[...]
        dout = err * (1.0 / d)
        y_ref[...] = _rms_bwd(dout, y, head)[0]
        first = pl.program_id(0) == 0
        part = 0.5 * jnp.sum(jnp.mean(err * err, axis=-1, keepdims=True), axis=0, keepdims=True)
        _accumulate(loss_ref, jnp.broadcast_to(part, loss_ref.shape), first)
        _accumulate(dhead_ref, jnp.sum(dout * yhat, axis=0, keepdims=True), first)

    row = pl.BlockSpec((tm, d), lambda i: (i, 0))
    wide = pl.BlockSpec((tm, f), lambda i: (i, 0))
    in_specs = [row, pl.BlockSpec((tm, Q_W), lambda i: (i, 0)), _resident(w_out.shape), _resident((1, d)), _resident(wg.shape),
                _resident(wu.shape), _resident(wd.shape)]
    out_specs = [row, row, wide, wide, wide, row]
    out_shape = ([jax.ShapeDtypeStruct((t, d), F32)] * 2 + [jax.ShapeDtypeStruct((t, f), BF16)] * 3
                 + [jax.ShapeDtypeStruct((t, d), BF16)])
    if has_loss:
        in_specs += [_resident((1, d)), row]
        out_specs += [pl.BlockSpec((1, LANES), lambda i: (0, 0)), pl.BlockSpec((1, d), lambda i: (0, 0))]
        out_shape += [jax.ShapeDtypeStruct((1, LANES), F32), jax.ShapeDtypeStruct((1, d), F32)]
    outs, exchanged = _hosted_call(
        body, exchange, name=f"ffn_fwd_{tag}", grid=(t // tm,), in_specs=in_specs, out_specs=out_specs, out_shape=out_shape,
        scratch_shapes=[], semantics=("arbitrary" if has_loss else "parallel",),
        args=(x0, o, w_out, gain, wg, wu, wd) + (tuple(loss_head) if has_loss else ()))
    return (*outs, exchanged)


def _ffn_bwd(dy, x, gain, g, u, wg, wu, wd, tag, exchange=()):
    t, d = x.shape
    f = wd.shape[0]
    tm = min(256, t)

    def body(dy_ref, x_ref, gain_ref, g_ref, u_ref, wg_ref, wu_ref, wd_ref, dx_ref, dg_ref, du_ref, dgain_ref):
        dyv = dy_ref[...]
        da = lax.dot_general(dyv.astype(BF16), wd_ref[...], NT, preferred_element_type=F32)
        gv, uv = g_ref[...].astype(F32), u_ref[...].astype(F32)
        sg = _sigmoid(gv)
        act = gv * sg
        du = (da * act).astype(BF16)
        dg = (da * uv * (sg * (1.0 + gv * (1.0 - sg)))).astype(BF16)
        du_ref[...] = du
        dg_ref[...] = dg
        dh = jnp.dot(dg, wg_ref[...], preferred_element_type=F32) + jnp.dot(du, wu_ref[...], preferred_element_type=F32)
        xv, gain_v = x_ref[...], gain_ref[...]
        dx, xhat = _rms_bwd(dh, xv, gain_v)
        dx_ref[...] = dyv + dx
        _accumulate(dgain_ref, jnp.sum(dh * xhat, axis=0, keepdims=True), pl.program_id(0) == 0)

    row = pl.BlockSpec((tm, d), lambda i: (i, 0))
    wide = pl.BlockSpec((tm, f), lambda i: (i, 0))
    outs, exchanged = _hosted_call(
        body, exchange, name=f"ffn_bwd_{tag}", grid=(t // tm,),
        in_specs=[row, row, _resident((1, d)), wide, wide, _resident(wg.shape), _resident(wu.shape), _resident(wd.shape)],
        out_specs=[row, wide, wide, pl.BlockSpec((1, d), lambda i: (0, 0))],
        out_shape=[jax.ShapeDtypeStruct((t, d), F32), jax.ShapeDtypeStruct((t, f), BF16), jax.ShapeDtypeStruct((t, f), BF16),
                   jax.ShapeDtypeStruct((1, d), F32)],
        scratch_shapes=[], semantics=("arbitrary",), args=(dy, x, gain, g, u, wg, wu, wd))
    return (*outs, exchanged)


def _tn_matmul(a, b, name, into=None, row_block=0, row_blocks=1):
    t, k = a.shape
    n = b.shape[1]
    tk = k // 2 if (k // 2) % LANES == 0 else k
    tt = min(2048, t)
    first = row_block * (k // tk)

    def body(a_ref, b_ref, *rest):
        o_ref, acc_ref = rest[-2:]
        prod = lax.dot_general(a_ref[...].astype(BF16), b_ref[...].astype(BF16), TN, preferred_element_type=F32)
        j = pl.program_id(1)

        @pl.when(j == 0)
        def _():
            acc_ref[...] = prod

        @pl.when(j > 0)
        def _():
            acc_ref[...] += prod

        @pl.when(j == pl.num_programs(1) - 1)
        def _():
            o_ref[...] = acc_ref[...].astype(BF16)

    return pl.pallas_call(
        body, name=name, grid=(k // tk, t // tt),
        in_specs=[pl.BlockSpec((tt, tk), lambda i, j: (j, i)), pl.BlockSpec((tt, n), lambda i, j: (j, 0))]
                 + ([ANY] if into is not None else []),
        out_specs=pl.BlockSpec((tk, n), lambda i, j: (first + i, 0)),
        out_shape=jax.ShapeDtypeStruct((row_blocks * k, n), BF16),
        scratch_shapes=[pltpu.VMEM((tk, n), F32)],
        input_output_aliases={2: 0} if into is not None else {},
        compiler_params=_params("parallel", "arbitrary"),
    )(a, b, *([into] if into is not None else []))


def _attn_out_bwd(dx, w, o, dils, seq, tag, lse=None, sink=None, exchange=()):
    t, d = dx.shape
    ts = _tile_rows(seq)
    bl = t // seq
    ng = len(dils)
    has_sink = sink is not None
    expand = _head_expand().T

    def body(*refs):
        refs = list(refs)
        dx_ref, w_ref, o_ref, e_ref = refs[:4]
        refs = refs[4:]
        lse_ref, sink_ref = (refs.pop(0), refs.pop(0)) if has_sink else (None, None)
        do_refs, dl_refs = refs[:ng], refs[ng:2 * ng]
        refs = refs[2 * ng:]
        dsink_ref = refs.pop(0) if has_sink else None
        dof_ref, dlf_ref = refs
        do = lax.dot_general(dx_ref[...].astype(BF16), w_ref[...], NT, preferred_element_type=F32)
        prod = do * o_ref[...].astype(F32)
        hi = prod.astype(BF16)
        lo = (prod - hi.astype(F32)).astype(BF16)
        e = e_ref[...]
        dl = jnp.dot(hi, e, preferred_element_type=F32) + jnp.dot(lo, e, preferred_element_type=F32)
        for g in range(ng):
            for r, part in enumerate(_split_rows(do, dof_ref, dils[g])):
                do_refs[g][r] = part.astype(BF16)
            for r, part in enumerate(_split_rows(dl, dlf_ref, dils[g])):
                dl_refs[g][r] = part
        if has_sink:
            part = -jnp.exp2(sink_ref[...] - lse_ref[...]) * dl
            _accumulate(dsink_ref, jnp.sum(part, axis=0, keepdims=True), pl.program_id(0) == 0)

    row = pl.BlockSpec((ts, d), lambda i: (i, 0))
    narrow = pl.BlockSpec((ts, LANES), lambda i: (i, 0))
    args = [dx, w, o, expand]
    in_specs = [row, _resident(w.shape), pl.BlockSpec((ts, Q_W), lambda i: (i, 0)), _resident(expand.shape)]
    if has_sink:
        args += [lse, jnp.pad(sink.reshape(1, N_HEADS) * LOG2E, ((0, 0), (0, LANES - N_HEADS)))]
        in_specs += [narrow, _resident((1, LANES))]
    out_specs = [_res_spec(seq, dl, Q_W) for dl in dils] + [_res_spec(seq, dl, LANES) for dl in dils]
    out_shape = ([jax.ShapeDtypeStruct(_res_shape(bl, seq, dl, Q_W), BF16) for dl in dils]
                 + [jax.ShapeDtypeStruct(_res_shape(bl, seq, dl, LANES), F32) for dl in dils])
    if has_sink:
        out_specs.append(pl.BlockSpec((1, LANES), lambda i: (0, 0)))
        out_shape.append(jax.ShapeDtypeStruct((1, LANES), F32))
    outs, exchanged = _hosted_call(
        body, exchange, name=f"attn_out_bwd_{tag}", grid=(t // ts,), in_specs=in_specs, out_specs=out_specs, out_shape=out_shape,
        scratch_shapes=[_stage(ts, Q_W), _stage(ts, LANES)], semantics=("arbitrary" if has_sink else "parallel",), args=args)
    return list(outs[:ng]), list(outs[ng:2 * ng]), (outs[2 * ng] if has_sink else None), exchanged


def _attn_bwd(qkv, do, lse, delta, cos, sin, w, tag, exchange=()):
    shape = qkv.shape
    dil = shape[1]
    rows_all = _seq_view(qkv)
    nseq, length, _ = rows_all.shape
    bq = min(QUERY_BLOCK, length)
    wk = _key_rows(bq, w, length)
    nb = length // bq
    per_step = _per_step(dil, length)

    def body(*refs):
        def sub(i, carry):
            one(*[ref.at[i] for ref in refs[:7]], *refs[7:])
            return carry

        if per_step == 1:
            sub(0, 0)
        else:
            lax.fori_loop(0, per_step, sub, 0)

    def one(qkv_ref, do_ref, lse_ref, dl_ref, cos_ref, sin_ref, dp_ref, kk_ref, vv_ref, dk_ref, dv_ref):
        _swap_halves(qkv_ref, Q_W, kk_ref)
        _swap_halves(qkv_ref, Q_W + KV_W, vv_ref)
        dk_ref[...] = jnp.zeros_like(dk_ref)
        dv_ref[...] = jnp.zeros_like(dv_ref)
        band = _band(bq, wk)
        lo_q = lax.broadcasted_iota(jnp.int32, (bq, LANES), 1) < HEAD_DIM
        hi_q = jnp.logical_not(lo_q)

        def block(i, carry):
            q0, k0 = _window(i, bq, w, wk, length)
            valid = jnp.abs(band + (q0 - k0)) <= w
            rows, krows = pl.ds(q0, bq), pl.ds(k0, wk)
            c, sn = cos_ref[rows, :], -sin_ref[rows, :]
            lse_t, dl_t = lse_ref[rows, :], dl_ref[rows, :]
            for kv in range(N_KV):
                heads = [(kv * GRP + h, h % 2) for h in range(GRP)]
                cols = [slice((kv * 2 + j) * LANES, (kv * 2 + j + 1) * LANES) for j in range(GRP // 2)]
                qp = [qkv_ref[rows, cs] for cs in cols]
                dop = [do_ref[rows, cs] for cs in cols]
                k2 = _pair_operand(qkv_ref, kk_ref, Q_W, kv, krows)
                v2 = _pair_operand(qkv_ref, vv_ref, Q_W + KV_W, kv, krows)
                sc2 = [lax.dot_general(q_, k2, NT, preferred_element_type=F32) for q_ in qp]
                dp2 = [lax.dot_general(d_, v2, NT, preferred_element_type=F32) for d_ in dop]
                sc = [s_[:, half * wk:(half + 1) * wk] for s_ in sc2 for half in range(2)]
                dp = [d_[:, half * wk:(half + 1) * wk] for d_ in dp2 for half in range(2)]
                p = [jnp.exp2(jnp.where(valid, s_, NEG_INF) - _over_keys(lse_t[:, hd:hd + 1], wk))
                     for s_, (hd, _) in zip(sc, heads)]
                ds = [(p_ * (dp_ - _over_keys(dl_t[:, hd:hd + 1], wk))).astype(BF16) for p_, dp_, (hd, _) in zip(p, dp, heads)]
                pb = [p_.astype(BF16) for p_ in p]
                for j in range(GRP // 2):
                    dq = jnp.dot(jnp.concatenate([ds[2 * j], ds[2 * j + 1]], axis=1), k2, preferred_element_type=F32) * SCALE
                    dp_ref[rows, cols[j]] = _rope(dq, c, sn).astype(BF16)
                zero = jnp.zeros((bq, LANES), BF16)
                q4 = jnp.concatenate([jnp.where(lo_q if h % 2 == 0 else hi_q, qp[h // 2], zero) for h in range(GRP)], axis=0)
                do4 = jnp.concatenate([jnp.where(lo_q if h % 2 == 0 else hi_q, dop[h // 2], zero) for h in range(GRP)], axis=0)
                dk_ref[kv, krows, :] += lax.dot_general(jnp.concatenate(ds, axis=0), q4, TN, preferred_element_type=F32)
                dv_ref[kv, krows, :] += lax.dot_general(jnp.concatenate(pb, axis=0), do4, TN, preferred_element_type=F32)
            return carry

        lax.fori_loop(0, nb, block, 0)
        lo = lax.broadcasted_iota(jnp.int32, (length, LANES), 1) < HEAD_DIM
        c, sn = cos_ref[...], -sin_ref[...]
        for ch in range(KV_W // LANES):
            halves = []
            for acc_ref in (dk_ref, dv_ref):
                even, odd = acc_ref[2 * ch], acc_ref[2 * ch + 1]
                even = even + pltpu.roll(even, HEAD_DIM, 1)
                odd = odd + pltpu.roll(odd, HEAD_DIM, 1)
                halves.append(jnp.where(lo, even, odd))
            dp_ref[:, Q_W + ch * LANES:Q_W + (ch + 1) * LANES] = _rope(halves[0] * LN2, c, sn).astype(BF16)
            dp_ref[:, Q_W + KV_W + ch * LANES:Q_W + KV_W + (ch + 1) * LANES] = halves[1].astype(BF16)

    def seq_block(c):
        return pl.BlockSpec((per_step, length, c), lambda i: (i, 0, 0))

    table = pl.BlockSpec((per_step, length, LANES), lambda i: (i % (dil // per_step), 0, 0))
    (out,), exchanged = _hosted_call(
        body, exchange, name=f"attn_bwd_{tag}", grid=(nseq // per_step,),
        in_specs=[seq_block(QKV_W), seq_block(Q_W), seq_block(LANES), seq_block(LANES), table, table],
        out_specs=[seq_block(QKV_W)],
        out_shape=[jax.ShapeDtypeStruct((nseq, length, QKV_W), BF16)],
        scratch_shapes=[pltpu.VMEM((KV_W // LANES, length, LANES), BF16), pltpu.VMEM((KV_W // LANES, length, LANES), BF16),
                        pltpu.VMEM((N_KV, length, LANES), F32), pltpu.VMEM((N_KV, length, LANES), F32)],
        semantics=("parallel",), args=(rows_all, _seq_view(do), _seq_view(lse), _seq_view(delta), cos, sin))
    return out.reshape(shape), exchanged


def _qkv_bwd(dy, x, gain, w, dps, dils, seq, tag, exchange=()):
    t, d = x.shape
    ts = _tile_rows(seq)
    ng = len(dps)

    def body(dy_ref, x_ref, gain_ref, w_ref, *refs):
        dp_refs, (dx_ref, dgain_ref, stage_ref) = refs[:ng], refs[ng:]
        dh = None
        for gi in range(ng):
            dil = dils[gi]
            n = ts // dil
            dp = dp_refs[gi][0] if dil == 1 else jnp.concatenate([dp_refs[gi][r] for r in range(dil)], axis=0)
            part = jnp.dot(dp, w_ref[gi * QKV_W:(gi + 1) * QKV_W, :], preferred_element_type=F32)
            part = _merge_rows([part[r * n:(r + 1) * n] for r in range(dil)], stage_ref, dil)
            dh = part if dh is None else dh + part
        xv, gain_v = x_ref[...], gain_ref[...]
        dx, xhat = _rms_bwd(dh, xv, gain_v)
        dx_ref[...] = dy_ref[...] + dx
        _accumulate(dgain_ref, jnp.sum(dh * xhat, axis=0, keepdims=True), pl.program_id(0) == 0)

    row = pl.BlockSpec((ts, d), lambda i: (i, 0))
    (dx, dgain), exchanged = _hosted_call(
        body, exchange, name=f"qkv_bwd_{tag}", grid=(t // ts,),
        in_specs=[row, row, _resident((1, d)), _resident(w.shape)] + [_res_spec(seq, dl, QKV_W) for dl in dils],
        out_specs=[row, pl.BlockSpec((1, d), lambda i: (0, 0))],
        out_shape=[jax.ShapeDtypeStruct((t, d), F32), jax.ShapeDtypeStruct((1, d), F32)],
        scratch_shapes=[_stage(ts, d)], semantics=("arbitrary",), args=(dy, x, gain, w, *dps))
    return dx, dgain, exchanged


ANY = pl.BlockSpec(memory_space=pl.ANY)


def _place():
    x, y, c = lax.axis_index("x"), lax.axis_index("y"), lax.axis_index("c")
    return x, y, c


def _exchange_steps(srcs, dsts, gather, send_sems, recv_sems, local_sems):
    x, y, c = _place()
    me, sibling = (x, y, c), (x, y, 1 - c)
    chips = [(1 - x, y), (x, 1 - y), (1 - x, 1 - y)]
    mine = 4 * x + 2 * y + c

    def slot(a, device):
        px, py, pc = device
        return dsts[a].at[4 * px + 2 * py + pc]

    def passes(a, k, block, to, src=None):
        rows = slot(a, block)
        return pltpu.make_async_remote_copy(src_ref=rows if src is None else src, dst_ref=rows, send_sem=send_sems.at[a, k],
                                            recv_sem=recv_sems.at[a, k], device_id=to, device_id_type=MESH)

    def scatters(a, k):
        peer = mine ^ k
        return pltpu.make_async_remote_copy(
            src_ref=srcs[a].at[peer], dst_ref=dsts[a].at[mine], send_sem=send_sems.at[a, k - 1], recv_sem=recv_sems.at[a, k - 1],
            device_id=(peer // 4, (peer // 2) % 2, peer % 2), device_id_type=MESH)

    def local(a):
        return pltpu.make_async_copy(srcs[a] if gather[a] else srcs[a].at[mine], dsts[a].at[mine], local_sems.at[a])

    def first_copies(a):
        if not gather[a]:
            return [scatters(a, k) for k in range(1, N_DEV)]
        return [passes(a, 0, me, sibling, src=srcs[a])] + [passes(a, 1 + j, me, (*chip, c), src=srcs[a]) for j, chip in enumerate(chips)]

    def start():
        for a in range(len(srcs)):
            local(a).start()
            for cp in first_copies(a):
                cp.start()

    def forward():
        for a in range(len(srcs)):
            if gather[a]:
                for j, chip in enumerate(chips):
                    passes(a, 1 + j, (*chip, c), me).wait_recv()
                    passes(a, 4 + j, (*chip, c), sibling).start()

    def finish():
        for a in range(len(srcs)):
            if gather[a]:
                passes(a, 0, sibling, me).wait_recv()
                for j, chip in enumerate(chips):
                    passes(a, 4 + j, (*chip, 1 - c), me).wait_recv()
                    passes(a, 4 + j, (*chip, c), sibling).wait_send()
                for cp in first_copies(a):
                    cp.wait_send()
            else:
                for cp in first_copies(a):
                    cp.wait()
            local(a).wait()

    return start, forward, finish


def _exchange_scratch(n):
    return [pltpu.SemaphoreType.DMA((n, N_DEV - 1)), pltpu.SemaphoreType.DMA((n, N_DEV - 1)), pltpu.SemaphoreType.DMA((n,))]


def _exchanged_shapes(exchange):
    return [jax.ShapeDtypeStruct(((N_DEV,) + a.shape) if g else a.shape, a.dtype) for a, g in exchange]


def _hosted_call(body, exchange, *, name, grid, in_specs, out_specs, out_shape, scratch_shapes, semantics, args):
    out_specs, out_shape, scratch = list(out_specs), list(out_shape), list(scratch_shapes)
    if not exchange:
        outs = pl.pallas_call(body, name=name, grid=grid, in_specs=in_specs, out_specs=out_specs, out_shape=out_shape,
                              scratch_shapes=scratch, compiler_params=_params(*semantics))(*args)
        return list(outs), []
    n, n_in, n_out, n_scr = len(exchange), len(in_specs), len(out_specs), len(scratch)
    gather = [g for _, g in exchange]
    steps = math.prod(grid)

    def hosted(*refs):
        own_in, x_in = refs[:n_in], refs[n_in:n_in + n]
        own_out, x_out = refs[n_in + n:n_in + n + n_out], refs[n_in + n + n_out:n_in + 2 * n + n_out]
        own_scr, sems = refs[n_in + 2 * n + n_out:n_in + 2 * n + n_out + n_scr], refs[-3:]
        step = pl.program_id(0)
        for axis in range(1, len(grid)):
            step = step * grid[axis] + pl.program_id(axis)
        start, forward, finish = _exchange_steps(x_in, x_out, gather, *sems)
        pl.when(step == 0)(start)
        body(*own_in, *own_out, *own_scr)
        pl.when(step == steps // 2)(forward)
        pl.when(step == steps - 1)(finish)

    outs = pl.pallas_call(
        hosted, name=name, grid=grid, in_specs=list(in_specs) + [ANY] * n, out_specs=out_specs + [ANY] * n,
        out_shape=out_shape + _exchanged_shapes(exchange), scratch_shapes=scratch + _exchange_scratch(n),
        compiler_params=_params(*["arbitrary"] * len(grid)),
    )(*args, *[a for a, _ in exchange])
    return list(outs[:n_out]), list(outs[n_out:])


def _exchange_now(exchange, name):
    n = len(exchange)
    gather = [g for _, g in exchange]

    def body(*refs):
        for step in _exchange_steps(refs[:n], refs[n:2 * n], gather, *refs[2 * n:]):
            step()

    return pl.pallas_call(
        body, name=name, in_specs=[ANY] * n, out_specs=[ANY] * n, out_shape=_exchanged_shapes(exchange),
        scratch_shapes=_exchange_scratch(n),
    )(*[a for a, _ in exchange])


def _all_reduce_small(v):
    def body(v_ref, o_ref, recv_ref, send_sems, recv_sems):
        x, y, c = _place()
        me = 4 * x + 2 * y + c
        copies = []
        for k in range(1, N_DEV):
            peer = me ^ k
            copies.append(pltpu.make_async_remote_copy(
                src_ref=v_ref, dst_ref=recv_ref.at[k], send_sem=send_sems.at[k - 1], recv_sem=recv_sems.at[k - 1],
                device_id=(peer // 4, (peer // 2) % 2, peer % 2), device_id_type=MESH))
        for cp in copies:
            cp.start()
        recv_ref[0] = v_ref[...]
        for cp in copies:
            cp.wait()
        acc = recv_ref[me]
        for src in range(1, N_DEV):
            acc = acc + recv_ref[me ^ src]
        o_ref[...] = acc

    vm = pl.BlockSpec(memory_space=pltpu.VMEM)
    return pl.pallas_call(
        body, name="all_reduce_small", in_specs=[vm], out_specs=vm, out_shape=jax.ShapeDtypeStruct(v.shape, F32),
        scratch_shapes=[pltpu.VMEM((N_DEV,) + v.shape, F32), pltpu.SemaphoreType.DMA((N_DEV - 1,)),
                        pltpu.SemaphoreType.DMA((N_DEV - 1,))],
    )(v)


def _adamw_math(w, g, m, v):
    m = ADAM_B1 * m + (1.0 - ADAM_B1) * g
    v = ADAM_B2 * v + (1.0 - ADAM_B2) * (g * g)
    m_hat = m / (1.0 - ADAM_B1 ** ADAM_STEP)
    v_hat = v / (1.0 - ADAM_B2 ** ADAM_STEP)
    delta = -ADAM_LR * (m_hat / (jnp.sqrt(v_hat) + ADAM_EPS) + ADAM_WD * w)
    return delta, m, v


def _adamw(parts, w, m, v, name, layer=None, into=None):
    r, c = w.shape[-2:]
    tr = r // 2 if r % 16 == 0 and r >= 256 else r
    n = len(parts)

    def body(*refs):
        w_ref, m_ref, v_ref = refs[n:n + 3]
        g_ref, d_ref, nm_ref, nv_ref = refs[-4:]
        g = refs[0][...].astype(F32)
        for p_ref in refs[1:n]:
            g = g + p_ref[...].astype(F32)
        g_ref[...] = g
        d_ref[...], nm_ref[...], nv_ref[...] = _adamw_math(w_ref[...], g, m_ref[...], v_ref[...])

    def slab(slot):
        return pl.BlockSpec((None, tr, c), lambda i: (slot, i, 0))

    tile = pl.BlockSpec((tr, c), lambda i: (i, 0)) if layer is None else slab(layer)
    arrays, in_specs = [], []
    for p in parts:
        if isinstance(p, tuple):
            arrays.append(p[0])
            in_specs.append(slab(p[1]))
        else:
            arrays.append(p)
            in_specs.append(tile)
    kept = list(into) if into is not None else []
    return pl.pallas_call(
        body, name=name, grid=(r // tr,), in_specs=in_specs + [tile] * 3 + [ANY] * len(kept), out_specs=[tile] * 4,
        out_shape=[jax.ShapeDtypeStruct(w.shape, F32)] * 4,
        input_output_aliases={n + 3 + k: k for k in range(len(kept))}, compiler_params=_params("parallel"),
    )(*arrays, w, m, v, *kept)


def _rows(g):
    return g.reshape(-1, g.shape[-1])


def _row_blocks(dw):
    k, n = dw.shape
    return dw.reshape(N_DEV, k // N_DEV, n)


def _pack_rows(rows, width):
    out = None
    for i, r in enumerate(rows):
        r = r.reshape(1, -1).astype(F32)
        r = jnp.pad(r, ((i, 8 - 1 - i), (0, width - r.shape[1])))
        out = r if out is None else out + r
    return out


def _mixer_fwd(x, gain, w_in, cos, sin, seq, groups, tag, sink=None, exchanges=None):
    exchanges = exchanges or {}
    os, lses, got = [], [], {}
    qkvs, hs, got["proj"] = _qkv_proj(x, gain, w_in, cos, sin, seq, [dil for dil, _ in groups], tag,
                                      exchange=exchanges.get("proj", ()))
    for gi, (dil, w) in enumerate(groups):
        o, lse, got[gi] = _attn_fwd(qkvs[gi], w, f"{tag}{gi}", sink=sink, exchange=exchanges.get(gi, ()))
        os.append(o)
        lses.append(lse)
    o, lses = _mix_groups(os, lses, [dl for dl, _ in groups], seq, tag)
    return (qkvs, hs, o, lses), got


def _mixer_bwd(dy, x_in, gain, w_in, w_out, saved, cos, sin, seq, groups, tag, sink=None, exchanges=None, scatter_own=False):
    qkvs, hs, o, lses = saved
    t, d = x_in.shape
    dils = [dl for dl, _ in groups]
    lse_tokens = lses[0].reshape(t, LANES) if sink is not None else None
    dw_out = _tn_matmul(o, dy, f"dw_out_{tag}")
    dos, dls, dsink, early = _attn_out_bwd(dy, w_out, o, dils, seq, tag, lse=lse_tokens, sink=sink,
                                           exchange=_to_send([dw_out]) if scatter_own else ())
    if scatter_own:
        (dw_out,) = early
    exchanges = exchanges or {}
    dps, got = [], {}
    for gi, (dil, w) in enumerate(groups):
        dp, got[gi] = _attn_bwd(qkvs[gi], dos[gi], lses[gi], dls[gi], _tables_by_residue(cos, seq, dil),
                                _tables_by_residue(sin, seq, dil), w, f"{tag}{gi}", exchange=exchanges.get(gi, ()))
        dps.append(dp)
    dw_in = None
    for gi in range(len(groups)):
        dw_in = _tn_matmul(dps[gi].reshape(t, QKV_W), hs[gi].reshape(t, d), f"dw_in_{tag}{gi}", into=dw_in, row_block=gi,
                           row_blocks=len(groups))
    dx, dgain, late = _qkv_bwd(dy, x_in, gain, w_in, dps, dils, seq, tag, exchange=_to_send([dw_in]) if scatter_own else ())
    if scatter_own:
        (dw_in,) = late
    return dx, dw_in, dw_out, dgain, dsink, got


def _ffn_layer_bwd(dy, x_in, gain, saved, wg, wu, wd, tag, exchange=()):
    g, u, act, h = saved
    dx, dg, du, dgain, got = _ffn_bwd(dy, x_in, gain, g, u, wg, wu, wd, tag, exchange=exchange)
    dwd = _tn_matmul(act, dy, f"dw_down_{tag}")
    dwg = _tn_matmul(dg, h, f"dw_gate_{tag}")
    dwu = _tn_matmul(du, h, f"dw_up_{tag}")
    return dx, dwg, dwu, dwd, dgain, got


def _to_send(dws):
    return [(_row_blocks(g), False) for g in dws]


def kernel(x, a_w_in, a_sink, a_w_out, b_w_in, b_w_out, norm_mix, norm_ffn, w_gate, w_up, w_down, final_norm, loss_target, m_a_w_in, m_a_sink, m_a_w_out, m_b_w_in, m_b_w_out, m_norm_mix, m_norm_ffn, m_w_gate, m_w_up, m_w_down, m_final_norm, v_a_w_in, v_a_sink, v_a_w_out, v_b_w_in, v_b_w_out, v_norm_mix, v_norm_ffn, v_w_gate, v_w_up, v_w_down, v_final_norm):
    bl, seq, d = x.shape
    t = bl * seq
    xf = x.reshape(t, d)
    target = loss_target.reshape(t, d)
    cos, sin = _rope_tables(seq)
    groups_a = [(1, ATTN_HALF_WINDOW)]
    groups_b = [(dil, window // 2 // dil) for window, dil in DILATED_GROUPS]

    def flip(w_):
        return jnp.swapaxes(w_, -1, -2)

    a_w_in, m_a_w_in, v_a_w_in, b_w_in, m_b_w_in, v_b_w_in = map(flip, (a_w_in, m_a_w_in, v_a_w_in, b_w_in, m_b_w_in, v_b_w_in))
    w_gate, m_w_gate, v_w_gate, w_up, m_w_up, v_w_up = map(flip, (w_gate, m_w_gate, v_w_gate, w_up, m_w_up, v_w_up))

    def shard(w_, layer):
        return (w_[layer].astype(BF16), True)

    (wa_in,) = map(_rows, _exchange_now([shard(a_w_in, 0)], "gather_first"))

    saved_a, got = _mixer_fwd(xf, norm_mix[0:1], wa_in, cos, sin, seq, groups_a, "a", sink=a_sink[0],
                              exchanges={"proj": [shard(w_down, 0), shard(a_w_out, 0)], 0: [shard(w_gate, 0), shard(w_up, 0)]})
    wg0, wu0, wd0, wa_out = map(_rows, got[0] + got["proj"])
    x1_0, x2_0, *saved_0, got = _ffn_fwd(xf, saved_a[2], wa_out, norm_ffn[0:1], wg0, wu0, wd0, "0",
                                         exchange=[shard(b_w_in, 0), shard(b_w_out, 0)])
    wb_in, wb_out = map(_rows, got)
    saved_b, got = _mixer_fwd(x2_0, norm_mix[1:2], wb_in, cos, sin, seq, groups_b, "b",
                              exchanges={0: [shard(w_gate, 1)], 1: [shard(w_up, 1)], 2: [shard(w_down, 1)]})
    wg1, wu1, wd1 = map(_rows, got[0] + got[1] + got[2])
    x1_1, dy, *saved_1, loss_part, d_final, _ = _ffn_fwd(x2_0, saved_b[2], wb_out, norm_ffn[1:2], wg1, wu1, wd1, "1",
                                                         loss_head=(final_norm.reshape(1, d), target))

    dy, dwg1, dwu1, dwd1, d_nf1, _ = _ffn_layer_bwd(dy, x1_1, norm_ffn[1:2], saved_1, wg1, wu1, wd1, "1")
    dy, dwb_in, dwb_out, d_nm1, _, got = _mixer_bwd(
        dy, x2_0, norm_mix[1:2], wb_in, wb_out, saved_b, cos, sin, seq, groups_b, "b",
        exchanges={0: _to_send([dwg1, dwd1]), 1: _to_send([dwu1])})
    (r_g1, r_d1), (r_u1,) = got[0], got[1]
    dy, dwg0, dwu0, dwd0, d_nf0, (r_b_in, r_b_out) = _ffn_layer_bwd(
        dy, x1_0, norm_ffn[0:1], saved_0, wg0, wu0, wd0, "0", exchange=_to_send([dwb_in, dwb_out]))
    dy, r_a_in, r_a_out, d_nm0, d_sink, got = _mixer_bwd(
        dy, xf, norm_mix[0:1], wa_in, wa_out, saved_a, cos, sin, seq, groups_a, "a", sink=a_sink[0],
        exchanges={0: _to_send([dwg0, dwu0, dwd0])}, scatter_own=True)
    r_g0, r_u0, r_d0 = got[0]
    grad_x = dy.reshape(bl, seq, d)

    def update(received, w_, m_, v_, name):
        out = None
        for layer in reversed(range(len(received))):
            out = _adamw([(received[layer], src) for src in range(N_DEV)], w_, m_, v_, f"adamw_{name}{layer}", layer=layer, into=out)
        return out

    u_a_in = update([r_a_in], a_w_in, m_a_w_in, v_a_w_in, "a_in")
    u_a_out = update([r_a_out], a_w_out, m_a_w_out, v_a_w_out, "a_out")
    u_b_in = update([r_b_in], b_w_in, m_b_w_in, v_b_w_in, "b_in")
    u_b_out = update([r_b_out], b_w_out, m_b_w_out, v_b_w_out, "b_out")
    u_gate = update([r_g0, r_g1], w_gate, m_w_gate, v_w_gate, "gate")
    u_up = update([r_u0, r_u1], w_up, m_w_up, v_w_up, "up")
    u_down = update([r_d0, r_d1], w_down, m_w_down, v_w_down, "down")

    small = _pack_rows([d_nm0, d_nm1, d_nf0, d_nf1, d_final, d_sink, loss_part], d)
    total = _all_reduce_small(small)
    small_w = _pack_rows([norm_mix[0], norm_mix[1], norm_ffn[0], norm_ffn[1], final_norm, a_sink], d)
    small_m = _pack_rows([m_norm_mix[0], m_norm_mix[1], m_norm_ffn[0], m_norm_ffn[1], m_final_norm, m_a_sink], d)
    small_v = _pack_rows([v_norm_mix[0], v_norm_mix[1], v_norm_ffn[0], v_norm_ffn[1], v_final_norm, v_a_sink], d)
    u_small = _adamw([total], small_w, small_m, small_v, "adamw_small")
    loss = total[6, 0]

    outs = []
    for k in range(4):
        sm = u_small[k]
        outs += [flip(u_a_in[k]), sm[5:6, :N_HEADS], u_a_out[k], flip(u_b_in[k]), u_b_out[k], sm[0:2], sm[2:4],
                 flip(u_gate[k]), flip(u_up[k]), u_down[k], sm[4]]
    return (loss, grad_x, *outs)
```

```python
import functools
import math

import jax
import jax.numpy as jnp
from jax import lax
from jax.experimental import pallas as pl
from jax.experimental.pallas import tpu as pltpu

F32 = jnp.float32
BF16 = jnp.bfloat16

HEAD_DIM = 64
N_HEADS = 16
N_KV = 4
GRP = N_HEADS // N_KV
Q_W = N_HEADS * HEAD_DIM
KV_W = N_KV * HEAD_DIM
QKV_W = Q_W + 2 * KV_W
ATTN_HALF_WINDOW = 128
DILATED_GROUPS = ((128, 1), (512, 4), (2048, 16))
ROPE_THETA = 10000.0
RMS_EPS = 1e-6
NEG_INF = -1e30
SCALE = 1.0 / math.sqrt(HEAD_DIM)
LOG2E = 1.0 / math.log(2.0)
LN2 = math.log(2.0)

ADAM_LR = 0.001
ADAM_B1 = 0.9
ADAM_B2 = 0.999
ADAM_EPS = 1e-08
ADAM_WD = 0.01
ADAM_STEP = 10

LANES = 128
VMEM_LIMIT = 56 * 1024 * 1024
QUERY_BLOCK = 128
N_DEV = 8
MESH = pl.DeviceIdType.MESH

NT = (((1,), (1,)), ((), ()))
TN = (((0,), (0,)), ((), ()))


def _params(*sem):
    return pltpu.CompilerParams(dimension_semantics=tuple(sem) if sem else None, vmem_limit_bytes=VMEM_LIMIT)


def _resident(shape):
    return pl.BlockSpec(shape, lambda *_: (0,) * len(shape), pipeline_mode=pl.Buffered(1))


def _rope_tables(seq):
    inv_freq = 1.0 / (ROPE_THETA ** (jnp.arange(0, HEAD_DIM, 2, dtype=F32) / HEAD_DIM))
    ang = jnp.arange(seq, dtype=F32)[:, None] * inv_freq[None, :]
    cos, sin = jnp.cos(ang), jnp.sin(ang)
    return jnp.tile(cos, (1, 4)), jnp.concatenate([-sin, sin, -sin, sin], axis=1)


def _rope(t, cos, sin_signed):
    lane = lax.broadcasted_iota(jnp.int32, t.shape, 1)
    first = (lane & (HEAD_DIM // 2)) == 0
    swapped = jnp.where(first, pltpu.roll(t, LANES - HEAD_DIM // 2, 1), pltpu.roll(t, HEAD_DIM // 2, 1))
    return t * cos + swapped * sin_signed


def _rms(x):
    return lax.rsqrt(jnp.mean(x * x, axis=-1, keepdims=True) + RMS_EPS)


def _rms_bwd(dh, x, gain):
    r = _rms(x)
    xhat = x * r
    dxh = dh * gain
    dx = r * (dxh - xhat * jnp.mean(dxh * xhat, axis=-1, keepdims=True))
    return dx, xhat


def _accumulate(ref, value, first):
    @pl.when(first)
    def _():
        ref[...] = jnp.zeros_like(ref)

    ref[...] += value


def _tile_rows(seq):
    return min(512, seq)


def _res_shape(bl, seq, dil, c):
    ts = _tile_rows(seq)
    return (bl, dil, seq // ts, ts // dil, c)


def _res_spec(seq, dil, c):
    ts = _tile_rows(seq)
    per_seq = seq // ts
    return pl.BlockSpec((None, dil, None, ts // dil, c), lambda i: (i // per_seq, 0, i % per_seq, 0, 0))


def _seq_view(a):
    bl, dil, tiles, n, c = a.shape
    return a.reshape(bl * dil, tiles * n, c)


def _stage(ts, c):
    return pltpu.VMEM((c // LANES, ts, LANES), F32)


def _split_rows(val, stage_ref, dil):
    if dil == 1:
        return [val]
    ts, c = val.shape
    n, nc = ts // dil, c // LANES
    for k in range(nc):
        stage_ref[k] = val[:, k * LANES:(k + 1) * LANES]
    return [jnp.concatenate([stage_ref[k, pl.ds(r, n, stride=dil), :] for k in range(nc)], axis=1) for r in range(dil)]


def _merge_rows(parts, stage_ref, dil):
    if dil == 1:
        return parts[0]
    n, c = parts[0].shape
    nc = c // LANES
    for r, part in enumerate(parts):
        for k in range(nc):
            stage_ref[k, pl.ds(r, n, stride=dil), :] = part[:, k * LANES:(k + 1) * LANES]
    return jnp.concatenate([stage_ref[k] for k in range(nc)], axis=1)


def _tables_tiled(table, seq, dil):
    ts = _tile_rows(seq)
    return table.reshape(seq // ts, ts // dil, dil, LANES).transpose(0, 2, 1, 3).reshape(seq, LANES)


def _tables_by_residue(table, seq, dil):
    return table.reshape(seq // dil, dil, LANES).transpose(1, 0, 2)


def _qkv_proj(x, gain, w, cos, sin, seq, dils, tag, exchange=()):
    t, d = x.shape
    ts = _tile_rows(seq)
    per_seq = seq // ts
    ng = len(dils)
    tables = [t_ for dil in dils for t_ in (_tables_tiled(cos, seq, dil), _tables_tiled(sin, seq, dil))]

    def body(x_ref, g_ref, w_ref, *refs):
        table_refs, o_refs, h_refs, stage_ref = refs[:2 * ng], refs[2 * ng:3 * ng], refs[3 * ng:4 * ng], refs[4 * ng]
        xv = x_ref[...]
        h_tokens = xv * _rms(xv) * g_ref[...]
        for gi, dil in enumerate(dils):
            n = ts // dil
            h = jnp.concatenate(_split_rows(h_tokens, stage_ref, dil), axis=0).astype(BF16)
            for r in range(dil):
                h_refs[gi][r] = h[r * n:(r + 1) * n]
            acc = lax.dot_general(h, w_ref[gi * QKV_W:(gi + 1) * QKV_W, :], NT, preferred_element_type=F32)
            c, s = table_refs[2 * gi][...], table_refs[2 * gi + 1][...]
            for j in range(QKV_W // LANES):
                cols = slice(j * LANES, (j + 1) * LANES)
                val = acc[:, cols]
                if j < (Q_W + KV_W) // LANES:
                    val = _rope(val, c, s)
                if j < Q_W // LANES:
                    val = val * (SCALE * LOG2E)
                val = val.astype(BF16)
                for r in range(dil):
                    o_refs[gi][r, :, cols] = val[r * n:(r + 1) * n]

    table = pl.BlockSpec((ts, LANES), lambda i: (i % per_seq, 0))
    outs, exchanged = _hosted_call(
        body, exchange, name=f"qkv_proj_{tag}", grid=(t // ts,),
        in_specs=[pl.BlockSpec((ts, d), lambda i: (i, 0)), _resident((1, d)), _resident(w.shape)] + [table] * (2 * ng),
        out_specs=[_res_spec(seq, dil, QKV_W) for dil in dils] + [_res_spec(seq, dil, d) for dil in dils],
        out_shape=[jax.ShapeDtypeStruct(_res_shape(t // seq, seq, dil, QKV_W), BF16) for dil in dils]
                  + [jax.ShapeDtypeStruct(_res_shape(t // seq, seq, dil, d), BF16) for dil in dils],
        scratch_shapes=[_stage(ts, d)], semantics=("parallel",), args=(x, gain, w, *tables))
    return outs[:ng], outs[ng:], exchanged


def _band(bq, wk):
    return lax.broadcasted_iota(jnp.int32, (bq, wk), 0) - lax.broadcasted_iota(jnp.int32, (bq, wk), 1)


def _swap_halves(src_ref, base, dst_ref):
    for c in range(KV_W // LANES):
        dst_ref[c] = pltpu.roll(src_ref[:, base + c * LANES:base + (c + 1) * LANES], HEAD_DIM, 1)


def _pair_operand(src_ref, swapped_ref, base, kv, rows):
    c = kv // 2
    chunk, swapped = src_ref[rows, base + c * LANES:base + (c + 1) * LANES], swapped_ref[c, rows, :]
    lo = lax.broadcasted_iota(jnp.int32, chunk.shape, 1) < HEAD_DIM
    zero = jnp.zeros_like(chunk)
    if kv % 2 == 0:
        return jnp.concatenate([jnp.where(lo, chunk, zero), jnp.where(lo, zero, swapped)], axis=0)
    return jnp.concatenate([jnp.where(lo, swapped, zero), jnp.where(lo, zero, chunk)], axis=0)


def _over_keys(col, wk):
    if wk % LANES:
        return jnp.broadcast_to(col, (col.shape[0], wk))
    wide = jnp.broadcast_to(col, (col.shape[0], LANES))
    return wide if wk == LANES else jnp.concatenate([wide] * (wk // LANES), axis=1)


def _per_step(dil, length):
    return max(1, min(dil, 512 // length))


def _key_rows(bq, w, length):
    return min(bq + 2 * w, length)


def _window(i, bq, w, wk, length):
    q0 = pl.multiple_of(i * bq, bq)
    k0 = pl.multiple_of(jnp.clip(q0 - w, 0, length - wk), min(w, bq))
    return q0, k0


def _attn_fwd(qkv, w, tag, sink=None, exchange=()):
    shape = qkv.shape
    rows_all = _seq_view(qkv)
    nseq, length, _ = rows_all.shape
    bq = min(QUERY_BLOCK, length)
    wk = _key_rows(bq, w, length)
    nb = length // bq
    has_sink = sink is not None
    per_step = _per_step(shape[1], length)

    def body(*refs):
        sink_ref = refs[1] if has_sink else None
        kk_ref, vv_ref = refs[-2:]
        for sub in range(per_step):
            one(refs[0].at[sub], refs[-4].at[sub], refs[-3].at[sub], sink_ref, kk_ref, vv_ref)

    def one(qkv_ref, o_ref, lse_ref, sink_ref, kk_ref, vv_ref):
        _swap_halves(qkv_ref, Q_W, kk_ref)
        _swap_halves(qkv_ref, Q_W + KV_W, vv_ref)
        band = _band(bq, wk)
        lane = lax.broadcasted_iota(jnp.int32, (bq, LANES), 1)
        lo = lane < HEAD_DIM

        def block(i, carry):
            q0, k0 = _window(i, bq, w, wk, length)
            valid = jnp.abs(band + (q0 - k0)) <= w
            rows, krows = pl.ds(q0, bq), pl.ds(k0, wk)
            lse_tile = jnp.zeros((bq, LANES), F32)
            for kv in range(N_KV):
                heads = [(kv * GRP + h, h % 2) for h in range(GRP)]
                qp = [qkv_ref[rows, (kv * 2 + j) * LANES:(kv * 2 + j + 1) * LANES] for j in range(GRP // 2)]
                k2 = _pair_operand(qkv_ref, kk_ref, Q_W, kv, krows)
                v2 = _pair_operand(qkv_ref, vv_ref, Q_W + KV_W, kv, krows)
                sc2 = [lax.dot_general(q_, k2, NT, preferred_element_type=F32) for q_ in qp]
                sc = [jnp.where(valid, s_[:, half * wk:(half + 1) * wk], NEG_INF) for s_ in sc2 for half in range(2)]
                m = [jnp.max(s_, axis=-1, keepdims=True) for s_ in sc]
                if has_sink:
                    m = [jnp.maximum(m_, sink_ref[hd]) for m_, (hd, _) in zip(m, heads)]
                mb = [jnp.broadcast_to(m_, (bq, LANES)) for m_ in m]
                p = [jnp.exp2(s_ - _over_keys(m_, wk)) for s_, m_ in zip(sc, m)]
                den = [jnp.sum(p_, axis=-1, keepdims=True) for p_ in p]
                if has_sink:
                    den = [d_ + jnp.exp2(sink_ref[hd] - m_) for d_, m_, (hd, _) in zip(den, m, heads)]
                inv = [jnp.broadcast_to(1.0 / d_, (bq, LANES)) for d_ in den]
                pb = [p_.astype(BF16) for p_ in p]
                for j in range(GRP // 2):
                    o = jnp.dot(jnp.concatenate([pb[2 * j], pb[2 * j + 1]], axis=1), v2, preferred_element_type=F32)
                    o = o * jnp.where(lo, inv[2 * j], inv[2 * j + 1])
                    o_ref[rows, (kv * 2 + j) * LANES:(kv * 2 + j + 1) * LANES] = o.astype(BF16)
                for h, (hd, _) in enumerate(heads):
                    lse_tile = jnp.where(lane == hd, mb[h] - jnp.log(inv[h]) * LOG2E, lse_tile)
            lse_ref[rows, :] = lse_tile
            return carry

        lax.fori_loop(0, nb, block, 0)

    def seq_block(c):
        return pl.BlockSpec((per_step, length, c), lambda i: (i, 0, 0))

    args = [rows_all]
    in_specs = [seq_block(QKV_W)]
    if has_sink:
        args.append(sink * LOG2E)
        in_specs.append(pl.BlockSpec(memory_space=pltpu.SMEM))
    (o, lse), exchanged = _hosted_call(
        body, exchange, name=f"attn_fwd_{tag}", grid=(nseq // per_step,), in_specs=in_specs,
        out_specs=[seq_block(Q_W), seq_block(LANES)],
        out_shape=[jax.ShapeDtypeStruct((nseq, length, Q_W), BF16), jax.ShapeDtypeStruct((nseq, length, LANES), F32)],
        scratch_shapes=[pltpu.VMEM((KV_W // LANES, length, LANES), BF16), pltpu.VMEM((KV_W // LANES, length, LANES), BF16)],
        semantics=("parallel",), args=args)
    return o.reshape(shape[:-1] + (Q_W,)), lse.reshape(shape[:-1] + (LANES,)), exchanged


def _head_expand():
    return (jnp.arange(LANES)[:, None] == jnp.arange(Q_W)[None, :] // HEAD_DIM).astype(BF16)


def _mix_groups(os, lses, dils, seq, tag):
    bl = os[0].shape[0]
    ts = _tile_rows(seq)
    t = bl * seq
    ng = len(os)
    if ng == 1 and dils[0] == 1:
        return os[0].reshape(t, Q_W), [lses[0]]

    def body(*refs):
        e_ref = refs[0]
        o_refs, l_refs = refs[1:1 + ng], refs[1 + ng:1 + 2 * ng]
        om_ref = refs[1 + 2 * ng]
        lt_refs = refs[2 + 2 * ng:2 + 3 * ng]
        wide_ref, narrow_ref = refs[2 + 3 * ng:]
        ls = [_merge_rows([l_refs[g][r] for r in range(dils[g])], narrow_ref, dils[g]) for g in range(ng)]
        mx = functools.reduce(jnp.maximum, ls)
        tot = mx + jnp.log(functools.reduce(lambda a, b: a + b, [jnp.exp2(l_ - mx) for l_ in ls])) * LOG2E
        e = e_ref[...]
        o = None
        for g in range(ng):
            wt = jnp.exp2(ls[g] - tot)
            hi = wt.astype(BF16)
            lo = (wt - hi.astype(F32)).astype(BF16)
            wide = jnp.dot(hi, e, preferred_element_type=F32) + jnp.dot(lo, e, preferred_element_type=F32)
            term = wide * _merge_rows([o_refs[g][r].astype(F32) for r in range(dils[g])], wide_ref, dils[g])
            o = term if o is None else o + term
        om_ref[...] = o.astype(BF16)
        for g in range(ng):
            for r, part in enumerate(_split_rows(tot, narrow_ref, dils[g])):
                lt_refs[g][r] = part

    e = _head_expand()
    outs = pl.pallas_call(
        body, name=f"mix_groups_{tag}", grid=(t // ts,),
        in_specs=[_resident(e.shape)] + [_res_spec(seq, dl, Q_W) for dl in dils] + [_res_spec(seq, dl, LANES) for dl in dils],
        out_specs=[pl.BlockSpec((ts, Q_W), lambda i: (i, 0))] + [_res_spec(seq, dl, LANES) for dl in dils],
        out_shape=[jax.ShapeDtypeStruct((t, Q_W), BF16)]
                  + [jax.ShapeDtypeStruct(_res_shape(bl, seq, dl, LANES), F32) for dl in dils],
        scratch_shapes=[_stage(ts, Q_W), _stage(ts, LANES)],
        compiler_params=_params("parallel"),
    )(e, *os, *lses)
    return outs[0], list(outs[1:])


def _sigmoid(g):
    return 1.0 / (1.0 + jnp.exp(-g))


def _ffn_fwd(x0, o, w_out, gain, wg, wu, wd, tag, exchange=(), loss_head=None):
    t, d = x0.shape
    f = wd.shape[0]
    tm = min(256, t)
    has_loss = loss_head is not None

    def body(*refs):
        x0_ref, o_ref, wo_ref, gain_ref, wg_ref, wu_ref, wd_ref = refs[:7]
        x_ref, y_ref, g_ref, u_ref, a_ref, h_ref = refs[-8:-2] if has_loss else refs[-6:]
        xv = x0_ref[...] + jnp.dot(o_ref[...], wo_ref[...], preferred_element_type=F32)
        x_ref[...] = xv
        h = (xv * _rms(xv) * gain_ref[...]).astype(BF16)
        h_ref[...] = h
        g = lax.dot_general(h, wg_ref[...], NT, preferred_element_type=F32)
        u = lax.dot_general(h, wu_ref[...], NT, preferred_element_type=F32)
        g_ref[...] = g.astype(BF16)
        u_ref[...] = u.astype(BF16)
        a = (g * _sigmoid(g) * u).astype(BF16)
        a_ref[...] = a
        y = xv + jnp.dot(a, wd_ref[...], preferred_element_type=F32)
        if not has_loss:
            y_ref[...] = y
            return
        head_ref, target_ref, loss_ref, dhead_ref = refs[7], refs[8], refs[-2], refs[-1]
        head = head_ref[...]
        yhat = y * _rms(y)
        err = yhat * head - target_ref[...]
        dout = err * (1.0 / d)
        y_ref[...] = _rms_bwd(dout, y, head)[0]
        first = pl.program_id(0) == 0
        part = 0.5 * jnp.sum(jnp.mean(err * err, axis=-1, keepdims=True), axis=0, keepdims=True)
        _accumulate(loss_ref, jnp.broadcast_to(part, loss_ref.shape), first)
        _accumulate(dhead_ref, jnp.sum(dout * yhat, axis=0, keepdims=True), first)

    row = pl.BlockSpec((tm, d), lambda i: (i, 0))
    wide = pl.BlockSpec((tm, f), lambda i: (i, 0))
    in_specs = [row, pl.BlockSpec((tm, Q_W), lambda i: (i, 0)), _resident(w_out.shape), _resident((1, d)), _resident(wg.shape),
                _resident(wu.shape), _resident(wd.shape)]
    out_specs = [row, row, wide, wide, wide, row]
    out_shape = ([jax.ShapeDtypeStruct((t, d), F32)] * 2 + [jax.ShapeDtypeStruct((t, f), BF16)] * 3
                 + [jax.ShapeDtypeStruct((t, d), BF16)])
    if has_loss:
        in_specs += [_resident((1, d)), row]
        out_specs += [pl.BlockSpec((1, LANES), lambda i: (0, 0)), pl.BlockSpec((1, d), lambda i: (0, 0))]
        out_shape += [jax.ShapeDtypeStruct((1, LANES), F32), jax.ShapeDtypeStruct((1, d), F32)]
    outs, exchanged = _hosted_call(
        body, exchange, name=f"ffn_fwd_{tag}", grid=(t // tm,), in_specs=in_specs, out_specs=out_specs, out_shape=out_shape,
        scratch_shapes=[], semantics=("arbitrary" if has_loss else "parallel",),
        args=(x0, o, w_out, gain, wg, wu, wd) + (tuple(loss_head) if has_loss else ()))
    return (*outs, exchanged)


def _ffn_bwd(dy, x, gain, g, u, wg, wu, wd, tag, exchange=()):
    t, d = x.shape
    f = wd.shape[0]
    tm = min(256, t)

    def body(dy_ref, x_ref, gain_ref, g_ref, u_ref, wg_ref, wu_ref, wd_ref, dx_ref, dg_ref, du_ref, dgain_ref):
        dyv = dy_ref[...]
        da = lax.dot_general(dyv.astype(BF16), wd_ref[...], NT, preferred_element_type=F32)
        gv, uv = g_ref[...].astype(F32), u_ref[...].astype(F32)
        sg = _sigmoid(gv)
        act = gv * sg
        du = (da * act).astype(BF16)
        dg = (da * uv * (sg * (1.0 + gv * (1.0 - sg)))).astype(BF16)
        du_ref[...] = du
        dg_ref[...] = dg
        dh = jnp.dot(dg, wg_ref[...], preferred_element_type=F32) + jnp.dot(du, wu_ref[...], preferred_element_type=F32)
        xv, gain_v = x_ref[...], gain_ref[...]
        dx, xhat = _rms_bwd(dh, xv, gain_v)
        dx_ref[...] = dyv + dx
        _accumulate(dgain_ref, jnp.sum(dh * xhat, axis=0, keepdims=True), pl.program_id(0) == 0)

    row = pl.BlockSpec((tm, d), lambda i: (i, 0))
    wide = pl.BlockSpec((tm, f), lambda i: (i, 0))
    outs, exchanged = _hosted_call(
        body, exchange, name=f"ffn_bwd_{tag}", grid=(t // tm,),
        in_specs=[row, row, _resident((1, d)), wide, wide, _resident(wg.shape), _resident(wu.shape), _resident(wd.shape)],
        out_specs=[row, wide, wide, pl.BlockSpec((1, d), lambda i: (0, 0))],
        out_shape=[jax.ShapeDtypeStruct((t, d), F32), jax.ShapeDtypeStruct((t, f), BF16), jax.ShapeDtypeStruct((t, f), BF16),
                   jax.ShapeDtypeStruct((1, d), F32)],
        scratch_shapes=[], semantics=("arbitrary",), args=(dy, x, gain, g, u, wg, wu, wd))
    return (*outs, exchanged)


def _tn_matmul(a, b, name, into=None, row_block=0, row_blocks=1):
    t, k = a.shape
    n = b.shape[1]
    tk = k // 2 if (k // 2) % LANES == 0 else k
    tt = min(2048, t)
    first = row_block * (k // tk)

    def body(a_ref, b_ref, *rest):
        o_ref, acc_ref = rest[-2:]
        prod = lax.dot_general(a_ref[...].astype(BF16), b_ref[...].astype(BF16), TN, preferred_element_type=F32)
        j = pl.program_id(1)

        @pl.when(j == 0)
        def _():
            acc_ref[...] = prod

        @pl.when(j > 0)
        def _():
            acc_ref[...] += prod

        @pl.when(j == pl.num_programs(1) - 1)
        def _():
            o_ref[...] = acc_ref[...].astype(BF16)

    return pl.pallas_call(
        body, name=name, grid=(k // tk, t // tt),
        in_specs=[pl.BlockSpec((tt, tk), lambda i, j: (j, i)), pl.BlockSpec((tt, n), lambda i, j: (j, 0))]
                 + ([ANY] if into is not None else []),
        out_specs=pl.BlockSpec((tk, n), lambda i, j: (first + i, 0)),
        out_shape=jax.ShapeDtypeStruct((row_blocks * k, n), BF16),
        scratch_shapes=[pltpu.VMEM((tk, n), F32)],
        input_output_aliases={2: 0} if into is not None else {},
        compiler_params=_params("parallel", "arbitrary"),
    )(a, b, *([into] if into is not None else []))


def _attn_out_bwd(dx, w, o, dils, seq, tag, lse=None, sink=None, exchange=()):
    t, d = dx.shape
    ts = _tile_rows(seq)
    bl = t // seq
    ng = len(dils)
    has_sink = sink is not None
    expand = _head_expand().T

    def body(*refs):
        refs = list(refs)
        dx_ref, w_ref, o_ref, e_ref = refs[:4]
        refs = refs[4:]
        lse_ref, sink_ref = (refs.pop(0), refs.pop(0)) if has_sink else (None, None)
        do_refs, dl_refs = refs[:ng], refs[ng:2 * ng]
        refs = refs[2 * ng:]
        dsink_ref = refs.pop(0) if has_sink else None
        dof_ref, dlf_ref = refs
        do = lax.dot_general(dx_ref[...].astype(BF16), w_ref[...], NT, preferred_element_type=F32)
        prod = do * o_ref[...].astype(F32)
        hi = prod.astype(BF16)
        lo = (prod - hi.astype(F32)).astype(BF16)
        e = e_ref[...]
        dl = jnp.dot(hi, e, preferred_element_type=F32) + jnp.dot(lo, e, preferred_element_type=F32)
        for g in range(ng):
            for r, part in enumerate(_split_rows(do, dof_ref, dils[g])):
                do_refs[g][r] = part.astype(BF16)
            for r, part in enumerate(_split_rows(dl, dlf_ref, dils[g])):
                dl_refs[g][r] = part
        if has_sink:
            part = -jnp.exp2(sink_ref[...] - lse_ref[...]) * dl
            _accumulate(dsink_ref, jnp.sum(part, axis=0, keepdims=True), pl.program_id(0) == 0)

    row = pl.BlockSpec((ts, d), lambda i: (i, 0))
    narrow = pl.BlockSpec((ts, LANES), lambda i: (i, 0))
    args = [dx, w, o, expand]
    in_specs = [row, _resident(w.shape), pl.BlockSpec((ts, Q_W), lambda i: (i, 0)), _resident(expand.shape)]
    if has_sink:
        args += [lse, jnp.pad(sink.reshape(1, N_HEADS) * LOG2E, ((0, 0), (0, LANES - N_HEADS)))]
        in_specs += [narrow, _resident((1, LANES))]
    out_specs = [_res_spec(seq, dl, Q_W) for dl in dils] + [_res_spec(seq, dl, LANES) for dl in dils]
    out_shape = ([jax.ShapeDtypeStruct(_res_shape(bl, seq, dl, Q_W), BF16) for dl in dils]
                 + [jax.ShapeDtypeStruct(_res_shape(bl, seq, dl, LANES), F32) for dl in dils])
    if has_sink:
        out_specs.append(pl.BlockSpec((1, LANES), lambda i: (0, 0)))
        out_shape.append(jax.ShapeDtypeStruct((1, LANES), F32))
    outs, exchanged = _hosted_call(
        body, exchange, name=f"attn_out_bwd_{tag}", grid=(t // ts,), in_specs=in_specs, out_specs=out_specs, out_shape=out_shape,
        scratch_shapes=[_stage(ts, Q_W), _stage(ts, LANES)], semantics=("arbitrary" if has_sink else "parallel",), args=args)
    return list(outs[:ng]), list(outs[ng:2 * ng]), (outs[2 * ng] if has_sink else None), exchanged


def _attn_bwd(qkv, do, lse, delta, cos, sin, w, tag, exchange=()):
    shape = qkv.shape
    dil = shape[1]
    rows_all = _seq_view(qkv)
    nseq, length, _ = rows_all.shape
    bq = min(QUERY_BLOCK, length)
    wk = _key_rows(bq, w, length)
    nb = length // bq
    per_step = _per_step(dil, length)

    def body(*refs):
        def sub(i, carry):
            one(*[ref.at[i] for ref in refs[:7]], *refs[7:])
            return carry

        if per_step == 1:
            sub(0, 0)
        else:
            lax.fori_loop(0, per_step, sub, 0)

    def one(qkv_ref, do_ref, lse_ref, dl_ref, cos_ref, sin_ref, dp_ref, kk_ref, vv_ref, dk_ref, dv_ref):
        _swap_halves(qkv_ref, Q_W, kk_ref)
        _swap_halves(qkv_ref, Q_W + KV_W, vv_ref)
        dk_ref[...] = jnp.zeros_like(dk_ref)
        dv_ref[...] = jnp.zeros_like(dv_ref)
        band = _band(bq, wk)
        lo_q = lax.broadcasted_iota(jnp.int32, (bq, LANES), 1) < HEAD_DIM
        hi_q = jnp.logical_not(lo_q)

        def block(i, carry):
            q0, k0 = _window(i, bq, w, wk, length)
            valid = jnp.abs(band + (q0 - k0)) <= w
            rows, krows = pl.ds(q0, bq), pl.ds(k0, wk)
            c, sn = cos_ref[rows, :], -sin_ref[rows, :]
            lse_t, dl_t = lse_ref[rows, :], dl_ref[rows, :]
            for kv in range(N_KV):
                heads = [(kv * GRP + h, h % 2) for h in range(GRP)]
                cols = [slice((kv * 2 + j) * LANES, (kv * 2 + j + 1) * LANES) for j in range(GRP // 2)]
                qp = [qkv_ref[rows, cs] for cs in cols]
                dop = [do_ref[rows, cs] for cs in cols]
                k2 = _pair_operand(qkv_ref, kk_ref, Q_W, kv, krows)
                v2 = _pair_operand(qkv_ref, vv_ref, Q_W + KV_W, kv, krows)
                sc2 = [lax.dot_general(q_, k2, NT, preferred_element_type=F32) for q_ in qp]
                dp2 = [lax.dot_general(d_, v2, NT, preferred_element_type=F32) for d_ in dop]
                sc = [s_[:, half * wk:(half + 1) * wk] for s_ in sc2 for half in range(2)]
                dp = [d_[:, half * wk:(half + 1) * wk] for d_ in dp2 for half in range(2)]
                p = [jnp.exp2(jnp.where(valid, s_, NEG_INF) - _over_keys(lse_t[:, hd:hd + 1], wk))
                     for s_, (hd, _) in zip(sc, heads)]
                ds = [(p_ * (dp_ - _over_keys(dl_t[:, hd:hd + 1], wk))).astype(BF16) for p_, dp_, (hd, _) in zip(p, dp, heads)]
                pb = [p_.astype(BF16) for p_ in p]
                for j in range(GRP // 2):
                    dq = jnp.dot(jnp.concatenate([ds[2 * j], ds[2 * j + 1]], axis=1), k2, preferred_element_type=F32) * SCALE
                    dp_ref[rows, cols[j]] = _rope(dq, c, sn).astype(BF16)
                zero = jnp.zeros((bq, LANES), BF16)
                q4 = jnp.concatenate([jnp.where(lo_q if h % 2 == 0 else hi_q, qp[h // 2], zero) for h in range(GRP)], axis=0)
                do4 = jnp.concatenate([jnp.where(lo_q if h % 2 == 0 else hi_q, dop[h // 2], zero) for h in range(GRP)], axis=0)
                dk_ref[kv, krows, :] += lax.dot_general(jnp.concatenate(ds, axis=0), q4, TN, preferred_element_type=F32)
                dv_ref[kv, krows, :] += lax.dot_general(jnp.concatenate(pb, axis=0), do4, TN, preferred_element_type=F32)
            return carry

        lax.fori_loop(0, nb, block, 0)
        lo = lax.broadcasted_iota(jnp.int32, (length, LANES), 1) < HEAD_DIM
        c, sn = cos_ref[...], -sin_ref[...]
        for ch in range(KV_W // LANES):
            halves = []
            for acc_ref in (dk_ref, dv_ref):
                even, odd = acc_ref[2 * ch], acc_ref[2 * ch + 1]
                even = even + pltpu.roll(even, HEAD_DIM, 1)
                odd = odd + pltpu.roll(odd, HEAD_DIM, 1)
                halves.append(jnp.where(lo, even, odd))
            dp_ref[:, Q_W + ch * LANES:Q_W + (ch + 1) * LANES] = _rope(halves[0] * LN2, c, sn).astype(BF16)
            dp_ref[:, Q_W + KV_W + ch * LANES:Q_W + KV_W + (ch + 1) * LANES] = halves[1].astype(BF16)

    def seq_block(c):
        return pl.BlockSpec((per_step, length, c), lambda i: (i, 0, 0))

    table = pl.BlockSpec((per_step, length, LANES), lambda i: (i % (dil // per_step), 0, 0))
    (out,), exchanged = _hosted_call(
        body, exchange, name=f"attn_bwd_{tag}", grid=(nseq // per_step,),
        in_specs=[seq_block(QKV_W), seq_block(Q_W), seq_block(LANES), seq_block(LANES), table, table],
        out_specs=[seq_block(QKV_W)],
        out_shape=[jax.ShapeDtypeStruct((nseq, length, QKV_W), BF16)],
        scratch_shapes=[pltpu.VMEM((KV_W // LANES, length, LANES), BF16), pltpu.VMEM((KV_W // LANES, length, LANES), BF16),
                        pltpu.VMEM((N_KV, length, LANES), F32), pltpu.VMEM((N_KV, length, LANES), F32)],
        semantics=("parallel",), args=(rows_all, _seq_view(do), _seq_view(lse), _seq_view(delta), cos, sin))
    return out.reshape(shape), exchanged


def _qkv_bwd(dy, x, gain, w, dps, dils, seq, tag, exchange=()):
    t, d = x.shape
    ts = _tile_rows(seq)
    ng = len(dps)

    def body(dy_ref, x_ref, gain_ref, w_ref, *refs):
        dp_refs, (dx_ref, dgain_ref, stage_ref) = refs[:ng], refs[ng:]
        dh = None
        for gi in range(ng):
            dil = dils[gi]
            n = ts // dil
            dp = dp_refs[gi][0] if dil == 1 else jnp.concatenate([dp_refs[gi][r] for r in range(dil)], axis=0)
            part = jnp.dot(dp, w_ref[gi * QKV_W:(gi + 1) * QKV_W, :], preferred_element_type=F32)
            part = _merge_rows([part[r * n:(r + 1) * n] for r in range(dil)], stage_ref, dil)
            dh = part if dh is None else dh + part
        xv, gain_v = x_ref[...], gain_ref[...]
        dx, xhat = _rms_bwd(dh, xv, gain_v)
        dx_ref[...] = dy_ref[...] + dx
        _accumulate(dgain_ref, jnp.sum(dh * xhat, axis=0, keepdims=True), pl.program_id(0) == 0)

    row = pl.BlockSpec((ts, d), lambda i: (i, 0))
    (dx, dgain), exchanged = _hosted_call(
        body, exchange, name=f"qkv_bwd_{tag}", grid=(t // ts,),
        in_specs=[row, row, _resident((1, d)), _resident(w.shape)] + [_res_spec(seq, dl, QKV_W) for dl in dils],
        out_specs=[row, pl.BlockSpec((1, d), lambda i: (0, 0))],
        out_shape=[jax.ShapeDtypeStruct((t, d), F32), jax.ShapeDtypeStruct((1, d), F32)],
        scratch_shapes=[_stage(ts, d)], semantics=("arbitrary",), args=(dy, x, gain, w, *dps))
    return dx, dgain, exchanged


ANY = pl.BlockSpec(memory_space=pl.ANY)


def _place():
    x, y, c = lax.axis_index("x"), lax.axis_index("y"), lax.axis_index("c")
    return x, y, c


def _exchange_steps(srcs, dsts, gather, send_sems, recv_sems, local_sems):
    x, y, c = _place()
    me, sibling = (x, y, c), (x, y, 1 - c)
    chips = [(1 - x, y), (x, 1 - y), (1 - x, 1 - y)]
    mine = 4 * x + 2 * y + c

    def slot(a, device):
        px, py, pc = device
        return dsts[a].at[4 * px + 2 * py + pc]

    def passes(a, k, block, to, src=None):
        rows = slot(a, block)
        return pltpu.make_async_remote_copy(src_ref=rows if src is None else src, dst_ref=rows, send_sem=send_sems.at[a, k],
                                            recv_sem=recv_sems.at[a, k], device_id=to, device_id_type=MESH)

    def scatters(a, k):
        peer = mine ^ k
        return pltpu.make_async_remote_copy(
            src_ref=srcs[a].at[peer], dst_ref=dsts[a].at[mine], send_sem=send_sems.at[a, k - 1], recv_sem=recv_sems.at[a, k - 1],
            device_id=(peer // 4, (peer // 2) % 2, peer % 2), device_id_type=MESH)

    def local(a):
        return pltpu.make_async_copy(srcs[a] if gather[a] else srcs[a].at[mine], dsts[a].at[mine], local_sems.at[a])

    def first_copies(a):
        if not gather[a]:
            return [scatters(a, k) for k in range(1, N_DEV)]
        return [passes(a, 0, me, sibling, src=srcs[a])] + [passes(a, 1 + j, me, (*chip, c), src=srcs[a]) for j, chip in enumerate(chips)]

    def start():
        for a in range(len(srcs)):
            local(a).start()
            for cp in first_copies(a):
                cp.start()

    def forward():
        for a in range(len(srcs)):
            if gather[a]:
                for j, chip in enumerate(chips):
                    passes(a, 1 + j, (*chip, c), me).wait_recv()
                    passes(a, 4 + j, (*chip, c), sibling).start()

    def finish():
        for a in range(len(srcs)):
            if gather[a]:
                passes(a, 0, sibling, me).wait_recv()
                for j, chip in enumerate(chips):
                    passes(a, 4 + j, (*chip, 1 - c), me).wait_recv()
                    passes(a, 4 + j, (*chip, c), sibling).wait_send()
                for cp in first_copies(a):
                    cp.wait_send()
            else:
                for cp in first_copies(a):
                    cp.wait()
            local(a).wait()

    return start, forward, finish


def _exchange_scratch(n):
    return [pltpu.SemaphoreType.DMA((n, N_DEV - 1)), pltpu.SemaphoreType.DMA((n, N_DEV - 1)), pltpu.SemaphoreType.DMA((n,))]


def _exchanged_shapes(exchange):
    return [jax.ShapeDtypeStruct(((N_DEV,) + a.shape) if g else a.shape, a.dtype) for a, g in exchange]


def _hosted_call(body, exchange, *, name, grid, in_specs, out_specs, out_shape, scratch_shapes, semantics, args):
    out_specs, out_shape, scratch = list(out_specs), list(out_shape), list(scratch_shapes)
    if not exchange:
        outs = pl.pallas_call(body, name=name, grid=grid, in_specs=in_specs, out_specs=out_specs, out_shape=out_shape,
                              scratch_shapes=scratch, compiler_params=_params(*semantics))(*args)
        return list(outs), []
    n, n_in, n_out, n_scr = len(exchange), len(in_specs), len(out_specs), len(scratch)
    gather = [g for _, g in exchange]
    steps = math.prod(grid)

    def hosted(*refs):
        own_in, x_in = refs[:n_in], refs[n_in:n_in + n]
        own_out, x_out = refs[n_in + n:n_in + n + n_out], refs[n_in + n + n_out:n_in + 2 * n + n_out]
        own_scr, sems = refs[n_in + 2 * n + n_out:n_in + 2 * n + n_out + n_scr], refs[-3:]
        step = pl.program_id(0)
        for axis in range(1, len(grid)):
            step = step * grid[axis] + pl.program_id(axis)
        start, forward, finish = _exchange_steps(x_in, x_out, gather, *sems)
        pl.when(step == 0)(start)
        body(*own_in, *own_out, *own_scr)
        pl.when(step == steps // 2)(forward)
        pl.when(step == steps - 1)(finish)

    outs = pl.pallas_call(
        hosted, name=name, grid=grid, in_specs=list(in_specs) + [ANY] * n, out_specs=out_specs + [ANY] * n,
        out_shape=out_shape + _exchanged_shapes(exchange), scratch_shapes=scratch + _exchange_scratch(n),
        compiler_params=_params(*["arbitrary"] * len(grid)),
    )(*args, *[a for a, _ in exchange])
    return list(outs[:n_out]), list(outs[n_out:])


def _exchange_now(exchange, name):
    n = len(exchange)
    gather = [g for _, g in exchange]

    def body(*refs):
        for step in _exchange_steps(refs[:n], refs[n:2 * n], gather, *refs[2 * n:]):
            step()

    return pl.pallas_call(
        body, name=name, in_specs=[ANY] * n, out_specs=[ANY] * n, out_shape=_exchanged_shapes(exchange),
        scratch_shapes=_exchange_scratch(n),
    )(*[a for a, _ in exchange])


def _all_reduce_small(v):
    def body(v_ref, o_ref, recv_ref, send_sems, recv_sems):
        x, y, c = _place()
        me = 4 * x + 2 * y + c
        copies = []
        for k in range(1, N_DEV):
            peer = me ^ k
            copies.append(pltpu.make_async_remote_copy(
                src_ref=v_ref, dst_ref=recv_ref.at[k], send_sem=send_sems.at[k - 1], recv_sem=recv_sems.at[k - 1],
                device_id=(peer // 4, (peer // 2) % 2, peer % 2), device_id_type=MESH))
        for cp in copies:
            cp.start()
        recv_ref[0] = v_ref[...]
        for cp in copies:
            cp.wait()
        acc = recv_ref[me]
        for src in range(1, N_DEV):
            acc = acc + recv_ref[me ^ src]
        o_ref[...] = acc

    vm = pl.BlockSpec(memory_space=pltpu.VMEM)
    return pl.pallas_call(
        body, name="all_reduce_small", in_specs=[vm], out_specs=vm, out_shape=jax.ShapeDtypeStruct(v.shape, F32),
        scratch_shapes=[pltpu.VMEM((N_DEV,) + v.shape, F32), pltpu.SemaphoreType.DMA((N_DEV - 1,)),
                        pltpu.SemaphoreType.DMA((N_DEV - 1,))],
    )(v)


def _adamw_math(w, g, m, v):
    m = ADAM_B1 * m + (1.0 - ADAM_B1) * g
    v = ADAM_B2 * v + (1.0 - ADAM_B2) * (g * g)
    m_hat = m / (1.0 - ADAM_B1 ** ADAM_STEP)
    v_hat = v / (1.0 - ADAM_B2 ** ADAM_STEP)
    delta = -ADAM_LR * (m_hat / (jnp.sqrt(v_hat) + ADAM_EPS) + ADAM_WD * w)
    return delta, m, v


def _adamw(parts, w, m, v, name, layer=None, into=None):
    r, c = w.shape[-2:]
    tr = r // 2 if r % 16 == 0 and r >= 256 else r
    n = len(parts)

    def body(*refs):
        w_ref, m_ref, v_ref = refs[n:n + 3]
        g_ref, d_ref, nm_ref, nv_ref = refs[-4:]
        g = refs[0][...].astype(F32)
        for p_ref in refs[1:n]:
            g = g + p_ref[...].astype(F32)
        g_ref[...] = g
        d_ref[...], nm_ref[...], nv_ref[...] = _adamw_math(w_ref[...], g, m_ref[...], v_ref[...])

    def slab(slot):
        return pl.BlockSpec((None, tr, c), lambda i: (slot, i, 0))

    tile = pl.BlockSpec((tr, c), lambda i: (i, 0)) if layer is None else slab(layer)
    arrays, in_specs = [], []
    for p in parts:
        if isinstance(p, tuple):
            arrays.append(p[0])
            in_specs.append(slab(p[1]))
        else:
            arrays.append(p)
            in_specs.append(tile)
    kept = list(into) if into is not None else []
    return pl.pallas_call(
        body, name=name, grid=(r // tr,), in_specs=in_specs + [tile] * 3 + [ANY] * len(kept), out_specs=[tile] * 4,
        out_shape=[jax.ShapeDtypeStruct(w.shape, F32)] * 4,
        input_output_aliases={n + 3 + k: k for k in range(len(kept))}, compiler_params=_params("parallel"),
    )(*arrays, w, m, v, *kept)


def _rows(g):
    return g.reshape(-1, g.shape[-1])


def _row_blocks(dw):
    k, n = dw.shape
    return dw.reshape(N_DEV, k // N_DEV, n)


def _pack_rows(rows, width):
    out = None
    for i, r in enumerate(rows):
        r = r.reshape(1, -1).astype(F32)
        r = jnp.pad(r, ((i, 8 - 1 - i), (0, width - r.shape[1])))
        out = r if out is None else out + r
    return out


def _mixer_fwd(x, gain, w_in, cos, sin, seq, groups, tag, sink=None, exchanges=None):
    exchanges = exchanges or {}
    os, lses, got = [], [], {}
    qkvs, hs, got["proj"] = _qkv_proj(x, gain, w_in, cos, sin, seq, [dil for dil, _ in groups], tag,
                                      exchange=exchanges.get("proj", ()))
    for gi, (dil, w) in enumerate(groups):
        o, lse, got[gi] = _attn_fwd(qkvs[gi], w, f"{tag}{gi}", sink=sink, exchange=exchanges.get(gi, ()))
        os.append(o)
        lses.append(lse)
    o, lses = _mix_groups(os, lses, [dl for dl, _ in groups], seq, tag)
    return (qkvs, hs, o, lses), got


def _mixer_bwd(dy, x_in, gain, w_in, w_out, saved, cos, sin, seq, groups, tag, sink=None, exchanges=None, scatter_own=False):
    qkvs, hs, o, lses = saved
    t, d = x_in.shape
    dils = [dl for dl, _ in groups]
    lse_tokens = lses[0].reshape(t, LANES) if sink is not None else None
    dw_out = _tn_matmul(o, dy, f"dw_out_{tag}")
    dos, dls, dsink, early = _attn_out_bwd(dy, w_out, o, dils, seq, tag, lse=lse_tokens, sink=sink,
                                           exchange=_to_send([dw_out]) if scatter_own else ())
    if scatter_own:
        (dw_out,) = early
    exchanges = exchanges or {}
    dps, got = [], {}
    for gi, (dil, w) in enumerate(groups):
        dp, got[gi] = _attn_bwd(qkvs[gi], dos[gi], lses[gi], dls[gi], _tables_by_residue(cos, seq, dil),
                                _tables_by_residue(sin, seq, dil), w, f"{tag}{gi}", exchange=exchanges.get(gi, ()))
        dps.append(dp)
    dw_in = None
    for gi in range(len(groups)):
        dw_in = _tn_matmul(dps[gi].reshape(t, QKV_W), hs[gi].reshape(t, d), f"dw_in_{tag}{gi}", into=dw_in, row_block=gi,
                           row_blocks=len(groups))
    dx, dgain, late = _qkv_bwd(dy, x_in, gain, w_in, dps, dils, seq, tag, exchange=_to_send([dw_in]) if scatter_own else ())
    if scatter_own:
        (dw_in,) = late
    return dx, dw_in, dw_out, dgain, dsink, got


def _ffn_layer_bwd(dy, x_in, gain, saved, wg, wu, wd, tag, exchange=()):
    g, u, act, h = saved
    dx, dg, du, dgain, got = _ffn_bwd(dy, x_in, gain, g, u, wg, wu, wd, tag, exchange=exchange)
    dwd = _tn_matmul(act, dy, f"dw_down_{tag}")
    dwg = _tn_matmul(dg, h, f"dw_gate_{tag}")
    dwu = _tn_matmul(du, h, f"dw_up_{tag}")
    return dx, dwg, dwu, dwd, dgain, got


def _to_send(dws):
    return [(_row_blocks(g), False) for g in dws]


def kernel(x, a_w_in, a_sink, a_w_out, b_w_in, b_w_out, norm_mix, norm_ffn, w_gate, w_up, w_down, final_norm, loss_target, m_a_w_in, m_a_sink, m_a_w_out, m_b_w_in, m_b_w_out, m_norm_mix, m_norm_ffn, m_w_gate, m_w_up, m_w_down, m_final_norm, v_a_w_in, v_a_sink, v_a_w_out, v_b_w_in, v_b_w_out, v_norm_mix, v_norm_ffn, v_w_gate, v_w_up, v_w_down, v_final_norm):
    bl, seq, d = x.shape
    t = bl * seq
    xf = x.reshape(t, d)
    target = loss_target.reshape(t, d)
    cos, sin = _rope_tables(seq)
    groups_a = [(1, ATTN_HALF_WINDOW)]
    groups_b = [(dil, window // 2 // dil) for window, dil in DILATED_GROUPS]

    def flip(w_):
        return jnp.swapaxes(w_, -1, -2)

    a_w_in, m_a_w_in, v_a_w_in, b_w_in, m_b_w_in, v_b_w_in = map(flip, (a_w_in, m_a_w_in, v_a_w_in, b_w_in, m_b_w_in, v_b_w_in))
    w_gate, m_w_gate, v_w_gate, w_up, m_w_up, v_w_up = map(flip, (w_gate, m_w_gate, v_w_gate, w_up, m_w_up, v_w_up))

    def shard(w_, layer):
        return (w_[layer].astype(BF16), True)

    (wa_in,) = map(_rows, _exchange_now([shard(a_w_in, 0)], "gather_first"))

    saved_a, got = _mixer_fwd(xf, norm_mix[0:1], wa_in, cos, sin, seq, groups_a, "a", sink=a_sink[0],
                              exchanges={"proj": [shard(w_down, 0)], 0: [shard(w_gate, 0), shard(w_up, 0), shard(a_w_out, 0)]})
    wg0, wu0, wa_out, wd0 = map(_rows, got[0] + got["proj"])
    x1_0, x2_0, *saved_0, got = _ffn_fwd(xf, saved_a[2], wa_out, norm_ffn[0:1], wg0, wu0, wd0, "0",
                                         exchange=[shard(b_w_in, 0), shard(b_w_out, 0)])
    wb_in, wb_out = map(_rows, got)
    saved_b, got = _mixer_fwd(x2_0, norm_mix[1:2], wb_in, cos, sin, seq, groups_b, "b",
                              exchanges={0: [shard(w_gate, 1)], 1: [shard(w_up, 1)], 2: [shard(w_down, 1)]})
    wg1, wu1, wd1 = map(_rows, got[0] + got[1] + got[2])
    x1_1, dy, *saved_1, loss_part, d_final, _ = _ffn_fwd(x2_0, saved_b[2], wb_out, norm_ffn[1:2], wg1, wu1, wd1, "1",
                                                         loss_head=(final_norm.reshape(1, d), target))

    dy, dwg1, dwu1, dwd1, d_nf1, _ = _ffn_layer_bwd(dy, x1_1, norm_ffn[1:2], saved_1, wg1, wu1, wd1, "1")
    dy, dwb_in, dwb_out, d_nm1, _, got = _mixer_bwd(
        dy, x2_0, norm_mix[1:2], wb_in, wb_out, saved_b, cos, sin, seq, groups_b, "b",
        exchanges={0: _to_send([dwg1, dwd1]), 1: _to_send([dwu1])})
    (r_g1, r_d1), (r_u1,) = got[0], got[1]
    dy, dwg0, dwu0, dwd0, d_nf0, (r_b_in, r_b_out) = _ffn_layer_bwd(
        dy, x1_0, norm_ffn[0:1], saved_0, wg0, wu0, wd0, "0", exchange=_to_send([dwb_in, dwb_out]))
    dy, r_a_in, r_a_out, d_nm0, d_sink, got = _mixer_bwd(
        dy, xf, norm_mix[0:1], wa_in, wa_out, saved_a, cos, sin, seq, groups_a, "a", sink=a_sink[0],
        exchanges={0: _to_send([dwg0, dwu0, dwd0])}, scatter_own=True)
    r_g0, r_u0, r_d0 = got[0]
    grad_x = dy.reshape(bl, seq, d)

    def update(received, w_, m_, v_, name):
        out = None
        for layer in reversed(range(len(received))):
            out = _adamw([(received[layer], src) for src in range(N_DEV)], w_, m_, v_, f"adamw_{name}{layer}", layer=layer, into=out)
        return out

    u_a_in = update([r_a_in], a_w_in, m_a_w_in, v_a_w_in, "a_in")
    u_a_out = update([r_a_out], a_w_out, m_a_w_out, v_a_w_out, "a_out")
    u_b_in = update([r_b_in], b_w_in, m_b_w_in, v_b_w_in, "b_in")
    u_b_out = update([r_b_out], b_w_out, m_b_w_out, v_b_w_out, "b_out")
    u_gate = update([r_g0, r_g1], w_gate, m_w_gate, v_w_gate, "gate")
    u_up = update([r_u0, r_u1], w_up, m_w_up, v_w_up, "up")
    u_down = update([r_d0, r_d1], w_down, m_w_down, v_w_down, "down")

    small = _pack_rows([d_nm0, d_nm1, d_nf0, d_nf1, d_final, d_sink, loss_part], d)
    total = _all_reduce_small(small)
    small_w = _pack_rows([norm_mix[0], norm_mix[1], norm_ffn[0], norm_ffn[1], final_norm, a_sink], d)
    small_m = _pack_rows([m_norm_mix[0], m_norm_mix[1], m_norm_ffn[0], m_norm_ffn[1], m_final_norm, m_a_sink], d)
    small_v = _pack_rows([v_norm_mix[0], v_norm_mix[1], v_norm_ffn[0], v_norm_ffn[1], v_final_norm, v_a_sink], d)
    u_small = _adamw([total], small_w, small_m, small_v, "adamw_small")
    loss = total[6, 0]

    outs = []
    for k in range(4):
        sm = u_small[k]
        outs += [flip(u_a_in[k]), sm[5:6, :N_HEADS], u_a_out[k], flip(u_b_in[k]), u_b_out[k], sm[0:2], sm[2:4],
                 flip(u_gate[k]), flip(u_up[k]), u_down[k], sm[4]]
    return (loss, grad_x, *outs)
```

```python
import functools
import math

import jax
import jax.numpy as jnp
from jax import lax
from jax.experimental import pallas as pl
from jax.experimental.pallas import tpu as pltpu

F32 = jnp.float32
BF16 = jnp.bfloat16

HEAD_DIM = 64
N_HEADS = 16
N_KV = 4
GRP = N_HEADS // N_KV
Q_W = N_HEADS * HEAD_DIM
KV_W = N_KV * HEAD_DIM
QKV_W = Q_W + 2 * KV_W
ATTN_HALF_WINDOW = 128
DILATED_GROUPS = ((128, 1), (512, 4), (2048, 16))
ROPE_THETA = 10000.0
RMS_EPS = 1e-6
NEG_INF = -1e30
SCALE = 1.0 / math.sqrt(HEAD_DIM)
LOG2E = 1.0 / math.log(2.0)
LN2 = math.log(2.0)

ADAM_LR = 0.001
ADAM_B1 = 0.9
ADAM_B2 = 0.999
ADAM_EPS = 1e-08
ADAM_WD = 0.01
ADAM_STEP = 10

LANES = 128
VMEM_LIMIT = 56 * 1024 * 1024
QUERY_BLOCK = 128
PAIRS_PER_PHASE = 4
N_DEV = 8
MESH = pl.DeviceIdType.MESH

NT = (((1,), (1,)), ((), ()))
TN = (((0,), (0,)), ((), ()))


def _params(*sem):
    return pltpu.CompilerParams(dimension_semantics=tuple(sem) if sem else None, vmem_limit_bytes=VMEM_LIMIT)


def _resident(shape):
    return pl.BlockSpec(shape, lambda *_: (0,) * len(shape), pipeline_mode=pl.Buffered(1))


def _rope_tables(seq):
    inv_freq = 1.0 / (ROPE_THETA ** (jnp.arange(0, HEAD_DIM, 2, dtype=F32) / HEAD_DIM))
    ang = jnp.arange(seq, dtype=F32)[:, None] * inv_freq[None, :]
    cos, sin = jnp.cos(ang), jnp.sin(ang)
    return jnp.tile(cos, (1, 4)), jnp.concatenate([-sin, sin, -sin, sin], axis=1)


def _rope(t, cos, sin_signed):
    lane = lax.broadcasted_iota(jnp.int32, t.shape, 1)
    first = (lane & (HEAD_DIM // 2)) == 0
    swapped = jnp.where(first, pltpu.roll(t, LANES - HEAD_DIM // 2, 1), pltpu.roll(t, HEAD_DIM // 2, 1))
    return t * cos + swapped * sin_signed


def _rms(x):
    return lax.rsqrt(jnp.mean(x * x, axis=-1, keepdims=True) + RMS_EPS)


def _rms_bwd(dh, x, gain):
    r = _rms(x)
    xhat = x * r
    dxh = dh * gain
    dx = r * (dxh - xhat * jnp.mean(dxh * xhat, axis=-1, keepdims=True))
    return dx, xhat


def _accumulate(ref, value, first):
    @pl.when(first)
    def _():
        ref[...] = jnp.zeros_like(ref)

    ref[...] += value


def _tile_rows(seq):
    return min(512, seq)


def _res_shape(bl, seq, dil, c):
    ts = _tile_rows(seq)
    return (bl, dil, seq // ts, ts // dil, c)


def _res_spec(seq, dil, c):
    ts = _tile_rows(seq)
    per_seq = seq // ts
    return pl.BlockSpec((None, dil, None, ts // dil, c), lambda i: (i // per_seq, 0, i % per_seq, 0, 0))


def _seq_view(a):
    bl, dil, tiles, n, c = a.shape
    return a.reshape(bl * dil, tiles * n, c)


def _stage(ts, c):
    return pltpu.VMEM((c // LANES, ts, LANES), F32)


def _split_rows(val, stage_ref, dil):
    if dil == 1:
        return [val]
    ts, c = val.shape
    n, nc = ts // dil, c // LANES
    for k in range(nc):
        stage_ref[k] = val[:, k * LANES:(k + 1) * LANES]
    return [jnp.concatenate([stage_ref[k, pl.ds(r, n, stride=dil), :] for k in range(nc)], axis=1) for r in range(dil)]


def _merge_rows(parts, stage_ref, dil):
    if dil == 1:
        return parts[0]
    n, c = parts[0].shape
    nc = c // LANES
    for r, part in enumerate(parts):
        for k in range(nc):
            stage_ref[k, pl.ds(r, n, stride=dil), :] = part[:, k * LANES:(k + 1) * LANES]
    return jnp.concatenate([stage_ref[k] for k in range(nc)], axis=1)


def _tables_tiled(table, seq, dil):
    ts = _tile_rows(seq)
    return table.reshape(seq // ts, ts // dil, dil, LANES).transpose(0, 2, 1, 3).reshape(seq, LANES)


def _tables_by_residue(table, seq, dil):
    return table.reshape(seq // dil, dil, LANES).transpose(1, 0, 2)


def _qkv_proj(x, gain, w, cos, sin, seq, dils, tag, exchange=()):
    t, d = x.shape
    ts = _tile_rows(seq)
    per_seq = seq // ts
    ng = len(dils)
    tables = [t_ for dil in dils for t_ in (_tables_tiled(cos, seq, dil), _tables_tiled(sin, seq, dil))]

    def body(x_ref, g_ref, w_ref, *refs):
        table_refs, o_refs, h_refs, stage_ref = refs[:2 * ng], refs[2 * ng:3 * ng], refs[3 * ng:4 * ng], refs[4 * ng]
        xv = x_ref[...]
        h_tokens = xv * _rms(xv) * g_ref[...]
        for gi, dil in enumerate(dils):
            n = ts // dil
            h = jnp.concatenate(_split_rows(h_tokens, stage_ref, dil), axis=0).astype(BF16)
            for r in range(dil):
                h_refs[gi][r] = h[r * n:(r + 1) * n]
            acc = lax.dot_general(h, w_ref[gi * QKV_W:(gi + 1) * QKV_W, :], NT, preferred_element_type=F32)
            c, s = table_refs[2 * gi][...], table_refs[2 * gi + 1][...]
            for j in range(QKV_W // LANES):
                cols = slice(j * LANES, (j + 1) * LANES)
                val = acc[:, cols]
                if j < (Q_W + KV_W) // LANES:
                    val = _rope(val, c, s)
                if j < Q_W // LANES:
                    val = val * (SCALE * LOG2E)
                val = val.astype(BF16)
                for r in range(dil):
                    o_refs[gi][r, :, cols] = val[r * n:(r + 1) * n]

    table = pl.BlockSpec((ts, LANES), lambda i: (i % per_seq, 0))
    outs, exchanged = _hosted_call(
        body, exchange, name=f"qkv_proj_{tag}", grid=(t // ts,),
        in_specs=[pl.BlockSpec((ts, d), lambda i: (i, 0)), _resident((1, d)), _resident(w.shape)] + [table] * (2 * ng),
        out_specs=[_res_spec(seq, dil, QKV_W) for dil in dils] + [_res_spec(seq, dil, d) for dil in dils],
        out_shape=[jax.ShapeDtypeStruct(_res_shape(t // seq, seq, dil, QKV_W), BF16) for dil in dils]
                  + [jax.ShapeDtypeStruct(_res_shape(t // seq, seq, dil, d), BF16) for dil in dils],
        scratch_shapes=[_stage(ts, d)], semantics=("parallel",), args=(x, gain, w, *tables))
    return outs[:ng], outs[ng:], exchanged


def _band(bq, wk):
    return lax.broadcasted_iota(jnp.int32, (bq, wk), 0) - lax.broadcasted_iota(jnp.int32, (bq, wk), 1)


def _swap_halves(src_ref, base, dst_ref):
    for c in range(KV_W // LANES):
        dst_ref[c] = pltpu.roll(src_ref[:, base + c * LANES:base + (c + 1) * LANES], HEAD_DIM, 1)


def _pair_operand(src_ref, swapped_ref, base, kv, rows):
    c = kv // 2
    chunk, swapped = src_ref[rows, base + c * LANES:base + (c + 1) * LANES], swapped_ref[c, rows, :]
    lo = lax.broadcasted_iota(jnp.int32, chunk.shape, 1) < HEAD_DIM
    zero = jnp.zeros_like(chunk)
    if kv % 2 == 0:
        return jnp.concatenate([jnp.where(lo, chunk, zero), jnp.where(lo, zero, swapped)], axis=0)
    return jnp.concatenate([jnp.where(lo, swapped, zero), jnp.where(lo, zero, chunk)], axis=0)


def _over_keys(col, wk):
    if wk % LANES:
        return jnp.broadcast_to(col, (col.shape[0], wk))
    wide = jnp.broadcast_to(col, (col.shape[0], LANES))
    return wide if wk == LANES else jnp.concatenate([wide] * (wk // LANES), axis=1)


def _per_step(dil, length):
    return max(1, min(dil, 512 // length))


def _key_rows(bq, w, length):
    return min(bq + 2 * w, length)


def _window(i, bq, w, wk, length):
    q0 = pl.multiple_of(i * bq, bq)
    k0 = pl.multiple_of(jnp.clip(q0 - w, 0, length - wk), min(w, bq))
    return q0, k0


def _attn_fwd(qkv, w, tag, sink=None, exchange=()):
    shape = qkv.shape
    rows_all = _seq_view(qkv)
    nseq, length, _ = rows_all.shape
    bq = min(QUERY_BLOCK, length)
    wk = _key_rows(bq, w, length)
    nb = length // bq
    has_sink = sink is not None
    per_step = _per_step(shape[1], length)

    def body(*refs):
        sink_ref = refs[1] if has_sink else None
        kk_ref, vv_ref = refs[-2:]
        for sub in range(per_step):
            one(refs[0].at[sub], refs[-4].at[sub], refs[-3].at[sub], sink_ref, kk_ref, vv_ref)

    def one(qkv_ref, o_ref, lse_ref, sink_ref, kk_ref, vv_ref):
        _swap_halves(qkv_ref, Q_W, kk_ref)
        _swap_halves(qkv_ref, Q_W + KV_W, vv_ref)
        band = _band(bq, wk)
        lane = lax.broadcasted_iota(jnp.int32, (bq, LANES), 1)
        lo = lane < HEAD_DIM

        def block(i, carry):
            q0, k0 = _window(i, bq, w, wk, length)
            valid = jnp.abs(band + (q0 - k0)) <= w
            rows, krows = pl.ds(q0, bq), pl.ds(k0, wk)
            lse_tile = jnp.zeros((bq, LANES), F32)
            for first in range(0, N_HEADS // 2, PAIRS_PER_PHASE):
                pairs = range(first, first + PAIRS_PER_PHASE)
                heads = [2 * j + half for j in pairs for half in range(2)]
                qp = [qkv_ref[rows, j * LANES:(j + 1) * LANES] for j in pairs]
                k2 = {kv: _pair_operand(qkv_ref, kk_ref, Q_W, kv, krows) for kv in {j // 2 for j in pairs}}
                v2 = {kv: _pair_operand(qkv_ref, vv_ref, Q_W + KV_W, kv, krows) for kv in k2}
                sc2 = [lax.dot_general(q_, k2[j // 2], NT, preferred_element_type=F32) for q_, j in zip(qp, pairs)]
                sc = [jnp.where(valid, s_[:, half * wk:(half + 1) * wk], NEG_INF) for s_ in sc2 for half in range(2)]
                m = [jnp.max(s_, axis=-1, keepdims=True) for s_ in sc]
                if has_sink:
                    m = [jnp.maximum(m_, sink_ref[hd]) for m_, hd in zip(m, heads)]
                mb = [jnp.broadcast_to(m_, (bq, LANES)) for m_ in m]
                p = [jnp.exp2(s_ - _over_keys(m_, wk)) for s_, m_ in zip(sc, m)]
                den = [jnp.sum(p_, axis=-1, keepdims=True) for p_ in p]
                if has_sink:
                    den = [d_ + jnp.exp2(sink_ref[hd] - m_) for d_, m_, hd in zip(den, m, heads)]
                inv = [jnp.broadcast_to(1.0 / d_, (bq, LANES)) for d_ in den]
                pb = [p_.astype(BF16) for p_ in p]
                for n_, j in enumerate(pairs):
                    o = jnp.dot(jnp.concatenate([pb[2 * n_], pb[2 * n_ + 1]], axis=1), v2[j // 2], preferred_element_type=F32)
                    o = o * jnp.where(lo, inv[2 * n_], inv[2 * n_ + 1])
                    o_ref[rows, j * LANES:(j + 1) * LANES] = o.astype(BF16)
                for n_, hd in enumerate(heads):
                    lse_tile = jnp.where(lane == hd, mb[n_] - jnp.log(inv[n_]) * LOG2E, lse_tile)
            lse_ref[rows, :] = lse_tile
            return carry

        lax.fori_loop(0, nb, block, 0)

    def seq_block(c):
        return pl.BlockSpec((per_step, length, c), lambda i: (i, 0, 0))

    args = [rows_all]
    in_specs = [seq_block(QKV_W)]
    if has_sink:
        args.append(sink * LOG2E)
        in_specs.append(pl.BlockSpec(memory_space=pltpu.SMEM))
    (o, lse), exchanged = _hosted_call(
        body, exchange, name=f"attn_fwd_{tag}", grid=(nseq // per_step,), in_specs=in_specs,
        out_specs=[seq_block(Q_W), seq_block(LANES)],
        out_shape=[jax.ShapeDtypeStruct((nseq, length, Q_W), BF16), jax.ShapeDtypeStruct((nseq, length, LANES), F32)],
        scratch_shapes=[pltpu.VMEM((KV_W // LANES, length, LANES), BF16), pltpu.VMEM((KV_W // LANES, length, LANES), BF16)],
        semantics=("parallel",), args=args)
    return o.reshape(shape[:-1] + (Q_W,)), lse.reshape(shape[:-1] + (LANES,)), exchanged


def _head_expand():
    return (jnp.arange(LANES)[:, None] == jnp.arange(Q_W)[None, :] // HEAD_DIM).astype(BF16)


def _mix_groups(os, lses, dils, seq, tag):
    bl = os[0].shape[0]
    ts = _tile_rows(seq)
    t = bl * seq
    ng = len(os)
    if ng == 1 and dils[0] == 1:
        return os[0].reshape(t, Q_W), [lses[0]]

    def body(*refs):
        e_ref = refs[0]
        o_refs, l_refs = refs[1:1 + ng], refs[1 + ng:1 + 2 * ng]
        om_ref = refs[1 + 2 * ng]
        lt_refs = refs[2 + 2 * ng:2 + 3 * ng]
        wide_ref, narrow_ref = refs[2 + 3 * ng:]
        ls = [_merge_rows([l_refs[g][r] for r in range(dils[g])], narrow_ref, dils[g]) for g in range(ng)]
        mx = functools.reduce(jnp.maximum, ls)
        tot = mx + jnp.log(functools.reduce(lambda a, b: a + b, [jnp.exp2(l_ - mx) for l_ in ls])) * LOG2E
        e = e_ref[...]
        o = None
        for g in range(ng):
            wt = jnp.exp2(ls[g] - tot)
            hi = wt.astype(BF16)
            lo = (wt - hi.astype(F32)).astype(BF16)
            wide = jnp.dot(hi, e, preferred_element_type=F32) + jnp.dot(lo, e, preferred_element_type=F32)
            term = wide * _merge_rows([o_refs[g][r].astype(F32) for r in range(dils[g])], wide_ref, dils[g])
            o = term if o is None else o + term
        om_ref[...] = o.astype(BF16)
        for g in range(ng):
            for r, part in enumerate(_split_rows(tot, narrow_ref, dils[g])):
                lt_refs[g][r] = part

    e = _head_expand()
    outs = pl.pallas_call(
        body, name=f"mix_groups_{tag}", grid=(t // ts,),
        in_specs=[_resident(e.shape)] + [_res_spec(seq, dl, Q_W) for dl in dils] + [_res_spec(seq, dl, LANES) for dl in dils],
        out_specs=[pl.BlockSpec((ts, Q_W), lambda i: (i, 0))] + [_res_spec(seq, dl, LANES) for dl in dils],
        out_shape=[jax.ShapeDtypeStruct((t, Q_W), BF16)]
                  + [jax.ShapeDtypeStruct(_res_shape(bl, seq, dl, LANES), F32) for dl in dils],
        scratch_shapes=[_stage(ts, Q_W), _stage(ts, LANES)],
        compiler_params=_params("parallel"),
    )(e, *os, *lses)
    return outs[0], list(outs[1:])


def _sigmoid(g):
    return 1.0 / (1.0 + jnp.exp(-g))


def _ffn_fwd(x0, o, w_out, gain, wg, wu, wd, tag, exchange=(), loss_head=None):
    t, d = x0.shape
    f = wd.shape[0]
    tm = min(256, t)
    has_loss = loss_head is not None

    def body(*refs):
        x0_ref, o_ref, wo_ref, gain_ref, wg_ref, wu_ref, wd_ref = refs[:7]
        x_ref, y_ref, g_ref, u_ref, a_ref, h_ref = refs[-8:-2] if has_loss else refs[-6:]
        xv = x0_ref[...] + jnp.dot(o_ref[...], wo_ref[...], preferred_element_type=F32)
        x_ref[...] = xv
        h = (xv * _rms(xv) * gain_ref[...]).astype(BF16)
        h_ref[...] = h
        g = lax.dot_general(h, wg_ref[...], NT, preferred_element_type=F32)
        u = lax.dot_general(h, wu_ref[...], NT, preferred_element_type=F32)
        g_ref[...] = g.astype(BF16)
        u_ref[...] = u.astype(BF16)
        a = (g * _sigmoid(g) * u).astype(BF16)
        a_ref[...] = a
        y = xv + jnp.dot(a, wd_ref[...], preferred_element_type=F32)
        if not has_loss:
            y_ref[...] = y
            return
        head_ref, target_ref, loss_ref, dhead_ref = refs[7], refs[8], refs[-2], refs[-1]
        head = head_ref[...]
        yhat = y * _rms(y)
        err = yhat * head - target_ref[...]
        dout = err * (1.0 / d)
        y_ref[...] = _rms_bwd(dout, y, head)[0]
        first = pl.program_id(0) == 0
        part = 0.5 * jnp.sum(jnp.mean(err * err, axis=-1, keepdims=True), axis=0, keepdims=True)
        _accumulate(loss_ref, jnp.broadcast_to(part, loss_ref.shape), first)
        _accumulate(dhead_ref, jnp.sum(dout * yhat, axis=0, keepdims=True), first)

    row = pl.BlockSpec((tm, d), lambda i: (i, 0))
    wide = pl.BlockSpec((tm, f), lambda i: (i, 0))
    in_specs = [row, pl.BlockSpec((tm, Q_W), lambda i: (i, 0)), _resident(w_out.shape), _resident((1, d)), _resident(wg.shape),
                _resident(wu.shape), _resident(wd.shape)]
    out_specs = [row, row, wide, wide, wide, row]
    out_shape = ([jax.ShapeDtypeStruct((t, d), F32)] * 2 + [jax.ShapeDtypeStruct((t, f), BF16)] * 3
                 + [jax.ShapeDtypeStruct((t, d), BF16)])
    if has_loss:
        in_specs += [_resident((1, d)), row]
        out_specs += [pl.BlockSpec((1, LANES), lambda i: (0, 0)), pl.BlockSpec((1, d), lambda i: (0, 0))]
        out_shape += [jax.ShapeDtypeStruct((1, LANES), F32), jax.ShapeDtypeStruct((1, d), F32)]
    outs, exchanged = _hosted_call(
        body, exchange, name=f"ffn_fwd_{tag}", grid=(t // tm,), in_specs=in_specs, out_specs=out_specs, out_shape=out_shape,
        scratch_shapes=[], semantics=("arbitrary" if has_loss else "parallel",),
        args=(x0, o, w_out, gain, wg, wu, wd) + (tuple(loss_head) if has_loss else ()))
    return (*outs, exchanged)


def _ffn_bwd(dy, x, gain, g, u, wg, wu, wd, tag, exchange=()):
    t, d = x.shape
    f = wd.shape[0]
    tm = min(256, t)

    def body(dy_ref, x_ref, gain_ref, g_ref, u_ref, wg_ref, wu_ref, wd_ref, dx_ref, dg_ref, du_ref, dgain_ref):
        dyv = dy_ref[...]
        da = lax.dot_general(dyv.astype(BF16), wd_ref[...], NT, preferred_element_type=F32)
        gv, uv = g_ref[...].astype(F32), u_ref[...].astype(F32)
        sg = _sigmoid(gv)
        act = gv * sg
        du = (da * act).astype(BF16)
        dg = (da * uv * (sg * (1.0 + gv * (1.0 - sg)))).astype(BF16)
        du_ref[...] = du
        dg_ref[...] = dg
        dh = jnp.dot(dg, wg_ref[...], preferred_element_type=F32) + jnp.dot(du, wu_ref[...], preferred_element_type=F32)
        xv, gain_v = x_ref[...], gain_ref[...]
        dx, xhat = _rms_bwd(dh, xv, gain_v)
        dx_ref[...] = dyv + dx
        _accumulate(dgain_ref, jnp.sum(dh * xhat, axis=0, keepdims=True), pl.program_id(0) == 0)

    row = pl.BlockSpec((tm, d), lambda i: (i, 0))
    wide = pl.BlockSpec((tm, f), lambda i: (i, 0))
    outs, exchanged = _hosted_call(
        body, exchange, name=f"ffn_bwd_{tag}", grid=(t // tm,),
        in_specs=[row, row, _resident((1, d)), wide, wide, _resident(wg.shape), _resident(wu.shape), _resident(wd.shape)],
        out_specs=[row, wide, wide, pl.BlockSpec((1, d), lambda i: (0, 0))],
        out_shape=[jax.ShapeDtypeStruct((t, d), F32), jax.ShapeDtypeStruct((t, f), BF16), jax.ShapeDtypeStruct((t, f), BF16),
                   jax.ShapeDtypeStruct((1, d), F32)],
        scratch_shapes=[], semantics=("arbitrary",), args=(dy, x, gain, g, u, wg, wu, wd))
    return (*outs, exchanged)


def _tn_matmul(a, b, name, into=None, row_block=0, row_blocks=1):
    t, k = a.shape
    n = b.shape[1]
    tk = k // 2 if (k // 2) % LANES == 0 else k
    tt = min(2048, t)
    first = row_block * (k // tk)

    def body(a_ref, b_ref, *rest):
        o_ref, acc_ref = rest[-2:]
        prod = lax.dot_general(a_ref[...].astype(BF16), b_ref[...].astype(BF16), TN, preferred_element_type=F32)
        j = pl.program_id(1)

        @pl.when(j == 0)
        def _():
            acc_ref[...] = prod

        @pl.when(j > 0)
        def _():
            acc_ref[...] += prod

        @pl.when(j == pl.num_programs(1) - 1)
        def _():
            o_ref[...] = acc_ref[...].astype(BF16)

    return pl.pallas_call(
        body, name=name, grid=(k // tk, t // tt),
        in_specs=[pl.BlockSpec((tt, tk), lambda i, j: (j, i)), pl.BlockSpec((tt, n), lambda i, j: (j, 0))]
                 + ([ANY] if into is not None else []),
        out_specs=pl.BlockSpec((tk, n), lambda i, j: (first + i, 0)),
        out_shape=jax.ShapeDtypeStruct((row_blocks * k, n), BF16),
        scratch_shapes=[pltpu.VMEM((tk, n), F32)],
        input_output_aliases={2: 0} if into is not None else {},
        compiler_params=_params("parallel", "arbitrary"),
    )(a, b, *([into] if into is not None else []))


def _attn_out_bwd(dx, w, o, dils, seq, tag, lse=None, sink=None, exchange=()):
    t, d = dx.shape
    ts = _tile_rows(seq)
    bl = t // seq
    ng = len(dils)
    has_sink = sink is not None
    expand = _head_expand().T

    def body(*refs):
        refs = list(refs)
        dx_ref, w_ref, o_ref, e_ref = refs[:4]
        refs = refs[4:]
        lse_ref, sink_ref = (refs.pop(0), refs.pop(0)) if has_sink else (None, None)
        do_refs, dl_refs = refs[:ng], refs[ng:2 * ng]
        refs = refs[2 * ng:]
        dsink_ref = refs.pop(0) if has_sink else None
        dof_ref, dlf_ref = refs
        do = lax.dot_general(dx_ref[...].astype(BF16), w_ref[...], NT, preferred_element_type=F32)
        prod = do * o_ref[...].astype(F32)
        hi = prod.astype(BF16)
        lo = (prod - hi.astype(F32)).astype(BF16)
        e = e_ref[...]
        dl = jnp.dot(hi, e, preferred_element_type=F32) + jnp.dot(lo, e, preferred_element_type=F32)
        for g in range(ng):
            for r, part in enumerate(_split_rows(do, dof_ref, dils[g])):
                do_refs[g][r] = part.astype(BF16)
            for r, part in enumerate(_split_rows(dl, dlf_ref, dils[g])):
                dl_refs[g][r] = part
        if has_sink:
            part = -jnp.exp2(sink_ref[...] - lse_ref[...]) * dl
            _accumulate(dsink_ref, jnp.sum(part, axis=0, keepdims=True), pl.program_id(0) == 0)

    row = pl.BlockSpec((ts, d), lambda i: (i, 0))
    narrow = pl.BlockSpec((ts, LANES), lambda i: (i, 0))
    args = [dx, w, o, expand]
    in_specs = [row, _resident(w.shape), pl.BlockSpec((ts, Q_W), lambda i: (i, 0)), _resident(expand.shape)]
    if has_sink:
        args += [lse, jnp.pad(sink.reshape(1, N_HEADS) * LOG2E, ((0, 0), (0, LANES - N_HEADS)))]
        in_specs += [narrow, _resident((1, LANES))]
    out_specs = [_res_spec(seq, dl, Q_W) for dl in dils] + [_res_spec(seq, dl, LANES) for dl in dils]
    out_shape = ([jax.ShapeDtypeStruct(_res_shape(bl, seq, dl, Q_W), BF16) for dl in dils]
                 + [jax.ShapeDtypeStruct(_res_shape(bl, seq, dl, LANES), F32) for dl in dils])
    if has_sink:
        out_specs.append(pl.BlockSpec((1, LANES), lambda i: (0, 0)))
        out_shape.append(jax.ShapeDtypeStruct((1, LANES), F32))
    outs, exchanged = _hosted_call(
        body, exchange, name=f"attn_out_bwd_{tag}", grid=(t // ts,), in_specs=in_specs, out_specs=out_specs, out_shape=out_shape,
        scratch_shapes=[_stage(ts, Q_W), _stage(ts, LANES)], semantics=("arbitrary" if has_sink else "parallel",), args=args)
    return list(outs[:ng]), list(outs[ng:2 * ng]), (outs[2 * ng] if has_sink else None), exchanged


def _attn_bwd(qkv, do, lse, delta, cos, sin, w, tag, exchange=()):
    shape = qkv.shape
    dil = shape[1]
    rows_all = _seq_view(qkv)
    nseq, length, _ = rows_all.shape
    bq = min(QUERY_BLOCK, length)
    wk = _key_rows(bq, w, length)
    nb = length // bq
    per_step = _per_step(dil, length)

    def body(*refs):
        def sub(i, carry):
            one(*[ref.at[i] for ref in refs[:7]], *refs[7:])
            return carry

        if per_step == 1:
            sub(0, 0)
        else:
            lax.fori_loop(0, per_step, sub, 0)

    def one(qkv_ref, do_ref, lse_ref, dl_ref, cos_ref, sin_ref, dp_ref, kk_ref, vv_ref, dk_ref, dv_ref):
        _swap_halves(qkv_ref, Q_W, kk_ref)
        _swap_halves(qkv_ref, Q_W + KV_W, vv_ref)
        dk_ref[...] = jnp.zeros_like(dk_ref)
        dv_ref[...] = jnp.zeros_like(dv_ref)
        band = _band(bq, wk)
        lo_q = lax.broadcasted_iota(jnp.int32, (bq, LANES), 1) < HEAD_DIM
        hi_q = jnp.logical_not(lo_q)

        def block(i, carry):
            q0, k0 = _window(i, bq, w, wk, length)
            valid = jnp.abs(band + (q0 - k0)) <= w
            rows, krows = pl.ds(q0, bq), pl.ds(k0, wk)
            c, sn = cos_ref[rows, :], -sin_ref[rows, :]
            lse_t, dl_t = lse_ref[rows, :], dl_ref[rows, :]
            for kv in range(N_KV):
                heads = [(kv * GRP + h, h % 2) for h in range(GRP)]
                cols = [slice((kv * 2 + j) * LANES, (kv * 2 + j + 1) * LANES) for j in range(GRP // 2)]
                qp = [qkv_ref[rows, cs] for cs in cols]
                dop = [do_ref[rows, cs] for cs in cols]
                k2 = _pair_operand(qkv_ref, kk_ref, Q_W, kv, krows)
                v2 = _pair_operand(qkv_ref, vv_ref, Q_W + KV_W, kv, krows)
                sc2 = [lax.dot_general(q_, k2, NT, preferred_element_type=F32) for q_ in qp]
                dp2 = [lax.dot_general(d_, v2, NT, preferred_element_type=F32) for d_ in dop]
                sc = [s_[:, half * wk:(half + 1) * wk] for s_ in sc2 for half in range(2)]
                dp = [d_[:, half * wk:(half + 1) * wk] for d_ in dp2 for half in range(2)]
                p = [jnp.exp2(jnp.where(valid, s_, NEG_INF) - _over_keys(lse_t[:, hd:hd + 1], wk))
                     for s_, (hd, _) in zip(sc, heads)]
                ds = [(p_ * (dp_ - _over_keys(dl_t[:, hd:hd + 1], wk))).astype(BF16) for p_, dp_, (hd, _) in zip(p, dp, heads)]
                pb = [p_.astype(BF16) for p_ in p]
                for j in range(GRP // 2):
                    dq = jnp.dot(jnp.concatenate([ds[2 * j], ds[2 * j + 1]], axis=1), k2, preferred_element_type=F32) * SCALE
                    dp_ref[rows, cols[j]] = _rope(dq, c, sn).astype(BF16)
                zero = jnp.zeros((bq, LANES), BF16)
                q4 = jnp.concatenate([jnp.where(lo_q if h % 2 == 0 else hi_q, qp[h // 2], zero) for h in range(GRP)], axis=0)
                do4 = jnp.concatenate([jnp.where(lo_q if h % 2 == 0 else hi_q, dop[h // 2], zero) for h in range(GRP)], axis=0)
                dk_ref[kv, krows, :] += lax.dot_general(jnp.concatenate(ds, axis=0), q4, TN, preferred_element_type=F32)
                dv_ref[kv, krows, :] += lax.dot_general(jnp.concatenate(pb, axis=0), do4, TN, preferred_element_type=F32)
            return carry

        lax.fori_loop(0, nb, block, 0)
        lo = lax.broadcasted_iota(jnp.int32, (length, LANES), 1) < HEAD_DIM
        c, sn = cos_ref[...], -sin_ref[...]
        for ch in range(KV_W // LANES):
            halves = []
            for acc_ref in (dk_ref, dv_ref):
                even, odd = acc_ref[2 * ch], acc_ref[2 * ch + 1]
                even = even + pltpu.roll(even, HEAD_DIM, 1)
                odd = odd + pltpu.roll(odd, HEAD_DIM, 1)
                halves.append(jnp.where(lo, even, odd))
            dp_ref[:, Q_W + ch * LANES:Q_W + (ch + 1) * LANES] = _rope(halves[0] * LN2, c, sn).astype(BF16)
            dp_ref[:, Q_W + KV_W + ch * LANES:Q_W + KV_W + (ch + 1) * LANES] = halves[1].astype(BF16)

    def seq_block(c):
        return pl.BlockSpec((per_step, length, c), lambda i: (i, 0, 0))

    table = pl.BlockSpec((per_step, length, LANES), lambda i: (i % (dil // per_step), 0, 0))
    (out,), exchanged = _hosted_call(
        body, exchange, name=f"attn_bwd_{tag}", grid=(nseq // per_step,),
        in_specs=[seq_block(QKV_W), seq_block(Q_W), seq_block(LANES), seq_block(LANES), table, table],
        out_specs=[seq_block(QKV_W)],
        out_shape=[jax.ShapeDtypeStruct((nseq, length, QKV_W), BF16)],
        scratch_shapes=[pltpu.VMEM((KV_W // LANES, length, LANES), BF16), pltpu.VMEM((KV_W // LANES, length, LANES), BF16),
                        pltpu.VMEM((N_KV, length, LANES), F32), pltpu.VMEM((N_KV, length, LANES), F32)],
        semantics=("parallel",), args=(rows_all, _seq_view(do), _seq_view(lse), _seq_view(delta), cos, sin))
    return out.reshape(shape), exchanged


def _qkv_bwd(dy, x, gain, w, dps, dils, seq, tag, exchange=()):
    t, d = x.shape
    ts = _tile_rows(seq)
    ng = len(dps)

    def body(dy_ref, x_ref, gain_ref, w_ref, *refs):
        dp_refs, (dx_ref, dgain_ref, stage_ref) = refs[:ng], refs[ng:]
        dh = None
        for gi in range(ng):
            dil = dils[gi]
            n = ts // dil
            dp = dp_refs[gi][0] if dil == 1 else jnp.concatenate([dp_refs[gi][r] for r in range(dil)], axis=0)
            part = jnp.dot(dp, w_ref[gi * QKV_W:(gi + 1) * QKV_W, :], preferred_element_type=F32)
            part = _merge_rows([part[r * n:(r + 1) * n] for r in range(dil)], stage_ref, dil)
            dh = part if dh is None else dh + part
        xv, gain_v = x_ref[...], gain_ref[...]
        dx, xhat = _rms_bwd(dh, xv, gain_v)
        dx_ref[...] = dy_ref[...] + dx
        _accumulate(dgain_ref, jnp.sum(dh * xhat, axis=0, keepdims=True), pl.program_id(0) == 0)

    row = pl.BlockSpec((ts, d), lambda i: (i, 0))
    (dx, dgain), exchanged = _hosted_call(
        body, exchange, name=f"qkv_bwd_{tag}", grid=(t // ts,),
        in_specs=[row, row, _resident((1, d)), _resident(w.shape)] + [_res_spec(seq, dl, QKV_W) for dl in dils],
        out_specs=[row, pl.BlockSpec((1, d), lambda i: (0, 0))],
        out_shape=[jax.ShapeDtypeStruct((t, d), F32), jax.ShapeDtypeStruct((1, d), F32)],
        scratch_shapes=[_stage(ts, d)], semantics=("arbitrary",), args=(dy, x, gain, w, *dps))
    return dx, dgain, exchanged


ANY = pl.BlockSpec(memory_space=pl.ANY)


def _place():
    x, y, c = lax.axis_index("x"), lax.axis_index("y"), lax.axis_index("c")
    return x, y, c


def _exchange_steps(srcs, dsts, gather, send_sems, recv_sems, local_sems):
    x, y, c = _place()
    me, sibling = (x, y, c), (x, y, 1 - c)
    chips = [(1 - x, y), (x, 1 - y), (1 - x, 1 - y)]
    mine = 4 * x + 2 * y + c

    def slot(a, device):
        px, py, pc = device
        return dsts[a].at[4 * px + 2 * py + pc]

    def passes(a, k, block, to, src=None):
        rows = slot(a, block)
        return pltpu.make_async_remote_copy(src_ref=rows if src is None else src, dst_ref=rows, send_sem=send_sems.at[a, k],
                                            recv_sem=recv_sems.at[a, k], device_id=to, device_id_type=MESH)

    def scatters(a, k):
        peer = mine ^ k
        return pltpu.make_async_remote_copy(
            src_ref=srcs[a].at[peer], dst_ref=dsts[a].at[mine], send_sem=send_sems.at[a, k - 1], recv_sem=recv_sems.at[a, k - 1],
            device_id=(peer // 4, (peer // 2) % 2, peer % 2), device_id_type=MESH)

    def local(a):
        return pltpu.make_async_copy(srcs[a] if gather[a] else srcs[a].at[mine], dsts[a].at[mine], local_sems.at[a])

    def first_copies(a):
        if not gather[a]:
            return [scatters(a, k) for k in range(1, N_DEV)]
        return [passes(a, 0, me, sibling, src=srcs[a])] + [passes(a, 1 + j, me, (*chip, c), src=srcs[a]) for j, chip in enumerate(chips)]

    def start():
        for a in range(len(srcs)):
            local(a).start()
            for cp in first_copies(a):
                cp.start()

    def forward():
        for a in range(len(srcs)):
            if gather[a]:
                for j, chip in enumerate(chips):
                    passes(a, 1 + j, (*chip, c), me).wait_recv()
                    passes(a, 4 + j, (*chip, c), sibling).start()

    def finish():
        for a in range(len(srcs)):
            if gather[a]:
                passes(a, 0, sibling, me).wait_recv()
                for j, chip in enumerate(chips):
                    passes(a, 4 + j, (*chip, 1 - c), me).wait_recv()
                    passes(a, 4 + j, (*chip, c), sibling).wait_send()
                for cp in first_copies(a):
                    cp.wait_send()
            else:
                for cp in first_copies(a):
                    cp.wait()
            local(a).wait()

    return start, forward, finish


def _exchange_scratch(n):
    return [pltpu.SemaphoreType.DMA((n, N_DEV - 1)), pltpu.SemaphoreType.DMA((n, N_DEV - 1)), pltpu.SemaphoreType.DMA((n,))]


def _exchanged_shapes(exchange):
    return [jax.ShapeDtypeStruct(((N_DEV,) + a.shape) if g else a.shape, a.dtype) for a, g in exchange]


def _hosted_call(body, exchange, *, name, grid, in_specs, out_specs, out_shape, scratch_shapes, semantics, args):
    out_specs, out_shape, scratch = list(out_specs), list(out_shape), list(scratch_shapes)
    if not exchange:
        outs = pl.pallas_call(body, name=name, grid=grid, in_specs=in_specs, out_specs=out_specs, out_shape=out_shape,
                              scratch_shapes=scratch, compiler_params=_params(*semantics))(*args)
        return list(outs), []
    n, n_in, n_out, n_scr = len(exchange), len(in_specs), len(out_specs), len(scratch)
    gather = [g for _, g in exchange]
    steps = math.prod(grid)

    def hosted(*refs):
        own_in, x_in = refs[:n_in], refs[n_in:n_in + n]
        own_out, x_out = refs[n_in + n:n_in + n + n_out], refs[n_in + n + n_out:n_in + 2 * n + n_out]
        own_scr, sems = refs[n_in + 2 * n + n_out:n_in + 2 * n + n_out + n_scr], refs[-3:]
        step = pl.program_id(0)
        for axis in range(1, len(grid)):
            step = step * grid[axis] + pl.program_id(axis)
        start, forward, finish = _exchange_steps(x_in, x_out, gather, *sems)
        pl.when(step == 0)(start)
        body(*own_in, *own_out, *own_scr)
        pl.when(step == steps // 2)(forward)
        pl.when(step == steps - 1)(finish)

    outs = pl.pallas_call(
        hosted, name=name, grid=grid, in_specs=list(in_specs) + [ANY] * n, out_specs=out_specs + [ANY] * n,
        out_shape=out_shape + _exchanged_shapes(exchange), scratch_shapes=scratch + _exchange_scratch(n),
        compiler_params=_params(*["arbitrary"] * len(grid)),
    )(*args, *[a for a, _ in exchange])
    return list(outs[:n_out]), list(outs[n_out:])


def _exchange_now(exchange, name):
    n = len(exchange)
    gather = [g for _, g in exchange]

    def body(*refs):
        for step in _exchange_steps(refs[:n], refs[n:2 * n], gather, *refs[2 * n:]):
            step()

    return pl.pallas_call(
        body, name=name, in_specs=[ANY] * n, out_specs=[ANY] * n, out_shape=_exchanged_shapes(exchange),
        scratch_shapes=_exchange_scratch(n),
    )(*[a for a, _ in exchange])


def _all_reduce_small(v):
    def body(v_ref, o_ref, recv_ref, send_sems, recv_sems):
        x, y, c = _place()
        me = 4 * x + 2 * y + c
        copies = []
        for k in range(1, N_DEV):
            peer = me ^ k
            copies.append(pltpu.make_async_remote_copy(
                src_ref=v_ref, dst_ref=recv_ref.at[k], send_sem=send_sems.at[k - 1], recv_sem=recv_sems.at[k - 1],
                device_id=(peer // 4, (peer // 2) % 2, peer % 2), device_id_type=MESH))
        for cp in copies:
            cp.start()
        recv_ref[0] = v_ref[...]
        for cp in copies:
            cp.wait()
        acc = recv_ref[me]
        for src in range(1, N_DEV):
            acc = acc + recv_ref[me ^ src]
        o_ref[...] = acc

    vm = pl.BlockSpec(memory_space=pltpu.VMEM)
    return pl.pallas_call(
        body, name="all_reduce_small", in_specs=[vm], out_specs=vm, out_shape=jax.ShapeDtypeStruct(v.shape, F32),
        scratch_shapes=[pltpu.VMEM((N_DEV,) + v.shape, F32), pltpu.SemaphoreType.DMA((N_DEV - 1,)),
                        pltpu.SemaphoreType.DMA((N_DEV - 1,))],
    )(v)


def _adamw_math(w, g, m, v):
    m = ADAM_B1 * m + (1.0 - ADAM_B1) * g
    v = ADAM_B2 * v + (1.0 - ADAM_B2) * (g * g)
    m_hat = m / (1.0 - ADAM_B1 ** ADAM_STEP)
    v_hat = v / (1.0 - ADAM_B2 ** ADAM_STEP)
    delta = -ADAM_LR * (m_hat / (jnp.sqrt(v_hat) + ADAM_EPS) + ADAM_WD * w)
    return delta, m, v


def _adamw(parts, w, m, v, name, layer=None, into=None):
    r, c = w.shape[-2:]
    tr = r // 2 if r % 16 == 0 and r >= 256 else r
    n = len(parts)

    def body(*refs):
        w_ref, m_ref, v_ref = refs[n:n + 3]
        g_ref, d_ref, nm_ref, nv_ref = refs[-4:]
        g = refs[0][...].astype(F32)
        for p_ref in refs[1:n]:
            g = g + p_ref[...].astype(F32)
        g_ref[...] = g
        d_ref[...], nm_ref[...], nv_ref[...] = _adamw_math(w_ref[...], g, m_ref[...], v_ref[...])

    def slab(slot):
        return pl.BlockSpec((None, tr, c), lambda i: (slot, i, 0))

    tile = pl.BlockSpec((tr, c), lambda i: (i, 0)) if layer is None else slab(layer)
    arrays, in_specs = [], []
    for p in parts:
        if isinstance(p, tuple):
            arrays.append(p[0])
            in_specs.append(slab(p[1]))
        else:
            arrays.append(p)
            in_specs.append(tile)
    kept = list(into) if into is not None else []
    return pl.pallas_call(
        body, name=name, grid=(r // tr,), in_specs=in_specs + [tile] * 3 + [ANY] * len(kept), out_specs=[tile] * 4,
        out_shape=[jax.ShapeDtypeStruct(w.shape, F32)] * 4,
        input_output_aliases={n + 3 + k: k for k in range(len(kept))}, compiler_params=_params("parallel"),
    )(*arrays, w, m, v, *kept)


def _rows(g):
    return g.reshape(-1, g.shape[-1])


def _row_blocks(dw):
    k, n = dw.shape
    return dw.reshape(N_DEV, k // N_DEV, n)


def _pack_rows(rows, width):
    out = None
    for i, r in enumerate(rows):
        r = r.reshape(1, -1).astype(F32)
        r = jnp.pad(r, ((i, 8 - 1 - i), (0, width - r.shape[1])))
        out = r if out is None else out + r
    return out


def _mixer_fwd(x, gain, w_in, cos, sin, seq, groups, tag, sink=None, exchanges=None):
    exchanges = exchanges or {}
    os, lses, got = [], [], {}
    qkvs, hs, got["proj"] = _qkv_proj(x, gain, w_in, cos, sin, seq, [dil for dil, _ in groups], tag,
                                      exchange=exchanges.get("proj", ()))
    for gi, (dil, w) in enumerate(groups):
        o, lse, got[gi] = _attn_fwd(qkvs[gi], w, f"{tag}{gi}", sink=sink, exchange=exchanges.get(gi, ()))
        os.append(o)
        lses.append(lse)
    o, lses = _mix_groups(os, lses, [dl for dl, _ in groups], seq, tag)
    return (qkvs, hs, o, lses), got


def _mixer_bwd(dy, x_in, gain, w_in, w_out, saved, cos, sin, seq, groups, tag, sink=None, exchanges=None, scatter_own=False):
    qkvs, hs, o, lses = saved
    t, d = x_in.shape
    dils = [dl for dl, _ in groups]
    lse_tokens = lses[0].reshape(t, LANES) if sink is not None else None
    dw_out = _tn_matmul(o, dy, f"dw_out_{tag}")
    dos, dls, dsink, early = _attn_out_bwd(dy, w_out, o, dils, seq, tag, lse=lse_tokens, sink=sink,
                                           exchange=_to_send([dw_out]) if scatter_own else ())
    if scatter_own:
        (dw_out,) = early
    exchanges = exchanges or {}
    dps, got = [], {}
    for gi, (dil, w) in enumerate(groups):
        dp, got[gi] = _attn_bwd(qkvs[gi], dos[gi], lses[gi], dls[gi], _tables_by_residue(cos, seq, dil),
                                _tables_by_residue(sin, seq, dil), w, f"{tag}{gi}", exchange=exchanges.get(gi, ()))
        dps.append(dp)
    dw_in = None
    for gi in range(len(groups)):
        dw_in = _tn_matmul(dps[gi].reshape(t, QKV_W), hs[gi].reshape(t, d), f"dw_in_{tag}{gi}", into=dw_in, row_block=gi,
                           row_blocks=len(groups))
    dx, dgain, late = _qkv_bwd(dy, x_in, gain, w_in, dps, dils, seq, tag, exchange=_to_send([dw_in]) if scatter_own else ())
    if scatter_own:
        (dw_in,) = late
    return dx, dw_in, dw_out, dgain, dsink, got


def _ffn_layer_bwd(dy, x_in, gain, saved, wg, wu, wd, tag, exchange=()):
    g, u, act, h = saved
    dx, dg, du, dgain, got = _ffn_bwd(dy, x_in, gain, g, u, wg, wu, wd, tag, exchange=exchange)
    dwd = _tn_matmul(act, dy, f"dw_down_{tag}")
    dwg = _tn_matmul(dg, h, f"dw_gate_{tag}")
    dwu = _tn_matmul(du, h, f"dw_up_{tag}")
    return dx, dwg, dwu, dwd, dgain, got


def _to_send(dws):
    return [(_row_blocks(g), False) for g in dws]


def kernel(x, a_w_in, a_sink, a_w_out, b_w_in, b_w_out, norm_mix, norm_ffn, w_gate, w_up, w_down, final_norm, loss_target, m_a_w_in, m_a_sink, m_a_w_out, m_b_w_in, m_b_w_out, m_norm_mix, m_norm_ffn, m_w_gate, m_w_up, m_w_down, m_final_norm, v_a_w_in, v_a_sink, v_a_w_out, v_b_w_in, v_b_w_out, v_norm_mix, v_norm_ffn, v_w_gate, v_w_up, v_w_down, v_final_norm):
    bl, seq, d = x.shape
    t = bl * seq
    xf = x.reshape(t, d)
    target = loss_target.reshape(t, d)
    cos, sin = _rope_tables(seq)
    groups_a = [(1, ATTN_HALF_WINDOW)]
    groups_b = [(dil, window // 2 // dil) for window, dil in DILATED_GROUPS]

    def flip(w_):
        return jnp.swapaxes(w_, -1, -2)

    a_w_in, m_a_w_in, v_a_w_in, b_w_in, m_b_w_in, v_b_w_in = map(flip, (a_w_in, m_a_w_in, v_a_w_in, b_w_in, m_b_w_in, v_b_w_in))
    w_gate, m_w_gate, v_w_gate, w_up, m_w_up, v_w_up = map(flip, (w_gate, m_w_gate, v_w_gate, w_up, m_w_up, v_w_up))

    def shard(w_, layer):
        return (w_[layer].astype(BF16), True)

    (wa_in,) = map(_rows, _exchange_now([shard(a_w_in, 0)], "gather_first"))

    saved_a, got = _mixer_fwd(xf, norm_mix[0:1], wa_in, cos, sin, seq, groups_a, "a", sink=a_sink[0],
                              exchanges={"proj": [shard(w_down, 0), shard(a_w_out, 0)], 0: [shard(w_gate, 0), shard(w_up, 0)]})
    wg0, wu0, wd0, wa_out = map(_rows, got[0] + got["proj"])
    x1_0, x2_0, *saved_0, got = _ffn_fwd(xf, saved_a[2], wa_out, norm_ffn[0:1], wg0, wu0, wd0, "0",
                                         exchange=[shard(b_w_in, 0), shard(b_w_out, 0)])
    wb_in, wb_out = map(_rows, got)
    saved_b, got = _mixer_fwd(x2_0, norm_mix[1:2], wb_in, cos, sin, seq, groups_b, "b",
                              exchanges={0: [shard(w_gate, 1)], 1: [shard(w_up, 1)], 2: [shard(w_down, 1)]})
    wg1, wu1, wd1 = map(_rows, got[0] + got[1] + got[2])
    x1_1, dy, *saved_1, loss_part, d_final, _ = _ffn_fwd(x2_0, saved_b[2], wb_out, norm_ffn[1:2], wg1, wu1, wd1, "1",
                                                         loss_head=(final_norm.reshape(1, d), target))

    dy, dwg1, dwu1, dwd1, d_nf1, _ = _ffn_layer_bwd(dy, x1_1, norm_ffn[1:2], saved_1, wg1, wu1, wd1, "1")
    dy, dwb_in, dwb_out, d_nm1, _, got = _mixer_bwd(
        dy, x2_0, norm_mix[1:2], wb_in, wb_out, saved_b, cos, sin, seq, groups_b, "b",
        exchanges={0: _to_send([dwg1, dwd1]), 1: _to_send([dwu1])})
    (r_g1, r_d1), (r_u1,) = got[0], got[1]
    dy, dwg0, dwu0, dwd0, d_nf0, (r_b_in, r_b_out) = _ffn_layer_bwd(
        dy, x1_0, norm_ffn[0:1], saved_0, wg0, wu0, wd0, "0", exchange=_to_send([dwb_in, dwb_out]))
    dy, r_a_in, r_a_out, d_nm0, d_sink, got = _mixer_bwd(
        dy, xf, norm_mix[0:1], wa_in, wa_out, saved_a, cos, sin, seq, groups_a, "a", sink=a_sink[0],
        exchanges={0: _to_send([dwg0, dwu0, dwd0])}, scatter_own=True)
    r_g0, r_u0, r_d0 = got[0]
    grad_x = dy.reshape(bl, seq, d)

    def update(received, w_, m_, v_, name):
        out = None
        for layer in reversed(range(len(received))):
            out = _adamw([(received[layer], src) for src in range(N_DEV)], w_, m_, v_, f"adamw_{name}{layer}", layer=layer, into=out)
        return out

    u_a_in = update([r_a_in], a_w_in, m_a_w_in, v_a_w_in, "a_in")
    u_a_out = update([r_a_out], a_w_out, m_a_w_out, v_a_w_out, "a_out")
    u_b_in = update([r_b_in], b_w_in, m_b_w_in, v_b_w_in, "b_in")
    u_b_out = update([r_b_out], b_w_out, m_b_w_out, v_b_w_out, "b_out")
    u_gate = update([r_g0, r_g1], w_gate, m_w_gate, v_w_gate, "gate")
    u_up = update([r_u0, r_u1], w_up, m_w_up, v_w_up, "up")
    u_down = update([r_d0, r_d1], w_down, m_w_down, v_w_down, "down")

    small = _pack_rows([d_nm0, d_nm1, d_nf0, d_nf1, d_final, d_sink, loss_part], d)
    total = _all_reduce_small(small)
    small_w = _pack_rows([norm_mix[0], norm_mix[1], norm_ffn[0], norm_ffn[1], final_norm, a_sink], d)
    small_m = _pack_rows([m_norm_mix[0], m_norm_mix[1], m_norm_ffn[0], m_norm_ffn[1], m_final_norm, m_a_sink], d)
    small_v = _pack_rows([v_norm_mix[0], v_norm_mix[1], v_norm_ffn[0], v_norm_ffn[1], v_final_norm, v_a_sink], d)
    u_small = _adamw([total], small_w, small_m, small_v, "adamw_small")
    loss = total[6, 0]

    outs = []
    for k in range(4):
        sm = u_small[k]
        outs += [flip(u_a_in[k]), sm[5:6, :N_HEADS], u_a_out[k], flip(u_b_in[k]), u_b_out[k], sm[0:2], sm[2:4],
                 flip(u_gate[k]), flip(u_up[k]), u_down[k], sm[4]]
    return (loss, grad_x, *outs)
```

```python
import functools
import math

import jax
import jax.numpy as jnp
from jax import lax
from jax.experimental import pallas as pl
from jax.experimental.pallas import tpu as pltpu

F32 = jnp.float32
BF16 = jnp.bfloat16

HEAD_DIM = 64
N_HEADS = 16
N_KV = 4
GRP = N_HEADS // N_KV
Q_W = N_HEADS * HEAD_DIM
KV_W = N_KV * HEAD_DIM
QKV_W = Q_W + 2 * KV_W
ATTN_HALF_WINDOW = 128
DILATED_GROUPS = ((128, 1), (512, 4), (2048, 16))
ROPE_THETA = 10000.0
RMS_EPS = 1e-6
NEG_INF = -1e30
SCALE = 1.0 / math.sqrt(HEAD_DIM)
LOG2E = 1.0 / math.log(2.0)
LN2 = math.log(2.0)

ADAM_LR = 0.001
ADAM_B1 = 0.9
ADAM_B2 = 0.999
ADAM_EPS = 1e-08
ADAM_WD = 0.01
ADAM_STEP = 10

LANES = 128
VMEM_LIMIT = 56 * 1024 * 1024
QUERY_BLOCK = 128
PAIRS_PER_PHASE = 4
KV_PER_PHASE = 2
N_DEV = 8
MESH = pl.DeviceIdType.MESH

NT = (((1,), (1,)), ((), ()))
TN = (((0,), (0,)), ((), ()))


def _params(*sem):
    return pltpu.CompilerParams(dimension_semantics=tuple(sem) if sem else None, vmem_limit_bytes=VMEM_LIMIT)


def _resident(shape):
    return pl.BlockSpec(shape, lambda *_: (0,) * len(shape), pipeline_mode=pl.Buffered(1))


def _rope_tables(seq):
    inv_freq = 1.0 / (ROPE_THETA ** (jnp.arange(0, HEAD_DIM, 2, dtype=F32) / HEAD_DIM))
    ang = jnp.arange(seq, dtype=F32)[:, None] * inv_freq[None, :]
    cos, sin = jnp.cos(ang), jnp.sin(ang)
    return jnp.tile(cos, (1, 4)), jnp.concatenate([-sin, sin, -sin, sin], axis=1)


def _rope(t, cos, sin_signed):
    lane = lax.broadcasted_iota(jnp.int32, t.shape, 1)
    first = (lane & (HEAD_DIM // 2)) == 0
    swapped = jnp.where(first, pltpu.roll(t, LANES - HEAD_DIM // 2, 1), pltpu.roll(t, HEAD_DIM // 2, 1))
    return t * cos + swapped * sin_signed


def _rms(x):
    return lax.rsqrt(jnp.mean(x * x, axis=-1, keepdims=True) + RMS_EPS)


def _rms_bwd(dh, x, gain):
    r = _rms(x)
    xhat = x * r
    dxh = dh * gain
    dx = r * (dxh - xhat * jnp.mean(dxh * xhat, axis=-1, keepdims=True))
    return dx, xhat


def _accumulate(ref, value, first):
    @pl.when(first)
    def _():
        ref[...] = jnp.zeros_like(ref)

    ref[...] += value


def _tile_rows(seq):
    return min(512, seq)


def _res_shape(bl, seq, dil, c):
    ts = _tile_rows(seq)
    return (bl, dil, seq // ts, ts // dil, c)


def _res_spec(seq, dil, c):
    ts = _tile_rows(seq)
    per_seq = seq // ts
    return pl.BlockSpec((None, dil, None, ts // dil, c), lambda i: (i // per_seq, 0, i % per_seq, 0, 0))


def _seq_view(a):
    bl, dil, tiles, n, c = a.shape
    return a.reshape(bl * dil, tiles * n, c)


def _stage(ts, c):
    return pltpu.VMEM((c // LANES, ts, LANES), F32)


def _split_rows(val, stage_ref, dil):
    if dil == 1:
        return [val]
    ts, c = val.shape
    n, nc = ts // dil, c // LANES
    for k in range(nc):
        stage_ref[k] = val[:, k * LANES:(k + 1) * LANES]
    return [jnp.concatenate([stage_ref[k, pl.ds(r, n, stride=dil), :] for k in range(nc)], axis=1) for r in range(dil)]


def _merge_rows(parts, stage_ref, dil):
    if dil == 1:
        return parts[0]
    n, c = parts[0].shape
    nc = c // LANES
    for r, part in enumerate(parts):
        for k in range(nc):
            stage_ref[k, pl.ds(r, n, stride=dil), :] = part[:, k * LANES:(k + 1) * LANES]
    return jnp.concatenate([stage_ref[k] for k in range(nc)], axis=1)


def _tables_tiled(table, seq, dil):
    ts = _tile_rows(seq)
    return table.reshape(seq // ts, ts // dil, dil, LANES).transpose(0, 2, 1, 3).reshape(seq, LANES)


def _tables_by_residue(table, seq, dil):
    return table.reshape(seq // dil, dil, LANES).transpose(1, 0, 2)


def _qkv_proj(x, gain, w, cos, sin, seq, dils, tag, exchange=()):
    t, d = x.shape
    ts = _tile_rows(seq)
    per_seq = seq // ts
    ng = len(dils)
    tables = [t_ for dil in dils for t_ in (_tables_tiled(cos, seq, dil), _tables_tiled(sin, seq, dil))]

    def body(x_ref, g_ref, w_ref, *refs):
        table_refs, o_refs, h_refs, stage_ref = refs[:2 * ng], refs[2 * ng:3 * ng], refs[3 * ng:4 * ng], refs[4 * ng]
        xv = x_ref[...]
        h_tokens = xv * _rms(xv) * g_ref[...]
        for gi, dil in enumerate(dils):
            n = ts // dil
            h = jnp.concatenate(_split_rows(h_tokens, stage_ref, dil), axis=0).astype(BF16)
            for r in range(dil):
                h_refs[gi][r] = h[r * n:(r + 1) * n]
            acc = lax.dot_general(h, w_ref[gi * QKV_W:(gi + 1) * QKV_W, :], NT, preferred_element_type=F32)
            c, s = table_refs[2 * gi][...], table_refs[2 * gi + 1][...]
            for j in range(QKV_W // LANES):
                cols = slice(j * LANES, (j + 1) * LANES)
                val = acc[:, cols]
                if j < (Q_W + KV_W) // LANES:
                    val = _rope(val, c, s)
                if j < Q_W // LANES:
                    val = val * (SCALE * LOG2E)
                val = val.astype(BF16)
                for r in range(dil):
                    o_refs[gi][r, :, cols] = val[r * n:(r + 1) * n]

    table = pl.BlockSpec((ts, LANES), lambda i: (i % per_seq, 0))
    outs, exchanged = _hosted_call(
        body, exchange, name=f"qkv_proj_{tag}", grid=(t // ts,),
        in_specs=[pl.BlockSpec((ts, d), lambda i: (i, 0)), _resident((1, d)), _resident(w.shape)] + [table] * (2 * ng),
        out_specs=[_res_spec(seq, dil, QKV_W) for dil in dils] + [_res_spec(seq, dil, d) for dil in dils],
        out_shape=[jax.ShapeDtypeStruct(_res_shape(t // seq, seq, dil, QKV_W), BF16) for dil in dils]
                  + [jax.ShapeDtypeStruct(_res_shape(t // seq, seq, dil, d), BF16) for dil in dils],
        scratch_shapes=[_stage(ts, d)], semantics=("parallel",), args=(x, gain, w, *tables))
    return outs[:ng], outs[ng:], exchanged


def _band(bq, wk):
    return lax.broadcasted_iota(jnp.int32, (bq, wk), 0) - lax.broadcasted_iota(jnp.int32, (bq, wk), 1)


def _swap_halves(src_ref, base, dst_ref):
    for c in range(KV_W // LANES):
        dst_ref[c] = pltpu.roll(src_ref[:, base + c * LANES:base + (c + 1) * LANES], HEAD_DIM, 1)


def _pair_operand(src_ref, swapped_ref, base, kv, rows):
    c = kv // 2
    chunk, swapped = src_ref[rows, base + c * LANES:base + (c + 1) * LANES], swapped_ref[c, rows, :]
    lo = lax.broadcasted_iota(jnp.int32, chunk.shape, 1) < HEAD_DIM
    zero = jnp.zeros_like(chunk)
    if kv % 2 == 0:
        return jnp.concatenate([jnp.where(lo, chunk, zero), jnp.where(lo, zero, swapped)], axis=0)
    return jnp.concatenate([jnp.where(lo, swapped, zero), jnp.where(lo, zero, chunk)], axis=0)


def _over_keys(col, wk):
    if wk % LANES:
        return jnp.broadcast_to(col, (col.shape[0], wk))
    wide = jnp.broadcast_to(col, (col.shape[0], LANES))
    return wide if wk == LANES else jnp.concatenate([wide] * (wk // LANES), axis=1)


def _per_step(dil, length):
    return max(1, min(dil, 512 // length))


def _key_rows(bq, w, length):
    return min(bq + 2 * w, length)


def _window(i, bq, w, wk, length):
    q0 = pl.multiple_of(i * bq, bq)
    k0 = pl.multiple_of(jnp.clip(q0 - w, 0, length - wk), min(w, bq))
    return q0, k0


def _attn_fwd(qkv, w, tag, sink=None, exchange=()):
    shape = qkv.shape
    rows_all = _seq_view(qkv)
    nseq, length, _ = rows_all.shape
    bq = min(QUERY_BLOCK, length)
    wk = _key_rows(bq, w, length)
    nb = length // bq
    has_sink = sink is not None
    per_step = _per_step(shape[1], length)

    def body(*refs):
        sink_ref = refs[1] if has_sink else None
        kk_ref, vv_ref = refs[-2:]
        for sub in range(per_step):
            one(refs[0].at[sub], refs[-4].at[sub], refs[-3].at[sub], sink_ref, kk_ref, vv_ref)

    def one(qkv_ref, o_ref, lse_ref, sink_ref, kk_ref, vv_ref):
        _swap_halves(qkv_ref, Q_W, kk_ref)
        _swap_halves(qkv_ref, Q_W + KV_W, vv_ref)
        band = _band(bq, wk)
        lane = lax.broadcasted_iota(jnp.int32, (bq, LANES), 1)
        lo = lane < HEAD_DIM

        def block(i, carry):
            q0, k0 = _window(i, bq, w, wk, length)
            valid = jnp.abs(band + (q0 - k0)) <= w
            rows, krows = pl.ds(q0, bq), pl.ds(k0, wk)
            lse_tile = jnp.zeros((bq, LANES), F32)
            for first in range(0, N_HEADS // 2, PAIRS_PER_PHASE):
                pairs = range(first, first + PAIRS_PER_PHASE)
                heads = [2 * j + half for j in pairs for half in range(2)]
                qp = [qkv_ref[rows, j * LANES:(j + 1) * LANES] for j in pairs]
                k2 = {kv: _pair_operand(qkv_ref, kk_ref, Q_W, kv, krows) for kv in {j // 2 for j in pairs}}
                v2 = {kv: _pair_operand(qkv_ref, vv_ref, Q_W + KV_W, kv, krows) for kv in k2}
                sc2 = [lax.dot_general(q_, k2[j // 2], NT, preferred_element_type=F32) for q_, j in zip(qp, pairs)]
                sc = [jnp.where(valid, s_[:, half * wk:(half + 1) * wk], NEG_INF) for s_ in sc2 for half in range(2)]
                m = [jnp.max(s_, axis=-1, keepdims=True) for s_ in sc]
                if has_sink:
                    m = [jnp.maximum(m_, sink_ref[hd]) for m_, hd in zip(m, heads)]
                mb = [jnp.broadcast_to(m_, (bq, LANES)) for m_ in m]
                p = [jnp.exp2(s_ - _over_keys(m_, wk)) for s_, m_ in zip(sc, m)]
                den = [jnp.sum(p_, axis=-1, keepdims=True) for p_ in p]
                if has_sink:
                    den = [d_ + jnp.exp2(sink_ref[hd] - m_) for d_, m_, hd in zip(den, m, heads)]
                inv = [jnp.broadcast_to(1.0 / d_, (bq, LANES)) for d_ in den]
                pb = [p_.astype(BF16) for p_ in p]
                for n_, j in enumerate(pairs):
                    o = jnp.dot(jnp.concatenate([pb[2 * n_], pb[2 * n_ + 1]], axis=1), v2[j // 2], preferred_element_type=F32)
                    o = o * jnp.where(lo, inv[2 * n_], inv[2 * n_ + 1])
                    o_ref[rows, j * LANES:(j + 1) * LANES] = o.astype(BF16)
                for n_, hd in enumerate(heads):
                    lse_tile = jnp.where(lane == hd, mb[n_] - jnp.log(inv[n_]) * LOG2E, lse_tile)
            lse_ref[rows, :] = lse_tile
            return carry

        lax.fori_loop(0, nb, block, 0)

    def seq_block(c):
        return pl.BlockSpec((per_step, length, c), lambda i: (i, 0, 0))

    args = [rows_all]
    in_specs = [seq_block(QKV_W)]
    if has_sink:
        args.append(sink * LOG2E)
        in_specs.append(pl.BlockSpec(memory_space=pltpu.SMEM))
    (o, lse), exchanged = _hosted_call(
        body, exchange, name=f"attn_fwd_{tag}", grid=(nseq // per_step,), in_specs=in_specs,
        out_specs=[seq_block(Q_W), seq_block(LANES)],
        out_shape=[jax.ShapeDtypeStruct((nseq, length, Q_W), BF16), jax.ShapeDtypeStruct((nseq, length, LANES), F32)],
        scratch_shapes=[pltpu.VMEM((KV_W // LANES, length, LANES), BF16), pltpu.VMEM((KV_W // LANES, length, LANES), BF16)],
        semantics=("parallel",), args=args)
    return o.reshape(shape[:-1] + (Q_W,)), lse.reshape(shape[:-1] + (LANES,)), exchanged


def _head_expand():
    return (jnp.arange(LANES)[:, None] == jnp.arange(Q_W)[None, :] // HEAD_DIM).astype(BF16)


def _mix_groups(os, lses, dils, seq, tag):
    bl = os[0].shape[0]
    ts = _tile_rows(seq)
    t = bl * seq
    ng = len(os)
    if ng == 1 and dils[0] == 1:
        return os[0].reshape(t, Q_W), [lses[0]]

    def body(*refs):
        e_ref = refs[0]
        o_refs, l_refs = refs[1:1 + ng], refs[1 + ng:1 + 2 * ng]
        om_ref = refs[1 + 2 * ng]
        lt_refs = refs[2 + 2 * ng:2 + 3 * ng]
        wide_ref, narrow_ref = refs[2 + 3 * ng:]
        ls = [_merge_rows([l_refs[g][r] for r in range(dils[g])], narrow_ref, dils[g]) for g in range(ng)]
        mx = functools.reduce(jnp.maximum, ls)
        tot = mx + jnp.log(functools.reduce(lambda a, b: a + b, [jnp.exp2(l_ - mx) for l_ in ls])) * LOG2E
        e = e_ref[...]
        o = None
        for g in range(ng):
            wt = jnp.exp2(ls[g] - tot)
            hi = wt.astype(BF16)
            lo = (wt - hi.astype(F32)).astype(BF16)
            wide = jnp.dot(hi, e, preferred_element_type=F32) + jnp.dot(lo, e, preferred_element_type=F32)
            term = wide * _merge_rows([o_refs[g][r].astype(F32) for r in range(dils[g])], wide_ref, dils[g])
            o = term if o is None else o + term
        om_ref[...] = o.astype(BF16)
        for g in range(ng):
            for r, part in enumerate(_split_rows(tot, narrow_ref, dils[g])):
                lt_refs[g][r] = part

    e = _head_expand()
    outs = pl.pallas_call(
        body, name=f"mix_groups_{tag}", grid=(t // ts,),
        in_specs=[_resident(e.shape)] + [_res_spec(seq, dl, Q_W) for dl in dils] + [_res_spec(seq, dl, LANES) for dl in dils],
        out_specs=[pl.BlockSpec((ts, Q_W), lambda i: (i, 0))] + [_res_spec(seq, dl, LANES) for dl in dils],
        out_shape=[jax.ShapeDtypeStruct((t, Q_W), BF16)]
                  + [jax.ShapeDtypeStruct(_res_shape(bl, seq, dl, LANES), F32) for dl in dils],
        scratch_shapes=[_stage(ts, Q_W), _stage(ts, LANES)],
        compiler_params=_params("parallel"),
    )(e, *os, *lses)
    return outs[0], list(outs[1:])


def _sigmoid(g):
    return 1.0 / (1.0 + jnp.exp(-g))


def _ffn_fwd(x0, o, w_out, gain, wg, wu, wd, tag, exchange=(), loss_head=None):
    t, d = x0.shape
    f = wd.shape[0]
    tm = min(256, t)
    has_loss = loss_head is not None

    def body(*refs):
        x0_ref, o_ref, wo_ref, gain_ref, wg_ref, wu_ref, wd_ref = refs[:7]
        x_ref, y_ref, g_ref, u_ref, a_ref, h_ref = refs[-8:-2] if has_loss else refs[-6:]
        xv = x0_ref[...] + jnp.dot(o_ref[...], wo_ref[...], preferred_element_type=F32)
        x_ref[...] = xv
        h = (xv * _rms(xv) * gain_ref[...]).astype(BF16)
        h_ref[...] = h
        g = lax.dot_general(h, wg_ref[...], NT, preferred_element_type=F32)
        u = lax.dot_general(h, wu_ref[...], NT, preferred_element_type=F32)
        g_ref[...] = g.astype(BF16)
        u_ref[...] = u.astype(BF16)
        a = (g * _sigmoid(g) * u).astype(BF16)
        a_ref[...] = a
        y = xv + jnp.dot(a, wd_ref[...], preferred_element_type=F32)
        if not has_loss:
            y_ref[...] = y
            return
        head_ref, target_ref, loss_ref, dhead_ref = refs[7], refs[8], refs[-2], refs[-1]
        head = head_ref[...]
        yhat = y * _rms(y)
        err = yhat * head - target_ref[...]
        dout = err * (1.0 / d)
        y_ref[...] = _rms_bwd(dout, y, head)[0]
        first = pl.program_id(0) == 0
        part = 0.5 * jnp.sum(jnp.mean(err * err, axis=-1, keepdims=True), axis=0, keepdims=True)
        _accumulate(loss_ref, jnp.broadcast_to(part, loss_ref.shape), first)
        _accumulate(dhead_ref, jnp.sum(dout * yhat, axis=0, keepdims=True), first)

    row = pl.BlockSpec((tm, d), lambda i: (i, 0))
    wide = pl.BlockSpec((tm, f), lambda i: (i, 0))
    in_specs = [row, pl.BlockSpec((tm, Q_W), lambda i: (i, 0)), _resident(w_out.shape), _resident((1, d)), _resident(wg.shape),
                _resident(wu.shape), _resident(wd.shape)]
    out_specs = [row, row, wide, wide, wide, row]
    out_shape = ([jax.ShapeDtypeStruct((t, d), F32)] * 2 + [jax.ShapeDtypeStruct((t, f), BF16)] * 3
                 + [jax.ShapeDtypeStruct((t, d), BF16)])
    if has_loss:
        in_specs += [_resident((1, d)), row]
        out_specs += [pl.BlockSpec((1, LANES), lambda i: (0, 0)), pl.BlockSpec((1, d), lambda i: (0, 0))]
        out_shape += [jax.ShapeDtypeStruct((1, LANES), F32), jax.ShapeDtypeStruct((1, d), F32)]
    outs, exchanged = _hosted_call(
        body, exchange, name=f"ffn_fwd_{tag}", grid=(t // tm,), in_specs=in_specs, out_specs=out_specs, out_shape=out_shape,
        scratch_shapes=[], semantics=("arbitrary" if has_loss else "parallel",),
        args=(x0, o, w_out, gain, wg, wu, wd) + (tuple(loss_head) if has_loss else ()))
    return (*outs, exchanged)


def _ffn_bwd(dy, x, gain, g, u, wg, wu, wd, tag, exchange=()):
    t, d = x.shape
    f = wd.shape[0]
    tm = min(256, t)

    def body(dy_ref, x_ref, gain_ref, g_ref, u_ref, wg_ref, wu_ref, wd_ref, dx_ref, dg_ref, du_ref, dgain_ref):
        dyv = dy_ref[...]
        da = lax.dot_general(dyv.astype(BF16), wd_ref[...], NT, preferred_element_type=F32)
        gv, uv = g_ref[...].astype(F32), u_ref[...].astype(F32)
        sg = _sigmoid(gv)
        act = gv * sg
        du = (da * act).astype(BF16)
        dg = (da * uv * (sg * (1.0 + gv * (1.0 - sg)))).astype(BF16)
        du_ref[...] = du
        dg_ref[...] = dg
        dh = jnp.dot(dg, wg_ref[...], preferred_element_type=F32) + jnp.dot(du, wu_ref[...], preferred_element_type=F32)
        xv, gain_v = x_ref[...], gain_ref[...]
        dx, xhat = _rms_bwd(dh, xv, gain_v)
        dx_ref[...] = dyv + dx
        _accumulate(dgain_ref, jnp.sum(dh * xhat, axis=0, keepdims=True), pl.program_id(0) == 0)

    row = pl.BlockSpec((tm, d), lambda i: (i, 0))
    wide = pl.BlockSpec((tm, f), lambda i: (i, 0))
    outs, exchanged = _hosted_call(
        body, exchange, name=f"ffn_bwd_{tag}", grid=(t // tm,),
        in_specs=[row, row, _resident((1, d)), wide, wide, _resident(wg.shape), _resident(wu.shape), _resident(wd.shape)],
        out_specs=[row, wide, wide, pl.BlockSpec((1, d), lambda i: (0, 0))],
        out_shape=[jax.ShapeDtypeStruct((t, d), F32), jax.ShapeDtypeStruct((t, f), BF16), jax.ShapeDtypeStruct((t, f), BF16),
                   jax.ShapeDtypeStruct((1, d), F32)],
        scratch_shapes=[], semantics=("arbitrary",), args=(dy, x, gain, g, u, wg, wu, wd))
    return (*outs, exchanged)


def _tn_matmul(a, b, name, into=None, row_block=0, row_blocks=1):
    t, k = a.shape
    n = b.shape[1]
    tk = k // 2 if (k // 2) % LANES == 0 else k
    tt = min(2048, t)
    first = row_block * (k // tk)

    def body(a_ref, b_ref, *rest):
        o_ref, acc_ref = rest[-2:]
        prod = lax.dot_general(a_ref[...].astype(BF16), b_ref[...].astype(BF16), TN, preferred_element_type=F32)
        j = pl.program_id(1)

        @pl.when(j == 0)
        def _():
            acc_ref[...] = prod

        @pl.when(j > 0)
        def _():
            acc_ref[...] += prod

        @pl.when(j == pl.num_programs(1) - 1)
        def _():
            o_ref[...] = acc_ref[...].astype(BF16)

    return pl.pallas_call(
        body, name=name, grid=(k // tk, t // tt),
        in_specs=[pl.BlockSpec((tt, tk), lambda i, j: (j, i)), pl.BlockSpec((tt, n), lambda i, j: (j, 0))]
                 + ([ANY] if into is not None else []),
        out_specs=pl.BlockSpec((tk, n), lambda i, j: (first + i, 0)),
        out_shape=jax.ShapeDtypeStruct((row_blocks * k, n), BF16),
        scratch_shapes=[pltpu.VMEM((tk, n), F32)],
        input_output_aliases={2: 0} if into is not None else {},
        compiler_params=_params("parallel", "arbitrary"),
    )(a, b, *([into] if into is not None else []))


def _attn_out_bwd(dx, w, o, dils, seq, tag, lse=None, sink=None, exchange=()):
    t, d = dx.shape
    ts = _tile_rows(seq)
    bl = t // seq
    ng = len(dils)
    has_sink = sink is not None
    expand = _head_expand().T

    def body(*refs):
        refs = list(refs)
        dx_ref, w_ref, o_ref, e_ref = refs[:4]
        refs = refs[4:]
        lse_ref, sink_ref = (refs.pop(0), refs.pop(0)) if has_sink else (None, None)
        do_refs, dl_refs = refs[:ng], refs[ng:2 * ng]
        refs = refs[2 * ng:]
        dsink_ref = refs.pop(0) if has_sink else None
        dof_ref, dlf_ref = refs
        do = lax.dot_general(dx_ref[...].astype(BF16), w_ref[...], NT, preferred_element_type=F32)
        prod = do * o_ref[...].astype(F32)
        hi = prod.astype(BF16)
        lo = (prod - hi.astype(F32)).astype(BF16)
        e = e_ref[...]
        dl = jnp.dot(hi, e, preferred_element_type=F32) + jnp.dot(lo, e, preferred_element_type=F32)
        for g in range(ng):
            for r, part in enumerate(_split_rows(do, dof_ref, dils[g])):
                do_refs[g][r] = part.astype(BF16)
            for r, part in enumerate(_split_rows(dl, dlf_ref, dils[g])):
                dl_refs[g][r] = part
        if has_sink:
            part = -jnp.exp2(sink_ref[...] - lse_ref[...]) * dl
            _accumulate(dsink_ref, jnp.sum(part, axis=0, keepdims=True), pl.program_id(0) == 0)

    row = pl.BlockSpec((ts, d), lambda i: (i, 0))
    narrow = pl.BlockSpec((ts, LANES), lambda i: (i, 0))
    args = [dx, w, o, expand]
    in_specs = [row, _resident(w.shape), pl.BlockSpec((ts, Q_W), lambda i: (i, 0)), _resident(expand.shape)]
    if has_sink:
        args += [lse, jnp.pad(sink.reshape(1, N_HEADS) * LOG2E, ((0, 0), (0, LANES - N_HEADS)))]
        in_specs += [narrow, _resident((1, LANES))]
    out_specs = [_res_spec(seq, dl, Q_W) for dl in dils] + [_res_spec(seq, dl, LANES) for dl in dils]
    out_shape = ([jax.ShapeDtypeStruct(_res_shape(bl, seq, dl, Q_W), BF16) for dl in dils]
                 + [jax.ShapeDtypeStruct(_res_shape(bl, seq, dl, LANES), F32) for dl in dils])
    if has_sink:
        out_specs.append(pl.BlockSpec((1, LANES), lambda i: (0, 0)))
        out_shape.append(jax.ShapeDtypeStruct((1, LANES), F32))
    outs, exchanged = _hosted_call(
        body, exchange, name=f"attn_out_bwd_{tag}", grid=(t // ts,), in_specs=in_specs, out_specs=out_specs, out_shape=out_shape,
        scratch_shapes=[_stage(ts, Q_W), _stage(ts, LANES)], semantics=("arbitrary" if has_sink else "parallel",), args=args)
    return list(outs[:ng]), list(outs[ng:2 * ng]), (outs[2 * ng] if has_sink else None), exchanged


def _attn_bwd(qkv, do, lse, delta, cos, sin, w, tag, exchange=()):
    shape = qkv.shape
    dil = shape[1]
    rows_all = _seq_view(qkv)
    nseq, length, _ = rows_all.shape
    bq = min(QUERY_BLOCK, length)
    wk = _key_rows(bq, w, length)
    nb = length // bq
    per_step = _per_step(dil, length)

    def body(*refs):
        def sub(i, carry):
            one(*[ref.at[i] for ref in refs[:7]], *refs[7:])
            return carry

        if per_step == 1:
            sub(0, 0)
        else:
            lax.fori_loop(0, per_step, sub, 0)

    def one(qkv_ref, do_ref, lse_ref, dl_ref, cos_ref, sin_ref, dp_ref, kk_ref, vv_ref, dk_ref, dv_ref):
        _swap_halves(qkv_ref, Q_W, kk_ref)
        _swap_halves(qkv_ref, Q_W + KV_W, vv_ref)
        dk_ref[...] = jnp.zeros_like(dk_ref)
        dv_ref[...] = jnp.zeros_like(dv_ref)
        band = _band(bq, wk)
        lo_q = lax.broadcasted_iota(jnp.int32, (bq, LANES), 1) < HEAD_DIM
        hi_q = jnp.logical_not(lo_q)

        def block(i, carry):
            q0, k0 = _window(i, bq, w, wk, length)
            valid = jnp.abs(band + (q0 - k0)) <= w
            rows, krows = pl.ds(q0, bq), pl.ds(k0, wk)
            c, sn = cos_ref[rows, :], -sin_ref[rows, :]
            lse_t, dl_t = lse_ref[rows, :], dl_ref[rows, :]
            zero = jnp.zeros((bq, LANES), BF16)
            for first in range(0, N_KV, KV_PER_PHASE):
                kvs = range(first, first + KV_PER_PHASE)
                pairs = [kv * 2 + j for kv in kvs for j in range(GRP // 2)]
                heads = [2 * j + half for j in pairs for half in range(2)]
                qp = {j: qkv_ref[rows, j * LANES:(j + 1) * LANES] for j in pairs}
                dop = {j: do_ref[rows, j * LANES:(j + 1) * LANES] for j in pairs}
                k2 = {kv: _pair_operand(qkv_ref, kk_ref, Q_W, kv, krows) for kv in kvs}
                v2 = {kv: _pair_operand(qkv_ref, vv_ref, Q_W + KV_W, kv, krows) for kv in kvs}
                sc2 = {j: lax.dot_general(qp[j], k2[j // 2], NT, preferred_element_type=F32) for j in pairs}
                dp2 = {j: lax.dot_general(dop[j], v2[j // 2], NT, preferred_element_type=F32) for j in pairs}
                sc = {hd: sc2[hd // 2][:, (hd % 2) * wk:(hd % 2 + 1) * wk] for hd in heads}
                dp = {hd: dp2[hd // 2][:, (hd % 2) * wk:(hd % 2 + 1) * wk] for hd in heads}
                p = {hd: jnp.exp2(jnp.where(valid, sc[hd], NEG_INF) - _over_keys(lse_t[:, hd:hd + 1], wk)) for hd in heads}
                ds = {hd: (p[hd] * (dp[hd] - _over_keys(dl_t[:, hd:hd + 1], wk))).astype(BF16) for hd in heads}
                pb = {hd: p[hd].astype(BF16) for hd in heads}
                for j in pairs:
                    dq = jnp.dot(jnp.concatenate([ds[2 * j], ds[2 * j + 1]], axis=1), k2[j // 2], preferred_element_type=F32) * SCALE
                    dp_ref[rows, j * LANES:(j + 1) * LANES] = _rope(dq, c, sn).astype(BF16)
                for kv in kvs:
                    own = range(kv * GRP, (kv + 1) * GRP)
                    q4 = jnp.concatenate([jnp.where(lo_q if hd % 2 == 0 else hi_q, qp[hd // 2], zero) for hd in own], axis=0)
                    do4 = jnp.concatenate([jnp.where(lo_q if hd % 2 == 0 else hi_q, dop[hd // 2], zero) for hd in own], axis=0)
                    dk_ref[kv, krows, :] += lax.dot_general(jnp.concatenate([ds[hd] for hd in own], axis=0), q4, TN,
                                                            preferred_element_type=F32)
                    dv_ref[kv, krows, :] += lax.dot_general(jnp.concatenate([pb[hd] for hd in own], axis=0), do4, TN,
                                                            preferred_element_type=F32)
            return carry

        lax.fori_loop(0, nb, block, 0)
        lo = lax.broadcasted_iota(jnp.int32, (length, LANES), 1) < HEAD_DIM
        c, sn = cos_ref[...], -sin_ref[...]
        for ch in range(KV_W // LANES):
            halves = []
            for acc_ref in (dk_ref, dv_ref):
                even, odd = acc_ref[2 * ch], acc_ref[2 * ch + 1]
                even = even + pltpu.roll(even, HEAD_DIM, 1)
                odd = odd + pltpu.roll(odd, HEAD_DIM, 1)
                halves.append(jnp.where(lo, even, odd))
            dp_ref[:, Q_W + ch * LANES:Q_W + (ch + 1) * LANES] = _rope(halves[0] * LN2, c, sn).astype(BF16)
            dp_ref[:, Q_W + KV_W + ch * LANES:Q_W + KV_W + (ch + 1) * LANES] = halves[1].astype(BF16)

    def seq_block(c):
        return pl.BlockSpec((per_step, length, c), lambda i: (i, 0, 0))

    table = pl.BlockSpec((per_step, length, LANES), lambda i: (i % (dil // per_step), 0, 0))
    (out,), exchanged = _hosted_call(
        body, exchange, name=f"attn_bwd_{tag}", grid=(nseq // per_step,),
        in_specs=[seq_block(QKV_W), seq_block(Q_W), seq_block(LANES), seq_block(LANES), table, table],
        out_specs=[seq_block(QKV_W)],
        out_shape=[jax.ShapeDtypeStruct((nseq, length, QKV_W), BF16)],
        scratch_shapes=[pltpu.VMEM((KV_W // LANES, length, LANES), BF16), pltpu.VMEM((KV_W // LANES, length, LANES), BF16),
                        pltpu.VMEM((N_KV, length, LANES), F32), pltpu.VMEM((N_KV, length, LANES), F32)],
        semantics=("parallel",), args=(rows_all, _seq_view(do), _seq_view(lse), _seq_view(delta), cos, sin))
    return out.reshape(shape), exchanged


def _qkv_bwd(dy, x, gain, w, dps, dils, seq, tag, exchange=()):
    t, d = x.shape
    ts = _tile_rows(seq)
    ng = len(dps)

    def body(dy_ref, x_ref, gain_ref, w_ref, *refs):
        dp_refs, (dx_ref, dgain_ref, stage_ref) = refs[:ng], refs[ng:]
        dh = None
        for gi in range(ng):
            dil = dils[gi]
            n = ts // dil
            dp = dp_refs[gi][0] if dil == 1 else jnp.concatenate([dp_refs[gi][r] for r in range(dil)], axis=0)
            part = jnp.dot(dp, w_ref[gi * QKV_W:(gi + 1) * QKV_W, :], preferred_element_type=F32)
            part = _merge_rows([part[r * n:(r + 1) * n] for r in range(dil)], stage_ref, dil)
            dh = part if dh is None else dh + part
        xv, gain_v = x_ref[...], gain_ref[...]
        dx, xhat = _rms_bwd(dh, xv, gain_v)
        dx_ref[...] = dy_ref[...] + dx
        _accumulate(dgain_ref, jnp.sum(dh * xhat, axis=0, keepdims=True), pl.program_id(0) == 0)

    row = pl.BlockSpec((ts, d), lambda i: (i, 0))
    (dx, dgain), exchanged = _hosted_call(
        body, exchange, name=f"qkv_bwd_{tag}", grid=(t // ts,),
        in_specs=[row, row, _resident((1, d)), _resident(w.shape)] + [_res_spec(seq, dl, QKV_W) for dl in dils],
        out_specs=[row, pl.BlockSpec((1, d), lambda i: (0, 0))],
        out_shape=[jax.ShapeDtypeStruct((t, d), F32), jax.ShapeDtypeStruct((1, d), F32)],
        scratch_shapes=[_stage(ts, d)], semantics=("arbitrary",), args=(dy, x, gain, w, *dps))
    return dx, dgain, exchanged


ANY = pl.BlockSpec(memory_space=pl.ANY)


def _place():
    x, y, c = lax.axis_index("x"), lax.axis_index("y"), lax.axis_index("c")
    return x, y, c


def _exchange_steps(srcs, dsts, gather, send_sems, recv_sems, local_sems):
    x, y, c = _place()
    me, sibling = (x, y, c), (x, y, 1 - c)
    chips = [(1 - x, y), (x, 1 - y), (1 - x, 1 - y)]
    mine = 4 * x + 2 * y + c

    def slot(a, device):
        px, py, pc = device
        return dsts[a].at[4 * px + 2 * py + pc]

    def passes(a, k, block, to, src=None):
        rows = slot(a, block)
        return pltpu.make_async_remote_copy(src_ref=rows if src is None else src, dst_ref=rows, send_sem=send_sems.at[a, k],
                                            recv_sem=recv_sems.at[a, k], device_id=to, device_id_type=MESH)

    def scatters(a, k):
        peer = mine ^ k
        return pltpu.make_async_remote_copy(
            src_ref=srcs[a].at[peer], dst_ref=dsts[a].at[mine], send_sem=send_sems.at[a, k - 1], recv_sem=recv_sems.at[a, k - 1],
            device_id=(peer // 4, (peer // 2) % 2, peer % 2), device_id_type=MESH)

    def local(a):
        return pltpu.make_async_copy(srcs[a] if gather[a] else srcs[a].at[mine], dsts[a].at[mine], local_sems.at[a])

    def first_copies(a):
        if not gather[a]:
            return [scatters(a, k) for k in range(1, N_DEV)]
        return [passes(a, 0, me, sibling, src=srcs[a])] + [passes(a, 1 + j, me, (*chip, c), src=srcs[a]) for j, chip in enumerate(chips)]

    def start():
        for a in range(len(srcs)):
            local(a).start()
            for cp in first_copies(a):
                cp.start()

    def forward():
        for a in range(len(srcs)):
            if gather[a]:
                for j, chip in enumerate(chips):
                    passes(a, 1 + j, (*chip, c), me).wait_recv()
                    passes(a, 4 + j, (*chip, c), sibling).start()

    def finish():
        for a in range(len(srcs)):
            if gather[a]:
                passes(a, 0, sibling, me).wait_recv()
                for j, chip in enumerate(chips):
                    passes(a, 4 + j, (*chip, 1 - c), me).wait_recv()
                    passes(a, 4 + j, (*chip, c), sibling).wait_send()
                for cp in first_copies(a):
                    cp.wait_send()
            else:
                for cp in first_copies(a):
                    cp.wait()
            local(a).wait()

    return start, forward, finish


def _exchange_scratch(n):
    return [pltpu.SemaphoreType.DMA((n, N_DEV - 1)), pltpu.SemaphoreType.DMA((n, N_DEV - 1)), pltpu.SemaphoreType.DMA((n,))]


def _exchanged_shapes(exchange):
    return [jax.ShapeDtypeStruct(((N_DEV,) + a.shape) if g else a.shape, a.dtype) for a, g in exchange]


def _hosted_call(body, exchange, *, name, grid, in_specs, out_specs, out_shape, scratch_shapes, semantics, args):
    out_specs, out_shape, scratch = list(out_specs), list(out_shape), list(scratch_shapes)
    if not exchange:
        outs = pl.pallas_call(body, name=name, grid=grid, in_specs=in_specs, out_specs=out_specs, out_shape=out_shape,
                              scratch_shapes=scratch, compiler_params=_params(*semantics))(*args)
        return list(outs), []
    n, n_in, n_out, n_scr = len(exchange), len(in_specs), len(out_specs), len(scratch)
    gather = [g for _, g in exchange]
    steps = math.prod(grid)

    def hosted(*refs):
        own_in, x_in = refs[:n_in], refs[n_in:n_in + n]
        own_out, x_out = refs[n_in + n:n_in + n + n_out], refs[n_in + n + n_out:n_in + 2 * n + n_out]
        own_scr, sems = refs[n_in + 2 * n + n_out:n_in + 2 * n + n_out + n_scr], refs[-3:]
        step = pl.program_id(0)
        for axis in range(1, len(grid)):
            step = step * grid[axis] + pl.program_id(axis)
        start, forward, finish = _exchange_steps(x_in, x_out, gather, *sems)
        pl.when(step == 0)(start)
        body(*own_in, *own_out, *own_scr)
        pl.when(step == steps // 2)(forward)
        pl.when(step == steps - 1)(finish)

    outs = pl.pallas_call(
        hosted, name=name, grid=grid, in_specs=list(in_specs) + [ANY] * n, out_specs=out_specs + [ANY] * n,
        out_shape=out_shape + _exchanged_shapes(exchange), scratch_shapes=scratch + _exchange_scratch(n),
        compiler_params=_params(*["arbitrary"] * len(grid)),
    )(*args, *[a for a, _ in exchange])
    return list(outs[:n_out]), list(outs[n_out:])


def _exchange_now(exchange, name):
    n = len(exchange)
    gather = [g for _, g in exchange]

    def body(*refs):
        for step in _exchange_steps(refs[:n], refs[n:2 * n], gather, *refs[2 * n:]):
            step()

    return pl.pallas_call(
        body, name=name, in_specs=[ANY] * n, out_specs=[ANY] * n, out_shape=_exchanged_shapes(exchange),
        scratch_shapes=_exchange_scratch(n),
    )(*[a for a, _ in exchange])


def _all_reduce_small(v):
    def body(v_ref, o_ref, recv_ref, send_sems, recv_sems):
        x, y, c = _place()
        me = 4 * x + 2 * y + c
        copies = []
        for k in range(1, N_DEV):
            peer = me ^ k
            copies.append(pltpu.make_async_remote_copy(
                src_ref=v_ref, dst_ref=recv_ref.at[k], send_sem=send_sems.at[k - 1], recv_sem=recv_sems.at[k - 1],
                device_id=(peer // 4, (peer // 2) % 2, peer % 2), device_id_type=MESH))
        for cp in copies:
            cp.start()
        recv_ref[0] = v_ref[...]
        for cp in copies:
            cp.wait()
        acc = recv_ref[me]
        for src in range(1, N_DEV):
            acc = acc + recv_ref[me ^ src]
        o_ref[...] = acc

    vm = pl.BlockSpec(memory_space=pltpu.VMEM)
    return pl.pallas_call(
        body, name="all_reduce_small", in_specs=[vm], out_specs=vm, out_shape=jax.ShapeDtypeStruct(v.shape, F32),
        scratch_shapes=[pltpu.VMEM((N_DEV,) + v.shape, F32), pltpu.SemaphoreType.DMA((N_DEV - 1,)),
                        pltpu.SemaphoreType.DMA((N_DEV - 1,))],
    )(v)


def _adamw_math(w, g, m, v):
    m = ADAM_B1 * m + (1.0 - ADAM_B1) * g
    v = ADAM_B2 * v + (1.0 - ADAM_B2) * (g * g)
    m_hat = m / (1.0 - ADAM_B1 ** ADAM_STEP)
    v_hat = v / (1.0 - ADAM_B2 ** ADAM_STEP)
    delta = -ADAM_LR * (m_hat / (jnp.sqrt(v_hat) + ADAM_EPS) + ADAM_WD * w)
    return delta, m, v


def _adamw(parts, w, m, v, name, layer=None, into=None):
    r, c = w.shape[-2:]
    tr = r // 2 if r % 16 == 0 and r >= 256 else r
    n = len(parts)

    def body(*refs):
        w_ref, m_ref, v_ref = refs[n:n + 3]
        g_ref, d_ref, nm_ref, nv_ref = refs[-4:]
        g = refs[0][...].astype(F32)
        for p_ref in refs[1:n]:
            g = g + p_ref[...].astype(F32)
        g_ref[...] = g
        d_ref[...], nm_ref[...], nv_ref[...] = _adamw_math(w_ref[...], g, m_ref[...], v_ref[...])

    def slab(slot):
        return pl.BlockSpec((None, tr, c), lambda i: (slot, i, 0))

    tile = pl.BlockSpec((tr, c), lambda i: (i, 0)) if layer is None else slab(layer)
    arrays, in_specs = [], []
    for p in parts:
        if isinstance(p, tuple):
            arrays.append(p[0])
            in_specs.append(slab(p[1]))
        else:
            arrays.append(p)
            in_specs.append(tile)
    kept = list(into) if into is not None else []
    return pl.pallas_call(
        body, name=name, grid=(r // tr,), in_specs=in_specs + [tile] * 3 + [ANY] * len(kept), out_specs=[tile] * 4,
        out_shape=[jax.ShapeDtypeStruct(w.shape, F32)] * 4,
        input_output_aliases={n + 3 + k: k for k in range(len(kept))}, compiler_params=_params("parallel"),
    )(*arrays, w, m, v, *kept)


def _rows(g):
    return g.reshape(-1, g.shape[-1])


def _row_blocks(dw):
    k, n = dw.shape
    return dw.reshape(N_DEV, k // N_DEV, n)


def _pack_rows(rows, width):
    out = None
    for i, r in enumerate(rows):
        r = r.reshape(1, -1).astype(F32)
        r = jnp.pad(r, ((i, 8 - 1 - i), (0, width - r.shape[1])))
        out = r if out is None else out + r
    return out


def _mixer_fwd(x, gain, w_in, cos, sin, seq, groups, tag, sink=None, exchanges=None):
    exchanges = exchanges or {}
    os, lses, got = [], [], {}
    qkvs, hs, got["proj"] = _qkv_proj(x, gain, w_in, cos, sin, seq, [dil for dil, _ in groups], tag,
                                      exchange=exchanges.get("proj", ()))
    for gi, (dil, w) in enumerate(groups):
        o, lse, got[gi] = _attn_fwd(qkvs[gi], w, f"{tag}{gi}", sink=sink, exchange=exchanges.get(gi, ()))
        os.append(o)
        lses.append(lse)
    o, lses = _mix_groups(os, lses, [dl for dl, _ in groups], seq, tag)
    return (qkvs, hs, o, lses), got


def _mixer_bwd(dy, x_in, gain, w_in, w_out, saved, cos, sin, seq, groups, tag, sink=None, exchanges=None, scatter_own=False):
    qkvs, hs, o, lses = saved
    t, d = x_in.shape
    dils = [dl for dl, _ in groups]
    lse_tokens = lses[0].reshape(t, LANES) if sink is not None else None
    dw_out = _tn_matmul(o, dy, f"dw_out_{tag}")
    dos, dls, dsink, early = _attn_out_bwd(dy, w_out, o, dils, seq, tag, lse=lse_tokens, sink=sink,
                                           exchange=_to_send([dw_out]) if scatter_own else ())
    if scatter_own:
        (dw_out,) = early
    exchanges = exchanges or {}
    dps, got = [], {}
    for gi, (dil, w) in enumerate(groups):
        dp, got[gi] = _attn_bwd(qkvs[gi], dos[gi], lses[gi], dls[gi], _tables_by_residue(cos, seq, dil),
                                _tables_by_residue(sin, seq, dil), w, f"{tag}{gi}", exchange=exchanges.get(gi, ()))
        dps.append(dp)
    dw_in = None
    for gi in range(len(groups)):
        dw_in = _tn_matmul(dps[gi].reshape(t, QKV_W), hs[gi].reshape(t, d), f"dw_in_{tag}{gi}", into=dw_in, row_block=gi,
                           row_blocks=len(groups))
    dx, dgain, late = _qkv_bwd(dy, x_in, gain, w_in, dps, dils, seq, tag, exchange=_to_send([dw_in]) if scatter_own else ())
    if scatter_own:
        (dw_in,) = late
    return dx, dw_in, dw_out, dgain, dsink, got


def _ffn_layer_bwd(dy, x_in, gain, saved, wg, wu, wd, tag, exchange=()):
    g, u, act, h = saved
    dx, dg, du, dgain, got = _ffn_bwd(dy, x_in, gain, g, u, wg, wu, wd, tag, exchange=exchange)
    dwd = _tn_matmul(act, dy, f"dw_down_{tag}")
    dwg = _tn_matmul(dg, h, f"dw_gate_{tag}")
    dwu = _tn_matmul(du, h, f"dw_up_{tag}")
    return dx, dwg, dwu, dwd, dgain, got


def _to_send(dws):
    return [(_row_blocks(g), False) for g in dws]


def kernel(x, a_w_in, a_sink, a_w_out, b_w_in, b_w_out, norm_mix, norm_ffn, w_gate, w_up, w_down, final_norm, loss_target, m_a_w_in, m_a_sink, m_a_w_out, m_b_w_in, m_b_w_out, m_norm_mix, m_norm_ffn, m_w_gate, m_w_up, m_w_down, m_final_norm, v_a_w_in, v_a_sink, v_a_w_out, v_b_w_in, v_b_w_out, v_norm_mix, v_norm_ffn, v_w_gate, v_w_up, v_w_down, v_final_norm):
    bl, seq, d = x.shape
    t = bl * seq
    xf = x.reshape(t, d)
    target = loss_target.reshape(t, d)
    cos, sin = _rope_tables(seq)
    groups_a = [(1, ATTN_HALF_WINDOW)]
    groups_b = [(dil, window // 2 // dil) for window, dil in DILATED_GROUPS]

    def flip(w_):
        return jnp.swapaxes(w_, -1, -2)

    a_w_in, m_a_w_in, v_a_w_in, b_w_in, m_b_w_in, v_b_w_in = map(flip, (a_w_in, m_a_w_in, v_a_w_in, b_w_in, m_b_w_in, v_b_w_in))
    w_gate, m_w_gate, v_w_gate, w_up, m_w_up, v_w_up = map(flip, (w_gate, m_w_gate, v_w_gate, w_up, m_w_up, v_w_up))

    def shard(w_, layer):
        return (w_[layer].astype(BF16), True)

    (wa_in,) = map(_rows, _exchange_now([shard(a_w_in, 0)], "gather_first"))

    saved_a, got = _mixer_fwd(xf, norm_mix[0:1], wa_in, cos, sin, seq, groups_a, "a", sink=a_sink[0],
                              exchanges={"proj": [shard(w_down, 0), shard(a_w_out, 0)], 0: [shard(w_gate, 0), shard(w_up, 0)]})
    wg0, wu0, wd0, wa_out = map(_rows, got[0] + got["proj"])
    x1_0, x2_0, *saved_0, got = _ffn_fwd(xf, saved_a[2], wa_out, norm_ffn[0:1], wg0, wu0, wd0, "0",
                                         exchange=[shard(b_w_in, 0), shard(b_w_out, 0)])
    wb_in, wb_out = map(_rows, got)
    saved_b, got = _mixer_fwd(x2_0, norm_mix[1:2], wb_in, cos, sin, seq, groups_b, "b",
                              exchanges={0: [shard(w_gate, 1)], 1: [shard(w_up, 1)], 2: [shard(w_down, 1)]})
    wg1, wu1, wd1 = map(_rows, got[0] + got[1] + got[2])
    x1_1, dy, *saved_1, loss_part, d_final, _ = _ffn_fwd(x2_0, saved_b[2], wb_out, norm_ffn[1:2], wg1, wu1, wd1, "1",
                                                         loss_head=(final_norm.reshape(1, d), target))

    dy, dwg1, dwu1, dwd1, d_nf1, _ = _ffn_layer_bwd(dy, x1_1, norm_ffn[1:2], saved_1, wg1, wu1, wd1, "1")
    dy, dwb_in, dwb_out, d_nm1, _, got = _mixer_bwd(
        dy, x2_0, norm_mix[1:2], wb_in, wb_out, saved_b, cos, sin, seq, groups_b, "b",
        exchanges={0: _to_send([dwg1, dwd1]), 1: _to_send([dwu1])})
    (r_g1, r_d1), (r_u1,) = got[0], got[1]
    dy, dwg0, dwu0, dwd0, d_nf0, (r_b_in, r_b_out) = _ffn_layer_bwd(
        dy, x1_0, norm_ffn[0:1], saved_0, wg0, wu0, wd0, "0", exchange=_to_send([dwb_in, dwb_out]))
    dy, r_a_in, r_a_out, d_nm0, d_sink, got = _mixer_bwd(
        dy, xf, norm_mix[0:1], wa_in, wa_out, saved_a, cos, sin, seq, groups_a, "a", sink=a_sink[0],
        exchanges={0: _to_send([dwg0, dwu0, dwd0])}, scatter_own=True)
    r_g0, r_u0, r_d0 = got[0]
    grad_x = dy.reshape(bl, seq, d)

    def update(received, w_, m_, v_, name):
        out = None
        for layer in reversed(range(len(received))):
            out = _adamw([(received[layer], src) for src in range(N_DEV)], w_, m_, v_, f"adamw_{name}{layer}", layer=layer, into=out)
        return out

    u_a_in = update([r_a_in], a_w_in, m_a_w_in, v_a_w_in, "a_in")
    u_a_out = update([r_a_out], a_w_out, m_a_w_out, v_a_w_out, "a_out")
    u_b_in = update([r_b_in], b_w_in, m_b_w_in, v_b_w_in, "b_in")
    u_b_out = update([r_b_out], b_w_out, m_b_w_out, v_b_w_out, "b_out")
    u_gate = update([r_g0, r_g1], w_gate, m_w_gate, v_w_gate, "gate")
    u_up = update([r_u0, r_u1], w_up, m_w_up, v_w_up, "up")
    u_down = update([r_d0, r_d1], w_down, m_w_down, v_w_down, "down")

    small = _pack_rows([d_nm0, d_nm1, d_nf0, d_nf1, d_final, d_sink, loss_part], d)
    total = _all_reduce_small(small)
    small_w = _pack_rows([norm_mix[0], norm_mix[1], norm_ffn[0], norm_ffn[1], final_norm, a_sink], d)
    small_m = _pack_rows([m_norm_mix[0], m_norm_mix[1], m_norm_ffn[0], m_norm_ffn[1], m_final_norm, m_a_sink], d)
    small_v = _pack_rows([v_norm_mix[0], v_norm_mix[1], v_norm_ffn[0], v_norm_ffn[1], v_final_norm, v_a_sink], d)
    u_small = _adamw([total], small_w, small_m, small_v, "adamw_small")
    loss = total[6, 0]

    outs = []
    for k in range(4):
        sm = u_small[k]
        outs += [flip(u_a_in[k]), sm[5:6, :N_HEADS], u_a_out[k], flip(u_b_in[k]), u_b_out[k], sm[0:2], sm[2:4],
                 flip(u_gate[k]), flip(u_up[k]), u_down[k], sm[4]]
    return (loss, grad_x, *outs)
```

```python
import functools
import math

import jax
import jax.numpy as jnp
from jax import lax
from jax.experimental import pallas as pl
from jax.experimental.pallas import tpu as pltpu

F32 = jnp.float32
BF16 = jnp.bfloat16

HEAD_DIM = 64
N_HEADS = 16
N_KV = 4
GRP = N_HEADS // N_KV
Q_W = N_HEADS * HEAD_DIM
KV_W = N_KV * HEAD_DIM
QKV_W = Q_W + 2 * KV_W
ATTN_HALF_WINDOW = 128
DILATED_GROUPS = ((128, 1), (512, 4), (2048, 16))
ROPE_THETA = 10000.0
RMS_EPS = 1e-6
NEG_INF = -1e30
SCALE = 1.0 / math.sqrt(HEAD_DIM)
LOG2E = 1.0 / math.log(2.0)
LN2 = math.log(2.0)

ADAM_LR = 0.001
ADAM_B1 = 0.9
ADAM_B2 = 0.999
ADAM_EPS = 1e-08
ADAM_WD = 0.01
ADAM_STEP = 10

LANES = 128
VMEM_LIMIT = 56 * 1024 * 1024
QUERY_BLOCK = 128
PAIRS_PER_PHASE = 4
KV_PER_PHASE = 2
N_DEV = 8
MESH = pl.DeviceIdType.MESH

NT = (((1,), (1,)), ((), ()))
TN = (((0,), (0,)), ((), ()))


def _params(*sem):
    return pltpu.CompilerParams(dimension_semantics=tuple(sem) if sem else None, vmem_limit_bytes=VMEM_LIMIT)


def _resident(shape):
    return pl.BlockSpec(shape, lambda *_: (0,) * len(shape), pipeline_mode=pl.Buffered(1))


def _rope_tables(seq):
    inv_freq = 1.0 / (ROPE_THETA ** (jnp.arange(0, HEAD_DIM, 2, dtype=F32) / HEAD_DIM))
    ang = jnp.arange(seq, dtype=F32)[:, None] * inv_freq[None, :]
    cos, sin = jnp.cos(ang), jnp.sin(ang)
    return jnp.tile(cos, (1, 4)), jnp.concatenate([-sin, sin, -sin, sin], axis=1)


def _rope(t, cos, sin_signed):
    lane = lax.broadcasted_iota(jnp.int32, t.shape, 1)
    first = (lane & (HEAD_DIM // 2)) == 0
    swapped = jnp.where(first, pltpu.roll(t, LANES - HEAD_DIM // 2, 1), pltpu.roll(t, HEAD_DIM // 2, 1))
    return t * cos + swapped * sin_signed


def _rms(x):
    return lax.rsqrt(jnp.mean(x * x, axis=-1, keepdims=True) + RMS_EPS)


def _rms_bwd(dh, x, gain):
    r = _rms(x)
    xhat = x * r
    dxh = dh * gain
    dx = r * (dxh - xhat * jnp.mean(dxh * xhat, axis=-1, keepdims=True))
    return dx, xhat


def _accumulate(ref, value, first):
    @pl.when(first)
    def _():
        ref[...] = jnp.zeros_like(ref)

    ref[...] += value


def _tile_rows(seq):
    return min(512, seq)


def _res_shape(bl, seq, dil, c):
    ts = _tile_rows(seq)
    return (bl, dil, seq // ts, ts // dil, c)


def _res_spec(seq, dil, c):
    ts = _tile_rows(seq)
    per_seq = seq // ts
    return pl.BlockSpec((None, dil, None, ts // dil, c), lambda i: (i // per_seq, 0, i % per_seq, 0, 0))


def _seq_view(a):
    bl, dil, tiles, n, c = a.shape
    return a.reshape(bl * dil, tiles * n, c)


def _stage(ts, c):
    return pltpu.VMEM((c // LANES, ts, LANES), F32)


def _split_rows(val, stage_ref, dil):
    if dil == 1:
        return [val]
    ts, c = val.shape
    n, nc = ts // dil, c // LANES
    for k in range(nc):
        stage_ref[k] = val[:, k * LANES:(k + 1) * LANES]
    return [jnp.concatenate([stage_ref[k, pl.ds(r, n, stride=dil), :] for k in range(nc)], axis=1) for r in range(dil)]


def _merge_rows(parts, stage_ref, dil):
    if dil == 1:
        return parts[0]
    n, c = parts[0].shape
    nc = c // LANES
    for r, part in enumerate(parts):
        for k in range(nc):
            stage_ref[k, pl.ds(r, n, stride=dil), :] = part[:, k * LANES:(k + 1) * LANES]
    return jnp.concatenate([stage_ref[k] for k in range(nc)], axis=1)


def _tables_tiled(table, seq, dil):
    ts = _tile_rows(seq)
    return table.reshape(seq // ts, ts // dil, dil, LANES).transpose(0, 2, 1, 3).reshape(seq, LANES)


def _tables_by_residue(table, seq, dil):
    return table.reshape(seq // dil, dil, LANES).transpose(1, 0, 2)


def _qkv_proj(x, gain, w, cos, sin, seq, dils, tag, exchange=()):
    t, d = x.shape
    ts = _tile_rows(seq)
    per_seq = seq // ts
    ng = len(dils)
    tables = [t_ for dil in dils for t_ in (_tables_tiled(cos, seq, dil), _tables_tiled(sin, seq, dil))]

    def body(x_ref, g_ref, w_ref, *refs):
        table_refs, o_refs, h_refs, stage_ref = refs[:2 * ng], refs[2 * ng:3 * ng], refs[3 * ng:4 * ng], refs[4 * ng]
        xv = x_ref[...]
        h_tokens = xv * _rms(xv) * g_ref[...]
        hs = [jnp.concatenate(_split_rows(h_tokens, stage_ref, dil), axis=0).astype(BF16) for dil in dils]
        accs = [lax.dot_general(h, w_ref[gi * QKV_W:(gi + 1) * QKV_W, :], NT, preferred_element_type=F32)
                for gi, h in enumerate(hs)]
        for gi, dil in enumerate(dils):
            n = ts // dil
            h, acc = hs[gi], accs[gi]
            for r in range(dil):
                h_refs[gi][r] = h[r * n:(r + 1) * n]
            c, s = table_refs[2 * gi][...], table_refs[2 * gi + 1][...]
            for j in range(QKV_W // LANES):
                cols = slice(j * LANES, (j + 1) * LANES)
                val = acc[:, cols]
                if j < (Q_W + KV_W) // LANES:
                    val = _rope(val, c, s)
                if j < Q_W // LANES:
                    val = val * (SCALE * LOG2E)
                val = val.astype(BF16)
                for r in range(dil):
                    o_refs[gi][r, :, cols] = val[r * n:(r + 1) * n]

    table = pl.BlockSpec((ts, LANES), lambda i: (i % per_seq, 0))
    outs, exchanged = _hosted_call(
        body, exchange, name=f"qkv_proj_{tag}", grid=(t // ts,),
        in_specs=[pl.BlockSpec((ts, d), lambda i: (i, 0)), _resident((1, d)), _resident(w.shape)] + [table] * (2 * ng),
        out_specs=[_res_spec(seq, dil, QKV_W) for dil in dils] + [_res_spec(seq, dil, d) for dil in dils],
        out_shape=[jax.ShapeDtypeStruct(_res_shape(t // seq, seq, dil, QKV_W), BF16) for dil in dils]
                  + [jax.ShapeDtypeStruct(_res_shape(t // seq, seq, dil, d), BF16) for dil in dils],
        scratch_shapes=[_stage(ts, d)], semantics=("parallel",), args=(x, gain, w, *tables))
    return outs[:ng], outs[ng:], exchanged


def _band(bq, wk):
    return lax.broadcasted_iota(jnp.int32, (bq, wk), 0) - lax.broadcasted_iota(jnp.int32, (bq, wk), 1)


def _swap_halves(src_ref, base, dst_ref):
    for c in range(KV_W // LANES):
        dst_ref[c] = pltpu.roll(src_ref[:, base + c * LANES:base + (c + 1) * LANES], HEAD_DIM, 1)


def _pair_operand(src_ref, swapped_ref, base, kv, rows):
    c = kv // 2
    chunk, swapped = src_ref[rows, base + c * LANES:base + (c + 1) * LANES], swapped_ref[c, rows, :]
    lo = lax.broadcasted_iota(jnp.int32, chunk.shape, 1) < HEAD_DIM
    zero = jnp.zeros_like(chunk)
    if kv % 2 == 0:
        return jnp.concatenate([jnp.where(lo, chunk, zero), jnp.where(lo, zero, swapped)], axis=0)
    return jnp.concatenate([jnp.where(lo, swapped, zero), jnp.where(lo, zero, chunk)], axis=0)


def _over_keys(col, wk):
    if wk % LANES:
        return jnp.broadcast_to(col, (col.shape[0], wk))
    wide = jnp.broadcast_to(col, (col.shape[0], LANES))
    return wide if wk == LANES else jnp.concatenate([wide] * (wk // LANES), axis=1)


def _per_step(dil, length):
    return max(1, min(dil, 512 // length))


def _key_rows(bq, w, length):
    return min(bq + 2 * w, length)


def _window(i, bq, w, wk, length):
    q0 = pl.multiple_of(i * bq, bq)
    k0 = pl.multiple_of(jnp.clip(q0 - w, 0, length - wk), min(w, bq))
    return q0, k0


def _attn_fwd(qkv, w, tag, sink=None, exchange=()):
    shape = qkv.shape
    rows_all = _seq_view(qkv)
    nseq, length, _ = rows_all.shape
    bq = min(QUERY_BLOCK, length)
    wk = _key_rows(bq, w, length)
    nb = length // bq
    has_sink = sink is not None
    per_step = _per_step(shape[1], length)

    def body(*refs):
        sink_ref = refs[1] if has_sink else None
        kk_ref, vv_ref = refs[-2:]
        for sub in range(per_step):
            one(refs[0].at[sub], refs[-4].at[sub], refs[-3].at[sub], sink_ref, kk_ref, vv_ref)

    def one(qkv_ref, o_ref, lse_ref, sink_ref, kk_ref, vv_ref):
        _swap_halves(qkv_ref, Q_W, kk_ref)
        _swap_halves(qkv_ref, Q_W + KV_W, vv_ref)
        band = _band(bq, wk)
        lane = lax.broadcasted_iota(jnp.int32, (bq, LANES), 1)
        lo = lane < HEAD_DIM

        def block(i, carry):
            q0, k0 = _window(i, bq, w, wk, length)
            valid = jnp.abs(band + (q0 - k0)) <= w
            rows, krows = pl.ds(q0, bq), pl.ds(k0, wk)
            lse_tile = jnp.zeros((bq, LANES), F32)
            for first in range(0, N_HEADS // 2, PAIRS_PER_PHASE):
                pairs = range(first, first + PAIRS_PER_PHASE)
                heads = [2 * j + half for j in pairs for half in range(2)]
                qp = [qkv_ref[rows, j * LANES:(j + 1) * LANES] for j in pairs]
                k2 = {kv: _pair_operand(qkv_ref, kk_ref, Q_W, kv, krows) for kv in {j // 2 for j in pairs}}
                v2 = {kv: _pair_operand(qkv_ref, vv_ref, Q_W + KV_W, kv, krows) for kv in k2}
                sc2 = [lax.dot_general(q_, k2[j // 2], NT, preferred_element_type=F32) for q_, j in zip(qp, pairs)]
                sc = [jnp.where(valid, s_[:, half * wk:(half + 1) * wk], NEG_INF) for s_ in sc2 for half in range(2)]
                m = [jnp.max(s_, axis=-1, keepdims=True) for s_ in sc]
                if has_sink:
                    m = [jnp.maximum(m_, sink_ref[hd]) for m_, hd in zip(m, heads)]
                mb = [jnp.broadcast_to(m_, (bq, LANES)) for m_ in m]
                p = [jnp.exp2(s_ - _over_keys(m_, wk)) for s_, m_ in zip(sc, m)]
                den = [jnp.sum(p_, axis=-1, keepdims=True) for p_ in p]
                if has_sink:
                    den = [d_ + jnp.exp2(sink_ref[hd] - m_) for d_, m_, hd in zip(den, m, heads)]
                inv = [jnp.broadcast_to(1.0 / d_, (bq, LANES)) for d_ in den]
                pb = [p_.astype(BF16) for p_ in p]
                for n_, j in enumerate(pairs):
                    o = jnp.dot(jnp.concatenate([pb[2 * n_], pb[2 * n_ + 1]], axis=1), v2[j // 2], preferred_element_type=F32)
                    o = o * jnp.where(lo, inv[2 * n_], inv[2 * n_ + 1])
                    o_ref[rows, j * LANES:(j + 1) * LANES] = o.astype(BF16)
                for n_, hd in enumerate(heads):
                    lse_tile = jnp.where(lane == hd, mb[n_] - jnp.log(inv[n_]) * LOG2E, lse_tile)
            lse_ref[rows, :] = lse_tile
            return carry

        lax.fori_loop(0, nb, block, 0)

    def seq_block(c):
        return pl.BlockSpec((per_step, length, c), lambda i: (i, 0, 0))

    args = [rows_all]
    in_specs = [seq_block(QKV_W)]
    if has_sink:
        args.append(sink * LOG2E)
        in_specs.append(pl.BlockSpec(memory_space=pltpu.SMEM))
    (o, lse), exchanged = _hosted_call(
        body, exchange, name=f"attn_fwd_{tag}", grid=(nseq // per_step,), in_specs=in_specs,
        out_specs=[seq_block(Q_W), seq_block(LANES)],
        out_shape=[jax.ShapeDtypeStruct((nseq, length, Q_W), BF16), jax.ShapeDtypeStruct((nseq, length, LANES), F32)],
        scratch_shapes=[pltpu.VMEM((KV_W // LANES, length, LANES), BF16), pltpu.VMEM((KV_W // LANES, length, LANES), BF16)],
        semantics=("parallel",), args=args)
    return o.reshape(shape[:-1] + (Q_W,)), lse.reshape(shape[:-1] + (LANES,)), exchanged


def _head_expand():
    return (jnp.arange(LANES)[:, None] == jnp.arange(Q_W)[None, :] // HEAD_DIM).astype(BF16)


def _mix_groups(os, lses, dils, seq, tag):
    bl = os[0].shape[0]
    ts = _tile_rows(seq)
    t = bl * seq
    ng = len(os)
    if ng == 1 and dils[0] == 1:
        return os[0].reshape(t, Q_W), [lses[0]]

    def body(*refs):
        e_ref = refs[0]
        o_refs, l_refs = refs[1:1 + ng], refs[1 + ng:1 + 2 * ng]
        om_ref = refs[1 + 2 * ng]
        lt_refs = refs[2 + 2 * ng:2 + 3 * ng]
        wide_ref, narrow_ref = refs[2 + 3 * ng:]
        ls = [_merge_rows([l_refs[g][r] for r in range(dils[g])], narrow_ref, dils[g]) for g in range(ng)]
        mx = functools.reduce(jnp.maximum, ls)
        tot = mx + jnp.log(functools.reduce(lambda a, b: a + b, [jnp.exp2(l_ - mx) for l_ in ls])) * LOG2E
        e = e_ref[...]
        o = None
        for g in range(ng):
            wt = jnp.exp2(ls[g] - tot)
            hi = wt.astype(BF16)
            lo = (wt - hi.astype(F32)).astype(BF16)
            wide = jnp.dot(hi, e, preferred_element_type=F32) + jnp.dot(lo, e, preferred_element_type=F32)
            term = wide * _merge_rows([o_refs[g][r].astype(F32) for r in range(dils[g])], wide_ref, dils[g])
            o = term if o is None else o + term
        om_ref[...] = o.astype(BF16)
        for g in range(ng):
            for r, part in enumerate(_split_rows(tot, narrow_ref, dils[g])):
                lt_refs[g][r] = part

    e = _head_expand()
    outs = pl.pallas_call(
        body, name=f"mix_groups_{tag}", grid=(t // ts,),
        in_specs=[_resident(e.shape)] + [_res_spec(seq, dl, Q_W) for dl in dils] + [_res_spec(seq, dl, LANES) for dl in dils],
        out_specs=[pl.BlockSpec((ts, Q_W), lambda i: (i, 0))] + [_res_spec(seq, dl, LANES) for dl in dils],
        out_shape=[jax.ShapeDtypeStruct((t, Q_W), BF16)]
                  + [jax.ShapeDtypeStruct(_res_shape(bl, seq, dl, LANES), F32) for dl in dils],
        scratch_shapes=[_stage(ts, Q_W), _stage(ts, LANES)],
        compiler_params=_params("parallel"),
    )(e, *os, *lses)
    return outs[0], list(outs[1:])


def _sigmoid(g):
    return 1.0 / (1.0 + jnp.exp(-g))


def _ffn_fwd(x0, o, w_out, gain, wg, wu, wd, tag, exchange=(), loss_head=None):
    t, d = x0.shape
    f = wd.shape[0]
    tm = min(256, t)
    has_loss = loss_head is not None

    def body(*refs):
        x0_ref, o_ref, wo_ref, gain_ref, wg_ref, wu_ref, wd_ref = refs[:7]
        x_ref, y_ref, g_ref, u_ref, a_ref, h_ref = refs[-8:-2] if has_loss else refs[-6:]
        xv = x0_ref[...] + jnp.dot(o_ref[...], wo_ref[...], preferred_element_type=F32)
        x_ref[...] = xv
        h = (xv * _rms(xv) * gain_ref[...]).astype(BF16)
        h_ref[...] = h
        g = lax.dot_general(h, wg_ref[...], NT, preferred_element_type=F32)
        u = lax.dot_general(h, wu_ref[...], NT, preferred_element_type=F32)
        g_ref[...] = g.astype(BF16)
        u_ref[...] = u.astype(BF16)
        a = (g * _sigmoid(g) * u).astype(BF16)
        a_ref[...] = a
        y = xv + jnp.dot(a, wd_ref[...], preferred_element_type=F32)
        if not has_loss:
            y_ref[...] = y
            return
        head_ref, target_ref, loss_ref, dhead_ref = refs[7], refs[8], refs[-2], refs[-1]
        head = head_ref[...]
        yhat = y * _rms(y)
        err = yhat * head - target_ref[...]
        dout = err * (1.0 / d)
        y_ref[...] = _rms_bwd(dout, y, head)[0]
        first = pl.program_id(0) == 0
        part = 0.5 * jnp.sum(jnp.mean(err * err, axis=-1, keepdims=True), axis=0, keepdims=True)
        _accumulate(loss_ref, jnp.broadcast_to(part, loss_ref.shape), first)
        _accumulate(dhead_ref, jnp.sum(dout * yhat, axis=0, keepdims=True), first)

    row = pl.BlockSpec((tm, d), lambda i: (i, 0))
    wide = pl.BlockSpec((tm, f), lambda i: (i, 0))
    in_specs = [row, pl.BlockSpec((tm, Q_W), lambda i: (i, 0)), _resident(w_out.shape), _resident((1, d)), _resident(wg.shape),
                _resident(wu.shape), _resident(wd.shape)]
    out_specs = [row, row, wide, wide, wide, row]
    out_shape = ([jax.ShapeDtypeStruct((t, d), F32)] * 2 + [jax.ShapeDtypeStruct((t, f), BF16)] * 3
                 + [jax.ShapeDtypeStruct((t, d), BF16)])
    if has_loss:
        in_specs += [_resident((1, d)), row]
        out_specs += [pl.BlockSpec((1, LANES), lambda i: (0, 0)), pl.BlockSpec((1, d), lambda i: (0, 0))]
        out_shape += [jax.ShapeDtypeStruct((1, LANES), F32), jax.ShapeDtypeStruct((1, d), F32)]
    outs, exchanged = _hosted_call(
        body, exchange, name=f"ffn_fwd_{tag}", grid=(t // tm,), in_specs=in_specs, out_specs=out_specs, out_shape=out_shape,
        scratch_shapes=[], semantics=("arbitrary" if has_loss else "parallel",),
        args=(x0, o, w_out, gain, wg, wu, wd) + (tuple(loss_head) if has_loss else ()))
    return (*outs, exchanged)


def _ffn_bwd(dy, x, gain, g, u, wg, wu, wd, tag, exchange=()):
    t, d = x.shape
    f = wd.shape[0]
    tm = min(256, t)

    def body(dy_ref, x_ref, gain_ref, g_ref, u_ref, wg_ref, wu_ref, wd_ref, dx_ref, dg_ref, du_ref, dgain_ref):
        dyv = dy_ref[...]
        da = lax.dot_general(dyv.astype(BF16), wd_ref[...], NT, preferred_element_type=F32)
        gv, uv = g_ref[...].astype(F32), u_ref[...].astype(F32)
        sg = _sigmoid(gv)
        act = gv * sg
        du = (da * act).astype(BF16)
        dg = (da * uv * (sg * (1.0 + gv * (1.0 - sg)))).astype(BF16)
        du_ref[...] = du
        dg_ref[...] = dg
        dh = jnp.dot(dg, wg_ref[...], preferred_element_type=F32) + jnp.dot(du, wu_ref[...], preferred_element_type=F32)
        xv, gain_v = x_ref[...], gain_ref[...]
        dx, xhat = _rms_bwd(dh, xv, gain_v)
        dx_ref[...] = dyv + dx
        _accumulate(dgain_ref, jnp.sum(dh * xhat, axis=0, keepdims=True), pl.program_id(0) == 0)

    row = pl.BlockSpec((tm, d), lambda i: (i, 0))
    wide = pl.BlockSpec((tm, f), lambda i: (i, 0))
    outs, exchanged = _hosted_call(
        body, exchange, name=f"ffn_bwd_{tag}", grid=(t // tm,),
        in_specs=[row, row, _resident((1, d)), wide, wide, _resident(wg.shape), _resident(wu.shape), _resident(wd.shape)],
        out_specs=[row, wide, wide, pl.BlockSpec((1, d), lambda i: (0, 0))],
        out_shape=[jax.ShapeDtypeStruct((t, d), F32), jax.ShapeDtypeStruct((t, f), BF16), jax.ShapeDtypeStruct((t, f), BF16),
                   jax.ShapeDtypeStruct((1, d), F32)],
        scratch_shapes=[], semantics=("arbitrary",), args=(dy, x, gain, g, u, wg, wu, wd))
    return (*outs, exchanged)


def _tn_matmul(a, b, name, into=None, row_block=0, row_blocks=1):
    t, k = a.shape
    n = b.shape[1]
    tk = k // 2 if (k // 2) % LANES == 0 else k
    tt = min(2048, t)
    first = row_block * (k // tk)

    def body(a_ref, b_ref, *rest):
        o_ref, acc_ref = rest[-2:]
        prod = lax.dot_general(a_ref[...].astype(BF16), b_ref[...].astype(BF16), TN, preferred_element_type=F32)
        j = pl.program_id(1)

        @pl.when(j == 0)
        def _():
            acc_ref[...] = prod

        @pl.when(j > 0)
        def _():
            acc_ref[...] += prod

        @pl.when(j == pl.num_programs(1) - 1)
        def _():
            o_ref[...] = acc_ref[...].astype(BF16)

    return pl.pallas_call(
        body, name=name, grid=(k // tk, t // tt),
        in_specs=[pl.BlockSpec((tt, tk), lambda i, j: (j, i)), pl.BlockSpec((tt, n), lambda i, j: (j, 0))]
                 + ([ANY] if into is not None else []),
        out_specs=pl.BlockSpec((tk, n), lambda i, j: (first + i, 0)),
        out_shape=jax.ShapeDtypeStruct((row_blocks * k, n), BF16),
        scratch_shapes=[pltpu.VMEM((tk, n), F32)],
        input_output_aliases={2: 0} if into is not None else {},
        compiler_params=_params("parallel", "arbitrary"),
    )(a, b, *([into] if into is not None else []))


def _attn_out_bwd(dx, w, o, dils, seq, tag, lse=None, sink=None, exchange=()):
    t, d = dx.shape
    ts = _tile_rows(seq)
    bl = t // seq
    ng = len(dils)
    has_sink = sink is not None
    expand = _head_expand().T

    def body(*refs):
        refs = list(refs)
        dx_ref, w_ref, o_ref, e_ref = refs[:4]
        refs = refs[4:]
        lse_ref, sink_ref = (refs.pop(0), refs.pop(0)) if has_sink else (None, None)
        do_refs, dl_refs = refs[:ng], refs[ng:2 * ng]
        refs = refs[2 * ng:]
        dsink_ref = refs.pop(0) if has_sink else None
        dof_ref, dlf_ref = refs
        do = lax.dot_general(dx_ref[...].astype(BF16), w_ref[...], NT, preferred_element_type=F32)
        prod = do * o_ref[...].astype(F32)
        hi = prod.astype(BF16)
        lo = (prod - hi.astype(F32)).astype(BF16)
        e = e_ref[...]
        dl = jnp.dot(hi, e, preferred_element_type=F32) + jnp.dot(lo, e, preferred_element_type=F32)
        for g in range(ng):
            for r, part in enumerate(_split_rows(do, dof_ref, dils[g])):
                do_refs[g][r] = part.astype(BF16)
            for r, part in enumerate(_split_rows(dl, dlf_ref, dils[g])):
                dl_refs[g][r] = part
        if has_sink:
            part = -jnp.exp2(sink_ref[...] - lse_ref[...]) * dl
            _accumulate(dsink_ref, jnp.sum(part, axis=0, keepdims=True), pl.program_id(0) == 0)

    row = pl.BlockSpec((ts, d), lambda i: (i, 0))
    narrow = pl.BlockSpec((ts, LANES), lambda i: (i, 0))
    args = [dx, w, o, expand]
    in_specs = [row, _resident(w.shape), pl.BlockSpec((ts, Q_W), lambda i: (i, 0)), _resident(expand.shape)]
    if has_sink:
        args += [lse, jnp.pad(sink.reshape(1, N_HEADS) * LOG2E, ((0, 0), (0, LANES - N_HEADS)))]
        in_specs += [narrow, _resident((1, LANES))]
    out_specs = [_res_spec(seq, dl, Q_W) for dl in dils] + [_res_spec(seq, dl, LANES) for dl in dils]
    out_shape = ([jax.ShapeDtypeStruct(_res_shape(bl, seq, dl, Q_W), BF16) for dl in dils]
                 + [jax.ShapeDtypeStruct(_res_shape(bl, seq, dl, LANES), F32) for dl in dils])
    if has_sink:
        out_specs.append(pl.BlockSpec((1, LANES), lambda i: (0, 0)))
        out_shape.append(jax.ShapeDtypeStruct((1, LANES), F32))
    outs, exchanged = _hosted_call(
        body, exchange, name=f"attn_out_bwd_{tag}", grid=(t // ts,), in_specs=in_specs, out_specs=out_specs, out_shape=out_shape,
        scratch_shapes=[_stage(ts, Q_W), _stage(ts, LANES)], semantics=("arbitrary" if has_sink else "parallel",), args=args)
    return list(outs[:ng]), list(outs[ng:2 * ng]), (outs[2 * ng] if has_sink else None), exchanged


def _attn_bwd(qkv, do, lse, delta, cos, sin, w, tag, exchange=()):
    shape = qkv.shape
    dil = shape[1]
    rows_all = _seq_view(qkv)
    nseq, length, _ = rows_all.shape
    bq = min(QUERY_BLOCK, length)
    wk = _key_rows(bq, w, length)
    nb = length // bq
    per_step = _per_step(dil, length)

    def body(*refs):
        def sub(i, carry):
            one(*[ref.at[i] for ref in refs[:7]], *refs[7:])
            return carry

        if per_step == 1:
            sub(0, 0)
        else:
            lax.fori_loop(0, per_step, sub, 0)

    def one(qkv_ref, do_ref, lse_ref, dl_ref, cos_ref, sin_ref, dp_ref, kk_ref, vv_ref, dk_ref, dv_ref):
        _swap_halves(qkv_ref, Q_W, kk_ref)
        _swap_halves(qkv_ref, Q_W + KV_W, vv_ref)
        dk_ref[...] = jnp.zeros_like(dk_ref)
        dv_ref[...] = jnp.zeros_like(dv_ref)
        band = _band(bq, wk)
        lo_q = lax.broadcasted_iota(jnp.int32, (bq, LANES), 1) < HEAD_DIM
        hi_q = jnp.logical_not(lo_q)

        def block(i, carry):
            q0, k0 = _window(i, bq, w, wk, length)
            valid = jnp.abs(band + (q0 - k0)) <= w
            rows, krows = pl.ds(q0, bq), pl.ds(k0, wk)
            c, sn = cos_ref[rows, :], -sin_ref[rows, :]
            lse_t, dl_t = lse_ref[rows, :], dl_ref[rows, :]
            zero = jnp.zeros((bq, LANES), BF16)
            for first in range(0, N_KV, KV_PER_PHASE):
                kvs = range(first, first + KV_PER_PHASE)
                pairs = [kv * 2 + j for kv in kvs for j in range(GRP // 2)]
                heads = [2 * j + half for j in pairs for half in range(2)]
                qp = {j: qkv_ref[rows, j * LANES:(j + 1) * LANES] for j in pairs}
                dop = {j: do_ref[rows, j * LANES:(j + 1) * LANES] for j in pairs}
                k2 = {kv: _pair_operand(qkv_ref, kk_ref, Q_W, kv, krows) for kv in kvs}
                v2 = {kv: _pair_operand(qkv_ref, vv_ref, Q_W + KV_W, kv, krows) for kv in kvs}
                sc2 = {j: lax.dot_general(qp[j], k2[j // 2], NT, preferred_element_type=F32) for j in pairs}
                dp2 = {j: lax.dot_general(dop[j], v2[j // 2], NT, preferred_element_type=F32) for j in pairs}
                sc = {hd: sc2[hd // 2][:, (hd % 2) * wk:(hd % 2 + 1) * wk] for hd in heads}
                dp = {hd: dp2[hd // 2][:, (hd % 2) * wk:(hd % 2 + 1) * wk] for hd in heads}
                p = {hd: jnp.exp2(jnp.where(valid, sc[hd], NEG_INF) - _over_keys(lse_t[:, hd:hd + 1], wk)) for hd in heads}
                ds = {hd: (p[hd] * (dp[hd] - _over_keys(dl_t[:, hd:hd + 1], wk))).astype(BF16) for hd in heads}
                pb = {hd: p[hd].astype(BF16) for hd in heads}
                for j in pairs:
                    dq = jnp.dot(jnp.concatenate([ds[2 * j], ds[2 * j + 1]], axis=1), k2[j // 2], preferred_element_type=F32) * SCALE
                    dp_ref[rows, j * LANES:(j + 1) * LANES] = _rope(dq, c, sn).astype(BF16)
                for kv in kvs:
                    own = range(kv * GRP, (kv + 1) * GRP)
                    q4 = jnp.concatenate([jnp.where(lo_q if hd % 2 == 0 else hi_q, qp[hd // 2], zero) for hd in own], axis=0)
                    do4 = jnp.concatenate([jnp.where(lo_q if hd % 2 == 0 else hi_q, dop[hd // 2], zero) for hd in own], axis=0)
                    dk_ref[kv, krows, :] += lax.dot_general(jnp.concatenate([ds[hd] for hd in own], axis=0), q4, TN,
                                                            preferred_element_type=F32)
                    dv_ref[kv, krows, :] += lax.dot_general(jnp.concatenate([pb[hd] for hd in own], axis=0), do4, TN,
                                                            preferred_element_type=F32)
            return carry

        lax.fori_loop(0, nb, block, 0)
        lo = lax.broadcasted_iota(jnp.int32, (length, LANES), 1) < HEAD_DIM
        c, sn = cos_ref[...], -sin_ref[...]
        for ch in range(KV_W // LANES):
            halves = []
            for acc_ref in (dk_ref, dv_ref):
                even, odd = acc_ref[2 * ch], acc_ref[2 * ch + 1]
                even = even + pltpu.roll(even, HEAD_DIM, 1)
                odd = odd + pltpu.roll(odd, HEAD_DIM, 1)
                halves.append(jnp.where(lo, even, odd))
            dp_ref[:, Q_W + ch * LANES:Q_W + (ch + 1) * LANES] = _rope(halves[0] * LN2, c, sn).astype(BF16)
            dp_ref[:, Q_W + KV_W + ch * LANES:Q_W + KV_W + (ch + 1) * LANES] = halves[1].astype(BF16)

    def seq_block(c):
        return pl.BlockSpec((per_step, length, c), lambda i: (i, 0, 0))

    table = pl.BlockSpec((per_step, length, LANES), lambda i: (i % (dil // per_step), 0, 0))
    (out,), exchanged = _hosted_call(
        body, exchange, name=f"attn_bwd_{tag}", grid=(nseq // per_step,),
        in_specs=[seq_block(QKV_W), seq_block(Q_W), seq_block(LANES), seq_block(LANES), table, table],
        out_specs=[seq_block(QKV_W)],
        out_shape=[jax.ShapeDtypeStruct((nseq, length, QKV_W), BF16)],
        scratch_shapes=[pltpu.VMEM((KV_W // LANES, length, LANES), BF16), pltpu.VMEM((KV_W // LANES, length, LANES), BF16),
                        pltpu.VMEM((N_KV, length, LANES), F32), pltpu.VMEM((N_KV, length, LANES), F32)],
        semantics=("parallel",), args=(rows_all, _seq_view(do), _seq_view(lse), _seq_view(delta), cos, sin))
    return out.reshape(shape), exchanged


def _qkv_bwd(dy, x, gain, w, dps, dils, seq, tag, exchange=()):
    t, d = x.shape
    ts = _tile_rows(seq)
    ng = len(dps)

    def body(dy_ref, x_ref, gain_ref, w_ref, *refs):
        dp_refs, (dx_ref, dgain_ref, stage_ref) = refs[:ng], refs[ng:]
        dps_rows = [dp_refs[gi][0] if dil == 1 else jnp.concatenate([dp_refs[gi][r] for r in range(dil)], axis=0)
                    for gi, dil in enumerate(dils)]
        parts = [jnp.dot(dp, w_ref[gi * QKV_W:(gi + 1) * QKV_W, :], preferred_element_type=F32) for gi, dp in enumerate(dps_rows)]
        dh = None
        for part, dil in zip(parts, dils):
            n = ts // dil
            part = _merge_rows([part[r * n:(r + 1) * n] for r in range(dil)], stage_ref, dil)
            dh = part if dh is None else dh + part
        xv, gain_v = x_ref[...], gain_ref[...]
        dx, xhat = _rms_bwd(dh, xv, gain_v)
        dx_ref[...] = dy_ref[...] + dx
        _accumulate(dgain_ref, jnp.sum(dh * xhat, axis=0, keepdims=True), pl.program_id(0) == 0)

    row = pl.BlockSpec((ts, d), lambda i: (i, 0))
    (dx, dgain), exchanged = _hosted_call(
        body, exchange, name=f"qkv_bwd_{tag}", grid=(t // ts,),
        in_specs=[row, row, _resident((1, d)), _resident(w.shape)] + [_res_spec(seq, dl, QKV_W) for dl in dils],
        out_specs=[row, pl.BlockSpec((1, d), lambda i: (0, 0))],
        out_shape=[jax.ShapeDtypeStruct((t, d), F32), jax.ShapeDtypeStruct((1, d), F32)],
        scratch_shapes=[_stage(ts, d)], semantics=("arbitrary",), args=(dy, x, gain, w, *dps))
    return dx, dgain, exchanged


ANY = pl.BlockSpec(memory_space=pl.ANY)


def _place():
    x, y, c = lax.axis_index("x"), lax.axis_index("y"), lax.axis_index("c")
    return x, y, c


def _exchange_steps(srcs, dsts, gather, send_sems, recv_sems, local_sems):
    x, y, c = _place()
    me, sibling = (x, y, c), (x, y, 1 - c)
    chips = [(1 - x, y), (x, 1 - y), (1 - x, 1 - y)]
    mine = 4 * x + 2 * y + c

    def slot(a, device):
        px, py, pc = device
        return dsts[a].at[4 * px + 2 * py + pc]

    def passes(a, k, block, to, src=None):
        rows = slot(a, block)
        return pltpu.make_async_remote_copy(src_ref=rows if src is None else src, dst_ref=rows, send_sem=send_sems.at[a, k],
                                            recv_sem=recv_sems.at[a, k], device_id=to, device_id_type=MESH)

    def scatters(a, k):
        peer = mine ^ k
        return pltpu.make_async_remote_copy(
            src_ref=srcs[a].at[peer], dst_ref=dsts[a].at[mine], send_sem=send_sems.at[a, k - 1], recv_sem=recv_sems.at[a, k - 1],
            device_id=(peer // 4, (peer // 2) % 2, peer % 2), device_id_type=MESH)

    def local(a):
        return pltpu.make_async_copy(srcs[a] if gather[a] else srcs[a].at[mine], dsts[a].at[mine], local_sems.at[a])

    def first_copies(a):
        if not gather[a]:
            return [scatters(a, k) for k in range(1, N_DEV)]
        return [passes(a, 0, me, sibling, src=srcs[a])] + [passes(a, 1 + j, me, (*chip, c), src=srcs[a]) for j, chip in enumerate(chips)]

    def start():
        for a in range(len(srcs)):
            local(a).start()
            for cp in first_copies(a):
                cp.start()

    def forward():
        for a in range(len(srcs)):
            if gather[a]:
                for j, chip in enumerate(chips):
                    passes(a, 1 + j, (*chip, c), me).wait_recv()
                    passes(a, 4 + j, (*chip, c), sibling).start()

    def finish():
        for a in range(len(srcs)):
            if gather[a]:
                passes(a, 0, sibling, me).wait_recv()
                for j, chip in enumerate(chips):
                    passes(a, 4 + j, (*chip, 1 - c), me).wait_recv()
                    passes(a, 4 + j, (*chip, c), sibling).wait_send()
                for cp in first_copies(a):
                    cp.wait_send()
            else:
                for cp in first_copies(a):
                    cp.wait()
            local(a).wait()

    return start, forward, finish


def _exchange_scratch(n):
    return [pltpu.SemaphoreType.DMA((n, N_DEV - 1)), pltpu.SemaphoreType.DMA((n, N_DEV - 1)), pltpu.SemaphoreType.DMA((n,))]


def _exchanged_shapes(exchange):
    return [jax.ShapeDtypeStruct(((N_DEV,) + a.shape) if g else a.shape, a.dtype) for a, g in exchange]


def _hosted_call(body, exchange, *, name, grid, in_specs, out_specs, out_shape, scratch_shapes, semantics, args):
    out_specs, out_shape, scratch = list(out_specs), list(out_shape), list(scratch_shapes)
    if not exchange:
        outs = pl.pallas_call(body, name=name, grid=grid, in_specs=in_specs, out_specs=out_specs, out_shape=out_shape,
                              scratch_shapes=scratch, compiler_params=_params(*semantics))(*args)
        return list(outs), []
    n, n_in, n_out, n_scr = len(exchange), len(in_specs), len(out_specs), len(scratch)
    gather = [g for _, g in exchange]
    steps = math.prod(grid)

    def hosted(*refs):
        own_in, x_in = refs[:n_in], refs[n_in:n_in + n]
        own_out, x_out = refs[n_in + n:n_in + n + n_out], refs[n_in + n + n_out:n_in + 2 * n + n_out]
        own_scr, sems = refs[n_in + 2 * n + n_out:n_in + 2 * n + n_out + n_scr], refs[-3:]
        step = pl.program_id(0)
        for axis in range(1, len(grid)):
            step = step * grid[axis] + pl.program_id(axis)
        start, forward, finish = _exchange_steps(x_in, x_out, gather, *sems)
        pl.when(step == 0)(start)
        body(*own_in, *own_out, *own_scr)
        pl.when(step == steps // 2)(forward)
        pl.when(step == steps - 1)(finish)

    outs = pl.pallas_call(
        hosted, name=name, grid=grid, in_specs=list(in_specs) + [ANY] * n, out_specs=out_specs + [ANY] * n,
        out_shape=out_shape + _exchanged_shapes(exchange), scratch_shapes=scratch + _exchange_scratch(n),
        compiler_params=_params(*["arbitrary"] * len(grid)),
    )(*args, *[a for a, _ in exchange])
    return list(outs[:n_out]), list(outs[n_out:])


def _exchange_now(exchange, name):
    n = len(exchange)
    gather = [g for _, g in exchange]

    def body(*refs):
        for step in _exchange_steps(refs[:n], refs[n:2 * n], gather, *refs[2 * n:]):
            step()

    return pl.pallas_call(
        body, name=name, in_specs=[ANY] * n, out_specs=[ANY] * n, out_shape=_exchanged_shapes(exchange),
        scratch_shapes=_exchange_scratch(n),
    )(*[a for a, _ in exchange])


def _all_reduce_small(v):
    def body(v_ref, o_ref, recv_ref, send_sems, recv_sems):
        x, y, c = _place()
        me = 4 * x + 2 * y + c
        copies = []
        for k in range(1, N_DEV):
            peer = me ^ k
            copies.append(pltpu.make_async_remote_copy(
                src_ref=v_ref, dst_ref=recv_ref.at[k], send_sem=send_sems.at[k - 1], recv_sem=recv_sems.at[k - 1],
                device_id=(peer // 4, (peer // 2) % 2, peer % 2), device_id_type=MESH))
        for cp in copies:
            cp.start()
        recv_ref[0] = v_ref[...]
        for cp in copies:
            cp.wait()
        acc = recv_ref[me]
        for src in range(1, N_DEV):
            acc = acc + recv_ref[me ^ src]
        o_ref[...] = acc

    vm = pl.BlockSpec(memory_space=pltpu.VMEM)
    return pl.pallas_call(
        body, name="all_reduce_small", in_specs=[vm], out_specs=vm, out_shape=jax.ShapeDtypeStruct(v.shape, F32),
        scratch_shapes=[pltpu.VMEM((N_DEV,) + v.shape, F32), pltpu.SemaphoreType.DMA((N_DEV - 1,)),
                        pltpu.SemaphoreType.DMA((N_DEV - 1,))],
    )(v)


def _adamw_math(w, g, m, v):
    m = ADAM_B1 * m + (1.0 - ADAM_B1) * g
    v = ADAM_B2 * v + (1.0 - ADAM_B2) * (g * g)
    m_hat = m / (1.0 - ADAM_B1 ** ADAM_STEP)
    v_hat = v / (1.0 - ADAM_B2 ** ADAM_STEP)
    delta = -ADAM_LR * (m_hat / (jnp.sqrt(v_hat) + ADAM_EPS) + ADAM_WD * w)
    return delta, m, v


def _adamw(parts, w, m, v, name, layer=None, into=None):
    r, c = w.shape[-2:]
    tr = r // 2 if r % 16 == 0 and r >= 256 else r
    n = len(parts)

    def body(*refs):
        w_ref, m_ref, v_ref = refs[n:n + 3]
        g_ref, d_ref, nm_ref, nv_ref = refs[-4:]
        g = refs[0][...].astype(F32)
        for p_ref in refs[1:n]:
            g = g + p_ref[...].astype(F32)
        g_ref[...] = g
        d_ref[...], nm_ref[...], nv_ref[...] = _adamw_math(w_ref[...], g, m_ref[...], v_ref[...])

    def slab(slot):
        return pl.BlockSpec((None, tr, c), lambda i: (slot, i, 0))

    tile = pl.BlockSpec((tr, c), lambda i: (i, 0)) if layer is None else slab(layer)
    arrays, in_specs = [], []
    for p in parts:
        if isinstance(p, tuple):
            arrays.append(p[0])
            in_specs.append(slab(p[1]))
        else:
            arrays.append(p)
            in_specs.append(tile)
    kept = list(into) if into is not None else []
    return pl.pallas_call(
        body, name=name, grid=(r // tr,), in_specs=in_specs + [tile] * 3 + [ANY] * len(kept), out_specs=[tile] * 4,
        out_shape=[jax.ShapeDtypeStruct(w.shape, F32)] * 4,
        input_output_aliases={n + 3 + k: k for k in range(len(kept))}, compiler_params=_params("parallel"),
    )(*arrays, w, m, v, *kept)


def _rows(g):
    return g.reshape(-1, g.shape[-1])


def _row_blocks(dw):
    k, n = dw.shape
    return dw.reshape(N_DEV, k // N_DEV, n)


def _pack_rows(rows, width):
    out = None
    for i, r in enumerate(rows):
        r = r.reshape(1, -1).astype(F32)
        r = jnp.pad(r, ((i, 8 - 1 - i), (0, width - r.shape[1])))
        out = r if out is None else out + r
    return out


def _mixer_fwd(x, gain, w_in, cos, sin, seq, groups, tag, sink=None, exchanges=None):
    exchanges = exchanges or {}
    os, lses, got = [], [], {}
    qkvs, hs, got["proj"] = _qkv_proj(x, gain, w_in, cos, sin, seq, [dil for dil, _ in groups], tag,
                                      exchange=exchanges.get("proj", ()))
    for gi, (dil, w) in enumerate(groups):
        o, lse, got[gi] = _attn_fwd(qkvs[gi], w, f"{tag}{gi}", sink=sink, exchange=exchanges.get(gi, ()))
        os.append(o)
        lses.append(lse)
    o, lses = _mix_groups(os, lses, [dl for dl, _ in groups], seq, tag)
    return (qkvs, hs, o, lses), got


def _mixer_bwd(dy, x_in, gain, w_in, w_out, saved, cos, sin, seq, groups, tag, sink=None, exchanges=None, scatter_own=False):
    qkvs, hs, o, lses = saved
    t, d = x_in.shape
    dils = [dl for dl, _ in groups]
    lse_tokens = lses[0].reshape(t, LANES) if sink is not None else None
    dw_out = _tn_matmul(o, dy, f"dw_out_{tag}")
    dos, dls, dsink, early = _attn_out_bwd(dy, w_out, o, dils, seq, tag, lse=lse_tokens, sink=sink,
                                           exchange=_to_send([dw_out]) if scatter_own else ())
    if scatter_own:
        (dw_out,) = early
    exchanges = exchanges or {}
    dps, got = [], {}
    for gi, (dil, w) in enumerate(groups):
        dp, got[gi] = _attn_bwd(qkvs[gi], dos[gi], lses[gi], dls[gi], _tables_by_residue(cos, seq, dil),
                                _tables_by_residue(sin, seq, dil), w, f"{tag}{gi}", exchange=exchanges.get(gi, ()))
        dps.append(dp)
    dw_in = None
    for gi in range(len(groups)):
        dw_in = _tn_matmul(dps[gi].reshape(t, QKV_W), hs[gi].reshape(t, d), f"dw_in_{tag}{gi}", into=dw_in, row_block=gi,
                           row_blocks=len(groups))
    dx, dgain, late = _qkv_bwd(dy, x_in, gain, w_in, dps, dils, seq, tag, exchange=_to_send([dw_in]) if scatter_own else ())
    if scatter_own:
        (dw_in,) = late
    return dx, dw_in, dw_out, dgain, dsink, got


def _ffn_layer_bwd(dy, x_in, gain, saved, wg, wu, wd, tag, exchange=()):
    g, u, act, h = saved
    dx, dg, du, dgain, got = _ffn_bwd(dy, x_in, gain, g, u, wg, wu, wd, tag, exchange=exchange)
    dwd = _tn_matmul(act, dy, f"dw_down_{tag}")
    dwg = _tn_matmul(dg, h, f"dw_gate_{tag}")
    dwu = _tn_matmul(du, h, f"dw_up_{tag}")
    return dx, dwg, dwu, dwd, dgain, got


def _to_send(dws):
    return [(_row_blocks(g), False) for g in dws]


def kernel(x, a_w_in, a_sink, a_w_out, b_w_in, b_w_out, norm_mix, norm_ffn, w_gate, w_up, w_down, final_norm, loss_target, m_a_w_in, m_a_sink, m_a_w_out, m_b_w_in, m_b_w_out, m_norm_mix, m_norm_ffn, m_w_gate, m_w_up, m_w_down, m_final_norm, v_a_w_in, v_a_sink, v_a_w_out, v_b_w_in, v_b_w_out, v_norm_mix, v_norm_ffn, v_w_gate, v_w_up, v_w_down, v_final_norm):
    bl, seq, d = x.shape
    t = bl * seq
    xf = x.reshape(t, d)
    target = loss_target.reshape(t, d)
    cos, sin = _rope_tables(seq)
    groups_a = [(1, ATTN_HALF_WINDOW)]
    groups_b = [(dil, window // 2 // dil) for window, dil in DILATED_GROUPS]

    def flip(w_):
        return jnp.swapaxes(w_, -1, -2)

    a_w_in, m_a_w_in, v_a_w_in, b_w_in, m_b_w_in, v_b_w_in = map(flip, (a_w_in, m_a_w_in, v_a_w_in, b_w_in, m_b_w_in, v_b_w_in))
    w_gate, m_w_gate, v_w_gate, w_up, m_w_up, v_w_up = map(flip, (w_gate, m_w_gate, v_w_gate, w_up, m_w_up, v_w_up))

    def shard(w_, layer):
        return (w_[layer].astype(BF16), True)

    (wa_in,) = map(_rows, _exchange_now([shard(a_w_in, 0)], "gather_first"))

    saved_a, got = _mixer_fwd(xf, norm_mix[0:1], wa_in, cos, sin, seq, groups_a, "a", sink=a_sink[0],
                              exchanges={"proj": [shard(w_down, 0), shard(a_w_out, 0)], 0: [shard(w_gate, 0), shard(w_up, 0)]})
    wg0, wu0, wd0, wa_out = map(_rows, got[0] + got["proj"])
    x1_0, x2_0, *saved_0, got = _ffn_fwd(xf, saved_a[2], wa_out, norm_ffn[0:1], wg0, wu0, wd0, "0",
                                         exchange=[shard(b_w_in, 0), shard(b_w_out, 0)])
    wb_in, wb_out = map(_rows, got)
    saved_b, got = _mixer_fwd(x2_0, norm_mix[1:2], wb_in, cos, sin, seq, groups_b, "b",
                              exchanges={0: [shard(w_gate, 1)], 1: [shard(w_up, 1)], 2: [shard(w_down, 1)]})
    wg1, wu1, wd1 = map(_rows, got[0] + got[1] + got[2])
    x1_1, dy, *saved_1, loss_part, d_final, _ = _ffn_fwd(x2_0, saved_b[2], wb_out, norm_ffn[1:2], wg1, wu1, wd1, "1",
                                                         loss_head=(final_norm.reshape(1, d), target))

    dy, dwg1, dwu1, dwd1, d_nf1, _ = _ffn_layer_bwd(dy, x1_1, norm_ffn[1:2], saved_1, wg1, wu1, wd1, "1")
    dy, dwb_in, dwb_out, d_nm1, _, got = _mixer_bwd(
        dy, x2_0, norm_mix[1:2], wb_in, wb_out, saved_b, cos, sin, seq, groups_b, "b",
        exchanges={0: _to_send([dwg1, dwd1]), 1: _to_send([dwu1])})
    (r_g1, r_d1), (r_u1,) = got[0], got[1]
    dy, dwg0, dwu0, dwd0, d_nf0, (r_b_in, r_b_out) = _ffn_layer_bwd(
        dy, x1_0, norm_ffn[0:1], saved_0, wg0, wu0, wd0, "0", exchange=_to_send([dwb_in, dwb_out]))
    dy, r_a_in, r_a_out, d_nm0, d_sink, got = _mixer_bwd(
        dy, xf, norm_mix[0:1], wa_in, wa_out, saved_a, cos, sin, seq, groups_a, "a", sink=a_sink[0],
        exchanges={0: _to_send([dwg0, dwu0, dwd0])}, scatter_own=True)
    r_g0, r_u0, r_d0 = got[0]
    grad_x = dy.reshape(bl, seq, d)

    def update(received, w_, m_, v_, name):
        out = None
        for layer in reversed(range(len(received))):
            out = _adamw([(received[layer], src) for src in range(N_DEV)], w_, m_, v_, f"adamw_{name}{layer}", layer=layer, into=out)
        return out

    u_a_in = update([r_a_in], a_w_in, m_a_w_in, v_a_w_in, "a_in")
    u_a_out = update([r_a_out], a_w_out, m_a_w_out, v_a_w_out, "a_out")
    u_b_in = update([r_b_in], b_w_in, m_b_w_in, v_b_w_in, "b_in")
    u_b_out = update([r_b_out], b_w_out, m_b_w_out, v_b_w_out, "b_out")
    u_gate = update([r_g0, r_g1], w_gate, m_w_gate, v_w_gate, "gate")
    u_up = update([r_u0, r_u1], w_up, m_w_up, v_w_up, "up")
    u_down = update([r_d0, r_d1], w_down, m_w_down, v_w_down, "down")

    small = _pack_rows([d_nm0, d_nm1, d_nf0, d_nf1, d_final, d_sink, loss_part], d)
    total = _all_reduce_small(small)
    small_w = _pack_rows([norm_mix[0], norm_mix[1], norm_ffn[0], norm_ffn[1], final_norm, a_sink], d)
    small_m = _pack_rows([m_norm_mix[0], m_norm_mix[1], m_norm_ffn[0], m_norm_ffn[1], m_final_norm, m_a_sink], d)
    small_v = _pack_rows([v_norm_mix[0], v_norm_mix[1], v_norm_ffn[0], v_norm_ffn[1], v_final_norm, v_a_sink], d)
    u_small = _adamw([total], small_w, small_m, small_v, "adamw_small")
    loss = total[6, 0]

    outs = []
    for k in range(4):
        sm = u_small[k]
        outs += [flip(u_a_in[k]), sm[5:6, :N_HEADS], u_a_out[k], flip(u_b_in[k]), u_b_out[k], sm[0:2], sm[2:4],
                 flip(u_gate[k]), flip(u_up[k]), u_down[k], sm[4]]
    return (loss, grad_x, *outs)
```

```python
import functools
import math

import jax
import jax.numpy as jnp
from jax import lax
from jax.experimental import pallas as pl
from jax.experimental.pallas import tpu as pltpu

F32 = jnp.float32
BF16 = jnp.bfloat16

HEAD_DIM = 64
N_HEADS = 16
N_KV = 4
GRP = N_HEADS // N_KV
Q_W = N_HEADS * HEAD_DIM
KV_W = N_KV * HEAD_DIM
QKV_W = Q_W + 2 * KV_W
ATTN_HALF_WINDOW = 128
DILATED_GROUPS = ((128, 1), (512, 4), (2048, 16))
ROPE_THETA = 10000.0
RMS_EPS = 1e-6
NEG_INF = -1e30
SCALE = 1.0 / math.sqrt(HEAD_DIM)
LOG2E = 1.0 / math.log(2.0)
LN2 = math.log(2.0)

ADAM_LR = 0.001
ADAM_B1 = 0.9
ADAM_B2 = 0.999
ADAM_EPS = 1e-08
ADAM_WD = 0.01
ADAM_STEP = 10

LANES = 128
VMEM_LIMIT = 56 * 1024 * 1024
QUERY_BLOCK = 128
PAIRS_PER_PHASE = 4
KV_PER_PHASE = 2
N_DEV = 8
MESH = pl.DeviceIdType.MESH

NT = (((1,), (1,)), ((), ()))
TN = (((0,), (0,)), ((), ()))


def _params(*sem):
    return pltpu.CompilerParams(dimension_semantics=tuple(sem) if sem else None, vmem_limit_bytes=VMEM_LIMIT)


def _resident(shape):
    return pl.BlockSpec(shape, lambda *_: (0,) * len(shape), pipeline_mode=pl.Buffered(1))


def _rope_tables(seq):
    inv_freq = 1.0 / (ROPE_THETA ** (jnp.arange(0, HEAD_DIM, 2, dtype=F32) / HEAD_DIM))
    ang = jnp.arange(seq, dtype=F32)[:, None] * inv_freq[None, :]
    cos, sin = jnp.cos(ang), jnp.sin(ang)
    return jnp.tile(cos, (1, 4)), jnp.concatenate([-sin, sin, -sin, sin], axis=1)


def _rope(t, cos, sin_signed):
    lane = lax.broadcasted_iota(jnp.int32, t.shape, 1)
    first = (lane & (HEAD_DIM // 2)) == 0
    swapped = jnp.where(first, pltpu.roll(t, LANES - HEAD_DIM // 2, 1), pltpu.roll(t, HEAD_DIM // 2, 1))
    return t * cos + swapped * sin_signed


def _rms(x):
    return lax.rsqrt(jnp.mean(x * x, axis=-1, keepdims=True) + RMS_EPS)


def _rms_bwd(dh, x, gain):
    r = _rms(x)
    xhat = x * r
    dxh = dh * gain
    dx = r * (dxh - xhat * jnp.mean(dxh * xhat, axis=-1, keepdims=True))
    return dx, xhat


def _accumulate(ref, value, first):
    @pl.when(first)
    def _():
        ref[...] = jnp.zeros_like(ref)

    ref[...] += value


def _tile_rows(seq):
    return min(512, seq)


def _res_shape(bl, seq, dil, c):
    ts = _tile_rows(seq)
    return (bl, dil, seq // ts, ts // dil, c)


def _res_spec(seq, dil, c):
    ts = _tile_rows(seq)
    per_seq = seq // ts
    return pl.BlockSpec((None, dil, None, ts // dil, c), lambda i: (i // per_seq, 0, i % per_seq, 0, 0))


def _seq_view(a):
    bl, dil, tiles, n, c = a.shape
    return a.reshape(bl * dil, tiles * n, c)


def _stage(ts, c):
    return pltpu.VMEM((c // LANES, ts, LANES), F32)


def _split_rows(val, stage_ref, dil):
    if dil == 1:
        return [val]
    ts, c = val.shape
    n, nc = ts // dil, c // LANES
    for k in range(nc):
        stage_ref[k] = val[:, k * LANES:(k + 1) * LANES]
    return [jnp.concatenate([stage_ref[k, pl.ds(r, n, stride=dil), :] for k in range(nc)], axis=1) for r in range(dil)]


def _merge_rows(parts, stage_ref, dil):
    if dil == 1:
        return parts[0]
    n, c = parts[0].shape
    nc = c // LANES
    for r, part in enumerate(parts):
        for k in range(nc):
            stage_ref[k, pl.ds(r, n, stride=dil), :] = part[:, k * LANES:(k + 1) * LANES]
    return jnp.concatenate([stage_ref[k] for k in range(nc)], axis=1)


def _tables_tiled(table, seq, dil):
    ts = _tile_rows(seq)
    return table.reshape(seq // ts, ts // dil, dil, LANES).transpose(0, 2, 1, 3).reshape(seq, LANES)


def _tables_by_residue(table, seq, dil):
    return table.reshape(seq // dil, dil, LANES).transpose(1, 0, 2)


def _qkv_proj(x, gain, w, cos, sin, seq, dils, tag, exchange=()):
    t, d = x.shape
    ts = _tile_rows(seq)
    per_seq = seq // ts
    ng = len(dils)
    tables = [t_ for dil in dils for t_ in (_tables_tiled(cos, seq, dil), _tables_tiled(sin, seq, dil))]

    def body(x_ref, g_ref, w_ref, *refs):
        table_refs, o_refs, h_refs, stage_ref = refs[:2 * ng], refs[2 * ng:3 * ng], refs[3 * ng:4 * ng], refs[4 * ng]
        xv = x_ref[...]
        h_tokens = xv * _rms(xv) * g_ref[...]
        for gi, dil in enumerate(dils):
            n = ts // dil
            h = jnp.concatenate(_split_rows(h_tokens, stage_ref, dil), axis=0).astype(BF16)
            for r in range(dil):
                h_refs[gi][r] = h[r * n:(r + 1) * n]
            acc = lax.dot_general(h, w_ref[gi * QKV_W:(gi + 1) * QKV_W, :], NT, preferred_element_type=F32)
            c, s = table_refs[2 * gi][...], table_refs[2 * gi + 1][...]
            for j in range(QKV_W // LANES):
                cols = slice(j * LANES, (j + 1) * LANES)
                val = acc[:, cols]
                if j < (Q_W + KV_W) // LANES:
                    val = _rope(val, c, s)
                if j < Q_W // LANES:
                    val = val * (SCALE * LOG2E)
                val = val.astype(BF16)
                for r in range(dil):
                    o_refs[gi][r, :, cols] = val[r * n:(r + 1) * n]

    table = pl.BlockSpec((ts, LANES), lambda i: (i % per_seq, 0))
    outs, exchanged = _hosted_call(
        body, exchange, name=f"qkv_proj_{tag}", grid=(t // ts,),
        in_specs=[pl.BlockSpec((ts, d), lambda i: (i, 0)), _resident((1, d)), _resident(w.shape)] + [table] * (2 * ng),
        out_specs=[_res_spec(seq, dil, QKV_W) for dil in dils] + [_res_spec(seq, dil, d) for dil in dils],
        out_shape=[jax.ShapeDtypeStruct(_res_shape(t // seq, seq, dil, QKV_W), BF16) for dil in dils]
                  + [jax.ShapeDtypeStruct(_res_shape(t // seq, seq, dil, d), BF16) for dil in dils],
        scratch_shapes=[_stage(ts, d)], semantics=("parallel",), args=(x, gain, w, *tables))
    return outs[:ng], outs[ng:], exchanged


def _band(bq, wk):
    return lax.broadcasted_iota(jnp.int32, (bq, wk), 0) - lax.broadcasted_iota(jnp.int32, (bq, wk), 1)


def _swap_halves(src_ref, base, dst_ref):
    for c in range(KV_W // LANES):
        dst_ref[c] = pltpu.roll(src_ref[:, base + c * LANES:base + (c + 1) * LANES], HEAD_DIM, 1)


def _pair_operand(src_ref, swapped_ref, base, kv, rows):
    c = kv // 2
    chunk, swapped = src_ref[rows, base + c * LANES:base + (c + 1) * LANES], swapped_ref[c, rows, :]
    lo = lax.broadcasted_iota(jnp.int32, chunk.shape, 1) < HEAD_DIM
    zero = jnp.zeros_like(chunk)
    if kv % 2 == 0:
        return jnp.concatenate([jnp.where(lo, chunk, zero), jnp.where(lo, zero, swapped)], axis=0)
    return jnp.concatenate([jnp.where(lo, swapped, zero), jnp.where(lo, zero, chunk)], axis=0)


def _over_keys(col, wk):
    if wk % LANES:
        return jnp.broadcast_to(col, (col.shape[0], wk))
    wide = jnp.broadcast_to(col, (col.shape[0], LANES))
    return wide if wk == LANES else jnp.concatenate([wide] * (wk // LANES), axis=1)


def _per_step(dil, length):
    return max(1, min(dil, 512 // length))


def _key_rows(bq, w, length):
    return min(bq + 2 * w, length)


def _window(i, bq, w, wk, length):
    q0 = pl.multiple_of(i * bq, bq)
    k0 = pl.multiple_of(jnp.clip(q0 - w, 0, length - wk), min(w, bq))
    return q0, k0


def _attn_fwd(qkv, w, tag, sink=None, exchange=()):
    shape = qkv.shape
    rows_all = _seq_view(qkv)
    nseq, length, _ = rows_all.shape
    bq = min(QUERY_BLOCK, length)
    wk = _key_rows(bq, w, length)
    nb = length // bq
    has_sink = sink is not None
    per_step = _per_step(shape[1], length)

    def body(*refs):
        sink_ref = refs[1] if has_sink else None
        kk_ref, vv_ref = refs[-2:]
        for sub in range(per_step):
            one(refs[0].at[sub], refs[-4].at[sub], refs[-3].at[sub], sink_ref, kk_ref, vv_ref)

    def one(qkv_ref, o_ref, lse_ref, sink_ref, kk_ref, vv_ref):
        _swap_halves(qkv_ref, Q_W, kk_ref)
        _swap_halves(qkv_ref, Q_W + KV_W, vv_ref)
        band = _band(bq, wk)
        lane = lax.broadcasted_iota(jnp.int32, (bq, LANES), 1)
        lo = lane < HEAD_DIM

        def block(i, carry):
            q0, k0 = _window(i, bq, w, wk, length)
            valid = jnp.abs(band + (q0 - k0)) <= w
            rows, krows = pl.ds(q0, bq), pl.ds(k0, wk)
            lse_tile = jnp.zeros((bq, LANES), F32)
            for first in range(0, N_HEADS // 2, PAIRS_PER_PHASE):
                pairs = range(first, first + PAIRS_PER_PHASE)
                heads = [2 * j + half for j in pairs for half in range(2)]
                qp = [qkv_ref[rows, j * LANES:(j + 1) * LANES] for j in pairs]
                k2 = {kv: _pair_operand(qkv_ref, kk_ref, Q_W, kv, krows) for kv in {j // 2 for j in pairs}}
                v2 = {kv: _pair_operand(qkv_ref, vv_ref, Q_W + KV_W, kv, krows) for kv in k2}
                sc2 = [lax.dot_general(q_, k2[j // 2], NT, preferred_element_type=F32) for q_, j in zip(qp, pairs)]
                sc = [jnp.where(valid, s_[:, half * wk:(half + 1) * wk], NEG_INF) for s_ in sc2 for half in range(2)]
                m = [jnp.max(s_, axis=-1, keepdims=True) for s_ in sc]
                if has_sink:
                    m = [jnp.maximum(m_, sink_ref[hd]) for m_, hd in zip(m, heads)]
                mb = [jnp.broadcast_to(m_, (bq, LANES)) for m_ in m]
                p = [jnp.exp2(s_ - _over_keys(m_, wk)) for s_, m_ in zip(sc, m)]
                den = [jnp.sum(p_, axis=-1, keepdims=True) for p_ in p]
                if has_sink:
                    den = [d_ + jnp.exp2(sink_ref[hd] - m_) for d_, m_, hd in zip(den, m, heads)]
                inv = [jnp.broadcast_to(1.0 / d_, (bq, LANES)) for d_ in den]
                pb = [p_.astype(BF16) for p_ in p]
                for n_, j in enumerate(pairs):
                    o = jnp.dot(jnp.concatenate([pb[2 * n_], pb[2 * n_ + 1]], axis=1), v2[j // 2], preferred_element_type=F32)
                    o = o * jnp.where(lo, inv[2 * n_], inv[2 * n_ + 1])
                    o_ref[rows, j * LANES:(j + 1) * LANES] = o.astype(BF16)
                for n_, hd in enumerate(heads):
                    lse_tile = jnp.where(lane == hd, mb[n_] - jnp.log(inv[n_]) * LOG2E, lse_tile)
            lse_ref[rows, :] = lse_tile
            return carry

        lax.fori_loop(0, nb, block, 0)

    def seq_block(c):
        return pl.BlockSpec((per_step, length, c), lambda i: (i, 0, 0))

    args = [rows_all]
    in_specs = [seq_block(QKV_W)]
    if has_sink:
        args.append(sink * LOG2E)
        in_specs.append(pl.BlockSpec(memory_space=pltpu.SMEM))
    (o, lse), exchanged = _hosted_call(
        body, exchange, name=f"attn_fwd_{tag}", grid=(nseq // per_step,), in_specs=in_specs,
        out_specs=[seq_block(Q_W), seq_block(LANES)],
        out_shape=[jax.ShapeDtypeStruct((nseq, length, Q_W), BF16), jax.ShapeDtypeStruct((nseq, length, LANES), F32)],
        scratch_shapes=[pltpu.VMEM((KV_W // LANES, length, LANES), BF16), pltpu.VMEM((KV_W // LANES, length, LANES), BF16)],
        semantics=("parallel",), args=args)
    return o.reshape(shape[:-1] + (Q_W,)), lse.reshape(shape[:-1] + (LANES,)), exchanged


def _head_expand():
    return (jnp.arange(LANES)[:, None] == jnp.arange(Q_W)[None, :] // HEAD_DIM).astype(BF16)


def _mix_groups(os, lses, dils, seq, tag):
    bl = os[0].shape[0]
    ts = _tile_rows(seq)
    t = bl * seq
    ng = len(os)
    if ng == 1 and dils[0] == 1:
        return os[0].reshape(t, Q_W), [lses[0]]

    def body(*refs):
        e_ref = refs[0]
        o_refs, l_refs = refs[1:1 + ng], refs[1 + ng:1 + 2 * ng]
        om_ref = refs[1 + 2 * ng]
        lt_refs = refs[2 + 2 * ng:2 + 3 * ng]
        wide_ref, narrow_ref = refs[2 + 3 * ng:]
        ls = [_merge_rows([l_refs[g][r] for r in range(dils[g])], narrow_ref, dils[g]) for g in range(ng)]
        mx = functools.reduce(jnp.maximum, ls)
        tot = mx + jnp.log(functools.reduce(lambda a, b: a + b, [jnp.exp2(l_ - mx) for l_ in ls])) * LOG2E
        e = e_ref[...]
        o = None
        for g in range(ng):
            wt = jnp.exp2(ls[g] - tot)
            hi = wt.astype(BF16)
            lo = (wt - hi.astype(F32)).astype(BF16)
            wide = jnp.dot(hi, e, preferred_element_type=F32) + jnp.dot(lo, e, preferred_element_type=F32)
            term = wide * _merge_rows([o_refs[g][r].astype(F32) for r in range(dils[g])], wide_ref, dils[g])
            o = term if o is None else o + term
        om_ref[...] = o.astype(BF16)
        for g in range(ng):
            for r, part in enumerate(_split_rows(tot, narrow_ref, dils[g])):
                lt_refs[g][r] = part

    e = _head_expand()
    outs = pl.pallas_call(
        body, name=f"mix_groups_{tag}", grid=(t // ts,),
        in_specs=[_resident(e.shape)] + [_res_spec(seq, dl, Q_W) for dl in dils] + [_res_spec(seq, dl, LANES) for dl in dils],
        out_specs=[pl.BlockSpec((ts, Q_W), lambda i: (i, 0))] + [_res_spec(seq, dl, LANES) for dl in dils],
        out_shape=[jax.ShapeDtypeStruct((t, Q_W), BF16)]
                  + [jax.ShapeDtypeStruct(_res_shape(bl, seq, dl, LANES), F32) for dl in dils],
        scratch_shapes=[_stage(ts, Q_W), _stage(ts, LANES)],
        compiler_params=_params("parallel"),
    )(e, *os, *lses)
    return outs[0], list(outs[1:])


def _sigmoid(g):
    return 1.0 / (1.0 + jnp.exp(-g))


def _ffn_fwd(x0, o, w_out, gain, wg, wu, wd, tag, exchange=(), loss_head=None):
    t, d = x0.shape
    f = wd.shape[0]
    tm = min(256, t)
    has_loss = loss_head is not None

    def body(*refs):
        x0_ref, o_ref, wo_ref, gain_ref, wg_ref, wu_ref, wd_ref = refs[:7]
        x_ref, y_ref, g_ref, u_ref, a_ref, h_ref = refs[-8:-2] if has_loss else refs[-6:]
        xv = x0_ref[...] + jnp.dot(o_ref[...], wo_ref[...], preferred_element_type=F32)
        x_ref[...] = xv
        h = (xv * _rms(xv) * gain_ref[...]).astype(BF16)
        h_ref[...] = h
        g = lax.dot_general(h, wg_ref[...], NT, preferred_element_type=F32)
        u = lax.dot_general(h, wu_ref[...], NT, preferred_element_type=F32)
        g_ref[...] = g.astype(BF16)
        u_ref[...] = u.astype(BF16)
        a = (g * _sigmoid(g) * u).astype(BF16)
        a_ref[...] = a
        y = xv + jnp.dot(a, wd_ref[...], preferred_element_type=F32)
        if not has_loss:
            y_ref[...] = y
            return
        head_ref, target_ref, loss_ref, dhead_ref = refs[7], refs[8], refs[-2], refs[-1]
        head = head_ref[...]
        yhat = y * _rms(y)
        err = yhat * head - target_ref[...]
        dout = err * (1.0 / d)
        y_ref[...] = _rms_bwd(dout, y, head)[0]
        first = pl.program_id(0) == 0
        part = 0.5 * jnp.sum(jnp.mean(err * err, axis=-1, keepdims=True), axis=0, keepdims=True)
        _accumulate(loss_ref, jnp.broadcast_to(part, loss_ref.shape), first)
        _accumulate(dhead_ref, jnp.sum(dout * yhat, axis=0, keepdims=True), first)

    row = pl.BlockSpec((tm, d), lambda i: (i, 0))
    wide = pl.BlockSpec((tm, f), lambda i: (i, 0))
    in_specs = [row, pl.BlockSpec((tm, Q_W), lambda i: (i, 0)), _resident(w_out.shape), _resident((1, d)), _resident(wg.shape),
                _resident(wu.shape), _resident(wd.shape)]
    out_specs = [row, row, wide, wide, wide, row]
    out_shape = ([jax.ShapeDtypeStruct((t, d), F32)] * 2 + [jax.ShapeDtypeStruct((t, f), BF16)] * 3
                 + [jax.ShapeDtypeStruct((t, d), BF16)])
    if has_loss:
        in_specs += [_resident((1, d)), row]
        out_specs += [pl.BlockSpec((1, LANES), lambda i: (0, 0)), pl.BlockSpec((1, d), lambda i: (0, 0))]
        out_shape += [jax.ShapeDtypeStruct((1, LANES), F32), jax.ShapeDtypeStruct((1, d), F32)]
    outs, exchanged = _hosted_call(
        body, exchange, name=f"ffn_fwd_{tag}", grid=(t // tm,), in_specs=in_specs, out_specs=out_specs, out_shape=out_shape,
        scratch_shapes=[], semantics=("arbitrary" if has_loss else "parallel",),
        args=(x0, o, w_out, gain, wg, wu, wd) + (tuple(loss_head) if has_loss else ()))
    return (*outs, exchanged)


def _ffn_bwd(dy, x, gain, g, u, wg, wu, wd, tag, exchange=()):
    t, d = x.shape
    f = wd.shape[0]
    tm = min(256, t)

    def body(dy_ref, x_ref, gain_ref, g_ref, u_ref, wg_ref, wu_ref, wd_ref, dx_ref, dg_ref, du_ref, dgain_ref):
        dyv = dy_ref[...]
        da = lax.dot_general(dyv.astype(BF16), wd_ref[...], NT, preferred_element_type=F32)
        gv, uv = g_ref[...].astype(F32), u_ref[...].astype(F32)
        sg = _sigmoid(gv)
        act = gv * sg
        du = (da * act).astype(BF16)
        dg = (da * uv * (sg * (1.0 + gv * (1.0 - sg)))).astype(BF16)
        du_ref[...] = du
        dg_ref[...] = dg
        dh = jnp.dot(dg, wg_ref[...], preferred_element_type=F32) + jnp.dot(du, wu_ref[...], preferred_element_type=F32)
        xv, gain_v = x_ref[...], gain_ref[...]
        dx, xhat = _rms_bwd(dh, xv, gain_v)
        dx_ref[...] = dyv + dx
        _accumulate(dgain_ref, jnp.sum(dh * xhat, axis=0, keepdims=True), pl.program_id(0) == 0)

    row = pl.BlockSpec((tm, d), lambda i: (i, 0))
    wide = pl.BlockSpec((tm, f), lambda i: (i, 0))
    outs, exchanged = _hosted_call(
        body, exchange, name=f"ffn_bwd_{tag}", grid=(t // tm,),
        in_specs=[row, row, _resident((1, d)), wide, wide, _resident(wg.shape), _resident(wu.shape), _resident(wd.shape)],
        out_specs=[row, wide, wide, pl.BlockSpec((1, d), lambda i: (0, 0))],
        out_shape=[jax.ShapeDtypeStruct((t, d), F32), jax.ShapeDtypeStruct((t, f), BF16), jax.ShapeDtypeStruct((t, f), BF16),
                   jax.ShapeDtypeStruct((1, d), F32)],
        scratch_shapes=[], semantics=("arbitrary",), args=(dy, x, gain, g, u, wg, wu, wd))
    return (*outs, exchanged)


def _tn_matmul(a, b, name, into=None, row_block=0, row_blocks=1):
    t, k = a.shape
    n = b.shape[1]
    tk = k // 2 if (k // 2) % LANES == 0 else k
    tt = min(2048, t)
    first = row_block * (k // tk)

    def body(a_ref, b_ref, *rest):
        o_ref, acc_ref = rest[-2:]
        prod = lax.dot_general(a_ref[...].astype(BF16), b_ref[...].astype(BF16), TN, preferred_element_type=F32)
        j = pl.program_id(1)

        @pl.when(j == 0)
        def _():
            acc_ref[...] = prod

        @pl.when(j > 0)
        def _():
            acc_ref[...] += prod

        @pl.when(j == pl.num_programs(1) - 1)
        def _():
            o_ref[...] = acc_ref[...].astype(BF16)

    return pl.pallas_call(
        body, name=name, grid=(k // tk, t // tt),
        in_specs=[pl.BlockSpec((tt, tk), lambda i, j: (j, i)), pl.BlockSpec((tt, n), lambda i, j: (j, 0))]
                 + ([ANY] if into is not None else []),
        out_specs=pl.BlockSpec((tk, n), lambda i, j: (first + i, 0)),
        out_shape=jax.ShapeDtypeStruct((row_blocks * k, n), BF16),
        scratch_shapes=[pltpu.VMEM((tk, n), F32)],
        input_output_aliases={2: 0} if into is not None else {},
        compiler_params=_params("parallel", "arbitrary"),
    )(a, b, *([into] if into is not None else []))


def _attn_out_bwd(dx, w, o, dils, seq, tag, lse=None, sink=None, exchange=()):
    t, d = dx.shape
    ts = _tile_rows(seq)
    bl = t // seq
    ng = len(dils)
    has_sink = sink is not None
    expand = _head_expand().T

    def body(*refs):
        refs = list(refs)
        dx_ref, w_ref, o_ref, e_ref = refs[:4]
        refs = refs[4:]
        lse_ref, sink_ref = (refs.pop(0), refs.pop(0)) if has_sink else (None, None)
        do_refs, dl_refs = refs[:ng], refs[ng:2 * ng]
        refs = refs[2 * ng:]
        dsink_ref = refs.pop(0) if has_sink else None
        dof_ref, dlf_ref = refs
        do = lax.dot_general(dx_ref[...].astype(BF16), w_ref[...], NT, preferred_element_type=F32)
        prod = do * o_ref[...].astype(F32)
        hi = prod.astype(BF16)
        lo = (prod - hi.astype(F32)).astype(BF16)
        e = e_ref[...]
        dl = jnp.dot(hi, e, preferred_element_type=F32) + jnp.dot(lo, e, preferred_element_type=F32)
        for g in range(ng):
            for r, part in enumerate(_split_rows(do, dof_ref, dils[g])):
                do_refs[g][r] = part.astype(BF16)
            for r, part in enumerate(_split_rows(dl, dlf_ref, dils[g])):
                dl_refs[g][r] = part
        if has_sink:
            part = -jnp.exp2(sink_ref[...] - lse_ref[...]) * dl
            _accumulate(dsink_ref, jnp.sum(part, axis=0, keepdims=True), pl.program_id(0) == 0)

    row = pl.BlockSpec((ts, d), lambda i: (i, 0))
    narrow = pl.BlockSpec((ts, LANES), lambda i: (i, 0))
    args = [dx, w, o, expand]
    in_specs = [row, _resident(w.shape), pl.BlockSpec((ts, Q_W), lambda i: (i, 0)), _resident(expand.shape)]
    if has_sink:
        args += [lse, jnp.pad(sink.reshape(1, N_HEADS) * LOG2E, ((0, 0), (0, LANES - N_HEADS)))]
        in_specs += [narrow, _resident((1, LANES))]
    out_specs = [_res_spec(seq, dl, Q_W) for dl in dils] + [_res_spec(seq, dl, LANES) for dl in dils]
    out_shape = ([jax.ShapeDtypeStruct(_res_shape(bl, seq, dl, Q_W), BF16) for dl in dils]
                 + [jax.ShapeDtypeStruct(_res_shape(bl, seq, dl, LANES), F32) for dl in dils])
    if has_sink:
        out_specs.append(pl.BlockSpec((1, LANES), lambda i: (0, 0)))
        out_shape.append(jax.ShapeDtypeStruct((1, LANES), F32))
    outs, exchanged = _hosted_call(
        body, exchange, name=f"attn_out_bwd_{tag}", grid=(t // ts,), in_specs=in_specs, out_specs=out_specs, out_shape=out_shape,
        scratch_shapes=[_stage(ts, Q_W), _stage(ts, LANES)], semantics=("arbitrary" if has_sink else "parallel",), args=args)
    return list(outs[:ng]), list(outs[ng:2 * ng]), (outs[2 * ng] if has_sink else None), exchanged


def _attn_bwd(qkv, do, lse, delta, cos, sin, w, tag, exchange=()):
    shape = qkv.shape
    dil = shape[1]
    rows_all = _seq_view(qkv)
    nseq, length, _ = rows_all.shape
    bq = min(QUERY_BLOCK, length)
    wk = _key_rows(bq, w, length)
    nb = length // bq
    per_step = _per_step(dil, length)

    def body(*refs):
        def sub(i, carry):
            one(*[ref.at[i] for ref in refs[:7]], *refs[7:])
            return carry

        if per_step == 1:
            sub(0, 0)
        else:
            lax.fori_loop(0, per_step, sub, 0)

    def one(qkv_ref, do_ref, lse_ref, dl_ref, cos_ref, sin_ref, dp_ref, kk_ref, vv_ref, dk_ref, dv_ref):
        _swap_halves(qkv_ref, Q_W, kk_ref)
        _swap_halves(qkv_ref, Q_W + KV_W, vv_ref)
        dk_ref[...] = jnp.zeros_like(dk_ref)
        dv_ref[...] = jnp.zeros_like(dv_ref)
        band = _band(bq, wk)
        lo_q = lax.broadcasted_iota(jnp.int32, (bq, LANES), 1) < HEAD_DIM
        hi_q = jnp.logical_not(lo_q)

        def block(i, carry):
            q0, k0 = _window(i, bq, w, wk, length)
            valid = jnp.abs(band + (q0 - k0)) <= w
            rows, krows = pl.ds(q0, bq), pl.ds(k0, wk)
            c, sn = cos_ref[rows, :], -sin_ref[rows, :]
            lse_t, dl_t = lse_ref[rows, :], dl_ref[rows, :]
            zero = jnp.zeros((bq, LANES), BF16)
            for first in range(0, N_KV, KV_PER_PHASE):
                kvs = range(first, first + KV_PER_PHASE)
                pairs = [kv * 2 + j for kv in kvs for j in range(GRP // 2)]
                heads = [2 * j + half for j in pairs for half in range(2)]
                qp = {j: qkv_ref[rows, j * LANES:(j + 1) * LANES] for j in pairs}
                dop = {j: do_ref[rows, j * LANES:(j + 1) * LANES] for j in pairs}
                k2 = {kv: _pair_operand(qkv_ref, kk_ref, Q_W, kv, krows) for kv in kvs}
                v2 = {kv: _pair_operand(qkv_ref, vv_ref, Q_W + KV_W, kv, krows) for kv in kvs}
                sc2 = {j: lax.dot_general(qp[j], k2[j // 2], NT, preferred_element_type=F32) for j in pairs}
                dp2 = {j: lax.dot_general(dop[j], v2[j // 2], NT, preferred_element_type=F32) for j in pairs}
                sc = {hd: sc2[hd // 2][:, (hd % 2) * wk:(hd % 2 + 1) * wk] for hd in heads}
                dp = {hd: dp2[hd // 2][:, (hd % 2) * wk:(hd % 2 + 1) * wk] for hd in heads}
                p = {hd: jnp.exp2(jnp.where(valid, sc[hd], NEG_INF) - _over_keys(lse_t[:, hd:hd + 1], wk)) for hd in heads}
                ds = {hd: (p[hd] * (dp[hd] - _over_keys(dl_t[:, hd:hd + 1], wk))).astype(BF16) for hd in heads}
                pb = {hd: p[hd].astype(BF16) for hd in heads}
                dq = {j: jnp.dot(jnp.concatenate([ds[2 * j], ds[2 * j + 1]], axis=1), k2[j // 2], preferred_element_type=F32) * SCALE
                      for j in pairs}
                own = {kv: range(kv * GRP, (kv + 1) * GRP) for kv in kvs}
                q4 = {kv: jnp.concatenate([jnp.where(lo_q if hd % 2 == 0 else hi_q, qp[hd // 2], zero) for hd in own[kv]], axis=0)
                      for kv in kvs}
                do4 = {kv: jnp.concatenate([jnp.where(lo_q if hd % 2 == 0 else hi_q, dop[hd // 2], zero) for hd in own[kv]], axis=0)
                       for kv in kvs}
                dk = {kv: lax.dot_general(jnp.concatenate([ds[hd] for hd in own[kv]], axis=0), q4[kv], TN,
                                          preferred_element_type=F32) for kv in kvs}
                dv = {kv: lax.dot_general(jnp.concatenate([pb[hd] for hd in own[kv]], axis=0), do4[kv], TN,
                                          preferred_element_type=F32) for kv in kvs}
                for j in pairs:
                    dp_ref[rows, j * LANES:(j + 1) * LANES] = _rope(dq[j], c, sn).astype(BF16)
                for kv in kvs:
                    dk_ref[kv, krows, :] += dk[kv]
                    dv_ref[kv, krows, :] += dv[kv]
            return carry

        lax.fori_loop(0, nb, block, 0)
        lo = lax.broadcasted_iota(jnp.int32, (length, LANES), 1) < HEAD_DIM
        c, sn = cos_ref[...], -sin_ref[...]
        for ch in range(KV_W // LANES):
            halves = []
            for acc_ref in (dk_ref, dv_ref):
                even, odd = acc_ref[2 * ch], acc_ref[2 * ch + 1]
                even = even + pltpu.roll(even, HEAD_DIM, 1)
                odd = odd + pltpu.roll(odd, HEAD_DIM, 1)
                halves.append(jnp.where(lo, even, odd))
            dp_ref[:, Q_W + ch * LANES:Q_W + (ch + 1) * LANES] = _rope(halves[0] * LN2, c, sn).astype(BF16)
            dp_ref[:, Q_W + KV_W + ch * LANES:Q_W + KV_W + (ch + 1) * LANES] = halves[1].astype(BF16)

    def seq_block(c):
        return pl.BlockSpec((per_step, length, c), lambda i: (i, 0, 0))

    table = pl.BlockSpec((per_step, length, LANES), lambda i: (i % (dil // per_step), 0, 0))
    (out,), exchanged = _hosted_call(
        body, exchange, name=f"attn_bwd_{tag}", grid=(nseq // per_step,),
        in_specs=[seq_block(QKV_W), seq_block(Q_W), seq_block(LANES), seq_block(LANES), table, table],
        out_specs=[seq_block(QKV_W)],
        out_shape=[jax.ShapeDtypeStruct((nseq, length, QKV_W), BF16)],
        scratch_shapes=[pltpu.VMEM((KV_W // LANES, length, LANES), BF16), pltpu.VMEM((KV_W // LANES, length, LANES), BF16),
                        pltpu.VMEM((N_KV, length, LANES), F32), pltpu.VMEM((N_KV, length, LANES), F32)],
        semantics=("parallel",), args=(rows_all, _seq_view(do), _seq_view(lse), _seq_view(delta), cos, sin))
    return out.reshape(shape), exchanged


def _qkv_bwd(dy, x, gain, w, dps, dils, seq, tag, exchange=()):
    t, d = x.shape
    ts = _tile_rows(seq)
    ng = len(dps)

    def body(dy_ref, x_ref, gain_ref, w_ref, *refs):
        dp_refs, (dx_ref, dgain_ref, stage_ref) = refs[:ng], refs[ng:]
        dh = None
        for gi in range(ng):
            dil = dils[gi]
            n = ts // dil
            dp = dp_refs[gi][0] if dil == 1 else jnp.concatenate([dp_refs[gi][r] for r in range(dil)], axis=0)
            part = jnp.dot(dp, w_ref[gi * QKV_W:(gi + 1) * QKV_W, :], preferred_element_type=F32)
            part = _merge_rows([part[r * n:(r + 1) * n] for r in range(dil)], stage_ref, dil)
            dh = part if dh is None else dh + part
        xv, gain_v = x_ref[...], gain_ref[...]
        dx, xhat = _rms_bwd(dh, xv, gain_v)
        dx_ref[...] = dy_ref[...] + dx
        _accumulate(dgain_ref, jnp.sum(dh * xhat, axis=0, keepdims=True), pl.program_id(0) == 0)

    row = pl.BlockSpec((ts, d), lambda i: (i, 0))
    (dx, dgain), exchanged = _hosted_call(
        body, exchange, name=f"qkv_bwd_{tag}", grid=(t // ts,),
        in_specs=[row, row, _resident((1, d)), _resident(w.shape)] + [_res_spec(seq, dl, QKV_W) for dl in dils],
        out_specs=[row, pl.BlockSpec((1, d), lambda i: (0, 0))],
        out_shape=[jax.ShapeDtypeStruct((t, d), F32), jax.ShapeDtypeStruct((1, d), F32)],
        scratch_shapes=[_stage(ts, d)], semantics=("arbitrary",), args=(dy, x, gain, w, *dps))
    return dx, dgain, exchanged


ANY = pl.BlockSpec(memory_space=pl.ANY)


def _place():
    x, y, c = lax.axis_index("x"), lax.axis_index("y"), lax.axis_index("c")
    return x, y, c


def _exchange_steps(srcs, dsts, gather, send_sems, recv_sems, local_sems):
    x, y, c = _place()
    me, sibling = (x, y, c), (x, y, 1 - c)
    chips = [(1 - x, y), (x, 1 - y), (1 - x, 1 - y)]
    mine = 4 * x + 2 * y + c

    def slot(a, device):
        px, py, pc = device
        return dsts[a].at[4 * px + 2 * py + pc]

    def passes(a, k, block, to, src=None):
        rows = slot(a, block)
        return pltpu.make_async_remote_copy(src_ref=rows if src is None else src, dst_ref=rows, send_sem=send_sems.at[a, k],
                                            recv_sem=recv_sems.at[a, k], device_id=to, device_id_type=MESH)

    def scatters(a, k):
        peer = mine ^ k
        return pltpu.make_async_remote_copy(
            src_ref=srcs[a].at[peer], dst_ref=dsts[a].at[mine], send_sem=send_sems.at[a, k - 1], recv_sem=recv_sems.at[a, k - 1],
            device_id=(peer // 4, (peer // 2) % 2, peer % 2), device_id_type=MESH)

    def local(a):
        return pltpu.make_async_copy(srcs[a] if gather[a] else srcs[a].at[mine], dsts[a].at[mine], local_sems.at[a])

    def first_copies(a):
        if not gather[a]:
            return [scatters(a, k) for k in range(1, N_DEV)]
        return [passes(a, 0, me, sibling, src=srcs[a])] + [passes(a, 1 + j, me, (*chip, c), src=srcs[a]) for j, chip in enumerate(chips)]

    def start():
        for a in range(len(srcs)):
            local(a).start()
            for cp in first_copies(a):
                cp.start()

    def forward():
        for a in range(len(srcs)):
            if gather[a]:
                for j, chip in enumerate(chips):
                    passes(a, 1 + j, (*chip, c), me).wait_recv()
                    passes(a, 4 + j, (*chip, c), sibling).start()

    def finish():
        for a in range(len(srcs)):
            if gather[a]:
                passes(a, 0, sibling, me).wait_recv()
                for j, chip in enumerate(chips):
                    passes(a, 4 + j, (*chip, 1 - c), me).wait_recv()
                    passes(a, 4 + j, (*chip, c), sibling).wait_send()
                for cp in first_copies(a):
                    cp.wait_send()
            else:
                for cp in first_copies(a):
                    cp.wait()
            local(a).wait()

    return start, forward, finish


def _exchange_scratch(n):
    return [pltpu.SemaphoreType.DMA((n, N_DEV - 1)), pltpu.SemaphoreType.DMA((n, N_DEV - 1)), pltpu.SemaphoreType.DMA((n,))]


def _exchanged_shapes(exchange):
    return [jax.ShapeDtypeStruct(((N_DEV,) + a.shape) if g else a.shape, a.dtype) for a, g in exchange]


def _hosted_call(body, exchange, *, name, grid, in_specs, out_specs, out_shape, scratch_shapes, semantics, args):
    out_specs, out_shape, scratch = list(out_specs), list(out_shape), list(scratch_shapes)
    if not exchange:
        outs = pl.pallas_call(body, name=name, grid=grid, in_specs=in_specs, out_specs=out_specs, out_shape=out_shape,
                              scratch_shapes=scratch, compiler_params=_params(*semantics))(*args)
        return list(outs), []
    n, n_in, n_out, n_scr = len(exchange), len(in_specs), len(out_specs), len(scratch)
    gather = [g for _, g in exchange]
    steps = math.prod(grid)

    def hosted(*refs):
        own_in, x_in = refs[:n_in], refs[n_in:n_in + n]
        own_out, x_out = refs[n_in + n:n_in + n + n_out], refs[n_in + n + n_out:n_in + 2 * n + n_out]
        own_scr, sems = refs[n_in + 2 * n + n_out:n_in + 2 * n + n_out + n_scr], refs[-3:]
        step = pl.program_id(0)
        for axis in range(1, len(grid)):
            step = step * grid[axis] + pl.program_id(axis)
        start, forward, finish = _exchange_steps(x_in, x_out, gather, *sems)
        pl.when(step == 0)(start)
        body(*own_in, *own_out, *own_scr)
        pl.when(step == steps // 2)(forward)
        pl.when(step == steps - 1)(finish)

    outs = pl.pallas_call(
        hosted, name=name, grid=grid, in_specs=list(in_specs) + [ANY] * n, out_specs=out_specs + [ANY] * n,
        out_shape=out_shape + _exchanged_shapes(exchange), scratch_shapes=scratch + _exchange_scratch(n),
        compiler_params=_params(*["arbitrary"] * len(grid)),
    )(*args, *[a for a, _ in exchange])
    return list(outs[:n_out]), list(outs[n_out:])


def _exchange_now(exchange, name):
    n = len(exchange)
    gather = [g for _, g in exchange]

    def body(*refs):
        for step in _exchange_steps(refs[:n], refs[n:2 * n], gather, *refs[2 * n:]):
            step()

    return pl.pallas_call(
        body, name=name, in_specs=[ANY] * n, out_specs=[ANY] * n, out_shape=_exchanged_shapes(exchange),
        scratch_shapes=_exchange_scratch(n),
    )(*[a for a, _ in exchange])


def _all_reduce_small(v):
    def body(v_ref, o_ref, recv_ref, send_sems, recv_sems):
        x, y, c = _place()
        me = 4 * x + 2 * y + c
        copies = []
        for k in range(1, N_DEV):
            peer = me ^ k
            copies.append(pltpu.make_async_remote_copy(
                src_ref=v_ref, dst_ref=recv_ref.at[k], send_sem=send_sems.at[k - 1], recv_sem=recv_sems.at[k - 1],
                device_id=(peer // 4, (peer // 2) % 2, peer % 2), device_id_type=MESH))
        for cp in copies:
            cp.start()
        recv_ref[0] = v_ref[...]
        for cp in copies:
            cp.wait()
        acc = recv_ref[me]
        for src in range(1, N_DEV):
            acc = acc + recv_ref[me ^ src]
        o_ref[...] = acc

    vm = pl.BlockSpec(memory_space=pltpu.VMEM)
    return pl.pallas_call(
        body, name="all_reduce_small", in_specs=[vm], out_specs=vm, out_shape=jax.ShapeDtypeStruct(v.shape, F32),
        scratch_shapes=[pltpu.VMEM((N_DEV,) + v.shape, F32), pltpu.SemaphoreType.DMA((N_DEV - 1,)),
                        pltpu.SemaphoreType.DMA((N_DEV - 1,))],
    )(v)


def _adamw_math(w, g, m, v):
    m = ADAM_B1 * m + (1.0 - ADAM_B1) * g
    v = ADAM_B2 * v + (1.0 - ADAM_B2) * (g * g)
    m_hat = m / (1.0 - ADAM_B1 ** ADAM_STEP)
    v_hat = v / (1.0 - ADAM_B2 ** ADAM_STEP)
    delta = -ADAM_LR * (m_hat / (jnp.sqrt(v_hat) + ADAM_EPS) + ADAM_WD * w)
    return delta, m, v


def _adamw(parts, w, m, v, name, layer=None, into=None):
    r, c = w.shape[-2:]
    tr = r // 2 if r % 16 == 0 and r >= 256 else r
    n = len(parts)

    def body(*refs):
        w_ref, m_ref, v_ref = refs[n:n + 3]
        g_ref, d_ref, nm_ref, nv_ref = refs[-4:]
        g = refs[0][...].astype(F32)
        for p_ref in refs[1:n]:
            g = g + p_ref[...].astype(F32)
        g_ref[...] = g
        d_ref[...], nm_ref[...], nv_ref[...] = _adamw_math(w_ref[...], g, m_ref[...], v_ref[...])

    def slab(slot):
        return pl.BlockSpec((None, tr, c), lambda i: (slot, i, 0))

    tile = pl.BlockSpec((tr, c), lambda i: (i, 0)) if layer is None else slab(layer)
    arrays, in_specs = [], []
    for p in parts:
        if isinstance(p, tuple):
            arrays.append(p[0])
            in_specs.append(slab(p[1]))
        else:
            arrays.append(p)
            in_specs.append(tile)
    kept = list(into) if into is not None else []
    return pl.pallas_call(
        body, name=name, grid=(r // tr,), in_specs=in_specs + [tile] * 3 + [ANY] * len(kept), out_specs=[tile] * 4,
        out_shape=[jax.ShapeDtypeStruct(w.shape, F32)] * 4,
        input_output_aliases={n + 3 + k: k for k in range(len(kept))}, compiler_params=_params("parallel"),
    )(*arrays, w, m, v, *kept)


def _rows(g):
    return g.reshape(-1, g.shape[-1])


def _row_blocks(dw):
    k, n = dw.shape
    return dw.reshape(N_DEV, k // N_DEV, n)


def _pack_rows(rows, width):
    out = None
    for i, r in enumerate(rows):
        r = r.reshape(1, -1).astype(F32)
        r = jnp.pad(r, ((i, 8 - 1 - i), (0, width - r.shape[1])))
        out = r if out is None else out + r
    return out


def _mixer_fwd(x, gain, w_in, cos, sin, seq, groups, tag, sink=None, exchanges=None):
    exchanges = exchanges or {}
    os, lses, got = [], [], {}
    qkvs, hs, got["proj"] = _qkv_proj(x, gain, w_in, cos, sin, seq, [dil for dil, _ in groups], tag,
                                      exchange=exchanges.get("proj", ()))
    for gi, (dil, w) in enumerate(groups):
        o, lse, got[gi] = _attn_fwd(qkvs[gi], w, f"{tag}{gi}", sink=sink, exchange=exchanges.get(gi, ()))
        os.append(o)
        lses.append(lse)
    o, lses = _mix_groups(os, lses, [dl for dl, _ in groups], seq, tag)
    return (qkvs, hs, o, lses), got


def _mixer_bwd(dy, x_in, gain, w_in, w_out, saved, cos, sin, seq, groups, tag, sink=None, exchanges=None, scatter_own=False):
    qkvs, hs, o, lses = saved
    t, d = x_in.shape
    dils = [dl for dl, _ in groups]
    lse_tokens = lses[0].reshape(t, LANES) if sink is not None else None
    dw_out = _tn_matmul(o, dy, f"dw_out_{tag}")
    dos, dls, dsink, early = _attn_out_bwd(dy, w_out, o, dils, seq, tag, lse=lse_tokens, sink=sink,
                                           exchange=_to_send([dw_out]) if scatter_own else ())
    if scatter_own:
        (dw_out,) = early
    exchanges = exchanges or {}
    dps, got = [], {}
    for gi, (dil, w) in enumerate(groups):
        dp, got[gi] = _attn_bwd(qkvs[gi], dos[gi], lses[gi], dls[gi], _tables_by_residue(cos, seq, dil),
                                _tables_by_residue(sin, seq, dil), w, f"{tag}{gi}", exchange=exchanges.get(gi, ()))
        dps.append(dp)
    dw_in = None
    for gi in range(len(groups)):
        dw_in = _tn_matmul(dps[gi].reshape(t, QKV_W), hs[gi].reshape(t, d), f"dw_in_{tag}{gi}", into=dw_in, row_block=gi,
                           row_blocks=len(groups))
    dx, dgain, late = _qkv_bwd(dy, x_in, gain, w_in, dps, dils, seq, tag, exchange=_to_send([dw_in]) if scatter_own else ())
    if scatter_own:
        (dw_in,) = late
    return dx, dw_in, dw_out, dgain, dsink, got


def _ffn_layer_bwd(dy, x_in, gain, saved, wg, wu, wd, tag, exchange=()):
    g, u, act, h = saved
    dx, dg, du, dgain, got = _ffn_bwd(dy, x_in, gain, g, u, wg, wu, wd, tag, exchange=exchange)
    dwd = _tn_matmul(act, dy, f"dw_down_{tag}")
    dwg = _tn_matmul(dg, h, f"dw_gate_{tag}")
    dwu = _tn_matmul(du, h, f"dw_up_{tag}")
    return dx, dwg, dwu, dwd, dgain, got


def _to_send(dws):
    return [(_row_blocks(g), False) for g in dws]


def kernel(x, a_w_in, a_sink, a_w_out, b_w_in, b_w_out, norm_mix, norm_ffn, w_gate, w_up, w_down, final_norm, loss_target, m_a_w_in, m_a_sink, m_a_w_out, m_b_w_in, m_b_w_out, m_norm_mix, m_norm_ffn, m_w_gate, m_w_up, m_w_down, m_final_norm, v_a_w_in, v_a_sink, v_a_w_out, v_b_w_in, v_b_w_out, v_norm_mix, v_norm_ffn, v_w_gate, v_w_up, v_w_down, v_final_norm):
    bl, seq, d = x.shape
    t = bl * seq
    xf = x.reshape(t, d)
    target = loss_target.reshape(t, d)
    cos, sin = _rope_tables(seq)
    groups_a = [(1, ATTN_HALF_WINDOW)]
    groups_b = [(dil, window // 2 // dil) for window, dil in DILATED_GROUPS]

    def flip(w_):
        return jnp.swapaxes(w_, -1, -2)

    a_w_in, m_a_w_in, v_a_w_in, b_w_in, m_b_w_in, v_b_w_in = map(flip, (a_w_in, m_a_w_in, v_a_w_in, b_w_in, m_b_w_in, v_b_w_in))
    w_gate, m_w_gate, v_w_gate, w_up, m_w_up, v_w_up = map(flip, (w_gate, m_w_gate, v_w_gate, w_up, m_w_up, v_w_up))

    def shard(w_, layer):
        return (w_[layer].astype(BF16), True)

    (wa_in,) = map(_rows, _exchange_now([shard(a_w_in, 0)], "gather_first"))

    saved_a, got = _mixer_fwd(xf, norm_mix[0:1], wa_in, cos, sin, seq, groups_a, "a", sink=a_sink[0],
                              exchanges={"proj": [shard(w_down, 0), shard(a_w_out, 0)], 0: [shard(w_gate, 0), shard(w_up, 0)]})
    wg0, wu0, wd0, wa_out = map(_rows, got[0] + got["proj"])
    x1_0, x2_0, *saved_0, got = _ffn_fwd(xf, saved_a[2], wa_out, norm_ffn[0:1], wg0, wu0, wd0, "0",
                                         exchange=[shard(b_w_in, 0), shard(b_w_out, 0)])
    wb_in, wb_out = map(_rows, got)
    saved_b, got = _mixer_fwd(x2_0, norm_mix[1:2], wb_in, cos, sin, seq, groups_b, "b",
                              exchanges={0: [shard(w_gate, 1)], 1: [shard(w_up, 1)], 2: [shard(w_down, 1)]})
    wg1, wu1, wd1 = map(_rows, got[0] + got[1] + got[2])
    x1_1, dy, *saved_1, loss_part, d_final, _ = _ffn_fwd(x2_0, saved_b[2], wb_out, norm_ffn[1:2], wg1, wu1, wd1, "1",
                                                         loss_head=(final_norm.reshape(1, d), target))

    dy, dwg1, dwu1, dwd1, d_nf1, _ = _ffn_layer_bwd(dy, x1_1, norm_ffn[1:2], saved_1, wg1, wu1, wd1, "1")
    dy, dwb_in, dwb_out, d_nm1, _, got = _mixer_bwd(
        dy, x2_0, norm_mix[1:2], wb_in, wb_out, saved_b, cos, sin, seq, groups_b, "b",
        exchanges={0: _to_send([dwg1, dwd1]), 1: _to_send([dwu1])})
    (r_g1, r_d1), (r_u1,) = got[0], got[1]
    dy, dwg0, dwu0, dwd0, d_nf0, (r_b_in, r_b_out) = _ffn_layer_bwd(
        dy, x1_0, norm_ffn[0:1], saved_0, wg0, wu0, wd0, "0", exchange=_to_send([dwb_in, dwb_out]))
    dy, r_a_in, r_a_out, d_nm0, d_sink, got = _mixer_bwd(
        dy, xf, norm_mix[0:1], wa_in, wa_out, saved_a, cos, sin, seq, groups_a, "a", sink=a_sink[0],
        exchanges={0: _to_send([dwg0, dwu0, dwd0])}, scatter_own=True)
    r_g0, r_u0, r_d0 = got[0]
    grad_x = dy.reshape(bl, seq, d)

    def update(received, w_, m_, v_, name):
        out = None
        for layer in reversed(range(len(received))):
            out = _adamw([(received[layer], src) for src in range(N_DEV)], w_, m_, v_, f"adamw_{name}{layer}", layer=layer, into=out)
        return out

    u_a_in = update([r_a_in], a_w_in, m_a_w_in, v_a_w_in, "a_in")
    u_a_out = update([r_a_out], a_w_out, m_a_w_out, v_a_w_out, "a_out")
    u_b_in = update([r_b_in], b_w_in, m_b_w_in, v_b_w_in, "b_in")
    u_b_out = update([r_b_out], b_w_out, m_b_w_out, v_b_w_out, "b_out")
    u_gate = update([r_g0, r_g1], w_gate, m_w_gate, v_w_gate, "gate")
    u_up = update([r_u0, r_u1], w_up, m_w_up, v_w_up, "up")
    u_down = update([r_d0, r_d1], w_down, m_w_down, v_w_down, "down")

    small = _pack_rows([d_nm0, d_nm1, d_nf0, d_nf1, d_final, d_sink, loss_part], d)
    total = _all_reduce_small(small)
    small_w = _pack_rows([norm_mix[0], norm_mix[1], norm_ffn[0], norm_ffn[1], final_norm, a_sink], d)
    small_m = _pack_rows([m_norm_mix[0], m_norm_mix[1], m_norm_ffn[0], m_norm_ffn[1], m_final_norm, m_a_sink], d)
    small_v = _pack_rows([v_norm_mix[0], v_norm_mix[1], v_norm_ffn[0], v_norm_ffn[1], v_final_norm, v_a_sink], d)
    u_small = _adamw([total], small_w, small_m, small_v, "adamw_small")
    loss = total[6, 0]

    outs = []
    for k in range(4):
        sm = u_small[k]
        outs += [flip(u_a_in[k]), sm[5:6, :N_HEADS], u_a_out[k], flip(u_b_in[k]), u_b_out[k], sm[0:2], sm[2:4],
                 flip(u_gate[k]), flip(u_up[k]), u_down[k], sm[4]]
    return (loss, grad_x, *outs)
```

```python
import functools
import math

import jax
import jax.numpy as jnp
from jax import lax
from jax.experimental import pallas as pl
from jax.experimental.pallas import tpu as pltpu

F32 = jnp.float32
BF16 = jnp.bfloat16

HEAD_DIM = 64
N_HEADS = 16
N_KV = 4
GRP = N_HEADS // N_KV
Q_W = N_HEADS * HEAD_DIM
KV_W = N_KV * HEAD_DIM
QKV_W = Q_W + 2 * KV_W
ATTN_HALF_WINDOW = 128
DILATED_GROUPS = ((128, 1), (512, 4), (2048, 16))
ROPE_THETA = 10000.0
RMS_EPS = 1e-6
NEG_INF = -1e30
SCALE = 1.0 / math.sqrt(HEAD_DIM)
LOG2E = 1.0 / math.log(2.0)
LN2 = math.log(2.0)

ADAM_LR = 0.001
ADAM_B1 = 0.9
ADAM_B2 = 0.999
ADAM_EPS = 1e-08
ADAM_WD = 0.01
ADAM_STEP = 10

LANES = 128
VMEM_LIMIT = 56 * 1024 * 1024
QUERY_BLOCK = 128
PAIRS_PER_PHASE = 4
KV_PER_PHASE = 2
N_DEV = 8
MESH = pl.DeviceIdType.MESH

NT = (((1,), (1,)), ((), ()))
TN = (((0,), (0,)), ((), ()))


def _params(*sem):
    return pltpu.CompilerParams(dimension_semantics=tuple(sem) if sem else None, vmem_limit_bytes=VMEM_LIMIT)


def _resident(shape):
    return pl.BlockSpec(shape, lambda *_: (0,) * len(shape), pipeline_mode=pl.Buffered(1))


def _rope_tables(seq):
    inv_freq = 1.0 / (ROPE_THETA ** (jnp.arange(0, HEAD_DIM, 2, dtype=F32) / HEAD_DIM))
    ang = jnp.arange(seq, dtype=F32)[:, None] * inv_freq[None, :]
    cos, sin = jnp.cos(ang), jnp.sin(ang)
    return jnp.tile(cos, (1, 4)), jnp.concatenate([-sin, sin, -sin, sin], axis=1)


def _rope(t, cos, sin_signed):
    lane = lax.broadcasted_iota(jnp.int32, t.shape, 1)
    first = (lane & (HEAD_DIM // 2)) == 0
    swapped = jnp.where(first, pltpu.roll(t, LANES - HEAD_DIM // 2, 1), pltpu.roll(t, HEAD_DIM // 2, 1))
    return t * cos + swapped * sin_signed


def _rms(x):
    return lax.rsqrt(jnp.mean(x * x, axis=-1, keepdims=True) + RMS_EPS)


def _rms_bwd(dh, x, gain):
    r = _rms(x)
    xhat = x * r
    dxh = dh * gain
    dx = r * (dxh - xhat * jnp.mean(dxh * xhat, axis=-1, keepdims=True))
    return dx, xhat


def _accumulate(ref, value, first):
    @pl.when(first)
    def _():
        ref[...] = jnp.zeros_like(ref)

    ref[...] += value


def _tile_rows(seq):
    return min(512, seq)


def _res_shape(bl, seq, dil, c):
    ts = _tile_rows(seq)
    return (bl, dil, seq // ts, ts // dil, c)


def _res_spec(seq, dil, c):
    ts = _tile_rows(seq)
    per_seq = seq // ts
    return pl.BlockSpec((None, dil, None, ts // dil, c), lambda i: (i // per_seq, 0, i % per_seq, 0, 0))


def _seq_view(a):
    bl, dil, tiles, n, c = a.shape
    return a.reshape(bl * dil, tiles * n, c)


def _stage(ts, c):
    return pltpu.VMEM((c // LANES, ts, LANES), F32)


def _split_rows(val, stage_ref, dil):
    if dil == 1:
        return [val]
    ts, c = val.shape
    n, nc = ts // dil, c // LANES
    for k in range(nc):
        stage_ref[k] = val[:, k * LANES:(k + 1) * LANES]
    return [jnp.concatenate([stage_ref[k, pl.ds(r, n, stride=dil), :] for k in range(nc)], axis=1) for r in range(dil)]


def _merge_rows(parts, stage_ref, dil):
    if dil == 1:
        return parts[0]
    n, c = parts[0].shape
    nc = c // LANES
    for r, part in enumerate(parts):
        for k in range(nc):
            stage_ref[k, pl.ds(r, n, stride=dil), :] = part[:, k * LANES:(k + 1) * LANES]
    return jnp.concatenate([stage_ref[k] for k in range(nc)], axis=1)


def _tables_tiled(table, seq, dil):
    ts = _tile_rows(seq)
    return table.reshape(seq // ts, ts // dil, dil, LANES).transpose(0, 2, 1, 3).reshape(seq, LANES)


def _tables_by_residue(table, seq, dil):
    return table.reshape(seq // dil, dil, LANES).transpose(1, 0, 2)


def _qkv_proj(x, gain, w, cos, sin, seq, dils, tag, exchange=()):
    t, d = x.shape
    ts = _tile_rows(seq)
    per_seq = seq // ts
    ng = len(dils)
    tables = [t_ for dil in dils for t_ in (_tables_tiled(cos, seq, dil), _tables_tiled(sin, seq, dil))]

    def body(x_ref, g_ref, w_ref, *refs):
        table_refs, o_refs, h_refs, stage_ref = refs[:2 * ng], refs[2 * ng:3 * ng], refs[3 * ng:4 * ng], refs[4 * ng]
        xv = x_ref[...]
        h_tokens = xv * _rms(xv) * g_ref[...]
        for gi, dil in enumerate(dils):
            n = ts // dil
            h = jnp.concatenate(_split_rows(h_tokens, stage_ref, dil), axis=0).astype(BF16)
            for r in range(dil):
                h_refs[gi][r] = h[r * n:(r + 1) * n]
            acc = lax.dot_general(h, w_ref[gi * QKV_W:(gi + 1) * QKV_W, :], NT, preferred_element_type=F32)
            c, s = table_refs[2 * gi][...], table_refs[2 * gi + 1][...]
            for j in range(QKV_W // LANES):
                cols = slice(j * LANES, (j + 1) * LANES)
                val = acc[:, cols]
                if j < (Q_W + KV_W) // LANES:
                    val = _rope(val, c, s)
                if j < Q_W // LANES:
                    val = val * (SCALE * LOG2E)
                val = val.astype(BF16)
                for r in range(dil):
                    o_refs[gi][r, :, cols] = val[r * n:(r + 1) * n]

    table = pl.BlockSpec((ts, LANES), lambda i: (i % per_seq, 0))
    outs, exchanged = _hosted_call(
        body, exchange, name=f"qkv_proj_{tag}", grid=(t // ts,),
        in_specs=[pl.BlockSpec((ts, d), lambda i: (i, 0)), _resident((1, d)), _resident(w.shape)] + [table] * (2 * ng),
        out_specs=[_res_spec(seq, dil, QKV_W) for dil in dils] + [_res_spec(seq, dil, d) for dil in dils],
        out_shape=[jax.ShapeDtypeStruct(_res_shape(t // seq, seq, dil, QKV_W), BF16) for dil in dils]
                  + [jax.ShapeDtypeStruct(_res_shape(t // seq, seq, dil, d), BF16) for dil in dils],
        scratch_shapes=[_stage(ts, d)], semantics=("parallel",), args=(x, gain, w, *tables))
    return outs[:ng], outs[ng:], exchanged


def _band(bq, wk):
    return lax.broadcasted_iota(jnp.int32, (bq, wk), 0) - lax.broadcasted_iota(jnp.int32, (bq, wk), 1)


def _swap_halves(src_ref, base, dst_ref):
    for c in range(KV_W // LANES):
        dst_ref[c] = pltpu.roll(src_ref[:, base + c * LANES:base + (c + 1) * LANES], HEAD_DIM, 1)


def _pair_operand(src_ref, swapped_ref, base, kv, rows):
    c = kv // 2
    chunk, swapped = src_ref[rows, base + c * LANES:base + (c + 1) * LANES], swapped_ref[c, rows, :]
    lo = lax.broadcasted_iota(jnp.int32, chunk.shape, 1) < HEAD_DIM
    zero = jnp.zeros_like(chunk)
    if kv % 2 == 0:
        return jnp.concatenate([jnp.where(lo, chunk, zero), jnp.where(lo, zero, swapped)], axis=0)
    return jnp.concatenate([jnp.where(lo, swapped, zero), jnp.where(lo, zero, chunk)], axis=0)


def _over_keys(col, wk):
    if wk % LANES:
        return jnp.broadcast_to(col, (col.shape[0], wk))
    wide = jnp.broadcast_to(col, (col.shape[0], LANES))
    return wide if wk == LANES else jnp.concatenate([wide] * (wk // LANES), axis=1)


def _per_step(dil, length):
    return max(1, min(dil, 512 // length))


def _key_rows(bq, w, length):
    return min(bq + 2 * w, length)


def _window(i, bq, w, wk, length):
    q0 = pl.multiple_of(i * bq, bq)
    k0 = pl.multiple_of(jnp.clip(q0 - w, 0, length - wk), min(w, bq))
    return q0, k0


def _attn_fwd(qkv, w, tag, sink=None, exchange=()):
    shape = qkv.shape
    rows_all = _seq_view(qkv)
    nseq, length, _ = rows_all.shape
    bq = min(QUERY_BLOCK, length)
    wk = _key_rows(bq, w, length)
    nb = length // bq
    has_sink = sink is not None
    per_step = _per_step(shape[1], length)

    def body(*refs):
        sink_ref = refs[1] if has_sink else None
        kk_ref, vv_ref = refs[-2:]
        for sub in range(per_step):
            one(refs[0].at[sub], refs[-4].at[sub], refs[-3].at[sub], sink_ref, kk_ref, vv_ref)

    def one(qkv_ref, o_ref, lse_ref, sink_ref, kk_ref, vv_ref):
        _swap_halves(qkv_ref, Q_W, kk_ref)
        _swap_halves(qkv_ref, Q_W + KV_W, vv_ref)
        band = _band(bq, wk)
        lane = lax.broadcasted_iota(jnp.int32, (bq, LANES), 1)
        lo = lane < HEAD_DIM

        def block(i, carry):
            q0, k0 = _window(i, bq, w, wk, length)
            valid = jnp.abs(band + (q0 - k0)) <= w
            rows, krows = pl.ds(q0, bq), pl.ds(k0, wk)
            lse_tile = jnp.zeros((bq, LANES), F32)
            for first in range(0, N_HEADS // 2, PAIRS_PER_PHASE):
                pairs = range(first, first + PAIRS_PER_PHASE)
                heads = [2 * j + half for j in pairs for half in range(2)]
                qp = [qkv_ref[rows, j * LANES:(j + 1) * LANES] for j in pairs]
                k2 = {kv: _pair_operand(qkv_ref, kk_ref, Q_W, kv, krows) for kv in {j // 2 for j in pairs}}
                v2 = {kv: _pair_operand(qkv_ref, vv_ref, Q_W + KV_W, kv, krows) for kv in k2}
                sc2 = [lax.dot_general(q_, k2[j // 2], NT, preferred_element_type=F32) for q_, j in zip(qp, pairs)]
                sc = [jnp.where(valid, s_[:, half * wk:(half + 1) * wk], NEG_INF) for s_ in sc2 for half in range(2)]
                m = [jnp.max(s_, axis=-1, keepdims=True) for s_ in sc]
                if has_sink:
                    m = [jnp.maximum(m_, sink_ref[hd]) for m_, hd in zip(m, heads)]
                mb = [jnp.broadcast_to(m_, (bq, LANES)) for m_ in m]
                p = [jnp.exp2(s_ - _over_keys(m_, wk)) for s_, m_ in zip(sc, m)]
                den = [jnp.sum(p_, axis=-1, keepdims=True) for p_ in p]
                if has_sink:
                    den = [d_ + jnp.exp2(sink_ref[hd] - m_) for d_, m_, hd in zip(den, m, heads)]
                inv = [jnp.broadcast_to(1.0 / d_, (bq, LANES)) for d_ in den]
                pb = [p_.astype(BF16) for p_ in p]
                for n_, j in enumerate(pairs):
                    o = jnp.dot(jnp.concatenate([pb[2 * n_], pb[2 * n_ + 1]], axis=1), v2[j // 2], preferred_element_type=F32)
                    o = o * jnp.where(lo, inv[2 * n_], inv[2 * n_ + 1])
                    o_ref[rows, j * LANES:(j + 1) * LANES] = o.astype(BF16)
                for n_, hd in enumerate(heads):
                    lse_tile = jnp.where(lane == hd, mb[n_] - jnp.log(inv[n_]) * LOG2E, lse_tile)
            lse_ref[rows, :] = lse_tile
            return carry

        lax.fori_loop(0, nb, block, 0)

    def seq_block(c):
        return pl.BlockSpec((per_step, length, c), lambda i: (i, 0, 0))

    args = [rows_all]
    in_specs = [seq_block(QKV_W)]
    if has_sink:
        args.append(sink * LOG2E)
        in_specs.append(pl.BlockSpec(memory_space=pltpu.SMEM))
    (o, lse), exchanged = _hosted_call(
        body, exchange, name=f"attn_fwd_{tag}", grid=(nseq // per_step,), in_specs=in_specs,
        out_specs=[seq_block(Q_W), seq_block(LANES)],
        out_shape=[jax.ShapeDtypeStruct((nseq, length, Q_W), BF16), jax.ShapeDtypeStruct((nseq, length, LANES), F32)],
        scratch_shapes=[pltpu.VMEM((KV_W // LANES, length, LANES), BF16), pltpu.VMEM((KV_W // LANES, length, LANES), BF16)],
        semantics=("parallel",), args=args)
    return o.reshape(shape[:-1] + (Q_W,)), lse.reshape(shape[:-1] + (LANES,)), exchanged


def _head_expand():
    return (jnp.arange(LANES)[:, None] == jnp.arange(Q_W)[None, :] // HEAD_DIM).astype(BF16)


def _mix_groups(os, lses, dils, seq, tag):
    bl = os[0].shape[0]
    ts = _tile_rows(seq)
    t = bl * seq
    ng = len(os)
    if ng == 1 and dils[0] == 1:
        return os[0].reshape(t, Q_W), [lses[0]]

    def body(*refs):
        e_ref = refs[0]
        o_refs, l_refs = refs[1:1 + ng], refs[1 + ng:1 + 2 * ng]
        om_ref = refs[1 + 2 * ng]
        lt_refs = refs[2 + 2 * ng:2 + 3 * ng]
        wide_ref, narrow_ref = refs[2 + 3 * ng:]
        ls = [_merge_rows([l_refs[g][r] for r in range(dils[g])], narrow_ref, dils[g]) for g in range(ng)]
        mx = functools.reduce(jnp.maximum, ls)
        tot = mx + jnp.log(functools.reduce(lambda a, b: a + b, [jnp.exp2(l_ - mx) for l_ in ls])) * LOG2E
        e = e_ref[...]
        o = None
        for g in range(ng):
            wt = jnp.exp2(ls[g] - tot)
            hi = wt.astype(BF16)
            lo = (wt - hi.astype(F32)).astype(BF16)
            wide = jnp.dot(hi, e, preferred_element_type=F32) + jnp.dot(lo, e, preferred_element_type=F32)
            term = wide * _merge_rows([o_refs[g][r].astype(F32) for r in range(dils[g])], wide_ref, dils[g])
            o = term if o is None else o + term
        om_ref[...] = o.astype(BF16)
        for g in range(ng):
            for r, part in enumerate(_split_rows(tot, narrow_ref, dils[g])):
                lt_refs[g][r] = part

    e = _head_expand()
    outs = pl.pallas_call(
        body, name=f"mix_groups_{tag}", grid=(t // ts,),
        in_specs=[_resident(e.shape)] + [_res_spec(seq, dl, Q_W) for dl in dils] + [_res_spec(seq, dl, LANES) for dl in dils],
        out_specs=[pl.BlockSpec((ts, Q_W), lambda i: (i, 0))] + [_res_spec(seq, dl, LANES) for dl in dils],
        out_shape=[jax.ShapeDtypeStruct((t, Q_W), BF16)]
                  + [jax.ShapeDtypeStruct(_res_shape(bl, seq, dl, LANES), F32) for dl in dils],
        scratch_shapes=[_stage(ts, Q_W), _stage(ts, LANES)],
        compiler_params=_params("parallel"),
    )(e, *os, *lses)
    return outs[0], list(outs[1:])


def _sigmoid(g):
    return 1.0 / (1.0 + jnp.exp(-g))


def _ffn_fwd(x0, o, w_out, gain, wg, wu, wd, tag, exchange=(), loss_head=None):
    t, d = x0.shape
    f = wd.shape[0]
    tm = min(256, t)
    has_loss = loss_head is not None

    def body(*refs):
        x0_ref, o_ref, wo_ref, gain_ref, wg_ref, wu_ref, wd_ref = refs[:7]
        x_ref, y_ref, g_ref, u_ref, a_ref, h_ref = refs[-8:-2] if has_loss else refs[-6:]
        xv = x0_ref[...] + jnp.dot(o_ref[...], wo_ref[...], preferred_element_type=F32)
        x_ref[...] = xv
        h = (xv * _rms(xv) * gain_ref[...]).astype(BF16)
        h_ref[...] = h
        g = lax.dot_general(h, wg_ref[...], NT, preferred_element_type=F32)
        u = lax.dot_general(h, wu_ref[...], NT, preferred_element_type=F32)
        halves = []
        for cols in (slice(0, f // 2), slice(f // 2, f)):
            gh, uh = g[:, cols], u[:, cols]
            g_ref[:, cols] = gh.astype(BF16)
            u_ref[:, cols] = uh.astype(BF16)
            halves.append((gh * _sigmoid(gh) * uh).astype(BF16))
            a_ref[:, cols] = halves[-1]
        a = jnp.concatenate(halves, axis=1)
        y = xv + jnp.dot(a, wd_ref[...], preferred_element_type=F32)
        if not has_loss:
            y_ref[...] = y
            return
        head_ref, target_ref, loss_ref, dhead_ref = refs[7], refs[8], refs[-2], refs[-1]
        head = head_ref[...]
        yhat = y * _rms(y)
        err = yhat * head - target_ref[...]
        dout = err * (1.0 / d)
        y_ref[...] = _rms_bwd(dout, y, head)[0]
        first = pl.program_id(0) == 0
        part = 0.5 * jnp.sum(jnp.mean(err * err, axis=-1, keepdims=True), axis=0, keepdims=True)
        _accumulate(loss_ref, jnp.broadcast_to(part, loss_ref.shape), first)
        _accumulate(dhead_ref, jnp.sum(dout * yhat, axis=0, keepdims=True), first)

    row = pl.BlockSpec((tm, d), lambda i: (i, 0))
    wide = pl.BlockSpec((tm, f), lambda i: (i, 0))
    in_specs = [row, pl.BlockSpec((tm, Q_W), lambda i: (i, 0)), _resident(w_out.shape), _resident((1, d)), _resident(wg.shape),
                _resident(wu.shape), _resident(wd.shape)]
    out_specs = [row, row, wide, wide, wide, row]
    out_shape = ([jax.ShapeDtypeStruct((t, d), F32)] * 2 + [jax.ShapeDtypeStruct((t, f), BF16)] * 3
                 + [jax.ShapeDtypeStruct((t, d), BF16)])
    if has_loss:
        in_specs += [_resident((1, d)), row]
        out_specs += [pl.BlockSpec((1, LANES), lambda i: (0, 0)), pl.BlockSpec((1, d), lambda i: (0, 0))]
        out_shape += [jax.ShapeDtypeStruct((1, LANES), F32), jax.ShapeDtypeStruct((1, d), F32)]
    outs, exchanged = _hosted_call(
        body, exchange, name=f"ffn_fwd_{tag}", grid=(t // tm,), in_specs=in_specs, out_specs=out_specs, out_shape=out_shape,
        scratch_shapes=[], semantics=("arbitrary" if has_loss else "parallel",),
        args=(x0, o, w_out, gain, wg, wu, wd) + (tuple(loss_head) if has_loss else ()))
    return (*outs, exchanged)


def _ffn_bwd(dy, x, gain, g, u, wg, wu, wd, tag, exchange=()):
    t, d = x.shape
    f = wd.shape[0]
    tm = min(256, t)

    def body(dy_ref, x_ref, gain_ref, g_ref, u_ref, wg_ref, wu_ref, wd_ref, dx_ref, dg_ref, du_ref, dgain_ref):
        dyv = dy_ref[...]
        da = lax.dot_general(dyv.astype(BF16), wd_ref[...], NT, preferred_element_type=F32)
        dgs, dus = [], []
        for cols in (slice(0, f // 2), slice(f // 2, f)):
            gv, uv, dah = g_ref[:, cols].astype(F32), u_ref[:, cols].astype(F32), da[:, cols]
            sg = _sigmoid(gv)
            dus.append((dah * (gv * sg)).astype(BF16))
            dgs.append((dah * uv * (sg * (1.0 + gv * (1.0 - sg)))).astype(BF16))
            du_ref[:, cols] = dus[-1]
            dg_ref[:, cols] = dgs[-1]
        du, dg = jnp.concatenate(dus, axis=1), jnp.concatenate(dgs, axis=1)
        dh = jnp.dot(dg, wg_ref[...], preferred_element_type=F32) + jnp.dot(du, wu_ref[...], preferred_element_type=F32)
        xv, gain_v = x_ref[...], gain_ref[...]
        dx, xhat = _rms_bwd(dh, xv, gain_v)
        dx_ref[...] = dyv + dx
        _accumulate(dgain_ref, jnp.sum(dh * xhat, axis=0, keepdims=True), pl.program_id(0) == 0)

    row = pl.BlockSpec((tm, d), lambda i: (i, 0))
    wide = pl.BlockSpec((tm, f), lambda i: (i, 0))
    outs, exchanged = _hosted_call(
        body, exchange, name=f"ffn_bwd_{tag}", grid=(t // tm,),
        in_specs=[row, row, _resident((1, d)), wide, wide, _resident(wg.shape), _resident(wu.shape), _resident(wd.shape)],
        out_specs=[row, wide, wide, pl.BlockSpec((1, d), lambda i: (0, 0))],
        out_shape=[jax.ShapeDtypeStruct((t, d), F32), jax.ShapeDtypeStruct((t, f), BF16), jax.ShapeDtypeStruct((t, f), BF16),
                   jax.ShapeDtypeStruct((1, d), F32)],
        scratch_shapes=[], semantics=("arbitrary",), args=(dy, x, gain, g, u, wg, wu, wd))
    return (*outs, exchanged)


def _tn_matmul(a, b, name, into=None, row_block=0, row_blocks=1):
    t, k = a.shape
    n = b.shape[1]
    tk = k // 2 if (k // 2) % LANES == 0 else k
    tt = min(2048, t)
    first = row_block * (k // tk)

    def body(a_ref, b_ref, *rest):
        o_ref, acc_ref = rest[-2:]
        prod = lax.dot_general(a_ref[...].astype(BF16), b_ref[...].astype(BF16), TN, preferred_element_type=F32)
        j = pl.program_id(1)

        @pl.when(j == 0)
        def _():
            acc_ref[...] = prod

        @pl.when(j > 0)
        def _():
            acc_ref[...] += prod

        @pl.when(j == pl.num_programs(1) - 1)
        def _():
            o_ref[...] = acc_ref[...].astype(BF16)

    return pl.pallas_call(
        body, name=name, grid=(k // tk, t // tt),
        in_specs=[pl.BlockSpec((tt, tk), lambda i, j: (j, i)), pl.BlockSpec((tt, n), lambda i, j: (j, 0))]
                 + ([ANY] if into is not None else []),
        out_specs=pl.BlockSpec((tk, n), lambda i, j: (first + i, 0)),
        out_shape=jax.ShapeDtypeStruct((row_blocks * k, n), BF16),
        scratch_shapes=[pltpu.VMEM((tk, n), F32)],
        input_output_aliases={2: 0} if into is not None else {},
        compiler_params=_params("parallel", "arbitrary"),
    )(a, b, *([into] if into is not None else []))


def _attn_out_bwd(dx, w, o, dils, seq, tag, lse=None, sink=None, exchange=()):
    t, d = dx.shape
    ts = _tile_rows(seq)
    bl = t // seq
    ng = len(dils)
    has_sink = sink is not None
    expand = _head_expand().T

    def body(*refs):
        refs = list(refs)
        dx_ref, w_ref, o_ref, e_ref = refs[:4]
        refs = refs[4:]
        lse_ref, sink_ref = (refs.pop(0), refs.pop(0)) if has_sink else (None, None)
        do_refs, dl_refs = refs[:ng], refs[ng:2 * ng]
        refs = refs[2 * ng:]
        dsink_ref = refs.pop(0) if has_sink else None
        dof_ref, dlf_ref = refs
        do = lax.dot_general(dx_ref[...].astype(BF16), w_ref[...], NT, preferred_element_type=F32)
        prod = do * o_ref[...].astype(F32)
        hi = prod.astype(BF16)
        lo = (prod - hi.astype(F32)).astype(BF16)
        e = e_ref[...]
        dl = jnp.dot(hi, e, preferred_element_type=F32) + jnp.dot(lo, e, preferred_element_type=F32)
        for g in range(ng):
            for r, part in enumerate(_split_rows(do, dof_ref, dils[g])):
                do_refs[g][r] = part.astype(BF16)
            for r, part in enumerate(_split_rows(dl, dlf_ref, dils[g])):
                dl_refs[g][r] = part
        if has_sink:
            part = -jnp.exp2(sink_ref[...] - lse_ref[...]) * dl
            _accumulate(dsink_ref, jnp.sum(part, axis=0, keepdims=True), pl.program_id(0) == 0)

    row = pl.BlockSpec((ts, d), lambda i: (i, 0))
    narrow = pl.BlockSpec((ts, LANES), lambda i: (i, 0))
    args = [dx, w, o, expand]
    in_specs = [row, _resident(w.shape), pl.BlockSpec((ts, Q_W), lambda i: (i, 0)), _resident(expand.shape)]
    if has_sink:
        args += [lse, jnp.pad(sink.reshape(1, N_HEADS) * LOG2E, ((0, 0), (0, LANES - N_HEADS)))]
        in_specs += [narrow, _resident((1, LANES))]
    out_specs = [_res_spec(seq, dl, Q_W) for dl in dils] + [_res_spec(seq, dl, LANES) for dl in dils]
    out_shape = ([jax.ShapeDtypeStruct(_res_shape(bl, seq, dl, Q_W), BF16) for dl in dils]
                 + [jax.ShapeDtypeStruct(_res_shape(bl, seq, dl, LANES), F32) for dl in dils])
    if has_sink:
        out_specs.append(pl.BlockSpec((1, LANES), lambda i: (0, 0)))
        out_shape.append(jax.ShapeDtypeStruct((1, LANES), F32))
    outs, exchanged = _hosted_call(
        body, exchange, name=f"attn_out_bwd_{tag}", grid=(t // ts,), in_specs=in_specs, out_specs=out_specs, out_shape=out_shape,
        scratch_shapes=[_stage(ts, Q_W), _stage(ts, LANES)], semantics=("arbitrary" if has_sink else "parallel",), args=args)
    return list(outs[:ng]), list(outs[ng:2 * ng]), (outs[2 * ng] if has_sink else None), exchanged


def _attn_bwd(qkv, do, lse, delta, cos, sin, w, tag, exchange=()):
    shape = qkv.shape
    dil = shape[1]
    rows_all = _seq_view(qkv)
    nseq, length, _ = rows_all.shape
    bq = min(QUERY_BLOCK, length)
    wk = _key_rows(bq, w, length)
    nb = length // bq
    per_step = _per_step(dil, length)

    def body(*refs):
        def sub(i, carry):
            one(*[ref.at[i] for ref in refs[:7]], *refs[7:])
            return carry

        if per_step == 1:
            sub(0, 0)
        else:
            lax.fori_loop(0, per_step, sub, 0)

    def one(qkv_ref, do_ref, lse_ref, dl_ref, cos_ref, sin_ref, dp_ref, kk_ref, vv_ref, dk_ref, dv_ref):
        _swap_halves(qkv_ref, Q_W, kk_ref)
        _swap_halves(qkv_ref, Q_W + KV_W, vv_ref)
        dk_ref[...] = jnp.zeros_like(dk_ref)
        dv_ref[...] = jnp.zeros_like(dv_ref)
        band = _band(bq, wk)
        lo_q = lax.broadcasted_iota(jnp.int32, (bq, LANES), 1) < HEAD_DIM
        hi_q = jnp.logical_not(lo_q)

        def block(i, carry):
            q0, k0 = _window(i, bq, w, wk, length)
            valid = jnp.abs(band + (q0 - k0)) <= w
            rows, krows = pl.ds(q0, bq), pl.ds(k0, wk)
            c, sn = cos_ref[rows, :], -sin_ref[rows, :]
            lse_t, dl_t = lse_ref[rows, :], dl_ref[rows, :]
            zero = jnp.zeros((bq, LANES), BF16)
            for first in range(0, N_KV, KV_PER_PHASE):
                kvs = range(first, first + KV_PER_PHASE)
                pairs = [kv * 2 + j for kv in kvs for j in range(GRP // 2)]
                heads = [2 * j + half for j in pairs for half in range(2)]
                qp = {j: qkv_ref[rows, j * LANES:(j + 1) * LANES] for j in pairs}
                dop = {j: do_ref[rows, j * LANES:(j + 1) * LANES] for j in pairs}
                k2 = {kv: _pair_operand(qkv_ref, kk_ref, Q_W, kv, krows) for kv in kvs}
                v2 = {kv: _pair_operand(qkv_ref, vv_ref, Q_W + KV_W, kv, krows) for kv in kvs}
                sc2 = {j: lax.dot_general(qp[j], k2[j // 2], NT, preferred_element_type=F32) for j in pairs}
                dp2 = {j: lax.dot_general(dop[j], v2[j // 2], NT, preferred_element_type=F32) for j in pairs}
                sc = {hd: sc2[hd // 2][:, (hd % 2) * wk:(hd % 2 + 1) * wk] for hd in heads}
                dp = {hd: dp2[hd // 2][:, (hd % 2) * wk:(hd % 2 + 1) * wk] for hd in heads}
                p = {hd: jnp.exp2(jnp.where(valid, sc[hd], NEG_INF) - _over_keys(lse_t[:, hd:hd + 1], wk)) for hd in heads}
                ds = {hd: (p[hd] * (dp[hd] - _over_keys(dl_t[:, hd:hd + 1], wk))).astype(BF16) for hd in heads}
                pb = {hd: p[hd].astype(BF16) for hd in heads}
                dq = {j: jnp.dot(jnp.concatenate([ds[2 * j], ds[2 * j + 1]], axis=1), k2[j // 2], preferred_element_type=F32) * SCALE
                      for j in pairs}
                own = {kv: range(kv * GRP, (kv + 1) * GRP) for kv in kvs}
                q4 = {kv: jnp.concatenate([jnp.where(lo_q if hd % 2 == 0 else hi_q, qp[hd // 2], zero) for hd in own[kv]], axis=0)
                      for kv in kvs}
                do4 = {kv: jnp.concatenate([jnp.where(lo_q if hd % 2 == 0 else hi_q, dop[hd // 2], zero) for hd in own[kv]], axis=0)
                       for kv in kvs}
                dk = {kv: lax.dot_general(jnp.concatenate([ds[hd] for hd in own[kv]], axis=0), q4[kv], TN,
                                          preferred_element_type=F32) for kv in kvs}
                dv = {kv: lax.dot_general(jnp.concatenate([pb[hd] for hd in own[kv]], axis=0), do4[kv], TN,
                                          preferred_element_type=F32) for kv in kvs}
                for j in pairs:
                    dp_ref[rows, j * LANES:(j + 1) * LANES] = _rope(dq[j], c, sn).astype(BF16)
                for kv in kvs:
                    dk_ref[kv, krows, :] += dk[kv]
                    dv_ref[kv, krows, :] += dv[kv]
            return carry

        lax.fori_loop(0, nb, block, 0)
        lo = lax.broadcasted_iota(jnp.int32, (length, LANES), 1) < HEAD_DIM
        c, sn = cos_ref[...], -sin_ref[...]
        for ch in range(KV_W // LANES):
            halves = []
            for acc_ref in (dk_ref, dv_ref):
                even, odd = acc_ref[2 * ch], acc_ref[2 * ch + 1]
                even = even + pltpu.roll(even, HEAD_DIM, 1)
                odd = odd + pltpu.roll(odd, HEAD_DIM, 1)
                halves.append(jnp.where(lo, even, odd))
            dp_ref[:, Q_W + ch * LANES:Q_W + (ch + 1) * LANES] = _rope(halves[0] * LN2, c, sn).astype(BF16)
            dp_ref[:, Q_W + KV_W + ch * LANES:Q_W + KV_W + (ch + 1) * LANES] = halves[1].astype(BF16)

    def seq_block(c):
        return pl.BlockSpec((per_step, length, c), lambda i: (i, 0, 0))

    table = pl.BlockSpec((per_step, length, LANES), lambda i: (i % (dil // per_step), 0, 0))
    (out,), exchanged = _hosted_call(
        body, exchange, name=f"attn_bwd_{tag}", grid=(nseq // per_step,),
        in_specs=[seq_block(QKV_W), seq_block(Q_W), seq_block(LANES), seq_block(LANES), table, table],
        out_specs=[seq_block(QKV_W)],
        out_shape=[jax.ShapeDtypeStruct((nseq, length, QKV_W), BF16)],
        scratch_shapes=[pltpu.VMEM((KV_W // LANES, length, LANES), BF16), pltpu.VMEM((KV_W // LANES, length, LANES), BF16),
                        pltpu.VMEM((N_KV, length, LANES), F32), pltpu.VMEM((N_KV, length, LANES), F32)],
        semantics=("parallel",), args=(rows_all, _seq_view(do), _seq_view(lse), _seq_view(delta), cos, sin))
    return out.reshape(shape), exchanged


def _qkv_bwd(dy, x, gain, w, dps, dils, seq, tag, exchange=()):
    t, d = x.shape
    ts = _tile_rows(seq)
    ng = len(dps)

    def body(dy_ref, x_ref, gain_ref, w_ref, *refs):
        dp_refs, (dx_ref, dgain_ref, stage_ref) = refs[:ng], refs[ng:]
        dh = None
        for gi in range(ng):
            dil = dils[gi]
            n = ts // dil
            dp = dp_refs[gi][0] if dil == 1 else jnp.concatenate([dp_refs[gi][r] for r in range(dil)], axis=0)
            part = jnp.dot(dp, w_ref[gi * QKV_W:(gi + 1) * QKV_W, :], preferred_element_type=F32)
            part = _merge_rows([part[r * n:(r + 1) * n] for r in range(dil)], stage_ref, dil)
            dh = part if dh is None else dh + part
        xv, gain_v = x_ref[...], gain_ref[...]
        dx, xhat = _rms_bwd(dh, xv, gain_v)
        dx_ref[...] = dy_ref[...] + dx
        _accumulate(dgain_ref, jnp.sum(dh * xhat, axis=0, keepdims=True), pl.program_id(0) == 0)

    row = pl.BlockSpec((ts, d), lambda i: (i, 0))
    (dx, dgain), exchanged = _hosted_call(
        body, exchange, name=f"qkv_bwd_{tag}", grid=(t // ts,),
        in_specs=[row, row, _resident((1, d)), _resident(w.shape)] + [_res_spec(seq, dl, QKV_W) for dl in dils],
        out_specs=[row, pl.BlockSpec((1, d), lambda i: (0, 0))],
        out_shape=[jax.ShapeDtypeStruct((t, d), F32), jax.ShapeDtypeStruct((1, d), F32)],
        scratch_shapes=[_stage(ts, d)], semantics=("arbitrary",), args=(dy, x, gain, w, *dps))
    return dx, dgain, exchanged


ANY = pl.BlockSpec(memory_space=pl.ANY)


def _place():
    x, y, c = lax.axis_index("x"), lax.axis_index("y"), lax.axis_index("c")
    return x, y, c


def _exchange_steps(srcs, dsts, gather, send_sems, recv_sems, local_sems):
    x, y, c = _place()
    me, sibling = (x, y, c), (x, y, 1 - c)
    chips = [(1 - x, y), (x, 1 - y), (1 - x, 1 - y)]
    mine = 4 * x + 2 * y + c

    def slot(a, device):
        px, py, pc = device
        return dsts[a].at[4 * px + 2 * py + pc]

    def passes(a, k, block, to, src=None):
        rows = slot(a, block)
        return pltpu.make_async_remote_copy(src_ref=rows if src is None else src, dst_ref=rows, send_sem=send_sems.at[a, k],
                                            recv_sem=recv_sems.at[a, k], device_id=to, device_id_type=MESH)

    def scatters(a, k):
        peer = mine ^ k
        return pltpu.make_async_remote_copy(
            src_ref=srcs[a].at[peer], dst_ref=dsts[a].at[mine], send_sem=send_sems.at[a, k - 1], recv_sem=recv_sems.at[a, k - 1],
            device_id=(peer // 4, (peer // 2) % 2, peer % 2), device_id_type=MESH)

    def local(a):
        return pltpu.make_async_copy(srcs[a] if gather[a] else srcs[a].at[mine], dsts[a].at[mine], local_sems.at[a])

    def first_copies(a):
        if not gather[a]:
            return [scatters(a, k) for k in range(1, N_DEV)]
        return [passes(a, 0, me, sibling, src=srcs[a])] + [passes(a, 1 + j, me, (*chip, c), src=srcs[a]) for j, chip in enumerate(chips)]

    def start():
        for a in range(len(srcs)):
            local(a).start()
            for cp in first_copies(a):
                cp.start()

    def forward():
        for a in range(len(srcs)):
            if gather[a]:
                for j, chip in enumerate(chips):
                    passes(a, 1 + j, (*chip, c), me).wait_recv()
                    passes(a, 4 + j, (*chip, c), sibling).start()

    def finish():
        for a in range(len(srcs)):
            if gather[a]:
                passes(a, 0, sibling, me).wait_recv()
                for j, chip in enumerate(chips):
                    passes(a, 4 + j, (*chip, 1 - c), me).wait_recv()
                    passes(a, 4 + j, (*chip, c), sibling).wait_send()
                for cp in first_copies(a):
                    cp.wait_send()
            else:
                for cp in first_copies(a):
                    cp.wait()
            local(a).wait()

    return start, forward, finish


def _exchange_scratch(n):
    return [pltpu.SemaphoreType.DMA((n, N_DEV - 1)), pltpu.SemaphoreType.DMA((n, N_DEV - 1)), pltpu.SemaphoreType.DMA((n,))]


def _exchanged_shapes(exchange):
    return [jax.ShapeDtypeStruct(((N_DEV,) + a.shape) if g else a.shape, a.dtype) for a, g in exchange]


def _hosted_call(body, exchange, *, name, grid, in_specs, out_specs, out_shape, scratch_shapes, semantics, args):
    out_specs, out_shape, scratch = list(out_specs), list(out_shape), list(scratch_shapes)
    if not exchange:
        outs = pl.pallas_call(body, name=name, grid=grid, in_specs=in_specs, out_specs=out_specs, out_shape=out_shape,
                              scratch_shapes=scratch, compiler_params=_params(*semantics))(*args)
        return list(outs), []
    n, n_in, n_out, n_scr = len(exchange), len(in_specs), len(out_specs), len(scratch)
    gather = [g for _, g in exchange]
    steps = math.prod(grid)

    def hosted(*refs):
        own_in, x_in = refs[:n_in], refs[n_in:n_in + n]
        own_out, x_out = refs[n_in + n:n_in + n + n_out], refs[n_in + n + n_out:n_in + 2 * n + n_out]
        own_scr, sems = refs[n_in + 2 * n + n_out:n_in + 2 * n + n_out + n_scr], refs[-3:]
        step = pl.program_id(0)
        for axis in range(1, len(grid)):
            step = step * grid[axis] + pl.program_id(axis)
        start, forward, finish = _exchange_steps(x_in, x_out, gather, *sems)
        pl.when(step == 0)(start)
        body(*own_in, *own_out, *own_scr)
        pl.when(step == steps // 2)(forward)
        pl.when(step == steps - 1)(finish)

    outs = pl.pallas_call(
        hosted, name=name, grid=grid, in_specs=list(in_specs) + [ANY] * n, out_specs=out_specs + [ANY] * n,
        out_shape=out_shape + _exchanged_shapes(exchange), scratch_shapes=scratch + _exchange_scratch(n),
        compiler_params=_params(*["arbitrary"] * len(grid)),
    )(*args, *[a for a, _ in exchange])
    return list(outs[:n_out]), list(outs[n_out:])


def _exchange_now(exchange, name):
    n = len(exchange)
    gather = [g for _, g in exchange]

    def body(*refs):
        for step in _exchange_steps(refs[:n], refs[n:2 * n], gather, *refs[2 * n:]):
            step()

    return pl.pallas_call(
        body, name=name, in_specs=[ANY] * n, out_specs=[ANY] * n, out_shape=_exchanged_shapes(exchange),
        scratch_shapes=_exchange_scratch(n),
    )(*[a for a, _ in exchange])


def _all_reduce_small(v):
    def body(v_ref, o_ref, recv_ref, send_sems, recv_sems):
        x, y, c = _place()
        me = 4 * x + 2 * y + c
        copies = []
        for k in range(1, N_DEV):
            peer = me ^ k
            copies.append(pltpu.make_async_remote_copy(
                src_ref=v_ref, dst_ref=recv_ref.at[k], send_sem=send_sems.at[k - 1], recv_sem=recv_sems.at[k - 1],
                device_id=(peer // 4, (peer // 2) % 2, peer % 2), device_id_type=MESH))
        for cp in copies:
            cp.start()
        recv_ref[0] = v_ref[...]
        for cp in copies:
            cp.wait()
        acc = recv_ref[me]
        for src in range(1, N_DEV):
            acc = acc + recv_ref[me ^ src]
        o_ref[...] = acc

    vm = pl.BlockSpec(memory_space=pltpu.VMEM)
    return pl.pallas_call(
        body, name="all_reduce_small", in_specs=[vm], out_specs=vm, out_shape=jax.ShapeDtypeStruct(v.shape, F32),
        scratch_shapes=[pltpu.VMEM((N_DEV,) + v.shape, F32), pltpu.SemaphoreType.DMA((N_DEV - 1,)),
                        pltpu.SemaphoreType.DMA((N_DEV - 1,))],
    )(v)


def _adamw_math(w, g, m, v):
    m = ADAM_B1 * m + (1.0 - ADAM_B1) * g
    v = ADAM_B2 * v + (1.0 - ADAM_B2) * (g * g)
    m_hat = m / (1.0 - ADAM_B1 ** ADAM_STEP)
    v_hat = v / (1.0 - ADAM_B2 ** ADAM_STEP)
    delta = -ADAM_LR * (m_hat / (jnp.sqrt(v_hat) + ADAM_EPS) + ADAM_WD * w)
    return delta, m, v


def _adamw(parts, w, m, v, name, layer=None, into=None):
    r, c = w.shape[-2:]
    tr = r // 2 if r % 16 == 0 and r >= 256 else r
    n = len(parts)

    def body(*refs):
        w_ref, m_ref, v_ref = refs[n:n + 3]
        g_ref, d_ref, nm_ref, nv_ref = refs[-4:]
        g = refs[0][...].astype(F32)
        for p_ref in refs[1:n]:
            g = g + p_ref[...].astype(F32)
        g_ref[...] = g
        d_ref[...], nm_ref[...], nv_ref[...] = _adamw_math(w_ref[...], g, m_ref[...], v_ref[...])

    def slab(slot):
        return pl.BlockSpec((None, tr, c), lambda i: (slot, i, 0))

    tile = pl.BlockSpec((tr, c), lambda i: (i, 0)) if layer is None else slab(layer)
    arrays, in_specs = [], []
    for p in parts:
        if isinstance(p, tuple):
            arrays.append(p[0])
            in_specs.append(slab(p[1]))
        else:
            arrays.append(p)
            in_specs.append(tile)
    kept = list(into) if into is not None else []
    return pl.pallas_call(
        body, name=name, grid=(r // tr,), in_specs=in_specs + [tile] * 3 + [ANY] * len(kept), out_specs=[tile] * 4,
        out_shape=[jax.ShapeDtypeStruct(w.shape, F32)] * 4,
        input_output_aliases={n + 3 + k: k for k in range(len(kept))}, compiler_params=_params("parallel"),
    )(*arrays, w, m, v, *kept)


def _rows(g):
    return g.reshape(-1, g.shape[-1])


def _row_blocks(dw):
    k, n = dw.shape
    return dw.reshape(N_DEV, k // N_DEV, n)


def _pack_rows(rows, width):
    out = None
    for i, r in enumerate(rows):
        r = r.reshape(1, -1).astype(F32)
        r = jnp.pad(r, ((i, 8 - 1 - i), (0, width - r.shape[1])))
        out = r if out is None else out + r
    return out


def _mixer_fwd(x, gain, w_in, cos, sin, seq, groups, tag, sink=None, exchanges=None):
    exchanges = exchanges or {}
    os, lses, got = [], [], {}
    qkvs, hs, got["proj"] = _qkv_proj(x, gain, w_in, cos, sin, seq, [dil for dil, _ in groups], tag,
                                      exchange=exchanges.get("proj", ()))
    for gi, (dil, w) in enumerate(groups):
        o, lse, got[gi] = _attn_fwd(qkvs[gi], w, f"{tag}{gi}", sink=sink, exchange=exchanges.get(gi, ()))
        os.append(o)
        lses.append(lse)
    o, lses = _mix_groups(os, lses, [dl for dl, _ in groups], seq, tag)
    return (qkvs, hs, o, lses), got


def _mixer_bwd(dy, x_in, gain, w_in, w_out, saved, cos, sin, seq, groups, tag, sink=None, exchanges=None, scatter_own=False):
    qkvs, hs, o, lses = saved
    t, d = x_in.shape
    dils = [dl for dl, _ in groups]
    lse_tokens = lses[0].reshape(t, LANES) if sink is not None else None
    dw_out = _tn_matmul(o, dy, f"dw_out_{tag}")
    dos, dls, dsink, early = _attn_out_bwd(dy, w_out, o, dils, seq, tag, lse=lse_tokens, sink=sink,
                                           exchange=_to_send([dw_out]) if scatter_own else ())
    if scatter_own:
        (dw_out,) = early
    exchanges = exchanges or {}
    dps, got = [], {}
    for gi, (dil, w) in enumerate(groups):
        dp, got[gi] = _attn_bwd(qkvs[gi], dos[gi], lses[gi], dls[gi], _tables_by_residue(cos, seq, dil),
                                _tables_by_residue(sin, seq, dil), w, f"{tag}{gi}", exchange=exchanges.get(gi, ()))
        dps.append(dp)
    dw_in = None
    for gi in range(len(groups)):
        dw_in = _tn_matmul(dps[gi].reshape(t, QKV_W), hs[gi].reshape(t, d), f"dw_in_{tag}{gi}", into=dw_in, row_block=gi,
                           row_blocks=len(groups))
    dx, dgain, late = _qkv_bwd(dy, x_in, gain, w_in, dps, dils, seq, tag, exchange=_to_send([dw_in]) if scatter_own else ())
    if scatter_own:
        (dw_in,) = late
    return dx, dw_in, dw_out, dgain, dsink, got


def _ffn_layer_bwd(dy, x_in, gain, saved, wg, wu, wd, tag, exchange=()):
    g, u, act, h = saved
    dx, dg, du, dgain, got = _ffn_bwd(dy, x_in, gain, g, u, wg, wu, wd, tag, exchange=exchange)
    dwd = _tn_matmul(act, dy, f"dw_down_{tag}")
    dwg = _tn_matmul(dg, h, f"dw_gate_{tag}")
    dwu = _tn_matmul(du, h, f"dw_up_{tag}")
    return dx, dwg, dwu, dwd, dgain, got


def _to_send(dws):
    return [(_row_blocks(g), False) for g in dws]


def kernel(x, a_w_in, a_sink, a_w_out, b_w_in, b_w_out, norm_mix, norm_ffn, w_gate, w_up, w_down, final_norm, loss_target, m_a_w_in, m_a_sink, m_a_w_out, m_b_w_in, m_b_w_out, m_norm_mix, m_norm_ffn, m_w_gate, m_w_up, m_w_down, m_final_norm, v_a_w_in, v_a_sink, v_a_w_out, v_b_w_in, v_b_w_out, v_norm_mix, v_norm_ffn, v_w_gate, v_w_up, v_w_down, v_final_norm):
    bl, seq, d = x.shape
    t = bl * seq
    xf = x.reshape(t, d)
    target = loss_target.reshape(t, d)
    cos, sin = _rope_tables(seq)
    groups_a = [(1, ATTN_HALF_WINDOW)]
    groups_b = [(dil, window // 2 // dil) for window, dil in DILATED_GROUPS]

    def flip(w_):
        return jnp.swapaxes(w_, -1, -2)

    a_w_in, m_a_w_in, v_a_w_in, b_w_in, m_b_w_in, v_b_w_in = map(flip, (a_w_in, m_a_w_in, v_a_w_in, b_w_in, m_b_w_in, v_b_w_in))
    w_gate, m_w_gate, v_w_gate, w_up, m_w_up, v_w_up = map(flip, (w_gate, m_w_gate, v_w_gate, w_up, m_w_up, v_w_up))

    def shard(w_, layer):
        return (w_[layer].astype(BF16), True)

    (wa_in,) = map(_rows, _exchange_now([shard(a_w_in, 0)], "gather_first"))

    saved_a, got = _mixer_fwd(xf, norm_mix[0:1], wa_in, cos, sin, seq, groups_a, "a", sink=a_sink[0],
                              exchanges={"proj": [shard(w_down, 0), shard(a_w_out, 0)], 0: [shard(w_gate, 0), shard(w_up, 0)]})
    wg0, wu0, wd0, wa_out = map(_rows, got[0] + got["proj"])
    x1_0, x2_0, *saved_0, got = _ffn_fwd(xf, saved_a[2], wa_out, norm_ffn[0:1], wg0, wu0, wd0, "0",
                                         exchange=[shard(b_w_in, 0), shard(b_w_out, 0)])
    wb_in, wb_out = map(_rows, got)
    saved_b, got = _mixer_fwd(x2_0, norm_mix[1:2], wb_in, cos, sin, seq, groups_b, "b",
                              exchanges={0: [shard(w_gate, 1)], 1: [shard(w_up, 1)], 2: [shard(w_down, 1)]})
    wg1, wu1, wd1 = map(_rows, got[0] + got[1] + got[2])
    x1_1, dy, *saved_1, loss_part, d_final, _ = _ffn_fwd(x2_0, saved_b[2], wb_out, norm_ffn[1:2], wg1, wu1, wd1, "1",
                                                         loss_head=(final_norm.reshape(1, d), target))

    dy, dwg1, dwu1, dwd1, d_nf1, _ = _ffn_layer_bwd(dy, x1_1, norm_ffn[1:2], saved_1, wg1, wu1, wd1, "1")
    dy, dwb_in, dwb_out, d_nm1, _, got = _mixer_bwd(
        dy, x2_0, norm_mix[1:2], wb_in, wb_out, saved_b, cos, sin, seq, groups_b, "b",
        exchanges={0: _to_send([dwg1, dwd1]), 1: _to_send([dwu1])})
    (r_g1, r_d1), (r_u1,) = got[0], got[1]
    dy, dwg0, dwu0, dwd0, d_nf0, (r_b_in, r_b_out) = _ffn_layer_bwd(
        dy, x1_0, norm_ffn[0:1], saved_0, wg0, wu0, wd0, "0", exchange=_to_send([dwb_in, dwb_out]))
    dy, r_a_in, r_a_out, d_nm0, d_sink, got = _mixer_bwd(
        dy, xf, norm_mix[0:1], wa_in, wa_out, saved_a, cos, sin, seq, groups_a, "a", sink=a_sink[0],
        exchanges={0: _to_send([dwg0, dwu0, dwd0])}, scatter_own=True)
    r_g0, r_u0, r_d0 = got[0]
    grad_x = dy.reshape(bl, seq, d)

    def update(received, w_, m_, v_, name):
        out = None
        for layer in reversed(range(len(received))):
            out = _adamw([(received[layer], src) for src in range(N_DEV)], w_, m_, v_, f"adamw_{name}{layer}", layer=layer, into=out)
        return out

    u_a_in = update([r_a_in], a_w_in, m_a_w_in, v_a_w_in, "a_in")
    u_a_out = update([r_a_out], a_w_out, m_a_w_out, v_a_w_out, "a_out")
    u_b_in = update([r_b_in], b_w_in, m_b_w_in, v_b_w_in, "b_in")
    u_b_out = update([r_b_out], b_w_out, m_b_w_out, v_b_w_out, "b_out")
    u_gate = update([r_g0, r_g1], w_gate, m_w_gate, v_w_gate, "gate")
    u_up = update([r_u0, r_u1], w_up, m_w_up, v_w_up, "up")
    u_down = update([r_d0, r_d1], w_down, m_w_down, v_w_down, "down")

    small = _pack_rows([d_nm0, d_nm1, d_nf0, d_nf1, d_final, d_sink, loss_part], d)
    total = _all_reduce_small(small)
    small_w = _pack_rows([norm_mix[0], norm_mix[1], norm_ffn[0], norm_ffn[1], final_norm, a_sink], d)
    small_m = _pack_rows([m_norm_mix[0], m_norm_mix[1], m_norm_ffn[0], m_norm_ffn[1], m_final_norm, m_a_sink], d)
    small_v = _pack_rows([v_norm_mix[0], v_norm_mix[1], v_norm_ffn[0], v_norm_ffn[1], v_final_norm, v_a_sink], d)
    u_small = _adamw([total], small_w, small_m, small_v, "adamw_small")
    loss = total[6, 0]

    outs = []
    for k in range(4):
        sm = u_small[k]
        outs += [flip(u_a_in[k]), sm[5:6, :N_HEADS], u_a_out[k], flip(u_b_in[k]), u_b_out[k], sm[0:2], sm[2:4],
                 flip(u_gate[k]), flip(u_up[k]), u_down[k], sm[4]]
    return (loss, grad_x, *outs)
```
